```python
import jax
import jax.numpy as jnp
from jax import lax
import numpy as np

D_MODEL = 1024
BATCH = 8
SEQ = 4096
DEPTH = 2

HEAD_DIM = 64
BLOCK = 128
SWA_HEADS = D_MODEL // (2 * HEAD_DIM)
SWA_KV_HEADS = SWA_HEADS // 4
SWA_GROUP = SWA_HEADS // SWA_KV_HEADS
WINDOW = 128
RWKV_HEADS = D_MODEL // (2 * HEAD_DIM)
RWKV_DIM = RWKV_HEADS * HEAD_DIM
DECAY_LORA = 64
ICLR_LORA = 64
GATE_LORA = 128
FOX_HEADS = D_MODEL // HEAD_DIM
FOX_DIM = FOX_HEADS * HEAD_DIM
D_FF = 2816
PLE_DIM = 256
N_EVEN = (DEPTH + 1) // 2
N_ODD = DEPTH // 2
NORM_EPS = 1e-6
GN_EPS = 64e-5
L2_EPS = 1e-12

SWA_Q = SWA_HEADS * HEAD_DIM
SWA_KV = SWA_KV_HEADS * HEAD_DIM
SWA_COLS = SWA_Q + 2 * SWA_KV
RWKV_COLS = 3 * RWKV_DIM + DECAY_LORA + ICLR_LORA + GATE_LORA
RWKV_SPLITS = [RWKV_DIM, 2 * RWKV_DIM, 3 * RWKV_DIM,
               3 * RWKV_DIM + DECAY_LORA, 3 * RWKV_DIM + DECAY_LORA + ICLR_LORA]
EVEN_IN = SWA_COLS + RWKV_COLS
EVEN_OUT = SWA_Q + RWKV_DIM
FOX_IN = 3 * FOX_DIM + FOX_HEADS

kernel_name = "hybrid_swa_rwkv7_fox_macaron"


def rms_norm(x, g):
    xf = x.astype(jnp.float32)
    y = xf * lax.rsqrt(jnp.mean(xf * xf, axis=-1, keepdims=True) + NORM_EPS)
    return (y * g.astype(jnp.float32)).astype(x.dtype)


def swiglu(h, w_gu, w_down):
    g, u = jnp.split(h @ w_gu, 2, axis=-1)
    return (jax.nn.silu(g) * u) @ w_down


def alibi_slopes(n):
    return 2.0 ** (-8.0 * jnp.arange(1, n + 1, dtype=jnp.float32) / n)


def sliding_window_attention(q, k, v, sinks):
    b, s = q.shape[:2]
    nb = s // BLOCK
    scale = HEAD_DIM ** -0.5
    qb = q.reshape(b, nb, BLOCK, SWA_KV_HEADS, SWA_GROUP, HEAD_DIM)
    pad = ((0, 0), (BLOCK, 0), (0, 0), (0, 0))
    kp = jnp.pad(k, pad).reshape(b, nb + 1, BLOCK, SWA_KV_HEADS, HEAD_DIM)
    vp = jnp.pad(v, pad).reshape(b, nb + 1, BLOCK, SWA_KV_HEADS, HEAD_DIM)
    kb = jnp.concatenate([kp[:, :-1], kp[:, 1:]], axis=2)
    vb = jnp.concatenate([vp[:, :-1], vp[:, 1:]], axis=2)
    logits = jnp.einsum('bnqhgd,bnkhd->bnhgqk', qb, kb).astype(jnp.float32) * scale
    qi = jnp.arange(BLOCK)[:, None]
    ki = jnp.arange(2 * BLOCK)[None, :]
    dist = qi + BLOCK - ki
    blk = jnp.arange(nb)[:, None, None]
    valid = (dist >= 0) & (dist < WINDOW) & (blk * BLOCK - BLOCK + ki >= 0)
    slopes = alibi_slopes(SWA_HEADS).reshape(SWA_KV_HEADS, SWA_GROUP)
    logits = logits - slopes[:, :, None, None] * dist.astype(jnp.float32)
    logits = jnp.where(valid[None, :, None, None], logits, -jnp.inf)
    sink = sinks.astype(jnp.float32).reshape(SWA_KV_HEADS, SWA_GROUP)[:, :, None]
    m = jnp.maximum(logits.max(axis=-1), sink)
    pr = jnp.exp(logits - m[..., None])
    denom = pr.sum(axis=-1) + jnp.exp(sink - m)
    pr = pr / denom[..., None]
    out = jnp.einsum('bnhgqk,bnkhd->bnqhgd', pr.astype(v.dtype), vb)
    return out.reshape(b, s, SWA_Q)


def rwkv7_time_mix(h, mu, w0, w2, a0, a2, g2, k_k, k_a, r_k, ln_w, ln_b):
    b, s, _ = h.shape
    f32 = jnp.float32
    shifted = jnp.pad(h[:, :-1], ((0, 0), (1, 0), (0, 0)))
    h = h + (shifted - h) * mu
    r, k, v, xw, xa, xg = jnp.split(h, RWKV_SPLITS, axis=-1)
    wlog = -jax.nn.softplus(-(w0 + jnp.tanh(xw) @ w2)) - 0.5
    a = jax.nn.sigmoid(a0 + xa @ a2)
    g = jax.nn.sigmoid(xg) @ g2
    hd = lambda t: t.astype(f32).reshape(b, s, RWKV_HEADS, HEAD_DIM)
    r, k, v, wlog, a = hd(r), hd(k), hd(v), hd(wlog), hd(a)
    kk = k * k_k.astype(f32).reshape(RWKV_HEADS, HEAD_DIM)
    kk = kk / jnp.maximum(jnp.sqrt(jnp.sum(kk * kk, axis=-1, keepdims=True)), L2_EPS)
    k = k * (1.0 + (a - 1.0) * k_a.astype(f32).reshape(RWKV_HEADS, HEAD_DIM))
    decay = jnp.exp(-jnp.exp(wlog))

    def step(state, inp):
        r_t, w_t, k_t, v_t, kk_t, a_t = inp
        s_kk = jnp.einsum('bhij,bhj->bhi', state, kk_t)
        state = (state * w_t[:, :, None, :]
                 - s_kk[..., None] * (kk_t * a_t)[:, :, None, :]
                 + v_t[..., None] * k_t[:, :, None, :])
        return state, jnp.einsum('bhij,bhj->bhi', state, r_t)

    xs = tuple(jnp.moveaxis(t, 1, 0) for t in (r, decay, k, v, kk, a))
    state0 = jnp.zeros((b, RWKV_HEADS, HEAD_DIM, HEAD_DIM), f32)
    _, y = lax.scan(step, state0, xs)
    y = jnp.moveaxis(y, 0, 1)
    mean = jnp.mean(y, axis=-1, keepdims=True)
    var = jnp.mean(jnp.square(y - mean), axis=-1, keepdims=True)
    y = ((y - mean) * lax.rsqrt(var + GN_EPS) * ln_w.astype(f32).reshape(RWKV_HEADS, HEAD_DIM)
         + ln_b.astype(f32).reshape(RWKV_HEADS, HEAD_DIM))
    y = y + jnp.sum(r * k * r_k.astype(f32), axis=-1, keepdims=True) * v
    return (y.reshape(b, s, RWKV_DIM) * g.astype(f32)).astype(h.dtype)


def forgetting_attention(q, k, v, log_f):
    b, s = q.shape[:2]
    nb = s // BLOCK
    scale = HEAD_DIM ** -0.5
    c = jnp.moveaxis(jnp.cumsum(log_f, axis=1), 1, 2)
    key_pos = jnp.arange(s)

    def one_block(n):
        start = n * BLOCK
        qb = lax.dynamic_slice_in_dim(q, start, BLOCK, axis=1)
        cq = lax.dynamic_slice_in_dim(c, start, BLOCK, axis=2)
        logits = (jnp.einsum('bqhd,bkhd->bhqk', qb, k).astype(jnp.float32) * scale
                  + cq[..., None] - c[:, :, None, :])
        qpos = start + jnp.arange(BLOCK)
        logits = jnp.where(key_pos[None, :] <= qpos[:, None], logits, -jnp.inf)
        pr = jax.nn.softmax(logits, axis=-1)
        return jnp.einsum('bhqk,bkhd->bqhd', pr.astype(v.dtype), v)

    out = lax.map(one_block, jnp.arange(nb))
    return jnp.moveaxis(out, 0, 1).reshape(b, s, FOX_DIM)


def _fwd_setup_inputs(seed: int = 0) -> dict:
    key = jax.random.key(seed)
    k = jax.random.split(key, 30)
    f32 = jnp.float32
    nrm = lambda kk, shape, scale: jax.random.normal(kk, shape, f32) * scale
    gain = lambda kk, shape: 1.0 + 0.05 * jax.random.normal(kk, shape, f32)
    unif = lambda kk, shape, lo, hi: jax.random.uniform(kk, shape, f32, lo, hi)
    return {
        'x': nrm(k[0], (BATCH, SEQ, D_MODEL), 1.0),
        'p': nrm(k[1], (DEPTH, BATCH, SEQ, PLE_DIM), 1.0),
        'ffn1_norm': gain(k[2], (DEPTH, D_MODEL)),
        'ffn1_w_gu': nrm(k[3], (DEPTH, D_MODEL, 2 * D_FF), D_MODEL ** -0.5),
        'ffn1_w_down': nrm(k[4], (DEPTH, D_FF, D_MODEL), D_FF ** -0.5),
        'mix_norm': gain(k[5], (DEPTH, D_MODEL)),
        'ffn2_norm': gain(k[6], (DEPTH, D_MODEL)),
        'ffn2_w_gu': nrm(k[7], (DEPTH, D_MODEL, 2 * D_FF), D_MODEL ** -0.5),
        'ffn2_w_down': nrm(k[8], (DEPTH, D_FF, D_MODEL), D_FF ** -0.5),
        'ple_norm': gain(k[9], (DEPTH, D_MODEL)),
        'ple_w_gate': nrm(k[10], (DEPTH, D_MODEL, D_MODEL), D_MODEL ** -0.5),
        'ple_w_proj': nrm(k[11], (DEPTH, PLE_DIM, D_MODEL), 0.5 * PLE_DIM ** -0.5),
        'even_w_in': nrm(k[12], (N_EVEN, D_MODEL, EVEN_IN), D_MODEL ** -0.5),
        'even_w_out': nrm(k[13], (N_EVEN, EVEN_OUT, D_MODEL), EVEN_OUT ** -0.5),
        'swa_sinks': nrm(k[14], (N_EVEN, SWA_HEADS), 0.5),
        'rwkv_mu': unif(k[15], (N_EVEN, RWKV_COLS), 0.0, 1.0),
        'rwkv_w0': unif(k[16], (N_EVEN, RWKV_DIM), -5.0, -0.5),
        'rwkv_w2': nrm(k[17], (N_EVEN, DECAY_LORA, RWKV_DIM), 0.5 * DECAY_LORA ** -0.5),
        'rwkv_a0': nrm(k[18], (N_EVEN, RWKV_DIM), 0.1),
        'rwkv_a2': nrm(k[19], (N_EVEN, ICLR_LORA, RWKV_DIM), 0.5 * ICLR_LORA ** -0.5),
        'rwkv_g2': nrm(k[20], (N_EVEN, GATE_LORA, RWKV_DIM), GATE_LORA ** -0.5),
        'rwkv_k_k': 0.85 + nrm(k[21], (N_EVEN, RWKV_DIM), 0.05),
        'rwkv_k_a': 1.0 + nrm(k[22], (N_EVEN, RWKV_DIM), 0.05),
        'rwkv_r_k': nrm(k[23], (N_EVEN, RWKV_HEADS, HEAD_DIM), 0.1),
        'rwkv_ln_w': gain(k[24], (N_EVEN, RWKV_DIM)),
        'rwkv_ln_b': nrm(k[25], (N_EVEN, RWKV_DIM), 0.01),
        'fox_w_in': nrm(k[26], (N_ODD, D_MODEL, FOX_IN), D_MODEL ** -0.5),
        'fox_b_f': unif(k[27], (N_ODD, FOX_HEADS), 1.0, 6.0),
        'fox_w_out': nrm(k[28], (N_ODD, FOX_DIM, D_MODEL), FOX_DIM ** -0.5),
        'final_norm': gain(k[29], (D_MODEL,)),
    }


def _fwd_reference(x, p, ffn1_norm, ffn1_w_gu, ffn1_w_down, mix_norm, ffn2_norm, ffn2_w_gu,
              ffn2_w_down, ple_norm, ple_w_gate, ple_w_proj, even_w_in, even_w_out,
              swa_sinks, rwkv_mu, rwkv_w0, rwkv_w2, rwkv_a0, rwkv_a2, rwkv_g2, rwkv_k_k,
              rwkv_k_a, rwkv_r_k, rwkv_ln_w, rwkv_ln_b, fox_w_in, fox_b_f, fox_w_out,
              final_norm):
    b, s, _ = x.shape
    for i in range(DEPTH):
        j = i // 2
        x = x + 0.5 * swiglu(rms_norm(x, ffn1_norm[i]), ffn1_w_gu[i], ffn1_w_down[i])
        hn = rms_norm(x, mix_norm[i])
        if i % 2 == 0:
            proj = hn @ even_w_in[j]
            qa, ka, va, hb = jnp.split(proj, [SWA_Q, SWA_Q + SWA_KV, SWA_COLS], axis=-1)
            ya = sliding_window_attention(
                qa.reshape(b, s, SWA_HEADS, HEAD_DIM),
                ka.reshape(b, s, SWA_KV_HEADS, HEAD_DIM),
                va.reshape(b, s, SWA_KV_HEADS, HEAD_DIM),
                swa_sinks[j])
            yb = rwkv7_time_mix(hb, rwkv_mu[j], rwkv_w0[j], rwkv_w2[j], rwkv_a0[j], rwkv_a2[j],
                                rwkv_g2[j], rwkv_k_k[j], rwkv_k_a[j], rwkv_r_k[j],
                                rwkv_ln_w[j], rwkv_ln_b[j])
            mixed = jnp.concatenate([ya, yb], axis=-1) @ even_w_out[j]
        else:
            proj = hn @ fox_w_in[j]
            qc, kc, vc, fz = jnp.split(proj, [FOX_DIM, 2 * FOX_DIM, 3 * FOX_DIM], axis=-1)
            log_f = jax.nn.log_sigmoid(fz.astype(jnp.float32) + fox_b_f[j].astype(jnp.float32))
            yc = forgetting_attention(
                qc.reshape(b, s, FOX_HEADS, HEAD_DIM),
                kc.reshape(b, s, FOX_HEADS, HEAD_DIM),
                vc.reshape(b, s, FOX_HEADS, HEAD_DIM),
                log_f)
            mixed = yc @ fox_w_out[j]
        x = x + mixed
        x = x + 0.5 * swiglu(rms_norm(x, ffn2_norm[i]), ffn2_w_gu[i], ffn2_w_down[i])
        gate = jax.nn.sigmoid(rms_norm(x, ple_norm[i]) @ ple_w_gate[i])
        x = x + gate * (p[i] @ ple_w_proj[i])
    return rms_norm(x, final_norm)


import jax as _jax
import jax.numpy as _jnp

TWIN_FORMAT = 'train_step'
FWD_PARAMS = ['x', 'p', 'ffn1_norm', 'ffn1_w_gu', 'ffn1_w_down', 'mix_norm', 'ffn2_norm', 'ffn2_w_gu', 'ffn2_w_down', 'ple_norm', 'ple_w_gate', 'ple_w_proj', 'even_w_in', 'even_w_out', 'swa_sinks', 'rwkv_mu', 'rwkv_w0', 'rwkv_w2', 'rwkv_a0', 'rwkv_a2', 'rwkv_g2', 'rwkv_k_k', 'rwkv_k_a', 'rwkv_r_k', 'rwkv_ln_w', 'rwkv_ln_b', 'fox_w_in', 'fox_b_f', 'fox_w_out', 'final_norm']
TWIN_WEIGHTS = ['ffn1_norm', 'ffn1_w_gu', 'ffn1_w_down', 'mix_norm', 'ffn2_norm', 'ffn2_w_gu', 'ffn2_w_down', 'ple_norm', 'ple_w_gate', 'ple_w_proj', 'even_w_in', 'even_w_out', 'swa_sinks', 'rwkv_mu', 'rwkv_w0', 'rwkv_w2', 'rwkv_a0', 'rwkv_a2', 'rwkv_g2', 'rwkv_k_k', 'rwkv_k_a', 'rwkv_r_k', 'rwkv_ln_w', 'rwkv_ln_b', 'fox_w_in', 'fox_b_f', 'fox_w_out', 'final_norm']
TWIN_DIFF_INPUT = 'x'
TWIN_INPUTS = ['x', 'p', 'ffn1_norm', 'ffn1_w_gu', 'ffn1_w_down', 'mix_norm', 'ffn2_norm', 'ffn2_w_gu', 'ffn2_w_down', 'ple_norm', 'ple_w_gate', 'ple_w_proj', 'even_w_in', 'even_w_out', 'swa_sinks', 'rwkv_mu', 'rwkv_w0', 'rwkv_w2', 'rwkv_a0', 'rwkv_a2', 'rwkv_g2', 'rwkv_k_k', 'rwkv_k_a', 'rwkv_r_k', 'rwkv_ln_w', 'rwkv_ln_b', 'fox_w_in', 'fox_b_f', 'fox_w_out', 'final_norm', 'loss_target', 'm_ffn1_norm', 'm_ffn1_w_gu', 'm_ffn1_w_down', 'm_mix_norm', 'm_ffn2_norm', 'm_ffn2_w_gu', 'm_ffn2_w_down', 'm_ple_norm', 'm_ple_w_gate', 'm_ple_w_proj', 'm_even_w_in', 'm_even_w_out', 'm_swa_sinks', 'm_rwkv_mu', 'm_rwkv_w0', 'm_rwkv_w2', 'm_rwkv_a0', 'm_rwkv_a2', 'm_rwkv_g2', 'm_rwkv_k_k', 'm_rwkv_k_a', 'm_rwkv_r_k', 'm_rwkv_ln_w', 'm_rwkv_ln_b', 'm_fox_w_in', 'm_fox_b_f', 'm_fox_w_out', 'm_final_norm', 'v_ffn1_norm', 'v_ffn1_w_gu', 'v_ffn1_w_down', 'v_mix_norm', 'v_ffn2_norm', 'v_ffn2_w_gu', 'v_ffn2_w_down', 'v_ple_norm', 'v_ple_w_gate', 'v_ple_w_proj', 'v_even_w_in', 'v_even_w_out', 'v_swa_sinks', 'v_rwkv_mu', 'v_rwkv_w0', 'v_rwkv_w2', 'v_rwkv_a0', 'v_rwkv_a2', 'v_rwkv_g2', 'v_rwkv_k_k', 'v_rwkv_k_a', 'v_rwkv_r_k', 'v_rwkv_ln_w', 'v_rwkv_ln_b', 'v_fox_w_in', 'v_fox_b_f', 'v_fox_w_out', 'v_final_norm']
TWIN_OUTPUTS = ['loss', 'grad_x', 'grad_ffn1_norm', 'grad_ffn1_w_gu', 'grad_ffn1_w_down', 'grad_mix_norm', 'grad_ffn2_norm', 'grad_ffn2_w_gu', 'grad_ffn2_w_down', 'grad_ple_norm', 'grad_ple_w_gate', 'grad_ple_w_proj', 'grad_even_w_in', 'grad_even_w_out', 'grad_swa_sinks', 'grad_rwkv_mu', 'grad_rwkv_w0', 'grad_rwkv_w2', 'grad_rwkv_a0', 'grad_rwkv_a2', 'grad_rwkv_g2', 'grad_rwkv_k_k', 'grad_rwkv_k_a', 'grad_rwkv_r_k', 'grad_rwkv_ln_w', 'grad_rwkv_ln_b', 'grad_fox_w_in', 'grad_fox_b_f', 'grad_fox_w_out', 'grad_final_norm', 'delta_ffn1_norm', 'delta_ffn1_w_gu', 'delta_ffn1_w_down', 'delta_mix_norm', 'delta_ffn2_norm', 'delta_ffn2_w_gu', 'delta_ffn2_w_down', 'delta_ple_norm', 'delta_ple_w_gate', 'delta_ple_w_proj', 'delta_even_w_in', 'delta_even_w_out', 'delta_swa_sinks', 'delta_rwkv_mu', 'delta_rwkv_w0', 'delta_rwkv_w2', 'delta_rwkv_a0', 'delta_rwkv_a2', 'delta_rwkv_g2', 'delta_rwkv_k_k', 'delta_rwkv_k_a', 'delta_rwkv_r_k', 'delta_rwkv_ln_w', 'delta_rwkv_ln_b', 'delta_fox_w_in', 'delta_fox_b_f', 'delta_fox_w_out', 'delta_final_norm', 'new_m_ffn1_norm', 'new_m_ffn1_w_gu', 'new_m_ffn1_w_down', 'new_m_mix_norm', 'new_m_ffn2_norm', 'new_m_ffn2_w_gu', 'new_m_ffn2_w_down', 'new_m_ple_norm', 'new_m_ple_w_gate', 'new_m_ple_w_proj', 'new_m_even_w_in', 'new_m_even_w_out', 'new_m_swa_sinks', 'new_m_rwkv_mu', 'new_m_rwkv_w0', 'new_m_rwkv_w2', 'new_m_rwkv_a0', 'new_m_rwkv_a2', 'new_m_rwkv_g2', 'new_m_rwkv_k_k', 'new_m_rwkv_k_a', 'new_m_rwkv_r_k', 'new_m_rwkv_ln_w', 'new_m_rwkv_ln_b', 'new_m_fox_w_in', 'new_m_fox_b_f', 'new_m_fox_w_out', 'new_m_final_norm', 'new_v_ffn1_norm', 'new_v_ffn1_w_gu', 'new_v_ffn1_w_down', 'new_v_mix_norm', 'new_v_ffn2_norm', 'new_v_ffn2_w_gu', 'new_v_ffn2_w_down', 'new_v_ple_norm', 'new_v_ple_w_gate', 'new_v_ple_w_proj', 'new_v_even_w_in', 'new_v_even_w_out', 'new_v_swa_sinks', 'new_v_rwkv_mu', 'new_v_rwkv_w0', 'new_v_rwkv_w2', 'new_v_rwkv_a0', 'new_v_rwkv_a2', 'new_v_rwkv_g2', 'new_v_rwkv_k_k', 'new_v_rwkv_k_a', 'new_v_rwkv_r_k', 'new_v_rwkv_ln_w', 'new_v_rwkv_ln_b', 'new_v_fox_w_in', 'new_v_fox_b_f', 'new_v_fox_w_out', 'new_v_final_norm']
TWIN_LEAF_KINDS = {'loss': 'loss', 'grad_x': 'grad_x', 'grad_ffn1_norm': 'grad_w', 'grad_ffn1_w_gu': 'grad_w', 'grad_ffn1_w_down': 'grad_w', 'grad_mix_norm': 'grad_w', 'grad_ffn2_norm': 'grad_w', 'grad_ffn2_w_gu': 'grad_w', 'grad_ffn2_w_down': 'grad_w', 'grad_ple_norm': 'grad_w', 'grad_ple_w_gate': 'grad_w', 'grad_ple_w_proj': 'grad_w', 'grad_even_w_in': 'grad_w', 'grad_even_w_out': 'grad_w', 'grad_swa_sinks': 'grad_w', 'grad_rwkv_mu': 'grad_w', 'grad_rwkv_w0': 'grad_w', 'grad_rwkv_w2': 'grad_w', 'grad_rwkv_a0': 'grad_w', 'grad_rwkv_a2': 'grad_w', 'grad_rwkv_g2': 'grad_w', 'grad_rwkv_k_k': 'grad_w', 'grad_rwkv_k_a': 'grad_w', 'grad_rwkv_r_k': 'grad_w', 'grad_rwkv_ln_w': 'grad_w', 'grad_rwkv_ln_b': 'grad_w', 'grad_fox_w_in': 'grad_w', 'grad_fox_b_f': 'grad_w', 'grad_fox_w_out': 'grad_w', 'grad_final_norm': 'grad_w', 'delta_ffn1_norm': 'delta_w', 'delta_ffn1_w_gu': 'delta_w', 'delta_ffn1_w_down': 'delta_w', 'delta_mix_norm': 'delta_w', 'delta_ffn2_norm': 'delta_w', 'delta_ffn2_w_gu': 'delta_w', 'delta_ffn2_w_down': 'delta_w', 'delta_ple_norm': 'delta_w', 'delta_ple_w_gate': 'delta_w', 'delta_ple_w_proj': 'delta_w', 'delta_even_w_in': 'delta_w', 'delta_even_w_out': 'delta_w', 'delta_swa_sinks': 'delta_w', 'delta_rwkv_mu': 'delta_w', 'delta_rwkv_w0': 'delta_w', 'delta_rwkv_w2': 'delta_w', 'delta_rwkv_a0': 'delta_w', 'delta_rwkv_a2': 'delta_w', 'delta_rwkv_g2': 'delta_w', 'delta_rwkv_k_k': 'delta_w', 'delta_rwkv_k_a': 'delta_w', 'delta_rwkv_r_k': 'delta_w', 'delta_rwkv_ln_w': 'delta_w', 'delta_rwkv_ln_b': 'delta_w', 'delta_fox_w_in': 'delta_w', 'delta_fox_b_f': 'delta_w', 'delta_fox_w_out': 'delta_w', 'delta_final_norm': 'delta_w', 'new_m_ffn1_norm': 'new_m', 'new_m_ffn1_w_gu': 'new_m', 'new_m_ffn1_w_down': 'new_m', 'new_m_mix_norm': 'new_m', 'new_m_ffn2_norm': 'new_m', 'new_m_ffn2_w_gu': 'new_m', 'new_m_ffn2_w_down': 'new_m', 'new_m_ple_norm': 'new_m', 'new_m_ple_w_gate': 'new_m', 'new_m_ple_w_proj': 'new_m', 'new_m_even_w_in': 'new_m', 'new_m_even_w_out': 'new_m', 'new_m_swa_sinks': 'new_m', 'new_m_rwkv_mu': 'new_m', 'new_m_rwkv_w0': 'new_m', 'new_m_rwkv_w2': 'new_m', 'new_m_rwkv_a0': 'new_m', 'new_m_rwkv_a2': 'new_m', 'new_m_rwkv_g2': 'new_m', 'new_m_rwkv_k_k': 'new_m', 'new_m_rwkv_k_a': 'new_m', 'new_m_rwkv_r_k': 'new_m', 'new_m_rwkv_ln_w': 'new_m', 'new_m_rwkv_ln_b': 'new_m', 'new_m_fox_w_in': 'new_m', 'new_m_fox_b_f': 'new_m', 'new_m_fox_w_out': 'new_m', 'new_m_final_norm': 'new_m', 'new_v_ffn1_norm': 'new_v', 'new_v_ffn1_w_gu': 'new_v', 'new_v_ffn1_w_down': 'new_v', 'new_v_mix_norm': 'new_v', 'new_v_ffn2_norm': 'new_v', 'new_v_ffn2_w_gu': 'new_v', 'new_v_ffn2_w_down': 'new_v', 'new_v_ple_norm': 'new_v', 'new_v_ple_w_gate': 'new_v', 'new_v_ple_w_proj': 'new_v', 'new_v_even_w_in': 'new_v', 'new_v_even_w_out': 'new_v', 'new_v_swa_sinks': 'new_v', 'new_v_rwkv_mu': 'new_v', 'new_v_rwkv_w0': 'new_v', 'new_v_rwkv_w2': 'new_v', 'new_v_rwkv_a0': 'new_v', 'new_v_rwkv_a2': 'new_v', 'new_v_rwkv_g2': 'new_v', 'new_v_rwkv_k_k': 'new_v', 'new_v_rwkv_k_a': 'new_v', 'new_v_rwkv_r_k': 'new_v', 'new_v_rwkv_ln_w': 'new_v', 'new_v_rwkv_ln_b': 'new_v', 'new_v_fox_w_in': 'new_v', 'new_v_fox_b_f': 'new_v', 'new_v_fox_w_out': 'new_v', 'new_v_final_norm': 'new_v'}


def _forward(args):
    return _fwd_reference(*[args[k] for k in FWD_PARAMS])


def _output_shape():
    def fwd():
        inp = _fwd_setup_inputs(0)
        return _fwd_reference(*[inp[k] for k in FWD_PARAMS])
    out = _jax.eval_shape(fwd)
    return out.shape, out.dtype

N_MICROBATCH = 1
ADAM_LR = 0.001
ADAM_B1 = 0.9
ADAM_B2 = 0.999
ADAM_EPS = 1e-08
ADAM_WD = 0.01
ADAM_STEP = 10
PER_EXAMPLE_BATCH_AXIS = {'x': 0, 'p': 1, 'loss_target': 0}
SHARED_INPUTS = []
_WEIGHT_DTYPES = {'ffn1_norm': _jnp.float32, 'ffn1_w_gu': _jnp.float32, 'ffn1_w_down': _jnp.float32, 'mix_norm': _jnp.float32, 'ffn2_norm': _jnp.float32, 'ffn2_w_gu': _jnp.float32, 'ffn2_w_down': _jnp.float32, 'ple_norm': _jnp.float32, 'ple_w_gate': _jnp.float32, 'ple_w_proj': _jnp.float32, 'even_w_in': _jnp.float32, 'even_w_out': _jnp.float32, 'swa_sinks': _jnp.float32, 'rwkv_mu': _jnp.float32, 'rwkv_w0': _jnp.float32, 'rwkv_w2': _jnp.float32, 'rwkv_a0': _jnp.float32, 'rwkv_a2': _jnp.float32, 'rwkv_g2': _jnp.float32, 'rwkv_k_k': _jnp.float32, 'rwkv_k_a': _jnp.float32, 'rwkv_r_k': _jnp.float32, 'rwkv_ln_w': _jnp.float32, 'rwkv_ln_b': _jnp.float32, 'fox_w_in': _jnp.float32, 'fox_b_f': _jnp.float32, 'fox_w_out': _jnp.float32, 'final_norm': _jnp.float32}
MOMENT_SCALE = {'ffn1_norm': 7.959767e-02, 'ffn1_w_gu': 3.402264e-02, 'ffn1_w_down': 5.545018e-02, 'mix_norm': 1.091191e-01, 'ffn2_norm': 6.267011e-02, 'ffn2_w_gu': 2.706399e-02, 'ffn2_w_down': 4.431986e-02, 'ple_norm': 1.555282e-02, 'ple_w_gate': 1.554133e-02, 'ple_w_proj': 8.000625e-02, 'even_w_in': 8.722213e-02, 'even_w_out': 7.837723e-02, 'swa_sinks': 5.645719e-02, 'rwkv_mu': 1.525013e-01, 'rwkv_w0': 4.606317e-02, 'rwkv_w2': 5.792709e-03, 'rwkv_a0': 4.005328e-02, 'rwkv_a2': 3.774783e-02, 'rwkv_g2': 1.003692e-01, 'rwkv_k_k': 1.029260e-01, 'rwkv_k_a': 1.007291e-01, 'rwkv_r_k': 2.205637e-01, 'rwkv_ln_w': 1.175452e-01, 'rwkv_ln_b': 1.204079e-01, 'fox_w_in': 4.024675e-02, 'fox_b_f': 1.933428e-01, 'fox_w_out': 4.795258e-02, 'final_norm': 3.201645e+01}


def _to_microbatches(a, axis):
    t = _jnp.moveaxis(a, axis, 0)
    t = t.reshape((N_MICROBATCH, t.shape[0] // N_MICROBATCH) + t.shape[1:])
    return _jnp.moveaxis(t, 1, axis + 1)


def setup_inputs(seed: int = 0) -> dict:
    inp = _fwd_setup_inputs(seed)
    key = _jax.random.fold_in(_jax.random.key(seed), 7919)
    shape, _ = _output_shape()
    out = dict(inp)
    out["loss_target"] = _jax.random.normal(_jax.random.fold_in(key, 0), shape, _jnp.float32)
    for i, name in enumerate(TWIN_WEIGHTS):
        w = inp[name].astype(_jnp.float32)
        if MOMENT_SCALE is None:
            s = _jnp.sqrt(_jnp.mean(_jnp.square(w)) + 1e-30)
        else:
            s = MOMENT_SCALE[name]
        km, kv = _jax.random.split(_jax.random.fold_in(key, i + 1))
        out[name] = w
        out["m_" + name] = s * _jax.random.normal(km, w.shape, _jnp.float32)
        out["v_" + name] = (s * s) * _jax.random.uniform(kv, w.shape, _jnp.float32, 0.5, 1.5)
    if N_MICROBATCH > 1:
        for name, axis in PER_EXAMPLE_BATCH_AXIS.items():
            out[name] = _to_microbatches(out[name], axis)
    return {'x': out['x'], 'p': out['p'], 'ffn1_norm': out['ffn1_norm'], 'ffn1_w_gu': out['ffn1_w_gu'], 'ffn1_w_down': out['ffn1_w_down'], 'mix_norm': out['mix_norm'], 'ffn2_norm': out['ffn2_norm'], 'ffn2_w_gu': out['ffn2_w_gu'], 'ffn2_w_down': out['ffn2_w_down'], 'ple_norm': out['ple_norm'], 'ple_w_gate': out['ple_w_gate'], 'ple_w_proj': out['ple_w_proj'], 'even_w_in': out['even_w_in'], 'even_w_out': out['even_w_out'], 'swa_sinks': out['swa_sinks'], 'rwkv_mu': out['rwkv_mu'], 'rwkv_w0': out['rwkv_w0'], 'rwkv_w2': out['rwkv_w2'], 'rwkv_a0': out['rwkv_a0'], 'rwkv_a2': out['rwkv_a2'], 'rwkv_g2': out['rwkv_g2'], 'rwkv_k_k': out['rwkv_k_k'], 'rwkv_k_a': out['rwkv_k_a'], 'rwkv_r_k': out['rwkv_r_k'], 'rwkv_ln_w': out['rwkv_ln_w'], 'rwkv_ln_b': out['rwkv_ln_b'], 'fox_w_in': out['fox_w_in'], 'fox_b_f': out['fox_b_f'], 'fox_w_out': out['fox_w_out'], 'final_norm': out['final_norm'], 'loss_target': out['loss_target'], 'm_ffn1_norm': out['m_ffn1_norm'], 'm_ffn1_w_gu': out['m_ffn1_w_gu'], 'm_ffn1_w_down': out['m_ffn1_w_down'], 'm_mix_norm': out['m_mix_norm'], 'm_ffn2_norm': out['m_ffn2_norm'], 'm_ffn2_w_gu': out['m_ffn2_w_gu'], 'm_ffn2_w_down': out['m_ffn2_w_down'], 'm_ple_norm': out['m_ple_norm'], 'm_ple_w_gate': out['m_ple_w_gate'], 'm_ple_w_proj': out['m_ple_w_proj'], 'm_even_w_in': out['m_even_w_in'], 'm_even_w_out': out['m_even_w_out'], 'm_swa_sinks': out['m_swa_sinks'], 'm_rwkv_mu': out['m_rwkv_mu'], 'm_rwkv_w0': out['m_rwkv_w0'], 'm_rwkv_w2': out['m_rwkv_w2'], 'm_rwkv_a0': out['m_rwkv_a0'], 'm_rwkv_a2': out['m_rwkv_a2'], 'm_rwkv_g2': out['m_rwkv_g2'], 'm_rwkv_k_k': out['m_rwkv_k_k'], 'm_rwkv_k_a': out['m_rwkv_k_a'], 'm_rwkv_r_k': out['m_rwkv_r_k'], 'm_rwkv_ln_w': out['m_rwkv_ln_w'], 'm_rwkv_ln_b': out['m_rwkv_ln_b'], 'm_fox_w_in': out['m_fox_w_in'], 'm_fox_b_f': out['m_fox_b_f'], 'm_fox_w_out': out['m_fox_w_out'], 'm_final_norm': out['m_final_norm'], 'v_ffn1_norm': out['v_ffn1_norm'], 'v_ffn1_w_gu': out['v_ffn1_w_gu'], 'v_ffn1_w_down': out['v_ffn1_w_down'], 'v_mix_norm': out['v_mix_norm'], 'v_ffn2_norm': out['v_ffn2_norm'], 'v_ffn2_w_gu': out['v_ffn2_w_gu'], 'v_ffn2_w_down': out['v_ffn2_w_down'], 'v_ple_norm': out['v_ple_norm'], 'v_ple_w_gate': out['v_ple_w_gate'], 'v_ple_w_proj': out['v_ple_w_proj'], 'v_even_w_in': out['v_even_w_in'], 'v_even_w_out': out['v_even_w_out'], 'v_swa_sinks': out['v_swa_sinks'], 'v_rwkv_mu': out['v_rwkv_mu'], 'v_rwkv_w0': out['v_rwkv_w0'], 'v_rwkv_w2': out['v_rwkv_w2'], 'v_rwkv_a0': out['v_rwkv_a0'], 'v_rwkv_a2': out['v_rwkv_a2'], 'v_rwkv_g2': out['v_rwkv_g2'], 'v_rwkv_k_k': out['v_rwkv_k_k'], 'v_rwkv_k_a': out['v_rwkv_k_a'], 'v_rwkv_r_k': out['v_rwkv_r_k'], 'v_rwkv_ln_w': out['v_rwkv_ln_w'], 'v_rwkv_ln_b': out['v_rwkv_ln_b'], 'v_fox_w_in': out['v_fox_w_in'], 'v_fox_b_f': out['v_fox_b_f'], 'v_fox_w_out': out['v_fox_w_out'], 'v_final_norm': out['v_final_norm']}


def _loss(weights, diff, rest, loss_target):
    with _jax.named_scope("forward"):
        args = {**rest, TWIN_DIFF_INPUT: diff, **{k: w.astype(_WEIGHT_DTYPES[k]) for k, w in weights.items()}}
        y = _forward(args)
    with _jax.named_scope("loss_head"):
        err = _jnp.square(y.astype(_jnp.float32) - loss_target)
        return 0.5 * _jnp.sum(_jnp.mean(err, axis=-1)) if err.ndim else 0.5 * err


def _adamw(w, g, m, v):
    m = ADAM_B1 * m + (1.0 - ADAM_B1) * g
    v = ADAM_B2 * v + (1.0 - ADAM_B2) * _jnp.square(g)
    m_hat = m / (1.0 - ADAM_B1 ** ADAM_STEP)
    v_hat = v / (1.0 - ADAM_B2 ** ADAM_STEP)
    delta = -ADAM_LR * (m_hat / (_jnp.sqrt(v_hat) + ADAM_EPS) + ADAM_WD * w)
    return delta, m, v


def reference(x, p, ffn1_norm, ffn1_w_gu, ffn1_w_down, mix_norm, ffn2_norm, ffn2_w_gu, ffn2_w_down, ple_norm, ple_w_gate, ple_w_proj, even_w_in, even_w_out, swa_sinks, rwkv_mu, rwkv_w0, rwkv_w2, rwkv_a0, rwkv_a2, rwkv_g2, rwkv_k_k, rwkv_k_a, rwkv_r_k, rwkv_ln_w, rwkv_ln_b, fox_w_in, fox_b_f, fox_w_out, final_norm, loss_target, m_ffn1_norm, m_ffn1_w_gu, m_ffn1_w_down, m_mix_norm, m_ffn2_norm, m_ffn2_w_gu, m_ffn2_w_down, m_ple_norm, m_ple_w_gate, m_ple_w_proj, m_even_w_in, m_even_w_out, m_swa_sinks, m_rwkv_mu, m_rwkv_w0, m_rwkv_w2, m_rwkv_a0, m_rwkv_a2, m_rwkv_g2, m_rwkv_k_k, m_rwkv_k_a, m_rwkv_r_k, m_rwkv_ln_w, m_rwkv_ln_b, m_fox_w_in, m_fox_b_f, m_fox_w_out, m_final_norm, v_ffn1_norm, v_ffn1_w_gu, v_ffn1_w_down, v_mix_norm, v_ffn2_norm, v_ffn2_w_gu, v_ffn2_w_down, v_ple_norm, v_ple_w_gate, v_ple_w_proj, v_even_w_in, v_even_w_out, v_swa_sinks, v_rwkv_mu, v_rwkv_w0, v_rwkv_w2, v_rwkv_a0, v_rwkv_a2, v_rwkv_g2, v_rwkv_k_k, v_rwkv_k_a, v_rwkv_r_k, v_rwkv_ln_w, v_rwkv_ln_b, v_fox_w_in, v_fox_b_f, v_fox_w_out, v_final_norm):
    given = dict(x=x, p=p, ffn1_norm=ffn1_norm, ffn1_w_gu=ffn1_w_gu, ffn1_w_down=ffn1_w_down, mix_norm=mix_norm, ffn2_norm=ffn2_norm, ffn2_w_gu=ffn2_w_gu, ffn2_w_down=ffn2_w_down, ple_norm=ple_norm, ple_w_gate=ple_w_gate, ple_w_proj=ple_w_proj, even_w_in=even_w_in, even_w_out=even_w_out, swa_sinks=swa_sinks, rwkv_mu=rwkv_mu, rwkv_w0=rwkv_w0, rwkv_w2=rwkv_w2, rwkv_a0=rwkv_a0, rwkv_a2=rwkv_a2, rwkv_g2=rwkv_g2, rwkv_k_k=rwkv_k_k, rwkv_k_a=rwkv_k_a, rwkv_r_k=rwkv_r_k, rwkv_ln_w=rwkv_ln_w, rwkv_ln_b=rwkv_ln_b, fox_w_in=fox_w_in, fox_b_f=fox_b_f, fox_w_out=fox_w_out, final_norm=final_norm, loss_target=loss_target, m_ffn1_norm=m_ffn1_norm, m_ffn1_w_gu=m_ffn1_w_gu, m_ffn1_w_down=m_ffn1_w_down, m_mix_norm=m_mix_norm, m_ffn2_norm=m_ffn2_norm, m_ffn2_w_gu=m_ffn2_w_gu, m_ffn2_w_down=m_ffn2_w_down, m_ple_norm=m_ple_norm, m_ple_w_gate=m_ple_w_gate, m_ple_w_proj=m_ple_w_proj, m_even_w_in=m_even_w_in, m_even_w_out=m_even_w_out, m_swa_sinks=m_swa_sinks, m_rwkv_mu=m_rwkv_mu, m_rwkv_w0=m_rwkv_w0, m_rwkv_w2=m_rwkv_w2, m_rwkv_a0=m_rwkv_a0, m_rwkv_a2=m_rwkv_a2, m_rwkv_g2=m_rwkv_g2, m_rwkv_k_k=m_rwkv_k_k, m_rwkv_k_a=m_rwkv_k_a, m_rwkv_r_k=m_rwkv_r_k, m_rwkv_ln_w=m_rwkv_ln_w, m_rwkv_ln_b=m_rwkv_ln_b, m_fox_w_in=m_fox_w_in, m_fox_b_f=m_fox_b_f, m_fox_w_out=m_fox_w_out, m_final_norm=m_final_norm, v_ffn1_norm=v_ffn1_norm, v_ffn1_w_gu=v_ffn1_w_gu, v_ffn1_w_down=v_ffn1_w_down, v_mix_norm=v_mix_norm, v_ffn2_norm=v_ffn2_norm, v_ffn2_w_gu=v_ffn2_w_gu, v_ffn2_w_down=v_ffn2_w_down, v_ple_norm=v_ple_norm, v_ple_w_gate=v_ple_w_gate, v_ple_w_proj=v_ple_w_proj, v_even_w_in=v_even_w_in, v_even_w_out=v_even_w_out, v_swa_sinks=v_swa_sinks, v_rwkv_mu=v_rwkv_mu, v_rwkv_w0=v_rwkv_w0, v_rwkv_w2=v_rwkv_w2, v_rwkv_a0=v_rwkv_a0, v_rwkv_a2=v_rwkv_a2, v_rwkv_g2=v_rwkv_g2, v_rwkv_k_k=v_rwkv_k_k, v_rwkv_k_a=v_rwkv_k_a, v_rwkv_r_k=v_rwkv_r_k, v_rwkv_ln_w=v_rwkv_ln_w, v_rwkv_ln_b=v_rwkv_ln_b, v_fox_w_in=v_fox_w_in, v_fox_b_f=v_fox_b_f, v_fox_w_out=v_fox_w_out, v_final_norm=v_final_norm)
    weights = {n: given[n] for n in TWIN_WEIGHTS}
    shared = {n: given[n] for n in SHARED_INPUTS}
    per_example = {n: given[n] for n in ['x', 'p']}
    grad_fn = _jax.value_and_grad(_loss, argnums=(0, 1))

    def one_microbatch(ex, loss_target):
        ex = dict(ex)
        diff = ex.pop(TWIN_DIFF_INPUT)
        return grad_fn(weights, diff, {**shared, **ex}, loss_target)

    if N_MICROBATCH == 1:
        loss, (grad_w, grad_x) = one_microbatch(per_example, given["loss_target"])
    else:
        def body(carry, xs):
            loss_sum, grad_sum = carry
            l_k, (gw_k, gx_k) = one_microbatch(xs[0], xs[1])
            with _jax.named_scope("update"):
                return (loss_sum + l_k, _jax.tree.map(_jnp.add, grad_sum, gw_k)), gx_k

        init = (_jnp.zeros((), _jnp.float32), _jax.tree.map(_jnp.zeros_like, weights))
        (loss, grad_w), grad_x = _jax.lax.scan(body, init, (per_example, given["loss_target"]))
    with _jax.named_scope("update"):
        delta_w, new_m, new_v = {}, {}, {}
        for n in TWIN_WEIGHTS:
            delta_w[n], new_m[n], new_v[n] = _adamw(weights[n], grad_w[n], given["m_" + n], given["v_" + n])
    return (loss, grad_x, *[grad_w[n] for n in TWIN_WEIGHTS], *[delta_w[n] for n in TWIN_WEIGHTS],
            *[new_m[n] for n in TWIN_WEIGHTS], *[new_v[n] for n in TWIN_WEIGHTS])
```

```python
import functools
import math

import numpy as np
import jax
import jax.numpy as jnp
from jax import lax
from jax.experimental import pallas as pl
from jax.experimental.pallas import tpu as pltpu

F32 = jnp.float32
BF16 = jnp.bfloat16

D_MODEL = 1024
HEAD_DIM = 64
BLOCK = 128
SWA_HEADS = 8
SWA_KV_HEADS = 2
SWA_GROUP = 4
RWKV_HEADS = 8
RWKV_DIM = 512
FOX_HEADS = 16
FOX_DIM = 1024
D_FF = 2816
NORM_EPS = 1e-6
GN_EPS = 64e-5
L2_EPS = 1e-12
SWA_Q = 512
SWA_KV = 128
SWA_COLS = 768
FOX_IN_PAD = 3200
N_DEV = 8
ADAM_LR = 0.001
ADAM_B1 = 0.9
ADAM_B2 = 0.999
ADAM_EPS = 1e-08
ADAM_WD = 0.01
ADAM_STEP = 10

V7X_VMEM_LIMIT = 56 * 1024 * 1024
SCAN_GROUP = 8
SCAN_CHUNK = 32

_NN = (((1,), (0,)), ((), ()))
_NT = (((1,), (1,)), ((), ()))
_TN = (((0,), (0,)), ((), ()))
_DIMS = {"nn": _NN, "nt": _NT, "tn": _TN}


def _pick(n, target, mult=128):
    best = None
    for t in range(mult, min(n, target) + 1, mult):
        if n % t == 0:
            best = t
    return best or n


def _cparams(sem):
    return pltpu.CompilerParams(dimension_semantics=sem, vmem_limit_bytes=V7X_VMEM_LIMIT)


def _dot(a, b, dims):
    return lax.dot_general(a.astype(BF16), b.astype(BF16), dims, preferred_element_type=F32)


@jax.custom_vjp
def bdot(a, b):
    return _dot(a, b, _NN)


def _bdot_fwd(a, b):
    return _dot(a, b, _NN), (a, b)


def _bdot_bwd(res, g):
    a, b = res
    return _dot(g, b, _NT), _dot(a, g, _TN)


bdot.defvjp(_bdot_fwd, _bdot_bwd)


@jax.custom_vjp
def bdot_nt(a, b):
    return _dot(a, b, _NT)


def _bdot_nt_fwd(a, b):
    return _dot(a, b, _NT), (a, b)


def _bdot_nt_bwd(res, g):
    a, b = res
    return _dot(g, b, _NN), _dot(g, a, _TN)


bdot_nt.defvjp(_bdot_nt_fwd, _bdot_nt_bwd)


def _segsum(x, bd):
    return jnp.dot(x, bd, precision=lax.Precision.HIGHEST, preferred_element_type=F32)


def _sigmoid(x):
    return 1.0 / (1.0 + jnp.exp(-x))


def _softplus(x):
    return jnp.maximum(x, 0.0) + jnp.log(1.0 + jnp.exp(-jnp.abs(x)))


def matmul(a, b, mode, name, out_dtype=F32, scale=1.0, res=None, tm=512, tn=1408, tk=1024):
    if mode == "nn":
        (M, K), (K2, N) = a.shape, b.shape
    elif mode == "nt":
        (M, K), (N, K2) = a.shape, b.shape
    else:
        (K, M), (K2, N) = a.shape, b.shape
    assert K == K2, (a.shape, b.shape, mode)
    tm, tn, tk = _pick(M, tm), _pick(N, tn), _pick(K, tk)
    nk = K // tk
    has_res = res is not None

    def body(*refs):
        if has_res:
            a_ref, b_ref, r_ref, o_ref, acc = refs
        else:
            a_ref, b_ref, o_ref, acc = refs
        kk = pl.program_id(2)

        @pl.when(kk == 0)
        def _():
            acc[...] = jnp.zeros_like(acc)

        acc[...] += _dot(a_ref[...], b_ref[...], _DIMS[mode])

        @pl.when(kk == nk - 1)
        def _():
            v = acc[...]
            if scale != 1.0:
                v = v * scale
            if has_res:
                v = v + r_ref[...].astype(F32)
            o_ref[...] = v.astype(out_dtype)

    if mode == "tn":
        a_spec = pl.BlockSpec((tk, tm), lambda i, j, k: (k, i))
    else:
        a_spec = pl.BlockSpec((tm, tk), lambda i, j, k: (i, k))
    if mode == "nt":
        b_spec = pl.BlockSpec((tn, tk), lambda i, j, k: (j, k))
    else:
        b_spec = pl.BlockSpec((tk, tn), lambda i, j, k: (k, j))
    o_spec = pl.BlockSpec((tm, tn), lambda i, j, k: (i, j))
    in_specs = [a_spec, b_spec] + ([o_spec] if has_res else [])
    args = (a, b) + ((res,) if has_res else ())
    return pl.pallas_call(
        body,
        grid=(M // tm, N // tn, nk),
        in_specs=in_specs,
        out_specs=o_spec,
        out_shape=jax.ShapeDtypeStruct((M, N), out_dtype),
        scratch_shapes=[pltpu.VMEM((tm, tn), F32)],
        compiler_params=_cparams(("parallel", "parallel", "arbitrary")),
        name=name,
    )(*args)


def _row_spec(r, tm):
    if isinstance(r, tuple):
        arr, width, blk = r
        return arr, pl.BlockSpec((tm, width), lambda i, blk=blk: (i, blk))
    return r, pl.BlockSpec((tm, r.shape[1]), lambda i: (i, 0))


def _whole_spec(p):
    return pl.BlockSpec(p.shape, lambda i: (0,) * p.ndim)


def rowwise(fn, rows, params, outs, name, tm=256):
    arrs, specs = zip(*[_row_spec(r, tm) for r in rows])
    S = arrs[0].shape[0]
    tm = min(tm, S)
    arrs, specs = zip(*[_row_spec(r, tm) for r in rows])
    n_in = len(rows) + len(params)

    def body(*refs):
        res = fn(*[r[...] for r in refs[:n_in]])
        for o_ref, v in zip(refs[n_in:], res):
            o_ref[...] = v.astype(o_ref.dtype)

    return pl.pallas_call(
        body,
        grid=(S // tm,),
        in_specs=list(specs) + [_whole_spec(p) for p in params],
        out_specs=[pl.BlockSpec((tm, c), lambda i: (i, 0)) for c, _ in outs],
        out_shape=[jax.ShapeDtypeStruct((S, c), dt) for c, dt in outs],
        compiler_params=_cparams(("parallel",)),
        name=name,
    )(*arrs, *params)


def rowwise_vjp(fn, rows, params, cots, name, need=None, row_dtype=F32, consts=(), tm=256):
    nr, npar, nc, nk = len(rows), len(params), len(cots), len(consts)
    need = [True] * nr if need is None else need
    arrs, _ = zip(*[_row_spec(r, tm) for r in rows])
    S = arrs[0].shape[0]
    tm = min(tm, S)
    arrs, specs = zip(*[_row_spec(r, tm) for r in rows])
    carrs, cspecs = zip(*[_row_spec(c, tm) for c in cots])
    widths = [s.block_shape[1] for s in specs]
    n_in = nr + npar + nk + nc

    def body(*refs):
        i = pl.program_id(0)
        xs = [r[...].astype(F32) for r in refs[:nr]]
        ps = [r[...] for r in refs[nr:nr + npar]]
        ks = [r[...] for r in refs[nr + npar:nr + npar + nk]]
        cs = [r[...].astype(F32) for r in refs[nr + npar + nk:n_in]]
        outs, vjp = jax.vjp(lambda *a: fn(*a, *ks), *xs, *ps)
        grads = vjp(tuple(cs))
        o = n_in
        for j in range(nr):
            if need[j]:
                refs[o][...] = grads[j].astype(refs[o].dtype)
                o += 1
        for j in range(npar):
            g_ref = refs[o + j]

            @pl.when(i == 0)
            def _(g_ref=g_ref):
                g_ref[...] = jnp.zeros_like(g_ref)

            g_ref[...] += grads[nr + j]

    out_specs = [pl.BlockSpec((tm, w), lambda i: (i, 0)) for w, nd in zip(widths, need) if nd]
    out_shape = [jax.ShapeDtypeStruct((S, w), row_dtype) for w, nd in zip(widths, need) if nd]
    out_specs += [_whole_spec(p) for p in params]
    out_shape += [jax.ShapeDtypeStruct(p.shape, F32) for p in params]
    res = pl.pallas_call(
        body,
        grid=(S // tm,),
        in_specs=list(specs) + [_whole_spec(p) for p in params] + [_whole_spec(k) for k in consts] + list(cspecs),
        out_specs=out_specs,
        out_shape=out_shape,
        compiler_params=_cparams(("arbitrary",)),
        name=name,
    )(*arrs, *params, *consts, *carrs)
    nrow = sum(need)
    return list(res[:nrow]), list(res[nrow:])


def _rms(x, g):
    return x * lax.rsqrt(jnp.mean(x * x, axis=-1, keepdims=True) + NORM_EPS) * g


def _f_rms(x, g):
    return (_rms(x, g),)


def _f_act(g, u):
    return (g * _sigmoid(g) * u,)


def _f_ple(x, z, pp):
    return (x + _sigmoid(z) * pp,)


def _f_mix(h, sh, mu):
    return (h + (sh - h) * mu,)


def _f_logf(fz, bf):
    return (-_softplus(-(fz + bf)),)


def _f_rwkv_pre(hk, hw, ha, hg, w0, w2, a0, a2, g2, k_k, k_a, bd):
    wlog = -_softplus(-(w0 + bdot(jnp.tanh(hw), w2))) - 0.5
    a = _sigmoid(a0 + bdot(ha, a2))
    g = bdot(_sigmoid(hg), g2)
    kk = hk * k_k
    kk = kk / jnp.maximum(jnp.sqrt(_segsum(kk * kk, bd)), L2_EPS)
    k2 = hk * (1.0 + (a - 1.0) * k_a)
    decay = jnp.exp(-jnp.exp(wlog))
    return decay, k2, kk, kk * a, g


def _f_rwkv_post(y, r, k2, v, g, ln_w, ln_b, r_k, bd):
    mean = _segsum(y, bd) * (1.0 / HEAD_DIM)
    d = y - mean
    var = _segsum(d * d, bd) * (1.0 / HEAD_DIM)
    yn = d * lax.rsqrt(var + GN_EPS) * ln_w + ln_b
    yn = yn + _segsum(r * k2 * r_k, bd) * v
    return (yn * g,)


def loss_head(x, target, gf, tm=256):
    S, D = x.shape
    tm = min(tm, S)

    def f(xt, g, tt):
        err = _rms(xt, g) - tt
        return 0.5 * jnp.sum(err * err) * (1.0 / D)

    def body(x_ref, t_ref, g_ref, dx_ref, dg_ref, l_ref):
        i = pl.program_id(0)
        val, (dx, dg) = jax.value_and_grad(f, argnums=(0, 1))(x_ref[...], g_ref[...], t_ref[...])

        @pl.when(i == 0)
        def _():
            dg_ref[...] = jnp.zeros_like(dg_ref)
            l_ref[...] = jnp.zeros_like(l_ref)

        dx_ref[...] = dx
        dg_ref[...] += dg
        l_ref[...] += jnp.full(l_ref.shape, val, F32)

    row = pl.BlockSpec((tm, D), lambda i: (i, 0))
    vec = pl.BlockSpec((1, D), lambda i: (0, 0))
    return pl.pallas_call(
        body,
        grid=(S // tm,),
        in_specs=[row, row, vec],
        out_specs=[row, vec, pl.BlockSpec((1, 128), lambda i: (0, 0))],
        out_shape=[jax.ShapeDtypeStruct((S, D), F32), jax.ShapeDtypeStruct((1, D), F32),
                   jax.ShapeDtypeStruct((1, 128), F32)],
        compiler_params=_cparams(("arbitrary",)),
        name="loss_head",
    )(x, target, gf)


def _swa_block(q, kp, kc, vp, vc, sink, slope, n):
    k = jnp.concatenate([kp, kc], axis=0)
    v = jnp.concatenate([vp, vc], axis=0)
    rows = q.shape[0]
    logits = bdot_nt(q, k) * (HEAD_DIM ** -0.5)
    qi = lax.broadcasted_iota(jnp.int32, (rows, 2 * BLOCK), 0) & (BLOCK - 1)
    ki = lax.broadcasted_iota(jnp.int32, (rows, 2 * BLOCK), 1)
    dist = qi + BLOCK - ki
    valid = (dist >= 0) & (dist < BLOCK) & ((n - 1) * BLOCK + ki >= 0)
    logits = logits - slope * dist.astype(F32)
    logits = jnp.where(valid, logits, -jnp.inf)
    m = jnp.maximum(jnp.max(logits, axis=-1, keepdims=True), sink)
    pr = jnp.exp(logits - m)
    denom = jnp.sum(pr, axis=-1, keepdims=True) + jnp.exp(sink - m)
    return bdot(pr / denom, v)


def _swa_specs(S):
    nb = S // BLOCK
    q_spec = pl.BlockSpec((None, SWA_GROUP, BLOCK, HEAD_DIM), lambda h, n: (h, 0, n, 0))
    kc_spec = pl.BlockSpec((None, BLOCK, HEAD_DIM), lambda h, n: (h, n, 0))
    kp_spec = pl.BlockSpec((None, BLOCK, HEAD_DIM), lambda h, n: (h, jnp.maximum(n - 1, 0), 0))
    col_spec = pl.BlockSpec((None, SWA_GROUP * BLOCK, 1), lambda h, n: (h, 0, 0))
    return nb, q_spec, kp_spec, kc_spec, col_spec


def swa_fwd(q, k, v, sink_col, slope_col):
    S = q.shape[2]
    nb, q_spec, kp_spec, kc_spec, col_spec = _swa_specs(S)

    def body(q_ref, kp_ref, kc_ref, vp_ref, vc_ref, s_ref, a_ref, o_ref):
        n = pl.program_id(1)
        qq = q_ref[...].reshape(SWA_GROUP * BLOCK, HEAD_DIM)
        out = _swa_block(qq, kp_ref[...], kc_ref[...], vp_ref[...], vc_ref[...], s_ref[...], a_ref[...], n)
        o_ref[...] = out.reshape(SWA_GROUP, BLOCK, HEAD_DIM)

    return pl.pallas_call(
        body,
        grid=(SWA_KV_HEADS, nb),
        in_specs=[q_spec, kp_spec, kc_spec, kp_spec, kc_spec, col_spec, col_spec],
        out_specs=q_spec,
        out_shape=jax.ShapeDtypeStruct(q.shape, F32),
        compiler_params=_cparams(("parallel", "parallel")),
        name="swa_fwd",
    )(q, k, k, v, v, sink_col, slope_col)


def swa_bwd(q, k, v, sink_col, slope_col, dout):
    S = q.shape[2]
    nb, q_spec, kp_spec, kc_spec, col_spec = _swa_specs(S)

    def body(q_ref, kp_ref, kc_ref, vp_ref, vc_ref, s_ref, a_ref, do_ref,
             dq_ref, dkp_ref, dkc_ref, dvp_ref, dvc_ref, ds_ref):
        n = pl.program_id(1)
        qq = q_ref[...].reshape(SWA_GROUP * BLOCK, HEAD_DIM)
        slope = a_ref[...]
        f = lambda a, b, c, d, e, s: _swa_block(a, b, c, d, e, s, slope, n)
        _, vjp = jax.vjp(f, qq, kp_ref[...], kc_ref[...], vp_ref[...], vc_ref[...], s_ref[...])
        dq, dkp, dkc, dvp, dvc, ds = vjp(do_ref[...].reshape(SWA_GROUP * BLOCK, HEAD_DIM))
        dq_ref[...] = dq.reshape(SWA_GROUP, BLOCK, HEAD_DIM)
        dkp_ref[...] = dkp
        dkc_ref[...] = dkc
        dvp_ref[...] = dvp
        dvc_ref[...] = dvc

        @pl.when(n == 0)
        def _():
            ds_ref[...] = jnp.zeros_like(ds_ref)

        ds_ref[...] += ds

    kv_shape = jax.ShapeDtypeStruct(k.shape, F32)
    return pl.pallas_call(
        body,
        grid=(SWA_KV_HEADS, nb),
        in_specs=[q_spec, kp_spec, kc_spec, kp_spec, kc_spec, col_spec, col_spec, q_spec],
        out_specs=[q_spec, kc_spec, kc_spec, kc_spec, kc_spec, col_spec],
        out_shape=[jax.ShapeDtypeStruct(q.shape, F32), kv_shape, kv_shape, kv_shape, kv_shape,
                   jax.ShapeDtypeStruct(sink_col.shape, F32)],
        compiler_params=_cparams(("parallel", "arbitrary")),
        name="swa_bwd",
    )(q, k, k, v, v, sink_col, slope_col, dout)


def rwkv_scan_fwd(r, w, k, kk, b, vT):
    S = r.shape[0]
    H, N, G = RWKV_HEADS, HEAD_DIM, SCAN_GROUP
    chunk = min(SCAN_CHUNK, S)
    nchunk, ng = S // chunk, chunk // G

    def body(r_ref, w_ref, k_ref, kk_ref, b_ref, vT_ref, yT_ref, ck_ref, S_ref):
        c = pl.program_id(0)

        @pl.when(c == 0)
        def _():
            S_ref[...] = jnp.zeros_like(S_ref)

        ck_ref[...] = S_ref[...]
        lane = lax.broadcasted_iota(jnp.int32, (N, G), 1)

        def group(g, carry):
            base = pl.multiple_of(g * G, G)
            rr = r_ref[pl.ds(base, G), :]
            ww = w_ref[pl.ds(base, G), :]
            kr = k_ref[pl.ds(base, G), :]
            kkr = kk_ref[pl.ds(base, G), :]
            br = b_ref[pl.ds(base, G), :]
            vc = vT_ref[g]
            for h in range(H):
                sl = slice(h * N, (h + 1) * N)
                St = S_ref[h]
                ys = jnp.zeros((N, G), F32)
                for tt in range(G):
                    u = jnp.sum(St * kkr[tt:tt + 1, sl], axis=1, keepdims=True)
                    St = St * ww[tt:tt + 1, sl] - u * br[tt:tt + 1, sl] + vc[sl, tt:tt + 1] * kr[tt:tt + 1, sl]
                    y = jnp.sum(St * rr[tt:tt + 1, sl], axis=1, keepdims=True)
                    ys = jnp.where(lane == tt, y, ys)
                S_ref[h] = St
                yT_ref[g, sl, :] = ys
            return carry

        lax.fori_loop(0, ng, group, 0)

    row = pl.BlockSpec((chunk, H * N), lambda c: (c, 0))
    col = pl.BlockSpec((ng, H * N, G), lambda c: (c, 0, 0))
    return pl.pallas_call(
        body,
        grid=(nchunk,),
        in_specs=[row] * 5 + [col],
        out_specs=[col, pl.BlockSpec((None, H, N, N), lambda c: (c, 0, 0, 0))],
        out_shape=[jax.ShapeDtypeStruct((S // G, H * N, G), F32),
                   jax.ShapeDtypeStruct((nchunk, H, N, N), F32)],
        scratch_shapes=[pltpu.VMEM((H, N, N), F32)],
        compiler_params=_cparams(("arbitrary",)),
        name="rwkv_scan_fwd",
    )(r, w, k, kk, b, vT)


def rwkv_scan_bwd(r, w, k, kk, b, vT, dyT, ckpt):
    S = r.shape[0]
    H, N, G = RWKV_HEADS, HEAD_DIM, SCAN_GROUP
    chunk = min(SCAN_CHUNK, S)
    nchunk, ng = S // chunk, chunk // G

    def body(r_ref, w_ref, k_ref, kk_ref, b_ref, vT_ref, dyT_ref, ck_ref,
             dr_ref, dw_ref, dk_ref, dkk_ref, db_ref, dvT_ref, G_ref, sbuf):
        c = pl.program_id(0)

        @pl.when(c == 0)
        def _():
            G_ref[...] = jnp.zeros_like(G_ref)

        sbuf[0] = ck_ref[...]

        def fgroup(g, carry):
            base = pl.multiple_of(g * G, G)
            ww = w_ref[pl.ds(base, G), :]
            kr = k_ref[pl.ds(base, G), :]
            kkr = kk_ref[pl.ds(base, G), :]
            br = b_ref[pl.ds(base, G), :]
            vc = vT_ref[g]
            for h in range(H):
                sl = slice(h * N, (h + 1) * N)
                St = sbuf[base, h]
                for tt in range(G):
                    u = jnp.sum(St * kkr[tt:tt + 1, sl], axis=1, keepdims=True)
                    St = St * ww[tt:tt + 1, sl] - u * br[tt:tt + 1, sl] + vc[sl, tt:tt + 1] * kr[tt:tt + 1, sl]
                    sbuf[base + tt + 1, h] = St
            return carry

        lax.fori_loop(0, ng, fgroup, 0)

        lane = lax.broadcasted_iota(jnp.int32, (N, G), 1)
        sub = lax.broadcasted_iota(jnp.int32, (G, N), 0)

        def bgroup(gi, carry):
            g = ng - 1 - gi
            base = pl.multiple_of(g * G, G)
            rr = r_ref[pl.ds(base, G), :]
            ww = w_ref[pl.ds(base, G), :]
            kr = k_ref[pl.ds(base, G), :]
            kkr = kk_ref[pl.ds(base, G), :]
            br = b_ref[pl.ds(base, G), :]
            vc = vT_ref[g]
            dyc = dyT_ref[g]
            for h in range(H):
                sl = slice(h * N, (h + 1) * N)
                Gt = G_ref[h]
                a_dr = jnp.zeros((G, N), F32)
                a_dw = jnp.zeros((G, N), F32)
                a_dk = jnp.zeros((G, N), F32)
                a_dkk = jnp.zeros((G, N), F32)
                a_db = jnp.zeros((G, N), F32)
                a_dv = jnp.zeros((N, G), F32)
                for tt in reversed(range(G)):
                    Sp = sbuf[base + tt, h]
                    Sc = sbuf[base + tt + 1, h]
                    dy = dyc[sl, tt:tt + 1]
                    vcol = vc[sl, tt:tt + 1]
                    r_t, w_t, k_t = rr[tt:tt + 1, sl], ww[tt:tt + 1, sl], kr[tt:tt + 1, sl]
                    kk_t, b_t = kkr[tt:tt + 1, sl], br[tt:tt + 1, sl]
                    Gt = Gt + dy * r_t
                    d_r = jnp.sum(Sc * dy, axis=0, keepdims=True)
                    u = jnp.sum(Sp * kk_t, axis=1, keepdims=True)
                    d_w = jnp.sum(Gt * Sp, axis=0, keepdims=True)
                    du = -jnp.sum(Gt * b_t, axis=1, keepdims=True)
                    d_b = -jnp.sum(Gt * u, axis=0, keepdims=True)
                    d_v = jnp.sum(Gt * k_t, axis=1, keepdims=True)
                    d_k = jnp.sum(Gt * vcol, axis=0, keepdims=True)
                    d_kk = jnp.sum(Sp * du, axis=0, keepdims=True)
                    Gt = Gt * w_t + du * kk_t
                    a_dr = jnp.where(sub == tt, d_r, a_dr)
                    a_dw = jnp.where(sub == tt, d_w, a_dw)
                    a_dk = jnp.where(sub == tt, d_k, a_dk)
                    a_dkk = jnp.where(sub == tt, d_kk, a_dkk)
                    a_db = jnp.where(sub == tt, d_b, a_db)
                    a_dv = jnp.where(lane == tt, d_v, a_dv)
                G_ref[h] = Gt
                dr_ref[pl.ds(base, G), sl] = a_dr
                dw_ref[pl.ds(base, G), sl] = a_dw
                dk_ref[pl.ds(base, G), sl] = a_dk
                dkk_ref[pl.ds(base, G), sl] = a_dkk
                db_ref[pl.ds(base, G), sl] = a_db
                dvT_ref[g, sl, :] = a_dv
            return carry

        lax.fori_loop(0, ng, bgroup, 0)

    rev = lambda c: nchunk - 1 - c
    row = pl.BlockSpec((chunk, H * N), lambda c: (rev(c), 0))
    col = pl.BlockSpec((ng, H * N, G), lambda c: (rev(c), 0, 0))
    rshape = jax.ShapeDtypeStruct((S, H * N), F32)
    return pl.pallas_call(
        body,
        grid=(nchunk,),
        in_specs=[row] * 5 + [col, col, pl.BlockSpec((None, H, N, N), lambda c: (rev(c), 0, 0, 0))],
        out_specs=[row] * 5 + [col],
        out_shape=[rshape] * 5 + [jax.ShapeDtypeStruct((S // G, H * N, G), F32)],
        scratch_shapes=[pltpu.VMEM((H, N, N), F32), pltpu.VMEM((chunk + 1, H, N, N), F32)],
        compiler_params=_cparams(("arbitrary",)),
        name="rwkv_scan_bwd",
    )(r, w, k, kk, b, vT, dyT, ckpt)


def _to_cols(a):
    S, C = a.shape
    return a.reshape(S // SCAN_GROUP, SCAN_GROUP, C).transpose(0, 2, 1)


def _from_cols(a):
    n, C, G = a.shape
    return a.transpose(0, 2, 1).reshape(n * G, C)


def seq_cumsum(x, reverse, name):
    S, C = x.shape
    tb = min(256, S)
    nb = S // tb

    def body(x_ref, o_ref, carry):
        i = pl.program_id(0)

        @pl.when(i == 0)
        def _():
            carry[...] = jnp.zeros_like(carry)

        ri = lax.broadcasted_iota(jnp.int32, (tb, tb), 0)
        ci = lax.broadcasted_iota(jnp.int32, (tb, tb), 1)
        tri = jnp.where((ci >= ri) if reverse else (ci <= ri), 1.0, 0.0).astype(F32)
        xb = x_ref[...]
        out = jnp.dot(tri, xb, precision=lax.Precision.HIGHEST, preferred_element_type=F32) + carry[...]
        o_ref[...] = out
        carry[...] = carry[...] + jnp.sum(xb, axis=0, keepdims=True)

    idx = (lambda i: (nb - 1 - i, 0)) if reverse else (lambda i: (i, 0))
    return pl.pallas_call(
        body,
        grid=(nb,),
        in_specs=[pl.BlockSpec((tb, C), idx)],
        out_specs=pl.BlockSpec((tb, C), idx),
        out_shape=jax.ShapeDtypeStruct((S, C), F32),
        scratch_shapes=[pltpu.VMEM((1, C), F32)],
        compiler_params=_cparams(("arbitrary",)),
        name=name,
    )(x)


def _fox_logits(q, k, cq, ck, qi, ki, tq, tk):
    s = _dot(q, k, _NT) * (HEAD_DIM ** -0.5) + cq - ck
    row = qi * tq + lax.broadcasted_iota(jnp.int32, (tq, tk), 0)
    col = ki * tk + lax.broadcasted_iota(jnp.int32, (tq, tk), 1)
    return jnp.where(col <= row, s, -jnp.inf)


def fox_fwd(q, k, v, c_col, c_row):
    Hh, S, Dh = q.shape
    tq = tk = min(512, S)
    nq, nk = S // tq, S // tk

    def body(q_ref, k_ref, v_ref, cq_ref, ck_ref, o_ref, lse_ref, m_s, l_s, acc_s):
        qi, ki = pl.program_id(1), pl.program_id(2)

        @pl.when(ki == 0)
        def _():
            m_s[...] = jnp.full_like(m_s, -jnp.inf)
            l_s[...] = jnp.zeros_like(l_s)
            acc_s[...] = jnp.zeros_like(acc_s)

        @pl.when(ki <= qi)
        def _():
            s = _fox_logits(q_ref[...], k_ref[...], cq_ref[...], ck_ref[...], qi, ki, tq, tk)
            m_new = jnp.maximum(m_s[...], jnp.max(s, axis=-1, keepdims=True))
            alpha = jnp.exp(m_s[...] - m_new)
            p = jnp.exp(s - m_new)
            l_s[...] = alpha * l_s[...] + jnp.sum(p, axis=-1, keepdims=True)
            acc_s[...] = alpha * acc_s[...] + _dot(p, v_ref[...], _NN)
            m_s[...] = m_new

        @pl.when(ki == nk - 1)
        def _():
            o_ref[...] = acc_s[...] / l_s[...]
            lse_ref[...] = m_s[...] + jnp.log(l_s[...])

    qs = pl.BlockSpec((None, tq, Dh), lambda h, i, j: (h, i, 0))
    ks = pl.BlockSpec((None, tk, Dh), lambda h, i, j: (h, jnp.minimum(i, j), 0))
    cqs = pl.BlockSpec((None, tq, 1), lambda h, i, j: (h, i, 0))
    cks = pl.BlockSpec((None, 1, tk), lambda h, i, j: (h, 0, jnp.minimum(i, j)))
    return pl.pallas_call(
        body,
        grid=(Hh, nq, nk),
        in_specs=[qs, ks, ks, cqs, cks],
        out_specs=[qs, cqs],
        out_shape=[jax.ShapeDtypeStruct((Hh, S, Dh), F32), jax.ShapeDtypeStruct((Hh, S, 1), F32)],
        scratch_shapes=[pltpu.VMEM((tq, 1), F32), pltpu.VMEM((tq, 1), F32), pltpu.VMEM((tq, Dh), F32)],
        compiler_params=_cparams(("parallel", "parallel", "arbitrary")),
        name="fox_fwd",
    )(q, k, v, c_col, c_row)


def fox_bwd_dq(q, k, v, c_col, c_row, o, lse, do):
    Hh, S, Dh = q.shape
    tq = tk = min(512, S)
    nq, nk = S // tq, S // tk

    def body(q_ref, k_ref, v_ref, cq_ref, ck_ref, o_ref, lse_ref, do_ref, dq_ref, dr_ref, acc_s, row_s):
        qi, ki = pl.program_id(1), pl.program_id(2)

        @pl.when(ki == 0)
        def _():
            acc_s[...] = jnp.zeros_like(acc_s)
            row_s[...] = jnp.zeros_like(row_s)

        @pl.when(ki <= qi)
        def _():
            s = _fox_logits(q_ref[...], k_ref[...], cq_ref[...], ck_ref[...], qi, ki, tq, tk)
            p = jnp.exp(s - lse_ref[...])
            do_t = do_ref[...]
            delta = jnp.sum(do_t * o_ref[...], axis=-1, keepdims=True)
            dp = _dot(do_t, v_ref[...], _NT)
            ds = p * (dp - delta)
            acc_s[...] += _dot(ds, k_ref[...], _NN)
            row_s[...] += jnp.sum(ds, axis=-1, keepdims=True)

        @pl.when(ki == nk - 1)
        def _():
            dq_ref[...] = acc_s[...] * (HEAD_DIM ** -0.5)
            dr_ref[...] = row_s[...]

    qs = pl.BlockSpec((None, tq, Dh), lambda h, i, j: (h, i, 0))
    ks = pl.BlockSpec((None, tk, Dh), lambda h, i, j: (h, jnp.minimum(i, j), 0))
    cqs = pl.BlockSpec((None, tq, 1), lambda h, i, j: (h, i, 0))
    cks = pl.BlockSpec((None, 1, tk), lambda h, i, j: (h, 0, jnp.minimum(i, j)))
    return pl.pallas_call(
        body,
        grid=(Hh, nq, nk),
        in_specs=[qs, ks, ks, cqs, cks, qs, cqs, qs],
        out_specs=[qs, cqs],
        out_shape=[jax.ShapeDtypeStruct((Hh, S, Dh), F32), jax.ShapeDtypeStruct((Hh, S, 1), F32)],
        scratch_shapes=[pltpu.VMEM((tq, Dh), F32), pltpu.VMEM((tq, 1), F32)],
        compiler_params=_cparams(("parallel", "parallel", "arbitrary")),
        name="fox_bwd_dq",
    )(q, k, v, c_col, c_row, o, lse, do)


def fox_bwd_dkv(q, k, v, c_col, c_row, o, lse, do):
    Hh, S, Dh = q.shape
    tq = tk = min(512, S)
    nq, nk = S // tq, S // tk

    def body(q_ref, k_ref, v_ref, cq_ref, ck_ref, o_ref, lse_ref, do_ref, dk_ref, dv_ref, dc_ref, dk_s, dv_s, dc_s):
        ki, qi = pl.program_id(1), pl.program_id(2)

        @pl.when(qi == 0)
        def _():
            dk_s[...] = jnp.zeros_like(dk_s)
            dv_s[...] = jnp.zeros_like(dv_s)
            dc_s[...] = jnp.zeros_like(dc_s)

        @pl.when(qi >= ki)
        def _():
            s = _fox_logits(q_ref[...], k_ref[...], cq_ref[...], ck_ref[...], qi, ki, tq, tk)
            p = jnp.exp(s - lse_ref[...])
            do_t = do_ref[...]
            delta = jnp.sum(do_t * o_ref[...], axis=-1, keepdims=True)
            dp = _dot(do_t, v_ref[...], _NT)
            ds = p * (dp - delta)
            dv_s[...] += _dot(p, do_t, _TN)
            dk_s[...] += _dot(ds, q_ref[...], _TN)
            dc_s[...] += jnp.sum(ds, axis=0, keepdims=True)

        @pl.when(qi == nq - 1)
        def _():
            dk_ref[...] = dk_s[...] * (HEAD_DIM ** -0.5)
            dv_ref[...] = dv_s[...]
            dc_ref[...] = dc_s[...]

    qs = pl.BlockSpec((None, tq, Dh), lambda h, j, i: (h, jnp.maximum(i, j), 0))
    ks = pl.BlockSpec((None, tk, Dh), lambda h, j, i: (h, j, 0))
    cqs = pl.BlockSpec((None, tq, 1), lambda h, j, i: (h, jnp.maximum(i, j), 0))
    cks = pl.BlockSpec((None, 1, tk), lambda h, j, i: (h, 0, j))
    return pl.pallas_call(
        body,
        grid=(Hh, nk, nq),
        in_specs=[qs, ks, ks, cqs, cks, qs, cqs, qs],
        out_specs=[ks, ks, cks],
        out_shape=[jax.ShapeDtypeStruct((Hh, S, Dh), F32), jax.ShapeDtypeStruct((Hh, S, Dh), F32),
                   jax.ShapeDtypeStruct((Hh, 1, S), F32)],
        scratch_shapes=[pltpu.VMEM((tk, Dh), F32), pltpu.VMEM((tk, Dh), F32), pltpu.VMEM((1, tk), F32)],
        compiler_params=_cparams(("parallel", "parallel", "arbitrary")),
        name="fox_bwd_dkv",
    )(q, k, v, c_col, c_row, o, lse, do)


def _heads(a, nh):
    S = a.shape[0]
    return a.reshape(S, nh, HEAD_DIM).transpose(1, 0, 2)


def _unheads(a):
    nh, S, _ = a.shape
    return a.transpose(1, 0, 2).reshape(S, nh * HEAD_DIM)


def _shift_down(a):
    return jnp.pad(a[:-1], ((1, 0), (0, 0)))


def _shift_up(a):
    return jnp.pad(a[1:], ((0, 1), (0, 0)))


def _block_diag_ones():
    i = np.arange(RWKV_DIM) // HEAD_DIM
    return jnp.asarray((i[:, None] == i[None, :]).astype(np.float32))


def ffn_fwd(x, g_norm, w_gu, w_down, tag):
    hn, = rowwise(_f_rms, [x], [g_norm], [(D_MODEL, BF16)], tag + "_rms")
    gu = matmul(hn, w_gu, "nn", tag + "_gu")
    act, = rowwise(_f_act, [(gu, D_FF, 0), (gu, D_FF, 1)], [], [(D_FF, BF16)], tag + "_act", tm=128)
    out = matmul(act, w_down, "nn", tag + "_down", scale=0.5, res=x)
    return out, (x, hn, gu, act)


def ffn_bwd(dy, saved, g_norm, w_gu, w_down, tag):
    x, hn, gu, act = saved
    dact = matmul(dy, w_down, "nt", tag + "_dact", scale=0.5)
    d_wdown = matmul(act, dy, "tn", tag + "_dwd", out_dtype=BF16, scale=0.5)
    (dg, du), _ = rowwise_vjp(_f_act, [(gu, D_FF, 0), (gu, D_FF, 1)], [], [dact], tag + "_dactf",
                              row_dtype=BF16, tm=128)
    dgu = jnp.concatenate([dg, du], axis=1)
    d_wgu = matmul(hn, dgu, "tn", tag + "_dwgu", out_dtype=BF16)
    dhn = matmul(dgu, w_gu, "nt", tag + "_dhn")
    (dx,), (dgn,) = rowwise_vjp(_f_rms, [x], [g_norm], [dhn], tag + "_drms")
    return dy + dx, dgn, d_wgu, d_wdown


def ple_fwd(x, p_i, g_norm, w_gate, w_proj, tag):
    hn, = rowwise(_f_rms, [x], [g_norm], [(D_MODEL, BF16)], tag + "_rms")
    z = matmul(hn, w_gate, "nn", tag + "_gate")
    pp = matmul(p_i, w_proj, "nn", tag + "_proj")
    out, = rowwise(_f_ple, [x, z, pp], [], [(D_MODEL, F32)], tag + "_mix")
    return out, (x, hn, z, pp)


def ple_bwd(dy, saved, p_i, g_norm, w_gate, tag):
    x, hn, z, pp = saved
    (dz, dpp), _ = rowwise_vjp(_f_ple, [x, z, pp], [], [dy], tag + "_dmix", need=[False, True, True],
                               row_dtype=BF16)
    d_wproj = matmul(p_i, dpp, "tn", tag + "_dwp", out_dtype=BF16)
    d_wgate = matmul(hn, dz, "tn", tag + "_dwg", out_dtype=BF16)
    dhn = matmul(dz, w_gate, "nt", tag + "_dhn")
    (dx,), (dgn,) = rowwise_vjp(_f_rms, [x], [g_norm], [dhn], tag + "_drms")
    return dy + dx, dgn, d_wgate, d_wproj


def _swa_consts(sinks):
    slopes = np.asarray([2.0 ** (-(i + 1)) for i in range(SWA_HEADS)], np.float32)
    slope_col = jnp.asarray(np.repeat(slopes, BLOCK).reshape(SWA_KV_HEADS, SWA_GROUP * BLOCK, 1))
    sink_col = jnp.repeat(sinks.reshape(SWA_HEADS), BLOCK).reshape(SWA_KV_HEADS, SWA_GROUP * BLOCK, 1)
    return sink_col, slope_col


def even_mix_fwd(x, W):
    S = x.shape[0]
    hn, = rowwise(_f_rms, [x], [W["mix_norm0"]], [(D_MODEL, BF16)], "emix_rms")
    proj = matmul(hn, W["even_w_in"], "nn", "emix_in")
    qa = _heads(proj[:, :SWA_Q], SWA_HEADS).reshape(SWA_KV_HEADS, SWA_GROUP, S, HEAD_DIM)
    ka = _heads(proj[:, SWA_Q:SWA_Q + SWA_KV], SWA_KV_HEADS)
    va = _heads(proj[:, SWA_Q + SWA_KV:SWA_COLS], SWA_KV_HEADS)
    sink_col, slope_col = _swa_consts(W["swa_sinks"])
    ya = swa_fwd(qa, ka, va, sink_col, slope_col)
    ya = _unheads(ya.reshape(SWA_HEADS, S, HEAD_DIM))
    hb = proj[:, SWA_COLS:]
    h, = rowwise(_f_mix, [hb, _shift_down(hb)], [W["rwkv_mu"]], [(hb.shape[1], F32)], "rwkv_shift")
    hr, hk, hv = h[:, :512], h[:, 512:1024], h[:, 1024:1536]
    hw, ha, hg = h[:, 1536:1600], h[:, 1600:1664], h[:, 1664:1792]
    bd = _block_diag_ones()
    pre_params = [W["rwkv_w0"], W["rwkv_w2"], W["rwkv_a0"], W["rwkv_a2"], W["rwkv_g2"], W["rwkv_k_k"],
                  W["rwkv_k_a"]]
    decay, k2, kk, b, g = rowwise(_f_rwkv_pre, [hk, hw, ha, hg], pre_params + [bd],
                                  [(RWKV_DIM, F32)] * 5, "rwkv_pre")
    vT = _to_cols(hv)
    yT, ckpt = rwkv_scan_fwd(hr, decay, k2, kk, b, vT)
    y = _from_cols(yT)
    post_params = [W["rwkv_ln_w"], W["rwkv_ln_b"], W["rwkv_r_k"]]
    yb, = rowwise(_f_rwkv_post, [y, hr, k2, hv, g], post_params + [bd], [(RWKV_DIM, F32)], "rwkv_post")
    cat = jnp.concatenate([ya, yb], axis=1).astype(BF16)
    out = matmul(cat, W["even_w_out"], "nn", "emix_out", res=x)
    saved = (x, hn, qa, ka, va, sink_col, slope_col, hb, hr, hk, hv, hw, ha, hg, decay, k2, kk, b, g, vT,
             ckpt, y, cat)
    return out, saved


def even_mix_bwd(dy, saved, W):
    (x, hn, qa, ka, va, sink_col, slope_col, hb, hr, hk, hv, hw, ha, hg, decay, k2, kk, b, g, vT, ckpt, y,
     cat) = saved
    S = x.shape[0]
    grads = {}
    dcat = matmul(dy, W["even_w_out"], "nt", "emix_dcat")
    grads["even_w_out"] = matmul(cat, dy, "tn", "emix_dwout", out_dtype=BF16)
    dya, dyb = dcat[:, :SWA_Q], dcat[:, SWA_Q:]
    dya_h = _heads(dya, SWA_HEADS).reshape(SWA_KV_HEADS, SWA_GROUP, S, HEAD_DIM)
    dqa, dkp, dkc, dvp, dvc, dsink = swa_bwd(qa, ka, va, sink_col, slope_col, dya_h)
    shift_blk = lambda a: jnp.pad(a[:, BLOCK:], ((0, 0), (0, BLOCK), (0, 0)))
    dka = dkc + shift_blk(dkp)
    dva = dvc + shift_blk(dvp)
    grads["swa_sinks"] = dsink.reshape(SWA_HEADS, BLOCK).sum(axis=1).reshape(1, SWA_HEADS)
    dqa = _unheads(dqa.reshape(SWA_HEADS, S, HEAD_DIM))
    dka, dva = _unheads(dka), _unheads(dva)
    bd = _block_diag_ones()
    post_params = [W["rwkv_ln_w"], W["rwkv_ln_b"], W["rwkv_r_k"]]
    (d_y, d_r1, d_k2a, d_v1, d_g), (d_lnw, d_lnb, d_rk) = rowwise_vjp(
        _f_rwkv_post, [y, hr, k2, hv, g], post_params, [dyb], "rwkv_dpost", consts=[bd], tm=128)
    grads["rwkv_ln_w"], grads["rwkv_ln_b"], grads["rwkv_r_k"] = d_lnw, d_lnb, d_rk
    d_r2, d_w, d_k2b, d_kk, d_b, d_vT = rwkv_scan_bwd(hr, decay, k2, kk, b, vT, _to_cols(d_y), ckpt)
    pre_params = [W["rwkv_w0"], W["rwkv_w2"], W["rwkv_a0"], W["rwkv_a2"], W["rwkv_g2"], W["rwkv_k_k"],
                  W["rwkv_k_a"]]
    (d_hk, d_hw, d_ha, d_hg), dpre = rowwise_vjp(
        _f_rwkv_pre, [hk, hw, ha, hg], pre_params, [d_w, d_k2a + d_k2b, d_kk, d_b, d_g], "rwkv_dpre",
        consts=[bd], tm=128)
    for nm, gval in zip(["rwkv_w0", "rwkv_w2", "rwkv_a0", "rwkv_a2", "rwkv_g2", "rwkv_k_k", "rwkv_k_a"], dpre):
        grads[nm] = gval
    d_h = jnp.concatenate([d_r1 + d_r2, d_hk, d_v1 + _from_cols(d_vT), d_hw, d_ha, d_hg], axis=1)
    (d_hb, d_sh), (d_mu,) = rowwise_vjp(_f_mix, [hb, _shift_down(hb)], [W["rwkv_mu"]], [d_h], "rwkv_dshift")
    grads["rwkv_mu"] = d_mu
    d_hb = d_hb + _shift_up(d_sh)
    dproj = jnp.concatenate([dqa, dka, dva, d_hb], axis=1).astype(BF16)
    grads["even_w_in"] = matmul(hn, dproj, "tn", "emix_dwin", out_dtype=BF16)
    dhn = matmul(dproj, W["even_w_in"], "nt", "emix_dhn")
    (dx,), (dgn,) = rowwise_vjp(_f_rms, [x], [W["mix_norm0"]], [dhn], "emix_drms")
    grads["mix_norm0"] = dgn
    return dy + dx, grads


def odd_mix_fwd(x, W):
    S = x.shape[0]
    hn, = rowwise(_f_rms, [x], [W["mix_norm1"]], [(D_MODEL, BF16)], "omix_rms")
    proj = matmul(hn, W["fox_w_in"], "nn", "omix_in")
    q = _heads(proj[:, :FOX_DIM], FOX_HEADS).astype(BF16)
    k = _heads(proj[:, FOX_DIM:2 * FOX_DIM], FOX_HEADS).astype(BF16)
    v = _heads(proj[:, 2 * FOX_DIM:3 * FOX_DIM], FOX_HEADS).astype(BF16)
    fz = proj[:, 3 * FOX_DIM:]
    logf, = rowwise(_f_logf, [fz], [W["fox_b_f"]], [(128, F32)], "fox_logf")
    c = seq_cumsum(logf, False, "fox_cumsum")[:, :FOX_HEADS]
    c_col = c.T.reshape(FOX_HEADS, S, 1)
    c_row = c.T.reshape(FOX_HEADS, 1, S)
    o, lse = fox_fwd(q, k, v, c_col, c_row)
    yc = _unheads(o).astype(BF16)
    out = matmul(yc, W["fox_w_out"], "nn", "omix_out", res=x)
    return out, (x, hn, q, k, v, fz, c_col, c_row, o, lse, yc)


def odd_mix_bwd(dy, saved, W):
    x, hn, q, k, v, fz, c_col, c_row, o, lse, yc = saved
    S = x.shape[0]
    grads = {}
    dyc = matmul(dy, W["fox_w_out"], "nt", "omix_dyc")
    grads["fox_w_out"] = matmul(yc, dy, "tn", "omix_dwout", out_dtype=BF16)
    do = _heads(dyc, FOX_HEADS)
    dq, drow = fox_bwd_dq(q, k, v, c_col, c_row, o, lse, do)
    dk, dv, dcol = fox_bwd_dkv(q, k, v, c_col, c_row, o, lse, do)
    dc = (drow.reshape(FOX_HEADS, S) - dcol.reshape(FOX_HEADS, S)).T
    dc = jnp.pad(dc, ((0, 0), (0, 128 - FOX_HEADS)))
    dlogf = seq_cumsum(dc, True, "fox_rcumsum")
    (dfz,), (dbf,) = rowwise_vjp(_f_logf, [fz], [W["fox_b_f"]], [dlogf], "fox_dlogf")
    grads["fox_b_f"] = dbf
    dproj = jnp.concatenate([_unheads(dq), _unheads(dk), _unheads(dv), dfz], axis=1).astype(BF16)
    grads["fox_w_in"] = matmul(hn, dproj, "tn", "omix_dwin", out_dtype=BF16)
    dhn = matmul(dproj, W["fox_w_in"], "nt", "omix_dhn")
    (dx,), (dgn,) = rowwise_vjp(_f_rms, [x], [W["mix_norm1"]], [dhn], "omix_drms")
    grads["mix_norm1"] = dgn
    return dy + dx, grads


def device_step(x, p, target, W):
    saved = []
    h = x
    for i in range(2):
        h, s1 = ffn_fwd(h, W[f"ffn1_norm{i}"], W[f"ffn1_w_gu{i}"], W[f"ffn1_w_down{i}"], f"ffn1_{i}")
        h, s2 = even_mix_fwd(h, W) if i == 0 else odd_mix_fwd(h, W)
        h, s3 = ffn_fwd(h, W[f"ffn2_norm{i}"], W[f"ffn2_w_gu{i}"], W[f"ffn2_w_down{i}"], f"ffn2_{i}")
        h, s4 = ple_fwd(h, p[i], W[f"ple_norm{i}"], W[f"ple_w_gate{i}"], W[f"ple_w_proj{i}"], f"ple_{i}")
        saved.append((s1, s2, s3, s4))
    dh, d_final, loss = loss_head(h, target, W["final_norm"])
    G = {"final_norm": d_final}
    for i in (1, 0):
        s1, s2, s3, s4 = saved[i]
        dh, G[f"ple_norm{i}"], G[f"ple_w_gate{i}"], G[f"ple_w_proj{i}"] = ple_bwd(
            dh, s4, p[i], W[f"ple_norm{i}"], W[f"ple_w_gate{i}"], f"ple_{i}")
        dh, G[f"ffn2_norm{i}"], G[f"ffn2_w_gu{i}"], G[f"ffn2_w_down{i}"] = ffn_bwd(
            dh, s3, W[f"ffn2_norm{i}"], W[f"ffn2_w_gu{i}"], W[f"ffn2_w_down{i}"], f"ffn2_{i}")
        dh, gm = even_mix_bwd(dh, s2, W) if i == 0 else odd_mix_bwd(dh, s2, W)
        G.update(gm)
        dh, G[f"ffn1_norm{i}"], G[f"ffn1_w_gu{i}"], G[f"ffn1_w_down{i}"] = ffn_bwd(
            dh, s1, W[f"ffn1_norm{i}"], W[f"ffn1_w_gu{i}"], W[f"ffn1_w_down{i}"], f"ffn1_{i}")
    return loss, dh, G


_MESH = pl.DeviceIdType.MESH
_ANY = pl.BlockSpec(memory_space=pl.ANY)


def all_gather(x, name):
    def body(x_ref, out_ref, send_sems, recv_sems, local_sem):
        x_, y_, c_ = lax.axis_index("x"), lax.axis_index("y"), lax.axis_index("c")
        me, sibling = (x_, y_, c_), (x_, y_, 1 - c_)
        chips = [(1 - x_, y_), (x_, 1 - y_), (1 - x_, 1 - y_)]

        def slot(px, py, pc):
            return out_ref.at[4 * px + 2 * py + pc]

        def copy(k, block, to, src=None):
            return pltpu.make_async_remote_copy(
                src_ref=slot(*block) if src is None else src, dst_ref=slot(*block),
                send_sem=send_sems.at[k], recv_sem=recv_sems.at[k], device_id=to, device_id_type=_MESH)

        mine = pltpu.make_async_copy(x_ref, slot(*me), local_sem)
        mine.start()
        first = [copy(0, me, sibling, src=x_ref)]
        first += [copy(1 + j, me, (*chip, c_), src=x_ref) for j, chip in enumerate(chips)]
        for cp in first:
            cp.start()
        passed = [copy(4 + j, (*chip, c_), sibling) for j, chip in enumerate(chips)]
        for j, chip in enumerate(chips):
            copy(1 + j, (*chip, c_), me).wait_recv()
            passed[j].start()
        copy(0, sibling, me).wait_recv()
        for j, chip in enumerate(chips):
            copy(4 + j, (*chip, 1 - c_), me).wait_recv()
        for cp in first + passed:
            cp.wait_send()
        mine.wait()

    return pl.pallas_call(
        body,
        out_shape=jax.ShapeDtypeStruct((N_DEV,) + x.shape, x.dtype),
        in_specs=[_ANY],
        out_specs=_ANY,
        scratch_shapes=[pltpu.SemaphoreType.DMA((7,)), pltpu.SemaphoreType.DMA((7,)), pltpu.SemaphoreType.DMA(())],
        name=name,
    )(x)


def all_to_all(send, name):
    def body(s_ref, r_ref, send_sems, recv_sems, local_sem):
        x_, y_, c_ = lax.axis_index("x"), lax.axis_index("y"), lax.axis_index("c")
        my = 4 * x_ + 2 * y_ + c_
        local = pltpu.make_async_copy(s_ref.at[my], r_ref.at[my], local_sem)
        local.start()
        copies = []
        for m in range(1, N_DEV):
            px = 1 - x_ if (m >> 2) & 1 else x_
            py = 1 - y_ if (m >> 1) & 1 else y_
            pc = 1 - c_ if m & 1 else c_
            cp = pltpu.make_async_remote_copy(
                src_ref=s_ref.at[4 * px + 2 * py + pc], dst_ref=r_ref.at[my],
                send_sem=send_sems.at[m - 1], recv_sem=recv_sems.at[m - 1],
                device_id=(px, py, pc), device_id_type=_MESH)
            cp.start()
            copies.append(cp)
        for cp in copies:
            cp.wait()
        local.wait()

    return pl.pallas_call(
        body,
        out_shape=jax.ShapeDtypeStruct(send.shape, send.dtype),
        in_specs=[_ANY],
        out_specs=_ANY,
        scratch_shapes=[pltpu.SemaphoreType.DMA((7,)), pltpu.SemaphoreType.DMA((7,)), pltpu.SemaphoreType.DMA(())],
        name=name,
    )(send)


def adamw(w, m, v, parts, name, tm=256):
    R, C = w.shape
    tm = _pick(R, tm, 8) if R >= 8 else R

    def body(w_ref, m_ref, v_ref, p_ref, g_ref, d_ref, nm_ref, nv_ref):
        g = p_ref[0].astype(F32)
        for s in range(1, N_DEV):
            g = g + p_ref[s].astype(F32)
        nm = ADAM_B1 * m_ref[...] + (1.0 - ADAM_B1) * g
        nv = ADAM_B2 * v_ref[...] + (1.0 - ADAM_B2) * (g * g)
        m_hat = nm / (1.0 - ADAM_B1 ** ADAM_STEP)
        v_hat = nv / (1.0 - ADAM_B2 ** ADAM_STEP)
        g_ref[...] = g
        d_ref[...] = -ADAM_LR * (m_hat / (jnp.sqrt(v_hat) + ADAM_EPS) + ADAM_WD * w_ref[...])
        nm_ref[...] = nm
        nv_ref[...] = nv

    row = pl.BlockSpec((tm, C), lambda i: (i, 0))
    out = jax.ShapeDtypeStruct((R, C), F32)
    return pl.pallas_call(
        body,
        grid=(R // tm,),
        in_specs=[row, row, row, pl.BlockSpec((N_DEV, tm, C), lambda i: (0, i, 0))],
        out_specs=[row] * 4,
        out_shape=[out] * 4,
        compiler_params=_cparams(("parallel",)),
        name=name,
    )(w, m, v, parts)


_WEIGHTS = ["ffn1_norm", "ffn1_w_gu", "ffn1_w_down", "mix_norm", "ffn2_norm", "ffn2_w_gu", "ffn2_w_down",
            "ple_norm", "ple_w_gate", "ple_w_proj", "even_w_in", "even_w_out", "swa_sinks", "rwkv_mu",
            "rwkv_w0", "rwkv_w2", "rwkv_a0", "rwkv_a2", "rwkv_g2", "rwkv_k_k", "rwkv_k_a", "rwkv_r_k",
            "rwkv_ln_w", "rwkv_ln_b", "fox_w_in", "fox_b_f", "fox_w_out", "final_norm"]
_SHARD_AXIS = {"ffn1_w_gu": 2, "ffn1_w_down": 1, "ffn2_w_gu": 2, "ffn2_w_down": 1, "ple_w_gate": 1,
               "ple_w_proj": 2, "even_w_in": 2, "even_w_out": 1, "rwkv_w2": 2, "rwkv_a2": 2, "rwkv_g2": 2,
               "fox_w_in": 2, "fox_w_out": 1}
_SHARDED = [n for n in _WEIGHTS if n in _SHARD_AXIS]
_REPLICATED = [n for n in _WEIGHTS if n not in _SHARD_AXIS]
_PACK_LANES = 1024
_PACK_ROW_TILE = 256


def _pack_rows(arrs):
    flat = jnp.concatenate([a.reshape(-1, _PACK_LANES) for a in arrs], axis=0)
    return jnp.pad(flat, ((0, -flat.shape[0] % _PACK_ROW_TILE), (0, 0)))


def _unpack_rows(flat, shapes):
    out, r0 = [], 0
    for shp in shapes:
        n = math.prod(shp) // _PACK_LANES
        out.append(flat[r0:r0 + n].reshape(shp))
        r0 += n
    return out


def _unshard(gathered, shard_shapes):
    full, r0 = {}, 0
    for name, shp in shard_shapes.items():
        n = math.prod(shp) // _PACK_LANES
        seg = gathered[:, r0:r0 + n].reshape((N_DEV,) + shp)
        ax = _SHARD_AXIS[name]
        seg = jnp.moveaxis(seg, 0, ax)
        full[name] = seg.reshape(shp[:ax] + (N_DEV * shp[ax],) + shp[ax + 1:])
        r0 += n
    return full


def _to_shards(full, shard_shapes):
    segs = []
    for name, shp in shard_shapes.items():
        ax = _SHARD_AXIS[name]
        a = full[name].reshape(shp[:ax] + (N_DEV, shp[ax]) + shp[ax + 1:])
        segs.append(jnp.moveaxis(a, ax, 0).reshape(N_DEV, -1, _PACK_LANES))
    flat = jnp.concatenate(segs, axis=1)
    return jnp.pad(flat, ((0, 0), (0, -flat.shape[1] % _PACK_ROW_TILE), (0, 0)))


def _pack_small(vals):
    flat = jnp.concatenate([v.reshape(1, -1) for v in vals], axis=1)
    n = flat.shape[1]
    return jnp.pad(flat, ((0, 0), (0, -n % 128)))


def _unpack_small(flat, shapes):
    out, c0 = [], 0
    for shp in shapes:
        n = math.prod(shp)
        out.append(flat[0, c0:c0 + n].reshape(shp))
        c0 += n
    return out


def kernel(x, p, ffn1_norm, ffn1_w_gu, ffn1_w_down, mix_norm, ffn2_norm, ffn2_w_gu, ffn2_w_down, ple_norm, ple_w_gate, ple_w_proj, even_w_in, even_w_out, swa_sinks, rwkv_mu, rwkv_w0, rwkv_w2, rwkv_a0, rwkv_a2, rwkv_g2, rwkv_k_k, rwkv_k_a, rwkv_r_k, rwkv_ln_w, rwkv_ln_b, fox_w_in, fox_b_f, fox_w_out, final_norm, loss_target, m_ffn1_norm, m_ffn1_w_gu, m_ffn1_w_down, m_mix_norm, m_ffn2_norm, m_ffn2_w_gu, m_ffn2_w_down, m_ple_norm, m_ple_w_gate, m_ple_w_proj, m_even_w_in, m_even_w_out, m_swa_sinks, m_rwkv_mu, m_rwkv_w0, m_rwkv_w2, m_rwkv_a0, m_rwkv_a2, m_rwkv_g2, m_rwkv_k_k, m_rwkv_k_a, m_rwkv_r_k, m_rwkv_ln_w, m_rwkv_ln_b, m_fox_w_in, m_fox_b_f, m_fox_w_out, m_final_norm, v_ffn1_norm, v_ffn1_w_gu, v_ffn1_w_down, v_mix_norm, v_ffn2_norm, v_ffn2_w_gu, v_ffn2_w_down, v_ple_norm, v_ple_w_gate, v_ple_w_proj, v_even_w_in, v_even_w_out, v_swa_sinks, v_rwkv_mu, v_rwkv_w0, v_rwkv_w2, v_rwkv_a0, v_rwkv_a2, v_rwkv_g2, v_rwkv_k_k, v_rwkv_k_a, v_rwkv_r_k, v_rwkv_ln_w, v_rwkv_ln_b, v_fox_w_in, v_fox_b_f, v_fox_w_out, v_final_norm):
    given = dict(locals())
    w = {n: given[n] for n in _WEIGHTS}
    m = {n: given["m_" + n] for n in _WEIGHTS}
    v = {n: given["v_" + n] for n in _WEIGHTS}
    shard_shapes = {n: w[n].shape for n in _SHARDED}
    small_shapes = [w[n].shape for n in _REPLICATED]

    w_rows = _pack_rows([w[n] for n in _SHARDED])
    gathered = all_gather(w_rows.astype(BF16), "weights_all_gather")
    full = _unshard(gathered, shard_shapes)

    W = {}
    for i in range(2):
        for n in ("ffn1_w_gu", "ffn1_w_down", "ffn2_w_gu", "ffn2_w_down", "ple_w_gate", "ple_w_proj"):
            W[f"{n}{i}"] = full[n][i]
        for n in ("ffn1_norm", "mix_norm", "ffn2_norm", "ple_norm"):
            W[f"{n}{i}"] = w[n][i].reshape(1, -1)
    for n in ("even_w_in", "even_w_out", "fox_w_out"):
        W[n] = full[n][0]
    W["fox_w_in"] = jnp.pad(full["fox_w_in"][0], ((0, 0), (0, FOX_IN_PAD - full["fox_w_in"].shape[2])))
    for n in ("rwkv_w2", "rwkv_a2", "rwkv_g2"):
        W[n] = full[n][0].astype(F32)
    for n in ("swa_sinks", "rwkv_mu", "rwkv_w0", "rwkv_a0", "rwkv_k_k", "rwkv_k_a", "rwkv_r_k", "rwkv_ln_w",
              "rwkv_ln_b", "final_norm"):
        W[n] = w[n].reshape(1, -1)
    n_f = fox_b_f.shape[1]
    W["fox_b_f"] = jnp.pad(fox_b_f.reshape(1, n_f), ((0, 0), (0, 128 - n_f)))

    loss_row, dx, G = device_step(x[0], p[:, 0], loss_target[0], W)

    gfull = {}
    for n in ("ffn1_w_gu", "ffn1_w_down", "ffn2_w_gu", "ffn2_w_down", "ple_w_gate", "ple_w_proj"):
        gfull[n] = jnp.stack([G[f"{n}0"], G[f"{n}1"]]).astype(BF16)
    for n in ("even_w_in", "even_w_out", "fox_w_out", "rwkv_w2", "rwkv_a2", "rwkv_g2"):
        gfull[n] = G[n][None].astype(BF16)
    gfull["fox_w_in"] = G["fox_w_in"][None, :, :fox_w_in.shape[2] * N_DEV].astype(BF16)
    parts = all_to_all(_to_shards(gfull, shard_shapes), "grads_all_to_all")
    g_rows, d_rows, nm_rows, nv_rows = adamw(
        w_rows, _pack_rows([m[n] for n in _SHARDED]), _pack_rows([v[n] for n in _SHARDED]), parts, "adamw_sharded")
    shapes = list(shard_shapes.values())
    out_g = dict(zip(_SHARDED, _unpack_rows(g_rows, shapes)))
    out_d = dict(zip(_SHARDED, _unpack_rows(d_rows, shapes)))
    out_m = dict(zip(_SHARDED, _unpack_rows(nm_rows, shapes)))
    out_v = dict(zip(_SHARDED, _unpack_rows(nv_rows, shapes)))

    gsmall = {}
    for n in ("ffn1_norm", "mix_norm", "ffn2_norm", "ple_norm"):
        gsmall[n] = jnp.concatenate([G[f"{n}0"], G[f"{n}1"]], axis=0)
    for n in ("swa_sinks", "rwkv_mu", "rwkv_w0", "rwkv_a0", "rwkv_k_k", "rwkv_k_a", "rwkv_r_k", "rwkv_ln_w",
              "rwkv_ln_b", "final_norm"):
        gsmall[n] = G[n]
    gsmall["fox_b_f"] = G["fox_b_f"][:, :n_f]
    small = _pack_small([gsmall[n] for n in _REPLICATED] + [loss_row[:, :1]])
    small_parts = all_gather(small, "small_all_gather")
    pad1 = lambda vals: _pack_small(vals + [jnp.zeros((1, 1), F32)])
    gs, ds, nms, nvs = adamw(pad1([w[n] for n in _REPLICATED]), pad1([m[n] for n in _REPLICATED]),
                             pad1([v[n] for n in _REPLICATED]), small_parts, "adamw_replicated")
    out_g.update(zip(_REPLICATED, _unpack_small(gs, small_shapes)))
    out_d.update(zip(_REPLICATED, _unpack_small(ds, small_shapes)))
    out_m.update(zip(_REPLICATED, _unpack_small(nms, small_shapes)))
    out_v.update(zip(_REPLICATED, _unpack_small(nvs, small_shapes)))
    n_small = sum(math.prod(s) for s in small_shapes)
    loss = gs[0, n_small]

    return (loss, dx[None], *[out_g[n] for n in _WEIGHTS], *[out_d[n] for n in _WEIGHTS],
            *[out_m[n] for n in _WEIGHTS], *[out_v[n] for n in _WEIGHTS])
```

```python
import functools
import math

import numpy as np
import jax
import jax.numpy as jnp
from jax import lax
from jax.experimental import pallas as pl
from jax.experimental.pallas import tpu as pltpu

F32 = jnp.float32
BF16 = jnp.bfloat16

D_MODEL = 1024
HEAD_DIM = 64
BLOCK = 128
SWA_HEADS = 8
SWA_KV_HEADS = 2
SWA_GROUP = 4
RWKV_HEADS = 8
RWKV_DIM = 512
FOX_HEADS = 16
FOX_DIM = 1024
D_FF = 2816
NORM_EPS = 1e-6
GN_EPS = 64e-5
L2_EPS = 1e-12
SWA_Q = 512
SWA_KV = 128
SWA_COLS = 768
FOX_IN_PAD = 3200
N_DEV = 8
ADAM_LR = 0.001
ADAM_B1 = 0.9
ADAM_B2 = 0.999
ADAM_EPS = 1e-08
ADAM_WD = 0.01
ADAM_STEP = 10

V7X_VMEM_LIMIT = 56 * 1024 * 1024
SCAN_GROUP = 8
SCAN_CHUNK = 32

_NN = (((1,), (0,)), ((), ()))
_NT = (((1,), (1,)), ((), ()))
_TN = (((0,), (0,)), ((), ()))
_DIMS = {"nn": _NN, "nt": _NT, "tn": _TN}


def _pick(n, target, mult=128):
    best = None
    for t in range(mult, min(n, target) + 1, mult):
        if n % t == 0:
            best = t
    return best or n


def _cparams(sem):
    return pltpu.CompilerParams(dimension_semantics=sem, vmem_limit_bytes=V7X_VMEM_LIMIT)


def _dot(a, b, dims):
    return lax.dot_general(a.astype(BF16), b.astype(BF16), dims, preferred_element_type=F32)


@jax.custom_vjp
def bdot(a, b):
    return _dot(a, b, _NN)


def _bdot_fwd(a, b):
    return _dot(a, b, _NN), (a, b)


def _bdot_bwd(res, g):
    a, b = res
    return _dot(g, b, _NT), _dot(a, g, _TN)


bdot.defvjp(_bdot_fwd, _bdot_bwd)


@jax.custom_vjp
def bdot_nt(a, b):
    return _dot(a, b, _NT)


def _bdot_nt_fwd(a, b):
    return _dot(a, b, _NT), (a, b)


def _bdot_nt_bwd(res, g):
    a, b = res
    return _dot(g, b, _NN), _dot(g, a, _TN)


bdot_nt.defvjp(_bdot_nt_fwd, _bdot_nt_bwd)


def _segsum(x, bd):
    return jnp.dot(x, bd, precision=lax.Precision.HIGHEST, preferred_element_type=F32)


def _sigmoid(x):
    return 1.0 / (1.0 + jnp.exp(-x))


def _softplus(x):
    return jnp.maximum(x, 0.0) + jnp.log(1.0 + jnp.exp(-jnp.abs(x)))


def matmul(a, b, mode, name, out_dtype=F32, scale=1.0, res=None, tm=512, tn=1408, tk=1024):
    if mode == "nn":
        (M, K), (K2, N) = a.shape, b.shape
    elif mode == "nt":
        (M, K), (N, K2) = a.shape, b.shape
    else:
        (K, M), (K2, N) = a.shape, b.shape
    assert K == K2, (a.shape, b.shape, mode)
    tm, tn, tk = _pick(M, tm), _pick(N, tn), _pick(K, tk)
    nk = K // tk
    has_res = res is not None

    def body(*refs):
        if has_res:
            a_ref, b_ref, r_ref, o_ref, acc = refs
        else:
            a_ref, b_ref, o_ref, acc = refs
        kk = pl.program_id(2)

        @pl.when(kk == 0)
        def _():
            acc[...] = jnp.zeros_like(acc)

        acc[...] += _dot(a_ref[...], b_ref[...], _DIMS[mode])

        @pl.when(kk == nk - 1)
        def _():
            v = acc[...]
            if scale != 1.0:
                v = v * scale
            if has_res:
                v = v + r_ref[...].astype(F32)
            o_ref[...] = v.astype(out_dtype)

    if mode == "tn":
        a_spec = pl.BlockSpec((tk, tm), lambda i, j, k: (k, i))
    else:
        a_spec = pl.BlockSpec((tm, tk), lambda i, j, k: (i, k))
    if mode == "nt":
        b_spec = pl.BlockSpec((tn, tk), lambda i, j, k: (j, k))
    else:
        b_spec = pl.BlockSpec((tk, tn), lambda i, j, k: (k, j))
    o_spec = pl.BlockSpec((tm, tn), lambda i, j, k: (i, j))
    in_specs = [a_spec, b_spec] + ([o_spec] if has_res else [])
    args = (a, b) + ((res,) if has_res else ())
    return pl.pallas_call(
        body,
        grid=(M // tm, N // tn, nk),
        in_specs=in_specs,
        out_specs=o_spec,
        out_shape=jax.ShapeDtypeStruct((M, N), out_dtype),
        scratch_shapes=[pltpu.VMEM((tm, tn), F32)],
        compiler_params=_cparams(("parallel", "parallel", "arbitrary")),
        name=name,
    )(*args)


def _row_spec(r, tm):
    if isinstance(r, tuple):
        arr, width, blk = r
        return arr, pl.BlockSpec((tm, width), lambda i, blk=blk: (i, blk))
    return r, pl.BlockSpec((tm, r.shape[1]), lambda i: (i, 0))


def _whole_spec(p):
    return pl.BlockSpec(p.shape, lambda i: (0,) * p.ndim)


def rowwise(fn, rows, params, outs, name, tm=256):
    arrs, specs = zip(*[_row_spec(r, tm) for r in rows])
    S = arrs[0].shape[0]
    tm = min(tm, S)
    arrs, specs = zip(*[_row_spec(r, tm) for r in rows])
    n_in = len(rows) + len(params)

    def body(*refs):
        res = fn(*[r[...] for r in refs[:n_in]])
        for o_ref, v in zip(refs[n_in:], res):
            o_ref[...] = v.astype(o_ref.dtype)

    return pl.pallas_call(
        body,
        grid=(S // tm,),
        in_specs=list(specs) + [_whole_spec(p) for p in params],
        out_specs=[pl.BlockSpec((tm, c), lambda i: (i, 0)) for c, _ in outs],
        out_shape=[jax.ShapeDtypeStruct((S, c), dt) for c, dt in outs],
        compiler_params=_cparams(("parallel",)),
        name=name,
    )(*arrs, *params)


def rowwise_vjp(fn, rows, params, cots, name, need=None, row_dtype=F32, consts=(), tm=256):
    nr, npar, nc, nk = len(rows), len(params), len(cots), len(consts)
    need = [True] * nr if need is None else need
    arrs, _ = zip(*[_row_spec(r, tm) for r in rows])
    S = arrs[0].shape[0]
    tm = min(tm, S)
    arrs, specs = zip(*[_row_spec(r, tm) for r in rows])
    carrs, cspecs = zip(*[_row_spec(c, tm) for c in cots])
    widths = [s.block_shape[1] for s in specs]
    n_in = nr + npar + nk + nc

    def body(*refs):
        i = pl.program_id(0)
        xs = [r[...].astype(F32) for r in refs[:nr]]
        ps = [r[...] for r in refs[nr:nr + npar]]
        ks = [r[...] for r in refs[nr + npar:nr + npar + nk]]
        cs = [r[...].astype(F32) for r in refs[nr + npar + nk:n_in]]
        outs, vjp = jax.vjp(lambda *a: fn(*a, *ks), *xs, *ps)
        grads = vjp(tuple(cs))
        o = n_in
        for j in range(nr):
            if need[j]:
                refs[o][...] = grads[j].astype(refs[o].dtype)
                o += 1
        for j in range(npar):
            g_ref = refs[o + j]

            @pl.when(i == 0)
            def _(g_ref=g_ref):
                g_ref[...] = jnp.zeros_like(g_ref)

            g_ref[...] += grads[nr + j]

    out_specs = [pl.BlockSpec((tm, w), lambda i: (i, 0)) for w, nd in zip(widths, need) if nd]
    out_shape = [jax.ShapeDtypeStruct((S, w), row_dtype) for w, nd in zip(widths, need) if nd]
    out_specs += [_whole_spec(p) for p in params]
    out_shape += [jax.ShapeDtypeStruct(p.shape, F32) for p in params]
    res = pl.pallas_call(
        body,
        grid=(S // tm,),
        in_specs=list(specs) + [_whole_spec(p) for p in params] + [_whole_spec(k) for k in consts] + list(cspecs),
        out_specs=out_specs,
        out_shape=out_shape,
        compiler_params=_cparams(("arbitrary",)),
        name=name,
    )(*arrs, *params, *consts, *carrs)
    nrow = sum(need)
    return list(res[:nrow]), list(res[nrow:])


def _rms(x, g):
    return x * lax.rsqrt(jnp.mean(x * x, axis=-1, keepdims=True) + NORM_EPS) * g


def _f_rms(x, g):
    return (_rms(x, g),)


def _f_act(g, u):
    return (g * _sigmoid(g) * u,)


def _f_ple(x, z, pp):
    return (x + _sigmoid(z) * pp,)


def _f_mix(h, sh, mu):
    return (h + (sh - h) * mu,)


def _f_logf(fz, bf):
    return (-_softplus(-(fz + bf)),)


def _f_rwkv_pre(hk, hw, ha, hg, w0, w2, a0, a2, g2, k_k, k_a, bd):
    wlog = -_softplus(-(w0 + bdot(jnp.tanh(hw), w2))) - 0.5
    a = _sigmoid(a0 + bdot(ha, a2))
    g = bdot(_sigmoid(hg), g2)
    kk = hk * k_k
    kk = kk / jnp.maximum(jnp.sqrt(_segsum(kk * kk, bd)), L2_EPS)
    k2 = hk * (1.0 + (a - 1.0) * k_a)
    decay = jnp.exp(-jnp.exp(wlog))
    return decay, k2, kk, kk * a, g


def _f_rwkv_post(y, r, k2, v, g, ln_w, ln_b, r_k, bd):
    mean = _segsum(y, bd) * (1.0 / HEAD_DIM)
    d = y - mean
    var = _segsum(d * d, bd) * (1.0 / HEAD_DIM)
    yn = d * lax.rsqrt(var + GN_EPS) * ln_w + ln_b
    yn = yn + _segsum(r * k2 * r_k, bd) * v
    return (yn * g,)


def loss_head(x, target, gf, tm=256):
    S, D = x.shape
    tm = min(tm, S)

    def f(xt, g, tt):
        err = _rms(xt, g) - tt
        return 0.5 * jnp.sum(err * err) * (1.0 / D)

    def body(x_ref, t_ref, g_ref, dx_ref, dg_ref, l_ref):
        i = pl.program_id(0)
        val, (dx, dg) = jax.value_and_grad(f, argnums=(0, 1))(x_ref[...], g_ref[...], t_ref[...])

        @pl.when(i == 0)
        def _():
            dg_ref[...] = jnp.zeros_like(dg_ref)
            l_ref[...] = jnp.zeros_like(l_ref)

        dx_ref[...] = dx
        dg_ref[...] += dg
        l_ref[...] += jnp.full(l_ref.shape, val, F32)

    row = pl.BlockSpec((tm, D), lambda i: (i, 0))
    vec = pl.BlockSpec((1, D), lambda i: (0, 0))
    return pl.pallas_call(
        body,
        grid=(S // tm,),
        in_specs=[row, row, vec],
        out_specs=[row, vec, pl.BlockSpec((1, 128), lambda i: (0, 0))],
        out_shape=[jax.ShapeDtypeStruct((S, D), F32), jax.ShapeDtypeStruct((1, D), F32),
                   jax.ShapeDtypeStruct((1, 128), F32)],
        compiler_params=_cparams(("arbitrary",)),
        name="loss_head",
    )(x, target, gf)


def _swa_block(q, kp, kc, vp, vc, sink, slope, n):
    k = jnp.concatenate([kp, kc], axis=0)
    v = jnp.concatenate([vp, vc], axis=0)
    rows = q.shape[0]
    logits = bdot_nt(q, k) * (HEAD_DIM ** -0.5)
    qi = lax.broadcasted_iota(jnp.int32, (rows, 2 * BLOCK), 0) & (BLOCK - 1)
    ki = lax.broadcasted_iota(jnp.int32, (rows, 2 * BLOCK), 1)
    dist = qi + BLOCK - ki
    valid = (dist >= 0) & (dist < BLOCK) & ((n - 1) * BLOCK + ki >= 0)
    logits = logits - slope * dist.astype(F32)
    logits = jnp.where(valid, logits, -jnp.inf)
    m = jnp.maximum(jnp.max(logits, axis=-1, keepdims=True), sink)
    pr = jnp.exp(logits - m)
    denom = jnp.sum(pr, axis=-1, keepdims=True) + jnp.exp(sink - m)
    return bdot(pr / denom, v)


def _swa_specs(S):
    nb = S // BLOCK
    q_spec = pl.BlockSpec((None, SWA_GROUP, BLOCK, HEAD_DIM), lambda h, n: (h, 0, n, 0))
    kc_spec = pl.BlockSpec((None, BLOCK, HEAD_DIM), lambda h, n: (h, n, 0))
    kp_spec = pl.BlockSpec((None, BLOCK, HEAD_DIM), lambda h, n: (h, jnp.maximum(n - 1, 0), 0))
    col_spec = pl.BlockSpec((None, SWA_GROUP * BLOCK, 1), lambda h, n: (h, 0, 0))
    return nb, q_spec, kp_spec, kc_spec, col_spec


def swa_fwd(q, k, v, sink_col, slope_col):
    S = q.shape[2]
    nb, q_spec, kp_spec, kc_spec, col_spec = _swa_specs(S)

    def body(q_ref, kp_ref, kc_ref, vp_ref, vc_ref, s_ref, a_ref, o_ref):
        n = pl.program_id(1)
        qq = q_ref[...].reshape(SWA_GROUP * BLOCK, HEAD_DIM)
        out = _swa_block(qq, kp_ref[...], kc_ref[...], vp_ref[...], vc_ref[...], s_ref[...], a_ref[...], n)
        o_ref[...] = out.reshape(SWA_GROUP, BLOCK, HEAD_DIM)

    return pl.pallas_call(
        body,
        grid=(SWA_KV_HEADS, nb),
        in_specs=[q_spec, kp_spec, kc_spec, kp_spec, kc_spec, col_spec, col_spec],
        out_specs=q_spec,
        out_shape=jax.ShapeDtypeStruct(q.shape, F32),
        compiler_params=_cparams(("parallel", "parallel")),
        name="swa_fwd",
    )(q, k, k, v, v, sink_col, slope_col)


def swa_bwd(q, k, v, sink_col, slope_col, dout):
    S = q.shape[2]
    nb, q_spec, kp_spec, kc_spec, col_spec = _swa_specs(S)

    def body(q_ref, kp_ref, kc_ref, vp_ref, vc_ref, s_ref, a_ref, do_ref,
             dq_ref, dkp_ref, dkc_ref, dvp_ref, dvc_ref, ds_ref):
        n = pl.program_id(1)
        qq = q_ref[...].reshape(SWA_GROUP * BLOCK, HEAD_DIM)
        slope = a_ref[...]
        f = lambda a, b, c, d, e, s: _swa_block(a, b, c, d, e, s, slope, n)
        _, vjp = jax.vjp(f, qq, kp_ref[...], kc_ref[...], vp_ref[...], vc_ref[...], s_ref[...])
        dq, dkp, dkc, dvp, dvc, ds = vjp(do_ref[...].reshape(SWA_GROUP * BLOCK, HEAD_DIM))
        dq_ref[...] = dq.reshape(SWA_GROUP, BLOCK, HEAD_DIM)
        dkp_ref[...] = dkp
        dkc_ref[...] = dkc
        dvp_ref[...] = dvp
        dvc_ref[...] = dvc

        @pl.when(n == 0)
        def _():
            ds_ref[...] = jnp.zeros_like(ds_ref)

        ds_ref[...] += ds

    kv_shape = jax.ShapeDtypeStruct(k.shape, F32)
    return pl.pallas_call(
        body,
        grid=(SWA_KV_HEADS, nb),
        in_specs=[q_spec, kp_spec, kc_spec, kp_spec, kc_spec, col_spec, col_spec, q_spec],
        out_specs=[q_spec, kc_spec, kc_spec, kc_spec, kc_spec, col_spec],
        out_shape=[jax.ShapeDtypeStruct(q.shape, F32), kv_shape, kv_shape, kv_shape, kv_shape,
                   jax.ShapeDtypeStruct(sink_col.shape, F32)],
        compiler_params=_cparams(("parallel", "arbitrary")),
        name="swa_bwd",
    )(q, k, k, v, v, sink_col, slope_col, dout)


def _split2(x):
    hi = x.astype(BF16)
    return (x - hi.astype(F32)).astype(BF16), hi


def _dot2(x, m):
    lo, hi = _split2(x)
    return jnp.dot(lo, m, preferred_element_type=F32) + jnp.dot(hi, m, preferred_element_type=F32)


def _seg_sum(x, bd):
    w = bd.shape[0]
    return jnp.concatenate([_dot2(x[:, i:i + w], bd) for i in range(0, x.shape[1], w)], axis=1)


def _scan_consts():
    r = np.arange(256)
    bd = (r[:, None] // HEAD_DIM == r[None, :] // HEAD_DIM).astype(np.float32)
    c = np.arange(RWKV_DIM)
    e = (np.arange(HEAD_DIM)[:, None] // SCAN_GROUP == c[None, :] // HEAD_DIM).astype(np.float32)
    diag = (np.arange(HEAD_DIM)[:, None] == c[None, :] % HEAD_DIM).astype(np.float32)
    return jnp.asarray(bd, BF16), jnp.asarray(e, BF16), jnp.asarray(diag, F32)


def _to_colblocks(a):
    S = a.shape[0]
    a = a.reshape(S // SCAN_GROUP, SCAN_GROUP, RWKV_HEADS, HEAD_DIM)
    return a.transpose(0, 3, 2, 1).reshape(S // SCAN_GROUP, HEAD_DIM, RWKV_HEADS * SCAN_GROUP)


def _scan_step(St, tt, base, col_g, lane_t, kk_ref, w_ref, b_ref, k_ref, bd, e):
    row = lambda ref: ref[pl.ds(base + tt, 1), :]
    u_b = _seg_sum(St * row(kk_ref), bd)
    v_b = _dot2(jnp.where(lane_t == tt, col_g, 0.0), e)
    return St * row(w_ref) - u_b * row(b_ref) + v_b * row(k_ref), u_b, v_b


def rwkv_scan_fwd(r, w, k, kk, b, vB):
    S, C = r.shape
    N, G = HEAD_DIM, SCAN_GROUP
    chunk = min(SCAN_CHUNK, S)
    nchunk, ng = S // chunk, chunk // G
    bd, e, diag = _scan_consts()

    def body(r_ref, w_ref, k_ref, kk_ref, b_ref, vB_ref, bd_ref, e_ref, dg_ref, y_ref, ck_ref, S_ref):
        c = pl.program_id(0)

        @pl.when(c == 0)
        def _():
            S_ref[...] = jnp.zeros_like(S_ref)

        ck_ref[...] = S_ref[...]
        sub = lax.broadcasted_iota(jnp.int32, (G, C), 0)
        lane_t = lax.broadcasted_iota(jnp.int32, (N, N), 1) & (G - 1)

        def group(g, St):
            base = pl.multiple_of(g * G, G)
            vb = vB_ref[g]
            ys = jnp.zeros((G, C), F32)
            for tt in range(G):
                St, _, _ = _scan_step(St, tt, base, vb, lane_t, kk_ref, w_ref, b_ref, k_ref, bd_ref[...], e_ref[...])
                y_b = _seg_sum(St * r_ref[pl.ds(base + tt, 1), :], bd_ref[...])
                ys = jnp.where(sub == tt, jnp.sum(y_b * dg_ref[...], axis=0, keepdims=True), ys)
            y_ref[pl.ds(base, G), :] = ys
            return St

        S_ref[...] = lax.fori_loop(0, ng, group, S_ref[...])

    row = pl.BlockSpec((chunk, C), lambda c: (c, 0))
    col = pl.BlockSpec((ng, N, N), lambda c: (c, 0, 0))
    return pl.pallas_call(
        body,
        grid=(nchunk,),
        in_specs=[row] * 5 + [col, _whole_spec(bd), _whole_spec(e), _whole_spec(diag)],
        out_specs=[row, pl.BlockSpec((None, N, C), lambda c: (c, 0, 0))],
        out_shape=[jax.ShapeDtypeStruct((S, C), F32), jax.ShapeDtypeStruct((nchunk, N, C), F32)],
        scratch_shapes=[pltpu.VMEM((N, C), F32)],
        compiler_params=_cparams(("arbitrary",)),
        name="rwkv_scan_fwd",
    )(r, w, k, kk, b, vB, bd, e, diag)


def rwkv_scan_bwd(r, w, k, kk, b, vB, dyB, ckpt):
    S, C = r.shape
    N, G = HEAD_DIM, SCAN_GROUP
    chunk = min(SCAN_CHUNK, S)
    nchunk, ng = S // chunk, chunk // G
    bd, e, diag = _scan_consts()

    def body(r_ref, w_ref, k_ref, kk_ref, b_ref, vB_ref, dyB_ref, ck_ref, bd_ref, e_ref, dg_ref,
             dr_ref, dw_ref, dk_ref, dkk_ref, db_ref, dv_ref, G_ref, sbuf, ubuf, vbuf):
        c = pl.program_id(0)

        @pl.when(c == 0)
        def _():
            G_ref[...] = jnp.zeros_like(G_ref)

        lane_t = lax.broadcasted_iota(jnp.int32, (N, N), 1) & (G - 1)
        sub = lax.broadcasted_iota(jnp.int32, (G, C), 0)

        def fgroup(g, St):
            base = pl.multiple_of(g * G, G)
            vb = vB_ref[g]
            for tt in range(G):
                sbuf[base + tt] = St
                St, u_b, v_b = _scan_step(St, tt, base, vb, lane_t, kk_ref, w_ref, b_ref, k_ref, bd_ref[...],
                                          e_ref[...])
                ubuf[base + tt] = u_b
                vbuf[base + tt] = v_b
            return St

        sbuf[chunk] = lax.fori_loop(0, ng, fgroup, ck_ref[...])

        def bgroup(gi, Gt):
            g = ng - 1 - gi
            base = pl.multiple_of(g * G, G)
            dyb = dyB_ref[g]
            rows = [jnp.zeros((G, C), F32) for _ in range(6)]
            colsum = lambda a: jnp.sum(a, axis=0, keepdims=True)
            for tt in reversed(range(G)):
                row = lambda ref: ref[pl.ds(base + tt, 1), :]
                Sp, Sc = sbuf[base + tt], sbuf[base + tt + 1]
                u_b, v_b = ubuf[base + tt], vbuf[base + tt]
                dy_b = _dot2(jnp.where(lane_t == tt, dyb, 0.0), e_ref[...])
                Gt = Gt + dy_b * row(r_ref)
                d_r = colsum(Sc * dy_b)
                d_w = colsum(Gt * Sp)
                du_b = -_seg_sum(Gt * row(b_ref), bd_ref[...])
                d_b = -colsum(Gt * u_b)
                d_v = colsum(_seg_sum(Gt * row(k_ref), bd_ref[...]) * dg_ref[...])
                d_k = colsum(Gt * v_b)
                d_kk = colsum(Sp * du_b)
                Gt = Gt * row(w_ref) + du_b * row(kk_ref)
                rows = [jnp.where(sub == tt, new, acc)
                        for new, acc in zip((d_r, d_w, d_k, d_kk, d_b, d_v), rows)]
            for ref, val in zip((dr_ref, dw_ref, dk_ref, dkk_ref, db_ref, dv_ref), rows):
                ref[pl.ds(base, G), :] = val
            return Gt

        G_ref[...] = lax.fori_loop(0, ng, bgroup, G_ref[...])

    rev = lambda c: nchunk - 1 - c
    row = pl.BlockSpec((chunk, C), lambda c: (rev(c), 0))
    col = pl.BlockSpec((ng, N, N), lambda c: (rev(c), 0, 0))
    rshape = jax.ShapeDtypeStruct((S, C), F32)
    return pl.pallas_call(
        body,
        grid=(nchunk,),
        in_specs=[row] * 5 + [col, col, pl.BlockSpec((None, N, C), lambda c: (rev(c), 0, 0)),
                              _whole_spec(bd), _whole_spec(e), _whole_spec(diag)],
        out_specs=[row] * 6,
        out_shape=[rshape] * 6,
        scratch_shapes=[pltpu.VMEM((N, C), F32), pltpu.VMEM((chunk + 1, N, C), F32),
                        pltpu.VMEM((chunk, N, C), F32), pltpu.VMEM((chunk, N, C), F32)],
        compiler_params=_cparams(("arbitrary",)),
        name="rwkv_scan_bwd",
    )(r, w, k, kk, b, vB, dyB, ckpt, bd, e, diag)


def seq_cumsum(x, reverse, name):
    S, C = x.shape
    tb = min(256, S)
    nb = S // tb

    def body(x_ref, o_ref, carry):
        i = pl.program_id(0)

        @pl.when(i == 0)
        def _():
            carry[...] = jnp.zeros_like(carry)

        ri = lax.broadcasted_iota(jnp.int32, (tb, tb), 0)
        ci = lax.broadcasted_iota(jnp.int32, (tb, tb), 1)
        tri = jnp.where((ci >= ri) if reverse else (ci <= ri), 1.0, 0.0).astype(F32)
        xb = x_ref[...]
        out = jnp.dot(tri, xb, precision=lax.Precision.HIGHEST, preferred_element_type=F32) + carry[...]
        o_ref[...] = out
        carry[...] = carry[...] + jnp.sum(xb, axis=0, keepdims=True)

    idx = (lambda i: (nb - 1 - i, 0)) if reverse else (lambda i: (i, 0))
    return pl.pallas_call(
        body,
        grid=(nb,),
        in_specs=[pl.BlockSpec((tb, C), idx)],
        out_specs=pl.BlockSpec((tb, C), idx),
        out_shape=jax.ShapeDtypeStruct((S, C), F32),
        scratch_shapes=[pltpu.VMEM((1, C), F32)],
        compiler_params=_cparams(("arbitrary",)),
        name=name,
    )(x)


def _fox_logits(q, k, cq, ck, qi, ki, tq, tk):
    s = _dot(q, k, _NT) * (HEAD_DIM ** -0.5) + cq - ck
    row = qi * tq + lax.broadcasted_iota(jnp.int32, (tq, tk), 0)
    col = ki * tk + lax.broadcasted_iota(jnp.int32, (tq, tk), 1)
    return jnp.where(col <= row, s, -jnp.inf)


def fox_fwd(q, k, v, c_col, c_row):
    Hh, S, Dh = q.shape
    tq = tk = min(512, S)
    nq, nk = S // tq, S // tk

    def body(q_ref, k_ref, v_ref, cq_ref, ck_ref, o_ref, lse_ref, m_s, l_s, acc_s):
        qi, ki = pl.program_id(1), pl.program_id(2)

        @pl.when(ki == 0)
        def _():
            m_s[...] = jnp.full_like(m_s, -jnp.inf)
            l_s[...] = jnp.zeros_like(l_s)
            acc_s[...] = jnp.zeros_like(acc_s)

        @pl.when(ki <= qi)
        def _():
            s = _fox_logits(q_ref[...], k_ref[...], cq_ref[...], ck_ref[...], qi, ki, tq, tk)
            m_new = jnp.maximum(m_s[...], jnp.max(s, axis=-1, keepdims=True))
            alpha = jnp.exp(m_s[...] - m_new)
            p = jnp.exp(s - m_new)
            l_s[...] = alpha * l_s[...] + jnp.sum(p, axis=-1, keepdims=True)
            acc_s[...] = alpha * acc_s[...] + _dot(p, v_ref[...], _NN)
            m_s[...] = m_new

        @pl.when(ki == nk - 1)
        def _():
            o_ref[...] = acc_s[...] / l_s[...]
            lse_ref[...] = m_s[...] + jnp.log(l_s[...])

    qs = pl.BlockSpec((None, tq, Dh), lambda h, i, j: (h, i, 0))
    ks = pl.BlockSpec((None, tk, Dh), lambda h, i, j: (h, jnp.minimum(i, j), 0))
    cqs = pl.BlockSpec((None, tq, 1), lambda h, i, j: (h, i, 0))
    cks = pl.BlockSpec((None, 1, tk), lambda h, i, j: (h, 0, jnp.minimum(i, j)))
    return pl.pallas_call(
        body,
        grid=(Hh, nq, nk),
        in_specs=[qs, ks, ks, cqs, cks],
        out_specs=[qs, cqs],
        out_shape=[jax.ShapeDtypeStruct((Hh, S, Dh), F32), jax.ShapeDtypeStruct((Hh, S, 1), F32)],
        scratch_shapes=[pltpu.VMEM((tq, 1), F32), pltpu.VMEM((tq, 1), F32), pltpu.VMEM((tq, Dh), F32)],
        compiler_params=_cparams(("parallel", "parallel", "arbitrary")),
        name="fox_fwd",
    )(q, k, v, c_col, c_row)


def fox_bwd_dq(q, k, v, c_col, c_row, o, lse, do):
    Hh, S, Dh = q.shape
    tq = tk = min(512, S)
    nq, nk = S // tq, S // tk

    def body(q_ref, k_ref, v_ref, cq_ref, ck_ref, o_ref, lse_ref, do_ref, dq_ref, dr_ref, acc_s, row_s):
        qi, ki = pl.program_id(1), pl.program_id(2)

        @pl.when(ki == 0)
        def _():
            acc_s[...] = jnp.zeros_like(acc_s)
            row_s[...] = jnp.zeros_like(row_s)

        @pl.when(ki <= qi)
        def _():
            s = _fox_logits(q_ref[...], k_ref[...], cq_ref[...], ck_ref[...], qi, ki, tq, tk)
            p = jnp.exp(s - lse_ref[...])
            do_t = do_ref[...]
            delta = jnp.sum(do_t * o_ref[...], axis=-1, keepdims=True)
            dp = _dot(do_t, v_ref[...], _NT)
            ds = p * (dp - delta)
            acc_s[...] += _dot(ds, k_ref[...], _NN)
            row_s[...] += jnp.sum(ds, axis=-1, keepdims=True)

        @pl.when(ki == nk - 1)
        def _():
            dq_ref[...] = acc_s[...] * (HEAD_DIM ** -0.5)
            dr_ref[...] = row_s[...]

    qs = pl.BlockSpec((None, tq, Dh), lambda h, i, j: (h, i, 0))
    ks = pl.BlockSpec((None, tk, Dh), lambda h, i, j: (h, jnp.minimum(i, j), 0))
    cqs = pl.BlockSpec((None, tq, 1), lambda h, i, j: (h, i, 0))
    cks = pl.BlockSpec((None, 1, tk), lambda h, i, j: (h, 0, jnp.minimum(i, j)))
    return pl.pallas_call(
        body,
        grid=(Hh, nq, nk),
        in_specs=[qs, ks, ks, cqs, cks, qs, cqs, qs],
        out_specs=[qs, cqs],
        out_shape=[jax.ShapeDtypeStruct((Hh, S, Dh), F32), jax.ShapeDtypeStruct((Hh, S, 1), F32)],
        scratch_shapes=[pltpu.VMEM((tq, Dh), F32), pltpu.VMEM((tq, 1), F32)],
        compiler_params=_cparams(("parallel", "parallel", "arbitrary")),
        name="fox_bwd_dq",
    )(q, k, v, c_col, c_row, o, lse, do)


def fox_bwd_dkv(q, k, v, c_col, c_row, o, lse, do):
    Hh, S, Dh = q.shape
    tq = tk = min(512, S)
    nq, nk = S // tq, S // tk

    def body(q_ref, k_ref, v_ref, cq_ref, ck_ref, o_ref, lse_ref, do_ref, dk_ref, dv_ref, dc_ref, dk_s, dv_s, dc_s):
        ki, qi = pl.program_id(1), pl.program_id(2)

        @pl.when(qi == 0)
        def _():
            dk_s[...] = jnp.zeros_like(dk_s)
            dv_s[...] = jnp.zeros_like(dv_s)
            dc_s[...] = jnp.zeros_like(dc_s)

        @pl.when(qi >= ki)
        def _():
            s = _fox_logits(q_ref[...], k_ref[...], cq_ref[...], ck_ref[...], qi, ki, tq, tk)
            p = jnp.exp(s - lse_ref[...])
            do_t = do_ref[...]
            delta = jnp.sum(do_t * o_ref[...], axis=-1, keepdims=True)
            dp = _dot(do_t, v_ref[...], _NT)
            ds = p * (dp - delta)
            dv_s[...] += _dot(p, do_t, _TN)
            dk_s[...] += _dot(ds, q_ref[...], _TN)
            dc_s[...] += jnp.sum(ds, axis=0, keepdims=True)

        @pl.when(qi == nq - 1)
        def _():
            dk_ref[...] = dk_s[...] * (HEAD_DIM ** -0.5)
            dv_ref[...] = dv_s[...]
            dc_ref[...] = dc_s[...]

    qs = pl.BlockSpec((None, tq, Dh), lambda h, j, i: (h, jnp.maximum(i, j), 0))
    ks = pl.BlockSpec((None, tk, Dh), lambda h, j, i: (h, j, 0))
    cqs = pl.BlockSpec((None, tq, 1), lambda h, j, i: (h, jnp.maximum(i, j), 0))
    cks = pl.BlockSpec((None, 1, tk), lambda h, j, i: (h, 0, j))
    return pl.pallas_call(
        body,
        grid=(Hh, nk, nq),
        in_specs=[qs, ks, ks, cqs, cks, qs, cqs, qs],
        out_specs=[ks, ks, cks],
        out_shape=[jax.ShapeDtypeStruct((Hh, S, Dh), F32), jax.ShapeDtypeStruct((Hh, S, Dh), F32),
                   jax.ShapeDtypeStruct((Hh, 1, S), F32)],
        scratch_shapes=[pltpu.VMEM((tk, Dh), F32), pltpu.VMEM((tk, Dh), F32), pltpu.VMEM((1, tk), F32)],
        compiler_params=_cparams(("parallel", "parallel", "arbitrary")),
        name="fox_bwd_dkv",
    )(q, k, v, c_col, c_row, o, lse, do)


def _heads(a, nh):
    S = a.shape[0]
    return a.reshape(S, nh, HEAD_DIM).transpose(1, 0, 2)


def _unheads(a):
    nh, S, _ = a.shape
    return a.transpose(1, 0, 2).reshape(S, nh * HEAD_DIM)


def _shift_down(a):
    return jnp.pad(a[:-1], ((1, 0), (0, 0)))


def _shift_up(a):
    return jnp.pad(a[1:], ((0, 1), (0, 0)))


def _block_diag_ones():
    i = np.arange(RWKV_DIM) // HEAD_DIM
    return jnp.asarray((i[:, None] == i[None, :]).astype(np.float32))


def ffn_fwd(x, g_norm, w_gu, w_down, tag):
    hn, = rowwise(_f_rms, [x], [g_norm], [(D_MODEL, BF16)], tag + "_rms")
    gu = matmul(hn, w_gu, "nn", tag + "_gu")
    act, = rowwise(_f_act, [(gu, D_FF, 0), (gu, D_FF, 1)], [], [(D_FF, BF16)], tag + "_act", tm=128)
    out = matmul(act, w_down, "nn", tag + "_down", scale=0.5, res=x)
    return out, (x, hn, gu, act)


def ffn_bwd(dy, saved, g_norm, w_gu, w_down, tag):
    x, hn, gu, act = saved
    dact = matmul(dy, w_down, "nt", tag + "_dact", scale=0.5)
    d_wdown = matmul(act, dy, "tn", tag + "_dwd", out_dtype=BF16, scale=0.5)
    (dg, du), _ = rowwise_vjp(_f_act, [(gu, D_FF, 0), (gu, D_FF, 1)], [], [dact], tag + "_dactf",
                              row_dtype=BF16, tm=128)
    dgu = jnp.concatenate([dg, du], axis=1)
    d_wgu = matmul(hn, dgu, "tn", tag + "_dwgu", out_dtype=BF16)
    dhn = matmul(dgu, w_gu, "nt", tag + "_dhn")
    (dx,), (dgn,) = rowwise_vjp(_f_rms, [x], [g_norm], [dhn], tag + "_drms")
    return dy + dx, dgn, d_wgu, d_wdown


def ple_fwd(x, p_i, g_norm, w_gate, w_proj, tag):
    hn, = rowwise(_f_rms, [x], [g_norm], [(D_MODEL, BF16)], tag + "_rms")
    z = matmul(hn, w_gate, "nn", tag + "_gate")
    pp = matmul(p_i, w_proj, "nn", tag + "_proj")
    out, = rowwise(_f_ple, [x, z, pp], [], [(D_MODEL, F32)], tag + "_mix")
    return out, (x, hn, z, pp)


def ple_bwd(dy, saved, p_i, g_norm, w_gate, tag):
    x, hn, z, pp = saved
    (dz, dpp), _ = rowwise_vjp(_f_ple, [x, z, pp], [], [dy], tag + "_dmix", need=[False, True, True],
                               row_dtype=BF16)
    d_wproj = matmul(p_i, dpp, "tn", tag + "_dwp", out_dtype=BF16)
    d_wgate = matmul(hn, dz, "tn", tag + "_dwg", out_dtype=BF16)
    dhn = matmul(dz, w_gate, "nt", tag + "_dhn")
    (dx,), (dgn,) = rowwise_vjp(_f_rms, [x], [g_norm], [dhn], tag + "_drms")
    return dy + dx, dgn, d_wgate, d_wproj


def _swa_consts(sinks):
    slopes = np.asarray([2.0 ** (-(i + 1)) for i in range(SWA_HEADS)], np.float32)
    slope_col = jnp.asarray(np.repeat(slopes, BLOCK).reshape(SWA_KV_HEADS, SWA_GROUP * BLOCK, 1))
    sink_col = jnp.repeat(sinks.reshape(SWA_HEADS), BLOCK).reshape(SWA_KV_HEADS, SWA_GROUP * BLOCK, 1)
    return sink_col, slope_col


def even_mix_fwd(x, W):
    S = x.shape[0]
    hn, = rowwise(_f_rms, [x], [W["mix_norm0"]], [(D_MODEL, BF16)], "emix_rms")
    proj = matmul(hn, W["even_w_in"], "nn", "emix_in")
    qa = _heads(proj[:, :SWA_Q], SWA_HEADS).reshape(SWA_KV_HEADS, SWA_GROUP, S, HEAD_DIM)
    ka = _heads(proj[:, SWA_Q:SWA_Q + SWA_KV], SWA_KV_HEADS)
    va = _heads(proj[:, SWA_Q + SWA_KV:SWA_COLS], SWA_KV_HEADS)
    sink_col, slope_col = _swa_consts(W["swa_sinks"])
    ya = swa_fwd(qa, ka, va, sink_col, slope_col)
    ya = _unheads(ya.reshape(SWA_HEADS, S, HEAD_DIM))
    hb = proj[:, SWA_COLS:]
    h, = rowwise(_f_mix, [hb, _shift_down(hb)], [W["rwkv_mu"]], [(hb.shape[1], F32)], "rwkv_shift")
    hr, hk, hv = h[:, :512], h[:, 512:1024], h[:, 1024:1536]
    hw, ha, hg = h[:, 1536:1600], h[:, 1600:1664], h[:, 1664:1792]
    bd = _block_diag_ones()
    pre_params = [W["rwkv_w0"], W["rwkv_w2"], W["rwkv_a0"], W["rwkv_a2"], W["rwkv_g2"], W["rwkv_k_k"],
                  W["rwkv_k_a"]]
    decay, k2, kk, b, g = rowwise(_f_rwkv_pre, [hk, hw, ha, hg], pre_params + [bd],
                                  [(RWKV_DIM, F32)] * 5, "rwkv_pre")
    vT = _to_colblocks(hv)
    y, ckpt = rwkv_scan_fwd(hr, decay, k2, kk, b, vT)
    post_params = [W["rwkv_ln_w"], W["rwkv_ln_b"], W["rwkv_r_k"]]
    yb, = rowwise(_f_rwkv_post, [y, hr, k2, hv, g], post_params + [bd], [(RWKV_DIM, F32)], "rwkv_post")
    cat = jnp.concatenate([ya, yb], axis=1).astype(BF16)
    out = matmul(cat, W["even_w_out"], "nn", "emix_out", res=x)
    saved = (x, hn, qa, ka, va, sink_col, slope_col, hb, hr, hk, hv, hw, ha, hg, decay, k2, kk, b, g, vT,
             ckpt, y, cat)
    return out, saved


def even_mix_bwd(dy, saved, W):
    (x, hn, qa, ka, va, sink_col, slope_col, hb, hr, hk, hv, hw, ha, hg, decay, k2, kk, b, g, vT, ckpt, y,
     cat) = saved
    S = x.shape[0]
    grads = {}
    dcat = matmul(dy, W["even_w_out"], "nt", "emix_dcat")
    grads["even_w_out"] = matmul(cat, dy, "tn", "emix_dwout", out_dtype=BF16)
    dya, dyb = dcat[:, :SWA_Q], dcat[:, SWA_Q:]
    dya_h = _heads(dya, SWA_HEADS).reshape(SWA_KV_HEADS, SWA_GROUP, S, HEAD_DIM)
    dqa, dkp, dkc, dvp, dvc, dsink = swa_bwd(qa, ka, va, sink_col, slope_col, dya_h)
    shift_blk = lambda a: jnp.pad(a[:, BLOCK:], ((0, 0), (0, BLOCK), (0, 0)))
    dka = dkc + shift_blk(dkp)
    dva = dvc + shift_blk(dvp)
    grads["swa_sinks"] = dsink.reshape(SWA_HEADS, BLOCK).sum(axis=1).reshape(1, SWA_HEADS)
    dqa = _unheads(dqa.reshape(SWA_HEADS, S, HEAD_DIM))
    dka, dva = _unheads(dka), _unheads(dva)
    bd = _block_diag_ones()
    post_params = [W["rwkv_ln_w"], W["rwkv_ln_b"], W["rwkv_r_k"]]
    (d_y, d_r1, d_k2a, d_v1, d_g), (d_lnw, d_lnb, d_rk) = rowwise_vjp(
        _f_rwkv_post, [y, hr, k2, hv, g], post_params, [dyb], "rwkv_dpost", consts=[bd], tm=128)
    grads["rwkv_ln_w"], grads["rwkv_ln_b"], grads["rwkv_r_k"] = d_lnw, d_lnb, d_rk
    d_r2, d_w, d_k2b, d_kk, d_b, d_v2 = rwkv_scan_bwd(hr, decay, k2, kk, b, vT, _to_colblocks(d_y), ckpt)
    pre_params = [W["rwkv_w0"], W["rwkv_w2"], W["rwkv_a0"], W["rwkv_a2"], W["rwkv_g2"], W["rwkv_k_k"],
                  W["rwkv_k_a"]]
    (d_hk, d_hw, d_ha, d_hg), dpre = rowwise_vjp(
        _f_rwkv_pre, [hk, hw, ha, hg], pre_params, [d_w, d_k2a + d_k2b, d_kk, d_b, d_g], "rwkv_dpre",
        consts=[bd], tm=128)
    for nm, gval in zip(["rwkv_w0", "rwkv_w2", "rwkv_a0", "rwkv_a2", "rwkv_g2", "rwkv_k_k", "rwkv_k_a"], dpre):
        grads[nm] = gval
    d_h = jnp.concatenate([d_r1 + d_r2, d_hk, d_v1 + d_v2, d_hw, d_ha, d_hg], axis=1)
    (d_hb, d_sh), (d_mu,) = rowwise_vjp(_f_mix, [hb, _shift_down(hb)], [W["rwkv_mu"]], [d_h], "rwkv_dshift")
    grads["rwkv_mu"] = d_mu
    d_hb = d_hb + _shift_up(d_sh)
    dproj = jnp.concatenate([dqa, dka, dva, d_hb], axis=1).astype(BF16)
    grads["even_w_in"] = matmul(hn, dproj, "tn", "emix_dwin", out_dtype=BF16)
    dhn = matmul(dproj, W["even_w_in"], "nt", "emix_dhn")
    (dx,), (dgn,) = rowwise_vjp(_f_rms, [x], [W["mix_norm0"]], [dhn], "emix_drms")
    grads["mix_norm0"] = dgn
    return dy + dx, grads


def odd_mix_fwd(x, W):
    S = x.shape[0]
    hn, = rowwise(_f_rms, [x], [W["mix_norm1"]], [(D_MODEL, BF16)], "omix_rms")
    proj = matmul(hn, W["fox_w_in"], "nn", "omix_in")
    q = _heads(proj[:, :FOX_DIM], FOX_HEADS).astype(BF16)
    k = _heads(proj[:, FOX_DIM:2 * FOX_DIM], FOX_HEADS).astype(BF16)
    v = _heads(proj[:, 2 * FOX_DIM:3 * FOX_DIM], FOX_HEADS).astype(BF16)
    fz = proj[:, 3 * FOX_DIM:]
    logf, = rowwise(_f_logf, [fz], [W["fox_b_f"]], [(128, F32)], "fox_logf")
    c = seq_cumsum(logf, False, "fox_cumsum")[:, :FOX_HEADS]
    c_col = c.T.reshape(FOX_HEADS, S, 1)
    c_row = c.T.reshape(FOX_HEADS, 1, S)
    o, lse = fox_fwd(q, k, v, c_col, c_row)
    yc = _unheads(o).astype(BF16)
    out = matmul(yc, W["fox_w_out"], "nn", "omix_out", res=x)
    return out, (x, hn, q, k, v, fz, c_col, c_row, o, lse, yc)


def odd_mix_bwd(dy, saved, W):
    x, hn, q, k, v, fz, c_col, c_row, o, lse, yc = saved
    S = x.shape[0]
    grads = {}
    dyc = matmul(dy, W["fox_w_out"], "nt", "omix_dyc")
    grads["fox_w_out"] = matmul(yc, dy, "tn", "omix_dwout", out_dtype=BF16)
    do = _heads(dyc, FOX_HEADS)
    dq, drow = fox_bwd_dq(q, k, v, c_col, c_row, o, lse, do)
    dk, dv, dcol = fox_bwd_dkv(q, k, v, c_col, c_row, o, lse, do)
    dc = (drow.reshape(FOX_HEADS, S) - dcol.reshape(FOX_HEADS, S)).T
    dc = jnp.pad(dc, ((0, 0), (0, 128 - FOX_HEADS)))
    dlogf = seq_cumsum(dc, True, "fox_rcumsum")
    (dfz,), (dbf,) = rowwise_vjp(_f_logf, [fz], [W["fox_b_f"]], [dlogf], "fox_dlogf")
    grads["fox_b_f"] = dbf
    dproj = jnp.concatenate([_unheads(dq), _unheads(dk), _unheads(dv), dfz], axis=1).astype(BF16)
    grads["fox_w_in"] = matmul(hn, dproj, "tn", "omix_dwin", out_dtype=BF16)
    dhn = matmul(dproj, W["fox_w_in"], "nt", "omix_dhn")
    (dx,), (dgn,) = rowwise_vjp(_f_rms, [x], [W["mix_norm1"]], [dhn], "omix_drms")
    grads["mix_norm1"] = dgn
    return dy + dx, grads


def device_step(x, p, target, W):
    saved = []
    h = x
    for i in range(2):
        h, s1 = ffn_fwd(h, W[f"ffn1_norm{i}"], W[f"ffn1_w_gu{i}"], W[f"ffn1_w_down{i}"], f"ffn1_{i}")
        h, s2 = even_mix_fwd(h, W) if i == 0 else odd_mix_fwd(h, W)
        h, s3 = ffn_fwd(h, W[f"ffn2_norm{i}"], W[f"ffn2_w_gu{i}"], W[f"ffn2_w_down{i}"], f"ffn2_{i}")
        h, s4 = ple_fwd(h, p[i], W[f"ple_norm{i}"], W[f"ple_w_gate{i}"], W[f"ple_w_proj{i}"], f"ple_{i}")
        saved.append((s1, s2, s3, s4))
    dh, d_final, loss = loss_head(h, target, W["final_norm"])
    G = {"final_norm": d_final}
    for i in (1, 0):
        s1, s2, s3, s4 = saved[i]
        dh, G[f"ple_norm{i}"], G[f"ple_w_gate{i}"], G[f"ple_w_proj{i}"] = ple_bwd(
            dh, s4, p[i], W[f"ple_norm{i}"], W[f"ple_w_gate{i}"], f"ple_{i}")
        dh, G[f"ffn2_norm{i}"], G[f"ffn2_w_gu{i}"], G[f"ffn2_w_down{i}"] = ffn_bwd(
            dh, s3, W[f"ffn2_norm{i}"], W[f"ffn2_w_gu{i}"], W[f"ffn2_w_down{i}"], f"ffn2_{i}")
        dh, gm = even_mix_bwd(dh, s2, W) if i == 0 else odd_mix_bwd(dh, s2, W)
        G.update(gm)
        dh, G[f"ffn1_norm{i}"], G[f"ffn1_w_gu{i}"], G[f"ffn1_w_down{i}"] = ffn_bwd(
            dh, s1, W[f"ffn1_norm{i}"], W[f"ffn1_w_gu{i}"], W[f"ffn1_w_down{i}"], f"ffn1_{i}")
    return loss, dh, G


_MESH = pl.DeviceIdType.MESH
_ANY = pl.BlockSpec(memory_space=pl.ANY)


def all_gather(x, name):
    def body(x_ref, out_ref, send_sems, recv_sems, local_sem):
        x_, y_, c_ = lax.axis_index("x"), lax.axis_index("y"), lax.axis_index("c")
        me, sibling = (x_, y_, c_), (x_, y_, 1 - c_)
        chips = [(1 - x_, y_), (x_, 1 - y_), (1 - x_, 1 - y_)]

        def slot(px, py, pc):
            return out_ref.at[4 * px + 2 * py + pc]

        def copy(k, block, to, src=None):
            return pltpu.make_async_remote_copy(
                src_ref=slot(*block) if src is None else src, dst_ref=slot(*block),
                send_sem=send_sems.at[k], recv_sem=recv_sems.at[k], device_id=to, device_id_type=_MESH)

        mine = pltpu.make_async_copy(x_ref, slot(*me), local_sem)
        mine.start()
        first = [copy(0, me, sibling, src=x_ref)]
        first += [copy(1 + j, me, (*chip, c_), src=x_ref) for j, chip in enumerate(chips)]
        for cp in first:
            cp.start()
        passed = [copy(4 + j, (*chip, c_), sibling) for j, chip in enumerate(chips)]
        for j, chip in enumerate(chips):
            copy(1 + j, (*chip, c_), me).wait_recv()
            passed[j].start()
        copy(0, sibling, me).wait_recv()
        for j, chip in enumerate(chips):
            copy(4 + j, (*chip, 1 - c_), me).wait_recv()
        for cp in first + passed:
            cp.wait_send()
        mine.wait()

    return pl.pallas_call(
        body,
        out_shape=jax.ShapeDtypeStruct((N_DEV,) + x.shape, x.dtype),
        in_specs=[_ANY],
        out_specs=_ANY,
        scratch_shapes=[pltpu.SemaphoreType.DMA((7,)), pltpu.SemaphoreType.DMA((7,)), pltpu.SemaphoreType.DMA(())],
        name=name,
    )(x)


def all_to_all(send, name):
    def body(s_ref, r_ref, send_sems, recv_sems, local_sem):
        x_, y_, c_ = lax.axis_index("x"), lax.axis_index("y"), lax.axis_index("c")
        my = 4 * x_ + 2 * y_ + c_
        local = pltpu.make_async_copy(s_ref.at[my], r_ref.at[my], local_sem)
        local.start()
        copies = []
        for m in range(1, N_DEV):
            px = 1 - x_ if (m >> 2) & 1 else x_
            py = 1 - y_ if (m >> 1) & 1 else y_
            pc = 1 - c_ if m & 1 else c_
            cp = pltpu.make_async_remote_copy(
                src_ref=s_ref.at[4 * px + 2 * py + pc], dst_ref=r_ref.at[my],
                send_sem=send_sems.at[m - 1], recv_sem=recv_sems.at[m - 1],
                device_id=(px, py, pc), device_id_type=_MESH)
            cp.start()
            copies.append(cp)
        for cp in copies:
            cp.wait()
        local.wait()

    return pl.pallas_call(
        body,
        out_shape=jax.ShapeDtypeStruct(send.shape, send.dtype),
        in_specs=[_ANY],
        out_specs=_ANY,
        scratch_shapes=[pltpu.SemaphoreType.DMA((7,)), pltpu.SemaphoreType.DMA((7,)), pltpu.SemaphoreType.DMA(())],
        name=name,
    )(send)


def adamw(w, m, v, parts, name, tm=256):
    R, C = w.shape
    tm = _pick(R, tm, 8) if R >= 8 else R

    def body(w_ref, m_ref, v_ref, p_ref, g_ref, d_ref, nm_ref, nv_ref):
        g = p_ref[0].astype(F32)
        for s in range(1, N_DEV):
            g = g + p_ref[s].astype(F32)
        nm = ADAM_B1 * m_ref[...] + (1.0 - ADAM_B1) * g
        nv = ADAM_B2 * v_ref[...] + (1.0 - ADAM_B2) * (g * g)
        m_hat = nm / (1.0 - ADAM_B1 ** ADAM_STEP)
        v_hat = nv / (1.0 - ADAM_B2 ** ADAM_STEP)
        g_ref[...] = g
        d_ref[...] = -ADAM_LR * (m_hat / (jnp.sqrt(v_hat) + ADAM_EPS) + ADAM_WD * w_ref[...])
        nm_ref[...] = nm
        nv_ref[...] = nv

    row = pl.BlockSpec((tm, C), lambda i: (i, 0))
    out = jax.ShapeDtypeStruct((R, C), F32)
    return pl.pallas_call(
        body,
        grid=(R // tm,),
        in_specs=[row, row, row, pl.BlockSpec((N_DEV, tm, C), lambda i: (0, i, 0))],
        out_specs=[row] * 4,
        out_shape=[out] * 4,
        compiler_params=_cparams(("parallel",)),
        name=name,
    )(w, m, v, parts)


_WEIGHTS = ["ffn1_norm", "ffn1_w_gu", "ffn1_w_down", "mix_norm", "ffn2_norm", "ffn2_w_gu", "ffn2_w_down",
            "ple_norm", "ple_w_gate", "ple_w_proj", "even_w_in", "even_w_out", "swa_sinks", "rwkv_mu",
            "rwkv_w0", "rwkv_w2", "rwkv_a0", "rwkv_a2", "rwkv_g2", "rwkv_k_k", "rwkv_k_a", "rwkv_r_k",
            "rwkv_ln_w", "rwkv_ln_b", "fox_w_in", "fox_b_f", "fox_w_out", "final_norm"]
_SHARD_AXIS = {"ffn1_w_gu": 2, "ffn1_w_down": 1, "ffn2_w_gu": 2, "ffn2_w_down": 1, "ple_w_gate": 1,
               "ple_w_proj": 2, "even_w_in": 2, "even_w_out": 1, "rwkv_w2": 2, "rwkv_a2": 2, "rwkv_g2": 2,
               "fox_w_in": 2, "fox_w_out": 1}
_SHARDED = [n for n in _WEIGHTS if n in _SHARD_AXIS]
_REPLICATED = [n for n in _WEIGHTS if n not in _SHARD_AXIS]
_PACK_LANES = 1024
_PACK_ROW_TILE = 256


def _pack_rows(arrs):
    flat = jnp.concatenate([a.reshape(-1, _PACK_LANES) for a in arrs], axis=0)
    return jnp.pad(flat, ((0, -flat.shape[0] % _PACK_ROW_TILE), (0, 0)))


def _unpack_rows(flat, shapes):
    out, r0 = [], 0
    for shp in shapes:
        n = math.prod(shp) // _PACK_LANES
        out.append(flat[r0:r0 + n].reshape(shp))
        r0 += n
    return out


def _unshard(gathered, shard_shapes):
    full, r0 = {}, 0
    for name, shp in shard_shapes.items():
        n = math.prod(shp) // _PACK_LANES
        seg = gathered[:, r0:r0 + n].reshape((N_DEV,) + shp)
        ax = _SHARD_AXIS[name]
        seg = jnp.moveaxis(seg, 0, ax)
        full[name] = seg.reshape(shp[:ax] + (N_DEV * shp[ax],) + shp[ax + 1:])
        r0 += n
    return full


def _to_shards(full, shard_shapes):
    segs = []
    for name, shp in shard_shapes.items():
        ax = _SHARD_AXIS[name]
        a = full[name].reshape(shp[:ax] + (N_DEV, shp[ax]) + shp[ax + 1:])
        segs.append(jnp.moveaxis(a, ax, 0).reshape(N_DEV, -1, _PACK_LANES))
    flat = jnp.concatenate(segs, axis=1)
    return jnp.pad(flat, ((0, 0), (0, -flat.shape[1] % _PACK_ROW_TILE), (0, 0)))


def _pack_small(vals):
    flat = jnp.concatenate([v.reshape(1, -1) for v in vals], axis=1)
    n = flat.shape[1]
    return jnp.pad(flat, ((0, 0), (0, -n % 128)))


def _unpack_small(flat, shapes):
    out, c0 = [], 0
    for shp in shapes:
        n = math.prod(shp)
        out.append(flat[0, c0:c0 + n].reshape(shp))
        c0 += n
    return out


def kernel(x, p, ffn1_norm, ffn1_w_gu, ffn1_w_down, mix_norm, ffn2_norm, ffn2_w_gu, ffn2_w_down, ple_norm, ple_w_gate, ple_w_proj, even_w_in, even_w_out, swa_sinks, rwkv_mu, rwkv_w0, rwkv_w2, rwkv_a0, rwkv_a2, rwkv_g2, rwkv_k_k, rwkv_k_a, rwkv_r_k, rwkv_ln_w, rwkv_ln_b, fox_w_in, fox_b_f, fox_w_out, final_norm, loss_target, m_ffn1_norm, m_ffn1_w_gu, m_ffn1_w_down, m_mix_norm, m_ffn2_norm, m_ffn2_w_gu, m_ffn2_w_down, m_ple_norm, m_ple_w_gate, m_ple_w_proj, m_even_w_in, m_even_w_out, m_swa_sinks, m_rwkv_mu, m_rwkv_w0, m_rwkv_w2, m_rwkv_a0, m_rwkv_a2, m_rwkv_g2, m_rwkv_k_k, m_rwkv_k_a, m_rwkv_r_k, m_rwkv_ln_w, m_rwkv_ln_b, m_fox_w_in, m_fox_b_f, m_fox_w_out, m_final_norm, v_ffn1_norm, v_ffn1_w_gu, v_ffn1_w_down, v_mix_norm, v_ffn2_norm, v_ffn2_w_gu, v_ffn2_w_down, v_ple_norm, v_ple_w_gate, v_ple_w_proj, v_even_w_in, v_even_w_out, v_swa_sinks, v_rwkv_mu, v_rwkv_w0, v_rwkv_w2, v_rwkv_a0, v_rwkv_a2, v_rwkv_g2, v_rwkv_k_k, v_rwkv_k_a, v_rwkv_r_k, v_rwkv_ln_w, v_rwkv_ln_b, v_fox_w_in, v_fox_b_f, v_fox_w_out, v_final_norm):
    given = dict(locals())
    w = {n: given[n] for n in _WEIGHTS}
    m = {n: given["m_" + n] for n in _WEIGHTS}
    v = {n: given["v_" + n] for n in _WEIGHTS}
    shard_shapes = {n: w[n].shape for n in _SHARDED}
    small_shapes = [w[n].shape for n in _REPLICATED]

    w_rows = _pack_rows([w[n] for n in _SHARDED])
    gathered = all_gather(w_rows.astype(BF16), "weights_all_gather")
    full = _unshard(gathered, shard_shapes)

    W = {}
    for i in range(2):
        for n in ("ffn1_w_gu", "ffn1_w_down", "ffn2_w_gu", "ffn2_w_down", "ple_w_gate", "ple_w_proj"):
            W[f"{n}{i}"] = full[n][i]
        for n in ("ffn1_norm", "mix_norm", "ffn2_norm", "ple_norm"):
            W[f"{n}{i}"] = w[n][i].reshape(1, -1)
    for n in ("even_w_in", "even_w_out", "fox_w_out"):
        W[n] = full[n][0]
    W["fox_w_in"] = jnp.pad(full["fox_w_in"][0], ((0, 0), (0, FOX_IN_PAD - full["fox_w_in"].shape[2])))
    for n in ("rwkv_w2", "rwkv_a2", "rwkv_g2"):
        W[n] = full[n][0].astype(F32)
    for n in ("swa_sinks", "rwkv_mu", "rwkv_w0", "rwkv_a0", "rwkv_k_k", "rwkv_k_a", "rwkv_r_k", "rwkv_ln_w",
              "rwkv_ln_b", "final_norm"):
        W[n] = w[n].reshape(1, -1)
    n_f = fox_b_f.shape[1]
    W["fox_b_f"] = jnp.pad(fox_b_f.reshape(1, n_f), ((0, 0), (0, 128 - n_f)))

    loss_row, dx, G = device_step(x[0], p[:, 0], loss_target[0], W)

    gfull = {}
    for n in ("ffn1_w_gu", "ffn1_w_down", "ffn2_w_gu", "ffn2_w_down", "ple_w_gate", "ple_w_proj"):
        gfull[n] = jnp.stack([G[f"{n}0"], G[f"{n}1"]]).astype(BF16)
    for n in ("even_w_in", "even_w_out", "fox_w_out", "rwkv_w2", "rwkv_a2", "rwkv_g2"):
        gfull[n] = G[n][None].astype(BF16)
    gfull["fox_w_in"] = G["fox_w_in"][None, :, :fox_w_in.shape[2] * N_DEV].astype(BF16)
    parts = all_to_all(_to_shards(gfull, shard_shapes), "grads_all_to_all")
    g_rows, d_rows, nm_rows, nv_rows = adamw(
        w_rows, _pack_rows([m[n] for n in _SHARDED]), _pack_rows([v[n] for n in _SHARDED]), parts, "adamw_sharded")
    shapes = list(shard_shapes.values())
    out_g = dict(zip(_SHARDED, _unpack_rows(g_rows, shapes)))
    out_d = dict(zip(_SHARDED, _unpack_rows(d_rows, shapes)))
    out_m = dict(zip(_SHARDED, _unpack_rows(nm_rows, shapes)))
    out_v = dict(zip(_SHARDED, _unpack_rows(nv_rows, shapes)))

    gsmall = {}
    for n in ("ffn1_norm", "mix_norm", "ffn2_norm", "ple_norm"):
        gsmall[n] = jnp.concatenate([G[f"{n}0"], G[f"{n}1"]], axis=0)
    for n in ("swa_sinks", "rwkv_mu", "rwkv_w0", "rwkv_a0", "rwkv_k_k", "rwkv_k_a", "rwkv_r_k", "rwkv_ln_w",
              "rwkv_ln_b", "final_norm"):
        gsmall[n] = G[n]
    gsmall["fox_b_f"] = G["fox_b_f"][:, :n_f]
    small = _pack_small([gsmall[n] for n in _REPLICATED] + [loss_row[:, :1]])
    small_parts = all_gather(small, "small_all_gather")
    pad1 = lambda vals: _pack_small(vals + [jnp.zeros((1, 1), F32)])
    gs, ds, nms, nvs = adamw(pad1([w[n] for n in _REPLICATED]), pad1([m[n] for n in _REPLICATED]),
                             pad1([v[n] for n in _REPLICATED]), small_parts, "adamw_replicated")
    out_g.update(zip(_REPLICATED, _unpack_small(gs, small_shapes)))
    out_d.update(zip(_REPLICATED, _unpack_small(ds, small_shapes)))
    out_m.update(zip(_REPLICATED, _unpack_small(nms, small_shapes)))
    out_v.update(zip(_REPLICATED, _unpack_small(nvs, small_shapes)))
    n_small = sum(math.prod(s) for s in small_shapes)
    loss = gs[0, n_small]

    return (loss, dx[None], *[out_g[n] for n in _WEIGHTS], *[out_d[n] for n in _WEIGHTS],
            *[out_m[n] for n in _WEIGHTS], *[out_v[n] for n in _WEIGHTS])
```

```python
import functools
import math

import numpy as np
import jax
import jax.numpy as jnp
from jax import lax
from jax.experimental import pallas as pl
from jax.experimental.pallas import tpu as pltpu

F32 = jnp.float32
BF16 = jnp.bfloat16

D_MODEL = 1024
HEAD_DIM = 64
BLOCK = 128
SWA_HEADS = 8
SWA_KV_HEADS = 2
SWA_GROUP = 4
RWKV_HEADS = 8
RWKV_DIM = 512
FOX_HEADS = 16
FOX_DIM = 1024
D_FF = 2816
NORM_EPS = 1e-6
GN_EPS = 64e-5
L2_EPS = 1e-12
SWA_Q = 512
SWA_KV = 128
SWA_COLS = 768
FOX_IN_PAD = 3200
N_DEV = 8
ADAM_LR = 0.001
ADAM_B1 = 0.9
ADAM_B2 = 0.999
ADAM_EPS = 1e-08
ADAM_WD = 0.01
ADAM_STEP = 10

V7X_VMEM_LIMIT = 56 * 1024 * 1024
SCAN_GROUP = 8
SCAN_CHUNK = 32

_NN = (((1,), (0,)), ((), ()))
_NT = (((1,), (1,)), ((), ()))
_TN = (((0,), (0,)), ((), ()))
_DIMS = {"nn": _NN, "nt": _NT, "tn": _TN}


def _pick(n, target, mult=128):
    best = None
    for t in range(mult, min(n, target) + 1, mult):
        if n % t == 0:
            best = t
    return best or n


def _cparams(sem):
    return pltpu.CompilerParams(dimension_semantics=sem, vmem_limit_bytes=V7X_VMEM_LIMIT)


def _dot(a, b, dims):
    return lax.dot_general(a.astype(BF16), b.astype(BF16), dims, preferred_element_type=F32)


@jax.custom_vjp
def bdot(a, b):
    return _dot(a, b, _NN)


def _bdot_fwd(a, b):
    return _dot(a, b, _NN), (a, b)


def _bdot_bwd(res, g):
    a, b = res
    return _dot(g, b, _NT), _dot(a, g, _TN)


bdot.defvjp(_bdot_fwd, _bdot_bwd)


@jax.custom_vjp
def bdot_nt(a, b):
    return _dot(a, b, _NT)


def _bdot_nt_fwd(a, b):
    return _dot(a, b, _NT), (a, b)


def _bdot_nt_bwd(res, g):
    a, b = res
    return _dot(g, b, _NN), _dot(g, a, _TN)


bdot_nt.defvjp(_bdot_nt_fwd, _bdot_nt_bwd)


def _segsum(x, bd):
    return jnp.dot(x, bd, precision=lax.Precision.HIGHEST, preferred_element_type=F32)


def _sigmoid(x):
    return 1.0 / (1.0 + jnp.exp(-x))


def _softplus(x):
    return jnp.maximum(x, 0.0) + jnp.log(1.0 + jnp.exp(-jnp.abs(x)))


def matmul(a, b, mode, name, out_dtype=F32, scale=1.0, res=None, tm=512, tn=1408, tk=1024):
    if mode == "nn":
        (M, K), (K2, N) = a.shape, b.shape
    elif mode == "nt":
        (M, K), (N, K2) = a.shape, b.shape
    else:
        (K, M), (K2, N) = a.shape, b.shape
    assert K == K2, (a.shape, b.shape, mode)
    tm, tn, tk = _pick(M, tm), _pick(N, tn), _pick(K, tk)
    nk = K // tk
    has_res = res is not None

    def body(*refs):
        if has_res:
            a_ref, b_ref, r_ref, o_ref, acc = refs
        else:
            a_ref, b_ref, o_ref, acc = refs
        kk = pl.program_id(2)

        @pl.when(kk == 0)
        def _():
            acc[...] = jnp.zeros_like(acc)

        acc[...] += _dot(a_ref[...], b_ref[...], _DIMS[mode])

        @pl.when(kk == nk - 1)
        def _():
            v = acc[...]
            if scale != 1.0:
                v = v * scale
            if has_res:
                v = v + r_ref[...].astype(F32)
            o_ref[...] = v.astype(out_dtype)

    if mode == "tn":
        a_spec = pl.BlockSpec((tk, tm), lambda i, j, k: (k, i))
    else:
        a_spec = pl.BlockSpec((tm, tk), lambda i, j, k: (i, k))
    if mode == "nt":
        b_spec = pl.BlockSpec((tn, tk), lambda i, j, k: (j, k))
    else:
        b_spec = pl.BlockSpec((tk, tn), lambda i, j, k: (k, j))
    o_spec = pl.BlockSpec((tm, tn), lambda i, j, k: (i, j))
    in_specs = [a_spec, b_spec] + ([o_spec] if has_res else [])
    args = (a, b) + ((res,) if has_res else ())
    return pl.pallas_call(
        body,
        grid=(M // tm, N // tn, nk),
        in_specs=in_specs,
        out_specs=o_spec,
        out_shape=jax.ShapeDtypeStruct((M, N), out_dtype),
        scratch_shapes=[pltpu.VMEM((tm, tn), F32)],
        compiler_params=_cparams(("parallel", "parallel", "arbitrary")),
        name=name,
    )(*args)


def _row_spec(r, tm):
    if isinstance(r, tuple):
        arr, width, blk = r
        return arr, pl.BlockSpec((tm, width), lambda i, blk=blk: (i, blk))
    return r, pl.BlockSpec((tm, r.shape[1]), lambda i: (i, 0))


def _whole_spec(p):
    return pl.BlockSpec(p.shape, lambda i: (0,) * p.ndim)


def rowwise(fn, rows, params, outs, name, tm=256):
    arrs, specs = zip(*[_row_spec(r, tm) for r in rows])
    S = arrs[0].shape[0]
    tm = min(tm, S)
    arrs, specs = zip(*[_row_spec(r, tm) for r in rows])
    n_in = len(rows) + len(params)

    def body(*refs):
        res = fn(*[r[...] for r in refs[:n_in]])
        for o_ref, v in zip(refs[n_in:], res):
            o_ref[...] = v.astype(o_ref.dtype)

    return pl.pallas_call(
        body,
        grid=(S // tm,),
        in_specs=list(specs) + [_whole_spec(p) for p in params],
        out_specs=[pl.BlockSpec((tm, c), lambda i: (i, 0)) for c, _ in outs],
        out_shape=[jax.ShapeDtypeStruct((S, c), dt) for c, dt in outs],
        compiler_params=_cparams(("parallel",)),
        name=name,
    )(*arrs, *params)


def rowwise_vjp(fn, rows, params, cots, name, need=None, row_dtype=F32, consts=(), tm=256):
    nr, npar, nc, nk = len(rows), len(params), len(cots), len(consts)
    need = [True] * nr if need is None else need
    arrs, _ = zip(*[_row_spec(r, tm) for r in rows])
    S = arrs[0].shape[0]
    tm = min(tm, S)
    arrs, specs = zip(*[_row_spec(r, tm) for r in rows])
    carrs, cspecs = zip(*[_row_spec(c, tm) for c in cots])
    widths = [s.block_shape[1] for s in specs]
    n_in = nr + npar + nk + nc

    def body(*refs):
        i = pl.program_id(0)
        xs = [r[...].astype(F32) for r in refs[:nr]]
        ps = [r[...] for r in refs[nr:nr + npar]]
        ks = [r[...] for r in refs[nr + npar:nr + npar + nk]]
        cs = [r[...].astype(F32) for r in refs[nr + npar + nk:n_in]]
        outs, vjp = jax.vjp(lambda *a: fn(*a, *ks), *xs, *ps)
        grads = vjp(tuple(cs))
        o = n_in
        for j in range(nr):
            if need[j]:
                refs[o][...] = grads[j].astype(refs[o].dtype)
                o += 1
        for j in range(npar):
            g_ref = refs[o + j]

            @pl.when(i == 0)
            def _(g_ref=g_ref):
                g_ref[...] = jnp.zeros_like(g_ref)

            g_ref[...] += grads[nr + j]

    out_specs = [pl.BlockSpec((tm, w), lambda i: (i, 0)) for w, nd in zip(widths, need) if nd]
    out_shape = [jax.ShapeDtypeStruct((S, w), row_dtype) for w, nd in zip(widths, need) if nd]
    out_specs += [_whole_spec(p) for p in params]
    out_shape += [jax.ShapeDtypeStruct(p.shape, F32) for p in params]
    res = pl.pallas_call(
        body,
        grid=(S // tm,),
        in_specs=list(specs) + [_whole_spec(p) for p in params] + [_whole_spec(k) for k in consts] + list(cspecs),
        out_specs=out_specs,
        out_shape=out_shape,
        compiler_params=_cparams(("arbitrary",)),
        name=name,
    )(*arrs, *params, *consts, *carrs)
    nrow = sum(need)
    return list(res[:nrow]), list(res[nrow:])


def _rms(x, g):
    return x * lax.rsqrt(jnp.mean(x * x, axis=-1, keepdims=True) + NORM_EPS) * g


def _f_rms(x, g):
    return (_rms(x, g),)


def _f_act(g, u):
    return (g * _sigmoid(g) * u,)


def _f_ple(x, z, pp):
    return (x + _sigmoid(z) * pp,)


def _f_mix(h, sh, mu):
    return (h + (sh - h) * mu,)


def _f_logf(fz, bf):
    return (-_softplus(-(fz + bf)),)


def _f_rwkv_pre(hk, hw, ha, hg, w0, w2, a0, a2, g2, k_k, k_a, bd):
    wlog = -_softplus(-(w0 + bdot(jnp.tanh(hw), w2))) - 0.5
    a = _sigmoid(a0 + bdot(ha, a2))
    g = bdot(_sigmoid(hg), g2)
    kk = hk * k_k
    kk = kk / jnp.maximum(jnp.sqrt(_segsum(kk * kk, bd)), L2_EPS)
    k2 = hk * (1.0 + (a - 1.0) * k_a)
    decay = jnp.exp(-jnp.exp(wlog))
    return decay, k2, kk, kk * a, g


def _f_rwkv_post(y, r, k2, v, g, ln_w, ln_b, r_k, bd):
    mean = _segsum(y, bd) * (1.0 / HEAD_DIM)
    d = y - mean
    var = _segsum(d * d, bd) * (1.0 / HEAD_DIM)
    yn = d * lax.rsqrt(var + GN_EPS) * ln_w + ln_b
    yn = yn + _segsum(r * k2 * r_k, bd) * v
    return (yn * g,)


def loss_head(x, target, gf, tm=256):
    S, D = x.shape
    tm = min(tm, S)

    def f(xt, g, tt):
        err = _rms(xt, g) - tt
        return 0.5 * jnp.sum(err * err) * (1.0 / D)

    def body(x_ref, t_ref, g_ref, dx_ref, dg_ref, l_ref):
        i = pl.program_id(0)
        val, (dx, dg) = jax.value_and_grad(f, argnums=(0, 1))(x_ref[...], g_ref[...], t_ref[...])

        @pl.when(i == 0)
        def _():
            dg_ref[...] = jnp.zeros_like(dg_ref)
            l_ref[...] = jnp.zeros_like(l_ref)

        dx_ref[...] = dx
        dg_ref[...] += dg
        l_ref[...] += jnp.full(l_ref.shape, val, F32)

    row = pl.BlockSpec((tm, D), lambda i: (i, 0))
    vec = pl.BlockSpec((1, D), lambda i: (0, 0))
    return pl.pallas_call(
        body,
        grid=(S // tm,),
        in_specs=[row, row, vec],
        out_specs=[row, vec, pl.BlockSpec((1, 128), lambda i: (0, 0))],
        out_shape=[jax.ShapeDtypeStruct((S, D), F32), jax.ShapeDtypeStruct((1, D), F32),
                   jax.ShapeDtypeStruct((1, 128), F32)],
        compiler_params=_cparams(("arbitrary",)),
        name="loss_head",
    )(x, target, gf)


def _swa_block(q, kp, kc, vp, vc, sink, slope, n):
    k = jnp.concatenate([kp, kc], axis=0)
    v = jnp.concatenate([vp, vc], axis=0)
    rows = q.shape[0]
    logits = bdot_nt(q, k) * (HEAD_DIM ** -0.5)
    qi = lax.broadcasted_iota(jnp.int32, (rows, 2 * BLOCK), 0) & (BLOCK - 1)
    ki = lax.broadcasted_iota(jnp.int32, (rows, 2 * BLOCK), 1)
    dist = qi + BLOCK - ki
    valid = (dist >= 0) & (dist < BLOCK) & ((n - 1) * BLOCK + ki >= 0)
    logits = logits - slope * dist.astype(F32)
    logits = jnp.where(valid, logits, -jnp.inf)
    m = jnp.maximum(jnp.max(logits, axis=-1, keepdims=True), sink)
    pr = jnp.exp(logits - m)
    denom = jnp.sum(pr, axis=-1, keepdims=True) + jnp.exp(sink - m)
    return bdot(pr / denom, v)


def _swa_specs(S):
    nb = S // BLOCK
    q_spec = pl.BlockSpec((None, SWA_GROUP, BLOCK, HEAD_DIM), lambda h, n: (h, 0, n, 0))
    kc_spec = pl.BlockSpec((None, BLOCK, HEAD_DIM), lambda h, n: (h, n, 0))
    kp_spec = pl.BlockSpec((None, BLOCK, HEAD_DIM), lambda h, n: (h, jnp.maximum(n - 1, 0), 0))
    col_spec = pl.BlockSpec((None, SWA_GROUP * BLOCK, 1), lambda h, n: (h, 0, 0))
    return nb, q_spec, kp_spec, kc_spec, col_spec


def swa_fwd(q, k, v, sink_col, slope_col):
    S = q.shape[2]
    nb, q_spec, kp_spec, kc_spec, col_spec = _swa_specs(S)

    def body(q_ref, kp_ref, kc_ref, vp_ref, vc_ref, s_ref, a_ref, o_ref):
        n = pl.program_id(1)
        qq = q_ref[...].reshape(SWA_GROUP * BLOCK, HEAD_DIM)
        out = _swa_block(qq, kp_ref[...], kc_ref[...], vp_ref[...], vc_ref[...], s_ref[...], a_ref[...], n)
        o_ref[...] = out.reshape(SWA_GROUP, BLOCK, HEAD_DIM)

    return pl.pallas_call(
        body,
        grid=(SWA_KV_HEADS, nb),
        in_specs=[q_spec, kp_spec, kc_spec, kp_spec, kc_spec, col_spec, col_spec],
        out_specs=q_spec,
        out_shape=jax.ShapeDtypeStruct(q.shape, F32),
        compiler_params=_cparams(("parallel", "parallel")),
        name="swa_fwd",
    )(q, k, k, v, v, sink_col, slope_col)


def swa_bwd(q, k, v, sink_col, slope_col, dout):
    S = q.shape[2]
    nb, q_spec, kp_spec, kc_spec, col_spec = _swa_specs(S)

    def body(q_ref, kp_ref, kc_ref, vp_ref, vc_ref, s_ref, a_ref, do_ref,
             dq_ref, dkp_ref, dkc_ref, dvp_ref, dvc_ref, ds_ref):
        n = pl.program_id(1)
        qq = q_ref[...].reshape(SWA_GROUP * BLOCK, HEAD_DIM)
        slope = a_ref[...]
        f = lambda a, b, c, d, e, s: _swa_block(a, b, c, d, e, s, slope, n)
        _, vjp = jax.vjp(f, qq, kp_ref[...], kc_ref[...], vp_ref[...], vc_ref[...], s_ref[...])
        dq, dkp, dkc, dvp, dvc, ds = vjp(do_ref[...].reshape(SWA_GROUP * BLOCK, HEAD_DIM))
        dq_ref[...] = dq.reshape(SWA_GROUP, BLOCK, HEAD_DIM)
        dkp_ref[...] = dkp
        dkc_ref[...] = dkc
        dvp_ref[...] = dvp
        dvc_ref[...] = dvc

        @pl.when(n == 0)
        def _():
            ds_ref[...] = jnp.zeros_like(ds_ref)

        ds_ref[...] += ds

    kv_shape = jax.ShapeDtypeStruct(k.shape, F32)
    return pl.pallas_call(
        body,
        grid=(SWA_KV_HEADS, nb),
        in_specs=[q_spec, kp_spec, kc_spec, kp_spec, kc_spec, col_spec, col_spec, q_spec],
        out_specs=[q_spec, kc_spec, kc_spec, kc_spec, kc_spec, col_spec],
        out_shape=[jax.ShapeDtypeStruct(q.shape, F32), kv_shape, kv_shape, kv_shape, kv_shape,
                   jax.ShapeDtypeStruct(sink_col.shape, F32)],
        compiler_params=_cparams(("parallel", "arbitrary")),
        name="swa_bwd",
    )(q, k, k, v, v, sink_col, slope_col, dout)


def _split2(x):
    hi = x.astype(BF16)
    return (x - hi.astype(F32)).astype(BF16), hi


def _dot2(x, m):
    lo, hi = _split2(x)
    return jnp.dot(lo, m, preferred_element_type=F32) + jnp.dot(hi, m, preferred_element_type=F32)


def _seg_sum(x, bd):
    w = bd.shape[0]
    return jnp.concatenate([_dot2(x[:, i:i + w], bd) for i in range(0, x.shape[1], w)], axis=1)


def _scan_consts():
    r = np.arange(256)
    bd = (r[:, None] // HEAD_DIM == r[None, :] // HEAD_DIM).astype(np.float32)
    c = np.arange(RWKV_DIM)
    e = (np.arange(HEAD_DIM)[:, None] // SCAN_GROUP == c[None, :] // HEAD_DIM).astype(np.float32)
    diag = (np.arange(HEAD_DIM)[:, None] == c[None, :] % HEAD_DIM).astype(np.float32)
    return jnp.asarray(bd, BF16), jnp.asarray(e, BF16), jnp.asarray(diag, F32)


def _to_colblocks(a):
    S = a.shape[0]
    a = a.reshape(S // SCAN_GROUP, SCAN_GROUP, RWKV_HEADS, HEAD_DIM)
    return a.transpose(0, 3, 2, 1).reshape(S // SCAN_GROUP, HEAD_DIM, RWKV_HEADS * SCAN_GROUP)


def _scan_step(St, tt, base, col_g, lane_t, kk_ref, w_ref, b_ref, k_ref, bd, e):
    row = lambda ref: ref[pl.ds(base + tt, 1), :]
    u_b = _seg_sum(St * row(kk_ref), bd)
    v_b = _dot2(jnp.where(lane_t == tt, col_g, 0.0), e)
    return St * row(w_ref) - u_b * row(b_ref) + v_b * row(k_ref), u_b, v_b


def rwkv_scan_fwd(r, w, k, kk, b, vB, gather_src):
    S, C = r.shape
    N, G = HEAD_DIM, SCAN_GROUP
    chunk = min(SCAN_CHUNK, S)
    nchunk, ng = S // chunk, chunk // G
    bd, e, diag = _scan_consts()

    def body(r_ref, w_ref, k_ref, kk_ref, b_ref, vB_ref, bd_ref, e_ref, dg_ref, xs_ref, y_ref, ck_ref, xr_ref,
             S_ref, send_sems, recv_sems, local_sem):
        c = pl.program_id(0)
        _exchange_during(c, nchunk, True, xs_ref, xr_ref, send_sems, recv_sems, local_sem)

        @pl.when(c == 0)
        def _():
            S_ref[...] = jnp.zeros_like(S_ref)

        ck_ref[...] = S_ref[...]
        sub = lax.broadcasted_iota(jnp.int32, (G, C), 0)
        lane_t = lax.broadcasted_iota(jnp.int32, (N, N), 1) & (G - 1)

        def group(g, St):
            base = pl.multiple_of(g * G, G)
            vb = vB_ref[g]
            ys = jnp.zeros((G, C), F32)
            for tt in range(G):
                St, _, _ = _scan_step(St, tt, base, vb, lane_t, kk_ref, w_ref, b_ref, k_ref, bd_ref[...], e_ref[...])
                y_b = _seg_sum(St * r_ref[pl.ds(base + tt, 1), :], bd_ref[...])
                ys = jnp.where(sub == tt, jnp.sum(y_b * dg_ref[...], axis=0, keepdims=True), ys)
            y_ref[pl.ds(base, G), :] = ys
            return St

        S_ref[...] = lax.fori_loop(0, ng, group, S_ref[...])

    row = pl.BlockSpec((chunk, C), lambda c: (c, 0))
    col = pl.BlockSpec((ng, N, N), lambda c: (c, 0, 0))
    return pl.pallas_call(
        body,
        grid=(nchunk,),
        in_specs=[row] * 5 + [col, _whole_spec(bd), _whole_spec(e), _whole_spec(diag), _ANY],
        out_specs=[row, pl.BlockSpec((None, N, C), lambda c: (c, 0, 0)), _ANY],
        out_shape=[jax.ShapeDtypeStruct((S, C), F32), jax.ShapeDtypeStruct((nchunk, N, C), F32),
                   _exchange_out_shape(True, gather_src)],
        scratch_shapes=[pltpu.VMEM((N, C), F32)] + _EXCHANGE_SEMS,
        compiler_params=_cparams(("arbitrary",)),
        name="rwkv_scan_fwd",
    )(r, w, k, kk, b, vB, bd, e, diag, gather_src)


def rwkv_scan_bwd(r, w, k, kk, b, vB, dyB, ckpt, scatter_src):
    S, C = r.shape
    N, G = HEAD_DIM, SCAN_GROUP
    chunk = min(SCAN_CHUNK, S)
    nchunk, ng = S // chunk, chunk // G
    bd, e, diag = _scan_consts()

    def body(r_ref, w_ref, k_ref, kk_ref, b_ref, vB_ref, dyB_ref, ck_ref, bd_ref, e_ref, dg_ref, xs_ref,
             dr_ref, dw_ref, dk_ref, dkk_ref, db_ref, dv_ref, xr_ref, G_ref, sbuf, ubuf, vbuf,
             send_sems, recv_sems, local_sem):
        c = pl.program_id(0)
        _exchange_during(c, nchunk, False, xs_ref, xr_ref, send_sems, recv_sems, local_sem)

        @pl.when(c == 0)
        def _():
            G_ref[...] = jnp.zeros_like(G_ref)

        lane_t = lax.broadcasted_iota(jnp.int32, (N, N), 1) & (G - 1)
        sub = lax.broadcasted_iota(jnp.int32, (G, C), 0)

        def fgroup(g, St):
            base = pl.multiple_of(g * G, G)
            vb = vB_ref[g]
            for tt in range(G):
                sbuf[base + tt] = St
                St, u_b, v_b = _scan_step(St, tt, base, vb, lane_t, kk_ref, w_ref, b_ref, k_ref, bd_ref[...],
                                          e_ref[...])
                ubuf[base + tt] = u_b
                vbuf[base + tt] = v_b
            return St

        sbuf[chunk] = lax.fori_loop(0, ng, fgroup, ck_ref[...])

        def bgroup(gi, Gt):
            g = ng - 1 - gi
            base = pl.multiple_of(g * G, G)
            dyb = dyB_ref[g]
            rows = [jnp.zeros((G, C), F32) for _ in range(6)]
            colsum = lambda a: jnp.sum(a, axis=0, keepdims=True)
            for tt in reversed(range(G)):
                row = lambda ref: ref[pl.ds(base + tt, 1), :]
                Sp, Sc = sbuf[base + tt], sbuf[base + tt + 1]
                u_b, v_b = ubuf[base + tt], vbuf[base + tt]
                dy_b = _dot2(jnp.where(lane_t == tt, dyb, 0.0), e_ref[...])
                Gt = Gt + dy_b * row(r_ref)
                d_r = colsum(Sc * dy_b)
                d_w = colsum(Gt * Sp)
                du_b = -_seg_sum(Gt * row(b_ref), bd_ref[...])
                d_b = -colsum(Gt * u_b)
                d_v = colsum(_seg_sum(Gt * row(k_ref), bd_ref[...]) * dg_ref[...])
                d_k = colsum(Gt * v_b)
                d_kk = colsum(Sp * du_b)
                Gt = Gt * row(w_ref) + du_b * row(kk_ref)
                rows = [jnp.where(sub == tt, new, acc)
                        for new, acc in zip((d_r, d_w, d_k, d_kk, d_b, d_v), rows)]
            for ref, val in zip((dr_ref, dw_ref, dk_ref, dkk_ref, db_ref, dv_ref), rows):
                ref[pl.ds(base, G), :] = val
            return Gt

        G_ref[...] = lax.fori_loop(0, ng, bgroup, G_ref[...])

    rev = lambda c: nchunk - 1 - c
    row = pl.BlockSpec((chunk, C), lambda c: (rev(c), 0))
    col = pl.BlockSpec((ng, N, N), lambda c: (rev(c), 0, 0))
    rshape = jax.ShapeDtypeStruct((S, C), F32)
    return pl.pallas_call(
        body,
        grid=(nchunk,),
        in_specs=[row] * 5 + [col, col, pl.BlockSpec((None, N, C), lambda c: (rev(c), 0, 0)),
                              _whole_spec(bd), _whole_spec(e), _whole_spec(diag), _ANY],
        out_specs=[row] * 6 + [_ANY],
        out_shape=[rshape] * 6 + [_exchange_out_shape(False, scatter_src)],
        scratch_shapes=[pltpu.VMEM((N, C), F32), pltpu.VMEM((chunk + 1, N, C), F32),
                        pltpu.VMEM((chunk, N, C), F32), pltpu.VMEM((chunk, N, C), F32)] + _EXCHANGE_SEMS,
        compiler_params=_cparams(("arbitrary",)),
        name="rwkv_scan_bwd",
    )(r, w, k, kk, b, vB, dyB, ckpt, bd, e, diag, scatter_src)


def seq_cumsum(x, reverse, name):
    S, C = x.shape
    tb = min(256, S)
    nb = S // tb

    def body(x_ref, o_ref, carry):
        i = pl.program_id(0)

        @pl.when(i == 0)
        def _():
            carry[...] = jnp.zeros_like(carry)

        ri = lax.broadcasted_iota(jnp.int32, (tb, tb), 0)
        ci = lax.broadcasted_iota(jnp.int32, (tb, tb), 1)
        tri = jnp.where((ci >= ri) if reverse else (ci <= ri), 1.0, 0.0).astype(F32)
        xb = x_ref[...]
        out = jnp.dot(tri, xb, precision=lax.Precision.HIGHEST, preferred_element_type=F32) + carry[...]
        o_ref[...] = out
        carry[...] = carry[...] + jnp.sum(xb, axis=0, keepdims=True)

    idx = (lambda i: (nb - 1 - i, 0)) if reverse else (lambda i: (i, 0))
    return pl.pallas_call(
        body,
        grid=(nb,),
        in_specs=[pl.BlockSpec((tb, C), idx)],
        out_specs=pl.BlockSpec((tb, C), idx),
        out_shape=jax.ShapeDtypeStruct((S, C), F32),
        scratch_shapes=[pltpu.VMEM((1, C), F32)],
        compiler_params=_cparams(("arbitrary",)),
        name=name,
    )(x)


def _fox_logits(q, k, cq, ck, qi, ki, tq, tk):
    s = _dot(q, k, _NT) * (HEAD_DIM ** -0.5) + cq - ck
    row = qi * tq + lax.broadcasted_iota(jnp.int32, (tq, tk), 0)
    col = ki * tk + lax.broadcasted_iota(jnp.int32, (tq, tk), 1)
    return jnp.where(col <= row, s, -jnp.inf)


def fox_fwd(q, k, v, c_col, c_row):
    Hh, S, Dh = q.shape
    tq = tk = min(512, S)
    nq, nk = S // tq, S // tk

    def body(q_ref, k_ref, v_ref, cq_ref, ck_ref, o_ref, lse_ref, m_s, l_s, acc_s):
        qi, ki = pl.program_id(1), pl.program_id(2)

        @pl.when(ki == 0)
        def _():
            m_s[...] = jnp.full_like(m_s, -jnp.inf)
            l_s[...] = jnp.zeros_like(l_s)
            acc_s[...] = jnp.zeros_like(acc_s)

        @pl.when(ki <= qi)
        def _():
            s = _fox_logits(q_ref[...], k_ref[...], cq_ref[...], ck_ref[...], qi, ki, tq, tk)
            m_new = jnp.maximum(m_s[...], jnp.max(s, axis=-1, keepdims=True))
            alpha = jnp.exp(m_s[...] - m_new)
            p = jnp.exp(s - m_new)
            l_s[...] = alpha * l_s[...] + jnp.sum(p, axis=-1, keepdims=True)
            acc_s[...] = alpha * acc_s[...] + _dot(p, v_ref[...], _NN)
            m_s[...] = m_new

        @pl.when(ki == nk - 1)
        def _():
            o_ref[...] = acc_s[...] / l_s[...]
            lse_ref[...] = m_s[...] + jnp.log(l_s[...])

    qs = pl.BlockSpec((None, tq, Dh), lambda h, i, j: (h, i, 0))
    ks = pl.BlockSpec((None, tk, Dh), lambda h, i, j: (h, jnp.minimum(i, j), 0))
    cqs = pl.BlockSpec((None, tq, 1), lambda h, i, j: (h, i, 0))
    cks = pl.BlockSpec((None, 1, tk), lambda h, i, j: (h, 0, jnp.minimum(i, j)))
    return pl.pallas_call(
        body,
        grid=(Hh, nq, nk),
        in_specs=[qs, ks, ks, cqs, cks],
        out_specs=[qs, cqs],
        out_shape=[jax.ShapeDtypeStruct((Hh, S, Dh), F32), jax.ShapeDtypeStruct((Hh, S, 1), F32)],
        scratch_shapes=[pltpu.VMEM((tq, 1), F32), pltpu.VMEM((tq, 1), F32), pltpu.VMEM((tq, Dh), F32)],
        compiler_params=_cparams(("parallel", "parallel", "arbitrary")),
        name="fox_fwd",
    )(q, k, v, c_col, c_row)


def fox_bwd_dq(q, k, v, c_col, c_row, o, lse, do):
    Hh, S, Dh = q.shape
    tq = tk = min(512, S)
    nq, nk = S // tq, S // tk

    def body(q_ref, k_ref, v_ref, cq_ref, ck_ref, o_ref, lse_ref, do_ref, dq_ref, dr_ref, acc_s, row_s):
        qi, ki = pl.program_id(1), pl.program_id(2)

        @pl.when(ki == 0)
        def _():
            acc_s[...] = jnp.zeros_like(acc_s)
            row_s[...] = jnp.zeros_like(row_s)

        @pl.when(ki <= qi)
        def _():
            s = _fox_logits(q_ref[...], k_ref[...], cq_ref[...], ck_ref[...], qi, ki, tq, tk)
            p = jnp.exp(s - lse_ref[...])
            do_t = do_ref[...]
            delta = jnp.sum(do_t * o_ref[...], axis=-1, keepdims=True)
            dp = _dot(do_t, v_ref[...], _NT)
            ds = p * (dp - delta)
            acc_s[...] += _dot(ds, k_ref[...], _NN)
            row_s[...] += jnp.sum(ds, axis=-1, keepdims=True)

        @pl.when(ki == nk - 1)
        def _():
            dq_ref[...] = acc_s[...] * (HEAD_DIM ** -0.5)
            dr_ref[...] = row_s[...]

    qs = pl.BlockSpec((None, tq, Dh), lambda h, i, j: (h, i, 0))
    ks = pl.BlockSpec((None, tk, Dh), lambda h, i, j: (h, jnp.minimum(i, j), 0))
    cqs = pl.BlockSpec((None, tq, 1), lambda h, i, j: (h, i, 0))
    cks = pl.BlockSpec((None, 1, tk), lambda h, i, j: (h, 0, jnp.minimum(i, j)))
    return pl.pallas_call(
        body,
        grid=(Hh, nq, nk),
        in_specs=[qs, ks, ks, cqs, cks, qs, cqs, qs],
        out_specs=[qs, cqs],
        out_shape=[jax.ShapeDtypeStruct((Hh, S, Dh), F32), jax.ShapeDtypeStruct((Hh, S, 1), F32)],
        scratch_shapes=[pltpu.VMEM((tq, Dh), F32), pltpu.VMEM((tq, 1), F32)],
        compiler_params=_cparams(("parallel", "parallel", "arbitrary")),
        name="fox_bwd_dq",
    )(q, k, v, c_col, c_row, o, lse, do)


def fox_bwd_dkv(q, k, v, c_col, c_row, o, lse, do):
    Hh, S, Dh = q.shape
    tq = tk = min(512, S)
    nq, nk = S // tq, S // tk

    def body(q_ref, k_ref, v_ref, cq_ref, ck_ref, o_ref, lse_ref, do_ref, dk_ref, dv_ref, dc_ref, dk_s, dv_s, dc_s):
        ki, qi = pl.program_id(1), pl.program_id(2)

        @pl.when(qi == 0)
        def _():
            dk_s[...] = jnp.zeros_like(dk_s)
            dv_s[...] = jnp.zeros_like(dv_s)
            dc_s[...] = jnp.zeros_like(dc_s)

        @pl.when(qi >= ki)
        def _():
            s = _fox_logits(q_ref[...], k_ref[...], cq_ref[...], ck_ref[...], qi, ki, tq, tk)
            p = jnp.exp(s - lse_ref[...])
            do_t = do_ref[...]
            delta = jnp.sum(do_t * o_ref[...], axis=-1, keepdims=True)
            dp = _dot(do_t, v_ref[...], _NT)
            ds = p * (dp - delta)
            dv_s[...] += _dot(p, do_t, _TN)
            dk_s[...] += _dot(ds, q_ref[...], _TN)
            dc_s[...] += jnp.sum(ds, axis=0, keepdims=True)

        @pl.when(qi == nq - 1)
        def _():
            dk_ref[...] = dk_s[...] * (HEAD_DIM ** -0.5)
            dv_ref[...] = dv_s[...]
            dc_ref[...] = dc_s[...]

    qs = pl.BlockSpec((None, tq, Dh), lambda h, j, i: (h, jnp.maximum(i, j), 0))
    ks = pl.BlockSpec((None, tk, Dh), lambda h, j, i: (h, j, 0))
    cqs = pl.BlockSpec((None, tq, 1), lambda h, j, i: (h, jnp.maximum(i, j), 0))
    cks = pl.BlockSpec((None, 1, tk), lambda h, j, i: (h, 0, j))
    return pl.pallas_call(
        body,
        grid=(Hh, nk, nq),
        in_specs=[qs, ks, ks, cqs, cks, qs, cqs, qs],
        out_specs=[ks, ks, cks],
        out_shape=[jax.ShapeDtypeStruct((Hh, S, Dh), F32), jax.ShapeDtypeStruct((Hh, S, Dh), F32),
                   jax.ShapeDtypeStruct((Hh, 1, S), F32)],
        scratch_shapes=[pltpu.VMEM((tk, Dh), F32), pltpu.VMEM((tk, Dh), F32), pltpu.VMEM((1, tk), F32)],
        compiler_params=_cparams(("parallel", "parallel", "arbitrary")),
        name="fox_bwd_dkv",
    )(q, k, v, c_col, c_row, o, lse, do)


def _heads(a, nh):
    S = a.shape[0]
    return a.reshape(S, nh, HEAD_DIM).transpose(1, 0, 2)


def _unheads(a):
    nh, S, _ = a.shape
    return a.transpose(1, 0, 2).reshape(S, nh * HEAD_DIM)


def _shift_down(a):
    return jnp.pad(a[:-1], ((1, 0), (0, 0)))


def _shift_up(a):
    return jnp.pad(a[1:], ((0, 1), (0, 0)))


def _block_diag_ones():
    i = np.arange(RWKV_DIM) // HEAD_DIM
    return jnp.asarray((i[:, None] == i[None, :]).astype(np.float32))


def ffn_fwd(x, g_norm, w_gu, w_down, tag):
    hn, = rowwise(_f_rms, [x], [g_norm], [(D_MODEL, BF16)], tag + "_rms")
    gu = matmul(hn, w_gu, "nn", tag + "_gu")
    act, = rowwise(_f_act, [(gu, D_FF, 0), (gu, D_FF, 1)], [], [(D_FF, BF16)], tag + "_act", tm=128)
    out = matmul(act, w_down, "nn", tag + "_down", scale=0.5, res=x)
    return out, (x, hn, gu, act)


def ffn_bwd(dy, saved, g_norm, w_gu, w_down, tag):
    x, hn, gu, act = saved
    dact = matmul(dy, w_down, "nt", tag + "_dact", scale=0.5)
    d_wdown = matmul(act, dy, "tn", tag + "_dwd", out_dtype=BF16, scale=0.5)
    (dg, du), _ = rowwise_vjp(_f_act, [(gu, D_FF, 0), (gu, D_FF, 1)], [], [dact], tag + "_dactf",
                              row_dtype=BF16, tm=128)
    dgu = jnp.concatenate([dg, du], axis=1)
    d_wgu = matmul(hn, dgu, "tn", tag + "_dwgu", out_dtype=BF16)
    dhn = matmul(dgu, w_gu, "nt", tag + "_dhn")
    (dx,), (dgn,) = rowwise_vjp(_f_rms, [x], [g_norm], [dhn], tag + "_drms")
    return dy + dx, dgn, d_wgu, d_wdown


def ple_fwd(x, p_i, g_norm, w_gate, w_proj, tag):
    hn, = rowwise(_f_rms, [x], [g_norm], [(D_MODEL, BF16)], tag + "_rms")
    z = matmul(hn, w_gate, "nn", tag + "_gate")
    pp = matmul(p_i, w_proj, "nn", tag + "_proj")
    out, = rowwise(_f_ple, [x, z, pp], [], [(D_MODEL, F32)], tag + "_mix")
    return out, (x, hn, z, pp)


def ple_bwd(dy, saved, p_i, g_norm, w_gate, tag):
    x, hn, z, pp = saved
    (dz, dpp), _ = rowwise_vjp(_f_ple, [x, z, pp], [], [dy], tag + "_dmix", need=[False, True, True],
                               row_dtype=BF16)
    d_wproj = matmul(p_i, dpp, "tn", tag + "_dwp", out_dtype=BF16)
    d_wgate = matmul(hn, dz, "tn", tag + "_dwg", out_dtype=BF16)
    dhn = matmul(dz, w_gate, "nt", tag + "_dhn")
    (dx,), (dgn,) = rowwise_vjp(_f_rms, [x], [g_norm], [dhn], tag + "_drms")
    return dy + dx, dgn, d_wgate, d_wproj


def _swa_consts(sinks):
    slopes = np.asarray([2.0 ** (-(i + 1)) for i in range(SWA_HEADS)], np.float32)
    slope_col = jnp.asarray(np.repeat(slopes, BLOCK).reshape(SWA_KV_HEADS, SWA_GROUP * BLOCK, 1))
    sink_col = jnp.repeat(sinks.reshape(SWA_HEADS), BLOCK).reshape(SWA_KV_HEADS, SWA_GROUP * BLOCK, 1)
    return sink_col, slope_col


def even_mix_fwd(x, W, gather_src):
    S = x.shape[0]
    hn, = rowwise(_f_rms, [x], [W["mix_norm0"]], [(D_MODEL, BF16)], "emix_rms")
    proj = matmul(hn, W["even_w_in"], "nn", "emix_in")
    qa = _heads(proj[:, :SWA_Q], SWA_HEADS).reshape(SWA_KV_HEADS, SWA_GROUP, S, HEAD_DIM)
    ka = _heads(proj[:, SWA_Q:SWA_Q + SWA_KV], SWA_KV_HEADS)
    va = _heads(proj[:, SWA_Q + SWA_KV:SWA_COLS], SWA_KV_HEADS)
    sink_col, slope_col = _swa_consts(W["swa_sinks"])
    ya = swa_fwd(qa, ka, va, sink_col, slope_col)
    ya = _unheads(ya.reshape(SWA_HEADS, S, HEAD_DIM))
    hb = proj[:, SWA_COLS:]
    h, = rowwise(_f_mix, [hb, _shift_down(hb)], [W["rwkv_mu"]], [(hb.shape[1], F32)], "rwkv_shift")
    hr, hk, hv = h[:, :512], h[:, 512:1024], h[:, 1024:1536]
    hw, ha, hg = h[:, 1536:1600], h[:, 1600:1664], h[:, 1664:1792]
    bd = _block_diag_ones()
    pre_params = [W["rwkv_w0"], W["rwkv_w2"], W["rwkv_a0"], W["rwkv_a2"], W["rwkv_g2"], W["rwkv_k_k"],
                  W["rwkv_k_a"]]
    decay, k2, kk, b, g = rowwise(_f_rwkv_pre, [hk, hw, ha, hg], pre_params + [bd],
                                  [(RWKV_DIM, F32)] * 5, "rwkv_pre")
    vT = _to_colblocks(hv)
    y, ckpt, gathered = rwkv_scan_fwd(hr, decay, k2, kk, b, vT, gather_src)
    post_params = [W["rwkv_ln_w"], W["rwkv_ln_b"], W["rwkv_r_k"]]
    yb, = rowwise(_f_rwkv_post, [y, hr, k2, hv, g], post_params + [bd], [(RWKV_DIM, F32)], "rwkv_post")
    cat = jnp.concatenate([ya, yb], axis=1).astype(BF16)
    out = matmul(cat, W["even_w_out"], "nn", "emix_out", res=x)
    saved = (x, hn, qa, ka, va, sink_col, slope_col, hb, hr, hk, hv, hw, ha, hg, decay, k2, kk, b, g, vT,
             ckpt, y, cat)
    return out, saved, gathered


def even_mix_bwd(dy, saved, W, scatter_src):
    (x, hn, qa, ka, va, sink_col, slope_col, hb, hr, hk, hv, hw, ha, hg, decay, k2, kk, b, g, vT, ckpt, y,
     cat) = saved
    S = x.shape[0]
    grads = {}
    dcat = matmul(dy, W["even_w_out"], "nt", "emix_dcat")
    grads["even_w_out"] = matmul(cat, dy, "tn", "emix_dwout", out_dtype=BF16)
    dya, dyb = dcat[:, :SWA_Q], dcat[:, SWA_Q:]
    dya_h = _heads(dya, SWA_HEADS).reshape(SWA_KV_HEADS, SWA_GROUP, S, HEAD_DIM)
    dqa, dkp, dkc, dvp, dvc, dsink = swa_bwd(qa, ka, va, sink_col, slope_col, dya_h)
    shift_blk = lambda a: jnp.pad(a[:, BLOCK:], ((0, 0), (0, BLOCK), (0, 0)))
    dka = dkc + shift_blk(dkp)
    dva = dvc + shift_blk(dvp)
    grads["swa_sinks"] = dsink.reshape(SWA_HEADS, BLOCK).sum(axis=1).reshape(1, SWA_HEADS)
    dqa = _unheads(dqa.reshape(SWA_HEADS, S, HEAD_DIM))
    dka, dva = _unheads(dka), _unheads(dva)
    bd = _block_diag_ones()
    post_params = [W["rwkv_ln_w"], W["rwkv_ln_b"], W["rwkv_r_k"]]
    (d_y, d_r1, d_k2a, d_v1, d_g), (d_lnw, d_lnb, d_rk) = rowwise_vjp(
        _f_rwkv_post, [y, hr, k2, hv, g], post_params, [dyb], "rwkv_dpost", consts=[bd], tm=128)
    grads["rwkv_ln_w"], grads["rwkv_ln_b"], grads["rwkv_r_k"] = d_lnw, d_lnb, d_rk
    d_r2, d_w, d_k2b, d_kk, d_b, d_v2, exchanged = rwkv_scan_bwd(hr, decay, k2, kk, b, vT, _to_colblocks(d_y), ckpt,
                                                                  scatter_src)
    pre_params = [W["rwkv_w0"], W["rwkv_w2"], W["rwkv_a0"], W["rwkv_a2"], W["rwkv_g2"], W["rwkv_k_k"],
                  W["rwkv_k_a"]]
    (d_hk, d_hw, d_ha, d_hg), dpre = rowwise_vjp(
        _f_rwkv_pre, [hk, hw, ha, hg], pre_params, [d_w, d_k2a + d_k2b, d_kk, d_b, d_g], "rwkv_dpre",
        consts=[bd], tm=128)
    for nm, gval in zip(["rwkv_w0", "rwkv_w2", "rwkv_a0", "rwkv_a2", "rwkv_g2", "rwkv_k_k", "rwkv_k_a"], dpre):
        grads[nm] = gval
    d_h = jnp.concatenate([d_r1 + d_r2, d_hk, d_v1 + d_v2, d_hw, d_ha, d_hg], axis=1)
    (d_hb, d_sh), (d_mu,) = rowwise_vjp(_f_mix, [hb, _shift_down(hb)], [W["rwkv_mu"]], [d_h], "rwkv_dshift")
    grads["rwkv_mu"] = d_mu
    d_hb = d_hb + _shift_up(d_sh)
    dproj = jnp.concatenate([dqa, dka, dva, d_hb], axis=1).astype(BF16)
    grads["even_w_in"] = matmul(hn, dproj, "tn", "emix_dwin", out_dtype=BF16)
    dhn = matmul(dproj, W["even_w_in"], "nt", "emix_dhn")
    (dx,), (dgn,) = rowwise_vjp(_f_rms, [x], [W["mix_norm0"]], [dhn], "emix_drms")
    grads["mix_norm0"] = dgn
    return dy + dx, grads, exchanged


def odd_mix_fwd(x, W):
    S = x.shape[0]
    hn, = rowwise(_f_rms, [x], [W["mix_norm1"]], [(D_MODEL, BF16)], "omix_rms")
    proj = matmul(hn, W["fox_w_in"], "nn", "omix_in")
    q = _heads(proj[:, :FOX_DIM], FOX_HEADS).astype(BF16)
    k = _heads(proj[:, FOX_DIM:2 * FOX_DIM], FOX_HEADS).astype(BF16)
    v = _heads(proj[:, 2 * FOX_DIM:3 * FOX_DIM], FOX_HEADS).astype(BF16)
    fz = proj[:, 3 * FOX_DIM:]
    logf, = rowwise(_f_logf, [fz], [W["fox_b_f"]], [(128, F32)], "fox_logf")
    c = seq_cumsum(logf, False, "fox_cumsum")[:, :FOX_HEADS]
    c_col = c.T.reshape(FOX_HEADS, S, 1)
    c_row = c.T.reshape(FOX_HEADS, 1, S)
    o, lse = fox_fwd(q, k, v, c_col, c_row)
    yc = _unheads(o).astype(BF16)
    out = matmul(yc, W["fox_w_out"], "nn", "omix_out", res=x)
    return out, (x, hn, q, k, v, fz, c_col, c_row, o, lse, yc)


def odd_mix_bwd(dy, saved, W):
    x, hn, q, k, v, fz, c_col, c_row, o, lse, yc = saved
    S = x.shape[0]
    grads = {}
    dyc = matmul(dy, W["fox_w_out"], "nt", "omix_dyc")
    grads["fox_w_out"] = matmul(yc, dy, "tn", "omix_dwout", out_dtype=BF16)
    do = _heads(dyc, FOX_HEADS)
    dq, drow = fox_bwd_dq(q, k, v, c_col, c_row, o, lse, do)
    dk, dv, dcol = fox_bwd_dkv(q, k, v, c_col, c_row, o, lse, do)
    dc = (drow.reshape(FOX_HEADS, S) - dcol.reshape(FOX_HEADS, S)).T
    dc = jnp.pad(dc, ((0, 0), (0, 128 - FOX_HEADS)))
    dlogf = seq_cumsum(dc, True, "fox_rcumsum")
    (dfz,), (dbf,) = rowwise_vjp(_f_logf, [fz], [W["fox_b_f"]], [dlogf], "fox_dlogf")
    grads["fox_b_f"] = dbf
    dproj = jnp.concatenate([_unheads(dq), _unheads(dk), _unheads(dv), dfz], axis=1).astype(BF16)
    grads["fox_w_in"] = matmul(hn, dproj, "tn", "omix_dwin", out_dtype=BF16)
    dhn = matmul(dproj, W["fox_w_in"], "nt", "omix_dhn")
    (dx,), (dgn,) = rowwise_vjp(_f_rms, [x], [W["mix_norm1"]], [dhn], "omix_drms")
    grads["mix_norm1"] = dgn
    return dy + dx, grads


def device_step(x, p, target, W, gather_src, layer1_weights, layer1_grads):
    W = dict(W)
    saved = []
    h = x
    for i in range(2):
        h, s1 = ffn_fwd(h, W[f"ffn1_norm{i}"], W[f"ffn1_w_gu{i}"], W[f"ffn1_w_down{i}"], f"ffn1_{i}")
        if i == 0:
            h, s2, gathered = even_mix_fwd(h, W, gather_src)
            W.update(layer1_weights(gathered))
        else:
            h, s2 = odd_mix_fwd(h, W)
        h, s3 = ffn_fwd(h, W[f"ffn2_norm{i}"], W[f"ffn2_w_gu{i}"], W[f"ffn2_w_down{i}"], f"ffn2_{i}")
        h, s4 = ple_fwd(h, p[i], W[f"ple_norm{i}"], W[f"ple_w_gate{i}"], W[f"ple_w_proj{i}"], f"ple_{i}")
        saved.append((s1, s2, s3, s4))
    dh, d_final, loss = loss_head(h, target, W["final_norm"])
    G = {"final_norm": d_final}
    for i in (1, 0):
        s1, s2, s3, s4 = saved[i]
        dh, G[f"ple_norm{i}"], G[f"ple_w_gate{i}"], G[f"ple_w_proj{i}"] = ple_bwd(
            dh, s4, p[i], W[f"ple_norm{i}"], W[f"ple_w_gate{i}"], f"ple_{i}")
        dh, G[f"ffn2_norm{i}"], G[f"ffn2_w_gu{i}"], G[f"ffn2_w_down{i}"] = ffn_bwd(
            dh, s3, W[f"ffn2_norm{i}"], W[f"ffn2_w_gu{i}"], W[f"ffn2_w_down{i}"], f"ffn2_{i}")
        if i == 0:
            dh, gm, exchanged = even_mix_bwd(dh, s2, W, layer1_grads(G))
        else:
            dh, gm = odd_mix_bwd(dh, s2, W)
        G.update(gm)
        dh, G[f"ffn1_norm{i}"], G[f"ffn1_w_gu{i}"], G[f"ffn1_w_down{i}"] = ffn_bwd(
            dh, s1, W[f"ffn1_norm{i}"], W[f"ffn1_w_gu{i}"], W[f"ffn1_w_down{i}"], f"ffn1_{i}")
    return loss, dh, G, exchanged


_MESH = pl.DeviceIdType.MESH
_ANY = pl.BlockSpec(memory_space=pl.ANY)


def all_gather(x, name):
    def body(x_ref, out_ref, send_sems, recv_sems, local_sem):
        x_, y_, c_ = lax.axis_index("x"), lax.axis_index("y"), lax.axis_index("c")
        me, sibling = (x_, y_, c_), (x_, y_, 1 - c_)
        chips = [(1 - x_, y_), (x_, 1 - y_), (1 - x_, 1 - y_)]

        def slot(px, py, pc):
            return out_ref.at[4 * px + 2 * py + pc]

        def copy(k, block, to, src=None):
            return pltpu.make_async_remote_copy(
                src_ref=slot(*block) if src is None else src, dst_ref=slot(*block),
                send_sem=send_sems.at[k], recv_sem=recv_sems.at[k], device_id=to, device_id_type=_MESH)

        mine = pltpu.make_async_copy(x_ref, slot(*me), local_sem)
        mine.start()
        first = [copy(0, me, sibling, src=x_ref)]
        first += [copy(1 + j, me, (*chip, c_), src=x_ref) for j, chip in enumerate(chips)]
        for cp in first:
            cp.start()
        passed = [copy(4 + j, (*chip, c_), sibling) for j, chip in enumerate(chips)]
        for j, chip in enumerate(chips):
            copy(1 + j, (*chip, c_), me).wait_recv()
            passed[j].start()
        copy(0, sibling, me).wait_recv()
        for j, chip in enumerate(chips):
            copy(4 + j, (*chip, 1 - c_), me).wait_recv()
        for cp in first + passed:
            cp.wait_send()
        mine.wait()

    return pl.pallas_call(
        body,
        out_shape=jax.ShapeDtypeStruct((N_DEV,) + x.shape, x.dtype),
        in_specs=[_ANY],
        out_specs=_ANY,
        scratch_shapes=[pltpu.SemaphoreType.DMA((7,)), pltpu.SemaphoreType.DMA((7,)), pltpu.SemaphoreType.DMA(())],
        name=name,
    )(x)


def _direct_exchange(gather, s_ref, r_ref, send_sems, recv_sems, local_sem):
    x_, y_, c_ = lax.axis_index("x"), lax.axis_index("y"), lax.axis_index("c")
    my = 4 * x_ + 2 * y_ + c_
    copies = [pltpu.make_async_copy(s_ref if gather else s_ref.at[my], r_ref.at[my], local_sem)]
    for m in range(1, N_DEV):
        px = 1 - x_ if (m >> 2) & 1 else x_
        py = 1 - y_ if (m >> 1) & 1 else y_
        pc = 1 - c_ if m & 1 else c_
        copies.append(pltpu.make_async_remote_copy(
            src_ref=s_ref if gather else s_ref.at[4 * px + 2 * py + pc], dst_ref=r_ref.at[my],
            send_sem=send_sems.at[m - 1], recv_sem=recv_sems.at[m - 1],
            device_id=(px, py, pc), device_id_type=_MESH))
    return copies


_EXCHANGE_SEMS = [pltpu.SemaphoreType.DMA((7,)), pltpu.SemaphoreType.DMA((7,)), pltpu.SemaphoreType.DMA(())]


def _exchange_during(step, n_steps, gather, s_ref, r_ref, send_sems, recv_sems, local_sem):
    copies = _direct_exchange(gather, s_ref, r_ref, send_sems, recv_sems, local_sem)

    @pl.when(step == 0)
    def _():
        for cp in copies:
            cp.start()

    @pl.when(step == n_steps - 1)
    def _():
        for cp in copies:
            cp.wait()


def _exchange_out_shape(gather, src):
    return jax.ShapeDtypeStruct(((N_DEV,) + src.shape) if gather else src.shape, src.dtype)


def all_to_all(send, name):
    def body(s_ref, r_ref, send_sems, recv_sems, local_sem):
        copies = _direct_exchange(False, s_ref, r_ref, send_sems, recv_sems, local_sem)
        for cp in copies:
            cp.start()
        for cp in copies:
            cp.wait()

    return pl.pallas_call(
        body,
        out_shape=jax.ShapeDtypeStruct(send.shape, send.dtype),
        in_specs=[_ANY],
        out_specs=_ANY,
        scratch_shapes=[pltpu.SemaphoreType.DMA((7,)), pltpu.SemaphoreType.DMA((7,)), pltpu.SemaphoreType.DMA(())],
        name=name,
    )(send)


def adamw(w, m, v, parts, name, tm=256):
    R, C = w.shape
    tm = _pick(R, tm, 8) if R >= 8 else R

    def body(w_ref, m_ref, v_ref, p_ref, g_ref, d_ref, nm_ref, nv_ref):
        g = p_ref[0].astype(F32)
        for s in range(1, N_DEV):
            g = g + p_ref[s].astype(F32)
        nm = ADAM_B1 * m_ref[...] + (1.0 - ADAM_B1) * g
        nv = ADAM_B2 * v_ref[...] + (1.0 - ADAM_B2) * (g * g)
        m_hat = nm / (1.0 - ADAM_B1 ** ADAM_STEP)
        v_hat = nv / (1.0 - ADAM_B2 ** ADAM_STEP)
        g_ref[...] = g
        d_ref[...] = -ADAM_LR * (m_hat / (jnp.sqrt(v_hat) + ADAM_EPS) + ADAM_WD * w_ref[...])
        nm_ref[...] = nm
        nv_ref[...] = nv

    row = pl.BlockSpec((tm, C), lambda i: (i, 0))
    out = jax.ShapeDtypeStruct((R, C), F32)
    return pl.pallas_call(
        body,
        grid=(R // tm,),
        in_specs=[row, row, row, pl.BlockSpec((N_DEV, tm, C), lambda i: (0, i, 0))],
        out_specs=[row] * 4,
        out_shape=[out] * 4,
        compiler_params=_cparams(("parallel",)),
        name=name,
    )(w, m, v, parts)


_WEIGHTS = ["ffn1_norm", "ffn1_w_gu", "ffn1_w_down", "mix_norm", "ffn2_norm", "ffn2_w_gu", "ffn2_w_down",
            "ple_norm", "ple_w_gate", "ple_w_proj", "even_w_in", "even_w_out", "swa_sinks", "rwkv_mu",
            "rwkv_w0", "rwkv_w2", "rwkv_a0", "rwkv_a2", "rwkv_g2", "rwkv_k_k", "rwkv_k_a", "rwkv_r_k",
            "rwkv_ln_w", "rwkv_ln_b", "fox_w_in", "fox_b_f", "fox_w_out", "final_norm"]
_SHARD_AXIS = {"ffn1_w_gu": 2, "ffn1_w_down": 1, "ffn2_w_gu": 2, "ffn2_w_down": 1, "ple_w_gate": 1,
               "ple_w_proj": 2, "even_w_in": 2, "even_w_out": 1, "rwkv_w2": 2, "rwkv_a2": 2, "rwkv_g2": 2,
               "fox_w_in": 2, "fox_w_out": 1}
_SHARDED = [n for n in _WEIGHTS if n in _SHARD_AXIS]
_REPLICATED = [n for n in _WEIGHTS if n not in _SHARD_AXIS]
_PER_LAYER = ("ffn1_w_gu", "ffn1_w_down", "ffn2_w_gu", "ffn2_w_down", "ple_w_gate", "ple_w_proj")
_PIECES = [[(n, 0) for n in _PER_LAYER] + [(n, 0) for n in ("even_w_in", "even_w_out", "rwkv_w2", "rwkv_a2", "rwkv_g2")],
           [(n, 1) for n in _PER_LAYER] + [("fox_w_in", 0), ("fox_w_out", 0)]]
_PACK_LANES = 1024
_PACK_ROW_TILE = 256


def _piece_key(piece):
    name, idx = piece
    return f"{name}{idx}" if name in _PER_LAYER else name


def _pack_rows(arrs):
    flat = jnp.concatenate([a.reshape(-1, _PACK_LANES) for a in arrs], axis=0)
    return jnp.pad(flat, ((0, -flat.shape[0] % _PACK_ROW_TILE), (0, 0)))


def _unpack_rows(flat, shapes):
    out, r0 = [], 0
    for shp in shapes:
        n = math.prod(shp) // _PACK_LANES
        out.append(flat[r0:r0 + n].reshape(shp))
        r0 += n
    return out


def _unshard(gathered, pieces, shapes):
    full, r0 = {}, 0
    for piece, shp in zip(pieces, shapes):
        n = math.prod(shp) // _PACK_LANES
        seg = gathered[:, r0:r0 + n].reshape((N_DEV,) + shp)
        ax = _SHARD_AXIS[piece[0]] - 1
        seg = jnp.moveaxis(seg, 0, ax)
        full[_piece_key(piece)] = seg.reshape(shp[:ax] + (N_DEV * shp[ax],) + shp[ax + 1:])
        r0 += n
    return full


def _to_shards(full, pieces, shapes):
    segs = []
    for piece, shp in zip(pieces, shapes):
        ax = _SHARD_AXIS[piece[0]] - 1
        a = full[_piece_key(piece)].astype(BF16).reshape(shp[:ax] + (N_DEV, shp[ax]) + shp[ax + 1:])
        segs.append(jnp.moveaxis(a, ax, 0).reshape(N_DEV, -1, _PACK_LANES))
    flat = jnp.concatenate(segs, axis=1)
    return jnp.pad(flat, ((0, 0), (0, -flat.shape[1] % _PACK_ROW_TILE), (0, 0)))


def _layer_weights(full):
    W = dict(full)
    if "fox_w_in" in W:
        W["fox_w_in"] = jnp.pad(W["fox_w_in"], ((0, 0), (0, FOX_IN_PAD - W["fox_w_in"].shape[1])))
    for n in ("rwkv_w2", "rwkv_a2", "rwkv_g2"):
        if n in W:
            W[n] = W[n].astype(F32)
    return W


def _pack_small(vals):
    flat = jnp.concatenate([v.reshape(1, -1) for v in vals], axis=1)
    n = flat.shape[1]
    return jnp.pad(flat, ((0, 0), (0, -n % 128)))


def _unpack_small(flat, shapes):
    out, c0 = [], 0
    for shp in shapes:
        n = math.prod(shp)
        out.append(flat[0, c0:c0 + n].reshape(shp))
        c0 += n
    return out


def kernel(x, p, ffn1_norm, ffn1_w_gu, ffn1_w_down, mix_norm, ffn2_norm, ffn2_w_gu, ffn2_w_down, ple_norm, ple_w_gate, ple_w_proj, even_w_in, even_w_out, swa_sinks, rwkv_mu, rwkv_w0, rwkv_w2, rwkv_a0, rwkv_a2, rwkv_g2, rwkv_k_k, rwkv_k_a, rwkv_r_k, rwkv_ln_w, rwkv_ln_b, fox_w_in, fox_b_f, fox_w_out, final_norm, loss_target, m_ffn1_norm, m_ffn1_w_gu, m_ffn1_w_down, m_mix_norm, m_ffn2_norm, m_ffn2_w_gu, m_ffn2_w_down, m_ple_norm, m_ple_w_gate, m_ple_w_proj, m_even_w_in, m_even_w_out, m_swa_sinks, m_rwkv_mu, m_rwkv_w0, m_rwkv_w2, m_rwkv_a0, m_rwkv_a2, m_rwkv_g2, m_rwkv_k_k, m_rwkv_k_a, m_rwkv_r_k, m_rwkv_ln_w, m_rwkv_ln_b, m_fox_w_in, m_fox_b_f, m_fox_w_out, m_final_norm, v_ffn1_norm, v_ffn1_w_gu, v_ffn1_w_down, v_mix_norm, v_ffn2_norm, v_ffn2_w_gu, v_ffn2_w_down, v_ple_norm, v_ple_w_gate, v_ple_w_proj, v_even_w_in, v_even_w_out, v_swa_sinks, v_rwkv_mu, v_rwkv_w0, v_rwkv_w2, v_rwkv_a0, v_rwkv_a2, v_rwkv_g2, v_rwkv_k_k, v_rwkv_k_a, v_rwkv_r_k, v_rwkv_ln_w, v_rwkv_ln_b, v_fox_w_in, v_fox_b_f, v_fox_w_out, v_final_norm):
    given = dict(locals())
    w = {n: given[n] for n in _WEIGHTS}
    m = {n: given["m_" + n] for n in _WEIGHTS}
    v = {n: given["v_" + n] for n in _WEIGHTS}
    small_shapes = [w[n].shape for n in _REPLICATED]
    piece = lambda d, pc: d[pc[0]][pc[1]]
    shapes = [[piece(w, pc).shape for pc in pieces] for pieces in _PIECES]
    w_rows = [_pack_rows([piece(w, pc) for pc in pieces]) for pieces in _PIECES]

    W = _layer_weights(_unshard(all_gather(w_rows[0].astype(BF16), "weights_all_gather"), _PIECES[0], shapes[0]))
    for i in range(2):
        for n in ("ffn1_norm", "mix_norm", "ffn2_norm", "ple_norm"):
            W[f"{n}{i}"] = w[n][i].reshape(1, -1)
    for n in ("swa_sinks", "rwkv_mu", "rwkv_w0", "rwkv_a0", "rwkv_k_k", "rwkv_k_a", "rwkv_r_k", "rwkv_ln_w",
              "rwkv_ln_b", "final_norm"):
        W[n] = w[n].reshape(1, -1)
    n_f = fox_b_f.shape[1]
    W["fox_b_f"] = jnp.pad(fox_b_f.reshape(1, n_f), ((0, 0), (0, 128 - n_f)))
    n_fox = fox_w_in.shape[2] * N_DEV

    def layer1_weights(gathered):
        return _layer_weights(_unshard(gathered, _PIECES[1], shapes[1]))

    def layer1_grads(G):
        G = dict(G, fox_w_in=G["fox_w_in"][:, :n_fox])
        return _to_shards(G, _PIECES[1], shapes[1])

    loss_row, dx, G, parts1 = device_step(x[0], p[:, 0], loss_target[0], W, w_rows[1].astype(BF16),
                                          layer1_weights, layer1_grads)

    parts = [all_to_all(_to_shards(G, _PIECES[0], shapes[0]), "grads_all_to_all"), parts1]
    out_g, out_d, out_m, out_v = {}, {}, {}, {}
    for li, pieces in enumerate(_PIECES):
        res = adamw(w_rows[li], _pack_rows([piece(m, pc) for pc in pieces]),
                    _pack_rows([piece(v, pc) for pc in pieces]), parts[li], f"adamw_sharded{li}")
        for out, rows in zip((out_g, out_d, out_m, out_v), res):
            for pc, a in zip(pieces, _unpack_rows(rows, shapes[li])):
                out.setdefault(pc[0], {})[pc[1]] = a
    for out in (out_g, out_d, out_m, out_v):
        for n in _SHARDED:
            out[n] = jnp.stack([out[n][i] for i in sorted(out[n])])

    gsmall = {}
    for n in ("ffn1_norm", "mix_norm", "ffn2_norm", "ple_norm"):
        gsmall[n] = jnp.concatenate([G[f"{n}0"], G[f"{n}1"]], axis=0)
    for n in ("swa_sinks", "rwkv_mu", "rwkv_w0", "rwkv_a0", "rwkv_k_k", "rwkv_k_a", "rwkv_r_k", "rwkv_ln_w",
              "rwkv_ln_b", "final_norm"):
        gsmall[n] = G[n]
    gsmall["fox_b_f"] = G["fox_b_f"][:, :n_f]
    small = _pack_small([gsmall[n] for n in _REPLICATED] + [loss_row[:, :1]])
    small_parts = all_gather(small, "small_all_gather")
    pad1 = lambda vals: _pack_small(vals + [jnp.zeros((1, 1), F32)])
    gs, ds, nms, nvs = adamw(pad1([w[n] for n in _REPLICATED]), pad1([m[n] for n in _REPLICATED]),
                             pad1([v[n] for n in _REPLICATED]), small_parts, "adamw_replicated")
    out_g.update(zip(_REPLICATED, _unpack_small(gs, small_shapes)))
    out_d.update(zip(_REPLICATED, _unpack_small(ds, small_shapes)))
    out_m.update(zip(_REPLICATED, _unpack_small(nms, small_shapes)))
    out_v.update(zip(_REPLICATED, _unpack_small(nvs, small_shapes)))
    n_small = sum(math.prod(s) for s in small_shapes)
    loss = gs[0, n_small]

    return (loss, dx[None], *[out_g[n] for n in _WEIGHTS], *[out_d[n] for n in _WEIGHTS],
            *[out_m[n] for n in _WEIGHTS], *[out_v[n] for n in _WEIGHTS])
```

```python
import functools
import math

import numpy as np
import jax
import jax.numpy as jnp
from jax import lax
from jax.experimental import pallas as pl
from jax.experimental.pallas import tpu as pltpu

F32 = jnp.float32
BF16 = jnp.bfloat16

D_MODEL = 1024
HEAD_DIM = 64
BLOCK = 128
SWA_HEADS = 8
SWA_KV_HEADS = 2
SWA_GROUP = 4
RWKV_HEADS = 8
RWKV_DIM = 512
FOX_HEADS = 16
FOX_DIM = 1024
D_FF = 2816
NORM_EPS = 1e-6
GN_EPS = 64e-5
L2_EPS = 1e-12
SWA_Q = 512
SWA_KV = 128
SWA_COLS = 768
FOX_IN_PAD = 3200
N_DEV = 8
ADAM_LR = 0.001
ADAM_B1 = 0.9
ADAM_B2 = 0.999
ADAM_EPS = 1e-08
ADAM_WD = 0.01
ADAM_STEP = 10

V7X_VMEM_LIMIT = 56 * 1024 * 1024
SCAN_GROUP = 8
SCAN_CHUNK = 32

_NN = (((1,), (0,)), ((), ()))
_NT = (((1,), (1,)), ((), ()))
_TN = (((0,), (0,)), ((), ()))
_DIMS = {"nn": _NN, "nt": _NT, "tn": _TN}


def _pick(n, target, mult=128):
    best = None
    for t in range(mult, min(n, target) + 1, mult):
        if n % t == 0:
            best = t
    return best or n


def _cparams(sem):
    return pltpu.CompilerParams(dimension_semantics=sem, vmem_limit_bytes=V7X_VMEM_LIMIT)


def _dot(a, b, dims):
    return lax.dot_general(a.astype(BF16), b.astype(BF16), dims, preferred_element_type=F32)


@jax.custom_vjp
def bdot(a, b):
    return _dot(a, b, _NN)


def _bdot_fwd(a, b):
    return _dot(a, b, _NN), (a, b)


def _bdot_bwd(res, g):
    a, b = res
    return _dot(g, b, _NT), _dot(a, g, _TN)


bdot.defvjp(_bdot_fwd, _bdot_bwd)


@jax.custom_vjp
def bdot_nt(a, b):
    return _dot(a, b, _NT)


def _bdot_nt_fwd(a, b):
    return _dot(a, b, _NT), (a, b)


def _bdot_nt_bwd(res, g):
    a, b = res
    return _dot(g, b, _NN), _dot(g, a, _TN)


bdot_nt.defvjp(_bdot_nt_fwd, _bdot_nt_bwd)


def _segsum(x, bd):
    return jnp.dot(x, bd, precision=lax.Precision.HIGHEST, preferred_element_type=F32)


def _sigmoid(x):
    return 1.0 / (1.0 + jnp.exp(-x))


def _sigmoid_tanh(x):
    return 0.5 * jnp.tanh(0.5 * x) + 0.5


def _softplus(x):
    return jnp.maximum(x, 0.0) + jnp.log(1.0 + jnp.exp(-jnp.abs(x)))


def matmul(a, b, mode, name, out_dtype=F32, scale=1.0, res=None, tm=512, tn=1408, tk=1024):
    if mode == "nn":
        (M, K), (K2, N) = a.shape, b.shape
    elif mode == "nt":
        (M, K), (N, K2) = a.shape, b.shape
    else:
        (K, M), (K2, N) = a.shape, b.shape
    assert K == K2, (a.shape, b.shape, mode)
    tm, tn, tk = _pick(M, tm), _pick(N, tn), _pick(K, tk)
    nk = K // tk
    has_res = res is not None

    def body(*refs):
        if has_res:
            a_ref, b_ref, r_ref, o_ref, acc = refs
        else:
            a_ref, b_ref, o_ref, acc = refs
        kk = pl.program_id(2)

        @pl.when(kk == 0)
        def _():
            acc[...] = jnp.zeros_like(acc)

        acc[...] += _dot(a_ref[...], b_ref[...], _DIMS[mode])

        @pl.when(kk == nk - 1)
        def _():
            v = acc[...]
            if scale != 1.0:
                v = v * scale
            if has_res:
                v = v + r_ref[...].astype(F32)
            o_ref[...] = v.astype(out_dtype)

    if mode == "tn":
        a_spec = pl.BlockSpec((tk, tm), lambda i, j, k: (k, i))
    else:
        a_spec = pl.BlockSpec((tm, tk), lambda i, j, k: (i, k))
    if mode == "nt":
        b_spec = pl.BlockSpec((tn, tk), lambda i, j, k: (j, k))
    else:
        b_spec = pl.BlockSpec((tk, tn), lambda i, j, k: (k, j))
    o_spec = pl.BlockSpec((tm, tn), lambda i, j, k: (i, j))
    in_specs = [a_spec, b_spec] + ([o_spec] if has_res else [])
    args = (a, b) + ((res,) if has_res else ())
    return pl.pallas_call(
        body,
        grid=(M // tm, N // tn, nk),
        in_specs=in_specs,
        out_specs=o_spec,
        out_shape=jax.ShapeDtypeStruct((M, N), out_dtype),
        scratch_shapes=[pltpu.VMEM((tm, tn), F32)],
        compiler_params=_cparams(("parallel", "parallel", "arbitrary")),
        name=name,
    )(*args)


def _row_spec(r, tm):
    if isinstance(r, tuple):
        arr, width, blk = r
        return arr, pl.BlockSpec((tm, width), lambda i, blk=blk: (i, blk))
    return r, pl.BlockSpec((tm, r.shape[1]), lambda i: (i, 0))


def _whole_spec(p):
    return pl.BlockSpec(p.shape, lambda i: (0,) * p.ndim)


def rowwise(fn, rows, params, outs, name, tm=256):
    arrs, specs = zip(*[_row_spec(r, tm) for r in rows])
    S = arrs[0].shape[0]
    tm = min(tm, S)
    arrs, specs = zip(*[_row_spec(r, tm) for r in rows])
    n_in = len(rows) + len(params)

    def body(*refs):
        res = fn(*[r[...] for r in refs[:n_in]])
        for o_ref, v in zip(refs[n_in:], res):
            o_ref[...] = v.astype(o_ref.dtype)

    return pl.pallas_call(
        body,
        grid=(S // tm,),
        in_specs=list(specs) + [_whole_spec(p) for p in params],
        out_specs=[pl.BlockSpec((tm, c), lambda i: (i, 0)) for c, _ in outs],
        out_shape=[jax.ShapeDtypeStruct((S, c), dt) for c, dt in outs],
        compiler_params=_cparams(("parallel",)),
        name=name,
    )(*arrs, *params)


def rowwise_vjp(fn, rows, params, cots, name, need=None, row_dtype=F32, consts=(), tm=256):
    nr, npar, nc, nk = len(rows), len(params), len(cots), len(consts)
    need = [True] * nr if need is None else need
    arrs, _ = zip(*[_row_spec(r, tm) for r in rows])
    S = arrs[0].shape[0]
    tm = min(tm, S)
    arrs, specs = zip(*[_row_spec(r, tm) for r in rows])
    carrs, cspecs = zip(*[_row_spec(c, tm) for c in cots])
    widths = [s.block_shape[1] for s in specs]
    n_in = nr + npar + nk + nc

    def body(*refs):
        i = pl.program_id(0)
        xs = [r[...].astype(F32) for r in refs[:nr]]
        ps = [r[...] for r in refs[nr:nr + npar]]
        ks = [r[...] for r in refs[nr + npar:nr + npar + nk]]
        cs = [r[...].astype(F32) for r in refs[nr + npar + nk:n_in]]
        outs, vjp = jax.vjp(lambda *a: fn(*a, *ks), *xs, *ps)
        grads = vjp(tuple(cs))
        o = n_in
        for j in range(nr):
            if need[j]:
                refs[o][...] = grads[j].astype(refs[o].dtype)
                o += 1
        for j in range(npar):
            g_ref = refs[o + j]

            @pl.when(i == 0)
            def _(g_ref=g_ref):
                g_ref[...] = jnp.zeros_like(g_ref)

            g_ref[...] += grads[nr + j]

    out_specs = [pl.BlockSpec((tm, w), lambda i: (i, 0)) for w, nd in zip(widths, need) if nd]
    out_shape = [jax.ShapeDtypeStruct((S, w), row_dtype) for w, nd in zip(widths, need) if nd]
    out_specs += [_whole_spec(p) for p in params]
    out_shape += [jax.ShapeDtypeStruct(p.shape, F32) for p in params]
    res = pl.pallas_call(
        body,
        grid=(S // tm,),
        in_specs=list(specs) + [_whole_spec(p) for p in params] + [_whole_spec(k) for k in consts] + list(cspecs),
        out_specs=out_specs,
        out_shape=out_shape,
        compiler_params=_cparams(("arbitrary",)),
        name=name,
    )(*arrs, *params, *consts, *carrs)
    nrow = sum(need)
    return list(res[:nrow]), list(res[nrow:])


def _rms(x, g):
    return x * lax.rsqrt(jnp.mean(x * x, axis=-1, keepdims=True) + NORM_EPS) * g


def _f_rms(x, g):
    return (_rms(x, g),)


def _f_act(g, u):
    return (g * _sigmoid(g) * u,)


def _f_ple(x, z, pp):
    return (x + _sigmoid(z) * pp,)


def _f_mix(h, sh, mu):
    return (h + (sh - h) * mu,)


def _f_logf(fz, bf):
    return (-_softplus(-(fz + bf)),)


def _f_rwkv_pre(hk, hw, ha, hg, w0, w2, a0, a2, g2, k_k, k_a, bd):
    wlog = -_softplus(-(w0 + bdot(jnp.tanh(hw), w2))) - 0.5
    a = _sigmoid(a0 + bdot(ha, a2))
    g = bdot(_sigmoid(hg), g2)
    kk = hk * k_k
    kk = kk / jnp.maximum(jnp.sqrt(_segsum(kk * kk, bd)), L2_EPS)
    k2 = hk * (1.0 + (a - 1.0) * k_a)
    decay = jnp.exp(-jnp.exp(wlog))
    return decay, k2, kk, kk * a, g


def _f_rwkv_post(y, r, k2, v, g, ln_w, ln_b, r_k, bd):
    mean = _segsum(y, bd) * (1.0 / HEAD_DIM)
    d = y - mean
    var = _segsum(d * d, bd) * (1.0 / HEAD_DIM)
    yn = d * lax.rsqrt(var + GN_EPS) * ln_w + ln_b
    yn = yn + _segsum(r * k2 * r_k, bd) * v
    return (yn * g,)


def loss_head(x, target, gf, tm=256):
    S, D = x.shape
    tm = min(tm, S)

    def f(xt, g, tt):
        err = _rms(xt, g) - tt
        return 0.5 * jnp.sum(err * err) * (1.0 / D)

    def body(x_ref, t_ref, g_ref, dx_ref, dg_ref, l_ref):
        i = pl.program_id(0)
        val, (dx, dg) = jax.value_and_grad(f, argnums=(0, 1))(x_ref[...], g_ref[...], t_ref[...])

        @pl.when(i == 0)
        def _():
            dg_ref[...] = jnp.zeros_like(dg_ref)
            l_ref[...] = jnp.zeros_like(l_ref)

        dx_ref[...] = dx
        dg_ref[...] += dg
        l_ref[...] += jnp.full(l_ref.shape, val, F32)

    row = pl.BlockSpec((tm, D), lambda i: (i, 0))
    vec = pl.BlockSpec((1, D), lambda i: (0, 0))
    return pl.pallas_call(
        body,
        grid=(S // tm,),
        in_specs=[row, row, vec],
        out_specs=[row, vec, pl.BlockSpec((1, 128), lambda i: (0, 0))],
        out_shape=[jax.ShapeDtypeStruct((S, D), F32), jax.ShapeDtypeStruct((1, D), F32),
                   jax.ShapeDtypeStruct((1, 128), F32)],
        compiler_params=_cparams(("arbitrary",)),
        name="loss_head",
    )(x, target, gf)


def _swa_block(q, kp, kc, vp, vc, sink, slope, n):
    k = jnp.concatenate([kp, kc], axis=0)
    v = jnp.concatenate([vp, vc], axis=0)
    rows = q.shape[0]
    logits = bdot_nt(q, k) * (HEAD_DIM ** -0.5)
    qi = lax.broadcasted_iota(jnp.int32, (rows, 2 * BLOCK), 0) & (BLOCK - 1)
    ki = lax.broadcasted_iota(jnp.int32, (rows, 2 * BLOCK), 1)
    dist = qi + BLOCK - ki
    valid = (dist >= 0) & (dist < BLOCK) & ((n - 1) * BLOCK + ki >= 0)
    logits = logits - slope * dist.astype(F32)
    logits = jnp.where(valid, logits, -jnp.inf)
    m = jnp.maximum(jnp.max(logits, axis=-1, keepdims=True), sink)
    pr = jnp.exp(logits - m)
    denom = jnp.sum(pr, axis=-1, keepdims=True) + jnp.exp(sink - m)
    return bdot(pr / denom, v)


def _swa_specs(S):
    nb = S // BLOCK
    q_spec = pl.BlockSpec((None, SWA_GROUP, BLOCK, HEAD_DIM), lambda h, n: (h, 0, n, 0))
    kc_spec = pl.BlockSpec((None, BLOCK, HEAD_DIM), lambda h, n: (h, n, 0))
    kp_spec = pl.BlockSpec((None, BLOCK, HEAD_DIM), lambda h, n: (h, jnp.maximum(n - 1, 0), 0))
    col_spec = pl.BlockSpec((None, SWA_GROUP * BLOCK, 1), lambda h, n: (h, 0, 0))
    return nb, q_spec, kp_spec, kc_spec, col_spec


def swa_fwd(q, k, v, sink_col, slope_col):
    S = q.shape[2]
    nb, q_spec, kp_spec, kc_spec, col_spec = _swa_specs(S)

    def body(q_ref, kp_ref, kc_ref, vp_ref, vc_ref, s_ref, a_ref, o_ref):
        n = pl.program_id(1)
        qq = q_ref[...].reshape(SWA_GROUP * BLOCK, HEAD_DIM)
        out = _swa_block(qq, kp_ref[...], kc_ref[...], vp_ref[...], vc_ref[...], s_ref[...], a_ref[...], n)
        o_ref[...] = out.reshape(SWA_GROUP, BLOCK, HEAD_DIM)

    return pl.pallas_call(
        body,
        grid=(SWA_KV_HEADS, nb),
        in_specs=[q_spec, kp_spec, kc_spec, kp_spec, kc_spec, col_spec, col_spec],
        out_specs=q_spec,
        out_shape=jax.ShapeDtypeStruct(q.shape, F32),
        compiler_params=_cparams(("parallel", "parallel")),
        name="swa_fwd",
    )(q, k, k, v, v, sink_col, slope_col)


def swa_bwd(q, k, v, sink_col, slope_col, dout):
    S = q.shape[2]
    nb, q_spec, kp_spec, kc_spec, col_spec = _swa_specs(S)

    def body(q_ref, kp_ref, kc_ref, vp_ref, vc_ref, s_ref, a_ref, do_ref,
             dq_ref, dkp_ref, dkc_ref, dvp_ref, dvc_ref, ds_ref):
        n = pl.program_id(1)
        qq = q_ref[...].reshape(SWA_GROUP * BLOCK, HEAD_DIM)
        slope = a_ref[...]
        f = lambda a, b, c, d, e, s: _swa_block(a, b, c, d, e, s, slope, n)
        _, vjp = jax.vjp(f, qq, kp_ref[...], kc_ref[...], vp_ref[...], vc_ref[...], s_ref[...])
        dq, dkp, dkc, dvp, dvc, ds = vjp(do_ref[...].reshape(SWA_GROUP * BLOCK, HEAD_DIM))
        dq_ref[...] = dq.reshape(SWA_GROUP, BLOCK, HEAD_DIM)
        dkp_ref[...] = dkp
        dkc_ref[...] = dkc
        dvp_ref[...] = dvp
        dvc_ref[...] = dvc

        @pl.when(n == 0)
        def _():
            ds_ref[...] = jnp.zeros_like(ds_ref)

        ds_ref[...] += ds

    kv_shape = jax.ShapeDtypeStruct(k.shape, F32)
    return pl.pallas_call(
        body,
        grid=(SWA_KV_HEADS, nb),
        in_specs=[q_spec, kp_spec, kc_spec, kp_spec, kc_spec, col_spec, col_spec, q_spec],
        out_specs=[q_spec, kc_spec, kc_spec, kc_spec, kc_spec, col_spec],
        out_shape=[jax.ShapeDtypeStruct(q.shape, F32), kv_shape, kv_shape, kv_shape, kv_shape,
                   jax.ShapeDtypeStruct(sink_col.shape, F32)],
        compiler_params=_cparams(("parallel", "arbitrary")),
        name="swa_bwd",
    )(q, k, k, v, v, sink_col, slope_col, dout)


def _split2(x):
    hi = x.astype(BF16)
    return (x - hi.astype(F32)).astype(BF16), hi


def _dot2(x, m):
    lo, hi = _split2(x)
    return jnp.dot(lo, m, preferred_element_type=F32) + jnp.dot(hi, m, preferred_element_type=F32)


def _seg_sum(x, bd):
    w = bd.shape[0]
    return jnp.concatenate([_dot2(x[:, i:i + w], bd) for i in range(0, x.shape[1], w)], axis=1)


def _scan_consts():
    r = np.arange(256)
    bd = (r[:, None] // HEAD_DIM == r[None, :] // HEAD_DIM).astype(np.float32)
    c = np.arange(RWKV_DIM)
    e = (np.arange(HEAD_DIM)[:, None] // SCAN_GROUP == c[None, :] // HEAD_DIM).astype(np.float32)
    diag = (np.arange(HEAD_DIM)[:, None] == c[None, :] % HEAD_DIM).astype(np.float32)
    return jnp.asarray(bd, BF16), jnp.asarray(e, BF16), jnp.asarray(diag, F32)


def _to_colblocks(a):
    S = a.shape[0]
    a = a.reshape(S // SCAN_GROUP, SCAN_GROUP, RWKV_HEADS, HEAD_DIM)
    return a.transpose(0, 3, 2, 1).reshape(S // SCAN_GROUP, HEAD_DIM, RWKV_HEADS * SCAN_GROUP)


def _scan_step(St, tt, base, col_g, lane_t, kk_ref, w_ref, b_ref, k_ref, bd, e):
    row = lambda ref: ref[pl.ds(base + tt, 1), :]
    u_b = _seg_sum(St * row(kk_ref), bd)
    v_b = _dot2(jnp.where(lane_t == tt, col_g, 0.0), e)
    return St * row(w_ref) - u_b * row(b_ref) + v_b * row(k_ref), u_b, v_b


def rwkv_scan_fwd(r, w, k, kk, b, vB, gather_src):
    S, C = r.shape
    N, G = HEAD_DIM, SCAN_GROUP
    chunk = min(SCAN_CHUNK, S)
    nchunk, ng = S // chunk, chunk // G
    bd, e, diag = _scan_consts()

    def body(r_ref, w_ref, k_ref, kk_ref, b_ref, vB_ref, bd_ref, e_ref, dg_ref, xs_ref, y_ref, ck_ref, xr_ref,
             S_ref, send_sems, recv_sems, local_sem):
        c = pl.program_id(0)
        _exchange_during(c, nchunk, True, xs_ref, xr_ref, send_sems, recv_sems, local_sem)

        @pl.when(c == 0)
        def _():
            S_ref[...] = jnp.zeros_like(S_ref)

        ck_ref[...] = S_ref[...]
        sub = lax.broadcasted_iota(jnp.int32, (G, C), 0)
        lane_t = lax.broadcasted_iota(jnp.int32, (N, N), 1) & (G - 1)

        def group(g, St):
            base = pl.multiple_of(g * G, G)
            vb = vB_ref[g]
            ys = jnp.zeros((G, C), F32)
            for tt in range(G):
                St, _, _ = _scan_step(St, tt, base, vb, lane_t, kk_ref, w_ref, b_ref, k_ref, bd_ref[...], e_ref[...])
                y_b = _seg_sum(St * r_ref[pl.ds(base + tt, 1), :], bd_ref[...])
                ys = jnp.where(sub == tt, jnp.sum(y_b * dg_ref[...], axis=0, keepdims=True), ys)
            y_ref[pl.ds(base, G), :] = ys
            return St

        S_ref[...] = lax.fori_loop(0, ng, group, S_ref[...])

    row = pl.BlockSpec((chunk, C), lambda c: (c, 0))
    col = pl.BlockSpec((ng, N, N), lambda c: (c, 0, 0))
    return pl.pallas_call(
        body,
        grid=(nchunk,),
        in_specs=[row] * 5 + [col, _whole_spec(bd), _whole_spec(e), _whole_spec(diag), _ANY],
        out_specs=[row, pl.BlockSpec((None, N, C), lambda c: (c, 0, 0)), _ANY],
        out_shape=[jax.ShapeDtypeStruct((S, C), F32), jax.ShapeDtypeStruct((nchunk, N, C), F32),
                   _exchange_out_shape(True, gather_src)],
        scratch_shapes=[pltpu.VMEM((N, C), F32)] + _EXCHANGE_SEMS,
        compiler_params=_cparams(("arbitrary",)),
        name="rwkv_scan_fwd",
    )(r, w, k, kk, b, vB, bd, e, diag, gather_src)


def rwkv_scan_bwd(r, w, k, kk, b, vB, dyB, ckpt, scatter_src):
    S, C = r.shape
    N, G = HEAD_DIM, SCAN_GROUP
    chunk = min(SCAN_CHUNK, S)
    nchunk, ng = S // chunk, chunk // G
    bd, e, diag = _scan_consts()

    def body(r_ref, w_ref, k_ref, kk_ref, b_ref, vB_ref, dyB_ref, ck_ref, bd_ref, e_ref, dg_ref, xs_ref,
             dr_ref, dw_ref, dk_ref, dkk_ref, db_ref, dv_ref, xr_ref, G_ref, sbuf, ubuf, vbuf,
             send_sems, recv_sems, local_sem):
        c = pl.program_id(0)
        _exchange_during(c, nchunk, False, xs_ref, xr_ref, send_sems, recv_sems, local_sem)

        @pl.when(c == 0)
        def _():
            G_ref[...] = jnp.zeros_like(G_ref)

        lane_t = lax.broadcasted_iota(jnp.int32, (N, N), 1) & (G - 1)
        sub = lax.broadcasted_iota(jnp.int32, (G, C), 0)

        def fgroup(g, St):
            base = pl.multiple_of(g * G, G)
            vb = vB_ref[g]
            for tt in range(G):
                sbuf[base + tt] = St
                St, u_b, v_b = _scan_step(St, tt, base, vb, lane_t, kk_ref, w_ref, b_ref, k_ref, bd_ref[...],
                                          e_ref[...])
                ubuf[base + tt] = u_b
                vbuf[base + tt] = v_b
            return St

        sbuf[chunk] = lax.fori_loop(0, ng, fgroup, ck_ref[...])

        def bgroup(gi, Gt):
            g = ng - 1 - gi
            base = pl.multiple_of(g * G, G)
            dyb = dyB_ref[g]
            rows = [jnp.zeros((G, C), F32) for _ in range(6)]
            colsum = lambda a: jnp.sum(a, axis=0, keepdims=True)
            for tt in reversed(range(G)):
                row = lambda ref: ref[pl.ds(base + tt, 1), :]
                Sp, Sc = sbuf[base + tt], sbuf[base + tt + 1]
                u_b, v_b = ubuf[base + tt], vbuf[base + tt]
                dy_b = _dot2(jnp.where(lane_t == tt, dyb, 0.0), e_ref[...])
                Gt = Gt + dy_b * row(r_ref)
                d_r = colsum(Sc * dy_b)
                d_w = colsum(Gt * Sp)
                du_b = -_seg_sum(Gt * row(b_ref), bd_ref[...])
                d_b = -colsum(Gt * u_b)
                d_v = colsum(_seg_sum(Gt * row(k_ref), bd_ref[...]) * dg_ref[...])
                d_k = colsum(Gt * v_b)
                d_kk = colsum(Sp * du_b)
                Gt = Gt * row(w_ref) + du_b * row(kk_ref)
                rows = [jnp.where(sub == tt, new, acc)
                        for new, acc in zip((d_r, d_w, d_k, d_kk, d_b, d_v), rows)]
            for ref, val in zip((dr_ref, dw_ref, dk_ref, dkk_ref, db_ref, dv_ref), rows):
                ref[pl.ds(base, G), :] = val
            return Gt

        G_ref[...] = lax.fori_loop(0, ng, bgroup, G_ref[...])

    rev = lambda c: nchunk - 1 - c
    row = pl.BlockSpec((chunk, C), lambda c: (rev(c), 0))
    col = pl.BlockSpec((ng, N, N), lambda c: (rev(c), 0, 0))
    rshape = jax.ShapeDtypeStruct((S, C), F32)
    return pl.pallas_call(
        body,
        grid=(nchunk,),
        in_specs=[row] * 5 + [col, col, pl.BlockSpec((None, N, C), lambda c: (rev(c), 0, 0)),
                              _whole_spec(bd), _whole_spec(e), _whole_spec(diag), _ANY],
        out_specs=[row] * 6 + [_ANY],
        out_shape=[rshape] * 6 + [_exchange_out_shape(False, scatter_src)],
        scratch_shapes=[pltpu.VMEM((N, C), F32), pltpu.VMEM((chunk + 1, N, C), F32),
                        pltpu.VMEM((chunk, N, C), F32), pltpu.VMEM((chunk, N, C), F32)] + _EXCHANGE_SEMS,
        compiler_params=_cparams(("arbitrary",)),
        name="rwkv_scan_bwd",
    )(r, w, k, kk, b, vB, dyB, ckpt, bd, e, diag, scatter_src)


def seq_cumsum(x, reverse, name):
    S, C = x.shape
    tb = min(256, S)
    nb = S // tb

    def body(x_ref, o_ref, carry):
        i = pl.program_id(0)

        @pl.when(i == 0)
        def _():
            carry[...] = jnp.zeros_like(carry)

        ri = lax.broadcasted_iota(jnp.int32, (tb, tb), 0)
        ci = lax.broadcasted_iota(jnp.int32, (tb, tb), 1)
        tri = jnp.where((ci >= ri) if reverse else (ci <= ri), 1.0, 0.0).astype(F32)
        xb = x_ref[...]
        out = jnp.dot(tri, xb, precision=lax.Precision.HIGHEST, preferred_element_type=F32) + carry[...]
        o_ref[...] = out
        carry[...] = carry[...] + jnp.sum(xb, axis=0, keepdims=True)

    idx = (lambda i: (nb - 1 - i, 0)) if reverse else (lambda i: (i, 0))
    return pl.pallas_call(
        body,
        grid=(nb,),
        in_specs=[pl.BlockSpec((tb, C), idx)],
        out_specs=pl.BlockSpec((tb, C), idx),
        out_shape=jax.ShapeDtypeStruct((S, C), F32),
        scratch_shapes=[pltpu.VMEM((1, C), F32)],
        compiler_params=_cparams(("arbitrary",)),
        name=name,
    )(x)


def _fox_logits(q, k, cq, ck, qi, ki, tq, tk):
    s = _dot(q, k, _NT) * (HEAD_DIM ** -0.5) + cq - ck
    row = qi * tq + lax.broadcasted_iota(jnp.int32, (tq, tk), 0)
    col = ki * tk + lax.broadcasted_iota(jnp.int32, (tq, tk), 1)
    return jnp.where(col <= row, s, -jnp.inf)


def fox_fwd(q, k, v, c_col, c_row):
    Hh, S, Dh = q.shape
    tq = tk = min(512, S)
    nq, nk = S // tq, S // tk

    def body(q_ref, k_ref, v_ref, cq_ref, ck_ref, o_ref, lse_ref, m_s, l_s, acc_s):
        qi, ki = pl.program_id(1), pl.program_id(2)

        @pl.when(ki == 0)
        def _():
            m_s[...] = jnp.full_like(m_s, -jnp.inf)
            l_s[...] = jnp.zeros_like(l_s)
            acc_s[...] = jnp.zeros_like(acc_s)

        @pl.when(ki <= qi)
        def _():
            s = _fox_logits(q_ref[...], k_ref[...], cq_ref[...], ck_ref[...], qi, ki, tq, tk)
            m_new = jnp.maximum(m_s[...], jnp.max(s, axis=-1, keepdims=True))
            alpha = jnp.exp(m_s[...] - m_new)
            p = jnp.exp(s - m_new)
            l_s[...] = alpha * l_s[...] + jnp.sum(p, axis=-1, keepdims=True)
            acc_s[...] = alpha * acc_s[...] + _dot(p, v_ref[...], _NN)
            m_s[...] = m_new

        @pl.when(ki == nk - 1)
        def _():
            o_ref[...] = acc_s[...] / l_s[...]
            lse_ref[...] = m_s[...] + jnp.log(l_s[...])

    qs = pl.BlockSpec((None, tq, Dh), lambda h, i, j: (h, i, 0))
    ks = pl.BlockSpec((None, tk, Dh), lambda h, i, j: (h, jnp.minimum(i, j), 0))
    cqs = pl.BlockSpec((None, tq, 1), lambda h, i, j: (h, i, 0))
    cks = pl.BlockSpec((None, 1, tk), lambda h, i, j: (h, 0, jnp.minimum(i, j)))
    return pl.pallas_call(
        body,
        grid=(Hh, nq, nk),
        in_specs=[qs, ks, ks, cqs, cks],
        out_specs=[qs, cqs],
        out_shape=[jax.ShapeDtypeStruct((Hh, S, Dh), F32), jax.ShapeDtypeStruct((Hh, S, 1), F32)],
        scratch_shapes=[pltpu.VMEM((tq, 1), F32), pltpu.VMEM((tq, 1), F32), pltpu.VMEM((tq, Dh), F32)],
        compiler_params=_cparams(("parallel", "parallel", "arbitrary")),
        name="fox_fwd",
    )(q, k, v, c_col, c_row)


def fox_bwd_dq(q, k, v, c_col, c_row, o, lse, do):
    Hh, S, Dh = q.shape
    tq = tk = min(512, S)
    nq, nk = S // tq, S // tk

    def body(q_ref, k_ref, v_ref, cq_ref, ck_ref, o_ref, lse_ref, do_ref, dq_ref, dr_ref, acc_s, row_s):
        qi, ki = pl.program_id(1), pl.program_id(2)

        @pl.when(ki == 0)
        def _():
            acc_s[...] = jnp.zeros_like(acc_s)
            row_s[...] = jnp.zeros_like(row_s)

        @pl.when(ki <= qi)
        def _():
            s = _fox_logits(q_ref[...], k_ref[...], cq_ref[...], ck_ref[...], qi, ki, tq, tk)
            p = jnp.exp(s - lse_ref[...])
            do_t = do_ref[...]
            delta = jnp.sum(do_t * o_ref[...], axis=-1, keepdims=True)
            dp = _dot(do_t, v_ref[...], _NT)
            ds = p * (dp - delta)
            acc_s[...] += _dot(ds, k_ref[...], _NN)
            row_s[...] += jnp.sum(ds, axis=-1, keepdims=True)

        @pl.when(ki == nk - 1)
        def _():
            dq_ref[...] = acc_s[...] * (HEAD_DIM ** -0.5)
            dr_ref[...] = row_s[...]

    qs = pl.BlockSpec((None, tq, Dh), lambda h, i, j: (h, i, 0))
    ks = pl.BlockSpec((None, tk, Dh), lambda h, i, j: (h, jnp.minimum(i, j), 0))
    cqs = pl.BlockSpec((None, tq, 1), lambda h, i, j: (h, i, 0))
    cks = pl.BlockSpec((None, 1, tk), lambda h, i, j: (h, 0, jnp.minimum(i, j)))
    return pl.pallas_call(
        body,
        grid=(Hh, nq, nk),
        in_specs=[qs, ks, ks, cqs, cks, qs, cqs, qs],
        out_specs=[qs, cqs],
        out_shape=[jax.ShapeDtypeStruct((Hh, S, Dh), F32), jax.ShapeDtypeStruct((Hh, S, 1), F32)],
        scratch_shapes=[pltpu.VMEM((tq, Dh), F32), pltpu.VMEM((tq, 1), F32)],
        compiler_params=_cparams(("parallel", "parallel", "arbitrary")),
        name="fox_bwd_dq",
    )(q, k, v, c_col, c_row, o, lse, do)


def fox_bwd_dkv(q, k, v, c_col, c_row, o, lse, do):
    Hh, S, Dh = q.shape
    tq = tk = min(512, S)
    nq, nk = S // tq, S // tk

    def body(q_ref, k_ref, v_ref, cq_ref, ck_ref, o_ref, lse_ref, do_ref, dk_ref, dv_ref, dc_ref, dk_s, dv_s, dc_s):
        ki, qi = pl.program_id(1), pl.program_id(2)

        @pl.when(qi == 0)
        def _():
            dk_s[...] = jnp.zeros_like(dk_s)
            dv_s[...] = jnp.zeros_like(dv_s)
            dc_s[...] = jnp.zeros_like(dc_s)

        @pl.when(qi >= ki)
        def _():
            s = _fox_logits(q_ref[...], k_ref[...], cq_ref[...], ck_ref[...], qi, ki, tq, tk)
            p = jnp.exp(s - lse_ref[...])
            do_t = do_ref[...]
            delta = jnp.sum(do_t * o_ref[...], axis=-1, keepdims=True)
            dp = _dot(do_t, v_ref[...], _NT)
            ds = p * (dp - delta)
            dv_s[...] += _dot(p, do_t, _TN)
            dk_s[...] += _dot(ds, q_ref[...], _TN)
            dc_s[...] += jnp.sum(ds, axis=0, keepdims=True)

        @pl.when(qi == nq - 1)
        def _():
            dk_ref[...] = dk_s[...] * (HEAD_DIM ** -0.5)
            dv_ref[...] = dv_s[...]
            dc_ref[...] = dc_s[...]

    qs = pl.BlockSpec((None, tq, Dh), lambda h, j, i: (h, jnp.maximum(i, j), 0))
    ks = pl.BlockSpec((None, tk, Dh), lambda h, j, i: (h, j, 0))
    cqs = pl.BlockSpec((None, tq, 1), lambda h, j, i: (h, jnp.maximum(i, j), 0))
    cks = pl.BlockSpec((None, 1, tk), lambda h, j, i: (h, 0, j))
    return pl.pallas_call(
        body,
        grid=(Hh, nk, nq),
        in_specs=[qs, ks, ks, cqs, cks, qs, cqs, qs],
        out_specs=[ks, ks, cks],
        out_shape=[jax.ShapeDtypeStruct((Hh, S, Dh), F32), jax.ShapeDtypeStruct((Hh, S, Dh), F32),
                   jax.ShapeDtypeStruct((Hh, 1, S), F32)],
        scratch_shapes=[pltpu.VMEM((tk, Dh), F32), pltpu.VMEM((tk, Dh), F32), pltpu.VMEM((1, tk), F32)],
        compiler_params=_cparams(("parallel", "parallel", "arbitrary")),
        name="fox_bwd_dkv",
    )(q, k, v, c_col, c_row, o, lse, do)


def _heads(a, nh):
    S = a.shape[0]
    return a.reshape(S, nh, HEAD_DIM).transpose(1, 0, 2)


def _unheads(a):
    nh, S, _ = a.shape
    return a.transpose(1, 0, 2).reshape(S, nh * HEAD_DIM)


def _shift_down(a):
    return jnp.pad(a[:-1], ((1, 0), (0, 0)))


def _shift_up(a):
    return jnp.pad(a[1:], ((0, 1), (0, 0)))


def _block_diag_ones():
    i = np.arange(RWKV_DIM) // HEAD_DIM
    return jnp.asarray((i[:, None] == i[None, :]).astype(np.float32))


FFN_ROWS = 1024
FFN_COLS = 256


def _ffn_specs(S, F, tm, fc):
    nf = F // fc
    row = pl.BlockSpec((tm, D_MODEL), lambda i, j: (i, 0))
    vec = pl.BlockSpec((1, D_MODEL), lambda i, j: (0, 0))
    wg = pl.BlockSpec((D_MODEL, fc), lambda i, j: (0, j))
    wu = pl.BlockSpec((D_MODEL, fc), lambda i, j: (0, nf + j))
    wd = pl.BlockSpec((fc, D_MODEL), lambda i, j: (j, 0))
    hid = pl.BlockSpec((tm, fc), lambda i, j: (i, j))
    return nf, row, vec, wg, wu, wd, hid


def ffn_fwd(x, g_norm, w_gu, w_down, tag):
    S, F = x.shape[0], w_down.shape[0]
    tm, fc = min(FFN_ROWS, S), FFN_COLS
    nf, row, vec, wg, wu, wd, _ = _ffn_specs(S, F, tm, fc)

    def body(x_ref, g_ref, wg_ref, wu_ref, wd_ref, o_ref, hn_ref, hn_s, acc):
        j = pl.program_id(1)

        @pl.when(j == 0)
        def _():
            hn_s[...] = _rms(x_ref[...], g_ref[...]).astype(BF16)
            hn_ref[...] = hn_s[...]
            acc[...] = jnp.zeros_like(acc)

        g = _dot(hn_s[...], wg_ref[...], _NN)
        u = _dot(hn_s[...], wu_ref[...], _NN)
        acc[...] += _dot(g * _sigmoid_tanh(g) * u, wd_ref[...], _NN)

        @pl.when(j == nf - 1)
        def _():
            o_ref[...] = x_ref[...] + 0.5 * acc[...]

    out, hn = pl.pallas_call(
        body,
        grid=(S // tm, nf),
        in_specs=[row, vec, wg, wu, wd],
        out_specs=[row, row],
        out_shape=[jax.ShapeDtypeStruct((S, D_MODEL), F32), jax.ShapeDtypeStruct((S, D_MODEL), BF16)],
        scratch_shapes=[pltpu.VMEM((tm, D_MODEL), BF16), pltpu.VMEM((tm, D_MODEL), F32)],
        compiler_params=_cparams(("parallel", "arbitrary")),
        name=tag + "_fwd",
    )(x, g_norm, w_gu, w_gu, w_down)
    return out, (x, hn)


def ffn_bwd(dy, saved, g_norm, w_gu, w_down, tag):
    x, hn = saved
    S, F = x.shape[0], w_down.shape[0]
    tm, fc = min(FFN_ROWS, S), FFN_COLS
    nf, row, vec, wg, wu, wd, hid = _ffn_specs(S, F, tm, fc)

    def body(dy_ref, x_ref, hn_ref, g_ref, wg_ref, wu_ref, wd_ref, dx_ref, dgn_ref, a_ref, dg_ref, du_ref,
             dyh_s, dhn):
        i, j = pl.program_id(0), pl.program_id(1)

        @pl.when(j == 0)
        def _():
            dyh_s[...] = (0.5 * dy_ref[...]).astype(BF16)
            dhn[...] = jnp.zeros_like(dhn)

        hn_t = hn_ref[...]
        g = _dot(hn_t, wg_ref[...], _NN)
        u = _dot(hn_t, wu_ref[...], _NN)
        da = _dot(dyh_s[...], wd_ref[...], _NT)
        sig = _sigmoid_tanh(g)
        gs = g * sig
        a_ref[...] = (gs * u).astype(BF16)
        dg = ((da * u) * (sig + gs * (1.0 - sig))).astype(BF16)
        du = (da * gs).astype(BF16)
        dg_ref[...] = dg
        du_ref[...] = du
        dhn[...] += _dot(jnp.concatenate([dg, du], axis=1),
                         jnp.concatenate([wg_ref[...], wu_ref[...]], axis=1), _NT)

        @pl.when(j == nf - 1)
        def _():
            _, vjp_n = jax.vjp(_rms, x_ref[...], g_ref[...])
            dx, dgn = vjp_n(dhn[...])
            dx_ref[...] = dy_ref[...] + dx

            @pl.when(i == 0)
            def _():
                dgn_ref[...] = jnp.zeros_like(dgn_ref)

            dgn_ref[...] += dgn

    hshape = jax.ShapeDtypeStruct((S, F), BF16)
    dx, dgn, act, dg, du = pl.pallas_call(
        body,
        grid=(S // tm, nf),
        in_specs=[row, row, row, vec, wg, wu, wd],
        out_specs=[row, vec, hid, hid, hid],
        out_shape=[jax.ShapeDtypeStruct((S, D_MODEL), F32), jax.ShapeDtypeStruct((1, D_MODEL), F32),
                   hshape, hshape, hshape],
        scratch_shapes=[pltpu.VMEM((tm, D_MODEL), BF16), pltpu.VMEM((tm, D_MODEL), F32)],
        compiler_params=_cparams(("arbitrary", "arbitrary")),
        name=tag + "_bwd",
    )(dy, x, hn, g_norm, w_gu, w_gu, w_down)
    d_wdown = matmul(act, dy, "tn", tag + "_dwd", out_dtype=BF16, scale=0.5)
    d_wgu = jnp.concatenate([matmul(hn, dg, "tn", tag + "_dwg", out_dtype=BF16),
                             matmul(hn, du, "tn", tag + "_dwu", out_dtype=BF16)], axis=1)
    return dx, dgn, d_wgu, d_wdown


def ple_fwd(x, p_i, g_norm, w_gate, w_proj, tag):
    hn, = rowwise(_f_rms, [x], [g_norm], [(D_MODEL, BF16)], tag + "_rms")
    z = matmul(hn, w_gate, "nn", tag + "_gate")
    pp = matmul(p_i, w_proj, "nn", tag + "_proj")
    out, = rowwise(_f_ple, [x, z, pp], [], [(D_MODEL, F32)], tag + "_mix")
    return out, (x, hn, z, pp)


def ple_bwd(dy, saved, p_i, g_norm, w_gate, tag):
    x, hn, z, pp = saved
    (dz, dpp), _ = rowwise_vjp(_f_ple, [x, z, pp], [], [dy], tag + "_dmix", need=[False, True, True],
                               row_dtype=BF16)
    d_wproj = matmul(p_i, dpp, "tn", tag + "_dwp", out_dtype=BF16)
    d_wgate = matmul(hn, dz, "tn", tag + "_dwg", out_dtype=BF16)
    dhn = matmul(dz, w_gate, "nt", tag + "_dhn")
    (dx,), (dgn,) = rowwise_vjp(_f_rms, [x], [g_norm], [dhn], tag + "_drms")
    return dy + dx, dgn, d_wgate, d_wproj


def _swa_consts(sinks):
    slopes = np.asarray([2.0 ** (-(i + 1)) for i in range(SWA_HEADS)], np.float32)
    slope_col = jnp.asarray(np.repeat(slopes, BLOCK).reshape(SWA_KV_HEADS, SWA_GROUP * BLOCK, 1))
    sink_col = jnp.repeat(sinks.reshape(SWA_HEADS), BLOCK).reshape(SWA_KV_HEADS, SWA_GROUP * BLOCK, 1)
    return sink_col, slope_col


def even_mix_fwd(x, W, gather_src):
    S = x.shape[0]
    hn, = rowwise(_f_rms, [x], [W["mix_norm0"]], [(D_MODEL, BF16)], "emix_rms")
    proj = matmul(hn, W["even_w_in"], "nn", "emix_in")
    qa = _heads(proj[:, :SWA_Q], SWA_HEADS).reshape(SWA_KV_HEADS, SWA_GROUP, S, HEAD_DIM)
    ka = _heads(proj[:, SWA_Q:SWA_Q + SWA_KV], SWA_KV_HEADS)
    va = _heads(proj[:, SWA_Q + SWA_KV:SWA_COLS], SWA_KV_HEADS)
    sink_col, slope_col = _swa_consts(W["swa_sinks"])
    ya = swa_fwd(qa, ka, va, sink_col, slope_col)
    ya = _unheads(ya.reshape(SWA_HEADS, S, HEAD_DIM))
    hb = proj[:, SWA_COLS:]
    h, = rowwise(_f_mix, [hb, _shift_down(hb)], [W["rwkv_mu"]], [(hb.shape[1], F32)], "rwkv_shift")
    hr, hk, hv = h[:, :512], h[:, 512:1024], h[:, 1024:1536]
    hw, ha, hg = h[:, 1536:1600], h[:, 1600:1664], h[:, 1664:1792]
    bd = _block_diag_ones()
    pre_params = [W["rwkv_w0"], W["rwkv_w2"], W["rwkv_a0"], W["rwkv_a2"], W["rwkv_g2"], W["rwkv_k_k"],
                  W["rwkv_k_a"]]
    decay, k2, kk, b, g = rowwise(_f_rwkv_pre, [hk, hw, ha, hg], pre_params + [bd],
                                  [(RWKV_DIM, F32)] * 5, "rwkv_pre")
    vT = _to_colblocks(hv)
    y, ckpt, gathered = rwkv_scan_fwd(hr, decay, k2, kk, b, vT, gather_src)
    post_params = [W["rwkv_ln_w"], W["rwkv_ln_b"], W["rwkv_r_k"]]
    yb, = rowwise(_f_rwkv_post, [y, hr, k2, hv, g], post_params + [bd], [(RWKV_DIM, F32)], "rwkv_post")
    cat = jnp.concatenate([ya, yb], axis=1).astype(BF16)
    out = matmul(cat, W["even_w_out"], "nn", "emix_out", res=x)
    saved = (x, hn, qa, ka, va, sink_col, slope_col, hb, hr, hk, hv, hw, ha, hg, decay, k2, kk, b, g, vT,
             ckpt, y, cat)
    return out, saved, gathered


def even_mix_bwd(dy, saved, W, scatter_src):
    (x, hn, qa, ka, va, sink_col, slope_col, hb, hr, hk, hv, hw, ha, hg, decay, k2, kk, b, g, vT, ckpt, y,
     cat) = saved
    S = x.shape[0]
    grads = {}
    dcat = matmul(dy, W["even_w_out"], "nt", "emix_dcat")
    grads["even_w_out"] = matmul(cat, dy, "tn", "emix_dwout", out_dtype=BF16)
    dya, dyb = dcat[:, :SWA_Q], dcat[:, SWA_Q:]
    dya_h = _heads(dya, SWA_HEADS).reshape(SWA_KV_HEADS, SWA_GROUP, S, HEAD_DIM)
    dqa, dkp, dkc, dvp, dvc, dsink = swa_bwd(qa, ka, va, sink_col, slope_col, dya_h)
    shift_blk = lambda a: jnp.pad(a[:, BLOCK:], ((0, 0), (0, BLOCK), (0, 0)))
    dka = dkc + shift_blk(dkp)
    dva = dvc + shift_blk(dvp)
    grads["swa_sinks"] = dsink.reshape(SWA_HEADS, BLOCK).sum(axis=1).reshape(1, SWA_HEADS)
    dqa = _unheads(dqa.reshape(SWA_HEADS, S, HEAD_DIM))
    dka, dva = _unheads(dka), _unheads(dva)
    bd = _block_diag_ones()
    post_params = [W["rwkv_ln_w"], W["rwkv_ln_b"], W["rwkv_r_k"]]
    (d_y, d_r1, d_k2a, d_v1, d_g), (d_lnw, d_lnb, d_rk) = rowwise_vjp(
        _f_rwkv_post, [y, hr, k2, hv, g], post_params, [dyb], "rwkv_dpost", consts=[bd], tm=128)
    grads["rwkv_ln_w"], grads["rwkv_ln_b"], grads["rwkv_r_k"] = d_lnw, d_lnb, d_rk
    d_r2, d_w, d_k2b, d_kk, d_b, d_v2, exchanged = rwkv_scan_bwd(hr, decay, k2, kk, b, vT, _to_colblocks(d_y), ckpt,
                                                                  scatter_src)
    pre_params = [W["rwkv_w0"], W["rwkv_w2"], W["rwkv_a0"], W["rwkv_a2"], W["rwkv_g2"], W["rwkv_k_k"],
                  W["rwkv_k_a"]]
    (d_hk, d_hw, d_ha, d_hg), dpre = rowwise_vjp(
        _f_rwkv_pre, [hk, hw, ha, hg], pre_params, [d_w, d_k2a + d_k2b, d_kk, d_b, d_g], "rwkv_dpre",
        consts=[bd], tm=128)
    for nm, gval in zip(["rwkv_w0", "rwkv_w2", "rwkv_a0", "rwkv_a2", "rwkv_g2", "rwkv_k_k", "rwkv_k_a"], dpre):
        grads[nm] = gval
    d_h = jnp.concatenate([d_r1 + d_r2, d_hk, d_v1 + d_v2, d_hw, d_ha, d_hg], axis=1)
    (d_hb, d_sh), (d_mu,) = rowwise_vjp(_f_mix, [hb, _shift_down(hb)], [W["rwkv_mu"]], [d_h], "rwkv_dshift")
    grads["rwkv_mu"] = d_mu
    d_hb = d_hb + _shift_up(d_sh)
    dproj = jnp.concatenate([dqa, dka, dva, d_hb], axis=1).astype(BF16)
    grads["even_w_in"] = matmul(hn, dproj, "tn", "emix_dwin", out_dtype=BF16)
    dhn = matmul(dproj, W["even_w_in"], "nt", "emix_dhn")
    (dx,), (dgn,) = rowwise_vjp(_f_rms, [x], [W["mix_norm0"]], [dhn], "emix_drms")
    grads["mix_norm0"] = dgn
    return dy + dx, grads, exchanged


def odd_mix_fwd(x, W):
    S = x.shape[0]
    hn, = rowwise(_f_rms, [x], [W["mix_norm1"]], [(D_MODEL, BF16)], "omix_rms")
    proj = matmul(hn, W["fox_w_in"], "nn", "omix_in")
    q = _heads(proj[:, :FOX_DIM], FOX_HEADS).astype(BF16)
    k = _heads(proj[:, FOX_DIM:2 * FOX_DIM], FOX_HEADS).astype(BF16)
    v = _heads(proj[:, 2 * FOX_DIM:3 * FOX_DIM], FOX_HEADS).astype(BF16)
    fz = proj[:, 3 * FOX_DIM:]
    logf, = rowwise(_f_logf, [fz], [W["fox_b_f"]], [(128, F32)], "fox_logf")
    c = seq_cumsum(logf, False, "fox_cumsum")[:, :FOX_HEADS]
    c_col = c.T.reshape(FOX_HEADS, S, 1)
    c_row = c.T.reshape(FOX_HEADS, 1, S)
    o, lse = fox_fwd(q, k, v, c_col, c_row)
    yc = _unheads(o).astype(BF16)
    out = matmul(yc, W["fox_w_out"], "nn", "omix_out", res=x)
    return out, (x, hn, q, k, v, fz, c_col, c_row, o, lse, yc)


def odd_mix_bwd(dy, saved, W):
    x, hn, q, k, v, fz, c_col, c_row, o, lse, yc = saved
    S = x.shape[0]
    grads = {}
    dyc = matmul(dy, W["fox_w_out"], "nt", "omix_dyc")
    grads["fox_w_out"] = matmul(yc, dy, "tn", "omix_dwout", out_dtype=BF16)
    do = _heads(dyc, FOX_HEADS)
    dq, drow = fox_bwd_dq(q, k, v, c_col, c_row, o, lse, do)
    dk, dv, dcol = fox_bwd_dkv(q, k, v, c_col, c_row, o, lse, do)
    dc = (drow.reshape(FOX_HEADS, S) - dcol.reshape(FOX_HEADS, S)).T
    dc = jnp.pad(dc, ((0, 0), (0, 128 - FOX_HEADS)))
    dlogf = seq_cumsum(dc, True, "fox_rcumsum")
    (dfz,), (dbf,) = rowwise_vjp(_f_logf, [fz], [W["fox_b_f"]], [dlogf], "fox_dlogf")
    grads["fox_b_f"] = dbf
    dproj = jnp.concatenate([_unheads(dq), _unheads(dk), _unheads(dv), dfz], axis=1).astype(BF16)
    grads["fox_w_in"] = matmul(hn, dproj, "tn", "omix_dwin", out_dtype=BF16)
    dhn = matmul(dproj, W["fox_w_in"], "nt", "omix_dhn")
    (dx,), (dgn,) = rowwise_vjp(_f_rms, [x], [W["mix_norm1"]], [dhn], "omix_drms")
    grads["mix_norm1"] = dgn
    return dy + dx, grads


def device_step(x, p, target, W, gather_src, layer1_weights, layer1_grads):
    W = dict(W)
    saved = []
    h = x
    for i in range(2):
        h, s1 = ffn_fwd(h, W[f"ffn1_norm{i}"], W[f"ffn1_w_gu{i}"], W[f"ffn1_w_down{i}"], f"ffn1_{i}")
        if i == 0:
            h, s2, gathered = even_mix_fwd(h, W, gather_src)
            W.update(layer1_weights(gathered))
        else:
            h, s2 = odd_mix_fwd(h, W)
        h, s3 = ffn_fwd(h, W[f"ffn2_norm{i}"], W[f"ffn2_w_gu{i}"], W[f"ffn2_w_down{i}"], f"ffn2_{i}")
        h, s4 = ple_fwd(h, p[i], W[f"ple_norm{i}"], W[f"ple_w_gate{i}"], W[f"ple_w_proj{i}"], f"ple_{i}")
        saved.append((s1, s2, s3, s4))
    dh, d_final, loss = loss_head(h, target, W["final_norm"])
    G = {"final_norm": d_final}
    for i in (1, 0):
        s1, s2, s3, s4 = saved[i]
        dh, G[f"ple_norm{i}"], G[f"ple_w_gate{i}"], G[f"ple_w_proj{i}"] = ple_bwd(
            dh, s4, p[i], W[f"ple_norm{i}"], W[f"ple_w_gate{i}"], f"ple_{i}")
        dh, G[f"ffn2_norm{i}"], G[f"ffn2_w_gu{i}"], G[f"ffn2_w_down{i}"] = ffn_bwd(
            dh, s3, W[f"ffn2_norm{i}"], W[f"ffn2_w_gu{i}"], W[f"ffn2_w_down{i}"], f"ffn2_{i}")
        if i == 0:
            dh, gm, exchanged = even_mix_bwd(dh, s2, W, layer1_grads(G))
        else:
            dh, gm = odd_mix_bwd(dh, s2, W)
        G.update(gm)
        dh, G[f"ffn1_norm{i}"], G[f"ffn1_w_gu{i}"], G[f"ffn1_w_down{i}"] = ffn_bwd(
            dh, s1, W[f"ffn1_norm{i}"], W[f"ffn1_w_gu{i}"], W[f"ffn1_w_down{i}"], f"ffn1_{i}")
    return loss, dh, G, exchanged


_MESH = pl.DeviceIdType.MESH
_ANY = pl.BlockSpec(memory_space=pl.ANY)


def all_gather(x, name):
    def body(x_ref, out_ref, send_sems, recv_sems, local_sem):
        x_, y_, c_ = lax.axis_index("x"), lax.axis_index("y"), lax.axis_index("c")
        me, sibling = (x_, y_, c_), (x_, y_, 1 - c_)
        chips = [(1 - x_, y_), (x_, 1 - y_), (1 - x_, 1 - y_)]

        def slot(px, py, pc):
            return out_ref.at[4 * px + 2 * py + pc]

        def copy(k, block, to, src=None):
            return pltpu.make_async_remote_copy(
                src_ref=slot(*block) if src is None else src, dst_ref=slot(*block),
                send_sem=send_sems.at[k], recv_sem=recv_sems.at[k], device_id=to, device_id_type=_MESH)

        mine = pltpu.make_async_copy(x_ref, slot(*me), local_sem)
        mine.start()
        first = [copy(0, me, sibling, src=x_ref)]
        first += [copy(1 + j, me, (*chip, c_), src=x_ref) for j, chip in enumerate(chips)]
        for cp in first:
            cp.start()
        passed = [copy(4 + j, (*chip, c_), sibling) for j, chip in enumerate(chips)]
        for j, chip in enumerate(chips):
            copy(1 + j, (*chip, c_), me).wait_recv()
            passed[j].start()
        copy(0, sibling, me).wait_recv()
        for j, chip in enumerate(chips):
            copy(4 + j, (*chip, 1 - c_), me).wait_recv()
        for cp in first + passed:
            cp.wait_send()
        mine.wait()

    return pl.pallas_call(
        body,
        out_shape=jax.ShapeDtypeStruct((N_DEV,) + x.shape, x.dtype),
        in_specs=[_ANY],
        out_specs=_ANY,
        scratch_shapes=[pltpu.SemaphoreType.DMA((7,)), pltpu.SemaphoreType.DMA((7,)), pltpu.SemaphoreType.DMA(())],
        name=name,
    )(x)


def _direct_exchange(gather, s_ref, r_ref, send_sems, recv_sems, local_sem):
    x_, y_, c_ = lax.axis_index("x"), lax.axis_index("y"), lax.axis_index("c")
    my = 4 * x_ + 2 * y_ + c_
    copies = [pltpu.make_async_copy(s_ref if gather else s_ref.at[my], r_ref.at[my], local_sem)]
    for m in range(1, N_DEV):
        px = 1 - x_ if (m >> 2) & 1 else x_
        py = 1 - y_ if (m >> 1) & 1 else y_
        pc = 1 - c_ if m & 1 else c_
        copies.append(pltpu.make_async_remote_copy(
            src_ref=s_ref if gather else s_ref.at[4 * px + 2 * py + pc], dst_ref=r_ref.at[my],
            send_sem=send_sems.at[m - 1], recv_sem=recv_sems.at[m - 1],
            device_id=(px, py, pc), device_id_type=_MESH))
    return copies


_EXCHANGE_SEMS = [pltpu.SemaphoreType.DMA((7,)), pltpu.SemaphoreType.DMA((7,)), pltpu.SemaphoreType.DMA(())]


def _exchange_during(step, n_steps, gather, s_ref, r_ref, send_sems, recv_sems, local_sem):
    copies = _direct_exchange(gather, s_ref, r_ref, send_sems, recv_sems, local_sem)

    @pl.when(step == 0)
    def _():
        for cp in copies:
            cp.start()

    @pl.when(step == n_steps - 1)
    def _():
        for cp in copies:
            cp.wait()


def _exchange_out_shape(gather, src):
    return jax.ShapeDtypeStruct(((N_DEV,) + src.shape) if gather else src.shape, src.dtype)


def all_to_all(send, name):
    def body(s_ref, r_ref, send_sems, recv_sems, local_sem):
        copies = _direct_exchange(False, s_ref, r_ref, send_sems, recv_sems, local_sem)
        for cp in copies:
            cp.start()
        for cp in copies:
            cp.wait()

    return pl.pallas_call(
        body,
        out_shape=jax.ShapeDtypeStruct(send.shape, send.dtype),
        in_specs=[_ANY],
        out_specs=_ANY,
        scratch_shapes=[pltpu.SemaphoreType.DMA((7,)), pltpu.SemaphoreType.DMA((7,)), pltpu.SemaphoreType.DMA(())],
        name=name,
    )(send)


def adamw(w, m, v, parts, name, tm=256):
    R, C = w.shape
    tm = _pick(R, tm, 8) if R >= 8 else R

    def body(w_ref, m_ref, v_ref, p_ref, g_ref, d_ref, nm_ref, nv_ref):
        g = p_ref[0].astype(F32)
        for s in range(1, N_DEV):
            g = g + p_ref[s].astype(F32)
        nm = ADAM_B1 * m_ref[...] + (1.0 - ADAM_B1) * g
        nv = ADAM_B2 * v_ref[...] + (1.0 - ADAM_B2) * (g * g)
        m_hat = nm / (1.0 - ADAM_B1 ** ADAM_STEP)
        v_hat = nv / (1.0 - ADAM_B2 ** ADAM_STEP)
        g_ref[...] = g
        d_ref[...] = -ADAM_LR * (m_hat / (jnp.sqrt(v_hat) + ADAM_EPS) + ADAM_WD * w_ref[...])
        nm_ref[...] = nm
        nv_ref[...] = nv

    row = pl.BlockSpec((tm, C), lambda i: (i, 0))
    out = jax.ShapeDtypeStruct((R, C), F32)
    return pl.pallas_call(
        body,
        grid=(R // tm,),
        in_specs=[row, row, row, pl.BlockSpec((N_DEV, tm, C), lambda i: (0, i, 0))],
        out_specs=[row] * 4,
        out_shape=[out] * 4,
        compiler_params=_cparams(("parallel",)),
        name=name,
    )(w, m, v, parts)


_WEIGHTS = ["ffn1_norm", "ffn1_w_gu", "ffn1_w_down", "mix_norm", "ffn2_norm", "ffn2_w_gu", "ffn2_w_down",
            "ple_norm", "ple_w_gate", "ple_w_proj", "even_w_in", "even_w_out", "swa_sinks", "rwkv_mu",
            "rwkv_w0", "rwkv_w2", "rwkv_a0", "rwkv_a2", "rwkv_g2", "rwkv_k_k", "rwkv_k_a", "rwkv_r_k",
            "rwkv_ln_w", "rwkv_ln_b", "fox_w_in", "fox_b_f", "fox_w_out", "final_norm"]
_SHARD_AXIS = {"ffn1_w_gu": 2, "ffn1_w_down": 1, "ffn2_w_gu": 2, "ffn2_w_down": 1, "ple_w_gate": 1,
               "ple_w_proj": 2, "even_w_in": 2, "even_w_out": 1, "rwkv_w2": 2, "rwkv_a2": 2, "rwkv_g2": 2,
               "fox_w_in": 2, "fox_w_out": 1}
_SHARDED = [n for n in _WEIGHTS if n in _SHARD_AXIS]
_REPLICATED = [n for n in _WEIGHTS if n not in _SHARD_AXIS]
_PER_LAYER = ("ffn1_w_gu", "ffn1_w_down", "ffn2_w_gu", "ffn2_w_down", "ple_w_gate", "ple_w_proj")
_PIECES = [[(n, 0) for n in _PER_LAYER] + [(n, 0) for n in ("even_w_in", "even_w_out", "rwkv_w2", "rwkv_a2", "rwkv_g2")],
           [(n, 1) for n in _PER_LAYER] + [("fox_w_in", 0), ("fox_w_out", 0)]]
_PACK_LANES = 1024
_PACK_ROW_TILE = 256


def _piece_key(piece):
    name, idx = piece
    return f"{name}{idx}" if name in _PER_LAYER else name


def _pack_rows(arrs):
    flat = jnp.concatenate([a.reshape(-1, _PACK_LANES) for a in arrs], axis=0)
    return jnp.pad(flat, ((0, -flat.shape[0] % _PACK_ROW_TILE), (0, 0)))


def _unpack_rows(flat, shapes):
    out, r0 = [], 0
    for shp in shapes:
        n = math.prod(shp) // _PACK_LANES
        out.append(flat[r0:r0 + n].reshape(shp))
        r0 += n
    return out


def _unshard(gathered, pieces, shapes):
    full, r0 = {}, 0
    for piece, shp in zip(pieces, shapes):
        n = math.prod(shp) // _PACK_LANES
        seg = gathered[:, r0:r0 + n].reshape((N_DEV,) + shp)
        ax = _SHARD_AXIS[piece[0]] - 1
        seg = jnp.moveaxis(seg, 0, ax)
        full[_piece_key(piece)] = seg.reshape(shp[:ax] + (N_DEV * shp[ax],) + shp[ax + 1:])
        r0 += n
    return full


def _to_shards(full, pieces, shapes):
    segs = []
    for piece, shp in zip(pieces, shapes):
        ax = _SHARD_AXIS[piece[0]] - 1
        a = full[_piece_key(piece)].astype(BF16).reshape(shp[:ax] + (N_DEV, shp[ax]) + shp[ax + 1:])
        segs.append(jnp.moveaxis(a, ax, 0).reshape(N_DEV, -1, _PACK_LANES))
    flat = jnp.concatenate(segs, axis=1)
    return jnp.pad(flat, ((0, 0), (0, -flat.shape[1] % _PACK_ROW_TILE), (0, 0)))


def _layer_weights(full):
    W = dict(full)
    if "fox_w_in" in W:
        W["fox_w_in"] = jnp.pad(W["fox_w_in"], ((0, 0), (0, FOX_IN_PAD - W["fox_w_in"].shape[1])))
    for n in ("rwkv_w2", "rwkv_a2", "rwkv_g2"):
        if n in W:
            W[n] = W[n].astype(F32)
    return W


def _pack_small(vals):
    flat = jnp.concatenate([v.reshape(1, -1) for v in vals], axis=1)
    n = flat.shape[1]
    return jnp.pad(flat, ((0, 0), (0, -n % 128)))


def _unpack_small(flat, shapes):
    out, c0 = [], 0
    for shp in shapes:
        n = math.prod(shp)
        out.append(flat[0, c0:c0 + n].reshape(shp))
        c0 += n
    return out


def kernel(x, p, ffn1_norm, ffn1_w_gu, ffn1_w_down, mix_norm, ffn2_norm, ffn2_w_gu, ffn2_w_down, ple_norm, ple_w_gate, ple_w_proj, even_w_in, even_w_out, swa_sinks, rwkv_mu, rwkv_w0, rwkv_w2, rwkv_a0, rwkv_a2, rwkv_g2, rwkv_k_k, rwkv_k_a, rwkv_r_k, rwkv_ln_w, rwkv_ln_b, fox_w_in, fox_b_f, fox_w_out, final_norm, loss_target, m_ffn1_norm, m_ffn1_w_gu, m_ffn1_w_down, m_mix_norm, m_ffn2_norm, m_ffn2_w_gu, m_ffn2_w_down, m_ple_norm, m_ple_w_gate, m_ple_w_proj, m_even_w_in, m_even_w_out, m_swa_sinks, m_rwkv_mu, m_rwkv_w0, m_rwkv_w2, m_rwkv_a0, m_rwkv_a2, m_rwkv_g2, m_rwkv_k_k, m_rwkv_k_a, m_rwkv_r_k, m_rwkv_ln_w, m_rwkv_ln_b, m_fox_w_in, m_fox_b_f, m_fox_w_out, m_final_norm, v_ffn1_norm, v_ffn1_w_gu, v_ffn1_w_down, v_mix_norm, v_ffn2_norm, v_ffn2_w_gu, v_ffn2_w_down, v_ple_norm, v_ple_w_gate, v_ple_w_proj, v_even_w_in, v_even_w_out, v_swa_sinks, v_rwkv_mu, v_rwkv_w0, v_rwkv_w2, v_rwkv_a0, v_rwkv_a2, v_rwkv_g2, v_rwkv_k_k, v_rwkv_k_a, v_rwkv_r_k, v_rwkv_ln_w, v_rwkv_ln_b, v_fox_w_in, v_fox_b_f, v_fox_w_out, v_final_norm):
    given = dict(locals())
    w = {n: given[n] for n in _WEIGHTS}
    m = {n: given["m_" + n] for n in _WEIGHTS}
    v = {n: given["v_" + n] for n in _WEIGHTS}
    small_shapes = [w[n].shape for n in _REPLICATED]
    piece = lambda d, pc: d[pc[0]][pc[1]]
    shapes = [[piece(w, pc).shape for pc in pieces] for pieces in _PIECES]
    w_rows = [_pack_rows([piece(w, pc) for pc in pieces]) for pieces in _PIECES]

    W = _layer_weights(_unshard(all_gather(w_rows[0].astype(BF16), "weights_all_gather"), _PIECES[0], shapes[0]))
    for i in range(2):
        for n in ("ffn1_norm", "mix_norm", "ffn2_norm", "ple_norm"):
            W[f"{n}{i}"] = w[n][i].reshape(1, -1)
    for n in ("swa_sinks", "rwkv_mu", "rwkv_w0", "rwkv_a0", "rwkv_k_k", "rwkv_k_a", "rwkv_r_k", "rwkv_ln_w",
              "rwkv_ln_b", "final_norm"):
        W[n] = w[n].reshape(1, -1)
    n_f = fox_b_f.shape[1]
    W["fox_b_f"] = jnp.pad(fox_b_f.reshape(1, n_f), ((0, 0), (0, 128 - n_f)))
    n_fox = fox_w_in.shape[2] * N_DEV

    def layer1_weights(gathered):
        return _layer_weights(_unshard(gathered, _PIECES[1], shapes[1]))

    def layer1_grads(G):
        G = dict(G, fox_w_in=G["fox_w_in"][:, :n_fox])
        return _to_shards(G, _PIECES[1], shapes[1])

    loss_row, dx, G, parts1 = device_step(x[0], p[:, 0], loss_target[0], W, w_rows[1].astype(BF16),
                                          layer1_weights, layer1_grads)

    parts = [all_to_all(_to_shards(G, _PIECES[0], shapes[0]), "grads_all_to_all"), parts1]
    out_g, out_d, out_m, out_v = {}, {}, {}, {}
    for li, pieces in enumerate(_PIECES):
        res = adamw(w_rows[li], _pack_rows([piece(m, pc) for pc in pieces]),
                    _pack_rows([piece(v, pc) for pc in pieces]), parts[li], f"adamw_sharded{li}")
        for out, rows in zip((out_g, out_d, out_m, out_v), res):
            for pc, a in zip(pieces, _unpack_rows(rows, shapes[li])):
                out.setdefault(pc[0], {})[pc[1]] = a
    for out in (out_g, out_d, out_m, out_v):
        for n in _SHARDED:
            out[n] = jnp.stack([out[n][i] for i in sorted(out[n])])

    gsmall = {}
    for n in ("ffn1_norm", "mix_norm", "ffn2_norm", "ple_norm"):
        gsmall[n] = jnp.concatenate([G[f"{n}0"], G[f"{n}1"]], axis=0)
    for n in ("swa_sinks", "rwkv_mu", "rwkv_w0", "rwkv_a0", "rwkv_k_k", "rwkv_k_a", "rwkv_r_k", "rwkv_ln_w",
              "rwkv_ln_b", "final_norm"):
        gsmall[n] = G[n]
    gsmall["fox_b_f"] = G["fox_b_f"][:, :n_f]
    small = _pack_small([gsmall[n] for n in _REPLICATED] + [loss_row[:, :1]])
    small_parts = all_gather(small, "small_all_gather")
    pad1 = lambda vals: _pack_small(vals + [jnp.zeros((1, 1), F32)])
    gs, ds, nms, nvs = adamw(pad1([w[n] for n in _REPLICATED]), pad1([m[n] for n in _REPLICATED]),
                             pad1([v[n] for n in _REPLICATED]), small_parts, "adamw_replicated")
    out_g.update(zip(_REPLICATED, _unpack_small(gs, small_shapes)))
    out_d.update(zip(_REPLICATED, _unpack_small(ds, small_shapes)))
    out_m.update(zip(_REPLICATED, _unpack_small(nms, small_shapes)))
    out_v.update(zip(_REPLICATED, _unpack_small(nvs, small_shapes)))
    n_small = sum(math.prod(s) for s in small_shapes)
    loss = gs[0, n_small]

    return (loss, dx[None], *[out_g[n] for n in _WEIGHTS], *[out_d[n] for n in _WEIGHTS],
            *[out_m[n] for n in _WEIGHTS], *[out_v[n] for n in _WEIGHTS])
```

```python
import functools
import math

import numpy as np
import jax
import jax.numpy as jnp
from jax import lax
from jax.experimental import pallas as pl
from jax.experimental.pallas import tpu as pltpu

F32 = jnp.float32
BF16 = jnp.bfloat16

D_MODEL = 1024
HEAD_DIM = 64
BLOCK = 128
SWA_HEADS = 8
SWA_KV_HEADS = 2
SWA_GROUP = 4
RWKV_HEADS = 8
RWKV_DIM = 512
FOX_HEADS = 16
FOX_DIM = 1024
D_FF = 2816
NORM_EPS = 1e-6
GN_EPS = 64e-5
L2_EPS = 1e-12
SWA_Q = 512
SWA_KV = 128
SWA_COLS = 768
FOX_IN_PAD = 3200
N_DEV = 8
ADAM_LR = 0.001
ADAM_B1 = 0.9
ADAM_B2 = 0.999
ADAM_EPS = 1e-08
ADAM_WD = 0.01
ADAM_STEP = 10

V7X_VMEM_LIMIT = 56 * 1024 * 1024
SCAN_GROUP = 8
SCAN_CHUNK = 32

_NN = (((1,), (0,)), ((), ()))
_NT = (((1,), (1,)), ((), ()))
_TN = (((0,), (0,)), ((), ()))
_DIMS = {"nn": _NN, "nt": _NT, "tn": _TN}


def _pick(n, target, mult=128):
    best = None
    for t in range(mult, min(n, target) + 1, mult):
        if n % t == 0:
            best = t
    return best or n


def _cparams(sem):
    return pltpu.CompilerParams(dimension_semantics=sem, vmem_limit_bytes=V7X_VMEM_LIMIT)


def _dot(a, b, dims):
    return lax.dot_general(a.astype(BF16), b.astype(BF16), dims, preferred_element_type=F32)


@jax.custom_vjp
def bdot(a, b):
    return _dot(a, b, _NN)


def _bdot_fwd(a, b):
    return _dot(a, b, _NN), (a, b)


def _bdot_bwd(res, g):
    a, b = res
    return _dot(g, b, _NT), _dot(a, g, _TN)


bdot.defvjp(_bdot_fwd, _bdot_bwd)


@jax.custom_vjp
def bdot_nt(a, b):
    return _dot(a, b, _NT)


def _bdot_nt_fwd(a, b):
    return _dot(a, b, _NT), (a, b)


def _bdot_nt_bwd(res, g):
    a, b = res
    return _dot(g, b, _NN), _dot(g, a, _TN)


bdot_nt.defvjp(_bdot_nt_fwd, _bdot_nt_bwd)


def _segsum(x, bd):
    return jnp.dot(x, bd, precision=lax.Precision.HIGHEST, preferred_element_type=F32)


def _sigmoid(x):
    return 1.0 / (1.0 + jnp.exp(-x))


def _sigmoid_tanh(x):
    return 0.5 * jnp.tanh(0.5 * x) + 0.5


def _softplus(x):
    return jnp.maximum(x, 0.0) + jnp.log(1.0 + jnp.exp(-jnp.abs(x)))


def matmul(a, b, mode, name, out_dtype=F32, scale=1.0, res=None, tm=512, tn=1408, tk=1024):
    if mode == "nn":
        (M, K), (K2, N) = a.shape, b.shape
    elif mode == "nt":
        (M, K), (N, K2) = a.shape, b.shape
    else:
        (K, M), (K2, N) = a.shape, b.shape
    assert K == K2, (a.shape, b.shape, mode)
    tm, tn, tk = _pick(M, tm), _pick(N, tn), _pick(K, tk)
    nk = K // tk
    has_res = res is not None

    def body(*refs):
        if has_res:
            a_ref, b_ref, r_ref, o_ref, acc = refs
        else:
            a_ref, b_ref, o_ref, acc = refs
        kk = pl.program_id(2)

        @pl.when(kk == 0)
        def _():
            acc[...] = jnp.zeros_like(acc)

        acc[...] += _dot(a_ref[...], b_ref[...], _DIMS[mode])

        @pl.when(kk == nk - 1)
        def _():
            v = acc[...]
            if scale != 1.0:
                v = v * scale
            if has_res:
                v = v + r_ref[...].astype(F32)
            o_ref[...] = v.astype(out_dtype)

    if mode == "tn":
        a_spec = pl.BlockSpec((tk, tm), lambda i, j, k: (k, i))
    else:
        a_spec = pl.BlockSpec((tm, tk), lambda i, j, k: (i, k))
    if mode == "nt":
        b_spec = pl.BlockSpec((tn, tk), lambda i, j, k: (j, k))
    else:
        b_spec = pl.BlockSpec((tk, tn), lambda i, j, k: (k, j))
    o_spec = pl.BlockSpec((tm, tn), lambda i, j, k: (i, j))
    in_specs = [a_spec, b_spec] + ([o_spec] if has_res else [])
    args = (a, b) + ((res,) if has_res else ())
    return pl.pallas_call(
        body,
        grid=(M // tm, N // tn, nk),
        in_specs=in_specs,
        out_specs=o_spec,
        out_shape=jax.ShapeDtypeStruct((M, N), out_dtype),
        scratch_shapes=[pltpu.VMEM((tm, tn), F32)],
        compiler_params=_cparams(("parallel", "parallel", "arbitrary")),
        name=name,
    )(*args)


def _row_spec(r, tm):
    if isinstance(r, tuple):
        arr, width, blk = r
        return arr, pl.BlockSpec((tm, width), lambda i, blk=blk: (i, blk))
    return r, pl.BlockSpec((tm, r.shape[1]), lambda i: (i, 0))


def _whole_spec(p):
    return pl.BlockSpec(p.shape, lambda i: (0,) * p.ndim)


def rowwise(fn, rows, params, outs, name, tm=256):
    arrs, specs = zip(*[_row_spec(r, tm) for r in rows])
    S = arrs[0].shape[0]
    tm = min(tm, S)
    arrs, specs = zip(*[_row_spec(r, tm) for r in rows])
    n_in = len(rows) + len(params)

    def body(*refs):
        res = fn(*[r[...] for r in refs[:n_in]])
        for o_ref, v in zip(refs[n_in:], res):
            o_ref[...] = v.astype(o_ref.dtype)

    return pl.pallas_call(
        body,
        grid=(S // tm,),
        in_specs=list(specs) + [_whole_spec(p) for p in params],
        out_specs=[pl.BlockSpec((tm, c), lambda i: (i, 0)) for c, _ in outs],
        out_shape=[jax.ShapeDtypeStruct((S, c), dt) for c, dt in outs],
        compiler_params=_cparams(("parallel",)),
        name=name,
    )(*arrs, *params)


def rowwise_vjp(fn, rows, params, cots, name, need=None, row_dtype=F32, consts=(), tm=256):
    nr, npar, nc, nk = len(rows), len(params), len(cots), len(consts)
    need = [True] * nr if need is None else need
    arrs, _ = zip(*[_row_spec(r, tm) for r in rows])
    S = arrs[0].shape[0]
    tm = min(tm, S)
    arrs, specs = zip(*[_row_spec(r, tm) for r in rows])
    carrs, cspecs = zip(*[_row_spec(c, tm) for c in cots])
    widths = [s.block_shape[1] for s in specs]
    n_in = nr + npar + nk + nc

    def body(*refs):
        i = pl.program_id(0)
        xs = [r[...].astype(F32) for r in refs[:nr]]
        ps = [r[...] for r in refs[nr:nr + npar]]
        ks = [r[...] for r in refs[nr + npar:nr + npar + nk]]
        cs = [r[...].astype(F32) for r in refs[nr + npar + nk:n_in]]
        outs, vjp = jax.vjp(lambda *a: fn(*a, *ks), *xs, *ps)
        grads = vjp(tuple(cs))
        o = n_in
        for j in range(nr):
            if need[j]:
                refs[o][...] = grads[j].astype(refs[o].dtype)
                o += 1
        for j in range(npar):
            g_ref = refs[o + j]

            @pl.when(i == 0)
            def _(g_ref=g_ref):
                g_ref[...] = jnp.zeros_like(g_ref)

            g_ref[...] += grads[nr + j]

    out_specs = [pl.BlockSpec((tm, w), lambda i: (i, 0)) for w, nd in zip(widths, need) if nd]
    out_shape = [jax.ShapeDtypeStruct((S, w), row_dtype) for w, nd in zip(widths, need) if nd]
    out_specs += [_whole_spec(p) for p in params]
    out_shape += [jax.ShapeDtypeStruct(p.shape, F32) for p in params]
    res = pl.pallas_call(
        body,
        grid=(S // tm,),
        in_specs=list(specs) + [_whole_spec(p) for p in params] + [_whole_spec(k) for k in consts] + list(cspecs),
        out_specs=out_specs,
        out_shape=out_shape,
        compiler_params=_cparams(("arbitrary",)),
        name=name,
    )(*arrs, *params, *consts, *carrs)
    nrow = sum(need)
    return list(res[:nrow]), list(res[nrow:])


def _rms(x, g):
    return x * lax.rsqrt(jnp.mean(x * x, axis=-1, keepdims=True) + NORM_EPS) * g


def _f_rms(x, g):
    return (_rms(x, g),)


def _f_rms_res(x, g):
    return _rms(x, g), x


def _f_ple(x, z, pp):
    return (x + _sigmoid(z) * pp,)


def _f_mix(h, sh, mu):
    return (h + (sh - h) * mu,)


def _f_logf(fz, bf):
    return (-_softplus(-(fz + bf)),)


def _f_rwkv_pre(hk, hw, ha, hg, w0, w2, a0, a2, g2, k_k, k_a, bd):
    wlog = -_softplus(-(w0 + bdot(jnp.tanh(hw), w2))) - 0.5
    a = _sigmoid(a0 + bdot(ha, a2))
    g = bdot(_sigmoid(hg), g2)
    kk = hk * k_k
    kk = kk / jnp.maximum(jnp.sqrt(_segsum(kk * kk, bd)), L2_EPS)
    k2 = hk * (1.0 + (a - 1.0) * k_a)
    decay = jnp.exp(-jnp.exp(wlog))
    return decay, k2, kk, kk * a, g


def _f_rwkv_post(y, r, k2, v, g, ln_w, ln_b, r_k, bd):
    mean = _segsum(y, bd) * (1.0 / HEAD_DIM)
    d = y - mean
    var = _segsum(d * d, bd) * (1.0 / HEAD_DIM)
    yn = d * lax.rsqrt(var + GN_EPS) * ln_w + ln_b
    yn = yn + _segsum(r * k2 * r_k, bd) * v
    return (yn * g,)


def loss_head(x, target, gf, tm=256):
    S, D = x.shape
    tm = min(tm, S)

    def f(xt, g, tt):
        err = _rms(xt, g) - tt
        return 0.5 * jnp.sum(err * err) * (1.0 / D)

    def body(x_ref, t_ref, g_ref, dx_ref, dg_ref, l_ref):
        i = pl.program_id(0)
        val, (dx, dg) = jax.value_and_grad(f, argnums=(0, 1))(x_ref[...], g_ref[...], t_ref[...])

        @pl.when(i == 0)
        def _():
            dg_ref[...] = jnp.zeros_like(dg_ref)
            l_ref[...] = jnp.zeros_like(l_ref)

        dx_ref[...] = dx
        dg_ref[...] += dg
        l_ref[...] += jnp.full(l_ref.shape, val, F32)

    row = pl.BlockSpec((tm, D), lambda i: (i, 0))
    vec = pl.BlockSpec((1, D), lambda i: (0, 0))
    return pl.pallas_call(
        body,
        grid=(S // tm,),
        in_specs=[row, row, vec],
        out_specs=[row, vec, pl.BlockSpec((1, 128), lambda i: (0, 0))],
        out_shape=[jax.ShapeDtypeStruct((S, D), F32), jax.ShapeDtypeStruct((1, D), F32),
                   jax.ShapeDtypeStruct((1, 128), F32)],
        compiler_params=_cparams(("arbitrary",)),
        name="loss_head",
    )(x, target, gf)


def _swa_block(q, kp, kc, vp, vc, sink, slope, n):
    k = jnp.concatenate([kp, kc], axis=0)
    v = jnp.concatenate([vp, vc], axis=0)
    rows = q.shape[0]
    logits = bdot_nt(q, k) * (HEAD_DIM ** -0.5)
    qi = lax.broadcasted_iota(jnp.int32, (rows, 2 * BLOCK), 0) & (BLOCK - 1)
    ki = lax.broadcasted_iota(jnp.int32, (rows, 2 * BLOCK), 1)
    dist = qi + BLOCK - ki
    valid = (dist >= 0) & (dist < BLOCK) & ((n - 1) * BLOCK + ki >= 0)
    logits = logits - slope * dist.astype(F32)
    logits = jnp.where(valid, logits, -jnp.inf)
    m = jnp.maximum(jnp.max(logits, axis=-1, keepdims=True), sink)
    pr = jnp.exp(logits - m)
    denom = jnp.sum(pr, axis=-1, keepdims=True) + jnp.exp(sink - m)
    return bdot(pr / denom, v)


def _swa_specs(S):
    nb = S // BLOCK
    q_spec = pl.BlockSpec((None, SWA_GROUP, BLOCK, HEAD_DIM), lambda h, n: (h, 0, n, 0))
    kc_spec = pl.BlockSpec((None, BLOCK, HEAD_DIM), lambda h, n: (h, n, 0))
    kp_spec = pl.BlockSpec((None, BLOCK, HEAD_DIM), lambda h, n: (h, jnp.maximum(n - 1, 0), 0))
    col_spec = pl.BlockSpec((None, SWA_GROUP * BLOCK, 1), lambda h, n: (h, 0, 0))
    return nb, q_spec, kp_spec, kc_spec, col_spec


def swa_fwd(q, k, v, sink_col, slope_col):
    S = q.shape[2]
    nb, q_spec, kp_spec, kc_spec, col_spec = _swa_specs(S)

    def body(q_ref, kp_ref, kc_ref, vp_ref, vc_ref, s_ref, a_ref, o_ref):
        n = pl.program_id(1)
        qq = q_ref[...].reshape(SWA_GROUP * BLOCK, HEAD_DIM)
        out = _swa_block(qq, kp_ref[...], kc_ref[...], vp_ref[...], vc_ref[...], s_ref[...], a_ref[...], n)
        o_ref[...] = out.reshape(SWA_GROUP, BLOCK, HEAD_DIM)

    return pl.pallas_call(
        body,
        grid=(SWA_KV_HEADS, nb),
        in_specs=[q_spec, kp_spec, kc_spec, kp_spec, kc_spec, col_spec, col_spec],
        out_specs=q_spec,
        out_shape=jax.ShapeDtypeStruct(q.shape, F32),
        compiler_params=_cparams(("parallel", "parallel")),
        name="swa_fwd",
    )(q, k, k, v, v, sink_col, slope_col)


def swa_bwd(q, k, v, sink_col, slope_col, dout):
    S = q.shape[2]
    nb, q_spec, kp_spec, kc_spec, col_spec = _swa_specs(S)

    def body(q_ref, kp_ref, kc_ref, vp_ref, vc_ref, s_ref, a_ref, do_ref,
             dq_ref, dkp_ref, dkc_ref, dvp_ref, dvc_ref, ds_ref):
        n = pl.program_id(1)
        qq = q_ref[...].reshape(SWA_GROUP * BLOCK, HEAD_DIM)
        slope = a_ref[...]
        f = lambda a, b, c, d, e, s: _swa_block(a, b, c, d, e, s, slope, n)
        _, vjp = jax.vjp(f, qq, kp_ref[...], kc_ref[...], vp_ref[...], vc_ref[...], s_ref[...])
        dq, dkp, dkc, dvp, dvc, ds = vjp(do_ref[...].reshape(SWA_GROUP * BLOCK, HEAD_DIM))
        dq_ref[...] = dq.reshape(SWA_GROUP, BLOCK, HEAD_DIM)
        dkp_ref[...] = dkp
        dkc_ref[...] = dkc
        dvp_ref[...] = dvp
        dvc_ref[...] = dvc

        @pl.when(n == 0)
        def _():
            ds_ref[...] = jnp.zeros_like(ds_ref)

        ds_ref[...] += ds

    kv_shape = jax.ShapeDtypeStruct(k.shape, F32)
    return pl.pallas_call(
        body,
        grid=(SWA_KV_HEADS, nb),
        in_specs=[q_spec, kp_spec, kc_spec, kp_spec, kc_spec, col_spec, col_spec, q_spec],
        out_specs=[q_spec, kc_spec, kc_spec, kc_spec, kc_spec, col_spec],
        out_shape=[jax.ShapeDtypeStruct(q.shape, F32), kv_shape, kv_shape, kv_shape, kv_shape,
                   jax.ShapeDtypeStruct(sink_col.shape, F32)],
        compiler_params=_cparams(("parallel", "arbitrary")),
        name="swa_bwd",
    )(q, k, k, v, v, sink_col, slope_col, dout)


def _split2(x):
    hi = x.astype(BF16)
    return (x - hi.astype(F32)).astype(BF16), hi


def _dot2(x, m):
    lo, hi = _split2(x)
    return jnp.dot(lo, m, preferred_element_type=F32) + jnp.dot(hi, m, preferred_element_type=F32)


def _seg_sum(x, bd):
    w = bd.shape[0]
    return jnp.concatenate([_dot2(x[:, i:i + w], bd) for i in range(0, x.shape[1], w)], axis=1)


def _scan_consts():
    r = np.arange(256)
    bd = (r[:, None] // HEAD_DIM == r[None, :] // HEAD_DIM).astype(np.float32)
    c = np.arange(RWKV_DIM)
    e = (np.arange(HEAD_DIM)[:, None] // SCAN_GROUP == c[None, :] // HEAD_DIM).astype(np.float32)
    diag = (np.arange(HEAD_DIM)[:, None] == c[None, :] % HEAD_DIM).astype(np.float32)
    return jnp.asarray(bd, BF16), jnp.asarray(e, BF16), jnp.asarray(diag, F32)


def _to_colblocks(a):
    S = a.shape[0]
    a = a.reshape(S // SCAN_GROUP, SCAN_GROUP, RWKV_HEADS, HEAD_DIM)
    return a.transpose(0, 3, 2, 1).reshape(S // SCAN_GROUP, HEAD_DIM, RWKV_HEADS * SCAN_GROUP)


def _scan_step(St, tt, base, col_g, lane_t, kk_ref, w_ref, b_ref, k_ref, bd, e):
    row = lambda ref: ref[pl.ds(base + tt, 1), :]
    u_b = _seg_sum(St * row(kk_ref), bd)
    v_b = _dot2(jnp.where(lane_t == tt, col_g, 0.0), e)
    return St * row(w_ref) - u_b * row(b_ref) + v_b * row(k_ref), u_b, v_b


def rwkv_scan_fwd(r, w, k, kk, b, vB, gather_src):
    S, C = r.shape
    N, G = HEAD_DIM, SCAN_GROUP
    chunk = min(SCAN_CHUNK, S)
    nchunk, ng = S // chunk, chunk // G
    bd, e, diag = _scan_consts()

    def body(r_ref, w_ref, k_ref, kk_ref, b_ref, vB_ref, bd_ref, e_ref, dg_ref, xs_ref, y_ref, ck_ref, xr_ref,
             S_ref, send_sems, recv_sems, local_sem):
        c = pl.program_id(0)
        _exchange_during(c, nchunk, True, xs_ref, xr_ref, send_sems, recv_sems, local_sem)

        @pl.when(c == 0)
        def _():
            S_ref[...] = jnp.zeros_like(S_ref)

        ck_ref[...] = S_ref[...]
        sub = lax.broadcasted_iota(jnp.int32, (G, C), 0)
        lane_t = lax.broadcasted_iota(jnp.int32, (N, N), 1) & (G - 1)

        def group(g, St):
            base = pl.multiple_of(g * G, G)
            vb = vB_ref[g]
            ys = jnp.zeros((G, C), F32)
            def emit(ys, S_t, tt):
                y_b = _seg_sum(S_t * r_ref[pl.ds(base + tt, 1), :], bd_ref[...])
                return jnp.where(sub == tt, jnp.sum(y_b * dg_ref[...], axis=0, keepdims=True), ys)

            for tt in range(G):
                S_new, _, _ = _scan_step(St, tt, base, vb, lane_t, kk_ref, w_ref, b_ref, k_ref, bd_ref[...],
                                         e_ref[...])
                if tt > 0:
                    ys = emit(ys, St, tt - 1)
                St = S_new
            y_ref[pl.ds(base, G), :] = emit(ys, St, G - 1)
            return St

        S_ref[...] = lax.fori_loop(0, ng, group, S_ref[...])

    row = pl.BlockSpec((chunk, C), lambda c: (c, 0))
    col = pl.BlockSpec((ng, N, N), lambda c: (c, 0, 0))
    return pl.pallas_call(
        body,
        grid=(nchunk,),
        in_specs=[row] * 5 + [col, _whole_spec(bd), _whole_spec(e), _whole_spec(diag), _ANY],
        out_specs=[row, pl.BlockSpec((None, N, C), lambda c: (c, 0, 0)), _ANY],
        out_shape=[jax.ShapeDtypeStruct((S, C), F32), jax.ShapeDtypeStruct((nchunk, N, C), F32),
                   _exchange_out_shape(True, gather_src)],
        scratch_shapes=[pltpu.VMEM((N, C), F32)] + _EXCHANGE_SEMS,
        compiler_params=_cparams(("arbitrary",)),
        name="rwkv_scan_fwd",
    )(r, w, k, kk, b, vB, bd, e, diag, gather_src)


def rwkv_scan_bwd(r, w, k, kk, b, vB, dyB, ckpt, scatter_src):
    S, C = r.shape
    N, G = HEAD_DIM, SCAN_GROUP
    chunk = min(SCAN_CHUNK, S)
    nchunk, ng = S // chunk, chunk // G
    bd, e, diag = _scan_consts()

    def body(r_ref, w_ref, k_ref, kk_ref, b_ref, vB_ref, dyB_ref, ck_ref, bd_ref, e_ref, dg_ref, xs_ref,
             dr_ref, dw_ref, dk_ref, dkk_ref, db_ref, dv_ref, xr_ref, G_ref, sbuf, ubuf, vbuf,
             send_sems, recv_sems, local_sem):
        c = pl.program_id(0)
        _exchange_during(c, nchunk, False, xs_ref, xr_ref, send_sems, recv_sems, local_sem)

        @pl.when(c == 0)
        def _():
            G_ref[...] = jnp.zeros_like(G_ref)

        lane_t = lax.broadcasted_iota(jnp.int32, (N, N), 1) & (G - 1)
        sub = lax.broadcasted_iota(jnp.int32, (G, C), 0)

        def fgroup(g, St):
            base = pl.multiple_of(g * G, G)
            vb = vB_ref[g]
            for tt in range(G):
                sbuf[base + tt] = St
                St, u_b, v_b = _scan_step(St, tt, base, vb, lane_t, kk_ref, w_ref, b_ref, k_ref, bd_ref[...],
                                          e_ref[...])
                ubuf[base + tt] = u_b
                vbuf[base + tt] = v_b
            return St

        sbuf[chunk] = lax.fori_loop(0, ng, fgroup, ck_ref[...])

        def bgroup(gi, Gt):
            g = ng - 1 - gi
            base = pl.multiple_of(g * G, G)
            dyb = dyB_ref[g]
            rows = [jnp.zeros((G, C), F32) for _ in range(6)]
            colsum = lambda a: jnp.sum(a, axis=0, keepdims=True)
            def emit(rows, tt, Gt, du_b, dy_b):
                Sp, Sc = sbuf[base + tt], sbuf[base + tt + 1]
                d_r = colsum(Sc * dy_b)
                d_w = colsum(Gt * Sp)
                d_b = -colsum(Gt * ubuf[base + tt])
                d_v = colsum(_seg_sum(Gt * k_ref[pl.ds(base + tt, 1), :], bd_ref[...]) * dg_ref[...])
                d_k = colsum(Gt * vbuf[base + tt])
                d_kk = colsum(Sp * du_b)
                return [jnp.where(sub == tt, new, acc) for new, acc in zip((d_r, d_w, d_k, d_kk, d_b, d_v), rows)]

            pending = None
            for tt in reversed(range(G)):
                row = lambda ref: ref[pl.ds(base + tt, 1), :]
                dy_b = _dot2(jnp.where(lane_t == tt, dyb, 0.0), e_ref[...])
                Gt = Gt + dy_b * row(r_ref)
                du_b = -_seg_sum(Gt * row(b_ref), bd_ref[...])
                G_next = Gt * row(w_ref) + du_b * row(kk_ref)
                if pending is not None:
                    rows = emit(rows, *pending)
                pending = (tt, Gt, du_b, dy_b)
                Gt = G_next
            rows = emit(rows, *pending)
            for ref, val in zip((dr_ref, dw_ref, dk_ref, dkk_ref, db_ref, dv_ref), rows):
                ref[pl.ds(base, G), :] = val
            return Gt

        G_ref[...] = lax.fori_loop(0, ng, bgroup, G_ref[...])

    rev = lambda c: nchunk - 1 - c
    row = pl.BlockSpec((chunk, C), lambda c: (rev(c), 0))
    col = pl.BlockSpec((ng, N, N), lambda c: (rev(c), 0, 0))
    rshape = jax.ShapeDtypeStruct((S, C), F32)
    return pl.pallas_call(
        body,
        grid=(nchunk,),
        in_specs=[row] * 5 + [col, col, pl.BlockSpec((None, N, C), lambda c: (rev(c), 0, 0)),
                              _whole_spec(bd), _whole_spec(e), _whole_spec(diag), _ANY],
        out_specs=[row] * 6 + [_ANY],
        out_shape=[rshape] * 6 + [_exchange_out_shape(False, scatter_src)],
        scratch_shapes=[pltpu.VMEM((N, C), F32), pltpu.VMEM((chunk + 1, N, C), F32),
                        pltpu.VMEM((chunk, N, C), F32), pltpu.VMEM((chunk, N, C), F32)] + _EXCHANGE_SEMS,
        compiler_params=_cparams(("arbitrary",)),
        name="rwkv_scan_bwd",
    )(r, w, k, kk, b, vB, dyB, ckpt, bd, e, diag, scatter_src)


def seq_cumsum(x, reverse, name):
    S, C = x.shape
    tb = min(256, S)
    nb = S // tb

    def body(x_ref, o_ref, carry):
        i = pl.program_id(0)

        @pl.when(i == 0)
        def _():
            carry[...] = jnp.zeros_like(carry)

        ri = lax.broadcasted_iota(jnp.int32, (tb, tb), 0)
        ci = lax.broadcasted_iota(jnp.int32, (tb, tb), 1)
        tri = jnp.where((ci >= ri) if reverse else (ci <= ri), 1.0, 0.0).astype(F32)
        xb = x_ref[...]
        out = jnp.dot(tri, xb, precision=lax.Precision.HIGHEST, preferred_element_type=F32) + carry[...]
        o_ref[...] = out
        carry[...] = carry[...] + jnp.sum(xb, axis=0, keepdims=True)

    idx = (lambda i: (nb - 1 - i, 0)) if reverse else (lambda i: (i, 0))
    return pl.pallas_call(
        body,
        grid=(nb,),
        in_specs=[pl.BlockSpec((tb, C), idx)],
        out_specs=pl.BlockSpec((tb, C), idx),
        out_shape=jax.ShapeDtypeStruct((S, C), F32),
        scratch_shapes=[pltpu.VMEM((1, C), F32)],
        compiler_params=_cparams(("arbitrary",)),
        name=name,
    )(x)


def _fox_logits(q, k, cq, ck, qi, ki, tq, tk):
    s = _dot(q, k, _NT) * (HEAD_DIM ** -0.5) + cq - ck
    row = qi * tq + lax.broadcasted_iota(jnp.int32, (tq, tk), 0)
    col = ki * tk + lax.broadcasted_iota(jnp.int32, (tq, tk), 1)
    return jnp.where(col <= row, s, -jnp.inf)


def _fox_tiles(n, by_query):
    pairs = [(i, j) for i in range(n) for j in range(i + 1)] if by_query else \
            [(i, j) for j in range(n) for i in range(j, n)]
    return (jnp.asarray(np.array([p[0] for p in pairs], np.int32)),
            jnp.asarray(np.array([p[1] for p in pairs], np.int32)))


def _fox_specs(t, Dh):
    qs = pl.BlockSpec((None, t, Dh), lambda h, s, qt, kt: (h, qt[s], 0))
    ks = pl.BlockSpec((None, t, Dh), lambda h, s, qt, kt: (h, kt[s], 0))
    cqs = pl.BlockSpec((None, t, 1), lambda h, s, qt, kt: (h, qt[s], 0))
    cks = pl.BlockSpec((None, 1, t), lambda h, s, qt, kt: (h, 0, kt[s]))
    return qs, ks, cqs, cks


def _fox_call(body, tiles, Hh, in_specs, out_specs, out_shape, scratch, name, args):
    spec = pltpu.PrefetchScalarGridSpec(num_scalar_prefetch=2, grid=(Hh, tiles[0].shape[0]), in_specs=in_specs,
                                        out_specs=out_specs, scratch_shapes=scratch)
    return pl.pallas_call(body, grid_spec=spec, out_shape=out_shape,
                          compiler_params=_cparams(("parallel", "arbitrary")), name=name)(*tiles, *args)


def fox_fwd(q, k, v, c_col, c_row):
    Hh, S, Dh = q.shape
    tq = tk = min(512, S)

    def body(qt_ref, kt_ref, q_ref, k_ref, v_ref, cq_ref, ck_ref, o_ref, lse_ref, m_s, l_s, acc_s):
        qi, ki = qt_ref[pl.program_id(1)], kt_ref[pl.program_id(1)]

        @pl.when(ki == 0)
        def _():
            m_s[...] = jnp.full_like(m_s, -jnp.inf)
            l_s[...] = jnp.zeros_like(l_s)
            acc_s[...] = jnp.zeros_like(acc_s)

        s = _fox_logits(q_ref[...], k_ref[...], cq_ref[...], ck_ref[...], qi, ki, tq, tk)
        m_new = jnp.maximum(m_s[...], jnp.max(s, axis=-1, keepdims=True))
        alpha = jnp.exp(m_s[...] - m_new)
        p = jnp.exp(s - m_new)
        l_s[...] = alpha * l_s[...] + jnp.sum(p, axis=-1, keepdims=True)
        acc_s[...] = alpha * acc_s[...] + _dot(p, v_ref[...], _NN)
        m_s[...] = m_new

        @pl.when(ki == qi)
        def _():
            o_ref[...] = acc_s[...] / l_s[...]
            lse_ref[...] = m_s[...] + jnp.log(l_s[...])

    qs, ks, cqs, cks = _fox_specs(tq, Dh)
    return _fox_call(
        body, _fox_tiles(S // tq, True), Hh, [qs, ks, ks, cqs, cks], [qs, cqs],
        [jax.ShapeDtypeStruct((Hh, S, Dh), F32), jax.ShapeDtypeStruct((Hh, S, 1), F32)],
        [pltpu.VMEM((tq, 1), F32), pltpu.VMEM((tq, 1), F32), pltpu.VMEM((tq, Dh), F32)],
        "fox_fwd", (q, k, v, c_col, c_row))


def fox_bwd_dq(q, k, v, c_col, c_row, o, lse, do):
    Hh, S, Dh = q.shape
    tq = tk = min(512, S)

    def body(qt_ref, kt_ref, q_ref, k_ref, v_ref, cq_ref, ck_ref, o_ref, lse_ref, do_ref, dq_ref, dr_ref,
             acc_s, row_s):
        qi, ki = qt_ref[pl.program_id(1)], kt_ref[pl.program_id(1)]

        @pl.when(ki == 0)
        def _():
            acc_s[...] = jnp.zeros_like(acc_s)
            row_s[...] = jnp.zeros_like(row_s)

        s = _fox_logits(q_ref[...], k_ref[...], cq_ref[...], ck_ref[...], qi, ki, tq, tk)
        p = jnp.exp(s - lse_ref[...])
        do_t = do_ref[...]
        delta = jnp.sum(do_t * o_ref[...], axis=-1, keepdims=True)
        dp = _dot(do_t, v_ref[...], _NT)
        ds = p * (dp - delta)
        acc_s[...] += _dot(ds, k_ref[...], _NN)
        row_s[...] += jnp.sum(ds, axis=-1, keepdims=True)

        @pl.when(ki == qi)
        def _():
            dq_ref[...] = acc_s[...] * (HEAD_DIM ** -0.5)
            dr_ref[...] = row_s[...]

    qs, ks, cqs, cks = _fox_specs(tq, Dh)
    return _fox_call(
        body, _fox_tiles(S // tq, True), Hh, [qs, ks, ks, cqs, cks, qs, cqs, qs], [qs, cqs],
        [jax.ShapeDtypeStruct((Hh, S, Dh), F32), jax.ShapeDtypeStruct((Hh, S, 1), F32)],
        [pltpu.VMEM((tq, Dh), F32), pltpu.VMEM((tq, 1), F32)],
        "fox_bwd_dq", (q, k, v, c_col, c_row, o, lse, do))


def fox_bwd_dkv(q, k, v, c_col, c_row, o, lse, do):
    Hh, S, Dh = q.shape
    tq = tk = min(512, S)
    nq = S // tq

    def body(qt_ref, kt_ref, q_ref, k_ref, v_ref, cq_ref, ck_ref, o_ref, lse_ref, do_ref, dk_ref, dv_ref, dc_ref,
             dk_s, dv_s, dc_s):
        qi, ki = qt_ref[pl.program_id(1)], kt_ref[pl.program_id(1)]

        @pl.when(qi == ki)
        def _():
            dk_s[...] = jnp.zeros_like(dk_s)
            dv_s[...] = jnp.zeros_like(dv_s)
            dc_s[...] = jnp.zeros_like(dc_s)

        s = _fox_logits(q_ref[...], k_ref[...], cq_ref[...], ck_ref[...], qi, ki, tq, tk)
        p = jnp.exp(s - lse_ref[...])
        do_t = do_ref[...]
        delta = jnp.sum(do_t * o_ref[...], axis=-1, keepdims=True)
        dp = _dot(do_t, v_ref[...], _NT)
        ds = p * (dp - delta)
        dv_s[...] += _dot(p, do_t, _TN)
        dk_s[...] += _dot(ds, q_ref[...], _TN)
        dc_s[...] += jnp.sum(ds, axis=0, keepdims=True)

        @pl.when(qi == nq - 1)
        def _():
            dk_ref[...] = dk_s[...] * (HEAD_DIM ** -0.5)
            dv_ref[...] = dv_s[...]
            dc_ref[...] = dc_s[...]

    qs, ks, cqs, cks = _fox_specs(tq, Dh)
    return _fox_call(
        body, _fox_tiles(nq, False), Hh, [qs, ks, ks, cqs, cks, qs, cqs, qs], [ks, ks, cks],
        [jax.ShapeDtypeStruct((Hh, S, Dh), F32), jax.ShapeDtypeStruct((Hh, S, Dh), F32),
         jax.ShapeDtypeStruct((Hh, 1, S), F32)],
        [pltpu.VMEM((tk, Dh), F32), pltpu.VMEM((tk, Dh), F32), pltpu.VMEM((1, tk), F32)],
        "fox_bwd_dkv", (q, k, v, c_col, c_row, o, lse, do))


def _heads(a, nh):
    S = a.shape[0]
    return a.reshape(S, nh, HEAD_DIM).transpose(1, 0, 2)


def _unheads(a):
    nh, S, _ = a.shape
    return a.transpose(1, 0, 2).reshape(S, nh * HEAD_DIM)


def _shift_down(a):
    return jnp.pad(a[:-1], ((1, 0), (0, 0)))


def _shift_up(a):
    return jnp.pad(a[1:], ((0, 1), (0, 0)))


def _block_diag_ones():
    i = np.arange(RWKV_DIM) // HEAD_DIM
    return jnp.asarray((i[:, None] == i[None, :]).astype(np.float32))


FFN_ROWS = 1024
FFN_COLS = 256


def _ffn_specs(S, F, tm, fc):
    nf = F // fc
    row = pl.BlockSpec((tm, D_MODEL), lambda i, j: (i, 0))
    vec = pl.BlockSpec((1, D_MODEL), lambda i, j: (0, 0))
    wg = pl.BlockSpec((D_MODEL, fc), lambda i, j: (0, j))
    wu = pl.BlockSpec((D_MODEL, fc), lambda i, j: (0, nf + j))
    wd = pl.BlockSpec((fc, D_MODEL), lambda i, j: (j, 0))
    hid = pl.BlockSpec((tm, fc), lambda i, j: (i, j))
    return nf, row, vec, wg, wu, wd, hid


def ffn_fwd(x, g_norm, w_gu, w_down, tag):
    S, F = x.shape[0], w_down.shape[0]
    tm, fc = min(FFN_ROWS, S), FFN_COLS
    nf, row, vec, wg, wu, wd, _ = _ffn_specs(S, F, tm, fc)

    def body(x_ref, g_ref, wg_ref, wu_ref, wd_ref, o_ref, hn_ref, hn_s, acc):
        j = pl.program_id(1)

        @pl.when(j == 0)
        def _():
            hn_s[...] = _rms(x_ref[...], g_ref[...]).astype(BF16)
            hn_ref[...] = hn_s[...]
            acc[...] = jnp.zeros_like(acc)

        g = _dot(hn_s[...], wg_ref[...], _NN)
        u = _dot(hn_s[...], wu_ref[...], _NN)
        acc[...] += _dot(g * _sigmoid_tanh(g) * u, wd_ref[...], _NN)

        @pl.when(j == nf - 1)
        def _():
            o_ref[...] = x_ref[...] + 0.5 * acc[...]

    out, hn = pl.pallas_call(
        body,
        grid=(S // tm, nf),
        in_specs=[row, vec, wg, wu, wd],
        out_specs=[row, row],
        out_shape=[jax.ShapeDtypeStruct((S, D_MODEL), F32), jax.ShapeDtypeStruct((S, D_MODEL), BF16)],
        scratch_shapes=[pltpu.VMEM((tm, D_MODEL), BF16), pltpu.VMEM((tm, D_MODEL), F32)],
        compiler_params=_cparams(("parallel", "arbitrary")),
        name=tag + "_fwd",
    )(x, g_norm, w_gu, w_gu, w_down)
    return out, (x, hn)


def ffn_bwd(dy, saved, g_norm, w_gu, w_down, tag):
    x, hn = saved
    S, F = x.shape[0], w_down.shape[0]
    tm, fc = min(FFN_ROWS, S), FFN_COLS
    nf, row, vec, wg, wu, wd, hid = _ffn_specs(S, F, tm, fc)

    def body(dy_ref, x_ref, hn_ref, g_ref, wg_ref, wu_ref, wd_ref, dx_ref, dgn_ref, a_ref, dg_ref, du_ref,
             dyh_s, dhn):
        i, j = pl.program_id(0), pl.program_id(1)

        @pl.when(j == 0)
        def _():
            dyh_s[...] = (0.5 * dy_ref[...]).astype(BF16)
            dhn[...] = jnp.zeros_like(dhn)

        hn_t = hn_ref[...]
        g = _dot(hn_t, wg_ref[...], _NN)
        u = _dot(hn_t, wu_ref[...], _NN)
        da = _dot(dyh_s[...], wd_ref[...], _NT)
        sig = _sigmoid_tanh(g)
        gs = g * sig
        a_ref[...] = (gs * u).astype(BF16)
        dg = ((da * u) * (sig + gs * (1.0 - sig))).astype(BF16)
        du = (da * gs).astype(BF16)
        dg_ref[...] = dg
        du_ref[...] = du
        dhn[...] += _dot(jnp.concatenate([dg, du], axis=1),
                         jnp.concatenate([wg_ref[...], wu_ref[...]], axis=1), _NT)

        @pl.when(j == nf - 1)
        def _():
            _, vjp_n = jax.vjp(_rms, x_ref[...], g_ref[...])
            dx, dgn = vjp_n(dhn[...])
            dx_ref[...] = dy_ref[...] + dx

            @pl.when(i == 0)
            def _():
                dgn_ref[...] = jnp.zeros_like(dgn_ref)

            dgn_ref[...] += dgn

    hshape = jax.ShapeDtypeStruct((S, F), BF16)
    dx, dgn, act, dg, du = pl.pallas_call(
        body,
        grid=(S // tm, nf),
        in_specs=[row, row, row, vec, wg, wu, wd],
        out_specs=[row, vec, hid, hid, hid],
        out_shape=[jax.ShapeDtypeStruct((S, D_MODEL), F32), jax.ShapeDtypeStruct((1, D_MODEL), F32),
                   hshape, hshape, hshape],
        scratch_shapes=[pltpu.VMEM((tm, D_MODEL), BF16), pltpu.VMEM((tm, D_MODEL), F32)],
        compiler_params=_cparams(("arbitrary", "arbitrary")),
        name=tag + "_bwd",
    )(dy, x, hn, g_norm, w_gu, w_gu, w_down)
    d_wdown = matmul(act, dy, "tn", tag + "_dwd", out_dtype=BF16, scale=0.5)
    d_wgu = jnp.concatenate([matmul(hn, dg, "tn", tag + "_dwg", out_dtype=BF16),
                             matmul(hn, du, "tn", tag + "_dwu", out_dtype=BF16)], axis=1)
    return dx, dgn, d_wgu, d_wdown


def ple_fwd(x, p_i, g_norm, w_gate, w_proj, tag):
    hn, = rowwise(_f_rms, [x], [g_norm], [(D_MODEL, BF16)], tag + "_rms")
    z = matmul(hn, w_gate, "nn", tag + "_gate")
    pp = matmul(p_i, w_proj, "nn", tag + "_proj")
    out, = rowwise(_f_ple, [x, z, pp], [], [(D_MODEL, F32)], tag + "_mix")
    return out, (x, hn, z, pp)


def ple_bwd(dy, saved, p_i, g_norm, w_gate, tag):
    x, hn, z, pp = saved
    (dz, dpp), _ = rowwise_vjp(_f_ple, [x, z, pp], [], [dy], tag + "_dmix", need=[False, True, True],
                               row_dtype=BF16)
    d_wproj = matmul(p_i, dpp, "tn", tag + "_dwp", out_dtype=BF16)
    d_wgate = matmul(hn, dz, "tn", tag + "_dwg", out_dtype=BF16)
    dhn = matmul(dz, w_gate, "nt", tag + "_dhn")
    (dx,), (dgn,) = rowwise_vjp(_f_rms_res, [x], [g_norm], [dhn, dy], tag + "_drms")
    return dx, dgn, d_wgate, d_wproj


def _swa_consts(sinks):
    slopes = np.asarray([2.0 ** (-(i + 1)) for i in range(SWA_HEADS)], np.float32)
    slope_col = jnp.asarray(np.repeat(slopes, BLOCK).reshape(SWA_KV_HEADS, SWA_GROUP * BLOCK, 1))
    sink_col = jnp.repeat(sinks.reshape(SWA_HEADS), BLOCK).reshape(SWA_KV_HEADS, SWA_GROUP * BLOCK, 1)
    return sink_col, slope_col


def even_mix_fwd(x, W, gather_src):
    S = x.shape[0]
    hn, = rowwise(_f_rms, [x], [W["mix_norm0"]], [(D_MODEL, BF16)], "emix_rms")
    proj = matmul(hn, W["even_w_in"], "nn", "emix_in")
    qa = _heads(proj[:, :SWA_Q], SWA_HEADS).reshape(SWA_KV_HEADS, SWA_GROUP, S, HEAD_DIM)
    ka = _heads(proj[:, SWA_Q:SWA_Q + SWA_KV], SWA_KV_HEADS)
    va = _heads(proj[:, SWA_Q + SWA_KV:SWA_COLS], SWA_KV_HEADS)
    sink_col, slope_col = _swa_consts(W["swa_sinks"])
    ya = swa_fwd(qa, ka, va, sink_col, slope_col)
    ya = _unheads(ya.reshape(SWA_HEADS, S, HEAD_DIM))
    hb = proj[:, SWA_COLS:]
    h, = rowwise(_f_mix, [hb, _shift_down(hb)], [W["rwkv_mu"]], [(hb.shape[1], F32)], "rwkv_shift")
    hr, hk, hv = h[:, :512], h[:, 512:1024], h[:, 1024:1536]
    hw, ha, hg = h[:, 1536:1600], h[:, 1600:1664], h[:, 1664:1792]
    bd = _block_diag_ones()
    pre_params = [W["rwkv_w0"], W["rwkv_w2"], W["rwkv_a0"], W["rwkv_a2"], W["rwkv_g2"], W["rwkv_k_k"],
                  W["rwkv_k_a"]]
    decay, k2, kk, b, g = rowwise(_f_rwkv_pre, [hk, hw, ha, hg], pre_params + [bd],
                                  [(RWKV_DIM, F32)] * 5, "rwkv_pre")
    vT = _to_colblocks(hv)
    y, ckpt, gathered = rwkv_scan_fwd(hr, decay, k2, kk, b, vT, gather_src)
    post_params = [W["rwkv_ln_w"], W["rwkv_ln_b"], W["rwkv_r_k"]]
    yb, = rowwise(_f_rwkv_post, [y, hr, k2, hv, g], post_params + [bd], [(RWKV_DIM, F32)], "rwkv_post")
    cat = jnp.concatenate([ya, yb], axis=1).astype(BF16)
    out = matmul(cat, W["even_w_out"], "nn", "emix_out", res=x)
    saved = (x, hn, qa, ka, va, sink_col, slope_col, hb, hr, hk, hv, hw, ha, hg, decay, k2, kk, b, g, vT,
             ckpt, y, cat)
    return out, saved, gathered


def even_mix_bwd(dy, saved, W, scatter_src):
    (x, hn, qa, ka, va, sink_col, slope_col, hb, hr, hk, hv, hw, ha, hg, decay, k2, kk, b, g, vT, ckpt, y,
     cat) = saved
    S = x.shape[0]
    grads = {}
    dcat = matmul(dy, W["even_w_out"], "nt", "emix_dcat")
    grads["even_w_out"] = matmul(cat, dy, "tn", "emix_dwout", out_dtype=BF16)
    dya, dyb = dcat[:, :SWA_Q], dcat[:, SWA_Q:]
    dya_h = _heads(dya, SWA_HEADS).reshape(SWA_KV_HEADS, SWA_GROUP, S, HEAD_DIM)
    dqa, dkp, dkc, dvp, dvc, dsink = swa_bwd(qa, ka, va, sink_col, slope_col, dya_h)
    shift_blk = lambda a: jnp.pad(a[:, BLOCK:], ((0, 0), (0, BLOCK), (0, 0)))
    dka = dkc + shift_blk(dkp)
    dva = dvc + shift_blk(dvp)
    grads["swa_sinks"] = dsink.reshape(SWA_HEADS, BLOCK).sum(axis=1).reshape(1, SWA_HEADS)
    dqa = _unheads(dqa.reshape(SWA_HEADS, S, HEAD_DIM))
    dka, dva = _unheads(dka), _unheads(dva)
    bd = _block_diag_ones()
    post_params = [W["rwkv_ln_w"], W["rwkv_ln_b"], W["rwkv_r_k"]]
    (d_y, d_r1, d_k2a, d_v1, d_g), (d_lnw, d_lnb, d_rk) = rowwise_vjp(
        _f_rwkv_post, [y, hr, k2, hv, g], post_params, [dyb], "rwkv_dpost", consts=[bd], tm=128)
    grads["rwkv_ln_w"], grads["rwkv_ln_b"], grads["rwkv_r_k"] = d_lnw, d_lnb, d_rk
    d_r2, d_w, d_k2b, d_kk, d_b, d_v2, exchanged = rwkv_scan_bwd(hr, decay, k2, kk, b, vT, _to_colblocks(d_y), ckpt,
                                                                  scatter_src)
    pre_params = [W["rwkv_w0"], W["rwkv_w2"], W["rwkv_a0"], W["rwkv_a2"], W["rwkv_g2"], W["rwkv_k_k"],
                  W["rwkv_k_a"]]
    (d_hk, d_hw, d_ha, d_hg), dpre = rowwise_vjp(
        _f_rwkv_pre, [hk, hw, ha, hg], pre_params, [d_w, d_k2a + d_k2b, d_kk, d_b, d_g], "rwkv_dpre",
        consts=[bd], tm=128)
    for nm, gval in zip(["rwkv_w0", "rwkv_w2", "rwkv_a0", "rwkv_a2", "rwkv_g2", "rwkv_k_k", "rwkv_k_a"], dpre):
        grads[nm] = gval
    d_h = jnp.concatenate([d_r1 + d_r2, d_hk, d_v1 + d_v2, d_hw, d_ha, d_hg], axis=1)
    (d_hb, d_sh), (d_mu,) = rowwise_vjp(_f_mix, [hb, _shift_down(hb)], [W["rwkv_mu"]], [d_h], "rwkv_dshift")
    grads["rwkv_mu"] = d_mu
    d_hb = d_hb + _shift_up(d_sh)
    dproj = jnp.concatenate([dqa, dka, dva, d_hb], axis=1).astype(BF16)
    grads["even_w_in"] = matmul(hn, dproj, "tn", "emix_dwin", out_dtype=BF16)
    dhn = matmul(dproj, W["even_w_in"], "nt", "emix_dhn")
    (dx,), (dgn,) = rowwise_vjp(_f_rms_res, [x], [W["mix_norm0"]], [dhn, dy], "emix_drms")
    grads["mix_norm0"] = dgn
    return dx, grads, exchanged


def odd_mix_fwd(x, W):
    S = x.shape[0]
    hn, = rowwise(_f_rms, [x], [W["mix_norm1"]], [(D_MODEL, BF16)], "omix_rms")
    proj = matmul(hn, W["fox_w_in"], "nn", "omix_in")
    q = _heads(proj[:, :FOX_DIM], FOX_HEADS).astype(BF16)
    k = _heads(proj[:, FOX_DIM:2 * FOX_DIM], FOX_HEADS).astype(BF16)
    v = _heads(proj[:, 2 * FOX_DIM:3 * FOX_DIM], FOX_HEADS).astype(BF16)
    fz = proj[:, 3 * FOX_DIM:]
    logf, = rowwise(_f_logf, [fz], [W["fox_b_f"]], [(128, F32)], "fox_logf")
    c = seq_cumsum(logf, False, "fox_cumsum")[:, :FOX_HEADS]
    c_col = c.T.reshape(FOX_HEADS, S, 1)
    c_row = c.T.reshape(FOX_HEADS, 1, S)
    o, lse = fox_fwd(q, k, v, c_col, c_row)
    yc = _unheads(o).astype(BF16)
    out = matmul(yc, W["fox_w_out"], "nn", "omix_out", res=x)
    return out, (x, hn, q, k, v, fz, c_col, c_row, o, lse, yc)


def odd_mix_bwd(dy, saved, W):
    x, hn, q, k, v, fz, c_col, c_row, o, lse, yc = saved
    S = x.shape[0]
    grads = {}
    dyc = matmul(dy, W["fox_w_out"], "nt", "omix_dyc")
    grads["fox_w_out"] = matmul(yc, dy, "tn", "omix_dwout", out_dtype=BF16)
    do = _heads(dyc, FOX_HEADS)
    dq, drow = fox_bwd_dq(q, k, v, c_col, c_row, o, lse, do)
    dk, dv, dcol = fox_bwd_dkv(q, k, v, c_col, c_row, o, lse, do)
    dc = (drow.reshape(FOX_HEADS, S) - dcol.reshape(FOX_HEADS, S)).T
    dc = jnp.pad(dc, ((0, 0), (0, 128 - FOX_HEADS)))
    dlogf = seq_cumsum(dc, True, "fox_rcumsum")
    (dfz,), (dbf,) = rowwise_vjp(_f_logf, [fz], [W["fox_b_f"]], [dlogf], "fox_dlogf")
    grads["fox_b_f"] = dbf
    dproj = jnp.concatenate([_unheads(dq), _unheads(dk), _unheads(dv), dfz], axis=1).astype(BF16)
    grads["fox_w_in"] = matmul(hn, dproj, "tn", "omix_dwin", out_dtype=BF16)
    dhn = matmul(dproj, W["fox_w_in"], "nt", "omix_dhn")
    (dx,), (dgn,) = rowwise_vjp(_f_rms_res, [x], [W["mix_norm1"]], [dhn, dy], "omix_drms")
    grads["mix_norm1"] = dgn
    return dx, grads


def device_step(x, p, target, W, gather_src, layer1_weights, layer1_grads):
    W = dict(W)
    saved = []
    h = x
    for i in range(2):
        h, s1 = ffn_fwd(h, W[f"ffn1_norm{i}"], W[f"ffn1_w_gu{i}"], W[f"ffn1_w_down{i}"], f"ffn1_{i}")
        if i == 0:
            h, s2, gathered = even_mix_fwd(h, W, gather_src)
            W.update(layer1_weights(gathered))
        else:
            h, s2 = odd_mix_fwd(h, W)
        h, s3 = ffn_fwd(h, W[f"ffn2_norm{i}"], W[f"ffn2_w_gu{i}"], W[f"ffn2_w_down{i}"], f"ffn2_{i}")
        h, s4 = ple_fwd(h, p[i], W[f"ple_norm{i}"], W[f"ple_w_gate{i}"], W[f"ple_w_proj{i}"], f"ple_{i}")
        saved.append((s1, s2, s3, s4))
    dh, d_final, loss = loss_head(h, target, W["final_norm"])
    G = {"final_norm": d_final}
    for i in (1, 0):
        s1, s2, s3, s4 = saved[i]
        dh, G[f"ple_norm{i}"], G[f"ple_w_gate{i}"], G[f"ple_w_proj{i}"] = ple_bwd(
            dh, s4, p[i], W[f"ple_norm{i}"], W[f"ple_w_gate{i}"], f"ple_{i}")
        dh, G[f"ffn2_norm{i}"], G[f"ffn2_w_gu{i}"], G[f"ffn2_w_down{i}"] = ffn_bwd(
            dh, s3, W[f"ffn2_norm{i}"], W[f"ffn2_w_gu{i}"], W[f"ffn2_w_down{i}"], f"ffn2_{i}")
        if i == 0:
            dh, gm, exchanged = even_mix_bwd(dh, s2, W, layer1_grads(G))
        else:
            dh, gm = odd_mix_bwd(dh, s2, W)
        G.update(gm)
        dh, G[f"ffn1_norm{i}"], G[f"ffn1_w_gu{i}"], G[f"ffn1_w_down{i}"] = ffn_bwd(
            dh, s1, W[f"ffn1_norm{i}"], W[f"ffn1_w_gu{i}"], W[f"ffn1_w_down{i}"], f"ffn1_{i}")
    return loss, dh, G, exchanged


_MESH = pl.DeviceIdType.MESH
_ANY = pl.BlockSpec(memory_space=pl.ANY)


def all_gather(x, name):
    def body(x_ref, out_ref, send_sems, recv_sems, local_sem):
        x_, y_, c_ = lax.axis_index("x"), lax.axis_index("y"), lax.axis_index("c")
        me, sibling = (x_, y_, c_), (x_, y_, 1 - c_)
        chips = [(1 - x_, y_), (x_, 1 - y_), (1 - x_, 1 - y_)]

        def slot(px, py, pc):
            return out_ref.at[4 * px + 2 * py + pc]

        def copy(k, block, to, src=None):
            return pltpu.make_async_remote_copy(
                src_ref=slot(*block) if src is None else src, dst_ref=slot(*block),
                send_sem=send_sems.at[k], recv_sem=recv_sems.at[k], device_id=to, device_id_type=_MESH)

        mine = pltpu.make_async_copy(x_ref, slot(*me), local_sem)
        mine.start()
        first = [copy(0, me, sibling, src=x_ref)]
        first += [copy(1 + j, me, (*chip, c_), src=x_ref) for j, chip in enumerate(chips)]
        for cp in first:
            cp.start()
        passed = [copy(4 + j, (*chip, c_), sibling) for j, chip in enumerate(chips)]
        for j, chip in enumerate(chips):
            copy(1 + j, (*chip, c_), me).wait_recv()
            passed[j].start()
        copy(0, sibling, me).wait_recv()
        for j, chip in enumerate(chips):
            copy(4 + j, (*chip, 1 - c_), me).wait_recv()
        for cp in first + passed:
            cp.wait_send()
        mine.wait()

    return pl.pallas_call(
        body,
        out_shape=jax.ShapeDtypeStruct((N_DEV,) + x.shape, x.dtype),
        in_specs=[_ANY],
        out_specs=_ANY,
        scratch_shapes=[pltpu.SemaphoreType.DMA((7,)), pltpu.SemaphoreType.DMA((7,)), pltpu.SemaphoreType.DMA(())],
        name=name,
    )(x)


def _direct_exchange(gather, s_ref, r_ref, send_sems, recv_sems, local_sem):
    x_, y_, c_ = lax.axis_index("x"), lax.axis_index("y"), lax.axis_index("c")
    my = 4 * x_ + 2 * y_ + c_
    copies = [pltpu.make_async_copy(s_ref if gather else s_ref.at[my], r_ref.at[my], local_sem)]
    for m in range(1, N_DEV):
        px = 1 - x_ if (m >> 2) & 1 else x_
        py = 1 - y_ if (m >> 1) & 1 else y_
        pc = 1 - c_ if m & 1 else c_
        copies.append(pltpu.make_async_remote_copy(
            src_ref=s_ref if gather else s_ref.at[4 * px + 2 * py + pc], dst_ref=r_ref.at[my],
            send_sem=send_sems.at[m - 1], recv_sem=recv_sems.at[m - 1],
            device_id=(px, py, pc), device_id_type=_MESH))
    return copies


_EXCHANGE_SEMS = [pltpu.SemaphoreType.DMA((7,)), pltpu.SemaphoreType.DMA((7,)), pltpu.SemaphoreType.DMA(())]


def _exchange_during(step, n_steps, gather, s_ref, r_ref, send_sems, recv_sems, local_sem):
    copies = _direct_exchange(gather, s_ref, r_ref, send_sems, recv_sems, local_sem)

    @pl.when(step == 0)
    def _():
        for cp in copies:
            cp.start()

    @pl.when(step == n_steps - 1)
    def _():
        for cp in copies:
            cp.wait()


def _exchange_out_shape(gather, src):
    return jax.ShapeDtypeStruct(((N_DEV,) + src.shape) if gather else src.shape, src.dtype)


def all_to_all(send, name):
    def body(s_ref, r_ref, send_sems, recv_sems, local_sem):
        copies = _direct_exchange(False, s_ref, r_ref, send_sems, recv_sems, local_sem)
        for cp in copies:
            cp.start()
        for cp in copies:
            cp.wait()

    return pl.pallas_call(
        body,
        out_shape=jax.ShapeDtypeStruct(send.shape, send.dtype),
        in_specs=[_ANY],
        out_specs=_ANY,
        scratch_shapes=[pltpu.SemaphoreType.DMA((7,)), pltpu.SemaphoreType.DMA((7,)), pltpu.SemaphoreType.DMA(())],
        name=name,
    )(send)


def adamw(w, m, v, parts, name, tm=256):
    R, C = w.shape
    tm = _pick(R, tm, 8) if R >= 8 else R

    def body(w_ref, m_ref, v_ref, p_ref, g_ref, d_ref, nm_ref, nv_ref):
        g = p_ref[0].astype(F32)
        for s in range(1, N_DEV):
            g = g + p_ref[s].astype(F32)
        nm = ADAM_B1 * m_ref[...] + (1.0 - ADAM_B1) * g
        nv = ADAM_B2 * v_ref[...] + (1.0 - ADAM_B2) * (g * g)
        m_hat = nm / (1.0 - ADAM_B1 ** ADAM_STEP)
        v_hat = nv / (1.0 - ADAM_B2 ** ADAM_STEP)
        g_ref[...] = g
        d_ref[...] = -ADAM_LR * (m_hat / (jnp.sqrt(v_hat) + ADAM_EPS) + ADAM_WD * w_ref[...])
        nm_ref[...] = nm
        nv_ref[...] = nv

    row = pl.BlockSpec((tm, C), lambda i: (i, 0))
    out = jax.ShapeDtypeStruct((R, C), F32)
    return pl.pallas_call(
        body,
        grid=(R // tm,),
        in_specs=[row, row, row, pl.BlockSpec((N_DEV, tm, C), lambda i: (0, i, 0))],
        out_specs=[row] * 4,
        out_shape=[out] * 4,
        compiler_params=_cparams(("parallel",)),
        name=name,
    )(w, m, v, parts)


_WEIGHTS = ["ffn1_norm", "ffn1_w_gu", "ffn1_w_down", "mix_norm", "ffn2_norm", "ffn2_w_gu", "ffn2_w_down",
            "ple_norm", "ple_w_gate", "ple_w_proj", "even_w_in", "even_w_out", "swa_sinks", "rwkv_mu",
            "rwkv_w0", "rwkv_w2", "rwkv_a0", "rwkv_a2", "rwkv_g2", "rwkv_k_k", "rwkv_k_a", "rwkv_r_k",
            "rwkv_ln_w", "rwkv_ln_b", "fox_w_in", "fox_b_f", "fox_w_out", "final_norm"]
_SHARD_AXIS = {"ffn1_w_gu": 2, "ffn1_w_down": 1, "ffn2_w_gu": 2, "ffn2_w_down": 1, "ple_w_gate": 1,
               "ple_w_proj": 2, "even_w_in": 2, "even_w_out": 1, "rwkv_w2": 2, "rwkv_a2": 2, "rwkv_g2": 2,
               "fox_w_in": 2, "fox_w_out": 1}
_SHARDED = [n for n in _WEIGHTS if n in _SHARD_AXIS]
_REPLICATED = [n for n in _WEIGHTS if n not in _SHARD_AXIS]
_PER_LAYER = ("ffn1_w_gu", "ffn1_w_down", "ffn2_w_gu", "ffn2_w_down", "ple_w_gate", "ple_w_proj")
_PIECES = [[(n, 0) for n in _PER_LAYER] + [(n, 0) for n in ("even_w_in", "even_w_out", "rwkv_w2", "rwkv_a2", "rwkv_g2")],
           [(n, 1) for n in _PER_LAYER] + [("fox_w_in", 0), ("fox_w_out", 0)]]
_PACK_LANES = 1024
_PACK_ROW_TILE = 256


def _piece_key(piece):
    name, idx = piece
    return f"{name}{idx}" if name in _PER_LAYER else name


def _pack_rows(arrs):
    flat = jnp.concatenate([a.reshape(-1, _PACK_LANES) for a in arrs], axis=0)
    return jnp.pad(flat, ((0, -flat.shape[0] % _PACK_ROW_TILE), (0, 0)))


def _unpack_rows(flat, shapes):
    out, r0 = [], 0
    for shp in shapes:
        n = math.prod(shp) // _PACK_LANES
        out.append(flat[r0:r0 + n].reshape(shp))
        r0 += n
    return out


def _unshard(gathered, pieces, shapes):
    full, r0 = {}, 0
    for piece, shp in zip(pieces, shapes):
        n = math.prod(shp) // _PACK_LANES
        seg = gathered[:, r0:r0 + n].reshape((N_DEV,) + shp)
        ax = _SHARD_AXIS[piece[0]] - 1
        seg = jnp.moveaxis(seg, 0, ax)
        full[_piece_key(piece)] = seg.reshape(shp[:ax] + (N_DEV * shp[ax],) + shp[ax + 1:])
        r0 += n
    return full


def _to_shards(full, pieces, shapes):
    segs = []
    for piece, shp in zip(pieces, shapes):
        ax = _SHARD_AXIS[piece[0]] - 1
        a = full[_piece_key(piece)].astype(BF16).reshape(shp[:ax] + (N_DEV, shp[ax]) + shp[ax + 1:])
        segs.append(jnp.moveaxis(a, ax, 0).reshape(N_DEV, -1, _PACK_LANES))
    flat = jnp.concatenate(segs, axis=1)
    return jnp.pad(flat, ((0, 0), (0, -flat.shape[1] % _PACK_ROW_TILE), (0, 0)))


def _layer_weights(full):
    W = dict(full)
    if "fox_w_in" in W:
        W["fox_w_in"] = jnp.pad(W["fox_w_in"], ((0, 0), (0, FOX_IN_PAD - W["fox_w_in"].shape[1])))
    for n in ("rwkv_w2", "rwkv_a2", "rwkv_g2"):
        if n in W:
            W[n] = W[n].astype(F32)
    return W


def _pack_small(vals):
    flat = jnp.concatenate([v.reshape(1, -1) for v in vals], axis=1)
    n = flat.shape[1]
    return jnp.pad(flat, ((0, 0), (0, -n % 128)))


def _unpack_small(flat, shapes):
    out, c0 = [], 0
    for shp in shapes:
        n = math.prod(shp)
        out.append(flat[0, c0:c0 + n].reshape(shp))
        c0 += n
    return out


def kernel(x, p, ffn1_norm, ffn1_w_gu, ffn1_w_down, mix_norm, ffn2_norm, ffn2_w_gu, ffn2_w_down, ple_norm, ple_w_gate, ple_w_proj, even_w_in, even_w_out, swa_sinks, rwkv_mu, rwkv_w0, rwkv_w2, rwkv_a0, rwkv_a2, rwkv_g2, rwkv_k_k, rwkv_k_a, rwkv_r_k, rwkv_ln_w, rwkv_ln_b, fox_w_in, fox_b_f, fox_w_out, final_norm, loss_target, m_ffn1_norm, m_ffn1_w_gu, m_ffn1_w_down, m_mix_norm, m_ffn2_norm, m_ffn2_w_gu, m_ffn2_w_down, m_ple_norm, m_ple_w_gate, m_ple_w_proj, m_even_w_in, m_even_w_out, m_swa_sinks, m_rwkv_mu, m_rwkv_w0, m_rwkv_w2, m_rwkv_a0, m_rwkv_a2, m_rwkv_g2, m_rwkv_k_k, m_rwkv_k_a, m_rwkv_r_k, m_rwkv_ln_w, m_rwkv_ln_b, m_fox_w_in, m_fox_b_f, m_fox_w_out, m_final_norm, v_ffn1_norm, v_ffn1_w_gu, v_ffn1_w_down, v_mix_norm, v_ffn2_norm, v_ffn2_w_gu, v_ffn2_w_down, v_ple_norm, v_ple_w_gate, v_ple_w_proj, v_even_w_in, v_even_w_out, v_swa_sinks, v_rwkv_mu, v_rwkv_w0, v_rwkv_w2, v_rwkv_a0, v_rwkv_a2, v_rwkv_g2, v_rwkv_k_k, v_rwkv_k_a, v_rwkv_r_k, v_rwkv_ln_w, v_rwkv_ln_b, v_fox_w_in, v_fox_b_f, v_fox_w_out, v_final_norm):
    given = dict(locals())
    w = {n: given[n] for n in _WEIGHTS}
    m = {n: given["m_" + n] for n in _WEIGHTS}
    v = {n: given["v_" + n] for n in _WEIGHTS}
    small_shapes = [w[n].shape for n in _REPLICATED]
    piece = lambda d, pc: d[pc[0]][pc[1]]
    shapes = [[piece(w, pc).shape for pc in pieces] for pieces in _PIECES]
    w_rows = [_pack_rows([piece(w, pc) for pc in pieces]) for pieces in _PIECES]

    W = _layer_weights(_unshard(all_gather(w_rows[0].astype(BF16), "weights_all_gather"), _PIECES[0], shapes[0]))
    for i in range(2):
        for n in ("ffn1_norm", "mix_norm", "ffn2_norm", "ple_norm"):
            W[f"{n}{i}"] = w[n][i].reshape(1, -1)
    for n in ("swa_sinks", "rwkv_mu", "rwkv_w0", "rwkv_a0", "rwkv_k_k", "rwkv_k_a", "rwkv_r_k", "rwkv_ln_w",
              "rwkv_ln_b", "final_norm"):
        W[n] = w[n].reshape(1, -1)
    n_f = fox_b_f.shape[1]
    W["fox_b_f"] = jnp.pad(fox_b_f.reshape(1, n_f), ((0, 0), (0, 128 - n_f)))
    n_fox = fox_w_in.shape[2] * N_DEV

    def layer1_weights(gathered):
        return _layer_weights(_unshard(gathered, _PIECES[1], shapes[1]))

    def layer1_grads(G):
        G = dict(G, fox_w_in=G["fox_w_in"][:, :n_fox])
        return _to_shards(G, _PIECES[1], shapes[1])

    loss_row, dx, G, parts1 = device_step(x[0], p[:, 0], loss_target[0], W, w_rows[1].astype(BF16),
                                          layer1_weights, layer1_grads)

    parts = [all_to_all(_to_shards(G, _PIECES[0], shapes[0]), "grads_all_to_all"), parts1]
    out_g, out_d, out_m, out_v = {}, {}, {}, {}
    for li, pieces in enumerate(_PIECES):
        res = adamw(w_rows[li], _pack_rows([piece(m, pc) for pc in pieces]),
                    _pack_rows([piece(v, pc) for pc in pieces]), parts[li], f"adamw_sharded{li}")
        for out, rows in zip((out_g, out_d, out_m, out_v), res):
            for pc, a in zip(pieces, _unpack_rows(rows, shapes[li])):
                out.setdefault(pc[0], {})[pc[1]] = a
    for out in (out_g, out_d, out_m, out_v):
        for n in _SHARDED:
            out[n] = jnp.stack([out[n][i] for i in sorted(out[n])])

    gsmall = {}
    for n in ("ffn1_norm", "mix_norm", "ffn2_norm", "ple_norm"):
        gsmall[n] = jnp.concatenate([G[f"{n}0"], G[f"{n}1"]], axis=0)
    for n in ("swa_sinks", "rwkv_mu", "rwkv_w0", "rwkv_a0", "rwkv_k_k", "rwkv_k_a", "rwkv_r_k", "rwkv_ln_w",
              "rwkv_ln_b", "final_norm"):
        gsmall[n] = G[n]
    gsmall["fox_b_f"] = G["fox_b_f"][:, :n_f]
    small = _pack_small([gsmall[n] for n in _REPLICATED] + [loss_row[:, :1]])
    small_parts = all_gather(small, "small_all_gather")
    pad1 = lambda vals: _pack_small(vals + [jnp.zeros((1, 1), F32)])
    gs, ds, nms, nvs = adamw(pad1([w[n] for n in _REPLICATED]), pad1([m[n] for n in _REPLICATED]),
                             pad1([v[n] for n in _REPLICATED]), small_parts, "adamw_replicated")
    out_g.update(zip(_REPLICATED, _unpack_small(gs, small_shapes)))
    out_d.update(zip(_REPLICATED, _unpack_small(ds, small_shapes)))
    out_m.update(zip(_REPLICATED, _unpack_small(nms, small_shapes)))
    out_v.update(zip(_REPLICATED, _unpack_small(nvs, small_shapes)))
    n_small = sum(math.prod(s) for s in small_shapes)
    loss = gs[0, n_small]

    return (loss, dx[None], *[out_g[n] for n in _WEIGHTS], *[out_d[n] for n in _WEIGHTS],
            *[out_m[n] for n in _WEIGHTS], *[out_v[n] for n in _WEIGHTS])
```

```python
import functools
import math

import numpy as np
import jax
import jax.numpy as jnp
from jax import lax
from jax.experimental import pallas as pl
from jax.experimental.pallas import tpu as pltpu

F32 = jnp.float32
BF16 = jnp.bfloat16

D_MODEL = 1024
HEAD_DIM = 64
BLOCK = 128
SWA_HEADS = 8
SWA_KV_HEADS = 2
SWA_GROUP = 4
RWKV_HEADS = 8
RWKV_DIM = 512
FOX_HEADS = 16
FOX_DIM = 1024
D_FF = 2816
NORM_EPS = 1e-6
GN_EPS = 64e-5
L2_EPS = 1e-12
SWA_Q = 512
SWA_KV = 128
SWA_COLS = 768
FOX_IN_PAD = 3200
N_DEV = 8
ADAM_LR = 0.001
ADAM_B1 = 0.9
ADAM_B2 = 0.999
ADAM_EPS = 1e-08
ADAM_WD = 0.01
ADAM_STEP = 10

V7X_VMEM_LIMIT = 56 * 1024 * 1024
SCAN_GROUP = 8
SCAN_CHUNK = 32

_NN = (((1,), (0,)), ((), ()))
_NT = (((1,), (1,)), ((), ()))
_TN = (((0,), (0,)), ((), ()))
_DIMS = {"nn": _NN, "nt": _NT, "tn": _TN}


def _pick(n, target, mult=128):
    best = None
    for t in range(mult, min(n, target) + 1, mult):
        if n % t == 0:
            best = t
    return best or n


def _cparams(sem):
    return pltpu.CompilerParams(dimension_semantics=sem, vmem_limit_bytes=V7X_VMEM_LIMIT)


def _dot(a, b, dims):
    return lax.dot_general(a.astype(BF16), b.astype(BF16), dims, preferred_element_type=F32)


@jax.custom_vjp
def bdot(a, b):
    return _dot(a, b, _NN)


def _bdot_fwd(a, b):
    return _dot(a, b, _NN), (a, b)


def _bdot_bwd(res, g):
    a, b = res
    return _dot(g, b, _NT), _dot(a, g, _TN)


bdot.defvjp(_bdot_fwd, _bdot_bwd)


@jax.custom_vjp
def bdot_nt(a, b):
    return _dot(a, b, _NT)


def _bdot_nt_fwd(a, b):
    return _dot(a, b, _NT), (a, b)


def _bdot_nt_bwd(res, g):
    a, b = res
    return _dot(g, b, _NN), _dot(g, a, _TN)


bdot_nt.defvjp(_bdot_nt_fwd, _bdot_nt_bwd)


def _segsum(x, bd):
    return jnp.dot(x, bd, precision=lax.Precision.HIGHEST, preferred_element_type=F32)


def _sigmoid(x):
    return 1.0 / (1.0 + jnp.exp(-x))


def _sigmoid_tanh(x):
    return 0.5 * jnp.tanh(0.5 * x) + 0.5


def _softplus(x):
    return jnp.maximum(x, 0.0) + jnp.log(1.0 + jnp.exp(-jnp.abs(x)))


def matmul(a, b, mode, name, out_dtype=F32, scale=1.0, res=None, tm=512, tn=1408, tk=1024):
    if mode == "nn":
        (M, K), (K2, N) = a.shape, b.shape
    elif mode == "nt":
        (M, K), (N, K2) = a.shape, b.shape
    else:
        (K, M), (K2, N) = a.shape, b.shape
    assert K == K2, (a.shape, b.shape, mode)
    tm, tn, tk = _pick(M, tm), _pick(N, tn), _pick(K, tk)
    nk = K // tk
    has_res = res is not None

    def body(*refs):
        if has_res:
            a_ref, b_ref, r_ref, o_ref, acc = refs
        else:
            a_ref, b_ref, o_ref, acc = refs
        kk = pl.program_id(2)

        @pl.when(kk == 0)
        def _():
            acc[...] = jnp.zeros_like(acc)

        acc[...] += _dot(a_ref[...], b_ref[...], _DIMS[mode])

        @pl.when(kk == nk - 1)
        def _():
            v = acc[...]
            if scale != 1.0:
                v = v * scale
            if has_res:
                v = v + r_ref[...].astype(F32)
            o_ref[...] = v.astype(out_dtype)

    if mode == "tn":
        a_spec = pl.BlockSpec((tk, tm), lambda i, j, k: (k, i))
    else:
        a_spec = pl.BlockSpec((tm, tk), lambda i, j, k: (i, k))
    if mode == "nt":
        b_spec = pl.BlockSpec((tn, tk), lambda i, j, k: (j, k))
    else:
        b_spec = pl.BlockSpec((tk, tn), lambda i, j, k: (k, j))
    o_spec = pl.BlockSpec((tm, tn), lambda i, j, k: (i, j))
    in_specs = [a_spec, b_spec] + ([o_spec] if has_res else [])
    args = (a, b) + ((res,) if has_res else ())
    return pl.pallas_call(
        body,
        grid=(M // tm, N // tn, nk),
        in_specs=in_specs,
        out_specs=o_spec,
        out_shape=jax.ShapeDtypeStruct((M, N), out_dtype),
        scratch_shapes=[pltpu.VMEM((tm, tn), F32)],
        compiler_params=_cparams(("parallel", "parallel", "arbitrary")),
        name=name,
    )(*args)


def _row_spec(r, tm):
    if isinstance(r, tuple):
        arr, width, blk = r
        return arr, pl.BlockSpec((tm, width), lambda i, blk=blk: (i, blk))
    return r, pl.BlockSpec((tm, r.shape[1]), lambda i: (i, 0))


def _whole_spec(p):
    return pl.BlockSpec(p.shape, lambda i: (0,) * p.ndim)


def rowwise(fn, rows, params, outs, name, tm=256):
    arrs, specs = zip(*[_row_spec(r, tm) for r in rows])
    S = arrs[0].shape[0]
    tm = min(tm, S)
    arrs, specs = zip(*[_row_spec(r, tm) for r in rows])
    n_in = len(rows) + len(params)

    def body(*refs):
        res = fn(*[r[...] for r in refs[:n_in]])
        for o_ref, v in zip(refs[n_in:], res):
            o_ref[...] = v.astype(o_ref.dtype)

    return pl.pallas_call(
        body,
        grid=(S // tm,),
        in_specs=list(specs) + [_whole_spec(p) for p in params],
        out_specs=[pl.BlockSpec((tm, c), lambda i: (i, 0)) for c, _ in outs],
        out_shape=[jax.ShapeDtypeStruct((S, c), dt) for c, dt in outs],
        compiler_params=_cparams(("parallel",)),
        name=name,
    )(*arrs, *params)


def rowwise_vjp(fn, rows, params, cots, name, need=None, row_dtype=F32, consts=(), tm=256):
    nr, npar, nc, nk = len(rows), len(params), len(cots), len(consts)
    need = [True] * nr if need is None else need
    arrs, _ = zip(*[_row_spec(r, tm) for r in rows])
    S = arrs[0].shape[0]
    tm = min(tm, S)
    arrs, specs = zip(*[_row_spec(r, tm) for r in rows])
    carrs, cspecs = zip(*[_row_spec(c, tm) for c in cots])
    widths = [s.block_shape[1] for s in specs]
    n_in = nr + npar + nk + nc

    def body(*refs):
        i = pl.program_id(0)
        xs = [r[...].astype(F32) for r in refs[:nr]]
        ps = [r[...] for r in refs[nr:nr + npar]]
        ks = [r[...] for r in refs[nr + npar:nr + npar + nk]]
        cs = [r[...].astype(F32) for r in refs[nr + npar + nk:n_in]]
        outs, vjp = jax.vjp(lambda *a: fn(*a, *ks), *xs, *ps)
        grads = vjp(tuple(cs))
        o = n_in
        for j in range(nr):
            if need[j]:
                refs[o][...] = grads[j].astype(refs[o].dtype)
                o += 1
        for j in range(npar):
            g_ref = refs[o + j]

            @pl.when(i == 0)
            def _(g_ref=g_ref):
                g_ref[...] = jnp.zeros_like(g_ref)

            g_ref[...] += grads[nr + j]

    out_specs = [pl.BlockSpec((tm, w), lambda i: (i, 0)) for w, nd in zip(widths, need) if nd]
    out_shape = [jax.ShapeDtypeStruct((S, w), row_dtype) for w, nd in zip(widths, need) if nd]
    out_specs += [_whole_spec(p) for p in params]
    out_shape += [jax.ShapeDtypeStruct(p.shape, F32) for p in params]
    res = pl.pallas_call(
        body,
        grid=(S // tm,),
        in_specs=list(specs) + [_whole_spec(p) for p in params] + [_whole_spec(k) for k in consts] + list(cspecs),
        out_specs=out_specs,
        out_shape=out_shape,
        compiler_params=_cparams(("arbitrary",)),
        name=name,
    )(*arrs, *params, *consts, *carrs)
    nrow = sum(need)
    return list(res[:nrow]), list(res[nrow:])


def _rms(x, g):
    return x * lax.rsqrt(jnp.mean(x * x, axis=-1, keepdims=True) + NORM_EPS) * g


def _f_rms(x, g):
    return (_rms(x, g),)


def _f_rms_res(x, g):
    return _rms(x, g), x


def _f_ple(x, z, pp):
    return (x + _sigmoid(z) * pp,)


def _f_mix(h, sh, mu):
    return (h + (sh - h) * mu,)


def _f_logf(fz, bf):
    return (-_softplus(-(fz + bf)),)


def _f_rwkv_pre(hk, hw, ha, hg, w0, w2, a0, a2, g2, k_k, k_a, bd):
    wlog = -_softplus(-(w0 + bdot(jnp.tanh(hw), w2))) - 0.5
    a = _sigmoid(a0 + bdot(ha, a2))
    g = bdot(_sigmoid(hg), g2)
    kk = hk * k_k
    kk = kk / jnp.maximum(jnp.sqrt(_segsum(kk * kk, bd)), L2_EPS)
    k2 = hk * (1.0 + (a - 1.0) * k_a)
    decay = jnp.exp(-jnp.exp(wlog))
    return decay, k2, kk, kk * a, g


def _f_rwkv_post(y, r, k2, v, g, ln_w, ln_b, r_k, bd):
    mean = _segsum(y, bd) * (1.0 / HEAD_DIM)
    d = y - mean
    var = _segsum(d * d, bd) * (1.0 / HEAD_DIM)
    yn = d * lax.rsqrt(var + GN_EPS) * ln_w + ln_b
    yn = yn + _segsum(r * k2 * r_k, bd) * v
    return (yn * g,)


def loss_head(x, target, gf, tm=256):
    S, D = x.shape
    tm = min(tm, S)

    def f(xt, g, tt):
        err = _rms(xt, g) - tt
        return 0.5 * jnp.sum(err * err) * (1.0 / D)

    def body(x_ref, t_ref, g_ref, dx_ref, dg_ref, l_ref):
        i = pl.program_id(0)
        val, (dx, dg) = jax.value_and_grad(f, argnums=(0, 1))(x_ref[...], g_ref[...], t_ref[...])

        @pl.when(i == 0)
        def _():
            dg_ref[...] = jnp.zeros_like(dg_ref)
            l_ref[...] = jnp.zeros_like(l_ref)

        dx_ref[...] = dx
        dg_ref[...] += dg
        l_ref[...] += jnp.full(l_ref.shape, val, F32)

    row = pl.BlockSpec((tm, D), lambda i: (i, 0))
    vec = pl.BlockSpec((1, D), lambda i: (0, 0))
    return pl.pallas_call(
        body,
        grid=(S // tm,),
        in_specs=[row, row, vec],
        out_specs=[row, vec, pl.BlockSpec((1, 128), lambda i: (0, 0))],
        out_shape=[jax.ShapeDtypeStruct((S, D), F32), jax.ShapeDtypeStruct((1, D), F32),
                   jax.ShapeDtypeStruct((1, 128), F32)],
        compiler_params=_cparams(("arbitrary",)),
        name="loss_head",
    )(x, target, gf)


def _swa_block(q, kp, kc, vp, vc, sink, slope, n):
    k = jnp.concatenate([kp, kc], axis=0)
    v = jnp.concatenate([vp, vc], axis=0)
    rows = q.shape[0]
    logits = bdot_nt(q, k) * (HEAD_DIM ** -0.5)
    qi = lax.broadcasted_iota(jnp.int32, (rows, 2 * BLOCK), 0) & (BLOCK - 1)
    ki = lax.broadcasted_iota(jnp.int32, (rows, 2 * BLOCK), 1)
    dist = qi + BLOCK - ki
    valid = (dist >= 0) & (dist < BLOCK) & ((n - 1) * BLOCK + ki >= 0)
    logits = logits - slope * dist.astype(F32)
    logits = jnp.where(valid, logits, -jnp.inf)
    m = jnp.maximum(jnp.max(logits, axis=-1, keepdims=True), sink)
    pr = jnp.exp(logits - m)
    denom = jnp.sum(pr, axis=-1, keepdims=True) + jnp.exp(sink - m)
    return bdot(pr / denom, v)


def _swa_specs(S):
    nb = S // BLOCK
    q_spec = pl.BlockSpec((None, SWA_GROUP, BLOCK, HEAD_DIM), lambda h, n: (h, 0, n, 0))
    kc_spec = pl.BlockSpec((None, BLOCK, HEAD_DIM), lambda h, n: (h, n, 0))
    kp_spec = pl.BlockSpec((None, BLOCK, HEAD_DIM), lambda h, n: (h, jnp.maximum(n - 1, 0), 0))
    col_spec = pl.BlockSpec((None, SWA_GROUP * BLOCK, 1), lambda h, n: (h, 0, 0))
    return nb, q_spec, kp_spec, kc_spec, col_spec


def swa_fwd(q, k, v, sink_col, slope_col):
    S = q.shape[2]
    nb, q_spec, kp_spec, kc_spec, col_spec = _swa_specs(S)

    def body(q_ref, kp_ref, kc_ref, vp_ref, vc_ref, s_ref, a_ref, o_ref):
        n = pl.program_id(1)
        qq = q_ref[...].reshape(SWA_GROUP * BLOCK, HEAD_DIM)
        out = _swa_block(qq, kp_ref[...], kc_ref[...], vp_ref[...], vc_ref[...], s_ref[...], a_ref[...], n)
        o_ref[...] = out.reshape(SWA_GROUP, BLOCK, HEAD_DIM)

    return pl.pallas_call(
        body,
        grid=(SWA_KV_HEADS, nb),
        in_specs=[q_spec, kp_spec, kc_spec, kp_spec, kc_spec, col_spec, col_spec],
        out_specs=q_spec,
        out_shape=jax.ShapeDtypeStruct(q.shape, F32),
        compiler_params=_cparams(("parallel", "parallel")),
        name="swa_fwd",
    )(q, k, k, v, v, sink_col, slope_col)


def swa_bwd(q, k, v, sink_col, slope_col, dout):
    S = q.shape[2]
    nb, q_spec, kp_spec, kc_spec, col_spec = _swa_specs(S)

    def body(q_ref, kp_ref, kc_ref, vp_ref, vc_ref, s_ref, a_ref, do_ref,
             dq_ref, dkp_ref, dkc_ref, dvp_ref, dvc_ref, ds_ref):
        n = pl.program_id(1)
        qq = q_ref[...].reshape(SWA_GROUP * BLOCK, HEAD_DIM)
        slope = a_ref[...]
        f = lambda a, b, c, d, e, s: _swa_block(a, b, c, d, e, s, slope, n)
        _, vjp = jax.vjp(f, qq, kp_ref[...], kc_ref[...], vp_ref[...], vc_ref[...], s_ref[...])
        dq, dkp, dkc, dvp, dvc, ds = vjp(do_ref[...].reshape(SWA_GROUP * BLOCK, HEAD_DIM))
        dq_ref[...] = dq.reshape(SWA_GROUP, BLOCK, HEAD_DIM)
        dkp_ref[...] = dkp
        dkc_ref[...] = dkc
        dvp_ref[...] = dvp
        dvc_ref[...] = dvc

        @pl.when(n == 0)
        def _():
            ds_ref[...] = jnp.zeros_like(ds_ref)

        ds_ref[...] += ds

    kv_shape = jax.ShapeDtypeStruct(k.shape, F32)
    return pl.pallas_call(
        body,
        grid=(SWA_KV_HEADS, nb),
        in_specs=[q_spec, kp_spec, kc_spec, kp_spec, kc_spec, col_spec, col_spec, q_spec],
        out_specs=[q_spec, kc_spec, kc_spec, kc_spec, kc_spec, col_spec],
        out_shape=[jax.ShapeDtypeStruct(q.shape, F32), kv_shape, kv_shape, kv_shape, kv_shape,
                   jax.ShapeDtypeStruct(sink_col.shape, F32)],
        compiler_params=_cparams(("parallel", "arbitrary")),
        name="swa_bwd",
    )(q, k, k, v, v, sink_col, slope_col, dout)


def _split2(x):
    hi = x.astype(BF16)
    return (x - hi.astype(F32)).astype(BF16), hi


def _dot2_many(xs, m):
    rows = xs[0].shape[0]
    res = jnp.dot(jnp.concatenate([p for x in xs for p in _split2(x)], axis=0), m, preferred_element_type=F32)
    return [res[(2 * i) * rows:(2 * i + 1) * rows] + res[(2 * i + 1) * rows:(2 * i + 2) * rows]
            for i in range(len(xs))]


def _dot2(x, m):
    return _dot2_many([x], m)[0]


def _seg_sums(xs, bd):
    w = bd.shape[0]
    halves = _dot2_many([x[:, i:i + w] for x in xs for i in range(0, x.shape[1], w)], bd)
    n = xs[0].shape[1] // w
    return [jnp.concatenate(halves[i * n:(i + 1) * n], axis=1) for i in range(len(xs))]


def _seg_sum(x, bd):
    return _seg_sums([x], bd)[0]


def _scan_consts():
    r = np.arange(256)
    bd = (r[:, None] // HEAD_DIM == r[None, :] // HEAD_DIM).astype(np.float32)
    c = np.arange(RWKV_DIM)
    e = (np.arange(HEAD_DIM)[:, None] // SCAN_GROUP == c[None, :] // HEAD_DIM).astype(np.float32)
    diag = (np.arange(HEAD_DIM)[:, None] == c[None, :] % HEAD_DIM).astype(np.float32)
    return jnp.asarray(bd, BF16), jnp.asarray(e, BF16), jnp.asarray(diag, F32)


def _to_colblocks(a):
    S = a.shape[0]
    a = a.reshape(S // SCAN_GROUP, SCAN_GROUP, RWKV_HEADS, HEAD_DIM)
    return a.transpose(0, 3, 2, 1).reshape(S // SCAN_GROUP, HEAD_DIM, RWKV_HEADS * SCAN_GROUP)


def _roll_up(rows):
    return pltpu.roll(rows, rows.shape[0] - 1, 0)


def _scan_pair_rows(aux, base, kk_ref, w_ref, b_ref, k_ref, bd):
    G = SCAN_GROUP
    kk_nx = _roll_up(kk_ref[pl.ds(base, G), :])
    aux[0] = w_ref[pl.ds(base, G), :] * kk_nx
    aux[1], aux[2] = _seg_sums([b_ref[pl.ds(base, G), :] * kk_nx, k_ref[pl.ds(base, G), :] * kk_nx], bd)


def _scan_pair(St, t0, base, col_g, lane_t, aux, kk_ref, w_ref, b_ref, k_ref, bd, e):
    t1 = t0 + 1
    row = lambda ref, t: ref[pl.ds(base + t, 1), :]
    arow = lambda i: aux[i, pl.ds(t0, 1), :]
    u0, m1 = _seg_sums([St * row(kk_ref, t0), St * arow(0)], bd)
    v0, v1 = _dot2_many([jnp.where(lane_t == t0, col_g, 0.0), jnp.where(lane_t == t1, col_g, 0.0)], e)
    u1 = m1 - u0 * arow(1) + v0 * arow(2)
    S0 = St * row(w_ref, t0) - u0 * row(b_ref, t0) + v0 * row(k_ref, t0)
    S1 = S0 * row(w_ref, t1) - u1 * row(b_ref, t1) + v1 * row(k_ref, t1)
    return (S0, S1), (u0, u1), (v0, v1)


def rwkv_scan_fwd(r, w, k, kk, b, vB, gather_src):
    S, C = r.shape
    N, G = HEAD_DIM, SCAN_GROUP
    chunk = min(SCAN_CHUNK, S)
    nchunk, ng = S // chunk, chunk // G
    bd, e, diag = _scan_consts()

    def body(r_ref, w_ref, k_ref, kk_ref, b_ref, vB_ref, bd_ref, e_ref, dg_ref, xs_ref, y_ref, ck_ref, xr_ref,
             S_ref, aux, send_sems, recv_sems, local_sem):
        c = pl.program_id(0)
        _exchange_during(c, nchunk, True, xs_ref, xr_ref, send_sems, recv_sems, local_sem)

        @pl.when(c == 0)
        def _():
            S_ref[...] = jnp.zeros_like(S_ref)

        ck_ref[...] = S_ref[...]
        sub = lax.broadcasted_iota(jnp.int32, (G, C), 0)
        lane_t = lax.broadcasted_iota(jnp.int32, (N, N), 1) & (G - 1)

        def group(g, St):
            base = pl.multiple_of(g * G, G)
            vb = vB_ref[g]
            _scan_pair_rows(aux, base, kk_ref, w_ref, b_ref, k_ref, bd_ref[...])
            ys = jnp.zeros((G, C), F32)
            def emit(ys, states, t0):
                steps = (t0, t0 + 1)
                y_bs = _seg_sums([S_t * r_ref[pl.ds(base + tt, 1), :] for S_t, tt in zip(states, steps)], bd_ref[...])
                for y_b, tt in zip(y_bs, steps):
                    ys = jnp.where(sub == tt, jnp.sum(y_b * dg_ref[...], axis=0, keepdims=True), ys)
                return ys

            pending = None
            for t0 in range(0, G, 2):
                states, _, _ = _scan_pair(St, t0, base, vb, lane_t, aux, kk_ref, w_ref, b_ref, k_ref, bd_ref[...],
                                          e_ref[...])
                if pending is not None:
                    ys = emit(ys, *pending)
                pending = (states, t0)
                St = states[1]
            y_ref[pl.ds(base, G), :] = emit(ys, *pending)
            return St

        S_ref[...] = lax.fori_loop(0, ng, group, S_ref[...])

    row = pl.BlockSpec((chunk, C), lambda c: (c, 0))
    col = pl.BlockSpec((ng, N, N), lambda c: (c, 0, 0))
    return pl.pallas_call(
        body,
        grid=(nchunk,),
        in_specs=[row] * 5 + [col, _whole_spec(bd), _whole_spec(e), _whole_spec(diag), _ANY],
        out_specs=[row, pl.BlockSpec((None, N, C), lambda c: (c, 0, 0)), _ANY],
        out_shape=[jax.ShapeDtypeStruct((S, C), F32), jax.ShapeDtypeStruct((nchunk, N, C), F32),
                   _exchange_out_shape(True, gather_src)],
        scratch_shapes=[pltpu.VMEM((N, C), F32), pltpu.VMEM((3, G, C), F32)] + _EXCHANGE_SEMS,
        compiler_params=_cparams(("arbitrary",)),
        name="rwkv_scan_fwd",
    )(r, w, k, kk, b, vB, bd, e, diag, gather_src)


def rwkv_scan_bwd(r, w, k, kk, b, vB, dyB, ckpt, scatter_src):
    S, C = r.shape
    N, G = HEAD_DIM, SCAN_GROUP
    chunk = min(SCAN_CHUNK, S)
    nchunk, ng = S // chunk, chunk // G
    bd, e, diag = _scan_consts()

    def body(r_ref, w_ref, k_ref, kk_ref, b_ref, vB_ref, dyB_ref, ck_ref, bd_ref, e_ref, dg_ref, xs_ref,
             dr_ref, dw_ref, dk_ref, dkk_ref, db_ref, dv_ref, xr_ref, G_ref, sbuf, ubuf, vbuf, aux,
             send_sems, recv_sems, local_sem):
        c = pl.program_id(0)
        _exchange_during(c, nchunk, False, xs_ref, xr_ref, send_sems, recv_sems, local_sem)

        @pl.when(c == 0)
        def _():
            G_ref[...] = jnp.zeros_like(G_ref)

        lane_t = lax.broadcasted_iota(jnp.int32, (N, N), 1) & (G - 1)
        sub = lax.broadcasted_iota(jnp.int32, (G, C), 0)

        def fgroup(g, St):
            base = pl.multiple_of(g * G, G)
            vb = vB_ref[g]
            _scan_pair_rows(aux, base, kk_ref, w_ref, b_ref, k_ref, bd_ref[...])
            for t0 in range(0, G, 2):
                states, us, vs = _scan_pair(St, t0, base, vb, lane_t, aux, kk_ref, w_ref, b_ref, k_ref, bd_ref[...],
                                            e_ref[...])
                for i, S_before in enumerate((St, states[0])):
                    sbuf[base + t0 + i] = S_before
                    ubuf[base + t0 + i] = us[i]
                    vbuf[base + t0 + i] = vs[i]
                St = states[1]
            return St

        sbuf[chunk] = lax.fori_loop(0, ng, fgroup, ck_ref[...])

        def bgroup(gi, Gt):
            g = ng - 1 - gi
            base = pl.multiple_of(g * G, G)
            dyb = dyB_ref[g]
            rows = [jnp.zeros((G, C), F32) for _ in range(6)]
            colsum = lambda a: jnp.sum(a, axis=0, keepdims=True)
            row = lambda ref, t: ref[pl.ds(base + t, 1), :]
            b8 = b_ref[pl.ds(base, G), :]
            aux[0] = _roll_up(w_ref[pl.ds(base, G), :]) * b8
            aux[1], aux[2] = _seg_sums([_roll_up(kk_ref[pl.ds(base, G), :]) * b8, r_ref[pl.ds(base, G), :] * b8],
                                       bd_ref[...])

            def emit(rows, steps):
                d_vs = _seg_sums([Gt_ * row(k_ref, tt) for tt, Gt_, _, _ in steps], bd_ref[...])
                for (tt, Gt_, du_b, dy_b), d_vb in zip(steps, d_vs):
                    Sp, Sc = sbuf[base + tt], sbuf[base + tt + 1]
                    new = (colsum(Sc * dy_b), colsum(Gt_ * Sp), colsum(Gt_ * vbuf[base + tt]), colsum(Sp * du_b),
                           -colsum(Gt_ * ubuf[base + tt]), colsum(d_vb * dg_ref[...]))
                    rows = [jnp.where(sub == tt, n_, acc) for n_, acc in zip(new, rows)]
                return rows

            pending = None
            for t0 in reversed(range(0, G, 2)):
                t1 = t0 + 1
                arow = lambda i: aux[i, pl.ds(t0, 1), :]
                dy1, dy0 = _dot2_many([jnp.where(lane_t == t1, dyb, 0.0), jnp.where(lane_t == t0, dyb, 0.0)],
                                      e_ref[...])
                G1 = Gt + dy1 * row(r_ref, t1)
                m1, m2 = _seg_sums([G1 * row(b_ref, t1), G1 * arow(0)], bd_ref[...])
                du1 = -m1
                du0 = -(m2 + du1 * arow(1) + dy0 * arow(2))
                G0 = G1 * row(w_ref, t1) + du1 * row(kk_ref, t1) + dy0 * row(r_ref, t0)
                G_next = G0 * row(w_ref, t0) + du0 * row(kk_ref, t0)
                if pending is not None:
                    rows = emit(rows, pending)
                pending = ((t1, G1, du1, dy1), (t0, G0, du0, dy0))
                Gt = G_next
            rows = emit(rows, pending)
            for ref, val in zip((dr_ref, dw_ref, dk_ref, dkk_ref, db_ref, dv_ref), rows):
                ref[pl.ds(base, G), :] = val
            return Gt

        G_ref[...] = lax.fori_loop(0, ng, bgroup, G_ref[...])

    rev = lambda c: nchunk - 1 - c
    row = pl.BlockSpec((chunk, C), lambda c: (rev(c), 0))
    col = pl.BlockSpec((ng, N, N), lambda c: (rev(c), 0, 0))
    rshape = jax.ShapeDtypeStruct((S, C), F32)
    return pl.pallas_call(
        body,
        grid=(nchunk,),
        in_specs=[row] * 5 + [col, col, pl.BlockSpec((None, N, C), lambda c: (rev(c), 0, 0)),
                              _whole_spec(bd), _whole_spec(e), _whole_spec(diag), _ANY],
        out_specs=[row] * 6 + [_ANY],
        out_shape=[rshape] * 6 + [_exchange_out_shape(False, scatter_src)],
        scratch_shapes=[pltpu.VMEM((N, C), F32), pltpu.VMEM((chunk + 1, N, C), F32),
                        pltpu.VMEM((chunk, N, C), F32), pltpu.VMEM((chunk, N, C), F32),
                        pltpu.VMEM((3, G, C), F32)] + _EXCHANGE_SEMS,
        compiler_params=_cparams(("arbitrary",)),
        name="rwkv_scan_bwd",
    )(r, w, k, kk, b, vB, dyB, ckpt, bd, e, diag, scatter_src)


def seq_cumsum(x, reverse, name):
    S, C = x.shape
    tb = min(256, S)
    nb = S // tb

    def body(x_ref, o_ref, carry):
        i = pl.program_id(0)

        @pl.when(i == 0)
        def _():
            carry[...] = jnp.zeros_like(carry)

        ri = lax.broadcasted_iota(jnp.int32, (tb, tb), 0)
        ci = lax.broadcasted_iota(jnp.int32, (tb, tb), 1)
        tri = jnp.where((ci >= ri) if reverse else (ci <= ri), 1.0, 0.0).astype(F32)
        xb = x_ref[...]
        out = jnp.dot(tri, xb, precision=lax.Precision.HIGHEST, preferred_element_type=F32) + carry[...]
        o_ref[...] = out
        carry[...] = carry[...] + jnp.sum(xb, axis=0, keepdims=True)

    idx = (lambda i: (nb - 1 - i, 0)) if reverse else (lambda i: (i, 0))
    return pl.pallas_call(
        body,
        grid=(nb,),
        in_specs=[pl.BlockSpec((tb, C), idx)],
        out_specs=pl.BlockSpec((tb, C), idx),
        out_shape=jax.ShapeDtypeStruct((S, C), F32),
        scratch_shapes=[pltpu.VMEM((1, C), F32)],
        compiler_params=_cparams(("arbitrary",)),
        name=name,
    )(x)


FOX_STRIP = 512


def _fox_logits(q, k, cq, ck, row0, col0):
    s = _dot(q, k, _NT) * (HEAD_DIM ** -0.5) + cq - ck
    row = row0 + lax.broadcasted_iota(jnp.int32, s.shape, 0)
    col = col0 + lax.broadcasted_iota(jnp.int32, s.shape, 1)
    return jnp.where(col <= row, s, -jnp.inf)


def _fox_strips(tq):
    st = min(FOX_STRIP, tq)
    return [(r * st, slice(r * st, (r + 1) * st)) for r in range(tq // st)]


def _fox_tiles(n, by_query):
    pairs = [(i, j) for i in range(n) for j in range(i + 1)] if by_query else \
            [(i, j) for j in range(n) for i in range(j, n)]
    return (jnp.asarray(np.array([p[0] for p in pairs], np.int32)),
            jnp.asarray(np.array([p[1] for p in pairs], np.int32)))


def _fox_specs(t, Dh):
    qs = pl.BlockSpec((None, t, Dh), lambda h, s, qt, kt: (h, qt[s], 0))
    ks = pl.BlockSpec((None, t, Dh), lambda h, s, qt, kt: (h, kt[s], 0))
    cqs = pl.BlockSpec((None, t, 1), lambda h, s, qt, kt: (h, qt[s], 0))
    cks = pl.BlockSpec((None, 1, t), lambda h, s, qt, kt: (h, 0, kt[s]))
    return qs, ks, cqs, cks


def _fox_call(body, tiles, Hh, in_specs, out_specs, out_shape, scratch, name, args):
    spec = pltpu.PrefetchScalarGridSpec(num_scalar_prefetch=2, grid=(Hh, tiles[0].shape[0]), in_specs=in_specs,
                                        out_specs=out_specs, scratch_shapes=scratch)
    return pl.pallas_call(body, grid_spec=spec, out_shape=out_shape,
                          compiler_params=_cparams(("parallel", "arbitrary")), name=name)(*tiles, *args)


def fox_fwd(q, k, v, c_col, c_row):
    Hh, S, Dh = q.shape
    tq = tk = min(512, S)

    def body(qt_ref, kt_ref, q_ref, k_ref, v_ref, cq_ref, ck_ref, o_ref, lse_ref, m_s, l_s, acc_s):
        qi, ki = qt_ref[pl.program_id(1)], kt_ref[pl.program_id(1)]

        @pl.when(ki == 0)
        def _():
            m_s[...] = jnp.full_like(m_s, -jnp.inf)
            l_s[...] = jnp.zeros_like(l_s)
            acc_s[...] = jnp.zeros_like(acc_s)

        kb, vb, ck = k_ref[...], v_ref[...], ck_ref[...]
        for r0, rs in _fox_strips(tq):
            s = _fox_logits(q_ref[rs, :], kb, cq_ref[rs, :], ck, qi * tq + r0, ki * tk)
            m_old = m_s[rs, :]
            m_new = jnp.maximum(m_old, jnp.max(s, axis=-1, keepdims=True))
            alpha = jnp.exp(m_old - m_new)
            p = jnp.exp(s - m_new)
            l_s[rs, :] = alpha * l_s[rs, :] + jnp.sum(p, axis=-1, keepdims=True)
            acc_s[rs, :] = alpha * acc_s[rs, :] + _dot(p, vb, _NN)
            m_s[rs, :] = m_new

        @pl.when(ki == qi)
        def _():
            o_ref[...] = acc_s[...] / l_s[...]
            lse_ref[...] = m_s[...] + jnp.log(l_s[...])

    qs, ks, cqs, cks = _fox_specs(tq, Dh)
    return _fox_call(
        body, _fox_tiles(S // tq, True), Hh, [qs, ks, ks, cqs, cks], [qs, cqs],
        [jax.ShapeDtypeStruct((Hh, S, Dh), F32), jax.ShapeDtypeStruct((Hh, S, 1), F32)],
        [pltpu.VMEM((tq, 1), F32), pltpu.VMEM((tq, 1), F32), pltpu.VMEM((tq, Dh), F32)],
        "fox_fwd", (q, k, v, c_col, c_row))


def fox_bwd_dq(q, k, v, c_col, c_row, o, lse, do):
    Hh, S, Dh = q.shape
    tq = tk = min(512, S)

    def body(qt_ref, kt_ref, q_ref, k_ref, v_ref, cq_ref, ck_ref, o_ref, lse_ref, do_ref, dq_ref, dr_ref,
             acc_s, row_s):
        qi, ki = qt_ref[pl.program_id(1)], kt_ref[pl.program_id(1)]

        @pl.when(ki == 0)
        def _():
            acc_s[...] = jnp.zeros_like(acc_s)
            row_s[...] = jnp.zeros_like(row_s)

        kb, vb, ck = k_ref[...], v_ref[...], ck_ref[...]
        for r0, rs in _fox_strips(tq):
            s = _fox_logits(q_ref[rs, :], kb, cq_ref[rs, :], ck, qi * tq + r0, ki * tk)
            p = jnp.exp(s - lse_ref[rs, :])
            do_t = do_ref[rs, :]
            delta = jnp.sum(do_t * o_ref[rs, :], axis=-1, keepdims=True)
            ds = p * (_dot(do_t, vb, _NT) - delta)
            acc_s[rs, :] += _dot(ds, kb, _NN)
            row_s[rs, :] += jnp.sum(ds, axis=-1, keepdims=True)

        @pl.when(ki == qi)
        def _():
            dq_ref[...] = acc_s[...] * (HEAD_DIM ** -0.5)
            dr_ref[...] = row_s[...]

    qs, ks, cqs, cks = _fox_specs(tq, Dh)
    return _fox_call(
        body, _fox_tiles(S // tq, True), Hh, [qs, ks, ks, cqs, cks, qs, cqs, qs], [qs, cqs],
        [jax.ShapeDtypeStruct((Hh, S, Dh), F32), jax.ShapeDtypeStruct((Hh, S, 1), F32)],
        [pltpu.VMEM((tq, Dh), F32), pltpu.VMEM((tq, 1), F32)],
        "fox_bwd_dq", (q, k, v, c_col, c_row, o, lse, do))


def fox_bwd_dkv(q, k, v, c_col, c_row, o, lse, do):
    Hh, S, Dh = q.shape
    tq = tk = min(512, S)
    nq = S // tq

    def body(qt_ref, kt_ref, q_ref, k_ref, v_ref, cq_ref, ck_ref, o_ref, lse_ref, do_ref, dk_ref, dv_ref, dc_ref,
             dk_s, dv_s, dc_s):
        qi, ki = qt_ref[pl.program_id(1)], kt_ref[pl.program_id(1)]

        @pl.when(qi == ki)
        def _():
            dk_s[...] = jnp.zeros_like(dk_s)
            dv_s[...] = jnp.zeros_like(dv_s)
            dc_s[...] = jnp.zeros_like(dc_s)

        kb, vb, ck = k_ref[...], v_ref[...], ck_ref[...]
        dk, dv, dc = dk_s[...], dv_s[...], dc_s[...]
        for r0, rs in _fox_strips(tq):
            q_t = q_ref[rs, :]
            s = _fox_logits(q_t, kb, cq_ref[rs, :], ck, qi * tq + r0, ki * tk)
            p = jnp.exp(s - lse_ref[rs, :])
            do_t = do_ref[rs, :]
            delta = jnp.sum(do_t * o_ref[rs, :], axis=-1, keepdims=True)
            ds = p * (_dot(do_t, vb, _NT) - delta)
            dv = dv + _dot(p, do_t, _TN)
            dk = dk + _dot(ds, q_t, _TN)
            dc = dc + jnp.sum(ds, axis=0, keepdims=True)
        dk_s[...], dv_s[...], dc_s[...] = dk, dv, dc

        @pl.when(qi == nq - 1)
        def _():
            dk_ref[...] = dk_s[...] * (HEAD_DIM ** -0.5)
            dv_ref[...] = dv_s[...]
            dc_ref[...] = dc_s[...]

    qs, ks, cqs, cks = _fox_specs(tq, Dh)
    return _fox_call(
        body, _fox_tiles(nq, False), Hh, [qs, ks, ks, cqs, cks, qs, cqs, qs], [ks, ks, cks],
        [jax.ShapeDtypeStruct((Hh, S, Dh), F32), jax.ShapeDtypeStruct((Hh, S, Dh), F32),
         jax.ShapeDtypeStruct((Hh, 1, S), F32)],
        [pltpu.VMEM((tk, Dh), F32), pltpu.VMEM((tk, Dh), F32), pltpu.VMEM((1, tk), F32)],
        "fox_bwd_dkv", (q, k, v, c_col, c_row, o, lse, do))


def _heads(a, nh):
    S = a.shape[0]
    return a.reshape(S, nh, HEAD_DIM).transpose(1, 0, 2)


def _unheads(a):
    nh, S, _ = a.shape
    return a.transpose(1, 0, 2).reshape(S, nh * HEAD_DIM)


def _shift_down(a):
    return jnp.pad(a[:-1], ((1, 0), (0, 0)))


def _shift_up(a):
    return jnp.pad(a[1:], ((0, 1), (0, 0)))


def _block_diag_ones():
    i = np.arange(RWKV_DIM) // HEAD_DIM
    return jnp.asarray((i[:, None] == i[None, :]).astype(np.float32))


FFN_ROWS = 1024
FFN_COLS = 256


def _ffn_specs(S, F, tm, fc):
    nf = F // fc
    row = pl.BlockSpec((tm, D_MODEL), lambda i, j: (i, 0))
    vec = pl.BlockSpec((1, D_MODEL), lambda i, j: (0, 0))
    wg = pl.BlockSpec((D_MODEL, fc), lambda i, j: (0, j))
    wu = pl.BlockSpec((D_MODEL, fc), lambda i, j: (0, nf + j))
    wd = pl.BlockSpec((fc, D_MODEL), lambda i, j: (j, 0))
    hid = pl.BlockSpec((tm, fc), lambda i, j: (i, j))
    return nf, row, vec, wg, wu, wd, hid


def ffn_fwd(x, g_norm, w_gu, w_down, tag):
    S, F = x.shape[0], w_down.shape[0]
    tm, fc = min(FFN_ROWS, S), FFN_COLS
    nf, row, vec, wg, wu, wd, _ = _ffn_specs(S, F, tm, fc)

    def body(x_ref, g_ref, wg_ref, wu_ref, wd_ref, o_ref, hn_ref, hn_s, acc):
        j = pl.program_id(1)

        @pl.when(j == 0)
        def _():
            hn_s[...] = _rms(x_ref[...], g_ref[...]).astype(BF16)
            hn_ref[...] = hn_s[...]
            acc[...] = jnp.zeros_like(acc)

        g = _dot(hn_s[...], wg_ref[...], _NN)
        u = _dot(hn_s[...], wu_ref[...], _NN)
        acc[...] += _dot(g * _sigmoid_tanh(g) * u, wd_ref[...], _NN)

        @pl.when(j == nf - 1)
        def _():
            o_ref[...] = x_ref[...] + 0.5 * acc[...]

    out, hn = pl.pallas_call(
        body,
        grid=(S // tm, nf),
        in_specs=[row, vec, wg, wu, wd],
        out_specs=[row, row],
        out_shape=[jax.ShapeDtypeStruct((S, D_MODEL), F32), jax.ShapeDtypeStruct((S, D_MODEL), BF16)],
        scratch_shapes=[pltpu.VMEM((tm, D_MODEL), BF16), pltpu.VMEM((tm, D_MODEL), F32)],
        compiler_params=_cparams(("parallel", "arbitrary")),
        name=tag + "_fwd",
    )(x, g_norm, w_gu, w_gu, w_down)
    return out, (x, hn)


def ffn_bwd(dy, saved, g_norm, w_gu, w_down, tag):
    x, hn = saved
    S, F = x.shape[0], w_down.shape[0]
    tm, fc = min(FFN_ROWS, S), FFN_COLS
    nf, row, vec, wg, wu, wd, hid = _ffn_specs(S, F, tm, fc)

    def body(dy_ref, x_ref, hn_ref, g_ref, wg_ref, wu_ref, wd_ref, dx_ref, dgn_ref, a_ref, dg_ref, du_ref,
             dyh_s, dhn):
        i, j = pl.program_id(0), pl.program_id(1)

        @pl.when(j == 0)
        def _():
            dyh_s[...] = (0.5 * dy_ref[...]).astype(BF16)
            dhn[...] = jnp.zeros_like(dhn)

        hn_t = hn_ref[...]
        g = _dot(hn_t, wg_ref[...], _NN)
        u = _dot(hn_t, wu_ref[...], _NN)
        da = _dot(dyh_s[...], wd_ref[...], _NT)
        sig = _sigmoid_tanh(g)
        gs = g * sig
        a_ref[...] = (gs * u).astype(BF16)
        dg = ((da * u) * (sig + gs * (1.0 - sig))).astype(BF16)
        du = (da * gs).astype(BF16)
        dg_ref[...] = dg
        du_ref[...] = du
        dhn[...] += _dot(jnp.concatenate([dg, du], axis=1),
                         jnp.concatenate([wg_ref[...], wu_ref[...]], axis=1), _NT)

        @pl.when(j == nf - 1)
        def _():
            _, vjp_n = jax.vjp(_rms, x_ref[...], g_ref[...])
            dx, dgn = vjp_n(dhn[...])
            dx_ref[...] = dy_ref[...] + dx

            @pl.when(i == 0)
            def _():
                dgn_ref[...] = jnp.zeros_like(dgn_ref)

            dgn_ref[...] += dgn

    hshape = jax.ShapeDtypeStruct((S, F), BF16)
    dx, dgn, act, dg, du = pl.pallas_call(
        body,
        grid=(S // tm, nf),
        in_specs=[row, row, row, vec, wg, wu, wd],
        out_specs=[row, vec, hid, hid, hid],
        out_shape=[jax.ShapeDtypeStruct((S, D_MODEL), F32), jax.ShapeDtypeStruct((1, D_MODEL), F32),
                   hshape, hshape, hshape],
        scratch_shapes=[pltpu.VMEM((tm, D_MODEL), BF16), pltpu.VMEM((tm, D_MODEL), F32)],
        compiler_params=_cparams(("arbitrary", "arbitrary")),
        name=tag + "_bwd",
    )(dy, x, hn, g_norm, w_gu, w_gu, w_down)
    d_wdown = matmul(act, dy, "tn", tag + "_dwd", out_dtype=BF16, scale=0.5)
    d_wgu = jnp.concatenate([matmul(hn, dg, "tn", tag + "_dwg", out_dtype=BF16),
                             matmul(hn, du, "tn", tag + "_dwu", out_dtype=BF16)], axis=1)
    return dx, dgn, d_wgu, d_wdown


def ple_fwd(x, p_i, g_norm, w_gate, w_proj, tag):
    hn, = rowwise(_f_rms, [x], [g_norm], [(D_MODEL, BF16)], tag + "_rms")
    z = matmul(hn, w_gate, "nn", tag + "_gate")
    pp = matmul(p_i, w_proj, "nn", tag + "_proj")
    out, = rowwise(_f_ple, [x, z, pp], [], [(D_MODEL, F32)], tag + "_mix")
    return out, (x, hn, z, pp)


def ple_bwd(dy, saved, p_i, g_norm, w_gate, tag):
    x, hn, z, pp = saved
    (dz, dpp), _ = rowwise_vjp(_f_ple, [x, z, pp], [], [dy], tag + "_dmix", need=[False, True, True],
                               row_dtype=BF16)
    d_wproj = matmul(p_i, dpp, "tn", tag + "_dwp", out_dtype=BF16)
    d_wgate = matmul(hn, dz, "tn", tag + "_dwg", out_dtype=BF16)
    dhn = matmul(dz, w_gate, "nt", tag + "_dhn")
    (dx,), (dgn,) = rowwise_vjp(_f_rms_res, [x], [g_norm], [dhn, dy], tag + "_drms")
    return dx, dgn, d_wgate, d_wproj


def _swa_consts(sinks):
    slopes = np.asarray([2.0 ** (-(i + 1)) for i in range(SWA_HEADS)], np.float32)
    slope_col = jnp.asarray(np.repeat(slopes, BLOCK).reshape(SWA_KV_HEADS, SWA_GROUP * BLOCK, 1))
    sink_col = jnp.repeat(sinks.reshape(SWA_HEADS), BLOCK).reshape(SWA_KV_HEADS, SWA_GROUP * BLOCK, 1)
    return sink_col, slope_col


def even_mix_fwd(x, W, gather_src):
    S = x.shape[0]
    hn, = rowwise(_f_rms, [x], [W["mix_norm0"]], [(D_MODEL, BF16)], "emix_rms")
    proj = matmul(hn, W["even_w_in"], "nn", "emix_in")
    qa = _heads(proj[:, :SWA_Q], SWA_HEADS).reshape(SWA_KV_HEADS, SWA_GROUP, S, HEAD_DIM)
    ka = _heads(proj[:, SWA_Q:SWA_Q + SWA_KV], SWA_KV_HEADS)
    va = _heads(proj[:, SWA_Q + SWA_KV:SWA_COLS], SWA_KV_HEADS)
    sink_col, slope_col = _swa_consts(W["swa_sinks"])
    ya = swa_fwd(qa, ka, va, sink_col, slope_col)
    ya = _unheads(ya.reshape(SWA_HEADS, S, HEAD_DIM))
    hb = proj[:, SWA_COLS:]
    h, = rowwise(_f_mix, [hb, _shift_down(hb)], [W["rwkv_mu"]], [(hb.shape[1], F32)], "rwkv_shift")
    hr, hk, hv = h[:, :512], h[:, 512:1024], h[:, 1024:1536]
    hw, ha, hg = h[:, 1536:1600], h[:, 1600:1664], h[:, 1664:1792]
    bd = _block_diag_ones()
    pre_params = [W["rwkv_w0"], W["rwkv_w2"], W["rwkv_a0"], W["rwkv_a2"], W["rwkv_g2"], W["rwkv_k_k"],
                  W["rwkv_k_a"]]
    decay, k2, kk, b, g = rowwise(_f_rwkv_pre, [hk, hw, ha, hg], pre_params + [bd],
                                  [(RWKV_DIM, F32)] * 5, "rwkv_pre")
    vT = _to_colblocks(hv)
    y, ckpt, gathered = rwkv_scan_fwd(hr, decay, k2, kk, b, vT, gather_src)
    post_params = [W["rwkv_ln_w"], W["rwkv_ln_b"], W["rwkv_r_k"]]
    yb, = rowwise(_f_rwkv_post, [y, hr, k2, hv, g], post_params + [bd], [(RWKV_DIM, F32)], "rwkv_post")
    cat = jnp.concatenate([ya, yb], axis=1).astype(BF16)
    out = matmul(cat, W["even_w_out"], "nn", "emix_out", res=x)
    saved = (x, hn, qa, ka, va, sink_col, slope_col, hb, hr, hk, hv, hw, ha, hg, decay, k2, kk, b, g, vT,
             ckpt, y, cat)
    return out, saved, gathered


def even_mix_bwd(dy, saved, W, scatter_src):
    (x, hn, qa, ka, va, sink_col, slope_col, hb, hr, hk, hv, hw, ha, hg, decay, k2, kk, b, g, vT, ckpt, y,
     cat) = saved
    S = x.shape[0]
    grads = {}
    dcat = matmul(dy, W["even_w_out"], "nt", "emix_dcat")
    grads["even_w_out"] = matmul(cat, dy, "tn", "emix_dwout", out_dtype=BF16)
    dya, dyb = dcat[:, :SWA_Q], dcat[:, SWA_Q:]
    dya_h = _heads(dya, SWA_HEADS).reshape(SWA_KV_HEADS, SWA_GROUP, S, HEAD_DIM)
    dqa, dkp, dkc, dvp, dvc, dsink = swa_bwd(qa, ka, va, sink_col, slope_col, dya_h)
    shift_blk = lambda a: jnp.pad(a[:, BLOCK:], ((0, 0), (0, BLOCK), (0, 0)))
    dka = dkc + shift_blk(dkp)
    dva = dvc + shift_blk(dvp)
    grads["swa_sinks"] = dsink.reshape(SWA_HEADS, BLOCK).sum(axis=1).reshape(1, SWA_HEADS)
    dqa = _unheads(dqa.reshape(SWA_HEADS, S, HEAD_DIM))
    dka, dva = _unheads(dka), _unheads(dva)
    bd = _block_diag_ones()
    post_params = [W["rwkv_ln_w"], W["rwkv_ln_b"], W["rwkv_r_k"]]
    (d_y, d_r1, d_k2a, d_v1, d_g), (d_lnw, d_lnb, d_rk) = rowwise_vjp(
        _f_rwkv_post, [y, hr, k2, hv, g], post_params, [dyb], "rwkv_dpost", consts=[bd], tm=128)
    grads["rwkv_ln_w"], grads["rwkv_ln_b"], grads["rwkv_r_k"] = d_lnw, d_lnb, d_rk
    d_r2, d_w, d_k2b, d_kk, d_b, d_v2, exchanged = rwkv_scan_bwd(hr, decay, k2, kk, b, vT, _to_colblocks(d_y), ckpt,
                                                                  scatter_src)
    pre_params = [W["rwkv_w0"], W["rwkv_w2"], W["rwkv_a0"], W["rwkv_a2"], W["rwkv_g2"], W["rwkv_k_k"],
                  W["rwkv_k_a"]]
    (d_hk, d_hw, d_ha, d_hg), dpre = rowwise_vjp(
        _f_rwkv_pre, [hk, hw, ha, hg], pre_params, [d_w, d_k2a + d_k2b, d_kk, d_b, d_g], "rwkv_dpre",
        consts=[bd], tm=128)
    for nm, gval in zip(["rwkv_w0", "rwkv_w2", "rwkv_a0", "rwkv_a2", "rwkv_g2", "rwkv_k_k", "rwkv_k_a"], dpre):
        grads[nm] = gval
    d_h = jnp.concatenate([d_r1 + d_r2, d_hk, d_v1 + d_v2, d_hw, d_ha, d_hg], axis=1)
    (d_hb, d_sh), (d_mu,) = rowwise_vjp(_f_mix, [hb, _shift_down(hb)], [W["rwkv_mu"]], [d_h], "rwkv_dshift")
    grads["rwkv_mu"] = d_mu
    d_hb = d_hb + _shift_up(d_sh)
    dproj = jnp.concatenate([dqa, dka, dva, d_hb], axis=1).astype(BF16)
    grads["even_w_in"] = matmul(hn, dproj, "tn", "emix_dwin", out_dtype=BF16)
    dhn = matmul(dproj, W["even_w_in"], "nt", "emix_dhn")
    (dx,), (dgn,) = rowwise_vjp(_f_rms_res, [x], [W["mix_norm0"]], [dhn, dy], "emix_drms")
    grads["mix_norm0"] = dgn
    return dx, grads, exchanged


def odd_mix_fwd(x, W):
    S = x.shape[0]
    hn, = rowwise(_f_rms, [x], [W["mix_norm1"]], [(D_MODEL, BF16)], "omix_rms")
    proj = matmul(hn, W["fox_w_in"], "nn", "omix_in")
    q = _heads(proj[:, :FOX_DIM], FOX_HEADS).astype(BF16)
    k = _heads(proj[:, FOX_DIM:2 * FOX_DIM], FOX_HEADS).astype(BF16)
    v = _heads(proj[:, 2 * FOX_DIM:3 * FOX_DIM], FOX_HEADS).astype(BF16)
    fz = proj[:, 3 * FOX_DIM:]
    logf, = rowwise(_f_logf, [fz], [W["fox_b_f"]], [(128, F32)], "fox_logf")
    c = seq_cumsum(logf, False, "fox_cumsum")[:, :FOX_HEADS]
    c_col = c.T.reshape(FOX_HEADS, S, 1)
    c_row = c.T.reshape(FOX_HEADS, 1, S)
    o, lse = fox_fwd(q, k, v, c_col, c_row)
    yc = _unheads(o).astype(BF16)
    out = matmul(yc, W["fox_w_out"], "nn", "omix_out", res=x)
    return out, (x, hn, q, k, v, fz, c_col, c_row, o, lse, yc)


def odd_mix_bwd(dy, saved, W):
    x, hn, q, k, v, fz, c_col, c_row, o, lse, yc = saved
    S = x.shape[0]
    grads = {}
    dyc = matmul(dy, W["fox_w_out"], "nt", "omix_dyc")
    grads["fox_w_out"] = matmul(yc, dy, "tn", "omix_dwout", out_dtype=BF16)
    do = _heads(dyc, FOX_HEADS)
    dq, drow = fox_bwd_dq(q, k, v, c_col, c_row, o, lse, do)
    dk, dv, dcol = fox_bwd_dkv(q, k, v, c_col, c_row, o, lse, do)
    dc = (drow.reshape(FOX_HEADS, S) - dcol.reshape(FOX_HEADS, S)).T
    dc = jnp.pad(dc, ((0, 0), (0, 128 - FOX_HEADS)))
    dlogf = seq_cumsum(dc, True, "fox_rcumsum")
    (dfz,), (dbf,) = rowwise_vjp(_f_logf, [fz], [W["fox_b_f"]], [dlogf], "fox_dlogf")
    grads["fox_b_f"] = dbf
    dproj = jnp.concatenate([_unheads(dq), _unheads(dk), _unheads(dv), dfz], axis=1).astype(BF16)
    grads["fox_w_in"] = matmul(hn, dproj, "tn", "omix_dwin", out_dtype=BF16)
    dhn = matmul(dproj, W["fox_w_in"], "nt", "omix_dhn")
    (dx,), (dgn,) = rowwise_vjp(_f_rms_res, [x], [W["mix_norm1"]], [dhn, dy], "omix_drms")
    grads["mix_norm1"] = dgn
    return dx, grads


def device_step(x, p, target, W, gather_src, layer1_weights, layer1_grads):
    W = dict(W)
    saved = []
    h = x
    for i in range(2):
        h, s1 = ffn_fwd(h, W[f"ffn1_norm{i}"], W[f"ffn1_w_gu{i}"], W[f"ffn1_w_down{i}"], f"ffn1_{i}")
        if i == 0:
            h, s2, gathered = even_mix_fwd(h, W, gather_src)
            W.update(layer1_weights(gathered))
        else:
            h, s2 = odd_mix_fwd(h, W)
        h, s3 = ffn_fwd(h, W[f"ffn2_norm{i}"], W[f"ffn2_w_gu{i}"], W[f"ffn2_w_down{i}"], f"ffn2_{i}")
        h, s4 = ple_fwd(h, p[i], W[f"ple_norm{i}"], W[f"ple_w_gate{i}"], W[f"ple_w_proj{i}"], f"ple_{i}")
        saved.append((s1, s2, s3, s4))
    dh, d_final, loss = loss_head(h, target, W["final_norm"])
    G = {"final_norm": d_final}
    for i in (1, 0):
        s1, s2, s3, s4 = saved[i]
        dh, G[f"ple_norm{i}"], G[f"ple_w_gate{i}"], G[f"ple_w_proj{i}"] = ple_bwd(
            dh, s4, p[i], W[f"ple_norm{i}"], W[f"ple_w_gate{i}"], f"ple_{i}")
        dh, G[f"ffn2_norm{i}"], G[f"ffn2_w_gu{i}"], G[f"ffn2_w_down{i}"] = ffn_bwd(
            dh, s3, W[f"ffn2_norm{i}"], W[f"ffn2_w_gu{i}"], W[f"ffn2_w_down{i}"], f"ffn2_{i}")
        if i == 0:
            dh, gm, exchanged = even_mix_bwd(dh, s2, W, layer1_grads(G))
        else:
            dh, gm = odd_mix_bwd(dh, s2, W)
        G.update(gm)
        dh, G[f"ffn1_norm{i}"], G[f"ffn1_w_gu{i}"], G[f"ffn1_w_down{i}"] = ffn_bwd(
            dh, s1, W[f"ffn1_norm{i}"], W[f"ffn1_w_gu{i}"], W[f"ffn1_w_down{i}"], f"ffn1_{i}")
    return loss, dh, G, exchanged


_MESH = pl.DeviceIdType.MESH
_ANY = pl.BlockSpec(memory_space=pl.ANY)


def all_gather(x, name):
    def body(x_ref, out_ref, send_sems, recv_sems, local_sem):
        x_, y_, c_ = lax.axis_index("x"), lax.axis_index("y"), lax.axis_index("c")
        me, sibling = (x_, y_, c_), (x_, y_, 1 - c_)
        chips = [(1 - x_, y_), (x_, 1 - y_), (1 - x_, 1 - y_)]

        def slot(px, py, pc):
            return out_ref.at[4 * px + 2 * py + pc]

        def copy(k, block, to, src=None):
            return pltpu.make_async_remote_copy(
                src_ref=slot(*block) if src is None else src, dst_ref=slot(*block),
                send_sem=send_sems.at[k], recv_sem=recv_sems.at[k], device_id=to, device_id_type=_MESH)

        mine = pltpu.make_async_copy(x_ref, slot(*me), local_sem)
        mine.start()
        first = [copy(0, me, sibling, src=x_ref)]
        first += [copy(1 + j, me, (*chip, c_), src=x_ref) for j, chip in enumerate(chips)]
        for cp in first:
            cp.start()
        passed = [copy(4 + j, (*chip, c_), sibling) for j, chip in enumerate(chips)]
        for j, chip in enumerate(chips):
            copy(1 + j, (*chip, c_), me).wait_recv()
            passed[j].start()
        copy(0, sibling, me).wait_recv()
        for j, chip in enumerate(chips):
            copy(4 + j, (*chip, 1 - c_), me).wait_recv()
        for cp in first + passed:
            cp.wait_send()
        mine.wait()

    return pl.pallas_call(
        body,
        out_shape=jax.ShapeDtypeStruct((N_DEV,) + x.shape, x.dtype),
        in_specs=[_ANY],
        out_specs=_ANY,
        scratch_shapes=[pltpu.SemaphoreType.DMA((7,)), pltpu.SemaphoreType.DMA((7,)), pltpu.SemaphoreType.DMA(())],
        name=name,
    )(x)


def _direct_exchange(gather, s_ref, r_ref, send_sems, recv_sems, local_sem):
    x_, y_, c_ = lax.axis_index("x"), lax.axis_index("y"), lax.axis_index("c")
    my = 4 * x_ + 2 * y_ + c_
    copies = [pltpu.make_async_copy(s_ref if gather else s_ref.at[my], r_ref.at[my], local_sem)]
    for m in range(1, N_DEV):
        px = 1 - x_ if (m >> 2) & 1 else x_
        py = 1 - y_ if (m >> 1) & 1 else y_
        pc = 1 - c_ if m & 1 else c_
        copies.append(pltpu.make_async_remote_copy(
            src_ref=s_ref if gather else s_ref.at[4 * px + 2 * py + pc], dst_ref=r_ref.at[my],
            send_sem=send_sems.at[m - 1], recv_sem=recv_sems.at[m - 1],
            device_id=(px, py, pc), device_id_type=_MESH))
    return copies


_EXCHANGE_SEMS = [pltpu.SemaphoreType.DMA((7,)), pltpu.SemaphoreType.DMA((7,)), pltpu.SemaphoreType.DMA(())]


def _exchange_during(step, n_steps, gather, s_ref, r_ref, send_sems, recv_sems, local_sem):
    copies = _direct_exchange(gather, s_ref, r_ref, send_sems, recv_sems, local_sem)

    @pl.when(step == 0)
    def _():
        for cp in copies:
            cp.start()

    @pl.when(step == n_steps - 1)
    def _():
        for cp in copies:
            cp.wait()


def _exchange_out_shape(gather, src):
    return jax.ShapeDtypeStruct(((N_DEV,) + src.shape) if gather else src.shape, src.dtype)


def all_to_all(send, name):
    def body(s_ref, r_ref, send_sems, recv_sems, local_sem):
        copies = _direct_exchange(False, s_ref, r_ref, send_sems, recv_sems, local_sem)
        for cp in copies:
            cp.start()
        for cp in copies:
            cp.wait()

    return pl.pallas_call(
        body,
        out_shape=jax.ShapeDtypeStruct(send.shape, send.dtype),
        in_specs=[_ANY],
        out_specs=_ANY,
        scratch_shapes=[pltpu.SemaphoreType.DMA((7,)), pltpu.SemaphoreType.DMA((7,)), pltpu.SemaphoreType.DMA(())],
        name=name,
    )(send)


def adamw(w, m, v, parts, name, tm=256):
    R, C = w.shape
    tm = _pick(R, tm, 8) if R >= 8 else R

    def body(w_ref, m_ref, v_ref, p_ref, g_ref, d_ref, nm_ref, nv_ref):
        g = p_ref[0].astype(F32)
        for s in range(1, N_DEV):
            g = g + p_ref[s].astype(F32)
        nm = ADAM_B1 * m_ref[...] + (1.0 - ADAM_B1) * g
        nv = ADAM_B2 * v_ref[...] + (1.0 - ADAM_B2) * (g * g)
        m_hat = nm / (1.0 - ADAM_B1 ** ADAM_STEP)
        v_hat = nv / (1.0 - ADAM_B2 ** ADAM_STEP)
        g_ref[...] = g
        d_ref[...] = -ADAM_LR * (m_hat / (jnp.sqrt(v_hat) + ADAM_EPS) + ADAM_WD * w_ref[...])
        nm_ref[...] = nm
        nv_ref[...] = nv

    row = pl.BlockSpec((tm, C), lambda i: (i, 0))
    out = jax.ShapeDtypeStruct((R, C), F32)
    return pl.pallas_call(
        body,
        grid=(R // tm,),
        in_specs=[row, row, row, pl.BlockSpec((N_DEV, tm, C), lambda i: (0, i, 0))],
        out_specs=[row] * 4,
        out_shape=[out] * 4,
        compiler_params=_cparams(("parallel",)),
        name=name,
    )(w, m, v, parts)


_WEIGHTS = ["ffn1_norm", "ffn1_w_gu", "ffn1_w_down", "mix_norm", "ffn2_norm", "ffn2_w_gu", "ffn2_w_down",
            "ple_norm", "ple_w_gate", "ple_w_proj", "even_w_in", "even_w_out", "swa_sinks", "rwkv_mu",
            "rwkv_w0", "rwkv_w2", "rwkv_a0", "rwkv_a2", "rwkv_g2", "rwkv_k_k", "rwkv_k_a", "rwkv_r_k",
            "rwkv_ln_w", "rwkv_ln_b", "fox_w_in", "fox_b_f", "fox_w_out", "final_norm"]
_SHARD_AXIS = {"ffn1_w_gu": 2, "ffn1_w_down": 1, "ffn2_w_gu": 2, "ffn2_w_down": 1, "ple_w_gate": 1,
               "ple_w_proj": 2, "even_w_in": 2, "even_w_out": 1, "rwkv_w2": 2, "rwkv_a2": 2, "rwkv_g2": 2,
               "fox_w_in": 2, "fox_w_out": 1}
_SHARDED = [n for n in _WEIGHTS if n in _SHARD_AXIS]
_REPLICATED = [n for n in _WEIGHTS if n not in _SHARD_AXIS]
_PER_LAYER = ("ffn1_w_gu", "ffn1_w_down", "ffn2_w_gu", "ffn2_w_down", "ple_w_gate", "ple_w_proj")
_PIECES = [[(n, 0) for n in _PER_LAYER] + [(n, 0) for n in ("even_w_in", "even_w_out", "rwkv_w2", "rwkv_a2", "rwkv_g2")],
           [(n, 1) for n in _PER_LAYER] + [("fox_w_in", 0), ("fox_w_out", 0)]]
_PACK_LANES = 1024
_PACK_ROW_TILE = 256


def _piece_key(piece):
    name, idx = piece
    return f"{name}{idx}" if name in _PER_LAYER else name


def _pack_rows(arrs):
    flat = jnp.concatenate([a.reshape(-1, _PACK_LANES) for a in arrs], axis=0)
    return jnp.pad(flat, ((0, -flat.shape[0] % _PACK_ROW_TILE), (0, 0)))


def _unpack_rows(flat, shapes):
    out, r0 = [], 0
    for shp in shapes:
        n = math.prod(shp) // _PACK_LANES
        out.append(flat[r0:r0 + n].reshape(shp))
        r0 += n
    return out


def _unshard(gathered, pieces, shapes):
    full, r0 = {}, 0
    for piece, shp in zip(pieces, shapes):
        n = math.prod(shp) // _PACK_LANES
        seg = gathered[:, r0:r0 + n].reshape((N_DEV,) + shp)
        ax = _SHARD_AXIS[piece[0]] - 1
        seg = jnp.moveaxis(seg, 0, ax)
        full[_piece_key(piece)] = seg.reshape(shp[:ax] + (N_DEV * shp[ax],) + shp[ax + 1:])
        r0 += n
    return full


def _to_shards(full, pieces, shapes):
    segs = []
    for piece, shp in zip(pieces, shapes):
        ax = _SHARD_AXIS[piece[0]] - 1
        a = full[_piece_key(piece)].astype(BF16).reshape(shp[:ax] + (N_DEV, shp[ax]) + shp[ax + 1:])
        segs.append(jnp.moveaxis(a, ax, 0).reshape(N_DEV, -1, _PACK_LANES))
    flat = jnp.concatenate(segs, axis=1)
    return jnp.pad(flat, ((0, 0), (0, -flat.shape[1] % _PACK_ROW_TILE), (0, 0)))


def _layer_weights(full):
    W = dict(full)
    if "fox_w_in" in W:
        W["fox_w_in"] = jnp.pad(W["fox_w_in"], ((0, 0), (0, FOX_IN_PAD - W["fox_w_in"].shape[1])))
    for n in ("rwkv_w2", "rwkv_a2", "rwkv_g2"):
        if n in W:
            W[n] = W[n].astype(F32)
    return W


def _pack_small(vals):
    flat = jnp.concatenate([v.reshape(1, -1) for v in vals], axis=1)
    n = flat.shape[1]
    return jnp.pad(flat, ((0, 0), (0, -n % 128)))


def _unpack_small(flat, shapes):
    out, c0 = [], 0
    for shp in shapes:
        n = math.prod(shp)
        out.append(flat[0, c0:c0 + n].reshape(shp))
        c0 += n
    return out


def kernel(x, p, ffn1_norm, ffn1_w_gu, ffn1_w_down, mix_norm, ffn2_norm, ffn2_w_gu, ffn2_w_down, ple_norm, ple_w_gate, ple_w_proj, even_w_in, even_w_out, swa_sinks, rwkv_mu, rwkv_w0, rwkv_w2, rwkv_a0, rwkv_a2, rwkv_g2, rwkv_k_k, rwkv_k_a, rwkv_r_k, rwkv_ln_w, rwkv_ln_b, fox_w_in, fox_b_f, fox_w_out, final_norm, loss_target, m_ffn1_norm, m_ffn1_w_gu, m_ffn1_w_down, m_mix_norm, m_ffn2_norm, m_ffn2_w_gu, m_ffn2_w_down, m_ple_norm, m_ple_w_gate, m_ple_w_proj, m_even_w_in, m_even_w_out, m_swa_sinks, m_rwkv_mu, m_rwkv_w0, m_rwkv_w2, m_rwkv_a0, m_rwkv_a2, m_rwkv_g2, m_rwkv_k_k, m_rwkv_k_a, m_rwkv_r_k, m_rwkv_ln_w, m_rwkv_ln_b, m_fox_w_in, m_fox_b_f, m_fox_w_out, m_final_norm, v_ffn1_norm, v_ffn1_w_gu, v_ffn1_w_down, v_mix_norm, v_ffn2_norm, v_ffn2_w_gu, v_ffn2_w_down, v_ple_norm, v_ple_w_gate, v_ple_w_proj, v_even_w_in, v_even_w_out, v_swa_sinks, v_rwkv_mu, v_rwkv_w0, v_rwkv_w2, v_rwkv_a0, v_rwkv_a2, v_rwkv_g2, v_rwkv_k_k, v_rwkv_k_a, v_rwkv_r_k, v_rwkv_ln_w, v_rwkv_ln_b, v_fox_w_in, v_fox_b_f, v_fox_w_out, v_final_norm):
    given = dict(locals())
    w = {n: given[n] for n in _WEIGHTS}
    m = {n: given["m_" + n] for n in _WEIGHTS}
    v = {n: given["v_" + n] for n in _WEIGHTS}
    small_shapes = [w[n].shape for n in _REPLICATED]
    piece = lambda d, pc: d[pc[0]][pc[1]]
    shapes = [[piece(w, pc).shape for pc in pieces] for pieces in _PIECES]
    w_rows = [_pack_rows([piece(w, pc) for pc in pieces]) for pieces in _PIECES]

    W = _layer_weights(_unshard(all_gather(w_rows[0].astype(BF16), "weights_all_gather"), _PIECES[0], shapes[0]))
    for i in range(2):
        for n in ("ffn1_norm", "mix_norm", "ffn2_norm", "ple_norm"):
            W[f"{n}{i}"] = w[n][i].reshape(1, -1)
    for n in ("swa_sinks", "rwkv_mu", "rwkv_w0", "rwkv_a0", "rwkv_k_k", "rwkv_k_a", "rwkv_r_k", "rwkv_ln_w",
              "rwkv_ln_b", "final_norm"):
        W[n] = w[n].reshape(1, -1)
    n_f = fox_b_f.shape[1]
    W["fox_b_f"] = jnp.pad(fox_b_f.reshape(1, n_f), ((0, 0), (0, 128 - n_f)))
    n_fox = fox_w_in.shape[2] * N_DEV

    def layer1_weights(gathered):
        return _layer_weights(_unshard(gathered, _PIECES[1], shapes[1]))

    def layer1_grads(G):
        G = dict(G, fox_w_in=G["fox_w_in"][:, :n_fox])
        return _to_shards(G, _PIECES[1], shapes[1])

    loss_row, dx, G, parts1 = device_step(x[0], p[:, 0], loss_target[0], W, w_rows[1].astype(BF16),
                                          layer1_weights, layer1_grads)

    parts = [all_to_all(_to_shards(G, _PIECES[0], shapes[0]), "grads_all_to_all"), parts1]
    out_g, out_d, out_m, out_v = {}, {}, {}, {}
    for li, pieces in enumerate(_PIECES):
        res = adamw(w_rows[li], _pack_rows([piece(m, pc) for pc in pieces]),
                    _pack_rows([piece(v, pc) for pc in pieces]), parts[li], f"adamw_sharded{li}")
        for out, rows in zip((out_g, out_d, out_m, out_v), res):
            for pc, a in zip(pieces, _unpack_rows(rows, shapes[li])):
                out.setdefault(pc[0], {})[pc[1]] = a
    for out in (out_g, out_d, out_m, out_v):
        for n in _SHARDED:
            out[n] = jnp.stack([out[n][i] for i in sorted(out[n])])

    gsmall = {}
    for n in ("ffn1_norm", "mix_norm", "ffn2_norm", "ple_norm"):
        gsmall[n] = jnp.concatenate([G[f"{n}0"], G[f"{n}1"]], axis=0)
    for n in ("swa_sinks", "rwkv_mu", "rwkv_w0", "rwkv_a0", "rwkv_k_k", "rwkv_k_a", "rwkv_r_k", "rwkv_ln_w",
              "rwkv_ln_b", "final_norm"):
        gsmall[n] = G[n]
    gsmall["fox_b_f"] = G["fox_b_f"][:, :n_f]
    small = _pack_small([gsmall[n] for n in _REPLICATED] + [loss_row[:, :1]])
    small_parts = all_gather(small, "small_all_gather")
    pad1 = lambda vals: _pack_small(vals + [jnp.zeros((1, 1), F32)])
    gs, ds, nms, nvs = adamw(pad1([w[n] for n in _REPLICATED]), pad1([m[n] for n in _REPLICATED]),
                             pad1([v[n] for n in _REPLICATED]), small_parts, "adamw_replicated")
    out_g.update(zip(_REPLICATED, _unpack_small(gs, small_shapes)))
    out_d.update(zip(_REPLICATED, _unpack_small(ds, small_shapes)))
    out_m.update(zip(_REPLICATED, _unpack_small(nms, small_shapes)))
    out_v.update(zip(_REPLICATED, _unpack_small(nvs, small_shapes)))
    n_small = sum(math.prod(s) for s in small_shapes)
    loss = gs[0, n_small]

    return (loss, dx[None], *[out_g[n] for n in _WEIGHTS], *[out_d[n] for n in _WEIGHTS],
            *[out_m[n] for n in _WEIGHTS], *[out_v[n] for n in _WEIGHTS])
```

```python
import functools
import math

import numpy as np
import jax
import jax.numpy as jnp
from jax import lax
from jax.experimental import pallas as pl
from jax.experimental.pallas import tpu as pltpu

F32 = jnp.float32
BF16 = jnp.bfloat16

D_MODEL = 1024
HEAD_DIM = 64
BLOCK = 128
SWA_HEADS = 8
SWA_KV_HEADS = 2
SWA_GROUP = 4
RWKV_HEADS = 8
RWKV_DIM = 512
FOX_HEADS = 16
FOX_DIM = 1024
D_FF = 2816
NORM_EPS = 1e-6
GN_EPS = 64e-5
L2_EPS = 1e-12
SWA_Q = 512
SWA_KV = 128
SWA_COLS = 768
FOX_IN_PAD = 3200
N_DEV = 8
ADAM_LR = 0.001
ADAM_B1 = 0.9
ADAM_B2 = 0.999
ADAM_EPS = 1e-08
ADAM_WD = 0.01
ADAM_STEP = 10

V7X_VMEM_LIMIT = 56 * 1024 * 1024
SCAN_GROUP = 8
SCAN_CHUNK = 32

_NN = (((1,), (0,)), ((), ()))
_NT = (((1,), (1,)), ((), ()))
_TN = (((0,), (0,)), ((), ()))
_DIMS = {"nn": _NN, "nt": _NT, "tn": _TN}


def _pick(n, target, mult=128):
    best = None
    for t in range(mult, min(n, target) + 1, mult):
        if n % t == 0:
            best = t
    return best or n


def _cparams(sem):
    return pltpu.CompilerParams(dimension_semantics=sem, vmem_limit_bytes=V7X_VMEM_LIMIT)


def _dot(a, b, dims):
    return lax.dot_general(a.astype(BF16), b.astype(BF16), dims, preferred_element_type=F32)


@jax.custom_vjp
def bdot(a, b):
    return _dot(a, b, _NN)


def _bdot_fwd(a, b):
    return _dot(a, b, _NN), (a, b)


def _bdot_bwd(res, g):
    a, b = res
    return _dot(g, b, _NT), _dot(a, g, _TN)


bdot.defvjp(_bdot_fwd, _bdot_bwd)


@jax.custom_vjp
def bdot_nt(a, b):
    return _dot(a, b, _NT)


def _bdot_nt_fwd(a, b):
    return _dot(a, b, _NT), (a, b)


def _bdot_nt_bwd(res, g):
    a, b = res
    return _dot(g, b, _NN), _dot(g, a, _TN)


bdot_nt.defvjp(_bdot_nt_fwd, _bdot_nt_bwd)


def _segsum(x, bd):
    return jnp.dot(x, bd, precision=lax.Precision.HIGHEST, preferred_element_type=F32)


def _sigmoid(x):
    return 1.0 / (1.0 + jnp.exp(-x))


def _sigmoid_tanh(x):
    return 0.5 * jnp.tanh(0.5 * x) + 0.5


def _softplus(x):
    return jnp.maximum(x, 0.0) + jnp.log(1.0 + jnp.exp(-jnp.abs(x)))


def matmul(a, b, mode, name, out_dtype=F32, scale=1.0, res=None, tm=512, tn=1408, tk=1024):
    if mode == "nn":
        (M, K), (K2, N) = a.shape, b.shape
    elif mode == "nt":
        (M, K), (N, K2) = a.shape, b.shape
    else:
        (K, M), (K2, N) = a.shape, b.shape
    assert K == K2, (a.shape, b.shape, mode)
    tm, tn, tk = _pick(M, tm), _pick(N, tn), _pick(K, tk)
    nk = K // tk
    has_res = res is not None

    def body(*refs):
        if has_res:
            a_ref, b_ref, r_ref, o_ref, acc = refs
        else:
            a_ref, b_ref, o_ref, acc = refs
        kk = pl.program_id(2)

        @pl.when(kk == 0)
        def _():
            acc[...] = jnp.zeros_like(acc)

        acc[...] += _dot(a_ref[...], b_ref[...], _DIMS[mode])

        @pl.when(kk == nk - 1)
        def _():
            v = acc[...]
            if scale != 1.0:
                v = v * scale
            if has_res:
                v = v + r_ref[...].astype(F32)
            o_ref[...] = v.astype(out_dtype)

    if mode == "tn":
        a_spec = pl.BlockSpec((tk, tm), lambda i, j, k: (k, i))
    else:
        a_spec = pl.BlockSpec((tm, tk), lambda i, j, k: (i, k))
    if mode == "nt":
        b_spec = pl.BlockSpec((tn, tk), lambda i, j, k: (j, k))
    else:
        b_spec = pl.BlockSpec((tk, tn), lambda i, j, k: (k, j))
    o_spec = pl.BlockSpec((tm, tn), lambda i, j, k: (i, j))
    in_specs = [a_spec, b_spec] + ([o_spec] if has_res else [])
    args = (a, b) + ((res,) if has_res else ())
    return pl.pallas_call(
        body,
        grid=(M // tm, N // tn, nk),
        in_specs=in_specs,
        out_specs=o_spec,
        out_shape=jax.ShapeDtypeStruct((M, N), out_dtype),
        scratch_shapes=[pltpu.VMEM((tm, tn), F32)],
        compiler_params=_cparams(("parallel", "parallel", "arbitrary")),
        name=name,
    )(*args)


def _row_spec(r, tm):
    if isinstance(r, tuple):
        arr, width, blk = r
        return arr, pl.BlockSpec((tm, width), lambda i, blk=blk: (i, blk))
    return r, pl.BlockSpec((tm, r.shape[1]), lambda i: (i, 0))


def _whole_spec(p):
    return pl.BlockSpec(p.shape, lambda i: (0,) * p.ndim)


def rowwise(fn, rows, params, outs, name, tm=256):
    arrs, specs = zip(*[_row_spec(r, tm) for r in rows])
    S = arrs[0].shape[0]
    tm = min(tm, S)
    arrs, specs = zip(*[_row_spec(r, tm) for r in rows])
    n_in = len(rows) + len(params)

    def body(*refs):
        res = fn(*[r[...] for r in refs[:n_in]])
        for o_ref, v in zip(refs[n_in:], res):
            o_ref[...] = v.astype(o_ref.dtype)

    return pl.pallas_call(
        body,
        grid=(S // tm,),
        in_specs=list(specs) + [_whole_spec(p) for p in params],
        out_specs=[pl.BlockSpec((tm, c), lambda i: (i, 0)) for c, _ in outs],
        out_shape=[jax.ShapeDtypeStruct((S, c), dt) for c, dt in outs],
        compiler_params=_cparams(("parallel",)),
        name=name,
    )(*arrs, *params)


def rowwise_vjp(fn, rows, params, cots, name, need=None, row_dtype=F32, consts=(), tm=256):
    nr, npar, nc, nk = len(rows), len(params), len(cots), len(consts)
    need = [True] * nr if need is None else need
    arrs, _ = zip(*[_row_spec(r, tm) for r in rows])
    S = arrs[0].shape[0]
    tm = min(tm, S)
    arrs, specs = zip(*[_row_spec(r, tm) for r in rows])
    carrs, cspecs = zip(*[_row_spec(c, tm) for c in cots])
    widths = [s.block_shape[1] for s in specs]
    n_in = nr + npar + nk + nc

    def body(*refs):
        i = pl.program_id(0)
        xs = [r[...].astype(F32) for r in refs[:nr]]
        ps = [r[...] for r in refs[nr:nr + npar]]
        ks = [r[...] for r in refs[nr + npar:nr + npar + nk]]
        cs = [r[...].astype(F32) for r in refs[nr + npar + nk:n_in]]
        outs, vjp = jax.vjp(lambda *a: fn(*a, *ks), *xs, *ps)
        grads = vjp(tuple(cs))
        o = n_in
        for j in range(nr):
            if need[j]:
                refs[o][...] = grads[j].astype(refs[o].dtype)
                o += 1
        for j in range(npar):
            g_ref = refs[o + j]

            @pl.when(i == 0)
            def _(g_ref=g_ref):
                g_ref[...] = jnp.zeros_like(g_ref)

            g_ref[...] += grads[nr + j]

    out_specs = [pl.BlockSpec((tm, w), lambda i: (i, 0)) for w, nd in zip(widths, need) if nd]
    out_shape = [jax.ShapeDtypeStruct((S, w), row_dtype) for w, nd in zip(widths, need) if nd]
    out_specs += [_whole_spec(p) for p in params]
    out_shape += [jax.ShapeDtypeStruct(p.shape, F32) for p in params]
    res = pl.pallas_call(
        body,
        grid=(S // tm,),
        in_specs=list(specs) + [_whole_spec(p) for p in params] + [_whole_spec(k) for k in consts] + list(cspecs),
        out_specs=out_specs,
        out_shape=out_shape,
        compiler_params=_cparams(("arbitrary",)),
        name=name,
    )(*arrs, *params, *consts, *carrs)
    nrow = sum(need)
    return list(res[:nrow]), list(res[nrow:])


def _rms(x, g):
    return x * lax.rsqrt(jnp.mean(x * x, axis=-1, keepdims=True) + NORM_EPS) * g


def _f_rms(x, g):
    return (_rms(x, g),)


def _f_rms_res(x, g):
    return _rms(x, g), x


def _f_ple(x, z, pp):
    return (x + _sigmoid(z) * pp,)


def _f_mix(h, sh, mu):
    return (h + (sh - h) * mu,)


def _f_logf(fz, bf):
    return (-_softplus(-(fz + bf)),)


def _f_rwkv_pre(hk, hw, ha, hg, w0, w2, a0, a2, g2, k_k, k_a, bd):
    wlog = -_softplus(-(w0 + bdot(jnp.tanh(hw), w2))) - 0.5
    a = _sigmoid(a0 + bdot(ha, a2))
    g = bdot(_sigmoid(hg), g2)
    kk = hk * k_k
    kk = kk / jnp.maximum(jnp.sqrt(_segsum(kk * kk, bd)), L2_EPS)
    k2 = hk * (1.0 + (a - 1.0) * k_a)
    decay = jnp.exp(-jnp.exp(wlog))
    return decay, k2, kk, kk * a, g


def _f_rwkv_post(y, r, k2, v, g, ln_w, ln_b, r_k, bd):
    mean = _segsum(y, bd) * (1.0 / HEAD_DIM)
    d = y - mean
    var = _segsum(d * d, bd) * (1.0 / HEAD_DIM)
    yn = d * lax.rsqrt(var + GN_EPS) * ln_w + ln_b
    yn = yn + _segsum(r * k2 * r_k, bd) * v
    return (yn * g,)


def loss_head(x, target, gf, tm=256):
    S, D = x.shape
    tm = min(tm, S)

    def f(xt, g, tt):
        err = _rms(xt, g) - tt
        return 0.5 * jnp.sum(err * err) * (1.0 / D)

    def body(x_ref, t_ref, g_ref, dx_ref, dg_ref, l_ref):
        i = pl.program_id(0)
        val, (dx, dg) = jax.value_and_grad(f, argnums=(0, 1))(x_ref[...], g_ref[...], t_ref[...])

        @pl.when(i == 0)
        def _():
            dg_ref[...] = jnp.zeros_like(dg_ref)
            l_ref[...] = jnp.zeros_like(l_ref)

        dx_ref[...] = dx
        dg_ref[...] += dg
        l_ref[...] += jnp.full(l_ref.shape, val, F32)

    row = pl.BlockSpec((tm, D), lambda i: (i, 0))
    vec = pl.BlockSpec((1, D), lambda i: (0, 0))
    return pl.pallas_call(
        body,
        grid=(S // tm,),
        in_specs=[row, row, vec],
        out_specs=[row, vec, pl.BlockSpec((1, 128), lambda i: (0, 0))],
        out_shape=[jax.ShapeDtypeStruct((S, D), F32), jax.ShapeDtypeStruct((1, D), F32),
                   jax.ShapeDtypeStruct((1, 128), F32)],
        compiler_params=_cparams(("arbitrary",)),
        name="loss_head",
    )(x, target, gf)


def _swa_block(q, kp, kc, vp, vc, sink, slope, n):
    k = jnp.concatenate([kp, kc], axis=0)
    v = jnp.concatenate([vp, vc], axis=0)
    rows = q.shape[0]
    logits = bdot_nt(q, k) * (HEAD_DIM ** -0.5)
    qi = lax.broadcasted_iota(jnp.int32, (rows, 2 * BLOCK), 0) & (BLOCK - 1)
    ki = lax.broadcasted_iota(jnp.int32, (rows, 2 * BLOCK), 1)
    dist = qi + BLOCK - ki
    valid = (dist >= 0) & (dist < BLOCK) & ((n - 1) * BLOCK + ki >= 0)
    logits = logits - slope * dist.astype(F32)
    logits = jnp.where(valid, logits, -jnp.inf)
    m = jnp.maximum(jnp.max(logits, axis=-1, keepdims=True), sink)
    pr = jnp.exp(logits - m)
    denom = jnp.sum(pr, axis=-1, keepdims=True) + jnp.exp(sink - m)
    return bdot(pr / denom, v)


def _swa_specs(S):
    nb = S // BLOCK
    q_spec = pl.BlockSpec((None, SWA_GROUP, BLOCK, HEAD_DIM), lambda h, n: (h, 0, n, 0))
    kc_spec = pl.BlockSpec((None, BLOCK, HEAD_DIM), lambda h, n: (h, n, 0))
    kp_spec = pl.BlockSpec((None, BLOCK, HEAD_DIM), lambda h, n: (h, jnp.maximum(n - 1, 0), 0))
    col_spec = pl.BlockSpec((None, SWA_GROUP * BLOCK, 1), lambda h, n: (h, 0, 0))
    return nb, q_spec, kp_spec, kc_spec, col_spec


def swa_fwd(q, k, v, sink_col, slope_col):
    S = q.shape[2]
    nb, q_spec, kp_spec, kc_spec, col_spec = _swa_specs(S)

    def body(q_ref, kp_ref, kc_ref, vp_ref, vc_ref, s_ref, a_ref, o_ref):
        n = pl.program_id(1)
        qq = q_ref[...].reshape(SWA_GROUP * BLOCK, HEAD_DIM)
        out = _swa_block(qq, kp_ref[...], kc_ref[...], vp_ref[...], vc_ref[...], s_ref[...], a_ref[...], n)
        o_ref[...] = out.reshape(SWA_GROUP, BLOCK, HEAD_DIM)

    return pl.pallas_call(
        body,
        grid=(SWA_KV_HEADS, nb),
        in_specs=[q_spec, kp_spec, kc_spec, kp_spec, kc_spec, col_spec, col_spec],
        out_specs=q_spec,
        out_shape=jax.ShapeDtypeStruct(q.shape, F32),
        compiler_params=_cparams(("parallel", "parallel")),
        name="swa_fwd",
    )(q, k, k, v, v, sink_col, slope_col)


def swa_bwd(q, k, v, sink_col, slope_col, dout):
    S = q.shape[2]
    nb, q_spec, kp_spec, kc_spec, col_spec = _swa_specs(S)

    def body(q_ref, kp_ref, kc_ref, vp_ref, vc_ref, s_ref, a_ref, do_ref,
             dq_ref, dkp_ref, dkc_ref, dvp_ref, dvc_ref, ds_ref):
        n = pl.program_id(1)
        qq = q_ref[...].reshape(SWA_GROUP * BLOCK, HEAD_DIM)
        slope = a_ref[...]
        f = lambda a, b, c, d, e, s: _swa_block(a, b, c, d, e, s, slope, n)
        _, vjp = jax.vjp(f, qq, kp_ref[...], kc_ref[...], vp_ref[...], vc_ref[...], s_ref[...])
        dq, dkp, dkc, dvp, dvc, ds = vjp(do_ref[...].reshape(SWA_GROUP * BLOCK, HEAD_DIM))
        dq_ref[...] = dq.reshape(SWA_GROUP, BLOCK, HEAD_DIM)
        dkp_ref[...] = dkp
        dkc_ref[...] = dkc
        dvp_ref[...] = dvp
        dvc_ref[...] = dvc

        @pl.when(n == 0)
        def _():
            ds_ref[...] = jnp.zeros_like(ds_ref)

        ds_ref[...] += ds

    kv_shape = jax.ShapeDtypeStruct(k.shape, F32)
    return pl.pallas_call(
        body,
        grid=(SWA_KV_HEADS, nb),
        in_specs=[q_spec, kp_spec, kc_spec, kp_spec, kc_spec, col_spec, col_spec, q_spec],
        out_specs=[q_spec, kc_spec, kc_spec, kc_spec, kc_spec, col_spec],
        out_shape=[jax.ShapeDtypeStruct(q.shape, F32), kv_shape, kv_shape, kv_shape, kv_shape,
                   jax.ShapeDtypeStruct(sink_col.shape, F32)],
        compiler_params=_cparams(("parallel", "arbitrary")),
        name="swa_bwd",
    )(q, k, k, v, v, sink_col, slope_col, dout)


def _split2(x):
    hi = x.astype(BF16)
    return (x - hi.astype(F32)).astype(BF16), hi


def _dot2_many(xs, m):
    rows = xs[0].shape[0]
    res = jnp.dot(jnp.concatenate([p for x in xs for p in _split2(x)], axis=0), m, preferred_element_type=F32)
    return [res[(2 * i) * rows:(2 * i + 1) * rows] + res[(2 * i + 1) * rows:(2 * i + 2) * rows]
            for i in range(len(xs))]


def _dot2(x, m):
    return _dot2_many([x], m)[0]


def _seg_sums(xs, bd):
    w = bd.shape[0]
    halves = _dot2_many([x[:, i:i + w] for x in xs for i in range(0, x.shape[1], w)], bd)
    n = xs[0].shape[1] // w
    return [jnp.concatenate(halves[i * n:(i + 1) * n], axis=1) for i in range(len(xs))]


def _seg_sum(x, bd):
    return _seg_sums([x], bd)[0]


def _scan_consts():
    r = np.arange(256)
    bd = (r[:, None] // HEAD_DIM == r[None, :] // HEAD_DIM).astype(np.float32)
    c = np.arange(RWKV_DIM)
    e = (np.arange(HEAD_DIM)[:, None] // SCAN_GROUP == c[None, :] // HEAD_DIM).astype(np.float32)
    diag = (np.arange(HEAD_DIM)[:, None] == c[None, :] % HEAD_DIM).astype(np.float32)
    return jnp.asarray(bd, BF16), jnp.asarray(e, BF16), jnp.asarray(diag, F32)


def _to_colblocks(a):
    S = a.shape[0]
    a = a.reshape(S // SCAN_GROUP, SCAN_GROUP, RWKV_HEADS, HEAD_DIM)
    return a.transpose(0, 3, 2, 1).reshape(S // SCAN_GROUP, HEAD_DIM, RWKV_HEADS * SCAN_GROUP)


def _roll_up(rows):
    return pltpu.roll(rows, rows.shape[0] - 1, 0)


def _scan_pair_rows(aux, base, kk_ref, w_ref, b_ref, k_ref, bd):
    G = SCAN_GROUP
    kk_nx = _roll_up(kk_ref[pl.ds(base, G), :])
    aux[0] = w_ref[pl.ds(base, G), :] * kk_nx
    aux[1], aux[2] = _seg_sums([b_ref[pl.ds(base, G), :] * kk_nx, k_ref[pl.ds(base, G), :] * kk_nx], bd)


def _scan_pair(St, t0, base, col_g, lane_t, aux, kk_ref, w_ref, b_ref, k_ref, bd, e):
    t1 = t0 + 1
    row = lambda ref, t: ref[pl.ds(base + t, 1), :]
    arow = lambda i: aux[i, pl.ds(t0, 1), :]
    u0, m1 = _seg_sums([St * row(kk_ref, t0), St * arow(0)], bd)
    v0, v1 = _dot2_many([jnp.where(lane_t == t0, col_g, 0.0), jnp.where(lane_t == t1, col_g, 0.0)], e)
    u1 = m1 - u0 * arow(1) + v0 * arow(2)
    S0 = St * row(w_ref, t0) - u0 * row(b_ref, t0) + v0 * row(k_ref, t0)
    S1 = S0 * row(w_ref, t1) - u1 * row(b_ref, t1) + v1 * row(k_ref, t1)
    return (S0, S1), (u0, u1), (v0, v1)


def rwkv_scan_fwd(r, w, k, kk, b, vB, gather_src):
    S, C = r.shape
    N, G = HEAD_DIM, SCAN_GROUP
    chunk = min(SCAN_CHUNK, S)
    nchunk, ng = S // chunk, chunk // G
    bd, e, diag = _scan_consts()

    def body(r_ref, w_ref, k_ref, kk_ref, b_ref, vB_ref, bd_ref, e_ref, dg_ref, xs_ref, y_ref, ck_ref, xr_ref,
             S_ref, aux, send_sems, recv_sems, local_sem):
        c = pl.program_id(0)
        _exchange_during(c, nchunk, True, xs_ref, xr_ref, send_sems, recv_sems, local_sem)

        @pl.when(c == 0)
        def _():
            S_ref[...] = jnp.zeros_like(S_ref)

        ck_ref[...] = S_ref[...]
        sub = lax.broadcasted_iota(jnp.int32, (G, C), 0)
        lane_t = lax.broadcasted_iota(jnp.int32, (N, N), 1) & (G - 1)

        def group(g, St):
            base = pl.multiple_of(g * G, G)
            vb = vB_ref[g]
            _scan_pair_rows(aux, base, kk_ref, w_ref, b_ref, k_ref, bd_ref[...])
            ys = jnp.zeros((G, C), F32)
            def emit(ys, states, t0):
                steps = (t0, t0 + 1)
                y_bs = _seg_sums([S_t * r_ref[pl.ds(base + tt, 1), :] for S_t, tt in zip(states, steps)], bd_ref[...])
                for y_b, tt in zip(y_bs, steps):
                    ys = jnp.where(sub == tt, jnp.sum(y_b * dg_ref[...], axis=0, keepdims=True), ys)
                return ys

            pending = None
            for t0 in range(0, G, 2):
                states, _, _ = _scan_pair(St, t0, base, vb, lane_t, aux, kk_ref, w_ref, b_ref, k_ref, bd_ref[...],
                                          e_ref[...])
                if pending is not None:
                    ys = emit(ys, *pending)
                pending = (states, t0)
                St = states[1]
            y_ref[pl.ds(base, G), :] = emit(ys, *pending)
            return St

        S_ref[...] = lax.fori_loop(0, ng, group, S_ref[...])

    row = pl.BlockSpec((chunk, C), lambda c: (c, 0))
    col = pl.BlockSpec((ng, N, N), lambda c: (c, 0, 0))
    return pl.pallas_call(
        body,
        grid=(nchunk,),
        in_specs=[row] * 5 + [col, _whole_spec(bd), _whole_spec(e), _whole_spec(diag), _ANY],
        out_specs=[row, pl.BlockSpec((None, N, C), lambda c: (c, 0, 0)), _ANY],
        out_shape=[jax.ShapeDtypeStruct((S, C), F32), jax.ShapeDtypeStruct((nchunk, N, C), F32),
                   _exchange_out_shape(True, gather_src)],
        scratch_shapes=[pltpu.VMEM((N, C), F32), pltpu.VMEM((3, G, C), F32)] + _EXCHANGE_SEMS,
        compiler_params=_cparams(("arbitrary",)),
        name="rwkv_scan_fwd",
    )(r, w, k, kk, b, vB, bd, e, diag, gather_src)


def rwkv_scan_bwd(r, w, k, kk, b, vB, dyB, ckpt, scatter_src):
    S, C = r.shape
    N, G = HEAD_DIM, SCAN_GROUP
    chunk = min(SCAN_CHUNK, S)
    nchunk, ng = S // chunk, chunk // G
    bd, e, diag = _scan_consts()

    def body(r_ref, w_ref, k_ref, kk_ref, b_ref, vB_ref, dyB_ref, ck_ref, bd_ref, e_ref, dg_ref, xs_ref,
             dr_ref, dw_ref, dk_ref, dkk_ref, db_ref, dv_ref, xr_ref, G_ref, sbuf, ubuf, vbuf, aux,
             send_sems, recv_sems, local_sem):
        c = pl.program_id(0)
        _exchange_during(c, nchunk, False, xs_ref, xr_ref, send_sems, recv_sems, local_sem)

        @pl.when(c == 0)
        def _():
            G_ref[...] = jnp.zeros_like(G_ref)

        lane_t = lax.broadcasted_iota(jnp.int32, (N, N), 1) & (G - 1)
        sub = lax.broadcasted_iota(jnp.int32, (G, C), 0)

        def fgroup(g, St):
            base = pl.multiple_of(g * G, G)
            vb = vB_ref[g]
            _scan_pair_rows(aux, base, kk_ref, w_ref, b_ref, k_ref, bd_ref[...])
            for t0 in range(0, G, 2):
                states, us, vs = _scan_pair(St, t0, base, vb, lane_t, aux, kk_ref, w_ref, b_ref, k_ref, bd_ref[...],
                                            e_ref[...])
                for i, S_before in enumerate((St, states[0])):
                    sbuf[base + t0 + i] = S_before
                    ubuf[base + t0 + i] = us[i]
                    vbuf[base + t0 + i] = vs[i]
                St = states[1]
            return St

        sbuf[chunk] = lax.fori_loop(0, ng, fgroup, ck_ref[...])

        def bgroup(gi, Gt):
            g = ng - 1 - gi
            base = pl.multiple_of(g * G, G)
            dyb = dyB_ref[g]
            rows = [jnp.zeros((G, C), F32) for _ in range(6)]
            colsum = lambda a: jnp.sum(a, axis=0, keepdims=True)
            row = lambda ref, t: ref[pl.ds(base + t, 1), :]
            b8 = b_ref[pl.ds(base, G), :]
            aux[0] = _roll_up(w_ref[pl.ds(base, G), :]) * b8
            aux[1], aux[2] = _seg_sums([_roll_up(kk_ref[pl.ds(base, G), :]) * b8, r_ref[pl.ds(base, G), :] * b8],
                                       bd_ref[...])

            def emit(rows, steps):
                d_vs = _seg_sums([Gt_ * row(k_ref, tt) for tt, Gt_, _, _ in steps], bd_ref[...])
                for (tt, Gt_, du_b, dy_b), d_vb in zip(steps, d_vs):
                    Sp, Sc = sbuf[base + tt], sbuf[base + tt + 1]
                    new = (colsum(Sc * dy_b), colsum(Gt_ * Sp), colsum(Gt_ * vbuf[base + tt]), colsum(Sp * du_b),
                           -colsum(Gt_ * ubuf[base + tt]), colsum(d_vb * dg_ref[...]))
                    rows = [jnp.where(sub == tt, n_, acc) for n_, acc in zip(new, rows)]
                return rows

            pending = None
            for t0 in reversed(range(0, G, 2)):
                t1 = t0 + 1
                arow = lambda i: aux[i, pl.ds(t0, 1), :]
                dy1, dy0 = _dot2_many([jnp.where(lane_t == t1, dyb, 0.0), jnp.where(lane_t == t0, dyb, 0.0)],
                                      e_ref[...])
                G1 = Gt + dy1 * row(r_ref, t1)
                m1, m2 = _seg_sums([G1 * row(b_ref, t1), G1 * arow(0)], bd_ref[...])
                du1 = -m1
                du0 = -(m2 + du1 * arow(1) + dy0 * arow(2))
                G0 = G1 * row(w_ref, t1) + du1 * row(kk_ref, t1) + dy0 * row(r_ref, t0)
                G_next = G0 * row(w_ref, t0) + du0 * row(kk_ref, t0)
                if pending is not None:
                    rows = emit(rows, pending)
                pending = ((t1, G1, du1, dy1), (t0, G0, du0, dy0))
                Gt = G_next
            rows = emit(rows, pending)
            for ref, val in zip((dr_ref, dw_ref, dk_ref, dkk_ref, db_ref, dv_ref), rows):
                ref[pl.ds(base, G), :] = val
            return Gt

        G_ref[...] = lax.fori_loop(0, ng, bgroup, G_ref[...])

    rev = lambda c: nchunk - 1 - c
    row = pl.BlockSpec((chunk, C), lambda c: (rev(c), 0))
    col = pl.BlockSpec((ng, N, N), lambda c: (rev(c), 0, 0))
    rshape = jax.ShapeDtypeStruct((S, C), F32)
    return pl.pallas_call(
        body,
        grid=(nchunk,),
        in_specs=[row] * 5 + [col, col, pl.BlockSpec((None, N, C), lambda c: (rev(c), 0, 0)),
                              _whole_spec(bd), _whole_spec(e), _whole_spec(diag), _ANY],
        out_specs=[row] * 6 + [_ANY],
        out_shape=[rshape] * 6 + [_exchange_out_shape(False, scatter_src)],
        scratch_shapes=[pltpu.VMEM((N, C), F32), pltpu.VMEM((chunk + 1, N, C), F32),
                        pltpu.VMEM((chunk, N, C), F32), pltpu.VMEM((chunk, N, C), F32),
                        pltpu.VMEM((3, G, C), F32)] + _EXCHANGE_SEMS,
        compiler_params=_cparams(("arbitrary",)),
        name="rwkv_scan_bwd",
    )(r, w, k, kk, b, vB, dyB, ckpt, bd, e, diag, scatter_src)


def seq_cumsum(x, reverse, name):
    S, C = x.shape
    tb = min(256, S)
    nb = S // tb

    def body(x_ref, o_ref, carry):
        i = pl.program_id(0)

        @pl.when(i == 0)
        def _():
            carry[...] = jnp.zeros_like(carry)

        ri = lax.broadcasted_iota(jnp.int32, (tb, tb), 0)
        ci = lax.broadcasted_iota(jnp.int32, (tb, tb), 1)
        tri = jnp.where((ci >= ri) if reverse else (ci <= ri), 1.0, 0.0).astype(F32)
        xb = x_ref[...]
        out = jnp.dot(tri, xb, precision=lax.Precision.HIGHEST, preferred_element_type=F32) + carry[...]
        o_ref[...] = out
        carry[...] = carry[...] + jnp.sum(xb, axis=0, keepdims=True)

    idx = (lambda i: (nb - 1 - i, 0)) if reverse else (lambda i: (i, 0))
    return pl.pallas_call(
        body,
        grid=(nb,),
        in_specs=[pl.BlockSpec((tb, C), idx)],
        out_specs=pl.BlockSpec((tb, C), idx),
        out_shape=jax.ShapeDtypeStruct((S, C), F32),
        scratch_shapes=[pltpu.VMEM((1, C), F32)],
        compiler_params=_cparams(("arbitrary",)),
        name=name,
    )(x)


FOX_STRIP = 512


def _fox_logits(q, k, cq, ck, row0, col0):
    s = _dot(q, k, _NT) * (HEAD_DIM ** -0.5) + cq - ck
    row = row0 + lax.broadcasted_iota(jnp.int32, s.shape, 0)
    col = col0 + lax.broadcasted_iota(jnp.int32, s.shape, 1)
    return jnp.where(col <= row, s, -jnp.inf)


def _fox_strips(tq):
    st = min(FOX_STRIP, tq)
    return [(r * st, slice(r * st, (r + 1) * st)) for r in range(tq // st)]


def _fox_tiles(n, by_query):
    pairs = [(i, j) for i in range(n) for j in range(i + 1)] if by_query else \
            [(i, j) for j in range(n) for i in range(j, n)]
    return (jnp.asarray(np.array([p[0] for p in pairs], np.int32)),
            jnp.asarray(np.array([p[1] for p in pairs], np.int32)))


def _fox_specs(t, Dh):
    qs = pl.BlockSpec((None, t, Dh), lambda h, s, qt, kt: (h, qt[s], 0))
    ks = pl.BlockSpec((None, t, Dh), lambda h, s, qt, kt: (h, kt[s], 0))
    cqs = pl.BlockSpec((None, t, 1), lambda h, s, qt, kt: (h, qt[s], 0))
    cks = pl.BlockSpec((None, 1, t), lambda h, s, qt, kt: (h, 0, kt[s]))
    return qs, ks, cqs, cks


def _fox_call(body, tiles, Hh, in_specs, out_specs, out_shape, scratch, name, args):
    spec = pltpu.PrefetchScalarGridSpec(num_scalar_prefetch=2, grid=(Hh, tiles[0].shape[0]), in_specs=in_specs,
                                        out_specs=out_specs, scratch_shapes=scratch)
    return pl.pallas_call(body, grid_spec=spec, out_shape=out_shape,
                          compiler_params=_cparams(("parallel", "arbitrary")), name=name)(*tiles, *args)


def fox_fwd(q, k, v, c_col, c_row):
    Hh, S, Dh = q.shape
    tq = tk = min(512, S)

    def body(qt_ref, kt_ref, q_ref, k_ref, v_ref, cq_ref, ck_ref, o_ref, lse_ref, m_s, l_s, acc_s):
        qi, ki = qt_ref[pl.program_id(1)], kt_ref[pl.program_id(1)]

        @pl.when(ki == 0)
        def _():
            m_s[...] = jnp.full_like(m_s, -jnp.inf)
            l_s[...] = jnp.zeros_like(l_s)
            acc_s[...] = jnp.zeros_like(acc_s)

        kb, vb, ck = k_ref[...], v_ref[...], ck_ref[...]
        for r0, rs in _fox_strips(tq):
            s = _fox_logits(q_ref[rs, :], kb, cq_ref[rs, :], ck, qi * tq + r0, ki * tk)
            m_old = m_s[rs, :]
            m_new = jnp.maximum(m_old, jnp.max(s, axis=-1, keepdims=True))
            alpha = jnp.exp(m_old - m_new)
            p = jnp.exp(s - m_new)
            l_s[rs, :] = alpha * l_s[rs, :] + jnp.sum(p, axis=-1, keepdims=True)
            acc_s[rs, :] = alpha * acc_s[rs, :] + _dot(p, vb, _NN)
            m_s[rs, :] = m_new

        @pl.when(ki == qi)
        def _():
            o_ref[...] = acc_s[...] / l_s[...]
            lse_ref[...] = m_s[...] + jnp.log(l_s[...])

    qs, ks, cqs, cks = _fox_specs(tq, Dh)
    return _fox_call(
        body, _fox_tiles(S // tq, True), Hh, [qs, ks, ks, cqs, cks], [qs, cqs],
        [jax.ShapeDtypeStruct((Hh, S, Dh), F32), jax.ShapeDtypeStruct((Hh, S, 1), F32)],
        [pltpu.VMEM((tq, 1), F32), pltpu.VMEM((tq, 1), F32), pltpu.VMEM((tq, Dh), F32)],
        "fox_fwd", (q, k, v, c_col, c_row))


def fox_bwd(q, k, v, c_col, c_row, o, lse, do):
    Hh, S, Dh = q.shape
    tq = tk = min(512, S)
    nk = S // tk

    def body(qt_ref, kt_ref, q_ref, k_ref, v_ref, cq_ref, ck_ref, o_ref, lse_ref, do_ref,
             dq_ref, dr_ref, dk_ref, dv_ref, dc_ref, acc_s, row_s):
        step = pl.program_id(1)
        qi, ki = qt_ref[step], kt_ref[step]

        @pl.when(step == 0)
        def _():
            dk_ref[...] = jnp.zeros_like(dk_ref)
            dv_ref[...] = jnp.zeros_like(dv_ref)
            dc_ref[...] = jnp.zeros_like(dc_ref)

        @pl.when(ki == 0)
        def _():
            acc_s[...] = jnp.zeros_like(acc_s)
            row_s[...] = jnp.zeros_like(row_s)

        q_t, kb, vb, do_t = q_ref[...], k_ref[...], v_ref[...], do_ref[...]
        s = _fox_logits(q_t, kb, cq_ref[...], ck_ref[...], qi * tq, ki * tk)
        p = jnp.exp(s - lse_ref[...])
        delta = jnp.sum(do_t * o_ref[...], axis=-1, keepdims=True)
        ds = p * (_dot(do_t, vb, _NT) - delta)
        acc_s[...] += _dot(ds, kb, _NN)
        row_s[...] += jnp.sum(ds, axis=-1, keepdims=True)
        dk_ref[ki] += _dot(ds, q_t, _TN) * (HEAD_DIM ** -0.5)
        dv_ref[ki] += _dot(p, do_t, _TN)
        dc_ref[ki] += jnp.sum(ds, axis=0, keepdims=True)

        @pl.when(ki == qi)
        def _():
            dq_ref[...] = acc_s[...] * (HEAD_DIM ** -0.5)
            dr_ref[...] = row_s[...]

    qs, ks, cqs, cks = _fox_specs(tq, Dh)
    head = lambda *blk: pl.BlockSpec((None,) + blk, lambda h, s, qt, kt: (h,) + (0,) * len(blk))
    dq, dr, dk, dv, dc = _fox_call(
        body, _fox_tiles(S // tq, True), Hh, [qs, ks, ks, cqs, cks, qs, cqs, qs],
        [qs, cqs, head(nk, tk, Dh), head(nk, tk, Dh), head(nk, 1, tk)],
        [jax.ShapeDtypeStruct((Hh, S, Dh), F32), jax.ShapeDtypeStruct((Hh, S, 1), F32),
         jax.ShapeDtypeStruct((Hh, nk, tk, Dh), F32), jax.ShapeDtypeStruct((Hh, nk, tk, Dh), F32),
         jax.ShapeDtypeStruct((Hh, nk, 1, tk), F32)],
        [pltpu.VMEM((tq, Dh), F32), pltpu.VMEM((tq, 1), F32)],
        "fox_bwd", (q, k, v, c_col, c_row, o, lse, do))
    return dq, dr, dk.reshape(Hh, S, Dh), dv.reshape(Hh, S, Dh), dc.reshape(Hh, 1, S)


def _heads(a, nh):
    S = a.shape[0]
    return a.reshape(S, nh, HEAD_DIM).transpose(1, 0, 2)


def _unheads(a):
    nh, S, _ = a.shape
    return a.transpose(1, 0, 2).reshape(S, nh * HEAD_DIM)


def _shift_down(a):
    return jnp.pad(a[:-1], ((1, 0), (0, 0)))


def _shift_up(a):
    return jnp.pad(a[1:], ((0, 1), (0, 0)))


def _block_diag_ones():
    i = np.arange(RWKV_DIM) // HEAD_DIM
    return jnp.asarray((i[:, None] == i[None, :]).astype(np.float32))


FFN_ROWS = 1024
FFN_COLS = 256


def _ffn_specs(S, F, tm, fc):
    nf = F // fc
    row = pl.BlockSpec((tm, D_MODEL), lambda i, j: (i, 0))
    vec = pl.BlockSpec((1, D_MODEL), lambda i, j: (0, 0))
    wg = pl.BlockSpec((D_MODEL, fc), lambda i, j: (0, j))
    wu = pl.BlockSpec((D_MODEL, fc), lambda i, j: (0, nf + j))
    wd = pl.BlockSpec((fc, D_MODEL), lambda i, j: (j, 0))
    hid = pl.BlockSpec((tm, fc), lambda i, j: (i, j))
    return nf, row, vec, wg, wu, wd, hid


def ffn_fwd(x, g_norm, w_gu, w_down, tag):
    S, F = x.shape[0], w_down.shape[0]
    tm, fc = min(FFN_ROWS, S), FFN_COLS
    nf, row, vec, wg, wu, wd, _ = _ffn_specs(S, F, tm, fc)

    def body(x_ref, g_ref, wg_ref, wu_ref, wd_ref, o_ref, hn_ref, hn_s, acc):
        j = pl.program_id(1)

        @pl.when(j == 0)
        def _():
            hn_s[...] = _rms(x_ref[...], g_ref[...]).astype(BF16)
            hn_ref[...] = hn_s[...]
            acc[...] = jnp.zeros_like(acc)

        g = _dot(hn_s[...], wg_ref[...], _NN)
        u = _dot(hn_s[...], wu_ref[...], _NN)
        acc[...] += _dot(g * _sigmoid_tanh(g) * u, wd_ref[...], _NN)

        @pl.when(j == nf - 1)
        def _():
            o_ref[...] = x_ref[...] + 0.5 * acc[...]

    out, hn = pl.pallas_call(
        body,
        grid=(S // tm, nf),
        in_specs=[row, vec, wg, wu, wd],
        out_specs=[row, row],
        out_shape=[jax.ShapeDtypeStruct((S, D_MODEL), F32), jax.ShapeDtypeStruct((S, D_MODEL), BF16)],
        scratch_shapes=[pltpu.VMEM((tm, D_MODEL), BF16), pltpu.VMEM((tm, D_MODEL), F32)],
        compiler_params=_cparams(("parallel", "arbitrary")),
        name=tag + "_fwd",
    )(x, g_norm, w_gu, w_gu, w_down)
    return out, (x, hn)


def ffn_bwd(dy, saved, g_norm, w_gu, w_down, tag):
    x, hn = saved
    S, F = x.shape[0], w_down.shape[0]
    tm, fc = min(FFN_ROWS, S), FFN_COLS
    nf, row, vec, wg, wu, wd, hid = _ffn_specs(S, F, tm, fc)

    def body(dy_ref, x_ref, hn_ref, g_ref, wg_ref, wu_ref, wd_ref, dx_ref, dgn_ref, a_ref, dg_ref, du_ref,
             dyh_s, dhn):
        i, j = pl.program_id(0), pl.program_id(1)

        @pl.when(j == 0)
        def _():
            dyh_s[...] = (0.5 * dy_ref[...]).astype(BF16)
            dhn[...] = jnp.zeros_like(dhn)

        hn_t = hn_ref[...]
        g = _dot(hn_t, wg_ref[...], _NN)
        u = _dot(hn_t, wu_ref[...], _NN)
        da = _dot(dyh_s[...], wd_ref[...], _NT)
        sig = _sigmoid_tanh(g)
        gs = g * sig
        a_ref[...] = (gs * u).astype(BF16)
        dg = ((da * u) * (sig + gs * (1.0 - sig))).astype(BF16)
        du = (da * gs).astype(BF16)
        dg_ref[...] = dg
        du_ref[...] = du
        dhn[...] += _dot(jnp.concatenate([dg, du], axis=1),
                         jnp.concatenate([wg_ref[...], wu_ref[...]], axis=1), _NT)

        @pl.when(j == nf - 1)
        def _():
            _, vjp_n = jax.vjp(_rms, x_ref[...], g_ref[...])
            dx, dgn = vjp_n(dhn[...])
            dx_ref[...] = dy_ref[...] + dx

            @pl.when(i == 0)
            def _():
                dgn_ref[...] = jnp.zeros_like(dgn_ref)

            dgn_ref[...] += dgn

    hshape = jax.ShapeDtypeStruct((S, F), BF16)
    dx, dgn, act, dg, du = pl.pallas_call(
        body,
        grid=(S // tm, nf),
        in_specs=[row, row, row, vec, wg, wu, wd],
        out_specs=[row, vec, hid, hid, hid],
        out_shape=[jax.ShapeDtypeStruct((S, D_MODEL), F32), jax.ShapeDtypeStruct((1, D_MODEL), F32),
                   hshape, hshape, hshape],
        scratch_shapes=[pltpu.VMEM((tm, D_MODEL), BF16), pltpu.VMEM((tm, D_MODEL), F32)],
        compiler_params=_cparams(("arbitrary", "arbitrary")),
        name=tag + "_bwd",
    )(dy, x, hn, g_norm, w_gu, w_gu, w_down)
    d_wdown = matmul(act, dy, "tn", tag + "_dwd", out_dtype=BF16, scale=0.5)
    d_wgu = jnp.concatenate([matmul(hn, dg, "tn", tag + "_dwg", out_dtype=BF16),
                             matmul(hn, du, "tn", tag + "_dwu", out_dtype=BF16)], axis=1)
    return dx, dgn, d_wgu, d_wdown


def ple_fwd(x, p_i, g_norm, w_gate, w_proj, tag):
    hn, = rowwise(_f_rms, [x], [g_norm], [(D_MODEL, BF16)], tag + "_rms")
    z = matmul(hn, w_gate, "nn", tag + "_gate")
    pp = matmul(p_i, w_proj, "nn", tag + "_proj")
    out, = rowwise(_f_ple, [x, z, pp], [], [(D_MODEL, F32)], tag + "_mix")
    return out, (x, hn, z, pp)


def ple_bwd(dy, saved, p_i, g_norm, w_gate, tag):
    x, hn, z, pp = saved
    (dz, dpp), _ = rowwise_vjp(_f_ple, [x, z, pp], [], [dy], tag + "_dmix", need=[False, True, True],
                               row_dtype=BF16)
    d_wproj = matmul(p_i, dpp, "tn", tag + "_dwp", out_dtype=BF16)
    d_wgate = matmul(hn, dz, "tn", tag + "_dwg", out_dtype=BF16)
    dhn = matmul(dz, w_gate, "nt", tag + "_dhn")
    (dx,), (dgn,) = rowwise_vjp(_f_rms_res, [x], [g_norm], [dhn, dy], tag + "_drms")
    return dx, dgn, d_wgate, d_wproj


def _swa_consts(sinks):
    slopes = np.asarray([2.0 ** (-(i + 1)) for i in range(SWA_HEADS)], np.float32)
    slope_col = jnp.asarray(np.repeat(slopes, BLOCK).reshape(SWA_KV_HEADS, SWA_GROUP * BLOCK, 1))
    sink_col = jnp.repeat(sinks.reshape(SWA_HEADS), BLOCK).reshape(SWA_KV_HEADS, SWA_GROUP * BLOCK, 1)
    return sink_col, slope_col


def even_mix_fwd(x, W, gather_src):
    S = x.shape[0]
    hn, = rowwise(_f_rms, [x], [W["mix_norm0"]], [(D_MODEL, BF16)], "emix_rms")
    proj = matmul(hn, W["even_w_in"], "nn", "emix_in")
    qa = _heads(proj[:, :SWA_Q], SWA_HEADS).reshape(SWA_KV_HEADS, SWA_GROUP, S, HEAD_DIM)
    ka = _heads(proj[:, SWA_Q:SWA_Q + SWA_KV], SWA_KV_HEADS)
    va = _heads(proj[:, SWA_Q + SWA_KV:SWA_COLS], SWA_KV_HEADS)
    sink_col, slope_col = _swa_consts(W["swa_sinks"])
    ya = swa_fwd(qa, ka, va, sink_col, slope_col)
    ya = _unheads(ya.reshape(SWA_HEADS, S, HEAD_DIM))
    hb = proj[:, SWA_COLS:]
    h, = rowwise(_f_mix, [hb, _shift_down(hb)], [W["rwkv_mu"]], [(hb.shape[1], F32)], "rwkv_shift")
    hr, hk, hv = h[:, :512], h[:, 512:1024], h[:, 1024:1536]
    hw, ha, hg = h[:, 1536:1600], h[:, 1600:1664], h[:, 1664:1792]
    bd = _block_diag_ones()
    pre_params = [W["rwkv_w0"], W["rwkv_w2"], W["rwkv_a0"], W["rwkv_a2"], W["rwkv_g2"], W["rwkv_k_k"],
                  W["rwkv_k_a"]]
    decay, k2, kk, b, g = rowwise(_f_rwkv_pre, [hk, hw, ha, hg], pre_params + [bd],
                                  [(RWKV_DIM, F32)] * 5, "rwkv_pre")
    vT = _to_colblocks(hv)
    y, ckpt, gathered = rwkv_scan_fwd(hr, decay, k2, kk, b, vT, gather_src)
    post_params = [W["rwkv_ln_w"], W["rwkv_ln_b"], W["rwkv_r_k"]]
    yb, = rowwise(_f_rwkv_post, [y, hr, k2, hv, g], post_params + [bd], [(RWKV_DIM, F32)], "rwkv_post")
    cat = jnp.concatenate([ya, yb], axis=1).astype(BF16)
    out = matmul(cat, W["even_w_out"], "nn", "emix_out", res=x)
    saved = (x, hn, qa, ka, va, sink_col, slope_col, hb, hr, hk, hv, hw, ha, hg, decay, k2, kk, b, g, vT,
             ckpt, y, cat)
    return out, saved, gathered


def even_mix_bwd(dy, saved, W, scatter_src):
    (x, hn, qa, ka, va, sink_col, slope_col, hb, hr, hk, hv, hw, ha, hg, decay, k2, kk, b, g, vT, ckpt, y,
     cat) = saved
    S = x.shape[0]
    grads = {}
    dcat = matmul(dy, W["even_w_out"], "nt", "emix_dcat")
    grads["even_w_out"] = matmul(cat, dy, "tn", "emix_dwout", out_dtype=BF16)
    dya, dyb = dcat[:, :SWA_Q], dcat[:, SWA_Q:]
    dya_h = _heads(dya, SWA_HEADS).reshape(SWA_KV_HEADS, SWA_GROUP, S, HEAD_DIM)
    dqa, dkp, dkc, dvp, dvc, dsink = swa_bwd(qa, ka, va, sink_col, slope_col, dya_h)
    shift_blk = lambda a: jnp.pad(a[:, BLOCK:], ((0, 0), (0, BLOCK), (0, 0)))
    dka = dkc + shift_blk(dkp)
    dva = dvc + shift_blk(dvp)
    grads["swa_sinks"] = dsink.reshape(SWA_HEADS, BLOCK).sum(axis=1).reshape(1, SWA_HEADS)
    dqa = _unheads(dqa.reshape(SWA_HEADS, S, HEAD_DIM))
    dka, dva = _unheads(dka), _unheads(dva)
    bd = _block_diag_ones()
    post_params = [W["rwkv_ln_w"], W["rwkv_ln_b"], W["rwkv_r_k"]]
    (d_y, d_r1, d_k2a, d_v1, d_g), (d_lnw, d_lnb, d_rk) = rowwise_vjp(
        _f_rwkv_post, [y, hr, k2, hv, g], post_params, [dyb], "rwkv_dpost", consts=[bd], tm=128)
    grads["rwkv_ln_w"], grads["rwkv_ln_b"], grads["rwkv_r_k"] = d_lnw, d_lnb, d_rk
    d_r2, d_w, d_k2b, d_kk, d_b, d_v2, exchanged = rwkv_scan_bwd(hr, decay, k2, kk, b, vT, _to_colblocks(d_y), ckpt,
                                                                  scatter_src)
    pre_params = [W["rwkv_w0"], W["rwkv_w2"], W["rwkv_a0"], W["rwkv_a2"], W["rwkv_g2"], W["rwkv_k_k"],
                  W["rwkv_k_a"]]
    (d_hk, d_hw, d_ha, d_hg), dpre = rowwise_vjp(
        _f_rwkv_pre, [hk, hw, ha, hg], pre_params, [d_w, d_k2a + d_k2b, d_kk, d_b, d_g], "rwkv_dpre",
        consts=[bd], tm=128)
    for nm, gval in zip(["rwkv_w0", "rwkv_w2", "rwkv_a0", "rwkv_a2", "rwkv_g2", "rwkv_k_k", "rwkv_k_a"], dpre):
        grads[nm] = gval
    d_h = jnp.concatenate([d_r1 + d_r2, d_hk, d_v1 + d_v2, d_hw, d_ha, d_hg], axis=1)
    (d_hb, d_sh), (d_mu,) = rowwise_vjp(_f_mix, [hb, _shift_down(hb)], [W["rwkv_mu"]], [d_h], "rwkv_dshift")
    grads["rwkv_mu"] = d_mu
    d_hb = d_hb + _shift_up(d_sh)
    dproj = jnp.concatenate([dqa, dka, dva, d_hb], axis=1).astype(BF16)
    grads["even_w_in"] = matmul(hn, dproj, "tn", "emix_dwin", out_dtype=BF16)
    dhn = matmul(dproj, W["even_w_in"], "nt", "emix_dhn")
    (dx,), (dgn,) = rowwise_vjp(_f_rms_res, [x], [W["mix_norm0"]], [dhn, dy], "emix_drms")
    grads["mix_norm0"] = dgn
    return dx, grads, exchanged


def odd_mix_fwd(x, W):
    S = x.shape[0]
    hn, = rowwise(_f_rms, [x], [W["mix_norm1"]], [(D_MODEL, BF16)], "omix_rms")
    proj = matmul(hn, W["fox_w_in"], "nn", "omix_in")
    q = _heads(proj[:, :FOX_DIM], FOX_HEADS).astype(BF16)
    k = _heads(proj[:, FOX_DIM:2 * FOX_DIM], FOX_HEADS).astype(BF16)
    v = _heads(proj[:, 2 * FOX_DIM:3 * FOX_DIM], FOX_HEADS).astype(BF16)
    fz = proj[:, 3 * FOX_DIM:]
    logf, = rowwise(_f_logf, [fz], [W["fox_b_f"]], [(128, F32)], "fox_logf")
    c = seq_cumsum(logf, False, "fox_cumsum")[:, :FOX_HEADS]
    c_col = c.T.reshape(FOX_HEADS, S, 1)
    c_row = c.T.reshape(FOX_HEADS, 1, S)
    o, lse = fox_fwd(q, k, v, c_col, c_row)
    yc = _unheads(o).astype(BF16)
    out = matmul(yc, W["fox_w_out"], "nn", "omix_out", res=x)
    return out, (x, hn, q, k, v, fz, c_col, c_row, o, lse, yc)


def odd_mix_bwd(dy, saved, W):
    x, hn, q, k, v, fz, c_col, c_row, o, lse, yc = saved
    S = x.shape[0]
    grads = {}
    dyc = matmul(dy, W["fox_w_out"], "nt", "omix_dyc")
    grads["fox_w_out"] = matmul(yc, dy, "tn", "omix_dwout", out_dtype=BF16)
    do = _heads(dyc, FOX_HEADS)
    dq, drow, dk, dv, dcol = fox_bwd(q, k, v, c_col, c_row, o, lse, do)
    dc = (drow.reshape(FOX_HEADS, S) - dcol.reshape(FOX_HEADS, S)).T
    dc = jnp.pad(dc, ((0, 0), (0, 128 - FOX_HEADS)))
    dlogf = seq_cumsum(dc, True, "fox_rcumsum")
    (dfz,), (dbf,) = rowwise_vjp(_f_logf, [fz], [W["fox_b_f"]], [dlogf], "fox_dlogf")
    grads["fox_b_f"] = dbf
    dproj = jnp.concatenate([_unheads(dq), _unheads(dk), _unheads(dv), dfz], axis=1).astype(BF16)
    grads["fox_w_in"] = matmul(hn, dproj, "tn", "omix_dwin", out_dtype=BF16)
    dhn = matmul(dproj, W["fox_w_in"], "nt", "omix_dhn")
    (dx,), (dgn,) = rowwise_vjp(_f_rms_res, [x], [W["mix_norm1"]], [dhn, dy], "omix_drms")
    grads["mix_norm1"] = dgn
    return dx, grads


def device_step(x, p, target, W, gather_src, layer1_weights, layer1_grads):
    W = dict(W)
    saved = []
    h = x
    for i in range(2):
        h, s1 = ffn_fwd(h, W[f"ffn1_norm{i}"], W[f"ffn1_w_gu{i}"], W[f"ffn1_w_down{i}"], f"ffn1_{i}")
        if i == 0:
            h, s2, gathered = even_mix_fwd(h, W, gather_src)
            W.update(layer1_weights(gathered))
        else:
            h, s2 = odd_mix_fwd(h, W)
        h, s3 = ffn_fwd(h, W[f"ffn2_norm{i}"], W[f"ffn2_w_gu{i}"], W[f"ffn2_w_down{i}"], f"ffn2_{i}")
        h, s4 = ple_fwd(h, p[i], W[f"ple_norm{i}"], W[f"ple_w_gate{i}"], W[f"ple_w_proj{i}"], f"ple_{i}")
        saved.append((s1, s2, s3, s4))
    dh, d_final, loss = loss_head(h, target, W["final_norm"])
    G = {"final_norm": d_final}
    for i in (1, 0):
        s1, s2, s3, s4 = saved[i]
        dh, G[f"ple_norm{i}"], G[f"ple_w_gate{i}"], G[f"ple_w_proj{i}"] = ple_bwd(
            dh, s4, p[i], W[f"ple_norm{i}"], W[f"ple_w_gate{i}"], f"ple_{i}")
        dh, G[f"ffn2_norm{i}"], G[f"ffn2_w_gu{i}"], G[f"ffn2_w_down{i}"] = ffn_bwd(
            dh, s3, W[f"ffn2_norm{i}"], W[f"ffn2_w_gu{i}"], W[f"ffn2_w_down{i}"], f"ffn2_{i}")
        if i == 0:
            dh, gm, exchanged = even_mix_bwd(dh, s2, W, layer1_grads(G))
        else:
            dh, gm = odd_mix_bwd(dh, s2, W)
        G.update(gm)
        dh, G[f"ffn1_norm{i}"], G[f"ffn1_w_gu{i}"], G[f"ffn1_w_down{i}"] = ffn_bwd(
            dh, s1, W[f"ffn1_norm{i}"], W[f"ffn1_w_gu{i}"], W[f"ffn1_w_down{i}"], f"ffn1_{i}")
    return loss, dh, G, exchanged


_MESH = pl.DeviceIdType.MESH
_ANY = pl.BlockSpec(memory_space=pl.ANY)


def all_gather(x, name):
    def body(x_ref, out_ref, send_sems, recv_sems, local_sem):
        x_, y_, c_ = lax.axis_index("x"), lax.axis_index("y"), lax.axis_index("c")
        me, sibling = (x_, y_, c_), (x_, y_, 1 - c_)
        chips = [(1 - x_, y_), (x_, 1 - y_), (1 - x_, 1 - y_)]

        def slot(px, py, pc):
            return out_ref.at[4 * px + 2 * py + pc]

        def copy(k, block, to, src=None):
            return pltpu.make_async_remote_copy(
                src_ref=slot(*block) if src is None else src, dst_ref=slot(*block),
                send_sem=send_sems.at[k], recv_sem=recv_sems.at[k], device_id=to, device_id_type=_MESH)

        mine = pltpu.make_async_copy(x_ref, slot(*me), local_sem)
        mine.start()
        first = [copy(0, me, sibling, src=x_ref)]
        first += [copy(1 + j, me, (*chip, c_), src=x_ref) for j, chip in enumerate(chips)]
        for cp in first:
            cp.start()
        passed = [copy(4 + j, (*chip, c_), sibling) for j, chip in enumerate(chips)]
        for j, chip in enumerate(chips):
            copy(1 + j, (*chip, c_), me).wait_recv()
            passed[j].start()
        copy(0, sibling, me).wait_recv()
        for j, chip in enumerate(chips):
            copy(4 + j, (*chip, 1 - c_), me).wait_recv()
        for cp in first + passed:
            cp.wait_send()
        mine.wait()

    return pl.pallas_call(
        body,
        out_shape=jax.ShapeDtypeStruct((N_DEV,) + x.shape, x.dtype),
        in_specs=[_ANY],
        out_specs=_ANY,
        scratch_shapes=[pltpu.SemaphoreType.DMA((7,)), pltpu.SemaphoreType.DMA((7,)), pltpu.SemaphoreType.DMA(())],
        name=name,
    )(x)


def _direct_exchange(gather, s_ref, r_ref, send_sems, recv_sems, local_sem):
    x_, y_, c_ = lax.axis_index("x"), lax.axis_index("y"), lax.axis_index("c")
    my = 4 * x_ + 2 * y_ + c_
    copies = [pltpu.make_async_copy(s_ref if gather else s_ref.at[my], r_ref.at[my], local_sem)]
    for m in range(1, N_DEV):
        px = 1 - x_ if (m >> 2) & 1 else x_
        py = 1 - y_ if (m >> 1) & 1 else y_
        pc = 1 - c_ if m & 1 else c_
        copies.append(pltpu.make_async_remote_copy(
            src_ref=s_ref if gather else s_ref.at[4 * px + 2 * py + pc], dst_ref=r_ref.at[my],
            send_sem=send_sems.at[m - 1], recv_sem=recv_sems.at[m - 1],
            device_id=(px, py, pc), device_id_type=_MESH))
    return copies


_EXCHANGE_SEMS = [pltpu.SemaphoreType.DMA((7,)), pltpu.SemaphoreType.DMA((7,)), pltpu.SemaphoreType.DMA(())]


def _exchange_during(step, n_steps, gather, s_ref, r_ref, send_sems, recv_sems, local_sem):
    copies = _direct_exchange(gather, s_ref, r_ref, send_sems, recv_sems, local_sem)

    @pl.when(step == 0)
    def _():
        for cp in copies:
            cp.start()

    @pl.when(step == n_steps - 1)
    def _():
        for cp in copies:
            cp.wait()


def _exchange_out_shape(gather, src):
    return jax.ShapeDtypeStruct(((N_DEV,) + src.shape) if gather else src.shape, src.dtype)


def all_to_all(send, name):
    def body(s_ref, r_ref, send_sems, recv_sems, local_sem):
        copies = _direct_exchange(False, s_ref, r_ref, send_sems, recv_sems, local_sem)
        for cp in copies:
            cp.start()
        for cp in copies:
            cp.wait()

    return pl.pallas_call(
        body,
        out_shape=jax.ShapeDtypeStruct(send.shape, send.dtype),
        in_specs=[_ANY],
        out_specs=_ANY,
        scratch_shapes=[pltpu.SemaphoreType.DMA((7,)), pltpu.SemaphoreType.DMA((7,)), pltpu.SemaphoreType.DMA(())],
        name=name,
    )(send)


def adamw(w, m, v, parts, name, tm=256):
    R, C = w.shape
    tm = _pick(R, tm, 8) if R >= 8 else R

    def body(w_ref, m_ref, v_ref, p_ref, g_ref, d_ref, nm_ref, nv_ref):
        g = p_ref[0].astype(F32)
        for s in range(1, N_DEV):
            g = g + p_ref[s].astype(F32)
        nm = ADAM_B1 * m_ref[...] + (1.0 - ADAM_B1) * g
        nv = ADAM_B2 * v_ref[...] + (1.0 - ADAM_B2) * (g * g)
        m_hat = nm / (1.0 - ADAM_B1 ** ADAM_STEP)
        v_hat = nv / (1.0 - ADAM_B2 ** ADAM_STEP)
        g_ref[...] = g
        d_ref[...] = -ADAM_LR * (m_hat / (jnp.sqrt(v_hat) + ADAM_EPS) + ADAM_WD * w_ref[...])
        nm_ref[...] = nm
        nv_ref[...] = nv

    row = pl.BlockSpec((tm, C), lambda i: (i, 0))
    out = jax.ShapeDtypeStruct((R, C), F32)
    return pl.pallas_call(
        body,
        grid=(R // tm,),
        in_specs=[row, row, row, pl.BlockSpec((N_DEV, tm, C), lambda i: (0, i, 0))],
        out_specs=[row] * 4,
        out_shape=[out] * 4,
        compiler_params=_cparams(("parallel",)),
        name=name,
    )(w, m, v, parts)


_WEIGHTS = ["ffn1_norm", "ffn1_w_gu", "ffn1_w_down", "mix_norm", "ffn2_norm", "ffn2_w_gu", "ffn2_w_down",
            "ple_norm", "ple_w_gate", "ple_w_proj", "even_w_in", "even_w_out", "swa_sinks", "rwkv_mu",
            "rwkv_w0", "rwkv_w2", "rwkv_a0", "rwkv_a2", "rwkv_g2", "rwkv_k_k", "rwkv_k_a", "rwkv_r_k",
            "rwkv_ln_w", "rwkv_ln_b", "fox_w_in", "fox_b_f", "fox_w_out", "final_norm"]
_SHARD_AXIS = {"ffn1_w_gu": 2, "ffn1_w_down": 1, "ffn2_w_gu": 2, "ffn2_w_down": 1, "ple_w_gate": 1,
               "ple_w_proj": 2, "even_w_in": 2, "even_w_out": 1, "rwkv_w2": 2, "rwkv_a2": 2, "rwkv_g2": 2,
               "fox_w_in": 2, "fox_w_out": 1}
_SHARDED = [n for n in _WEIGHTS if n in _SHARD_AXIS]
_REPLICATED = [n for n in _WEIGHTS if n not in _SHARD_AXIS]
_PER_LAYER = ("ffn1_w_gu", "ffn1_w_down", "ffn2_w_gu", "ffn2_w_down", "ple_w_gate", "ple_w_proj")
_PIECES = [[(n, 0) for n in _PER_LAYER] + [(n, 0) for n in ("even_w_in", "even_w_out", "rwkv_w2", "rwkv_a2", "rwkv_g2")],
           [(n, 1) for n in _PER_LAYER] + [("fox_w_in", 0), ("fox_w_out", 0)]]
_LATE_GRADS = [(n, 0) for n in ("ffn1_w_gu", "ffn1_w_down", "even_w_in", "even_w_out", "rwkv_w2", "rwkv_a2", "rwkv_g2")]
_GRAD_PIECES = [_LATE_GRADS, [pc for pc in _PIECES[0] if pc not in _LATE_GRADS] + _PIECES[1]]
_PACK_LANES = 1024
_PACK_ROW_TILE = 256


def _piece_key(piece):
    name, idx = piece
    return f"{name}{idx}" if name in _PER_LAYER else name


def _pack_rows(arrs):
    flat = jnp.concatenate([a.reshape(-1, _PACK_LANES) for a in arrs], axis=0)
    return jnp.pad(flat, ((0, -flat.shape[0] % _PACK_ROW_TILE), (0, 0)))


def _unpack_rows(flat, shapes):
    out, r0 = [], 0
    for shp in shapes:
        n = math.prod(shp) // _PACK_LANES
        out.append(flat[r0:r0 + n].reshape(shp))
        r0 += n
    return out


def _unshard(gathered, pieces, shapes):
    full, r0 = {}, 0
    for piece, shp in zip(pieces, shapes):
        n = math.prod(shp) // _PACK_LANES
        seg = gathered[:, r0:r0 + n].reshape((N_DEV,) + shp)
        ax = _SHARD_AXIS[piece[0]] - 1
        seg = jnp.moveaxis(seg, 0, ax)
        full[_piece_key(piece)] = seg.reshape(shp[:ax] + (N_DEV * shp[ax],) + shp[ax + 1:])
        r0 += n
    return full


def _to_shards(full, pieces, shapes):
    segs = []
    for piece, shp in zip(pieces, shapes):
        ax = _SHARD_AXIS[piece[0]] - 1
        a = full[_piece_key(piece)].astype(BF16).reshape(shp[:ax] + (N_DEV, shp[ax]) + shp[ax + 1:])
        segs.append(jnp.moveaxis(a, ax, 0).reshape(N_DEV, -1, _PACK_LANES))
    flat = jnp.concatenate(segs, axis=1)
    return jnp.pad(flat, ((0, 0), (0, -flat.shape[1] % _PACK_ROW_TILE), (0, 0)))


def _layer_weights(full):
    W = dict(full)
    if "fox_w_in" in W:
        W["fox_w_in"] = jnp.pad(W["fox_w_in"], ((0, 0), (0, FOX_IN_PAD - W["fox_w_in"].shape[1])))
    for n in ("rwkv_w2", "rwkv_a2", "rwkv_g2"):
        if n in W:
            W[n] = W[n].astype(F32)
    return W


def _pack_small(vals):
    flat = jnp.concatenate([v.reshape(1, -1) for v in vals], axis=1)
    n = flat.shape[1]
    return jnp.pad(flat, ((0, 0), (0, -n % 128)))


def _unpack_small(flat, shapes):
    out, c0 = [], 0
    for shp in shapes:
        n = math.prod(shp)
        out.append(flat[0, c0:c0 + n].reshape(shp))
        c0 += n
    return out


def kernel(x, p, ffn1_norm, ffn1_w_gu, ffn1_w_down, mix_norm, ffn2_norm, ffn2_w_gu, ffn2_w_down, ple_norm, ple_w_gate, ple_w_proj, even_w_in, even_w_out, swa_sinks, rwkv_mu, rwkv_w0, rwkv_w2, rwkv_a0, rwkv_a2, rwkv_g2, rwkv_k_k, rwkv_k_a, rwkv_r_k, rwkv_ln_w, rwkv_ln_b, fox_w_in, fox_b_f, fox_w_out, final_norm, loss_target, m_ffn1_norm, m_ffn1_w_gu, m_ffn1_w_down, m_mix_norm, m_ffn2_norm, m_ffn2_w_gu, m_ffn2_w_down, m_ple_norm, m_ple_w_gate, m_ple_w_proj, m_even_w_in, m_even_w_out, m_swa_sinks, m_rwkv_mu, m_rwkv_w0, m_rwkv_w2, m_rwkv_a0, m_rwkv_a2, m_rwkv_g2, m_rwkv_k_k, m_rwkv_k_a, m_rwkv_r_k, m_rwkv_ln_w, m_rwkv_ln_b, m_fox_w_in, m_fox_b_f, m_fox_w_out, m_final_norm, v_ffn1_norm, v_ffn1_w_gu, v_ffn1_w_down, v_mix_norm, v_ffn2_norm, v_ffn2_w_gu, v_ffn2_w_down, v_ple_norm, v_ple_w_gate, v_ple_w_proj, v_even_w_in, v_even_w_out, v_swa_sinks, v_rwkv_mu, v_rwkv_w0, v_rwkv_w2, v_rwkv_a0, v_rwkv_a2, v_rwkv_g2, v_rwkv_k_k, v_rwkv_k_a, v_rwkv_r_k, v_rwkv_ln_w, v_rwkv_ln_b, v_fox_w_in, v_fox_b_f, v_fox_w_out, v_final_norm):
    given = dict(locals())
    w = {n: given[n] for n in _WEIGHTS}
    m = {n: given["m_" + n] for n in _WEIGHTS}
    v = {n: given["v_" + n] for n in _WEIGHTS}
    small_shapes = [w[n].shape for n in _REPLICATED]
    piece = lambda d, pc: d[pc[0]][pc[1]]
    shapes = [[piece(w, pc).shape for pc in pieces] for pieces in _PIECES]
    gshapes = [[piece(w, pc).shape for pc in pieces] for pieces in _GRAD_PIECES]
    w_send = [_pack_rows([piece(w, pc).astype(BF16) for pc in pieces]) for pieces in _PIECES]

    W = _layer_weights(_unshard(all_gather(w_send[0], "weights_all_gather"), _PIECES[0], shapes[0]))
    for i in range(2):
        for n in ("ffn1_norm", "mix_norm", "ffn2_norm", "ple_norm"):
            W[f"{n}{i}"] = w[n][i].reshape(1, -1)
    for n in ("swa_sinks", "rwkv_mu", "rwkv_w0", "rwkv_a0", "rwkv_k_k", "rwkv_k_a", "rwkv_r_k", "rwkv_ln_w",
              "rwkv_ln_b", "final_norm"):
        W[n] = w[n].reshape(1, -1)
    n_f = fox_b_f.shape[1]
    W["fox_b_f"] = jnp.pad(fox_b_f.reshape(1, n_f), ((0, 0), (0, 128 - n_f)))
    n_fox = fox_w_in.shape[2] * N_DEV

    def layer1_weights(gathered):
        return _layer_weights(_unshard(gathered, _PIECES[1], shapes[1]))

    def early_grads(G):
        G = dict(G, fox_w_in=G["fox_w_in"][:, :n_fox])
        return _to_shards(G, _GRAD_PIECES[1], gshapes[1])

    loss_row, dx, G, parts_early = device_step(x[0], p[:, 0], loss_target[0], W, w_send[1], layer1_weights,
                                               early_grads)

    parts = [all_to_all(_to_shards(G, _GRAD_PIECES[0], gshapes[0]), "grads_all_to_all"), parts_early]
    out_g, out_d, out_m, out_v = {}, {}, {}, {}
    for li, pieces in enumerate(_GRAD_PIECES):
        res = adamw(*[_pack_rows([piece(d, pc) for pc in pieces]) for d in (w, m, v)], parts[li],
                    f"adamw_sharded{li}")
        for out, rows in zip((out_g, out_d, out_m, out_v), res):
            for pc, a in zip(pieces, _unpack_rows(rows, gshapes[li])):
                out.setdefault(pc[0], {})[pc[1]] = a
    for out in (out_g, out_d, out_m, out_v):
        for n in _SHARDED:
            out[n] = jnp.stack([out[n][i] for i in sorted(out[n])])

    gsmall = {}
    for n in ("ffn1_norm", "mix_norm", "ffn2_norm", "ple_norm"):
        gsmall[n] = jnp.concatenate([G[f"{n}0"], G[f"{n}1"]], axis=0)
    for n in ("swa_sinks", "rwkv_mu", "rwkv_w0", "rwkv_a0", "rwkv_k_k", "rwkv_k_a", "rwkv_r_k", "rwkv_ln_w",
              "rwkv_ln_b", "final_norm"):
        gsmall[n] = G[n]
    gsmall["fox_b_f"] = G["fox_b_f"][:, :n_f]
    small = _pack_small([gsmall[n] for n in _REPLICATED] + [loss_row[:, :1]])
    small_parts = all_gather(small, "small_all_gather")
    pad1 = lambda vals: _pack_small(vals + [jnp.zeros((1, 1), F32)])
    gs, ds, nms, nvs = adamw(pad1([w[n] for n in _REPLICATED]), pad1([m[n] for n in _REPLICATED]),
                             pad1([v[n] for n in _REPLICATED]), small_parts, "adamw_replicated")
    out_g.update(zip(_REPLICATED, _unpack_small(gs, small_shapes)))
    out_d.update(zip(_REPLICATED, _unpack_small(ds, small_shapes)))
    out_m.update(zip(_REPLICATED, _unpack_small(nms, small_shapes)))
    out_v.update(zip(_REPLICATED, _unpack_small(nvs, small_shapes)))
    n_small = sum(math.prod(s) for s in small_shapes)
    loss = gs[0, n_small]

    return (loss, dx[None], *[out_g[n] for n in _WEIGHTS], *[out_d[n] for n in _WEIGHTS],
            *[out_m[n] for n in _WEIGHTS], *[out_v[n] for n in _WEIGHTS])
```

```python
import functools
import math

import numpy as np
import jax
import jax.numpy as jnp
from jax import lax
from jax.experimental import pallas as pl
from jax.experimental.pallas import tpu as pltpu

F32 = jnp.float32
BF16 = jnp.bfloat16

D_MODEL = 1024
HEAD_DIM = 64
BLOCK = 128
SWA_HEADS = 8
SWA_KV_HEADS = 2
SWA_GROUP = 4
RWKV_HEADS = 8
RWKV_DIM = 512
FOX_HEADS = 16
FOX_DIM = 1024
D_FF = 2816
NORM_EPS = 1e-6
GN_EPS = 64e-5
L2_EPS = 1e-12
SWA_Q = 512
SWA_KV = 128
SWA_COLS = 768
FOX_IN_PAD = 3200
N_DEV = 8
ADAM_LR = 0.001
ADAM_B1 = 0.9
ADAM_B2 = 0.999
ADAM_EPS = 1e-08
ADAM_WD = 0.01
ADAM_STEP = 10

V7X_VMEM_LIMIT = 56 * 1024 * 1024
SCAN_GROUP = 8
SCAN_CHUNK = 32

_NN = (((1,), (0,)), ((), ()))
_NT = (((1,), (1,)), ((), ()))
_TN = (((0,), (0,)), ((), ()))
_DIMS = {"nn": _NN, "nt": _NT, "tn": _TN}


def _pick(n, target, mult=128):
    best = None
    for t in range(mult, min(n, target) + 1, mult):
        if n % t == 0:
            best = t
    return best or n


def _cparams(sem):
    return pltpu.CompilerParams(dimension_semantics=sem, vmem_limit_bytes=V7X_VMEM_LIMIT)


def _dot(a, b, dims):
    return lax.dot_general(a.astype(BF16), b.astype(BF16), dims, preferred_element_type=F32)


@jax.custom_vjp
def bdot(a, b):
    return _dot(a, b, _NN)


def _bdot_fwd(a, b):
    return _dot(a, b, _NN), (a, b)


def _bdot_bwd(res, g):
    a, b = res
    return _dot(g, b, _NT), _dot(a, g, _TN)


bdot.defvjp(_bdot_fwd, _bdot_bwd)


@jax.custom_vjp
def bdot_nt(a, b):
    return _dot(a, b, _NT)


def _bdot_nt_fwd(a, b):
    return _dot(a, b, _NT), (a, b)


def _bdot_nt_bwd(res, g):
    a, b = res
    return _dot(g, b, _NN), _dot(g, a, _TN)


bdot_nt.defvjp(_bdot_nt_fwd, _bdot_nt_bwd)


def _segsum(x, bd):
    return jnp.dot(x, bd, precision=lax.Precision.HIGHEST, preferred_element_type=F32)


def _sigmoid(x):
    return 1.0 / (1.0 + jnp.exp(-x))


def _sigmoid_tanh(x):
    return 0.5 * jnp.tanh(0.5 * x) + 0.5


def _softplus(x):
    return jnp.maximum(x, 0.0) + jnp.log(1.0 + jnp.exp(-jnp.abs(x)))


def matmul(a, b, mode, name, out_dtype=F32, scale=1.0, res=None, tm=512, tn=1408, tk=1024):
    if mode == "nn":
        (M, K), (K2, N) = a.shape, b.shape
    elif mode == "nt":
        (M, K), (N, K2) = a.shape, b.shape
    else:
        (K, M), (K2, N) = a.shape, b.shape
    assert K == K2, (a.shape, b.shape, mode)
    tm, tn, tk = _pick(M, tm), _pick(N, tn), _pick(K, tk)
    nk = K // tk
    has_res = res is not None

    def body(*refs):
        if has_res:
            a_ref, b_ref, r_ref, o_ref, acc = refs
        else:
            a_ref, b_ref, o_ref, acc = refs
        kk = pl.program_id(2)

        @pl.when(kk == 0)
        def _():
            acc[...] = jnp.zeros_like(acc)

        acc[...] += _dot(a_ref[...], b_ref[...], _DIMS[mode])

        @pl.when(kk == nk - 1)
        def _():
            v = acc[...]
            if scale != 1.0:
                v = v * scale
            if has_res:
                v = v + r_ref[...].astype(F32)
            o_ref[...] = v.astype(out_dtype)

    if mode == "tn":
        a_spec = pl.BlockSpec((tk, tm), lambda i, j, k: (k, i))
    else:
        a_spec = pl.BlockSpec((tm, tk), lambda i, j, k: (i, k))
    if mode == "nt":
        b_spec = pl.BlockSpec((tn, tk), lambda i, j, k: (j, k))
    else:
        b_spec = pl.BlockSpec((tk, tn), lambda i, j, k: (k, j))
    o_spec = pl.BlockSpec((tm, tn), lambda i, j, k: (i, j))
    in_specs = [a_spec, b_spec] + ([o_spec] if has_res else [])
    args = (a, b) + ((res,) if has_res else ())
    return pl.pallas_call(
        body,
        grid=(M // tm, N // tn, nk),
        in_specs=in_specs,
        out_specs=o_spec,
        out_shape=jax.ShapeDtypeStruct((M, N), out_dtype),
        scratch_shapes=[pltpu.VMEM((tm, tn), F32)],
        compiler_params=_cparams(("parallel", "parallel", "arbitrary")),
        name=name,
    )(*args)


def _row_spec(r, tm):
    if isinstance(r, tuple):
        arr, width, blk = r
        return arr, pl.BlockSpec((tm, width), lambda i, blk=blk: (i, blk))
    return r, pl.BlockSpec((tm, r.shape[1]), lambda i: (i, 0))


def _whole_spec(p):
    return pl.BlockSpec(p.shape, lambda i: (0,) * p.ndim)


def rowwise(fn, rows, params, outs, name, tm=256):
    arrs, specs = zip(*[_row_spec(r, tm) for r in rows])
    S = arrs[0].shape[0]
    tm = min(tm, S)
    arrs, specs = zip(*[_row_spec(r, tm) for r in rows])
    n_in = len(rows) + len(params)

    def body(*refs):
        res = fn(*[r[...] for r in refs[:n_in]])
        for o_ref, v in zip(refs[n_in:], res):
            o_ref[...] = v.astype(o_ref.dtype)

    return pl.pallas_call(
        body,
        grid=(S // tm,),
        in_specs=list(specs) + [_whole_spec(p) for p in params],
        out_specs=[pl.BlockSpec((tm, c), lambda i: (i, 0)) for c, _ in outs],
        out_shape=[jax.ShapeDtypeStruct((S, c), dt) for c, dt in outs],
        compiler_params=_cparams(("parallel",)),
        name=name,
    )(*arrs, *params)


def rowwise_vjp(fn, rows, params, cots, name, need=None, row_dtype=F32, consts=(), tm=256):
    nr, npar, nc, nk = len(rows), len(params), len(cots), len(consts)
    need = [True] * nr if need is None else need
    arrs, _ = zip(*[_row_spec(r, tm) for r in rows])
    S = arrs[0].shape[0]
    tm = min(tm, S)
    arrs, specs = zip(*[_row_spec(r, tm) for r in rows])
    carrs, cspecs = zip(*[_row_spec(c, tm) for c in cots])
    widths = [s.block_shape[1] for s in specs]
    n_in = nr + npar + nk + nc

    def body(*refs):
        i = pl.program_id(0)
        xs = [r[...].astype(F32) for r in refs[:nr]]
        ps = [r[...] for r in refs[nr:nr + npar]]
        ks = [r[...] for r in refs[nr + npar:nr + npar + nk]]
        cs = [r[...].astype(F32) for r in refs[nr + npar + nk:n_in]]
        outs, vjp = jax.vjp(lambda *a: fn(*a, *ks), *xs, *ps)
        grads = vjp(tuple(cs))
        o = n_in
        for j in range(nr):
            if need[j]:
                refs[o][...] = grads[j].astype(refs[o].dtype)
                o += 1
        for j in range(npar):
            g_ref = refs[o + j]

            @pl.when(i == 0)
            def _(g_ref=g_ref):
                g_ref[...] = jnp.zeros_like(g_ref)

            g_ref[...] += grads[nr + j]

    out_specs = [pl.BlockSpec((tm, w), lambda i: (i, 0)) for w, nd in zip(widths, need) if nd]
    out_shape = [jax.ShapeDtypeStruct((S, w), row_dtype) for w, nd in zip(widths, need) if nd]
    out_specs += [_whole_spec(p) for p in params]
    out_shape += [jax.ShapeDtypeStruct(p.shape, F32) for p in params]
    res = pl.pallas_call(
        body,
        grid=(S // tm,),
        in_specs=list(specs) + [_whole_spec(p) for p in params] + [_whole_spec(k) for k in consts] + list(cspecs),
        out_specs=out_specs,
        out_shape=out_shape,
        compiler_params=_cparams(("arbitrary",)),
        name=name,
    )(*arrs, *params, *consts, *carrs)
    nrow = sum(need)
    return list(res[:nrow]), list(res[nrow:])


def _rms(x, g):
    return x * lax.rsqrt(jnp.mean(x * x, axis=-1, keepdims=True) + NORM_EPS) * g


def _f_rms(x, g):
    return (_rms(x, g),)


def _f_rms_res(x, g):
    return _rms(x, g), x


def _f_ple(x, z, pp):
    return (x + _sigmoid(z) * pp,)


def _f_mix(h, sh, mu):
    return (h + (sh - h) * mu,)


def _f_logf(fz, bf):
    return (-_softplus(-(fz + bf)),)


def _f_rwkv_pre(hk, hw, ha, hg, w0, w2, a0, a2, g2, k_k, k_a, bd):
    wlog = -_softplus(-(w0 + bdot(jnp.tanh(hw), w2))) - 0.5
    a = _sigmoid(a0 + bdot(ha, a2))
    g = bdot(_sigmoid(hg), g2)
    kk = hk * k_k
    kk = kk / jnp.maximum(jnp.sqrt(_segsum(kk * kk, bd)), L2_EPS)
    k2 = hk * (1.0 + (a - 1.0) * k_a)
    decay = jnp.exp(-jnp.exp(wlog))
    return decay, k2, kk, kk * a, g


def _f_rwkv_post(y, r, k2, v, g, ln_w, ln_b, r_k, bd):
    mean = _segsum(y, bd) * (1.0 / HEAD_DIM)
    d = y - mean
    var = _segsum(d * d, bd) * (1.0 / HEAD_DIM)
    yn = d * lax.rsqrt(var + GN_EPS) * ln_w + ln_b
    yn = yn + _segsum(r * k2 * r_k, bd) * v
    return (yn * g,)


def loss_head(x, target, gf, tm=256):
    S, D = x.shape
    tm = min(tm, S)

    def f(xt, g, tt):
        err = _rms(xt, g) - tt
        return 0.5 * jnp.sum(err * err) * (1.0 / D)

    def body(x_ref, t_ref, g_ref, dx_ref, dg_ref, l_ref):
        i = pl.program_id(0)
        val, (dx, dg) = jax.value_and_grad(f, argnums=(0, 1))(x_ref[...], g_ref[...], t_ref[...])

        @pl.when(i == 0)
        def _():
            dg_ref[...] = jnp.zeros_like(dg_ref)
            l_ref[...] = jnp.zeros_like(l_ref)

        dx_ref[...] = dx
        dg_ref[...] += dg
        l_ref[...] += jnp.full(l_ref.shape, val, F32)

    row = pl.BlockSpec((tm, D), lambda i: (i, 0))
    vec = pl.BlockSpec((1, D), lambda i: (0, 0))
    return pl.pallas_call(
        body,
        grid=(S // tm,),
        in_specs=[row, row, vec],
        out_specs=[row, vec, pl.BlockSpec((1, 128), lambda i: (0, 0))],
        out_shape=[jax.ShapeDtypeStruct((S, D), F32), jax.ShapeDtypeStruct((1, D), F32),
                   jax.ShapeDtypeStruct((1, 128), F32)],
        compiler_params=_cparams(("arbitrary",)),
        name="loss_head",
    )(x, target, gf)


def _swa_block(q, kp, kc, vp, vc, sink, slope, n):
    k = jnp.concatenate([kp, kc], axis=0)
    v = jnp.concatenate([vp, vc], axis=0)
    rows = q.shape[0]
    logits = bdot_nt(q, k) * (HEAD_DIM ** -0.5)
    qi = lax.broadcasted_iota(jnp.int32, (rows, 2 * BLOCK), 0) & (BLOCK - 1)
    ki = lax.broadcasted_iota(jnp.int32, (rows, 2 * BLOCK), 1)
    dist = qi + BLOCK - ki
    valid = (dist >= 0) & (dist < BLOCK) & ((n - 1) * BLOCK + ki >= 0)
    logits = logits - slope * dist.astype(F32)
    logits = jnp.where(valid, logits, -jnp.inf)
    m = jnp.maximum(jnp.max(logits, axis=-1, keepdims=True), sink)
    pr = jnp.exp(logits - m)
    denom = jnp.sum(pr, axis=-1, keepdims=True) + jnp.exp(sink - m)
    return bdot(pr / denom, v)


def _swa_specs(S):
    nb = S // BLOCK
    q_spec = pl.BlockSpec((None, SWA_GROUP, BLOCK, HEAD_DIM), lambda h, n: (h, 0, n, 0))
    kc_spec = pl.BlockSpec((None, BLOCK, HEAD_DIM), lambda h, n: (h, n, 0))
    kp_spec = pl.BlockSpec((None, BLOCK, HEAD_DIM), lambda h, n: (h, jnp.maximum(n - 1, 0), 0))
    col_spec = pl.BlockSpec((None, SWA_GROUP * BLOCK, 1), lambda h, n: (h, 0, 0))
    return nb, q_spec, kp_spec, kc_spec, col_spec


def swa_fwd(q, k, v, sink_col, slope_col):
    S = q.shape[2]
    nb, q_spec, kp_spec, kc_spec, col_spec = _swa_specs(S)

    def body(q_ref, kp_ref, kc_ref, vp_ref, vc_ref, s_ref, a_ref, o_ref):
        n = pl.program_id(1)
        qq = q_ref[...].reshape(SWA_GROUP * BLOCK, HEAD_DIM)
        out = _swa_block(qq, kp_ref[...], kc_ref[...], vp_ref[...], vc_ref[...], s_ref[...], a_ref[...], n)
        o_ref[...] = out.reshape(SWA_GROUP, BLOCK, HEAD_DIM)

    return pl.pallas_call(
        body,
        grid=(SWA_KV_HEADS, nb),
        in_specs=[q_spec, kp_spec, kc_spec, kp_spec, kc_spec, col_spec, col_spec],
        out_specs=q_spec,
        out_shape=jax.ShapeDtypeStruct(q.shape, F32),
        compiler_params=_cparams(("parallel", "parallel")),
        name="swa_fwd",
    )(q, k, k, v, v, sink_col, slope_col)


def swa_bwd(q, k, v, sink_col, slope_col, dout):
    S = q.shape[2]
    nb, q_spec, kp_spec, kc_spec, col_spec = _swa_specs(S)

    def body(q_ref, kp_ref, kc_ref, vp_ref, vc_ref, s_ref, a_ref, do_ref,
             dq_ref, dkp_ref, dkc_ref, dvp_ref, dvc_ref, ds_ref):
        n = pl.program_id(1)
        qq = q_ref[...].reshape(SWA_GROUP * BLOCK, HEAD_DIM)
        slope = a_ref[...]
        f = lambda a, b, c, d, e, s: _swa_block(a, b, c, d, e, s, slope, n)
        _, vjp = jax.vjp(f, qq, kp_ref[...], kc_ref[...], vp_ref[...], vc_ref[...], s_ref[...])
        dq, dkp, dkc, dvp, dvc, ds = vjp(do_ref[...].reshape(SWA_GROUP * BLOCK, HEAD_DIM))
        dq_ref[...] = dq.reshape(SWA_GROUP, BLOCK, HEAD_DIM)
        dkp_ref[...] = dkp
        dkc_ref[...] = dkc
        dvp_ref[...] = dvp
        dvc_ref[...] = dvc

        @pl.when(n == 0)
        def _():
            ds_ref[...] = jnp.zeros_like(ds_ref)

        ds_ref[...] += ds

    kv_shape = jax.ShapeDtypeStruct(k.shape, F32)
    return pl.pallas_call(
        body,
        grid=(SWA_KV_HEADS, nb),
        in_specs=[q_spec, kp_spec, kc_spec, kp_spec, kc_spec, col_spec, col_spec, q_spec],
        out_specs=[q_spec, kc_spec, kc_spec, kc_spec, kc_spec, col_spec],
        out_shape=[jax.ShapeDtypeStruct(q.shape, F32), kv_shape, kv_shape, kv_shape, kv_shape,
                   jax.ShapeDtypeStruct(sink_col.shape, F32)],
        compiler_params=_cparams(("parallel", "arbitrary")),
        name="swa_bwd",
    )(q, k, k, v, v, sink_col, slope_col, dout)


def _split2(x):
    hi = x.astype(BF16)
    return (x - hi.astype(F32)).astype(BF16), hi


def _dot2_many(xs, m):
    rows = xs[0].shape[0]
    res = jnp.dot(jnp.concatenate([p for x in xs for p in _split2(x)], axis=0), m, preferred_element_type=F32)
    return [res[(2 * i) * rows:(2 * i + 1) * rows] + res[(2 * i + 1) * rows:(2 * i + 2) * rows]
            for i in range(len(xs))]


def _dot2(x, m):
    return _dot2_many([x], m)[0]


def _seg_sums(xs, bd):
    w = bd.shape[0]
    halves = _dot2_many([x[:, i:i + w] for x in xs for i in range(0, x.shape[1], w)], bd)
    n = xs[0].shape[1] // w
    return [jnp.concatenate(halves[i * n:(i + 1) * n], axis=1) for i in range(len(xs))]


def _seg_sum(x, bd):
    return _seg_sums([x], bd)[0]


def _scan_consts():
    r = np.arange(256)
    bd = (r[:, None] // HEAD_DIM == r[None, :] // HEAD_DIM).astype(np.float32)
    c = np.arange(RWKV_DIM)
    e = (np.arange(HEAD_DIM)[:, None] // SCAN_GROUP == c[None, :] // HEAD_DIM).astype(np.float32)
    diag = (np.arange(HEAD_DIM)[:, None] == c[None, :] % HEAD_DIM).astype(np.float32)
    return jnp.asarray(bd, BF16), jnp.asarray(e, BF16), jnp.asarray(diag, F32)


def _to_colblocks(a):
    S = a.shape[0]
    a = a.reshape(S // SCAN_GROUP, SCAN_GROUP, RWKV_HEADS, HEAD_DIM)
    return a.transpose(0, 3, 2, 1).reshape(S // SCAN_GROUP, HEAD_DIM, RWKV_HEADS * SCAN_GROUP)


def _roll_up(rows):
    return pltpu.roll(rows, rows.shape[0] - 1, 0)


def _scan_pair_rows(aux, base, kk_ref, w_ref, b_ref, k_ref, bd):
    G = SCAN_GROUP
    kk_nx = _roll_up(kk_ref[pl.ds(base, G), :])
    aux[0] = w_ref[pl.ds(base, G), :] * kk_nx
    aux[1], aux[2] = _seg_sums([b_ref[pl.ds(base, G), :] * kk_nx, k_ref[pl.ds(base, G), :] * kk_nx], bd)


def _scan_pair(St, t0, base, col_g, lane_t, aux, kk_ref, w_ref, b_ref, k_ref, bd, e):
    t1 = t0 + 1
    row = lambda ref, t: ref[pl.ds(base + t, 1), :]
    arow = lambda i: aux[i, pl.ds(t0, 1), :]
    u0, m1 = _seg_sums([St * row(kk_ref, t0), St * arow(0)], bd)
    v0, v1 = _dot2_many([jnp.where(lane_t == t0, col_g, 0.0), jnp.where(lane_t == t1, col_g, 0.0)], e)
    u1 = m1 - u0 * arow(1) + v0 * arow(2)
    S0 = St * row(w_ref, t0) - u0 * row(b_ref, t0) + v0 * row(k_ref, t0)
    S1 = S0 * row(w_ref, t1) - u1 * row(b_ref, t1) + v1 * row(k_ref, t1)
    return (S0, S1), (u0, u1), (v0, v1)


def rwkv_scan_fwd(r, w, k, kk, b, vB, gather_srcs):
    S, C = r.shape
    N, G = HEAD_DIM, SCAN_GROUP
    chunk = min(SCAN_CHUNK, S)
    nchunk, ng = S // chunk, chunk // G
    bd, e, diag = _scan_consts()

    nx = len(gather_srcs)

    def body(*refs):
        r_ref, w_ref, k_ref, kk_ref, b_ref, vB_ref, bd_ref, e_ref, dg_ref = refs[:9]
        y_ref, ck_ref = refs[9 + nx:11 + nx]
        S_ref, aux, send_sems, recv_sems, local_sems = refs[11 + 2 * nx:]
        c = pl.program_id(0)
        _exchange_during(c, nchunk, True, refs[9:9 + nx], refs[11 + nx:11 + 2 * nx], send_sems, recv_sems, local_sems)

        @pl.when(c == 0)
        def _():
            S_ref[...] = jnp.zeros_like(S_ref)

        ck_ref[...] = S_ref[...]
        sub = lax.broadcasted_iota(jnp.int32, (G, C), 0)
        lane_t = lax.broadcasted_iota(jnp.int32, (N, N), 1) & (G - 1)

        def group(g, St):
            base = pl.multiple_of(g * G, G)
            vb = vB_ref[g]
            _scan_pair_rows(aux, base, kk_ref, w_ref, b_ref, k_ref, bd_ref[...])
            ys = jnp.zeros((G, C), F32)
            def emit(ys, states, t0):
                steps = (t0, t0 + 1)
                y_bs = _seg_sums([S_t * r_ref[pl.ds(base + tt, 1), :] for S_t, tt in zip(states, steps)], bd_ref[...])
                for y_b, tt in zip(y_bs, steps):
                    ys = jnp.where(sub == tt, jnp.sum(y_b * dg_ref[...], axis=0, keepdims=True), ys)
                return ys

            pending = None
            for t0 in range(0, G, 2):
                states, _, _ = _scan_pair(St, t0, base, vb, lane_t, aux, kk_ref, w_ref, b_ref, k_ref, bd_ref[...],
                                          e_ref[...])
                if pending is not None:
                    ys = emit(ys, *pending)
                pending = (states, t0)
                St = states[1]
            y_ref[pl.ds(base, G), :] = emit(ys, *pending)
            return St

        S_ref[...] = lax.fori_loop(0, ng, group, S_ref[...])

    row = pl.BlockSpec((chunk, C), lambda c: (c, 0))
    col = pl.BlockSpec((ng, N, N), lambda c: (c, 0, 0))
    res = pl.pallas_call(
        body,
        grid=(nchunk,),
        in_specs=[row] * 5 + [col, _whole_spec(bd), _whole_spec(e), _whole_spec(diag)] + [_ANY] * nx,
        out_specs=[row, pl.BlockSpec((None, N, C), lambda c: (c, 0, 0))] + [_ANY] * nx,
        out_shape=[jax.ShapeDtypeStruct((S, C), F32), jax.ShapeDtypeStruct((nchunk, N, C), F32)]
        + _exchange_out_shapes(True, gather_srcs),
        scratch_shapes=[pltpu.VMEM((N, C), F32), pltpu.VMEM((3, G, C), F32)] + _exchange_sems(nx),
        compiler_params=_cparams(("arbitrary",)),
        name="rwkv_scan_fwd",
    )(r, w, k, kk, b, vB, bd, e, diag, *gather_srcs)
    return res[0], res[1], list(res[2:])


def rwkv_scan_bwd(r, w, k, kk, b, vB, dyB, ckpt, scatter_srcs):
    S, C = r.shape
    N, G = HEAD_DIM, SCAN_GROUP
    chunk = min(SCAN_CHUNK, S)
    nchunk, ng = S // chunk, chunk // G
    bd, e, diag = _scan_consts()

    nx = len(scatter_srcs)

    def body(*refs):
        r_ref, w_ref, k_ref, kk_ref, b_ref, vB_ref, dyB_ref, ck_ref, bd_ref, e_ref, dg_ref = refs[:11]
        dr_ref, dw_ref, dk_ref, dkk_ref, db_ref, dv_ref = refs[11 + nx:17 + nx]
        G_ref, sbuf, ubuf, vbuf, aux, send_sems, recv_sems, local_sems = refs[17 + 2 * nx:]
        c = pl.program_id(0)
        _exchange_during(c, nchunk, False, refs[11:11 + nx], refs[17 + nx:17 + 2 * nx], send_sems, recv_sems,
                         local_sems)

        @pl.when(c == 0)
        def _():
            G_ref[...] = jnp.zeros_like(G_ref)

        lane_t = lax.broadcasted_iota(jnp.int32, (N, N), 1) & (G - 1)
        sub = lax.broadcasted_iota(jnp.int32, (G, C), 0)

        def fgroup(g, St):
            base = pl.multiple_of(g * G, G)
            vb = vB_ref[g]
            _scan_pair_rows(aux, base, kk_ref, w_ref, b_ref, k_ref, bd_ref[...])
            for t0 in range(0, G, 2):
                states, us, vs = _scan_pair(St, t0, base, vb, lane_t, aux, kk_ref, w_ref, b_ref, k_ref, bd_ref[...],
                                            e_ref[...])
                for i, S_before in enumerate((St, states[0])):
                    sbuf[base + t0 + i] = S_before
                    ubuf[base + t0 + i] = us[i]
                    vbuf[base + t0 + i] = vs[i]
                St = states[1]
            return St

        sbuf[chunk] = lax.fori_loop(0, ng, fgroup, ck_ref[...])

        def bgroup(gi, Gt):
            g = ng - 1 - gi
            base = pl.multiple_of(g * G, G)
            dyb = dyB_ref[g]
            rows = [jnp.zeros((G, C), F32) for _ in range(6)]
            colsum = lambda a: jnp.sum(a, axis=0, keepdims=True)
            row = lambda ref, t: ref[pl.ds(base + t, 1), :]
            b8 = b_ref[pl.ds(base, G), :]
            aux[0] = _roll_up(w_ref[pl.ds(base, G), :]) * b8
            aux[1], aux[2] = _seg_sums([_roll_up(kk_ref[pl.ds(base, G), :]) * b8, r_ref[pl.ds(base, G), :] * b8],
                                       bd_ref[...])

            def emit(rows, steps):
                d_vs = _seg_sums([Gt_ * row(k_ref, tt) for tt, Gt_, _, _ in steps], bd_ref[...])
                for (tt, Gt_, du_b, dy_b), d_vb in zip(steps, d_vs):
                    Sp, Sc = sbuf[base + tt], sbuf[base + tt + 1]
                    new = (colsum(Sc * dy_b), colsum(Gt_ * Sp), colsum(Gt_ * vbuf[base + tt]), colsum(Sp * du_b),
                           -colsum(Gt_ * ubuf[base + tt]), colsum(d_vb * dg_ref[...]))
                    rows = [jnp.where(sub == tt, n_, acc) for n_, acc in zip(new, rows)]
                return rows

            pending = None
            for t0 in reversed(range(0, G, 2)):
                t1 = t0 + 1
                arow = lambda i: aux[i, pl.ds(t0, 1), :]
                dy1, dy0 = _dot2_many([jnp.where(lane_t == t1, dyb, 0.0), jnp.where(lane_t == t0, dyb, 0.0)],
                                      e_ref[...])
                G1 = Gt + dy1 * row(r_ref, t1)
                m1, m2 = _seg_sums([G1 * row(b_ref, t1), G1 * arow(0)], bd_ref[...])
                du1 = -m1
                du0 = -(m2 + du1 * arow(1) + dy0 * arow(2))
                G0 = G1 * row(w_ref, t1) + du1 * row(kk_ref, t1) + dy0 * row(r_ref, t0)
                G_next = G0 * row(w_ref, t0) + du0 * row(kk_ref, t0)
                if pending is not None:
                    rows = emit(rows, pending)
                pending = ((t1, G1, du1, dy1), (t0, G0, du0, dy0))
                Gt = G_next
            rows = emit(rows, pending)
            for ref, val in zip((dr_ref, dw_ref, dk_ref, dkk_ref, db_ref, dv_ref), rows):
                ref[pl.ds(base, G), :] = val
            return Gt

        G_ref[...] = lax.fori_loop(0, ng, bgroup, G_ref[...])

    rev = lambda c: nchunk - 1 - c
    row = pl.BlockSpec((chunk, C), lambda c: (rev(c), 0))
    col = pl.BlockSpec((ng, N, N), lambda c: (rev(c), 0, 0))
    rshape = jax.ShapeDtypeStruct((S, C), F32)
    res = pl.pallas_call(
        body,
        grid=(nchunk,),
        in_specs=[row] * 5 + [col, col, pl.BlockSpec((None, N, C), lambda c: (rev(c), 0, 0)),
                              _whole_spec(bd), _whole_spec(e), _whole_spec(diag)] + [_ANY] * nx,
        out_specs=[row] * 6 + [_ANY] * nx,
        out_shape=[rshape] * 6 + _exchange_out_shapes(False, scatter_srcs),
        scratch_shapes=[pltpu.VMEM((N, C), F32), pltpu.VMEM((chunk + 1, N, C), F32),
                        pltpu.VMEM((chunk, N, C), F32), pltpu.VMEM((chunk, N, C), F32),
                        pltpu.VMEM((3, G, C), F32)] + _exchange_sems(nx),
        compiler_params=_cparams(("arbitrary",)),
        name="rwkv_scan_bwd",
    )(r, w, k, kk, b, vB, dyB, ckpt, bd, e, diag, *scatter_srcs)
    return tuple(res[:6]) + (list(res[6:]),)


def seq_cumsum(x, reverse, name):
    S, C = x.shape
    tb = min(256, S)
    nb = S // tb

    def body(x_ref, o_ref, carry):
        i = pl.program_id(0)

        @pl.when(i == 0)
        def _():
            carry[...] = jnp.zeros_like(carry)

        ri = lax.broadcasted_iota(jnp.int32, (tb, tb), 0)
        ci = lax.broadcasted_iota(jnp.int32, (tb, tb), 1)
        tri = jnp.where((ci >= ri) if reverse else (ci <= ri), 1.0, 0.0).astype(F32)
        xb = x_ref[...]
        out = jnp.dot(tri, xb, precision=lax.Precision.HIGHEST, preferred_element_type=F32) + carry[...]
        o_ref[...] = out
        carry[...] = carry[...] + jnp.sum(xb, axis=0, keepdims=True)

    idx = (lambda i: (nb - 1 - i, 0)) if reverse else (lambda i: (i, 0))
    return pl.pallas_call(
        body,
        grid=(nb,),
        in_specs=[pl.BlockSpec((tb, C), idx)],
        out_specs=pl.BlockSpec((tb, C), idx),
        out_shape=jax.ShapeDtypeStruct((S, C), F32),
        scratch_shapes=[pltpu.VMEM((1, C), F32)],
        compiler_params=_cparams(("arbitrary",)),
        name=name,
    )(x)


FOX_STRIP = 512


def _fox_logits(q, k, cq, ck, row0, col0):
    s = _dot(q, k, _NT) * (HEAD_DIM ** -0.5) + cq - ck
    row = row0 + lax.broadcasted_iota(jnp.int32, s.shape, 0)
    col = col0 + lax.broadcasted_iota(jnp.int32, s.shape, 1)
    return jnp.where(col <= row, s, -jnp.inf)


def _fox_strips(tq):
    st = min(FOX_STRIP, tq)
    return [(r * st, slice(r * st, (r + 1) * st)) for r in range(tq // st)]


def _fox_tiles(n, by_query):
    pairs = [(i, j) for i in range(n) for j in range(i + 1)] if by_query else \
            [(i, j) for j in range(n) for i in range(j, n)]
    return (jnp.asarray(np.array([p[0] for p in pairs], np.int32)),
            jnp.asarray(np.array([p[1] for p in pairs], np.int32)))


def _fox_specs(t, Dh):
    qs = pl.BlockSpec((None, t, Dh), lambda h, s, qt, kt: (h, qt[s], 0))
    ks = pl.BlockSpec((None, t, Dh), lambda h, s, qt, kt: (h, kt[s], 0))
    cqs = pl.BlockSpec((None, t, 1), lambda h, s, qt, kt: (h, qt[s], 0))
    cks = pl.BlockSpec((None, 1, t), lambda h, s, qt, kt: (h, 0, kt[s]))
    return qs, ks, cqs, cks


def _fox_call(body, tiles, Hh, in_specs, out_specs, out_shape, scratch, name, args):
    spec = pltpu.PrefetchScalarGridSpec(num_scalar_prefetch=2, grid=(Hh, tiles[0].shape[0]), in_specs=in_specs,
                                        out_specs=out_specs, scratch_shapes=scratch)
    return pl.pallas_call(body, grid_spec=spec, out_shape=out_shape,
                          compiler_params=_cparams(("parallel", "arbitrary")), name=name)(*tiles, *args)


def fox_fwd(q, k, v, c_col, c_row):
    Hh, S, Dh = q.shape
    tq = tk = min(512, S)

    def body(qt_ref, kt_ref, q_ref, k_ref, v_ref, cq_ref, ck_ref, o_ref, lse_ref, m_s, l_s, acc_s):
        qi, ki = qt_ref[pl.program_id(1)], kt_ref[pl.program_id(1)]

        @pl.when(ki == 0)
        def _():
            m_s[...] = jnp.full_like(m_s, -jnp.inf)
            l_s[...] = jnp.zeros_like(l_s)
            acc_s[...] = jnp.zeros_like(acc_s)

        kb, vb, ck = k_ref[...], v_ref[...], ck_ref[...]
        for r0, rs in _fox_strips(tq):
            s = _fox_logits(q_ref[rs, :], kb, cq_ref[rs, :], ck, qi * tq + r0, ki * tk)
            m_old = m_s[rs, :]
            m_new = jnp.maximum(m_old, jnp.max(s, axis=-1, keepdims=True))
            alpha = jnp.exp(m_old - m_new)
            p = jnp.exp(s - m_new)
            l_s[rs, :] = alpha * l_s[rs, :] + jnp.sum(p, axis=-1, keepdims=True)
            acc_s[rs, :] = alpha * acc_s[rs, :] + _dot(p, vb, _NN)
            m_s[rs, :] = m_new

        @pl.when(ki == qi)
        def _():
            o_ref[...] = acc_s[...] / l_s[...]
            lse_ref[...] = m_s[...] + jnp.log(l_s[...])

    qs, ks, cqs, cks = _fox_specs(tq, Dh)
    return _fox_call(
        body, _fox_tiles(S // tq, True), Hh, [qs, ks, ks, cqs, cks], [qs, cqs],
        [jax.ShapeDtypeStruct((Hh, S, Dh), F32), jax.ShapeDtypeStruct((Hh, S, 1), F32)],
        [pltpu.VMEM((tq, 1), F32), pltpu.VMEM((tq, 1), F32), pltpu.VMEM((tq, Dh), F32)],
        "fox_fwd", (q, k, v, c_col, c_row))


def fox_bwd(q, k, v, c_col, c_row, o, lse, do):
    Hh, S, Dh = q.shape
    tq = tk = min(512, S)
    nk = S // tk

    def body(qt_ref, kt_ref, q_ref, k_ref, v_ref, cq_ref, ck_ref, o_ref, lse_ref, do_ref,
             dq_ref, dr_ref, dk_ref, dv_ref, dc_ref, acc_s, row_s):
        step = pl.program_id(1)
        qi, ki = qt_ref[step], kt_ref[step]

        @pl.when(step == 0)
        def _():
            dk_ref[...] = jnp.zeros_like(dk_ref)
            dv_ref[...] = jnp.zeros_like(dv_ref)
            dc_ref[...] = jnp.zeros_like(dc_ref)

        @pl.when(ki == 0)
        def _():
            acc_s[...] = jnp.zeros_like(acc_s)
            row_s[...] = jnp.zeros_like(row_s)

        q_t, kb, vb, do_t = q_ref[...], k_ref[...], v_ref[...], do_ref[...]
        s = _fox_logits(q_t, kb, cq_ref[...], ck_ref[...], qi * tq, ki * tk)
        p = jnp.exp(s - lse_ref[...])
        delta = jnp.sum(do_t * o_ref[...], axis=-1, keepdims=True)
        ds = p * (_dot(do_t, vb, _NT) - delta)
        acc_s[...] += _dot(ds, kb, _NN)
        row_s[...] += jnp.sum(ds, axis=-1, keepdims=True)
        dk_ref[ki] += _dot(ds, q_t, _TN) * (HEAD_DIM ** -0.5)
        dv_ref[ki] += _dot(p, do_t, _TN)
        dc_ref[ki] += jnp.sum(ds, axis=0, keepdims=True)

        @pl.when(ki == qi)
        def _():
            dq_ref[...] = acc_s[...] * (HEAD_DIM ** -0.5)
            dr_ref[...] = row_s[...]

    qs, ks, cqs, cks = _fox_specs(tq, Dh)
    head = lambda *blk: pl.BlockSpec((None,) + blk, lambda h, s, qt, kt: (h,) + (0,) * len(blk))
    dq, dr, dk, dv, dc = _fox_call(
        body, _fox_tiles(S // tq, True), Hh, [qs, ks, ks, cqs, cks, qs, cqs, qs],
        [qs, cqs, head(nk, tk, Dh), head(nk, tk, Dh), head(nk, 1, tk)],
        [jax.ShapeDtypeStruct((Hh, S, Dh), F32), jax.ShapeDtypeStruct((Hh, S, 1), F32),
         jax.ShapeDtypeStruct((Hh, nk, tk, Dh), F32), jax.ShapeDtypeStruct((Hh, nk, tk, Dh), F32),
         jax.ShapeDtypeStruct((Hh, nk, 1, tk), F32)],
        [pltpu.VMEM((tq, Dh), F32), pltpu.VMEM((tq, 1), F32)],
        "fox_bwd", (q, k, v, c_col, c_row, o, lse, do))
    return dq, dr, dk.reshape(Hh, S, Dh), dv.reshape(Hh, S, Dh), dc.reshape(Hh, 1, S)


def _heads(a, nh):
    S = a.shape[0]
    return a.reshape(S, nh, HEAD_DIM).transpose(1, 0, 2)


def _unheads(a):
    nh, S, _ = a.shape
    return a.transpose(1, 0, 2).reshape(S, nh * HEAD_DIM)


def _shift_down(a):
    return jnp.pad(a[:-1], ((1, 0), (0, 0)))


def _shift_up(a):
    return jnp.pad(a[1:], ((0, 1), (0, 0)))


def _block_diag_ones():
    i = np.arange(RWKV_DIM) // HEAD_DIM
    return jnp.asarray((i[:, None] == i[None, :]).astype(np.float32))


FFN_ROWS = 1024
FFN_COLS = 256


def _ffn_specs(S, F, tm, fc):
    nf = F // fc
    row = pl.BlockSpec((tm, D_MODEL), lambda i, j: (i, 0))
    vec = pl.BlockSpec((1, D_MODEL), lambda i, j: (0, 0))
    wg = pl.BlockSpec((D_MODEL, fc), lambda i, j: (0, j))
    wu = pl.BlockSpec((D_MODEL, fc), lambda i, j: (0, nf + j))
    wd = pl.BlockSpec((fc, D_MODEL), lambda i, j: (j, 0))
    hid = pl.BlockSpec((tm, fc), lambda i, j: (i, j))
    return nf, row, vec, wg, wu, wd, hid


def ffn_fwd(x, g_norm, w_gu, w_down, tag):
    S, F = x.shape[0], w_down.shape[0]
    tm, fc = min(FFN_ROWS, S), FFN_COLS
    nf, row, vec, wg, wu, wd, _ = _ffn_specs(S, F, tm, fc)

    def body(x_ref, g_ref, wg_ref, wu_ref, wd_ref, o_ref, hn_ref, hn_s, acc):
        j = pl.program_id(1)

        @pl.when(j == 0)
        def _():
            hn_s[...] = _rms(x_ref[...], g_ref[...]).astype(BF16)
            hn_ref[...] = hn_s[...]
            acc[...] = jnp.zeros_like(acc)

        g = _dot(hn_s[...], wg_ref[...], _NN)
        u = _dot(hn_s[...], wu_ref[...], _NN)
        acc[...] += _dot(g * _sigmoid_tanh(g) * u, wd_ref[...], _NN)

        @pl.when(j == nf - 1)
        def _():
            o_ref[...] = x_ref[...] + 0.5 * acc[...]

    out, hn = pl.pallas_call(
        body,
        grid=(S // tm, nf),
        in_specs=[row, vec, wg, wu, wd],
        out_specs=[row, row],
        out_shape=[jax.ShapeDtypeStruct((S, D_MODEL), F32), jax.ShapeDtypeStruct((S, D_MODEL), BF16)],
        scratch_shapes=[pltpu.VMEM((tm, D_MODEL), BF16), pltpu.VMEM((tm, D_MODEL), F32)],
        compiler_params=_cparams(("parallel", "arbitrary")),
        name=tag + "_fwd",
    )(x, g_norm, w_gu, w_gu, w_down)
    return out, (x, hn)


def ffn_bwd(dy, saved, g_norm, w_gu, w_down, tag):
    x, hn = saved
    S, F = x.shape[0], w_down.shape[0]
    tm, fc = min(FFN_ROWS, S), FFN_COLS
    nf, row, vec, wg, wu, wd, hid = _ffn_specs(S, F, tm, fc)

    def body(dy_ref, x_ref, hn_ref, g_ref, wg_ref, wu_ref, wd_ref, dx_ref, dgn_ref, a_ref, dg_ref, du_ref,
             dyh_s, dhn):
        i, j = pl.program_id(0), pl.program_id(1)

        @pl.when(j == 0)
        def _():
            dyh_s[...] = (0.5 * dy_ref[...]).astype(BF16)
            dhn[...] = jnp.zeros_like(dhn)

        hn_t = hn_ref[...]
        g = _dot(hn_t, wg_ref[...], _NN)
        u = _dot(hn_t, wu_ref[...], _NN)
        da = _dot(dyh_s[...], wd_ref[...], _NT)
        sig = _sigmoid_tanh(g)
        gs = g * sig
        a_ref[...] = (gs * u).astype(BF16)
        dg = ((da * u) * (sig + gs * (1.0 - sig))).astype(BF16)
        du = (da * gs).astype(BF16)
        dg_ref[...] = dg
        du_ref[...] = du
        dhn[...] += _dot(jnp.concatenate([dg, du], axis=1),
                         jnp.concatenate([wg_ref[...], wu_ref[...]], axis=1), _NT)

        @pl.when(j == nf - 1)
        def _():
            _, vjp_n = jax.vjp(_rms, x_ref[...], g_ref[...])
            dx, dgn = vjp_n(dhn[...])
            dx_ref[...] = dy_ref[...] + dx

            @pl.when(i == 0)
            def _():
                dgn_ref[...] = jnp.zeros_like(dgn_ref)

            dgn_ref[...] += dgn

    hshape = jax.ShapeDtypeStruct((S, F), BF16)
    dx, dgn, act, dg, du = pl.pallas_call(
        body,
        grid=(S // tm, nf),
        in_specs=[row, row, row, vec, wg, wu, wd],
        out_specs=[row, vec, hid, hid, hid],
        out_shape=[jax.ShapeDtypeStruct((S, D_MODEL), F32), jax.ShapeDtypeStruct((1, D_MODEL), F32),
                   hshape, hshape, hshape],
        scratch_shapes=[pltpu.VMEM((tm, D_MODEL), BF16), pltpu.VMEM((tm, D_MODEL), F32)],
        compiler_params=_cparams(("arbitrary", "arbitrary")),
        name=tag + "_bwd",
    )(dy, x, hn, g_norm, w_gu, w_gu, w_down)
    d_wdown = matmul(act, dy, "tn", tag + "_dwd", out_dtype=BF16, scale=0.5)
    d_wgu = jnp.concatenate([matmul(hn, dg, "tn", tag + "_dwg", out_dtype=BF16),
                             matmul(hn, du, "tn", tag + "_dwu", out_dtype=BF16)], axis=1)
    return dx, dgn, d_wgu, d_wdown


def ple_fwd(x, p_i, g_norm, w_gate, w_proj, tag):
    hn, = rowwise(_f_rms, [x], [g_norm], [(D_MODEL, BF16)], tag + "_rms")
    z = matmul(hn, w_gate, "nn", tag + "_gate")
    pp = matmul(p_i, w_proj, "nn", tag + "_proj")
    out, = rowwise(_f_ple, [x, z, pp], [], [(D_MODEL, F32)], tag + "_mix")
    return out, (x, hn, z, pp)


def ple_bwd(dy, saved, p_i, g_norm, w_gate, tag):
    x, hn, z, pp = saved
    (dz, dpp), _ = rowwise_vjp(_f_ple, [x, z, pp], [], [dy], tag + "_dmix", need=[False, True, True],
                               row_dtype=BF16)
    d_wproj = matmul(p_i, dpp, "tn", tag + "_dwp", out_dtype=BF16)
    d_wgate = matmul(hn, dz, "tn", tag + "_dwg", out_dtype=BF16)
    dhn = matmul(dz, w_gate, "nt", tag + "_dhn")
    (dx,), (dgn,) = rowwise_vjp(_f_rms_res, [x], [g_norm], [dhn, dy], tag + "_drms")
    return dx, dgn, d_wgate, d_wproj


def _swa_consts(sinks):
    slopes = np.asarray([2.0 ** (-(i + 1)) for i in range(SWA_HEADS)], np.float32)
    slope_col = jnp.asarray(np.repeat(slopes, BLOCK).reshape(SWA_KV_HEADS, SWA_GROUP * BLOCK, 1))
    sink_col = jnp.repeat(sinks.reshape(SWA_HEADS), BLOCK).reshape(SWA_KV_HEADS, SWA_GROUP * BLOCK, 1)
    return sink_col, slope_col


def even_mix_fwd(x, W, gather_src):
    S = x.shape[0]
    hn, = rowwise(_f_rms, [x], [W["mix_norm0"]], [(D_MODEL, BF16)], "emix_rms")
    proj = matmul(hn, W["even_w_in"], "nn", "emix_in")
    qa = _heads(proj[:, :SWA_Q], SWA_HEADS).reshape(SWA_KV_HEADS, SWA_GROUP, S, HEAD_DIM)
    ka = _heads(proj[:, SWA_Q:SWA_Q + SWA_KV], SWA_KV_HEADS)
    va = _heads(proj[:, SWA_Q + SWA_KV:SWA_COLS], SWA_KV_HEADS)
    sink_col, slope_col = _swa_consts(W["swa_sinks"])
    ya = swa_fwd(qa, ka, va, sink_col, slope_col)
    ya = _unheads(ya.reshape(SWA_HEADS, S, HEAD_DIM))
    hb = proj[:, SWA_COLS:]
    h, = rowwise(_f_mix, [hb, _shift_down(hb)], [W["rwkv_mu"]], [(hb.shape[1], F32)], "rwkv_shift")
    hr, hk, hv = h[:, :512], h[:, 512:1024], h[:, 1024:1536]
    hw, ha, hg = h[:, 1536:1600], h[:, 1600:1664], h[:, 1664:1792]
    bd = _block_diag_ones()
    pre_params = [W["rwkv_w0"], W["rwkv_w2"], W["rwkv_a0"], W["rwkv_a2"], W["rwkv_g2"], W["rwkv_k_k"],
                  W["rwkv_k_a"]]
    decay, k2, kk, b, g = rowwise(_f_rwkv_pre, [hk, hw, ha, hg], pre_params + [bd],
                                  [(RWKV_DIM, F32)] * 5, "rwkv_pre")
    vT = _to_colblocks(hv)
    y, ckpt, gathered = rwkv_scan_fwd(hr, decay, k2, kk, b, vT, gather_src)
    post_params = [W["rwkv_ln_w"], W["rwkv_ln_b"], W["rwkv_r_k"]]
    yb, = rowwise(_f_rwkv_post, [y, hr, k2, hv, g], post_params + [bd], [(RWKV_DIM, F32)], "rwkv_post")
    cat = jnp.concatenate([ya, yb], axis=1).astype(BF16)
    out = matmul(cat, W["even_w_out"], "nn", "emix_out", res=x)
    saved = (x, hn, qa, ka, va, sink_col, slope_col, hb, hr, hk, hv, hw, ha, hg, decay, k2, kk, b, g, vT,
             ckpt, y, cat)
    return out, saved, gathered


def even_mix_bwd(dy, saved, W, scatter_src):
    (x, hn, qa, ka, va, sink_col, slope_col, hb, hr, hk, hv, hw, ha, hg, decay, k2, kk, b, g, vT, ckpt, y,
     cat) = saved
    S = x.shape[0]
    grads = {}
    dcat = matmul(dy, W["even_w_out"], "nt", "emix_dcat")
    grads["even_w_out"] = matmul(cat, dy, "tn", "emix_dwout", out_dtype=BF16)
    dya, dyb = dcat[:, :SWA_Q], dcat[:, SWA_Q:]
    dya_h = _heads(dya, SWA_HEADS).reshape(SWA_KV_HEADS, SWA_GROUP, S, HEAD_DIM)
    dqa, dkp, dkc, dvp, dvc, dsink = swa_bwd(qa, ka, va, sink_col, slope_col, dya_h)
    shift_blk = lambda a: jnp.pad(a[:, BLOCK:], ((0, 0), (0, BLOCK), (0, 0)))
    dka = dkc + shift_blk(dkp)
    dva = dvc + shift_blk(dvp)
    grads["swa_sinks"] = dsink.reshape(SWA_HEADS, BLOCK).sum(axis=1).reshape(1, SWA_HEADS)
    dqa = _unheads(dqa.reshape(SWA_HEADS, S, HEAD_DIM))
    dka, dva = _unheads(dka), _unheads(dva)
    bd = _block_diag_ones()
    post_params = [W["rwkv_ln_w"], W["rwkv_ln_b"], W["rwkv_r_k"]]
    (d_y, d_r1, d_k2a, d_v1, d_g), (d_lnw, d_lnb, d_rk) = rowwise_vjp(
        _f_rwkv_post, [y, hr, k2, hv, g], post_params, [dyb], "rwkv_dpost", consts=[bd], tm=128)
    grads["rwkv_ln_w"], grads["rwkv_ln_b"], grads["rwkv_r_k"] = d_lnw, d_lnb, d_rk
    d_r2, d_w, d_k2b, d_kk, d_b, d_v2, exchanged = rwkv_scan_bwd(hr, decay, k2, kk, b, vT, _to_colblocks(d_y), ckpt,
                                                                  scatter_src)
    pre_params = [W["rwkv_w0"], W["rwkv_w2"], W["rwkv_a0"], W["rwkv_a2"], W["rwkv_g2"], W["rwkv_k_k"],
                  W["rwkv_k_a"]]
    (d_hk, d_hw, d_ha, d_hg), dpre = rowwise_vjp(
        _f_rwkv_pre, [hk, hw, ha, hg], pre_params, [d_w, d_k2a + d_k2b, d_kk, d_b, d_g], "rwkv_dpre",
        consts=[bd], tm=128)
    for nm, gval in zip(["rwkv_w0", "rwkv_w2", "rwkv_a0", "rwkv_a2", "rwkv_g2", "rwkv_k_k", "rwkv_k_a"], dpre):
        grads[nm] = gval
    d_h = jnp.concatenate([d_r1 + d_r2, d_hk, d_v1 + d_v2, d_hw, d_ha, d_hg], axis=1)
    (d_hb, d_sh), (d_mu,) = rowwise_vjp(_f_mix, [hb, _shift_down(hb)], [W["rwkv_mu"]], [d_h], "rwkv_dshift")
    grads["rwkv_mu"] = d_mu
    d_hb = d_hb + _shift_up(d_sh)
    dproj = jnp.concatenate([dqa, dka, dva, d_hb], axis=1).astype(BF16)
    grads["even_w_in"] = matmul(hn, dproj, "tn", "emix_dwin", out_dtype=BF16)
    dhn = matmul(dproj, W["even_w_in"], "nt", "emix_dhn")
    (dx,), (dgn,) = rowwise_vjp(_f_rms_res, [x], [W["mix_norm0"]], [dhn, dy], "emix_drms")
    grads["mix_norm0"] = dgn
    return dx, grads, exchanged


def odd_mix_fwd(x, W):
    S = x.shape[0]
    hn, = rowwise(_f_rms, [x], [W["mix_norm1"]], [(D_MODEL, BF16)], "omix_rms")
    proj = matmul(hn, W["fox_w_in"], "nn", "omix_in")
    q = _heads(proj[:, :FOX_DIM], FOX_HEADS).astype(BF16)
    k = _heads(proj[:, FOX_DIM:2 * FOX_DIM], FOX_HEADS).astype(BF16)
    v = _heads(proj[:, 2 * FOX_DIM:3 * FOX_DIM], FOX_HEADS).astype(BF16)
    fz = proj[:, 3 * FOX_DIM:]
    logf, = rowwise(_f_logf, [fz], [W["fox_b_f"]], [(128, F32)], "fox_logf")
    c = seq_cumsum(logf, False, "fox_cumsum")[:, :FOX_HEADS]
    c_col = c.T.reshape(FOX_HEADS, S, 1)
    c_row = c.T.reshape(FOX_HEADS, 1, S)
    o, lse = fox_fwd(q, k, v, c_col, c_row)
    yc = _unheads(o).astype(BF16)
    out = matmul(yc, W["fox_w_out"], "nn", "omix_out", res=x)
    return out, (x, hn, q, k, v, fz, c_col, c_row, o, lse, yc)


def odd_mix_bwd(dy, saved, W):
    x, hn, q, k, v, fz, c_col, c_row, o, lse, yc = saved
    S = x.shape[0]
    grads = {}
    dyc = matmul(dy, W["fox_w_out"], "nt", "omix_dyc")
    grads["fox_w_out"] = matmul(yc, dy, "tn", "omix_dwout", out_dtype=BF16)
    do = _heads(dyc, FOX_HEADS)
    dq, drow, dk, dv, dcol = fox_bwd(q, k, v, c_col, c_row, o, lse, do)
    dc = (drow.reshape(FOX_HEADS, S) - dcol.reshape(FOX_HEADS, S)).T
    dc = jnp.pad(dc, ((0, 0), (0, 128 - FOX_HEADS)))
    dlogf = seq_cumsum(dc, True, "fox_rcumsum")
    (dfz,), (dbf,) = rowwise_vjp(_f_logf, [fz], [W["fox_b_f"]], [dlogf], "fox_dlogf")
    grads["fox_b_f"] = dbf
    dproj = jnp.concatenate([_unheads(dq), _unheads(dk), _unheads(dv), dfz], axis=1).astype(BF16)
    grads["fox_w_in"] = matmul(hn, dproj, "tn", "omix_dwin", out_dtype=BF16)
    dhn = matmul(dproj, W["fox_w_in"], "nt", "omix_dhn")
    (dx,), (dgn,) = rowwise_vjp(_f_rms_res, [x], [W["mix_norm1"]], [dhn, dy], "omix_drms")
    grads["mix_norm1"] = dgn
    return dx, grads


def device_step(x, p, target, W, gather_src, layer1_weights, layer1_grads):
    W = dict(W)
    saved = []
    h = x
    for i in range(2):
        h, s1 = ffn_fwd(h, W[f"ffn1_norm{i}"], W[f"ffn1_w_gu{i}"], W[f"ffn1_w_down{i}"], f"ffn1_{i}")
        if i == 0:
            h, s2, gathered = even_mix_fwd(h, W, gather_src)
            W.update(layer1_weights(gathered))
        else:
            h, s2 = odd_mix_fwd(h, W)
        h, s3 = ffn_fwd(h, W[f"ffn2_norm{i}"], W[f"ffn2_w_gu{i}"], W[f"ffn2_w_down{i}"], f"ffn2_{i}")
        h, s4 = ple_fwd(h, p[i], W[f"ple_norm{i}"], W[f"ple_w_gate{i}"], W[f"ple_w_proj{i}"], f"ple_{i}")
        saved.append((s1, s2, s3, s4))
    dh, d_final, loss = loss_head(h, target, W["final_norm"])
    G = {"final_norm": d_final}
    for i in (1, 0):
        s1, s2, s3, s4 = saved[i]
        dh, G[f"ple_norm{i}"], G[f"ple_w_gate{i}"], G[f"ple_w_proj{i}"] = ple_bwd(
            dh, s4, p[i], W[f"ple_norm{i}"], W[f"ple_w_gate{i}"], f"ple_{i}")
        dh, G[f"ffn2_norm{i}"], G[f"ffn2_w_gu{i}"], G[f"ffn2_w_down{i}"] = ffn_bwd(
            dh, s3, W[f"ffn2_norm{i}"], W[f"ffn2_w_gu{i}"], W[f"ffn2_w_down{i}"], f"ffn2_{i}")
        if i == 0:
            dh, gm, exchanged = even_mix_bwd(dh, s2, W, layer1_grads(G))
        else:
            dh, gm = odd_mix_bwd(dh, s2, W)
        G.update(gm)
        dh, G[f"ffn1_norm{i}"], G[f"ffn1_w_gu{i}"], G[f"ffn1_w_down{i}"] = ffn_bwd(
            dh, s1, W[f"ffn1_norm{i}"], W[f"ffn1_w_gu{i}"], W[f"ffn1_w_down{i}"], f"ffn1_{i}")
    return loss, dh, G, exchanged


_MESH = pl.DeviceIdType.MESH
_ANY = pl.BlockSpec(memory_space=pl.ANY)


def _exchange_sems(n):
    return [pltpu.SemaphoreType.DMA((7 * n,)), pltpu.SemaphoreType.DMA((7 * n,)), pltpu.SemaphoreType.DMA((n,))]


def all_gather(xs, name):
    n = len(xs)

    def body(*refs):
        x_refs, out_refs = refs[:n], refs[n:2 * n]
        send_sems, recv_sems, local_sems = refs[2 * n:]
        x_, y_, c_ = lax.axis_index("x"), lax.axis_index("y"), lax.axis_index("c")
        me, sibling = (x_, y_, c_), (x_, y_, 1 - c_)
        chips = [(1 - x_, y_), (x_, 1 - y_), (1 - x_, 1 - y_)]

        def copy(b, k, block, to, from_input=False):
            slot = out_refs[b].at[4 * block[0] + 2 * block[1] + block[2]]
            return pltpu.make_async_remote_copy(
                src_ref=x_refs[b] if from_input else slot, dst_ref=slot,
                send_sem=send_sems.at[7 * b + k], recv_sem=recv_sems.at[7 * b + k], device_id=to,
                device_id_type=_MESH)

        bufs = range(n)
        mine = [pltpu.make_async_copy(x_refs[b], out_refs[b].at[4 * x_ + 2 * y_ + c_], local_sems.at[b]) for b in bufs]
        first = [copy(b, 0, me, sibling, True) for b in bufs]
        first += [copy(b, 1 + j, me, (*chip, c_), True) for j, chip in enumerate(chips) for b in bufs]
        for cp in mine + first:
            cp.start()
        passed = []
        for j, chip in enumerate(chips):
            for b in bufs:
                copy(b, 1 + j, (*chip, c_), me).wait_recv()
                passed.append(copy(b, 4 + j, (*chip, c_), sibling))
                passed[-1].start()
        for b in bufs:
            copy(b, 0, sibling, me).wait_recv()
            for j, chip in enumerate(chips):
                copy(b, 4 + j, (*chip, 1 - c_), me).wait_recv()
        for cp in first + passed:
            cp.wait_send()
        for cp in mine:
            cp.wait()

    return pl.pallas_call(
        body,
        out_shape=[jax.ShapeDtypeStruct((N_DEV,) + x.shape, x.dtype) for x in xs],
        in_specs=[_ANY] * n,
        out_specs=[_ANY] * n,
        scratch_shapes=_exchange_sems(n),
        name=name,
    )(*xs)


def _direct_exchange(gather, s_refs, r_refs, send_sems, recv_sems, local_sems):
    x_, y_, c_ = lax.axis_index("x"), lax.axis_index("y"), lax.axis_index("c")
    my = 4 * x_ + 2 * y_ + c_
    copies = []
    for b, (s_ref, r_ref) in enumerate(zip(s_refs, r_refs)):
        copies.append(pltpu.make_async_copy(s_ref if gather else s_ref.at[my], r_ref.at[my], local_sems.at[b]))
        for m in range(1, N_DEV):
            px = 1 - x_ if (m >> 2) & 1 else x_
            py = 1 - y_ if (m >> 1) & 1 else y_
            pc = 1 - c_ if m & 1 else c_
            copies.append(pltpu.make_async_remote_copy(
                src_ref=s_ref if gather else s_ref.at[4 * px + 2 * py + pc], dst_ref=r_ref.at[my],
                send_sem=send_sems.at[7 * b + m - 1], recv_sem=recv_sems.at[7 * b + m - 1],
                device_id=(px, py, pc), device_id_type=_MESH))
    return copies


def _exchange_during(step, n_steps, gather, s_refs, r_refs, send_sems, recv_sems, local_sems):
    copies = _direct_exchange(gather, s_refs, r_refs, send_sems, recv_sems, local_sems)

    @pl.when(step == 0)
    def _():
        for cp in copies:
            cp.start()

    @pl.when(step == n_steps - 1)
    def _():
        for cp in copies:
            cp.wait()


def _exchange_out_shapes(gather, srcs):
    return [jax.ShapeDtypeStruct(((N_DEV,) + s.shape) if gather else s.shape, s.dtype) for s in srcs]


def all_to_all(sends, name):
    n = len(sends)

    def body(*refs):
        copies = _direct_exchange(False, refs[:n], refs[n:2 * n], *refs[2 * n:])
        for cp in copies:
            cp.start()
        for cp in copies:
            cp.wait()

    return pl.pallas_call(
        body,
        out_shape=_exchange_out_shapes(False, sends),
        in_specs=[_ANY] * n,
        out_specs=[_ANY] * n,
        scratch_shapes=_exchange_sems(n),
        name=name,
    )(*sends)


def adamw(w, m, v, parts, name, tm=256):
    R, C = w.shape
    tm = _pick(R, tm, 8) if R >= 8 else R

    def body(w_ref, m_ref, v_ref, p_ref, g_ref, d_ref, nm_ref, nv_ref):
        g = p_ref[0].astype(F32)
        for s in range(1, N_DEV):
            g = g + p_ref[s].astype(F32)
        nm = ADAM_B1 * m_ref[...] + (1.0 - ADAM_B1) * g
        nv = ADAM_B2 * v_ref[...] + (1.0 - ADAM_B2) * (g * g)
        m_hat = nm / (1.0 - ADAM_B1 ** ADAM_STEP)
        v_hat = nv / (1.0 - ADAM_B2 ** ADAM_STEP)
        g_ref[...] = g
        d_ref[...] = -ADAM_LR * (m_hat / (jnp.sqrt(v_hat) + ADAM_EPS) + ADAM_WD * w_ref[...])
        nm_ref[...] = nm
        nv_ref[...] = nv

    row = pl.BlockSpec((tm, C), lambda i: (i, 0))
    out = jax.ShapeDtypeStruct((R, C), F32)
    return pl.pallas_call(
        body,
        grid=(R // tm,),
        in_specs=[row, row, row, pl.BlockSpec((N_DEV, tm, C), lambda i: (0, i, 0))],
        out_specs=[row] * 4,
        out_shape=[out] * 4,
        compiler_params=_cparams(("parallel",)),
        name=name,
    )(w, m, v, parts)


_WEIGHTS = ["ffn1_norm", "ffn1_w_gu", "ffn1_w_down", "mix_norm", "ffn2_norm", "ffn2_w_gu", "ffn2_w_down",
            "ple_norm", "ple_w_gate", "ple_w_proj", "even_w_in", "even_w_out", "swa_sinks", "rwkv_mu",
            "rwkv_w0", "rwkv_w2", "rwkv_a0", "rwkv_a2", "rwkv_g2", "rwkv_k_k", "rwkv_k_a", "rwkv_r_k",
            "rwkv_ln_w", "rwkv_ln_b", "fox_w_in", "fox_b_f", "fox_w_out", "final_norm"]
_SHARD_AXIS = {"ffn1_w_gu": 2, "ffn1_w_down": 1, "ffn2_w_gu": 2, "ffn2_w_down": 1, "ple_w_gate": 1,
               "ple_w_proj": 2, "even_w_in": 2, "even_w_out": 1, "rwkv_w2": 2, "rwkv_a2": 2, "rwkv_g2": 2,
               "fox_w_in": 2, "fox_w_out": 1}
_SHARDED = [n for n in _WEIGHTS if n in _SHARD_AXIS]
_REPLICATED = [n for n in _WEIGHTS if n not in _SHARD_AXIS]
_PER_LAYER = ("ffn1_w_gu", "ffn1_w_down", "ffn2_w_gu", "ffn2_w_down", "ple_w_gate", "ple_w_proj")
_PIECES = [[(n, 0) for n in _PER_LAYER] + [(n, 0) for n in ("even_w_in", "even_w_out", "rwkv_w2", "rwkv_a2", "rwkv_g2")],
           [(n, 1) for n in _PER_LAYER] + [("fox_w_in", 0), ("fox_w_out", 0)]]
_LATE_GRADS = [(n, 0) for n in ("ffn1_w_gu", "ffn1_w_down", "even_w_in", "even_w_out", "rwkv_w2", "rwkv_a2", "rwkv_g2")]
_GRAD_PIECES = [_LATE_GRADS, [pc for pc in _PIECES[0] if pc not in _LATE_GRADS] + _PIECES[1]]
_PACK_LANES = 1024
_PACK_ROW_TILE = 256


def _piece_key(piece):
    name, idx = piece
    return f"{name}{idx}" if name in _PER_LAYER else name


_KINDS = ("gu", "rows", "misc")


def _kind(piece):
    if piece[0] in ("ffn1_w_gu", "ffn2_w_gu"):
        return "gu"
    return "rows" if _SHARD_AXIS[piece[0]] == 1 else "misc"


def _of_kind(pieces, shapes, kind):
    return [(pc, shp) for pc, shp in zip(pieces, shapes) if _kind(pc) == kind]


def _pad_rows(flat, axis):
    pad = [(0, 0)] * flat.ndim
    pad[axis] = (0, -flat.shape[axis] % _PACK_ROW_TILE)
    return jnp.pad(flat, pad)


def _bundle(get, pieces, dtype):
    take = lambda kind: [get(pc).astype(dtype) for pc in pieces if _kind(pc) == kind]
    return [jnp.stack(take("gu")), jnp.concatenate(take("rows"), axis=0),
            _pad_rows(jnp.concatenate([a.reshape(-1, _PACK_LANES) for a in take("misc")], axis=0), 0)]


def _unbundle(bufs, pieces, shapes):
    out = {}
    gu = _of_kind(pieces, shapes, "gu")
    stacked = bufs[0].reshape((len(gu),) + gu[0][1])
    for j, (pc, _) in enumerate(gu):
        out[pc] = stacked[j]
    for buf, kind in ((bufs[1], "rows"), (bufs[2], "misc")):
        r0 = 0
        for pc, shp in _of_kind(pieces, shapes, kind):
            n = math.prod(shp) // _PACK_LANES
            out[pc] = buf[r0:r0 + n].reshape(shp)
            r0 += n
    return out


def _unshard(gathered, pieces, shapes):
    full = {}
    for j, (pc, shp) in enumerate(_of_kind(pieces, shapes, "gu")):
        full[_piece_key(pc)] = jnp.moveaxis(gathered[0][:, j], 0, 1).reshape(shp[0], N_DEV * shp[1])
    r0 = 0
    for pc, shp in _of_kind(pieces, shapes, "rows"):
        full[_piece_key(pc)] = gathered[1][:, r0:r0 + shp[0]].reshape(N_DEV * shp[0], shp[1])
        r0 += shp[0]
    r0 = 0
    for pc, shp in _of_kind(pieces, shapes, "misc"):
        n = math.prod(shp) // _PACK_LANES
        seg = gathered[2][:, r0:r0 + n].reshape((N_DEV,) + shp)
        full[_piece_key(pc)] = jnp.moveaxis(seg, 0, 1).reshape(shp[0], N_DEV * shp[1])
        r0 += n
    return full


def _to_shards(full, pieces, shapes):
    get = lambda pc: full[_piece_key(pc)].astype(BF16)
    cols = lambda pc, shp: jnp.moveaxis(get(pc).reshape(shp[0], N_DEV, shp[1]), 1, 0)
    gu = jnp.stack([cols(pc, shp) for pc, shp in _of_kind(pieces, shapes, "gu")], axis=1)
    rows = jnp.concatenate([get(pc).reshape((N_DEV,) + shp) for pc, shp in _of_kind(pieces, shapes, "rows")], axis=1)
    misc = jnp.concatenate([cols(pc, shp).reshape(N_DEV, -1, _PACK_LANES)
                            for pc, shp in _of_kind(pieces, shapes, "misc")], axis=1)
    return [gu, rows, _pad_rows(misc, 1)]


def _layer_weights(full):
    W = dict(full)
    if "fox_w_in" in W:
        W["fox_w_in"] = jnp.pad(W["fox_w_in"], ((0, 0), (0, FOX_IN_PAD - W["fox_w_in"].shape[1])))
    for n in ("rwkv_w2", "rwkv_a2", "rwkv_g2"):
        if n in W:
            W[n] = W[n].astype(F32)
    return W


def _pack_small(vals):
    flat = jnp.concatenate([v.reshape(1, -1) for v in vals], axis=1)
    n = flat.shape[1]
    return jnp.pad(flat, ((0, 0), (0, -n % 128)))


def _unpack_small(flat, shapes):
    out, c0 = [], 0
    for shp in shapes:
        n = math.prod(shp)
        out.append(flat[0, c0:c0 + n].reshape(shp))
        c0 += n
    return out


def kernel(x, p, ffn1_norm, ffn1_w_gu, ffn1_w_down, mix_norm, ffn2_norm, ffn2_w_gu, ffn2_w_down, ple_norm, ple_w_gate, ple_w_proj, even_w_in, even_w_out, swa_sinks, rwkv_mu, rwkv_w0, rwkv_w2, rwkv_a0, rwkv_a2, rwkv_g2, rwkv_k_k, rwkv_k_a, rwkv_r_k, rwkv_ln_w, rwkv_ln_b, fox_w_in, fox_b_f, fox_w_out, final_norm, loss_target, m_ffn1_norm, m_ffn1_w_gu, m_ffn1_w_down, m_mix_norm, m_ffn2_norm, m_ffn2_w_gu, m_ffn2_w_down, m_ple_norm, m_ple_w_gate, m_ple_w_proj, m_even_w_in, m_even_w_out, m_swa_sinks, m_rwkv_mu, m_rwkv_w0, m_rwkv_w2, m_rwkv_a0, m_rwkv_a2, m_rwkv_g2, m_rwkv_k_k, m_rwkv_k_a, m_rwkv_r_k, m_rwkv_ln_w, m_rwkv_ln_b, m_fox_w_in, m_fox_b_f, m_fox_w_out, m_final_norm, v_ffn1_norm, v_ffn1_w_gu, v_ffn1_w_down, v_mix_norm, v_ffn2_norm, v_ffn2_w_gu, v_ffn2_w_down, v_ple_norm, v_ple_w_gate, v_ple_w_proj, v_even_w_in, v_even_w_out, v_swa_sinks, v_rwkv_mu, v_rwkv_w0, v_rwkv_w2, v_rwkv_a0, v_rwkv_a2, v_rwkv_g2, v_rwkv_k_k, v_rwkv_k_a, v_rwkv_r_k, v_rwkv_ln_w, v_rwkv_ln_b, v_fox_w_in, v_fox_b_f, v_fox_w_out, v_final_norm):
    given = dict(locals())
    w = {n: given[n] for n in _WEIGHTS}
    m = {n: given["m_" + n] for n in _WEIGHTS}
    v = {n: given["v_" + n] for n in _WEIGHTS}
    small_shapes = [w[n].shape for n in _REPLICATED]
    piece = lambda d, pc: d[pc[0]][pc[1]]
    shapes = [[piece(w, pc).shape for pc in pieces] for pieces in _PIECES]
    gshapes = [[piece(w, pc).shape for pc in pieces] for pieces in _GRAD_PIECES]
    w_send = [_bundle(lambda pc: piece(w, pc), pieces, BF16) for pieces in _PIECES]

    W = _layer_weights(_unshard(all_gather(w_send[0], "weights_all_gather"), _PIECES[0], shapes[0]))
    for i in range(2):
        for n in ("ffn1_norm", "mix_norm", "ffn2_norm", "ple_norm"):
            W[f"{n}{i}"] = w[n][i].reshape(1, -1)
    for n in ("swa_sinks", "rwkv_mu", "rwkv_w0", "rwkv_a0", "rwkv_k_k", "rwkv_k_a", "rwkv_r_k", "rwkv_ln_w",
              "rwkv_ln_b", "final_norm"):
        W[n] = w[n].reshape(1, -1)
    n_f = fox_b_f.shape[1]
    W["fox_b_f"] = jnp.pad(fox_b_f.reshape(1, n_f), ((0, 0), (0, 128 - n_f)))
    n_fox = fox_w_in.shape[2] * N_DEV

    def layer1_weights(gathered):
        return _layer_weights(_unshard(gathered, _PIECES[1], shapes[1]))

    def early_grads(G):
        G = dict(G, fox_w_in=G["fox_w_in"][:, :n_fox])
        return _to_shards(G, _GRAD_PIECES[1], gshapes[1])

    loss_row, dx, G, parts_early = device_step(x[0], p[:, 0], loss_target[0], W, w_send[1], layer1_weights,
                                               early_grads)

    parts = [all_to_all(_to_shards(G, _GRAD_PIECES[0], gshapes[0]), "grads_all_to_all"), parts_early]
    out_g, out_d, out_m, out_v = {}, {}, {}, {}
    rows2d = lambda a, lead: a.reshape(a.shape[:lead] + (-1, a.shape[-1]))
    for li, pieces in enumerate(_GRAD_PIECES):
        wmv = [_bundle(lambda pc, d=d: piece(d, pc), pieces, F32) for d in (w, m, v)]
        res = [adamw(*[rows2d(b[ki], 0) for b in wmv], rows2d(parts[li][ki], 1), f"adamw_{kind}{li}")
               for ki, kind in enumerate(_KINDS)]
        for oi, out in enumerate((out_g, out_d, out_m, out_v)):
            for pc, a in _unbundle([r[oi] for r in res], pieces, gshapes[li]).items():
                out.setdefault(pc[0], {})[pc[1]] = a
    for out in (out_g, out_d, out_m, out_v):
        for n in _SHARDED:
            out[n] = jnp.stack([out[n][i] for i in sorted(out[n])])

    gsmall = {}
    for n in ("ffn1_norm", "mix_norm", "ffn2_norm", "ple_norm"):
        gsmall[n] = jnp.concatenate([G[f"{n}0"], G[f"{n}1"]], axis=0)
    for n in ("swa_sinks", "rwkv_mu", "rwkv_w0", "rwkv_a0", "rwkv_k_k", "rwkv_k_a", "rwkv_r_k", "rwkv_ln_w",
              "rwkv_ln_b", "final_norm"):
        gsmall[n] = G[n]
    gsmall["fox_b_f"] = G["fox_b_f"][:, :n_f]
    small = _pack_small([gsmall[n] for n in _REPLICATED] + [loss_row[:, :1]])
    small_parts = all_gather([small], "small_all_gather")[0]
    pad1 = lambda vals: _pack_small(vals + [jnp.zeros((1, 1), F32)])
    gs, ds, nms, nvs = adamw(pad1([w[n] for n in _REPLICATED]), pad1([m[n] for n in _REPLICATED]),
                             pad1([v[n] for n in _REPLICATED]), small_parts, "adamw_replicated")
    out_g.update(zip(_REPLICATED, _unpack_small(gs, small_shapes)))
    out_d.update(zip(_REPLICATED, _unpack_small(ds, small_shapes)))
    out_m.update(zip(_REPLICATED, _unpack_small(nms, small_shapes)))
    out_v.update(zip(_REPLICATED, _unpack_small(nvs, small_shapes)))
    n_small = sum(math.prod(s) for s in small_shapes)
    loss = gs[0, n_small]

    return (loss, dx[None], *[out_g[n] for n in _WEIGHTS], *[out_d[n] for n in _WEIGHTS],
            *[out_m[n] for n in _WEIGHTS], *[out_v[n] for n in _WEIGHTS])
```

```python
import functools
import math

import numpy as np
import jax
import jax.numpy as jnp
from jax import lax
from jax.experimental import pallas as pl
from jax.experimental.pallas import tpu as pltpu

F32 = jnp.float32
BF16 = jnp.bfloat16

D_MODEL = 1024
HEAD_DIM = 64
BLOCK = 128
SWA_HEADS = 8
SWA_KV_HEADS = 2
SWA_GROUP = 4
RWKV_HEADS = 8
RWKV_DIM = 512
FOX_HEADS = 16
FOX_DIM = 1024
D_FF = 2816
NORM_EPS = 1e-6
GN_EPS = 64e-5
L2_EPS = 1e-12
SWA_Q = 512
SWA_KV = 128
SWA_COLS = 768
FOX_IN_PAD = 3200
N_DEV = 8
ADAM_LR = 0.001
ADAM_B1 = 0.9
ADAM_B2 = 0.999
ADAM_EPS = 1e-08
ADAM_WD = 0.01
ADAM_STEP = 10

V7X_VMEM_LIMIT = 56 * 1024 * 1024
SCAN_GROUP = 8
SCAN_CHUNK = 32

_NN = (((1,), (0,)), ((), ()))
_NT = (((1,), (1,)), ((), ()))
_TN = (((0,), (0,)), ((), ()))
_DIMS = {"nn": _NN, "nt": _NT, "tn": _TN}


def _pick(n, target, mult=128):
    best = None
    for t in range(mult, min(n, target) + 1, mult):
        if n % t == 0:
            best = t
    return best or n


def _cparams(sem):
    return pltpu.CompilerParams(dimension_semantics=sem, vmem_limit_bytes=V7X_VMEM_LIMIT)


def _dot(a, b, dims):
    return lax.dot_general(a.astype(BF16), b.astype(BF16), dims, preferred_element_type=F32)


@jax.custom_vjp
def bdot(a, b):
    return _dot(a, b, _NN)


def _bdot_fwd(a, b):
    return _dot(a, b, _NN), (a, b)


def _bdot_bwd(res, g):
    a, b = res
    return _dot(g, b, _NT), _dot(a, g, _TN)


bdot.defvjp(_bdot_fwd, _bdot_bwd)


@jax.custom_vjp
def bdot_nt(a, b):
    return _dot(a, b, _NT)


def _bdot_nt_fwd(a, b):
    return _dot(a, b, _NT), (a, b)


def _bdot_nt_bwd(res, g):
    a, b = res
    return _dot(g, b, _NN), _dot(g, a, _TN)


bdot_nt.defvjp(_bdot_nt_fwd, _bdot_nt_bwd)


@jax.custom_vjp
def _segsum(x, bd):
    return _dot2(x, bd.astype(BF16))


def _segsum_fwd(x, bd):
    return _segsum(x, bd), bd


def _segsum_bwd(bd, g):
    return _dot2(g, bd.astype(BF16)), jnp.zeros_like(bd)


_segsum.defvjp(_segsum_fwd, _segsum_bwd)


def _sigmoid(x):
    return 1.0 / (1.0 + jnp.exp(-x))


def _sigmoid_tanh(x):
    return 0.5 * jnp.tanh(0.5 * x) + 0.5


def _softplus(x):
    return jnp.maximum(x, 0.0) + jnp.log(1.0 + jnp.exp(-jnp.abs(x)))


def matmul(a, b, mode, name, out_dtype=F32, scale=1.0, res=None, tm=512, tn=1408, tk=1024):
    if mode == "nn":
        (M, K), (K2, N) = a.shape, b.shape
    elif mode == "nt":
        (M, K), (N, K2) = a.shape, b.shape
    else:
        (K, M), (K2, N) = a.shape, b.shape
    assert K == K2, (a.shape, b.shape, mode)
    tm, tn, tk = _pick(M, tm), _pick(N, tn), _pick(K, tk)
    nk = K // tk
    has_res = res is not None

    def body(*refs):
        if has_res:
            a_ref, b_ref, r_ref, o_ref, acc = refs
        else:
            a_ref, b_ref, o_ref, acc = refs
        kk = pl.program_id(2)

        @pl.when(kk == 0)
        def _():
            acc[...] = jnp.zeros_like(acc)

        acc[...] += _dot(a_ref[...], b_ref[...], _DIMS[mode])

        @pl.when(kk == nk - 1)
        def _():
            v = acc[...]
            if scale != 1.0:
                v = v * scale
            if has_res:
                v = v + r_ref[...].astype(F32)
            o_ref[...] = v.astype(out_dtype)

    if mode == "tn":
        a_spec = pl.BlockSpec((tk, tm), lambda i, j, k: (k, i))
    else:
        a_spec = pl.BlockSpec((tm, tk), lambda i, j, k: (i, k))
    if mode == "nt":
        b_spec = pl.BlockSpec((tn, tk), lambda i, j, k: (j, k))
    else:
        b_spec = pl.BlockSpec((tk, tn), lambda i, j, k: (k, j))
    o_spec = pl.BlockSpec((tm, tn), lambda i, j, k: (i, j))
    in_specs = [a_spec, b_spec] + ([o_spec] if has_res else [])
    args = (a, b) + ((res,) if has_res else ())
    return pl.pallas_call(
        body,
        grid=(M // tm, N // tn, nk),
        in_specs=in_specs,
        out_specs=o_spec,
        out_shape=jax.ShapeDtypeStruct((M, N), out_dtype),
        scratch_shapes=[pltpu.VMEM((tm, tn), F32)],
        compiler_params=_cparams(("parallel", "parallel", "arbitrary")),
        name=name,
    )(*args)


def _row_spec(r, tm):
    if isinstance(r, tuple):
        arr, width, blk = r
        return arr, pl.BlockSpec((tm, width), lambda i, blk=blk: (i, blk))
    return r, pl.BlockSpec((tm, r.shape[1]), lambda i: (i, 0))


def _whole_spec(p):
    return pl.BlockSpec(p.shape, lambda i: (0,) * p.ndim)


def rowwise(fn, rows, params, outs, name, tm=256):
    arrs, specs = zip(*[_row_spec(r, tm) for r in rows])
    S = arrs[0].shape[0]
    tm = min(tm, S)
    arrs, specs = zip(*[_row_spec(r, tm) for r in rows])
    n_in = len(rows) + len(params)

    def body(*refs):
        res = fn(*[r[...] for r in refs[:n_in]])
        for o_ref, v in zip(refs[n_in:], res):
            o_ref[...] = v.astype(o_ref.dtype)

    return pl.pallas_call(
        body,
        grid=(S // tm,),
        in_specs=list(specs) + [_whole_spec(p) for p in params],
        out_specs=[pl.BlockSpec((tm, c), lambda i: (i, 0)) for c, _ in outs],
        out_shape=[jax.ShapeDtypeStruct((S, c), dt) for c, dt in outs],
        compiler_params=_cparams(("parallel",)),
        name=name,
    )(*arrs, *params)


def rowwise_vjp(fn, rows, params, cots, name, need=None, row_dtype=F32, consts=(), tm=256):
    nr, npar, nc, nk = len(rows), len(params), len(cots), len(consts)
    need = [True] * nr if need is None else need
    arrs, _ = zip(*[_row_spec(r, tm) for r in rows])
    S = arrs[0].shape[0]
    tm = min(tm, S)
    arrs, specs = zip(*[_row_spec(r, tm) for r in rows])
    carrs, cspecs = zip(*[_row_spec(c, tm) for c in cots])
    widths = [s.block_shape[1] for s in specs]
    n_in = nr + npar + nk + nc

    def body(*refs):
        i = pl.program_id(0)
        xs = [r[...].astype(F32) for r in refs[:nr]]
        ps = [r[...] for r in refs[nr:nr + npar]]
        ks = [r[...] for r in refs[nr + npar:nr + npar + nk]]
        cs = [r[...].astype(F32) for r in refs[nr + npar + nk:n_in]]
        outs, vjp = jax.vjp(lambda *a: fn(*a, *ks), *xs, *ps)
        grads = vjp(tuple(cs))
        o = n_in
        for j in range(nr):
            if need[j]:
                refs[o][...] = grads[j].astype(refs[o].dtype)
                o += 1
        for j in range(npar):
            g_ref = refs[o + j]

            @pl.when(i == 0)
            def _(g_ref=g_ref):
                g_ref[...] = jnp.zeros_like(g_ref)

            g_ref[...] += grads[nr + j]

    out_specs = [pl.BlockSpec((tm, w), lambda i: (i, 0)) for w, nd in zip(widths, need) if nd]
    out_shape = [jax.ShapeDtypeStruct((S, w), row_dtype) for w, nd in zip(widths, need) if nd]
    out_specs += [_whole_spec(p) for p in params]
    out_shape += [jax.ShapeDtypeStruct(p.shape, F32) for p in params]
    res = pl.pallas_call(
        body,
        grid=(S // tm,),
        in_specs=list(specs) + [_whole_spec(p) for p in params] + [_whole_spec(k) for k in consts] + list(cspecs),
        out_specs=out_specs,
        out_shape=out_shape,
        compiler_params=_cparams(("arbitrary",)),
        name=name,
    )(*arrs, *params, *consts, *carrs)
    nrow = sum(need)
    return list(res[:nrow]), list(res[nrow:])


def _rms(x, g):
    return x * lax.rsqrt(jnp.mean(x * x, axis=-1, keepdims=True) + NORM_EPS) * g


def _f_rms(x, g):
    return (_rms(x, g),)


def _f_rms_res(x, g):
    return _rms(x, g), x


def _f_ple(x, z, pp):
    return (x + _sigmoid(z) * pp,)


def _f_mix(h, sh, mu):
    return (h + (sh - h) * mu,)


def _f_logf(fz, bf):
    return (-_softplus(-(fz + bf)),)


def _f_rwkv_pre(hk, hw, ha, hg, w0, w2, a0, a2, g2, k_k, k_a, bd):
    wlog = -_softplus(-(w0 + bdot(jnp.tanh(hw), w2))) - 0.5
    a = _sigmoid(a0 + bdot(ha, a2))
    g = bdot(_sigmoid(hg), g2)
    kk = hk * k_k
    kk = kk / jnp.maximum(jnp.sqrt(_segsum(kk * kk, bd)), L2_EPS)
    k2 = hk * (1.0 + (a - 1.0) * k_a)
    decay = jnp.exp(-jnp.exp(wlog))
    return decay, k2, kk, kk * a, g


def _f_rwkv_post(y, r, k2, v, g, ln_w, ln_b, r_k, bd):
    mean = _segsum(y, bd) * (1.0 / HEAD_DIM)
    d = y - mean
    var = _segsum(d * d, bd) * (1.0 / HEAD_DIM)
    yn = d * lax.rsqrt(var + GN_EPS) * ln_w + ln_b
    yn = yn + _segsum(r * k2 * r_k, bd) * v
    return (yn * g,)


def loss_head(x, target, gf, tm=256):
    S, D = x.shape
    tm = min(tm, S)

    def f(xt, g, tt):
        err = _rms(xt, g) - tt
        return 0.5 * jnp.sum(err * err) * (1.0 / D)

    def body(x_ref, t_ref, g_ref, dx_ref, dg_ref, l_ref):
        i = pl.program_id(0)
        val, (dx, dg) = jax.value_and_grad(f, argnums=(0, 1))(x_ref[...], g_ref[...], t_ref[...])

        @pl.when(i == 0)
        def _():
            dg_ref[...] = jnp.zeros_like(dg_ref)
            l_ref[...] = jnp.zeros_like(l_ref)

        dx_ref[...] = dx
        dg_ref[...] += dg
        l_ref[...] += jnp.full(l_ref.shape, val, F32)

    row = pl.BlockSpec((tm, D), lambda i: (i, 0))
    vec = pl.BlockSpec((1, D), lambda i: (0, 0))
    return pl.pallas_call(
        body,
        grid=(S // tm,),
        in_specs=[row, row, vec],
        out_specs=[row, vec, pl.BlockSpec((1, 128), lambda i: (0, 0))],
        out_shape=[jax.ShapeDtypeStruct((S, D), F32), jax.ShapeDtypeStruct((1, D), F32),
                   jax.ShapeDtypeStruct((1, 128), F32)],
        compiler_params=_cparams(("arbitrary",)),
        name="loss_head",
    )(x, target, gf)


def _swa_block(q, kp, kc, vp, vc, sink, slope, n):
    k = jnp.concatenate([kp, kc], axis=0)
    v = jnp.concatenate([vp, vc], axis=0)
    rows = q.shape[0]
    logits = bdot_nt(q, k) * (HEAD_DIM ** -0.5)
    qi = lax.broadcasted_iota(jnp.int32, (rows, 2 * BLOCK), 0) & (BLOCK - 1)
    ki = lax.broadcasted_iota(jnp.int32, (rows, 2 * BLOCK), 1)
    dist = qi + BLOCK - ki
    valid = (dist >= 0) & (dist < BLOCK) & ((n - 1) * BLOCK + ki >= 0)
    logits = logits - slope * dist.astype(F32)
    logits = jnp.where(valid, logits, -jnp.inf)
    m = jnp.maximum(jnp.max(logits, axis=-1, keepdims=True), sink)
    pr = jnp.exp(logits - m)
    denom = jnp.sum(pr, axis=-1, keepdims=True) + jnp.exp(sink - m)
    return bdot(pr / denom, v)


def _swa_specs(S):
    nb = S // BLOCK
    q_spec = pl.BlockSpec((None, SWA_GROUP, BLOCK, HEAD_DIM), lambda h, n: (h, 0, n, 0))
    kc_spec = pl.BlockSpec((None, BLOCK, HEAD_DIM), lambda h, n: (h, n, 0))
    kp_spec = pl.BlockSpec((None, BLOCK, HEAD_DIM), lambda h, n: (h, jnp.maximum(n - 1, 0), 0))
    col_spec = pl.BlockSpec((None, SWA_GROUP * BLOCK, 1), lambda h, n: (h, 0, 0))
    return nb, q_spec, kp_spec, kc_spec, col_spec


def swa_fwd(q, k, v, sink_col, slope_col):
    S = q.shape[2]
    nb, q_spec, kp_spec, kc_spec, col_spec = _swa_specs(S)

    def body(q_ref, kp_ref, kc_ref, vp_ref, vc_ref, s_ref, a_ref, o_ref):
        n = pl.program_id(1)
        qq = q_ref[...].reshape(SWA_GROUP * BLOCK, HEAD_DIM)
        out = _swa_block(qq, kp_ref[...], kc_ref[...], vp_ref[...], vc_ref[...], s_ref[...], a_ref[...], n)
        o_ref[...] = out.reshape(SWA_GROUP, BLOCK, HEAD_DIM)

    return pl.pallas_call(
        body,
        grid=(SWA_KV_HEADS, nb),
        in_specs=[q_spec, kp_spec, kc_spec, kp_spec, kc_spec, col_spec, col_spec],
        out_specs=q_spec,
        out_shape=jax.ShapeDtypeStruct(q.shape, F32),
        compiler_params=_cparams(("parallel", "parallel")),
        name="swa_fwd",
    )(q, k, k, v, v, sink_col, slope_col)


def swa_bwd(q, k, v, sink_col, slope_col, dout):
    S = q.shape[2]
    nb, q_spec, kp_spec, kc_spec, col_spec = _swa_specs(S)

    def body(q_ref, kp_ref, kc_ref, vp_ref, vc_ref, s_ref, a_ref, do_ref,
             dq_ref, dkp_ref, dkc_ref, dvp_ref, dvc_ref, ds_ref):
        n = pl.program_id(1)
        qq = q_ref[...].reshape(SWA_GROUP * BLOCK, HEAD_DIM)
        slope = a_ref[...]
        f = lambda a, b, c, d, e, s: _swa_block(a, b, c, d, e, s, slope, n)
        _, vjp = jax.vjp(f, qq, kp_ref[...], kc_ref[...], vp_ref[...], vc_ref[...], s_ref[...])
        dq, dkp, dkc, dvp, dvc, ds = vjp(do_ref[...].reshape(SWA_GROUP * BLOCK, HEAD_DIM))
        dq_ref[...] = dq.reshape(SWA_GROUP, BLOCK, HEAD_DIM)
        dkp_ref[...] = dkp
        dkc_ref[...] = dkc
        dvp_ref[...] = dvp
        dvc_ref[...] = dvc

        @pl.when(n == 0)
        def _():
            ds_ref[...] = jnp.zeros_like(ds_ref)

        ds_ref[...] += ds

    kv_shape = jax.ShapeDtypeStruct(k.shape, F32)
    return pl.pallas_call(
        body,
        grid=(SWA_KV_HEADS, nb),
        in_specs=[q_spec, kp_spec, kc_spec, kp_spec, kc_spec, col_spec, col_spec, q_spec],
        out_specs=[q_spec, kc_spec, kc_spec, kc_spec, kc_spec, col_spec],
        out_shape=[jax.ShapeDtypeStruct(q.shape, F32), kv_shape, kv_shape, kv_shape, kv_shape,
                   jax.ShapeDtypeStruct(sink_col.shape, F32)],
        compiler_params=_cparams(("parallel", "arbitrary")),
        name="swa_bwd",
    )(q, k, k, v, v, sink_col, slope_col, dout)


def _split2(x):
    hi = x.astype(BF16)
    return (x - hi.astype(F32)).astype(BF16), hi


def _dot2_many(xs, m):
    rows = xs[0].shape[0]
    res = jnp.dot(jnp.concatenate([p for x in xs for p in _split2(x)], axis=0), m, preferred_element_type=F32)
    return [res[(2 * i) * rows:(2 * i + 1) * rows] + res[(2 * i + 1) * rows:(2 * i + 2) * rows]
            for i in range(len(xs))]


def _dot2(x, m):
    return _dot2_many([x], m)[0]


def _seg_sums(xs, bd):
    w = bd.shape[0]
    halves = _dot2_many([x[:, i:i + w] for x in xs for i in range(0, x.shape[1], w)], bd)
    n = xs[0].shape[1] // w
    return [jnp.concatenate(halves[i * n:(i + 1) * n], axis=1) for i in range(len(xs))]


def _seg_sum(x, bd):
    return _seg_sums([x], bd)[0]


def _scan_consts():
    r = np.arange(256)
    bd = (r[:, None] // HEAD_DIM == r[None, :] // HEAD_DIM).astype(np.float32)
    c = np.arange(RWKV_DIM)
    e = (np.arange(HEAD_DIM)[:, None] // SCAN_GROUP == c[None, :] // HEAD_DIM).astype(np.float32)
    diag = (np.arange(HEAD_DIM)[:, None] == c[None, :] % HEAD_DIM).astype(np.float32)
    return jnp.asarray(bd, BF16), jnp.asarray(e, BF16), jnp.asarray(diag, F32)


def _to_colblocks(a):
    S = a.shape[0]
    a = a.reshape(S // SCAN_GROUP, SCAN_GROUP, RWKV_HEADS, HEAD_DIM)
    return a.transpose(0, 3, 2, 1).reshape(S // SCAN_GROUP, HEAD_DIM, RWKV_HEADS * SCAN_GROUP)


def _roll_up(rows):
    return pltpu.roll(rows, rows.shape[0] - 1, 0)


def _scan_pair_rows(aux, base, kk_ref, w_ref, b_ref, k_ref, bd):
    G = SCAN_GROUP
    kk_nx = _roll_up(kk_ref[pl.ds(base, G), :])
    aux[0] = w_ref[pl.ds(base, G), :] * kk_nx
    aux[1], aux[2] = _seg_sums([b_ref[pl.ds(base, G), :] * kk_nx, k_ref[pl.ds(base, G), :] * kk_nx], bd)


def _scan_pair(St, t0, base, col_g, lane_t, aux, kk_ref, w_ref, b_ref, k_ref, bd, e):
    t1 = t0 + 1
    row = lambda ref, t: ref[pl.ds(base + t, 1), :]
    arow = lambda i: aux[i, pl.ds(t0, 1), :]
    u0, m1 = _seg_sums([St * row(kk_ref, t0), St * arow(0)], bd)
    v0, v1 = _dot2_many([jnp.where(lane_t == t0, col_g, 0.0), jnp.where(lane_t == t1, col_g, 0.0)], e)
    u1 = m1 - u0 * arow(1) + v0 * arow(2)
    S0 = St * row(w_ref, t0) - u0 * row(b_ref, t0) + v0 * row(k_ref, t0)
    S1 = S0 * row(w_ref, t1) - u1 * row(b_ref, t1) + v1 * row(k_ref, t1)
    return (S0, S1), (u0, u1), (v0, v1)


def rwkv_scan_fwd(r, w, k, kk, b, vB, gather_srcs):
    S, C = r.shape
    N, G = HEAD_DIM, SCAN_GROUP
    chunk = min(SCAN_CHUNK, S)
    nchunk, ng = S // chunk, chunk // G
    bd, e, diag = _scan_consts()

    nx = len(gather_srcs)

    def body(*refs):
        r_ref, w_ref, k_ref, kk_ref, b_ref, vB_ref, bd_ref, e_ref, dg_ref = refs[:9]
        y_ref, ck_ref = refs[9 + nx:11 + nx]
        S_ref, aux, send_sems, recv_sems, local_sems = refs[11 + 2 * nx:]
        c = pl.program_id(0)
        _exchange_during(c, nchunk, True, refs[9:9 + nx], refs[11 + nx:11 + 2 * nx], send_sems, recv_sems, local_sems)

        @pl.when(c == 0)
        def _():
            S_ref[...] = jnp.zeros_like(S_ref)

        ck_ref[...] = S_ref[...]
        sub = lax.broadcasted_iota(jnp.int32, (G, C), 0)
        lane_t = lax.broadcasted_iota(jnp.int32, (N, N), 1) & (G - 1)

        def group(g, St):
            base = pl.multiple_of(g * G, G)
            vb = vB_ref[g]
            _scan_pair_rows(aux, base, kk_ref, w_ref, b_ref, k_ref, bd_ref[...])
            ys = jnp.zeros((G, C), F32)
            def emit(ys, states, t0):
                steps = (t0, t0 + 1)
                y_bs = _seg_sums([S_t * r_ref[pl.ds(base + tt, 1), :] for S_t, tt in zip(states, steps)], bd_ref[...])
                for y_b, tt in zip(y_bs, steps):
                    ys = jnp.where(sub == tt, jnp.sum(y_b * dg_ref[...], axis=0, keepdims=True), ys)
                return ys

            pending = None
            for t0 in range(0, G, 2):
                states, _, _ = _scan_pair(St, t0, base, vb, lane_t, aux, kk_ref, w_ref, b_ref, k_ref, bd_ref[...],
                                          e_ref[...])
                if pending is not None:
                    ys = emit(ys, *pending)
                pending = (states, t0)
                St = states[1]
            y_ref[pl.ds(base, G), :] = emit(ys, *pending)
            return St

        S_ref[...] = lax.fori_loop(0, ng, group, S_ref[...])

    row = pl.BlockSpec((chunk, C), lambda c: (c, 0))
    col = pl.BlockSpec((ng, N, N), lambda c: (c, 0, 0))
    res = pl.pallas_call(
        body,
        grid=(nchunk,),
        in_specs=[row] * 5 + [col, _whole_spec(bd), _whole_spec(e), _whole_spec(diag)] + [_ANY] * nx,
        out_specs=[row, pl.BlockSpec((None, N, C), lambda c: (c, 0, 0))] + [_ANY] * nx,
        out_shape=[jax.ShapeDtypeStruct((S, C), F32), jax.ShapeDtypeStruct((nchunk, N, C), F32)]
        + _exchange_out_shapes(True, gather_srcs),
        scratch_shapes=[pltpu.VMEM((N, C), F32), pltpu.VMEM((3, G, C), F32)] + _exchange_sems(nx),
        compiler_params=_cparams(("arbitrary",)),
        name="rwkv_scan_fwd",
    )(r, w, k, kk, b, vB, bd, e, diag, *gather_srcs)
    return res[0], res[1], list(res[2:])


def rwkv_scan_bwd(r, w, k, kk, b, vB, dyB, ckpt, scatter_srcs):
    S, C = r.shape
    N, G = HEAD_DIM, SCAN_GROUP
    chunk = min(SCAN_CHUNK, S)
    nchunk, ng = S // chunk, chunk // G
    bd, e, diag = _scan_consts()

    nx = len(scatter_srcs)

    def body(*refs):
        r_ref, w_ref, k_ref, kk_ref, b_ref, vB_ref, dyB_ref, ck_ref, bd_ref, e_ref, dg_ref = refs[:11]
        dr_ref, dw_ref, dk_ref, dkk_ref, db_ref, dv_ref = refs[11 + nx:17 + nx]
        G_ref, sbuf, ubuf, vbuf, aux, send_sems, recv_sems, local_sems = refs[17 + 2 * nx:]
        c = pl.program_id(0)
        _exchange_during(c, nchunk, False, refs[11:11 + nx], refs[17 + nx:17 + 2 * nx], send_sems, recv_sems,
                         local_sems)

        @pl.when(c == 0)
        def _():
            G_ref[...] = jnp.zeros_like(G_ref)

        lane_t = lax.broadcasted_iota(jnp.int32, (N, N), 1) & (G - 1)
        sub = lax.broadcasted_iota(jnp.int32, (G, C), 0)

        def fgroup(g, St):
            base = pl.multiple_of(g * G, G)
            vb = vB_ref[g]
            _scan_pair_rows(aux, base, kk_ref, w_ref, b_ref, k_ref, bd_ref[...])
            for t0 in range(0, G, 2):
                states, us, vs = _scan_pair(St, t0, base, vb, lane_t, aux, kk_ref, w_ref, b_ref, k_ref, bd_ref[...],
                                            e_ref[...])
                for i, S_before in enumerate((St, states[0])):
                    sbuf[base + t0 + i] = S_before
                    ubuf[base + t0 + i] = us[i]
                    vbuf[base + t0 + i] = vs[i]
                St = states[1]
            return St

        sbuf[chunk] = lax.fori_loop(0, ng, fgroup, ck_ref[...])

        def bgroup(gi, Gt):
            g = ng - 1 - gi
            base = pl.multiple_of(g * G, G)
            dyb = dyB_ref[g]
            rows = [jnp.zeros((G, C), F32) for _ in range(6)]
            colsum = lambda a: jnp.sum(a, axis=0, keepdims=True)
            row = lambda ref, t: ref[pl.ds(base + t, 1), :]
            b8 = b_ref[pl.ds(base, G), :]
            aux[0] = _roll_up(w_ref[pl.ds(base, G), :]) * b8
            aux[1], aux[2] = _seg_sums([_roll_up(kk_ref[pl.ds(base, G), :]) * b8, r_ref[pl.ds(base, G), :] * b8],
                                       bd_ref[...])

            def emit(rows, steps):
                d_vs = _seg_sums([Gt_ * row(k_ref, tt) for tt, Gt_, _, _ in steps], bd_ref[...])
                for (tt, Gt_, du_b, dy_b), d_vb in zip(steps, d_vs):
                    Sp, Sc = sbuf[base + tt], sbuf[base + tt + 1]
                    new = (colsum(Sc * dy_b), colsum(Gt_ * Sp), colsum(Gt_ * vbuf[base + tt]), colsum(Sp * du_b),
                           -colsum(Gt_ * ubuf[base + tt]), colsum(d_vb * dg_ref[...]))
                    rows = [jnp.where(sub == tt, n_, acc) for n_, acc in zip(new, rows)]
                return rows

            pending = None
            for t0 in reversed(range(0, G, 2)):
                t1 = t0 + 1
                arow = lambda i: aux[i, pl.ds(t0, 1), :]
                dy1, dy0 = _dot2_many([jnp.where(lane_t == t1, dyb, 0.0), jnp.where(lane_t == t0, dyb, 0.0)],
                                      e_ref[...])
                G1 = Gt + dy1 * row(r_ref, t1)
                m1, m2 = _seg_sums([G1 * row(b_ref, t1), G1 * arow(0)], bd_ref[...])
                du1 = -m1
                du0 = -(m2 + du1 * arow(1) + dy0 * arow(2))
                G0 = G1 * row(w_ref, t1) + du1 * row(kk_ref, t1) + dy0 * row(r_ref, t0)
                G_next = G0 * row(w_ref, t0) + du0 * row(kk_ref, t0)
                if pending is not None:
                    rows = emit(rows, pending)
                pending = ((t1, G1, du1, dy1), (t0, G0, du0, dy0))
                Gt = G_next
            rows = emit(rows, pending)
            for ref, val in zip((dr_ref, dw_ref, dk_ref, dkk_ref, db_ref, dv_ref), rows):
                ref[pl.ds(base, G), :] = val
            return Gt

        G_ref[...] = lax.fori_loop(0, ng, bgroup, G_ref[...])

    rev = lambda c: nchunk - 1 - c
    row = pl.BlockSpec((chunk, C), lambda c: (rev(c), 0))
    col = pl.BlockSpec((ng, N, N), lambda c: (rev(c), 0, 0))
    rshape = jax.ShapeDtypeStruct((S, C), F32)
    res = pl.pallas_call(
        body,
        grid=(nchunk,),
        in_specs=[row] * 5 + [col, col, pl.BlockSpec((None, N, C), lambda c: (rev(c), 0, 0)),
                              _whole_spec(bd), _whole_spec(e), _whole_spec(diag)] + [_ANY] * nx,
        out_specs=[row] * 6 + [_ANY] * nx,
        out_shape=[rshape] * 6 + _exchange_out_shapes(False, scatter_srcs),
        scratch_shapes=[pltpu.VMEM((N, C), F32), pltpu.VMEM((chunk + 1, N, C), F32),
                        pltpu.VMEM((chunk, N, C), F32), pltpu.VMEM((chunk, N, C), F32),
                        pltpu.VMEM((3, G, C), F32)] + _exchange_sems(nx),
        compiler_params=_cparams(("arbitrary",)),
        name="rwkv_scan_bwd",
    )(r, w, k, kk, b, vB, dyB, ckpt, bd, e, diag, *scatter_srcs)
    return tuple(res[:6]) + (list(res[6:]),)


def seq_cumsum(x, reverse, name):
    S, C = x.shape
    tb = min(256, S)
    nb = S // tb

    def body(x_ref, o_ref, carry):
        i = pl.program_id(0)

        @pl.when(i == 0)
        def _():
            carry[...] = jnp.zeros_like(carry)

        ri = lax.broadcasted_iota(jnp.int32, (tb, tb), 0)
        ci = lax.broadcasted_iota(jnp.int32, (tb, tb), 1)
        tri = jnp.where((ci >= ri) if reverse else (ci <= ri), 1.0, 0.0).astype(F32)
        xb = x_ref[...]
        out = jnp.dot(tri, xb, precision=lax.Precision.HIGHEST, preferred_element_type=F32) + carry[...]
        o_ref[...] = out
        carry[...] = carry[...] + jnp.sum(xb, axis=0, keepdims=True)

    idx = (lambda i: (nb - 1 - i, 0)) if reverse else (lambda i: (i, 0))
    return pl.pallas_call(
        body,
        grid=(nb,),
        in_specs=[pl.BlockSpec((tb, C), idx)],
        out_specs=pl.BlockSpec((tb, C), idx),
        out_shape=jax.ShapeDtypeStruct((S, C), F32),
        scratch_shapes=[pltpu.VMEM((1, C), F32)],
        compiler_params=_cparams(("arbitrary",)),
        name=name,
    )(x)


FOX_STRIP = 512


def _fox_logits(q, k, cq, ck, row0, col0):
    s = _dot(q, k, _NT) * (HEAD_DIM ** -0.5) + cq - ck
    row = row0 + lax.broadcasted_iota(jnp.int32, s.shape, 0)
    col = col0 + lax.broadcasted_iota(jnp.int32, s.shape, 1)
    return jnp.where(col <= row, s, -jnp.inf)


def _fox_strips(tq):
    st = min(FOX_STRIP, tq)
    return [(r * st, slice(r * st, (r + 1) * st)) for r in range(tq // st)]


def _fox_tiles(n, by_query):
    pairs = [(i, j) for i in range(n) for j in range(i + 1)] if by_query else \
            [(i, j) for j in range(n) for i in range(j, n)]
    return (jnp.asarray(np.array([p[0] for p in pairs], np.int32)),
            jnp.asarray(np.array([p[1] for p in pairs], np.int32)))


def _fox_specs(t, Dh):
    qs = pl.BlockSpec((None, t, Dh), lambda h, s, qt, kt: (h, qt[s], 0))
    ks = pl.BlockSpec((None, t, Dh), lambda h, s, qt, kt: (h, kt[s], 0))
    cqs = pl.BlockSpec((None, t, 1), lambda h, s, qt, kt: (h, qt[s], 0))
    cks = pl.BlockSpec((None, 1, t), lambda h, s, qt, kt: (h, 0, kt[s]))
    return qs, ks, cqs, cks


def _fox_call(body, tiles, Hh, in_specs, out_specs, out_shape, scratch, name, args):
    spec = pltpu.PrefetchScalarGridSpec(num_scalar_prefetch=2, grid=(Hh, tiles[0].shape[0]), in_specs=in_specs,
                                        out_specs=out_specs, scratch_shapes=scratch)
    return pl.pallas_call(body, grid_spec=spec, out_shape=out_shape,
                          compiler_params=_cparams(("parallel", "arbitrary")), name=name)(*tiles, *args)


def fox_fwd(q, k, v, c_col, c_row):
    Hh, S, Dh = q.shape
    tq = tk = min(512, S)

    def body(qt_ref, kt_ref, q_ref, k_ref, v_ref, cq_ref, ck_ref, o_ref, lse_ref, m_s, l_s, acc_s):
        qi, ki = qt_ref[pl.program_id(1)], kt_ref[pl.program_id(1)]

        @pl.when(ki == 0)
        def _():
            m_s[...] = jnp.full_like(m_s, -jnp.inf)
            l_s[...] = jnp.zeros_like(l_s)
            acc_s[...] = jnp.zeros_like(acc_s)

        kb, vb, ck = k_ref[...], v_ref[...], ck_ref[...]
        for r0, rs in _fox_strips(tq):
            s = _fox_logits(q_ref[rs, :], kb, cq_ref[rs, :], ck, qi * tq + r0, ki * tk)
            m_old = m_s[rs, :]
            m_new = jnp.maximum(m_old, jnp.max(s, axis=-1, keepdims=True))
            alpha = jnp.exp(m_old - m_new)
            p = jnp.exp(s - m_new)
            l_s[rs, :] = alpha * l_s[rs, :] + jnp.sum(p, axis=-1, keepdims=True)
            acc_s[rs, :] = alpha * acc_s[rs, :] + _dot(p, vb, _NN)
            m_s[rs, :] = m_new

        @pl.when(ki == qi)
        def _():
            o_ref[...] = acc_s[...] / l_s[...]
            lse_ref[...] = m_s[...] + jnp.log(l_s[...])

    qs, ks, cqs, cks = _fox_specs(tq, Dh)
    return _fox_call(
        body, _fox_tiles(S // tq, True), Hh, [qs, ks, ks, cqs, cks], [qs, cqs],
        [jax.ShapeDtypeStruct((Hh, S, Dh), F32), jax.ShapeDtypeStruct((Hh, S, 1), F32)],
        [pltpu.VMEM((tq, 1), F32), pltpu.VMEM((tq, 1), F32), pltpu.VMEM((tq, Dh), F32)],
        "fox_fwd", (q, k, v, c_col, c_row))


def fox_bwd(q, k, v, c_col, c_row, o, lse, do):
    Hh, S, Dh = q.shape
    tq = tk = min(512, S)
    nk = S // tk

    def body(qt_ref, kt_ref, q_ref, k_ref, v_ref, cq_ref, ck_ref, o_ref, lse_ref, do_ref,
             dq_ref, dr_ref, dk_ref, dv_ref, dc_ref, acc_s, row_s):
        step = pl.program_id(1)
        qi, ki = qt_ref[step], kt_ref[step]

        @pl.when(step == 0)
        def _():
            dk_ref[...] = jnp.zeros_like(dk_ref)
            dv_ref[...] = jnp.zeros_like(dv_ref)
            dc_ref[...] = jnp.zeros_like(dc_ref)

        @pl.when(ki == 0)
        def _():
            acc_s[...] = jnp.zeros_like(acc_s)
            row_s[...] = jnp.zeros_like(row_s)

        q_t, kb, vb, do_t = q_ref[...], k_ref[...], v_ref[...], do_ref[...]
        s = _fox_logits(q_t, kb, cq_ref[...], ck_ref[...], qi * tq, ki * tk)
        p = jnp.exp(s - lse_ref[...])
        delta = jnp.sum(do_t * o_ref[...], axis=-1, keepdims=True)
        ds = p * (_dot(do_t, vb, _NT) - delta)
        acc_s[...] += _dot(ds, kb, _NN)
        row_s[...] += jnp.sum(ds, axis=-1, keepdims=True)
        dk_ref[ki] += _dot(ds, q_t, _TN) * (HEAD_DIM ** -0.5)
        dv_ref[ki] += _dot(p, do_t, _TN)
        dc_ref[ki] += jnp.sum(ds, axis=0, keepdims=True)

        @pl.when(ki == qi)
        def _():
            dq_ref[...] = acc_s[...] * (HEAD_DIM ** -0.5)
            dr_ref[...] = row_s[...]

    qs, ks, cqs, cks = _fox_specs(tq, Dh)
    head = lambda *blk: pl.BlockSpec((None,) + blk, lambda h, s, qt, kt: (h,) + (0,) * len(blk))
    dq, dr, dk, dv, dc = _fox_call(
        body, _fox_tiles(S // tq, True), Hh, [qs, ks, ks, cqs, cks, qs, cqs, qs],
        [qs, cqs, head(nk, tk, Dh), head(nk, tk, Dh), head(nk, 1, tk)],
        [jax.ShapeDtypeStruct((Hh, S, Dh), F32), jax.ShapeDtypeStruct((Hh, S, 1), F32),
         jax.ShapeDtypeStruct((Hh, nk, tk, Dh), F32), jax.ShapeDtypeStruct((Hh, nk, tk, Dh), F32),
         jax.ShapeDtypeStruct((Hh, nk, 1, tk), F32)],
        [pltpu.VMEM((tq, Dh), F32), pltpu.VMEM((tq, 1), F32)],
        "fox_bwd", (q, k, v, c_col, c_row, o, lse, do))
    return dq, dr, dk.reshape(Hh, S, Dh), dv.reshape(Hh, S, Dh), dc.reshape(Hh, 1, S)


def _heads(a, nh):
    S = a.shape[0]
    return a.reshape(S, nh, HEAD_DIM).transpose(1, 0, 2)


def _unheads(a):
    nh, S, _ = a.shape
    return a.transpose(1, 0, 2).reshape(S, nh * HEAD_DIM)


def _shift_down(a):
    return jnp.pad(a[:-1], ((1, 0), (0, 0)))


def _shift_up(a):
    return jnp.pad(a[1:], ((0, 1), (0, 0)))


def _block_diag_ones():
    i = np.arange(RWKV_DIM) // HEAD_DIM
    return jnp.asarray((i[:, None] == i[None, :]).astype(np.float32))


FFN_ROWS = 1024
FFN_COLS = 256


def _ffn_specs(S, F, tm, fc):
    nf = F // fc
    row = pl.BlockSpec((tm, D_MODEL), lambda i, j: (i, 0))
    vec = pl.BlockSpec((1, D_MODEL), lambda i, j: (0, 0))
    wg = pl.BlockSpec((D_MODEL, fc), lambda i, j: (0, j))
    wu = pl.BlockSpec((D_MODEL, fc), lambda i, j: (0, nf + j))
    wd = pl.BlockSpec((fc, D_MODEL), lambda i, j: (j, 0))
    hid = pl.BlockSpec((tm, fc), lambda i, j: (i, j))
    return nf, row, vec, wg, wu, wd, hid


def ffn_fwd(x, g_norm, w_gu, w_down, tag):
    S, F = x.shape[0], w_down.shape[0]
    tm, fc = min(FFN_ROWS, S), FFN_COLS
    nf, row, vec, wg, wu, wd, _ = _ffn_specs(S, F, tm, fc)

    def body(x_ref, g_ref, wg_ref, wu_ref, wd_ref, o_ref, hn_ref, hn_s, acc):
        j = pl.program_id(1)

        @pl.when(j == 0)
        def _():
            hn_s[...] = _rms(x_ref[...], g_ref[...]).astype(BF16)
            hn_ref[...] = hn_s[...]
            acc[...] = jnp.zeros_like(acc)

        g = _dot(hn_s[...], wg_ref[...], _NN)
        u = _dot(hn_s[...], wu_ref[...], _NN)
        acc[...] += _dot(g * _sigmoid_tanh(g) * u, wd_ref[...], _NN)

        @pl.when(j == nf - 1)
        def _():
            o_ref[...] = x_ref[...] + 0.5 * acc[...]

    out, hn = pl.pallas_call(
        body,
        grid=(S // tm, nf),
        in_specs=[row, vec, wg, wu, wd],
        out_specs=[row, row],
        out_shape=[jax.ShapeDtypeStruct((S, D_MODEL), F32), jax.ShapeDtypeStruct((S, D_MODEL), BF16)],
        scratch_shapes=[pltpu.VMEM((tm, D_MODEL), BF16), pltpu.VMEM((tm, D_MODEL), F32)],
        compiler_params=_cparams(("parallel", "arbitrary")),
        name=tag + "_fwd",
    )(x, g_norm, w_gu, w_gu, w_down)
    return out, (x, hn)


def ffn_bwd(dy, saved, g_norm, w_gu, w_down, tag):
    x, hn = saved
    S, F = x.shape[0], w_down.shape[0]
    tm, fc = min(FFN_ROWS, S), FFN_COLS
    nf, row, vec, wg, wu, wd, hid = _ffn_specs(S, F, tm, fc)

    def body(dy_ref, x_ref, hn_ref, g_ref, wg_ref, wu_ref, wd_ref, dx_ref, dgn_ref, a_ref, dg_ref, du_ref,
             dyh_s, dhn):
        i, j = pl.program_id(0), pl.program_id(1)

        @pl.when(j == 0)
        def _():
            dyh_s[...] = (0.5 * dy_ref[...]).astype(BF16)
            dhn[...] = jnp.zeros_like(dhn)

        hn_t = hn_ref[...]
        g = _dot(hn_t, wg_ref[...], _NN)
        u = _dot(hn_t, wu_ref[...], _NN)
        da = _dot(dyh_s[...], wd_ref[...], _NT)
        sig = _sigmoid_tanh(g)
        gs = g * sig
        a_ref[...] = (gs * u).astype(BF16)
        dg = ((da * u) * (sig + gs * (1.0 - sig))).astype(BF16)
        du = (da * gs).astype(BF16)
        dg_ref[...] = dg
        du_ref[...] = du
        dhn[...] += _dot(jnp.concatenate([dg, du], axis=1),
                         jnp.concatenate([wg_ref[...], wu_ref[...]], axis=1), _NT)

        @pl.when(j == nf - 1)
        def _():
            _, vjp_n = jax.vjp(_rms, x_ref[...], g_ref[...])
            dx, dgn = vjp_n(dhn[...])
            dx_ref[...] = dy_ref[...] + dx

            @pl.when(i == 0)
            def _():
                dgn_ref[...] = jnp.zeros_like(dgn_ref)

            dgn_ref[...] += dgn

    hshape = jax.ShapeDtypeStruct((S, F), BF16)
    dx, dgn, act, dg, du = pl.pallas_call(
        body,
        grid=(S // tm, nf),
        in_specs=[row, row, row, vec, wg, wu, wd],
        out_specs=[row, vec, hid, hid, hid],
        out_shape=[jax.ShapeDtypeStruct((S, D_MODEL), F32), jax.ShapeDtypeStruct((1, D_MODEL), F32),
                   hshape, hshape, hshape],
        scratch_shapes=[pltpu.VMEM((tm, D_MODEL), BF16), pltpu.VMEM((tm, D_MODEL), F32)],
        compiler_params=_cparams(("arbitrary", "arbitrary")),
        name=tag + "_bwd",
    )(dy, x, hn, g_norm, w_gu, w_gu, w_down)
    d_wdown = matmul(act, dy, "tn", tag + "_dwd", out_dtype=BF16, scale=0.5, tm=1408)
    d_wgu = jnp.concatenate([matmul(hn, dg, "tn", tag + "_dwg", out_dtype=BF16),
                             matmul(hn, du, "tn", tag + "_dwu", out_dtype=BF16)], axis=1)
    return dx, dgn, d_wgu, d_wdown


def ple_fwd(x, p_i, g_norm, w_gate, w_proj, tag):
    hn, = rowwise(_f_rms, [x], [g_norm], [(D_MODEL, BF16)], tag + "_rms")
    z = matmul(hn, w_gate, "nn", tag + "_gate")
    pp = matmul(p_i, w_proj, "nn", tag + "_proj")
    out, = rowwise(_f_ple, [x, z, pp], [], [(D_MODEL, F32)], tag + "_mix")
    return out, (x, hn, z, pp)


def ple_bwd(dy, saved, p_i, g_norm, w_gate, tag):
    x, hn, z, pp = saved
    (dz, dpp), _ = rowwise_vjp(_f_ple, [x, z, pp], [], [dy], tag + "_dmix", need=[False, True, True],
                               row_dtype=BF16)
    d_wproj = matmul(p_i, dpp, "tn", tag + "_dwp", out_dtype=BF16)
    d_wgate = matmul(hn, dz, "tn", tag + "_dwg", out_dtype=BF16)
    dhn = matmul(dz, w_gate, "nt", tag + "_dhn")
    (dx,), (dgn,) = rowwise_vjp(_f_rms_res, [x], [g_norm], [dhn, dy], tag + "_drms")
    return dx, dgn, d_wgate, d_wproj


def _swa_consts(sinks):
    slopes = np.asarray([2.0 ** (-(i + 1)) for i in range(SWA_HEADS)], np.float32)
    slope_col = jnp.asarray(np.repeat(slopes, BLOCK).reshape(SWA_KV_HEADS, SWA_GROUP * BLOCK, 1))
    sink_col = jnp.repeat(sinks.reshape(SWA_HEADS), BLOCK).reshape(SWA_KV_HEADS, SWA_GROUP * BLOCK, 1)
    return sink_col, slope_col


def even_mix_fwd(x, W, gather_src, later_weights):
    S = x.shape[0]
    hn, = rowwise(_f_rms, [x], [W["mix_norm0"]], [(D_MODEL, BF16)], "emix_rms")
    proj = matmul(hn, W["even_w_in"], "nn", "emix_in")
    qa = _heads(proj[:, :SWA_Q], SWA_HEADS).reshape(SWA_KV_HEADS, SWA_GROUP, S, HEAD_DIM)
    ka = _heads(proj[:, SWA_Q:SWA_Q + SWA_KV], SWA_KV_HEADS)
    va = _heads(proj[:, SWA_Q + SWA_KV:SWA_COLS], SWA_KV_HEADS)
    sink_col, slope_col = _swa_consts(W["swa_sinks"])
    ya = swa_fwd(qa, ka, va, sink_col, slope_col)
    ya = _unheads(ya.reshape(SWA_HEADS, S, HEAD_DIM))
    hb = proj[:, SWA_COLS:]
    h, = rowwise(_f_mix, [hb, _shift_down(hb)], [W["rwkv_mu"]], [(hb.shape[1], F32)], "rwkv_shift")
    hr, hk, hv = h[:, :512], h[:, 512:1024], h[:, 1024:1536]
    hw, ha, hg = h[:, 1536:1600], h[:, 1600:1664], h[:, 1664:1792]
    bd = _block_diag_ones()
    pre_params = [W["rwkv_w0"], W["rwkv_w2"], W["rwkv_a0"], W["rwkv_a2"], W["rwkv_g2"], W["rwkv_k_k"],
                  W["rwkv_k_a"]]
    decay, k2, kk, b, g = rowwise(_f_rwkv_pre, [hk, hw, ha, hg], pre_params + [bd],
                                  [(RWKV_DIM, F32)] * 5, "rwkv_pre")
    vT = _to_colblocks(hv)
    y, ckpt, gathered = rwkv_scan_fwd(hr, decay, k2, kk, b, vT, gather_src)
    late = later_weights(gathered)
    post_params = [W["rwkv_ln_w"], W["rwkv_ln_b"], W["rwkv_r_k"]]
    yb, = rowwise(_f_rwkv_post, [y, hr, k2, hv, g], post_params + [bd], [(RWKV_DIM, F32)], "rwkv_post")
    cat = jnp.concatenate([ya, yb], axis=1).astype(BF16)
    out = matmul(cat, late["even_w_out"], "nn", "emix_out", res=x)
    saved = (x, hn, qa, ka, va, sink_col, slope_col, hb, hr, hk, hv, hw, ha, hg, decay, k2, kk, b, g, vT,
             ckpt, y, cat)
    return out, saved, late


def even_mix_bwd(dy, saved, W, scatter_src):
    (x, hn, qa, ka, va, sink_col, slope_col, hb, hr, hk, hv, hw, ha, hg, decay, k2, kk, b, g, vT, ckpt, y,
     cat) = saved
    S = x.shape[0]
    grads = {}
    dcat = matmul(dy, W["even_w_out"], "nt", "emix_dcat")
    grads["even_w_out"] = matmul(cat, dy, "tn", "emix_dwout", out_dtype=BF16)
    dya, dyb = dcat[:, :SWA_Q], dcat[:, SWA_Q:]
    dya_h = _heads(dya, SWA_HEADS).reshape(SWA_KV_HEADS, SWA_GROUP, S, HEAD_DIM)
    dqa, dkp, dkc, dvp, dvc, dsink = swa_bwd(qa, ka, va, sink_col, slope_col, dya_h)
    shift_blk = lambda a: jnp.pad(a[:, BLOCK:], ((0, 0), (0, BLOCK), (0, 0)))
    dka = dkc + shift_blk(dkp)
    dva = dvc + shift_blk(dvp)
    grads["swa_sinks"] = dsink.reshape(SWA_HEADS, BLOCK).sum(axis=1).reshape(1, SWA_HEADS)
    dqa = _unheads(dqa.reshape(SWA_HEADS, S, HEAD_DIM))
    dka, dva = _unheads(dka), _unheads(dva)
    bd = _block_diag_ones()
    post_params = [W["rwkv_ln_w"], W["rwkv_ln_b"], W["rwkv_r_k"]]
    (d_y, d_r1, d_k2a, d_v1, d_g), (d_lnw, d_lnb, d_rk) = rowwise_vjp(
        _f_rwkv_post, [y, hr, k2, hv, g], post_params, [dyb], "rwkv_dpost", consts=[bd], tm=128)
    grads["rwkv_ln_w"], grads["rwkv_ln_b"], grads["rwkv_r_k"] = d_lnw, d_lnb, d_rk
    d_r2, d_w, d_k2b, d_kk, d_b, d_v2, exchanged = rwkv_scan_bwd(hr, decay, k2, kk, b, vT, _to_colblocks(d_y), ckpt,
                                                                  scatter_src)
    pre_params = [W["rwkv_w0"], W["rwkv_w2"], W["rwkv_a0"], W["rwkv_a2"], W["rwkv_g2"], W["rwkv_k_k"],
                  W["rwkv_k_a"]]
    (d_hk, d_hw, d_ha, d_hg), dpre = rowwise_vjp(
        _f_rwkv_pre, [hk, hw, ha, hg], pre_params, [d_w, d_k2a + d_k2b, d_kk, d_b, d_g], "rwkv_dpre",
        consts=[bd], tm=128)
    for nm, gval in zip(["rwkv_w0", "rwkv_w2", "rwkv_a0", "rwkv_a2", "rwkv_g2", "rwkv_k_k", "rwkv_k_a"], dpre):
        grads[nm] = gval
    d_h = jnp.concatenate([d_r1 + d_r2, d_hk, d_v1 + d_v2, d_hw, d_ha, d_hg], axis=1)
    (d_hb, d_sh), (d_mu,) = rowwise_vjp(_f_mix, [hb, _shift_down(hb)], [W["rwkv_mu"]], [d_h], "rwkv_dshift")
    grads["rwkv_mu"] = d_mu
    d_hb = d_hb + _shift_up(d_sh)
    dproj = jnp.concatenate([dqa, dka, dva, d_hb], axis=1).astype(BF16)
    grads["even_w_in"] = matmul(hn, dproj, "tn", "emix_dwin", out_dtype=BF16)
    dhn = matmul(dproj, W["even_w_in"], "nt", "emix_dhn")
    (dx,), (dgn,) = rowwise_vjp(_f_rms_res, [x], [W["mix_norm0"]], [dhn, dy], "emix_drms")
    grads["mix_norm0"] = dgn
    return dx, grads, exchanged


def odd_mix_fwd(x, W):
    S = x.shape[0]
    hn, = rowwise(_f_rms, [x], [W["mix_norm1"]], [(D_MODEL, BF16)], "omix_rms")
    proj = matmul(hn, W["fox_w_in"], "nn", "omix_in")
    q = _heads(proj[:, :FOX_DIM], FOX_HEADS).astype(BF16)
    k = _heads(proj[:, FOX_DIM:2 * FOX_DIM], FOX_HEADS).astype(BF16)
    v = _heads(proj[:, 2 * FOX_DIM:3 * FOX_DIM], FOX_HEADS).astype(BF16)
    fz = proj[:, 3 * FOX_DIM:]
    logf, = rowwise(_f_logf, [fz], [W["fox_b_f"]], [(128, F32)], "fox_logf")
    c = seq_cumsum(logf, False, "fox_cumsum")[:, :FOX_HEADS]
    c_col = c.T.reshape(FOX_HEADS, S, 1)
    c_row = c.T.reshape(FOX_HEADS, 1, S)
    o, lse = fox_fwd(q, k, v, c_col, c_row)
    yc = _unheads(o).astype(BF16)
    out = matmul(yc, W["fox_w_out"], "nn", "omix_out", res=x)
    return out, (x, hn, q, k, v, fz, c_col, c_row, o, lse, yc)


def odd_mix_bwd(dy, saved, W):
    x, hn, q, k, v, fz, c_col, c_row, o, lse, yc = saved
    S = x.shape[0]
    grads = {}
    dyc = matmul(dy, W["fox_w_out"], "nt", "omix_dyc")
    grads["fox_w_out"] = matmul(yc, dy, "tn", "omix_dwout", out_dtype=BF16)
    do = _heads(dyc, FOX_HEADS)
    dq, drow, dk, dv, dcol = fox_bwd(q, k, v, c_col, c_row, o, lse, do)
    dc = (drow.reshape(FOX_HEADS, S) - dcol.reshape(FOX_HEADS, S)).T
    dc = jnp.pad(dc, ((0, 0), (0, 128 - FOX_HEADS)))
    dlogf = seq_cumsum(dc, True, "fox_rcumsum")
    (dfz,), (dbf,) = rowwise_vjp(_f_logf, [fz], [W["fox_b_f"]], [dlogf], "fox_dlogf")
    grads["fox_b_f"] = dbf
    dproj = jnp.concatenate([_unheads(dq), _unheads(dk), _unheads(dv), dfz], axis=1).astype(BF16)
    grads["fox_w_in"] = matmul(hn, dproj, "tn", "omix_dwin", out_dtype=BF16)
    dhn = matmul(dproj, W["fox_w_in"], "nt", "omix_dhn")
    (dx,), (dgn,) = rowwise_vjp(_f_rms_res, [x], [W["mix_norm1"]], [dhn, dy], "omix_drms")
    grads["mix_norm1"] = dgn
    return dx, grads


def device_step(x, p, target, W, gather_src, layer1_weights, layer1_grads):
    W = dict(W)
    saved = []
    h = x
    for i in range(2):
        h, s1 = ffn_fwd(h, W[f"ffn1_norm{i}"], W[f"ffn1_w_gu{i}"], W[f"ffn1_w_down{i}"], f"ffn1_{i}")
        if i == 0:
            h, s2, late = even_mix_fwd(h, W, gather_src, layer1_weights)
            W.update(late)
        else:
            h, s2 = odd_mix_fwd(h, W)
        h, s3 = ffn_fwd(h, W[f"ffn2_norm{i}"], W[f"ffn2_w_gu{i}"], W[f"ffn2_w_down{i}"], f"ffn2_{i}")
        h, s4 = ple_fwd(h, p[i], W[f"ple_norm{i}"], W[f"ple_w_gate{i}"], W[f"ple_w_proj{i}"], f"ple_{i}")
        saved.append((s1, s2, s3, s4))
    dh, d_final, loss = loss_head(h, target, W["final_norm"])
    G = {"final_norm": d_final}
    for i in (1, 0):
        s1, s2, s3, s4 = saved[i]
        dh, G[f"ple_norm{i}"], G[f"ple_w_gate{i}"], G[f"ple_w_proj{i}"] = ple_bwd(
            dh, s4, p[i], W[f"ple_norm{i}"], W[f"ple_w_gate{i}"], f"ple_{i}")
        dh, G[f"ffn2_norm{i}"], G[f"ffn2_w_gu{i}"], G[f"ffn2_w_down{i}"] = ffn_bwd(
            dh, s3, W[f"ffn2_norm{i}"], W[f"ffn2_w_gu{i}"], W[f"ffn2_w_down{i}"], f"ffn2_{i}")
        if i == 0:
            dh, gm, exchanged = even_mix_bwd(dh, s2, W, layer1_grads(G))
        else:
            dh, gm = odd_mix_bwd(dh, s2, W)
        G.update(gm)
        dh, G[f"ffn1_norm{i}"], G[f"ffn1_w_gu{i}"], G[f"ffn1_w_down{i}"] = ffn_bwd(
            dh, s1, W[f"ffn1_norm{i}"], W[f"ffn1_w_gu{i}"], W[f"ffn1_w_down{i}"], f"ffn1_{i}")
    return loss, dh, G, exchanged


_MESH = pl.DeviceIdType.MESH
_ANY = pl.BlockSpec(memory_space=pl.ANY)


def _exchange_sems(n):
    return [pltpu.SemaphoreType.DMA((7 * n,)), pltpu.SemaphoreType.DMA((7 * n,)), pltpu.SemaphoreType.DMA((n,))]


def all_gather(xs, name):
    n = len(xs)

    def body(*refs):
        x_refs, out_refs = refs[:n], refs[n:2 * n]
        send_sems, recv_sems, local_sems = refs[2 * n:]
        x_, y_, c_ = lax.axis_index("x"), lax.axis_index("y"), lax.axis_index("c")
        me, sibling = (x_, y_, c_), (x_, y_, 1 - c_)
        chips = [(1 - x_, y_), (x_, 1 - y_), (1 - x_, 1 - y_)]

        def copy(b, k, block, to, from_input=False):
            slot = out_refs[b].at[4 * block[0] + 2 * block[1] + block[2]]
            return pltpu.make_async_remote_copy(
                src_ref=x_refs[b] if from_input else slot, dst_ref=slot,
                send_sem=send_sems.at[7 * b + k], recv_sem=recv_sems.at[7 * b + k], device_id=to,
                device_id_type=_MESH)

        bufs = range(n)
        mine = [pltpu.make_async_copy(x_refs[b], out_refs[b].at[4 * x_ + 2 * y_ + c_], local_sems.at[b]) for b in bufs]
        first = [copy(b, 0, me, sibling, True) for b in bufs]
        first += [copy(b, 1 + j, me, (*chip, c_), True) for j, chip in enumerate(chips) for b in bufs]
        for cp in mine + first:
            cp.start()
        passed = []
        for j, chip in enumerate(chips):
            for b in bufs:
                copy(b, 1 + j, (*chip, c_), me).wait_recv()
                passed.append(copy(b, 4 + j, (*chip, c_), sibling))
                passed[-1].start()
        for b in bufs:
            copy(b, 0, sibling, me).wait_recv()
            for j, chip in enumerate(chips):
                copy(b, 4 + j, (*chip, 1 - c_), me).wait_recv()
        for cp in first + passed:
            cp.wait_send()
        for cp in mine:
            cp.wait()

    return pl.pallas_call(
        body,
        out_shape=[jax.ShapeDtypeStruct((N_DEV,) + x.shape, x.dtype) for x in xs],
        in_specs=[_ANY] * n,
        out_specs=[_ANY] * n,
        scratch_shapes=_exchange_sems(n),
        name=name,
    )(*xs)


def _direct_exchange(gather, s_refs, r_refs, send_sems, recv_sems, local_sems):
    x_, y_, c_ = lax.axis_index("x"), lax.axis_index("y"), lax.axis_index("c")
    my = 4 * x_ + 2 * y_ + c_
    copies = []
    for b, (s_ref, r_ref) in enumerate(zip(s_refs, r_refs)):
        copies.append(pltpu.make_async_copy(s_ref if gather else s_ref.at[my], r_ref.at[my], local_sems.at[b]))
        for m in range(1, N_DEV):
            px = 1 - x_ if (m >> 2) & 1 else x_
            py = 1 - y_ if (m >> 1) & 1 else y_
            pc = 1 - c_ if m & 1 else c_
            copies.append(pltpu.make_async_remote_copy(
                src_ref=s_ref if gather else s_ref.at[4 * px + 2 * py + pc], dst_ref=r_ref.at[my],
                send_sem=send_sems.at[7 * b + m - 1], recv_sem=recv_sems.at[7 * b + m - 1],
                device_id=(px, py, pc), device_id_type=_MESH))
    return copies


def _exchange_during(step, n_steps, gather, s_refs, r_refs, send_sems, recv_sems, local_sems):
    copies = _direct_exchange(gather, s_refs, r_refs, send_sems, recv_sems, local_sems)

    @pl.when(step == 0)
    def _():
        for cp in copies:
            cp.start()

    @pl.when(step == n_steps - 1)
    def _():
        for cp in copies:
            cp.wait()


def _exchange_out_shapes(gather, srcs):
    return [jax.ShapeDtypeStruct(((N_DEV,) + s.shape) if gather else s.shape, s.dtype) for s in srcs]


def all_to_all(sends, name):
    n = len(sends)

    def body(*refs):
        copies = _direct_exchange(False, refs[:n], refs[n:2 * n], *refs[2 * n:])
        for cp in copies:
            cp.start()
        for cp in copies:
            cp.wait()

    return pl.pallas_call(
        body,
        out_shape=_exchange_out_shapes(False, sends),
        in_specs=[_ANY] * n,
        out_specs=[_ANY] * n,
        scratch_shapes=_exchange_sems(n),
        name=name,
    )(*sends)


def adamw(w, m, v, parts, name, tm=256):
    R, C = w.shape
    tm = _pick(R, tm, 8) if R >= 8 else R

    def body(w_ref, m_ref, v_ref, p_ref, g_ref, d_ref, nm_ref, nv_ref):
        g = p_ref[0].astype(F32)
        for s in range(1, N_DEV):
            g = g + p_ref[s].astype(F32)
        nm = ADAM_B1 * m_ref[...] + (1.0 - ADAM_B1) * g
        nv = ADAM_B2 * v_ref[...] + (1.0 - ADAM_B2) * (g * g)
        m_hat = nm / (1.0 - ADAM_B1 ** ADAM_STEP)
        v_hat = nv / (1.0 - ADAM_B2 ** ADAM_STEP)
        g_ref[...] = g
        d_ref[...] = -ADAM_LR * (m_hat / (jnp.sqrt(v_hat) + ADAM_EPS) + ADAM_WD * w_ref[...])
        nm_ref[...] = nm
        nv_ref[...] = nv

    row = pl.BlockSpec((tm, C), lambda i: (i, 0))
    out = jax.ShapeDtypeStruct((R, C), F32)
    return pl.pallas_call(
        body,
        grid=(R // tm,),
        in_specs=[row, row, row, pl.BlockSpec((N_DEV, tm, C), lambda i: (0, i, 0))],
        out_specs=[row] * 4,
        out_shape=[out] * 4,
        compiler_params=_cparams(("parallel",)),
        name=name,
    )(w, m, v, parts)


_WEIGHTS = ["ffn1_norm", "ffn1_w_gu", "ffn1_w_down", "mix_norm", "ffn2_norm", "ffn2_w_gu", "ffn2_w_down",
            "ple_norm", "ple_w_gate", "ple_w_proj", "even_w_in", "even_w_out", "swa_sinks", "rwkv_mu",
            "rwkv_w0", "rwkv_w2", "rwkv_a0", "rwkv_a2", "rwkv_g2", "rwkv_k_k", "rwkv_k_a", "rwkv_r_k",
            "rwkv_ln_w", "rwkv_ln_b", "fox_w_in", "fox_b_f", "fox_w_out", "final_norm"]
_SHARD_AXIS = {"ffn1_w_gu": 2, "ffn1_w_down": 1, "ffn2_w_gu": 2, "ffn2_w_down": 1, "ple_w_gate": 1,
               "ple_w_proj": 2, "even_w_in": 2, "even_w_out": 1, "rwkv_w2": 2, "rwkv_a2": 2, "rwkv_g2": 2,
               "fox_w_in": 2, "fox_w_out": 1}
_SHARDED = [n for n in _WEIGHTS if n in _SHARD_AXIS]
_REPLICATED = [n for n in _WEIGHTS if n not in _SHARD_AXIS]
_PER_LAYER = ("ffn1_w_gu", "ffn1_w_down", "ffn2_w_gu", "ffn2_w_down", "ple_w_gate", "ple_w_proj")
_ALL_PIECES = ([(n, 0) for n in _PER_LAYER] + [(n, 0) for n in ("even_w_in", "even_w_out", "rwkv_w2", "rwkv_a2", "rwkv_g2")]
               + [(n, 1) for n in _PER_LAYER] + [("fox_w_in", 0), ("fox_w_out", 0)])
_FIRST_WEIGHTS = [(n, 0) for n in ("ffn1_w_gu", "ffn1_w_down", "even_w_in", "rwkv_w2", "rwkv_a2", "rwkv_g2")]
_PIECES = [_FIRST_WEIGHTS, [pc for pc in _ALL_PIECES if pc not in _FIRST_WEIGHTS]]
_LATE_GRADS = _FIRST_WEIGHTS + [("even_w_out", 0)]
_GRAD_PIECES = [_LATE_GRADS, [pc for pc in _ALL_PIECES if pc not in _LATE_GRADS]]
_PACK_LANES = 1024
_PACK_ROW_TILE = 256


def _piece_key(piece):
    name, idx = piece
    return f"{name}{idx}" if name in _PER_LAYER else name


_KINDS = ("gu", "rows", "misc")


def _kind(piece):
    if piece[0] in ("ffn1_w_gu", "ffn2_w_gu"):
        return "gu"
    return "rows" if _SHARD_AXIS[piece[0]] == 1 else "misc"


def _of_kind(pieces, shapes, kind):
    return [(pc, shp) for pc, shp in zip(pieces, shapes) if _kind(pc) == kind]


def _pad_rows(flat, axis):
    pad = [(0, 0)] * flat.ndim
    pad[axis] = (0, -flat.shape[axis] % _PACK_ROW_TILE)
    return jnp.pad(flat, pad)


def _bundle(get, pieces, dtype):
    take = lambda kind: [get(pc).astype(dtype) for pc in pieces if _kind(pc) == kind]
    return [jnp.stack(take("gu")), jnp.concatenate(take("rows"), axis=0),
            _pad_rows(jnp.concatenate([a.reshape(-1, _PACK_LANES) for a in take("misc")], axis=0), 0)]


def _unbundle(bufs, pieces, shapes):
    out = {}
    gu = _of_kind(pieces, shapes, "gu")
    stacked = bufs[0].reshape((len(gu),) + gu[0][1])
    for j, (pc, _) in enumerate(gu):
        out[pc] = stacked[j]
    for buf, kind in ((bufs[1], "rows"), (bufs[2], "misc")):
        r0 = 0
        for pc, shp in _of_kind(pieces, shapes, kind):
            n = math.prod(shp) // _PACK_LANES
            out[pc] = buf[r0:r0 + n].reshape(shp)
            r0 += n
    return out


def _unshard(gathered, pieces, shapes):
    full = {}
    for j, (pc, shp) in enumerate(_of_kind(pieces, shapes, "gu")):
        full[_piece_key(pc)] = jnp.moveaxis(gathered[0][:, j], 0, 1).reshape(shp[0], N_DEV * shp[1])
    r0 = 0
    for pc, shp in _of_kind(pieces, shapes, "rows"):
        full[_piece_key(pc)] = gathered[1][:, r0:r0 + shp[0]].reshape(N_DEV * shp[0], shp[1])
        r0 += shp[0]
    r0 = 0
    for pc, shp in _of_kind(pieces, shapes, "misc"):
        n = math.prod(shp) // _PACK_LANES
        seg = gathered[2][:, r0:r0 + n].reshape((N_DEV,) + shp)
        full[_piece_key(pc)] = jnp.moveaxis(seg, 0, 1).reshape(shp[0], N_DEV * shp[1])
        r0 += n
    return full


def _to_shards(full, pieces, shapes):
    get = lambda pc: full[_piece_key(pc)].astype(BF16)
    cols = lambda pc, shp: jnp.moveaxis(get(pc).reshape(shp[0], N_DEV, shp[1]), 1, 0)
    gu = jnp.stack([cols(pc, shp) for pc, shp in _of_kind(pieces, shapes, "gu")], axis=1)
    rows = jnp.concatenate([get(pc).reshape((N_DEV,) + shp) for pc, shp in _of_kind(pieces, shapes, "rows")], axis=1)
    misc = jnp.concatenate([cols(pc, shp).reshape(N_DEV, -1, _PACK_LANES)
                            for pc, shp in _of_kind(pieces, shapes, "misc")], axis=1)
    return [gu, rows, _pad_rows(misc, 1)]


def _layer_weights(full):
    W = dict(full)
    if "fox_w_in" in W:
        W["fox_w_in"] = jnp.pad(W["fox_w_in"], ((0, 0), (0, FOX_IN_PAD - W["fox_w_in"].shape[1])))
    for n in ("rwkv_w2", "rwkv_a2", "rwkv_g2"):
        if n in W:
            W[n] = W[n].astype(F32)
    return W


def _pack_small(vals):
    flat = jnp.concatenate([v.reshape(1, -1) for v in vals], axis=1)
    n = flat.shape[1]
    return jnp.pad(flat, ((0, 0), (0, -n % 128)))


def _unpack_small(flat, shapes):
    out, c0 = [], 0
    for shp in shapes:
        n = math.prod(shp)
        out.append(flat[0, c0:c0 + n].reshape(shp))
        c0 += n
    return out


def kernel(x, p, ffn1_norm, ffn1_w_gu, ffn1_w_down, mix_norm, ffn2_norm, ffn2_w_gu, ffn2_w_down, ple_norm, ple_w_gate, ple_w_proj, even_w_in, even_w_out, swa_sinks, rwkv_mu, rwkv_w0, rwkv_w2, rwkv_a0, rwkv_a2, rwkv_g2, rwkv_k_k, rwkv_k_a, rwkv_r_k, rwkv_ln_w, rwkv_ln_b, fox_w_in, fox_b_f, fox_w_out, final_norm, loss_target, m_ffn1_norm, m_ffn1_w_gu, m_ffn1_w_down, m_mix_norm, m_ffn2_norm, m_ffn2_w_gu, m_ffn2_w_down, m_ple_norm, m_ple_w_gate, m_ple_w_proj, m_even_w_in, m_even_w_out, m_swa_sinks, m_rwkv_mu, m_rwkv_w0, m_rwkv_w2, m_rwkv_a0, m_rwkv_a2, m_rwkv_g2, m_rwkv_k_k, m_rwkv_k_a, m_rwkv_r_k, m_rwkv_ln_w, m_rwkv_ln_b, m_fox_w_in, m_fox_b_f, m_fox_w_out, m_final_norm, v_ffn1_norm, v_ffn1_w_gu, v_ffn1_w_down, v_mix_norm, v_ffn2_norm, v_ffn2_w_gu, v_ffn2_w_down, v_ple_norm, v_ple_w_gate, v_ple_w_proj, v_even_w_in, v_even_w_out, v_swa_sinks, v_rwkv_mu, v_rwkv_w0, v_rwkv_w2, v_rwkv_a0, v_rwkv_a2, v_rwkv_g2, v_rwkv_k_k, v_rwkv_k_a, v_rwkv_r_k, v_rwkv_ln_w, v_rwkv_ln_b, v_fox_w_in, v_fox_b_f, v_fox_w_out, v_final_norm):
    given = dict(locals())
    w = {n: given[n] for n in _WEIGHTS}
    m = {n: given["m_" + n] for n in _WEIGHTS}
    v = {n: given["v_" + n] for n in _WEIGHTS}
    small_shapes = [w[n].shape for n in _REPLICATED]
    piece = lambda d, pc: d[pc[0]][pc[1]]
    shapes = [[piece(w, pc).shape for pc in pieces] for pieces in _PIECES]
    gshapes = [[piece(w, pc).shape for pc in pieces] for pieces in _GRAD_PIECES]
    w_send = [_bundle(lambda pc: piece(w, pc), pieces, BF16) for pieces in _PIECES]

    W = _layer_weights(_unshard(all_gather(w_send[0], "weights_all_gather"), _PIECES[0], shapes[0]))
    for i in range(2):
        for n in ("ffn1_norm", "mix_norm", "ffn2_norm", "ple_norm"):
            W[f"{n}{i}"] = w[n][i].reshape(1, -1)
    for n in ("swa_sinks", "rwkv_mu", "rwkv_w0", "rwkv_a0", "rwkv_k_k", "rwkv_k_a", "rwkv_r_k", "rwkv_ln_w",
              "rwkv_ln_b", "final_norm"):
        W[n] = w[n].reshape(1, -1)
    n_f = fox_b_f.shape[1]
    W["fox_b_f"] = jnp.pad(fox_b_f.reshape(1, n_f), ((0, 0), (0, 128 - n_f)))
    n_fox = fox_w_in.shape[2] * N_DEV

    def layer1_weights(gathered):
        return _layer_weights(_unshard(gathered, _PIECES[1], shapes[1]))

    def early_grads(G):
        G = dict(G, fox_w_in=G["fox_w_in"][:, :n_fox])
        return _to_shards(G, _GRAD_PIECES[1], gshapes[1])

    loss_row, dx, G, parts_early = device_step(x[0], p[:, 0], loss_target[0], W, w_send[1], layer1_weights,
                                               early_grads)

    parts = [all_to_all(_to_shards(G, _GRAD_PIECES[0], gshapes[0]), "grads_all_to_all"), parts_early]
    out_g, out_d, out_m, out_v = {}, {}, {}, {}
    rows2d = lambda a, lead: a.reshape(a.shape[:lead] + (-1, a.shape[-1]))
    for li, pieces in enumerate(_GRAD_PIECES):
        wmv = [_bundle(lambda pc, d=d: piece(d, pc), pieces, F32) for d in (w, m, v)]
        res = [adamw(*[rows2d(b[ki], 0) for b in wmv], rows2d(parts[li][ki], 1), f"adamw_{kind}{li}")
               for ki, kind in enumerate(_KINDS)]
        for oi, out in enumerate((out_g, out_d, out_m, out_v)):
            for pc, a in _unbundle([r[oi] for r in res], pieces, gshapes[li]).items():
                out.setdefault(pc[0], {})[pc[1]] = a
    for out in (out_g, out_d, out_m, out_v):
        for n in _SHARDED:
            out[n] = jnp.stack([out[n][i] for i in sorted(out[n])])

    gsmall = {}
    for n in ("ffn1_norm", "mix_norm", "ffn2_norm", "ple_norm"):
        gsmall[n] = jnp.concatenate([G[f"{n}0"], G[f"{n}1"]], axis=0)
    for n in ("swa_sinks", "rwkv_mu", "rwkv_w0", "rwkv_a0", "rwkv_k_k", "rwkv_k_a", "rwkv_r_k", "rwkv_ln_w",
              "rwkv_ln_b", "final_norm"):
        gsmall[n] = G[n]
    gsmall["fox_b_f"] = G["fox_b_f"][:, :n_f]
    small = _pack_small([gsmall[n] for n in _REPLICATED] + [loss_row[:, :1]])
    small_parts = all_gather([small], "small_all_gather")[0]
    pad1 = lambda vals: _pack_small(vals + [jnp.zeros((1, 1), F32)])
    gs, ds, nms, nvs = adamw(pad1([w[n] for n in _REPLICATED]), pad1([m[n] for n in _REPLICATED]),
                             pad1([v[n] for n in _REPLICATED]), small_parts, "adamw_replicated")
    out_g.update(zip(_REPLICATED, _unpack_small(gs, small_shapes)))
    out_d.update(zip(_REPLICATED, _unpack_small(ds, small_shapes)))
    out_m.update(zip(_REPLICATED, _unpack_small(nms, small_shapes)))
    out_v.update(zip(_REPLICATED, _unpack_small(nvs, small_shapes)))
    n_small = sum(math.prod(s) for s in small_shapes)
    loss = gs[0, n_small]

    return (loss, dx[None], *[out_g[n] for n in _WEIGHTS], *[out_d[n] for n in _WEIGHTS],
            *[out_m[n] for n in _WEIGHTS], *[out_v[n] for n in _WEIGHTS])
```

```python
import functools
import math

import numpy as np
import jax
import jax.numpy as jnp
from jax import lax
from jax.experimental import pallas as pl
from jax.experimental.pallas import tpu as pltpu

F32 = jnp.float32
BF16 = jnp.bfloat16

D_MODEL = 1024
HEAD_DIM = 64
BLOCK = 128
SWA_HEADS = 8
SWA_KV_HEADS = 2
SWA_GROUP = 4
RWKV_HEADS = 8
RWKV_DIM = 512
FOX_HEADS = 16
FOX_DIM = 1024
D_FF = 2816
NORM_EPS = 1e-6
GN_EPS = 64e-5
L2_EPS = 1e-12
SWA_Q = 512
SWA_KV = 128
SWA_COLS = 768
FOX_IN_PAD = 3200
N_DEV = 8
ADAM_LR = 0.001
ADAM_B1 = 0.9
ADAM_B2 = 0.999
ADAM_EPS = 1e-08
ADAM_WD = 0.01
ADAM_STEP = 10

V7X_VMEM_LIMIT = 56 * 1024 * 1024
SCAN_GROUP = 8
SCAN_CHUNK = 64

_NN = (((1,), (0,)), ((), ()))
_NT = (((1,), (1,)), ((), ()))
_TN = (((0,), (0,)), ((), ()))
_DIMS = {"nn": _NN, "nt": _NT, "tn": _TN}


def _pick(n, target, mult=128):
    best = None
    for t in range(mult, min(n, target) + 1, mult):
        if n % t == 0:
            best = t
    return best or n


def _cparams(sem):
    return pltpu.CompilerParams(dimension_semantics=sem, vmem_limit_bytes=V7X_VMEM_LIMIT)


def _dot(a, b, dims):
    return lax.dot_general(a.astype(BF16), b.astype(BF16), dims, preferred_element_type=F32)


@jax.custom_vjp
def bdot(a, b):
    return _dot(a, b, _NN)


def _bdot_fwd(a, b):
    return _dot(a, b, _NN), (a, b)


def _bdot_bwd(res, g):
    a, b = res
    return _dot(g, b, _NT), _dot(a, g, _TN)


bdot.defvjp(_bdot_fwd, _bdot_bwd)


@jax.custom_vjp
def bdot_nt(a, b):
    return _dot(a, b, _NT)


def _bdot_nt_fwd(a, b):
    return _dot(a, b, _NT), (a, b)


def _bdot_nt_bwd(res, g):
    a, b = res
    return _dot(g, b, _NN), _dot(g, a, _TN)


bdot_nt.defvjp(_bdot_nt_fwd, _bdot_nt_bwd)


@jax.custom_vjp
def _segsum(x, bd):
    return _dot2(x, bd.astype(BF16))


def _segsum_fwd(x, bd):
    return _segsum(x, bd), bd


def _segsum_bwd(bd, g):
    return _dot2(g, bd.astype(BF16)), jnp.zeros_like(bd)


_segsum.defvjp(_segsum_fwd, _segsum_bwd)


def _sigmoid(x):
    return 1.0 / (1.0 + jnp.exp(-x))


def _sigmoid_tanh(x):
    return 0.5 * jnp.tanh(0.5 * x) + 0.5


def _softplus(x):
    return jnp.maximum(x, 0.0) + jnp.log(1.0 + jnp.exp(-jnp.abs(x)))


def matmul(a, b, mode, name, out_dtype=F32, scale=1.0, res=None, tm=512, tn=1408, tk=1024):
    if mode == "nn":
        (M, K), (K2, N) = a.shape, b.shape
    elif mode == "nt":
        (M, K), (N, K2) = a.shape, b.shape
    else:
        (K, M), (K2, N) = a.shape, b.shape
    assert K == K2, (a.shape, b.shape, mode)
    tm, tn, tk = _pick(M, tm), _pick(N, tn), _pick(K, tk)
    nk = K // tk
    has_res = res is not None

    def body(*refs):
        if has_res:
            a_ref, b_ref, r_ref, o_ref, acc = refs
        else:
            a_ref, b_ref, o_ref, acc = refs
        kk = pl.program_id(2)

        @pl.when(kk == 0)
        def _():
            acc[...] = jnp.zeros_like(acc)

        acc[...] += _dot(a_ref[...], b_ref[...], _DIMS[mode])

        @pl.when(kk == nk - 1)
        def _():
            v = acc[...]
            if scale != 1.0:
                v = v * scale
            if has_res:
                v = v + r_ref[...].astype(F32)
            o_ref[...] = v.astype(out_dtype)

    if mode == "tn":
        a_spec = pl.BlockSpec((tk, tm), lambda i, j, k: (k, i))
    else:
        a_spec = pl.BlockSpec((tm, tk), lambda i, j, k: (i, k))
    if mode == "nt":
        b_spec = pl.BlockSpec((tn, tk), lambda i, j, k: (j, k))
    else:
        b_spec = pl.BlockSpec((tk, tn), lambda i, j, k: (k, j))
    o_spec = pl.BlockSpec((tm, tn), lambda i, j, k: (i, j))
    in_specs = [a_spec, b_spec] + ([o_spec] if has_res else [])
    args = (a, b) + ((res,) if has_res else ())
    return pl.pallas_call(
        body,
        grid=(M // tm, N // tn, nk),
        in_specs=in_specs,
        out_specs=o_spec,
        out_shape=jax.ShapeDtypeStruct((M, N), out_dtype),
        scratch_shapes=[pltpu.VMEM((tm, tn), F32)],
        compiler_params=_cparams(("parallel", "parallel", "arbitrary")),
        name=name,
    )(*args)


def _row_spec(r, tm):
    if isinstance(r, tuple):
        arr, width, blk = r
        return arr, pl.BlockSpec((tm, width), lambda i, blk=blk: (i, blk))
    return r, pl.BlockSpec((tm, r.shape[1]), lambda i: (i, 0))


def _whole_spec(p):
    return pl.BlockSpec(p.shape, lambda i: (0,) * p.ndim)


def rowwise(fn, rows, params, outs, name, tm=256):
    arrs, specs = zip(*[_row_spec(r, tm) for r in rows])
    S = arrs[0].shape[0]
    tm = min(tm, S)
    arrs, specs = zip(*[_row_spec(r, tm) for r in rows])
    n_in = len(rows) + len(params)

    def body(*refs):
        res = fn(*[r[...] for r in refs[:n_in]])
        for o_ref, v in zip(refs[n_in:], res):
            o_ref[...] = v.astype(o_ref.dtype)

    return pl.pallas_call(
        body,
        grid=(S // tm,),
        in_specs=list(specs) + [_whole_spec(p) for p in params],
        out_specs=[pl.BlockSpec((tm, c), lambda i: (i, 0)) for c, _ in outs],
        out_shape=[jax.ShapeDtypeStruct((S, c), dt) for c, dt in outs],
        compiler_params=_cparams(("parallel",)),
        name=name,
    )(*arrs, *params)


def rowwise_vjp(fn, rows, params, cots, name, need=None, row_dtype=F32, consts=(), tm=256):
    nr, npar, nc, nk = len(rows), len(params), len(cots), len(consts)
    need = [True] * nr if need is None else need
    arrs, _ = zip(*[_row_spec(r, tm) for r in rows])
    S = arrs[0].shape[0]
    tm = min(tm, S)
    arrs, specs = zip(*[_row_spec(r, tm) for r in rows])
    carrs, cspecs = zip(*[_row_spec(c, tm) for c in cots])
    widths = [s.block_shape[1] for s in specs]
    n_in = nr + npar + nk + nc

    def body(*refs):
        i = pl.program_id(0)
        xs = [r[...].astype(F32) for r in refs[:nr]]
        ps = [r[...] for r in refs[nr:nr + npar]]
        ks = [r[...] for r in refs[nr + npar:nr + npar + nk]]
        cs = [r[...].astype(F32) for r in refs[nr + npar + nk:n_in]]
        outs, vjp = jax.vjp(lambda *a: fn(*a, *ks), *xs, *ps)
        grads = vjp(tuple(cs))
        o = n_in
        for j in range(nr):
            if need[j]:
                refs[o][...] = grads[j].astype(refs[o].dtype)
                o += 1
        for j in range(npar):
            g_ref = refs[o + j]

            @pl.when(i == 0)
            def _(g_ref=g_ref):
                g_ref[...] = jnp.zeros_like(g_ref)

            g_ref[...] += grads[nr + j]

    out_specs = [pl.BlockSpec((tm, w), lambda i: (i, 0)) for w, nd in zip(widths, need) if nd]
    out_shape = [jax.ShapeDtypeStruct((S, w), row_dtype) for w, nd in zip(widths, need) if nd]
    out_specs += [_whole_spec(p) for p in params]
    out_shape += [jax.ShapeDtypeStruct(p.shape, F32) for p in params]
    res = pl.pallas_call(
        body,
        grid=(S // tm,),
        in_specs=list(specs) + [_whole_spec(p) for p in params] + [_whole_spec(k) for k in consts] + list(cspecs),
        out_specs=out_specs,
        out_shape=out_shape,
        compiler_params=_cparams(("arbitrary",)),
        name=name,
    )(*arrs, *params, *consts, *carrs)
    nrow = sum(need)
    return list(res[:nrow]), list(res[nrow:])


def _rms(x, g):
    return x * lax.rsqrt(jnp.mean(x * x, axis=-1, keepdims=True) + NORM_EPS) * g


def _f_rms(x, g):
    return (_rms(x, g),)


def _f_rms_res(x, g):
    return _rms(x, g), x


def _f_ple(x, z, pp):
    return (x + _sigmoid(z) * pp,)


def _f_mix(h, sh, mu):
    return (h + (sh - h) * mu,)


def _f_logf(fz, bf):
    return (-_softplus(-(fz + bf)),)


def _f_rwkv_pre(hk, hw, ha, hg, w0, w2, a0, a2, g2, k_k, k_a, bd):
    wlog = -_softplus(-(w0 + bdot(jnp.tanh(hw), w2))) - 0.5
    a = _sigmoid(a0 + bdot(ha, a2))
    g = bdot(_sigmoid(hg), g2)
    kk = hk * k_k
    kk = kk / jnp.maximum(jnp.sqrt(_segsum(kk * kk, bd)), L2_EPS)
    k2 = hk * (1.0 + (a - 1.0) * k_a)
    decay = jnp.exp(-jnp.exp(wlog))
    return decay, k2, kk, kk * a, g


def _f_rwkv_post(y, r, k2, v, g, ln_w, ln_b, r_k, bd):
    mean = _segsum(y, bd) * (1.0 / HEAD_DIM)
    d = y - mean
    var = _segsum(d * d, bd) * (1.0 / HEAD_DIM)
    yn = d * lax.rsqrt(var + GN_EPS) * ln_w + ln_b
    yn = yn + _segsum(r * k2 * r_k, bd) * v
    return (yn * g,)


def loss_head(x, target, gf, tm=256):
    S, D = x.shape
    tm = min(tm, S)

    def f(xt, g, tt):
        err = _rms(xt, g) - tt
        return 0.5 * jnp.sum(err * err) * (1.0 / D)

    def body(x_ref, t_ref, g_ref, dx_ref, dg_ref, l_ref):
        i = pl.program_id(0)
        val, (dx, dg) = jax.value_and_grad(f, argnums=(0, 1))(x_ref[...], g_ref[...], t_ref[...])

        @pl.when(i == 0)
        def _():
            dg_ref[...] = jnp.zeros_like(dg_ref)
            l_ref[...] = jnp.zeros_like(l_ref)

        dx_ref[...] = dx
        dg_ref[...] += dg
        l_ref[...] += jnp.full(l_ref.shape, val, F32)

    row = pl.BlockSpec((tm, D), lambda i: (i, 0))
    vec = pl.BlockSpec((1, D), lambda i: (0, 0))
    return pl.pallas_call(
        body,
        grid=(S // tm,),
        in_specs=[row, row, vec],
        out_specs=[row, vec, pl.BlockSpec((1, 128), lambda i: (0, 0))],
        out_shape=[jax.ShapeDtypeStruct((S, D), F32), jax.ShapeDtypeStruct((1, D), F32),
                   jax.ShapeDtypeStruct((1, 128), F32)],
        compiler_params=_cparams(("arbitrary",)),
        name="loss_head",
    )(x, target, gf)


def _swa_block(q, kp, kc, vp, vc, sink, slope, n):
    k = jnp.concatenate([kp, kc], axis=0)
    v = jnp.concatenate([vp, vc], axis=0)
    rows = q.shape[0]
    logits = bdot_nt(q, k) * (HEAD_DIM ** -0.5)
    qi = lax.broadcasted_iota(jnp.int32, (rows, 2 * BLOCK), 0) & (BLOCK - 1)
    ki = lax.broadcasted_iota(jnp.int32, (rows, 2 * BLOCK), 1)
    dist = qi + BLOCK - ki
    valid = (dist >= 0) & (dist < BLOCK) & ((n - 1) * BLOCK + ki >= 0)
    logits = logits - slope * dist.astype(F32)
    logits = jnp.where(valid, logits, -jnp.inf)
    m = jnp.maximum(jnp.max(logits, axis=-1, keepdims=True), sink)
    pr = jnp.exp(logits - m)
    denom = jnp.sum(pr, axis=-1, keepdims=True) + jnp.exp(sink - m)
    return bdot(pr / denom, v)


def _swa_specs(S):
    nb = S // BLOCK
    q_spec = pl.BlockSpec((None, SWA_GROUP, BLOCK, HEAD_DIM), lambda h, n: (h, 0, n, 0))
    kc_spec = pl.BlockSpec((None, BLOCK, HEAD_DIM), lambda h, n: (h, n, 0))
    kp_spec = pl.BlockSpec((None, BLOCK, HEAD_DIM), lambda h, n: (h, jnp.maximum(n - 1, 0), 0))
    col_spec = pl.BlockSpec((None, SWA_GROUP * BLOCK, 1), lambda h, n: (h, 0, 0))
    return nb, q_spec, kp_spec, kc_spec, col_spec


def swa_fwd(q, k, v, sink_col, slope_col):
    S = q.shape[2]
    nb, q_spec, kp_spec, kc_spec, col_spec = _swa_specs(S)

    def body(q_ref, kp_ref, kc_ref, vp_ref, vc_ref, s_ref, a_ref, o_ref):
        n = pl.program_id(1)
        qq = q_ref[...].reshape(SWA_GROUP * BLOCK, HEAD_DIM)
        out = _swa_block(qq, kp_ref[...], kc_ref[...], vp_ref[...], vc_ref[...], s_ref[...], a_ref[...], n)
        o_ref[...] = out.reshape(SWA_GROUP, BLOCK, HEAD_DIM)

    return pl.pallas_call(
        body,
        grid=(SWA_KV_HEADS, nb),
        in_specs=[q_spec, kp_spec, kc_spec, kp_spec, kc_spec, col_spec, col_spec],
        out_specs=q_spec,
        out_shape=jax.ShapeDtypeStruct(q.shape, F32),
        compiler_params=_cparams(("parallel", "parallel")),
        name="swa_fwd",
    )(q, k, k, v, v, sink_col, slope_col)


def swa_bwd(q, k, v, sink_col, slope_col, dout):
    S = q.shape[2]
    nb, q_spec, kp_spec, kc_spec, col_spec = _swa_specs(S)

    def body(q_ref, kp_ref, kc_ref, vp_ref, vc_ref, s_ref, a_ref, do_ref,
             dq_ref, dkp_ref, dkc_ref, dvp_ref, dvc_ref, ds_ref):
        n = pl.program_id(1)
        qq = q_ref[...].reshape(SWA_GROUP * BLOCK, HEAD_DIM)
        slope = a_ref[...]
        f = lambda a, b, c, d, e, s: _swa_block(a, b, c, d, e, s, slope, n)
        _, vjp = jax.vjp(f, qq, kp_ref[...], kc_ref[...], vp_ref[...], vc_ref[...], s_ref[...])
        dq, dkp, dkc, dvp, dvc, ds = vjp(do_ref[...].reshape(SWA_GROUP * BLOCK, HEAD_DIM))
        dq_ref[...] = dq.reshape(SWA_GROUP, BLOCK, HEAD_DIM)
        dkp_ref[...] = dkp
        dkc_ref[...] = dkc
        dvp_ref[...] = dvp
        dvc_ref[...] = dvc

        @pl.when(n == 0)
        def _():
            ds_ref[...] = jnp.zeros_like(ds_ref)

        ds_ref[...] += ds

    kv_shape = jax.ShapeDtypeStruct(k.shape, F32)
    return pl.pallas_call(
        body,
        grid=(SWA_KV_HEADS, nb),
        in_specs=[q_spec, kp_spec, kc_spec, kp_spec, kc_spec, col_spec, col_spec, q_spec],
        out_specs=[q_spec, kc_spec, kc_spec, kc_spec, kc_spec, col_spec],
        out_shape=[jax.ShapeDtypeStruct(q.shape, F32), kv_shape, kv_shape, kv_shape, kv_shape,
                   jax.ShapeDtypeStruct(sink_col.shape, F32)],
        compiler_params=_cparams(("parallel", "arbitrary")),
        name="swa_bwd",
    )(q, k, k, v, v, sink_col, slope_col, dout)


def _split2(x):
    hi = x.astype(BF16)
    return (x - hi.astype(F32)).astype(BF16), hi


def _dot2_many(xs, m):
    rows = xs[0].shape[0]
    res = jnp.dot(jnp.concatenate([p for x in xs for p in _split2(x)], axis=0), m, preferred_element_type=F32)
    return [res[(2 * i) * rows:(2 * i + 1) * rows] + res[(2 * i + 1) * rows:(2 * i + 2) * rows]
            for i in range(len(xs))]


def _dot2(x, m):
    return _dot2_many([x], m)[0]


def _seg_sums(xs, bd):
    w = bd.shape[0]
    halves = _dot2_many([x[:, i:i + w] for x in xs for i in range(0, x.shape[1], w)], bd)
    n = xs[0].shape[1] // w
    return [jnp.concatenate(halves[i * n:(i + 1) * n], axis=1) for i in range(len(xs))]


def _seg_sum(x, bd):
    return _seg_sums([x], bd)[0]


def _scan_consts():
    r = np.arange(256)
    bd = (r[:, None] // HEAD_DIM == r[None, :] // HEAD_DIM).astype(np.float32)
    c = np.arange(RWKV_DIM)
    e = (np.arange(HEAD_DIM)[:, None] // SCAN_GROUP == c[None, :] // HEAD_DIM).astype(np.float32)
    diag = (np.arange(HEAD_DIM)[:, None] == c[None, :] % HEAD_DIM).astype(np.float32)
    return jnp.asarray(bd, BF16), jnp.asarray(e, BF16), jnp.asarray(diag, F32)


def _to_colblocks(a):
    S = a.shape[0]
    a = a.reshape(S // SCAN_GROUP, SCAN_GROUP, RWKV_HEADS, HEAD_DIM)
    return a.transpose(0, 3, 2, 1).reshape(S // SCAN_GROUP, HEAD_DIM, RWKV_HEADS * SCAN_GROUP)


def _roll_up(rows):
    return pltpu.roll(rows, rows.shape[0] - 1, 0)


def _scan_pair_rows(aux, base, kk_ref, w_ref, b_ref, k_ref, bd):
    G = SCAN_GROUP
    kk_nx = _roll_up(kk_ref[pl.ds(base, G), :])
    aux[0] = w_ref[pl.ds(base, G), :] * kk_nx
    aux[1], aux[2] = _seg_sums([b_ref[pl.ds(base, G), :] * kk_nx, k_ref[pl.ds(base, G), :] * kk_nx], bd)


def _scan_pair(St, t0, base, col_g, lane_t, aux, kk_ref, w_ref, b_ref, k_ref, bd, e):
    t1 = t0 + 1
    row = lambda ref, t: ref[pl.ds(base + t, 1), :]
    arow = lambda i: aux[i, pl.ds(t0, 1), :]
    u0, m1 = _seg_sums([St * row(kk_ref, t0), St * arow(0)], bd)
    v0, v1 = _dot2_many([jnp.where(lane_t == t0, col_g, 0.0), jnp.where(lane_t == t1, col_g, 0.0)], e)
    u1 = m1 - u0 * arow(1) + v0 * arow(2)
    S0 = St * row(w_ref, t0) - u0 * row(b_ref, t0) + v0 * row(k_ref, t0)
    S1 = S0 * row(w_ref, t1) - u1 * row(b_ref, t1) + v1 * row(k_ref, t1)
    return (S0, S1), (u0, u1), (v0, v1)


def rwkv_scan_fwd(r, w, k, kk, b, vB, gather_srcs):
    S, C = r.shape
    N, G = HEAD_DIM, SCAN_GROUP
    chunk = min(SCAN_CHUNK, S)
    nchunk, ng = S // chunk, chunk // G
    bd, e, diag = _scan_consts()

    nx = len(gather_srcs)

    def body(*refs):
        r_ref, w_ref, k_ref, kk_ref, b_ref, vB_ref, bd_ref, e_ref, dg_ref = refs[:9]
        y_ref, ck_ref = refs[9 + nx:11 + nx]
        S_ref, aux, send_sems, recv_sems, local_sems = refs[11 + 2 * nx:]
        c = pl.program_id(0)
        _exchange_during(c, nchunk, True, refs[9:9 + nx], refs[11 + nx:11 + 2 * nx], send_sems, recv_sems, local_sems)

        @pl.when(c == 0)
        def _():
            S_ref[...] = jnp.zeros_like(S_ref)

        ck_ref[...] = S_ref[...]
        sub = lax.broadcasted_iota(jnp.int32, (G, C), 0)
        lane_t = lax.broadcasted_iota(jnp.int32, (N, N), 1) & (G - 1)

        def group(g, St):
            base = pl.multiple_of(g * G, G)
            vb = vB_ref[g]
            _scan_pair_rows(aux, base, kk_ref, w_ref, b_ref, k_ref, bd_ref[...])
            ys = jnp.zeros((G, C), F32)
            def emit(ys, states, t0):
                steps = (t0, t0 + 1)
                y_bs = _seg_sums([S_t * r_ref[pl.ds(base + tt, 1), :] for S_t, tt in zip(states, steps)], bd_ref[...])
                for y_b, tt in zip(y_bs, steps):
                    ys = jnp.where(sub == tt, jnp.sum(y_b * dg_ref[...], axis=0, keepdims=True), ys)
                return ys

            pending = None
            for t0 in range(0, G, 2):
                states, _, _ = _scan_pair(St, t0, base, vb, lane_t, aux, kk_ref, w_ref, b_ref, k_ref, bd_ref[...],
                                          e_ref[...])
                if pending is not None:
                    ys = emit(ys, *pending)
                pending = (states, t0)
                St = states[1]
            y_ref[pl.ds(base, G), :] = emit(ys, *pending)
            return St

        S_ref[...] = lax.fori_loop(0, ng, group, S_ref[...])

    row = pl.BlockSpec((chunk, C), lambda c: (c, 0))
    col = pl.BlockSpec((ng, N, N), lambda c: (c, 0, 0))
    res = pl.pallas_call(
        body,
        grid=(nchunk,),
        in_specs=[row] * 5 + [col, _whole_spec(bd), _whole_spec(e), _whole_spec(diag)] + [_ANY] * nx,
        out_specs=[row, pl.BlockSpec((None, N, C), lambda c: (c, 0, 0))] + [_ANY] * nx,
        out_shape=[jax.ShapeDtypeStruct((S, C), F32), jax.ShapeDtypeStruct((nchunk, N, C), F32)]
        + _exchange_out_shapes(True, gather_srcs),
        scratch_shapes=[pltpu.VMEM((N, C), F32), pltpu.VMEM((3, G, C), F32)] + _exchange_sems(nx),
        compiler_params=_cparams(("arbitrary",)),
        name="rwkv_scan_fwd",
    )(r, w, k, kk, b, vB, bd, e, diag, *gather_srcs)
    return res[0], res[1], list(res[2:])


def rwkv_scan_bwd(r, w, k, kk, b, vB, dyB, ckpt, scatter_srcs):
    S, C = r.shape
    N, G = HEAD_DIM, SCAN_GROUP
    chunk = min(SCAN_CHUNK, S)
    nchunk, ng = S // chunk, chunk // G
    bd, e, diag = _scan_consts()

    nx = len(scatter_srcs)

    def body(*refs):
        r_ref, w_ref, k_ref, kk_ref, b_ref, vB_ref, dyB_ref, ck_ref, bd_ref, e_ref, dg_ref = refs[:11]
        dr_ref, dw_ref, dk_ref, dkk_ref, db_ref, dv_ref = refs[11 + nx:17 + nx]
        G_ref, sbuf, ubuf, vbuf, aux, send_sems, recv_sems, local_sems = refs[17 + 2 * nx:]
        c = pl.program_id(0)
        _exchange_during(c, nchunk, False, refs[11:11 + nx], refs[17 + nx:17 + 2 * nx], send_sems, recv_sems,
                         local_sems)

        @pl.when(c == 0)
        def _():
            G_ref[...] = jnp.zeros_like(G_ref)

        lane_t = lax.broadcasted_iota(jnp.int32, (N, N), 1) & (G - 1)
        sub = lax.broadcasted_iota(jnp.int32, (G, C), 0)

        def fgroup(g, St):
            base = pl.multiple_of(g * G, G)
            vb = vB_ref[g]
            _scan_pair_rows(aux, base, kk_ref, w_ref, b_ref, k_ref, bd_ref[...])
            for t0 in range(0, G, 2):
                states, us, vs = _scan_pair(St, t0, base, vb, lane_t, aux, kk_ref, w_ref, b_ref, k_ref, bd_ref[...],
                                            e_ref[...])
                for i, S_before in enumerate((St, states[0])):
                    sbuf[base + t0 + i] = S_before
                    ubuf[base + t0 + i] = us[i]
                    vbuf[base + t0 + i] = vs[i]
                St = states[1]
            return St

        sbuf[chunk] = lax.fori_loop(0, ng, fgroup, ck_ref[...])

        def bgroup(gi, Gt):
            g = ng - 1 - gi
            base = pl.multiple_of(g * G, G)
            dyb = dyB_ref[g]
            rows = [jnp.zeros((G, C), F32) for _ in range(6)]
            colsum = lambda a: jnp.sum(a, axis=0, keepdims=True)
            row = lambda ref, t: ref[pl.ds(base + t, 1), :]
            b8 = b_ref[pl.ds(base, G), :]
            aux[0] = _roll_up(w_ref[pl.ds(base, G), :]) * b8
            aux[1], aux[2] = _seg_sums([_roll_up(kk_ref[pl.ds(base, G), :]) * b8, r_ref[pl.ds(base, G), :] * b8],
                                       bd_ref[...])

            def emit(rows, steps):
                d_vs = _seg_sums([Gt_ * row(k_ref, tt) for tt, Gt_, _, _ in steps], bd_ref[...])
                for (tt, Gt_, du_b, dy_b), d_vb in zip(steps, d_vs):
                    Sp, Sc = sbuf[base + tt], sbuf[base + tt + 1]
                    new = (colsum(Sc * dy_b), colsum(Gt_ * Sp), colsum(Gt_ * vbuf[base + tt]), colsum(Sp * du_b),
                           -colsum(Gt_ * ubuf[base + tt]), colsum(d_vb * dg_ref[...]))
                    rows = [jnp.where(sub == tt, n_, acc) for n_, acc in zip(new, rows)]
                return rows

            pending = None
            for t0 in reversed(range(0, G, 2)):
                t1 = t0 + 1
                arow = lambda i: aux[i, pl.ds(t0, 1), :]
                dy1, dy0 = _dot2_many([jnp.where(lane_t == t1, dyb, 0.0), jnp.where(lane_t == t0, dyb, 0.0)],
                                      e_ref[...])
                G1 = Gt + dy1 * row(r_ref, t1)
                m1, m2 = _seg_sums([G1 * row(b_ref, t1), G1 * arow(0)], bd_ref[...])
                du1 = -m1
                du0 = -(m2 + du1 * arow(1) + dy0 * arow(2))
                G0 = G1 * row(w_ref, t1) + du1 * row(kk_ref, t1) + dy0 * row(r_ref, t0)
                G_next = G0 * row(w_ref, t0) + du0 * row(kk_ref, t0)
                if pending is not None:
                    rows = emit(rows, pending)
                pending = ((t1, G1, du1, dy1), (t0, G0, du0, dy0))
                Gt = G_next
            rows = emit(rows, pending)
            for ref, val in zip((dr_ref, dw_ref, dk_ref, dkk_ref, db_ref, dv_ref), rows):
                ref[pl.ds(base, G), :] = val
            return Gt

        G_ref[...] = lax.fori_loop(0, ng, bgroup, G_ref[...])

    rev = lambda c: nchunk - 1 - c
    row = pl.BlockSpec((chunk, C), lambda c: (rev(c), 0))
    col = pl.BlockSpec((ng, N, N), lambda c: (rev(c), 0, 0))
    rshape = jax.ShapeDtypeStruct((S, C), F32)
    res = pl.pallas_call(
        body,
        grid=(nchunk,),
        in_specs=[row] * 5 + [col, col, pl.BlockSpec((None, N, C), lambda c: (rev(c), 0, 0)),
                              _whole_spec(bd), _whole_spec(e), _whole_spec(diag)] + [_ANY] * nx,
        out_specs=[row] * 6 + [_ANY] * nx,
        out_shape=[rshape] * 6 + _exchange_out_shapes(False, scatter_srcs),
        scratch_shapes=[pltpu.VMEM((N, C), F32), pltpu.VMEM((chunk + 1, N, C), F32),
                        pltpu.VMEM((chunk, N, C), F32), pltpu.VMEM((chunk, N, C), F32),
                        pltpu.VMEM((3, G, C), F32)] + _exchange_sems(nx),
        compiler_params=_cparams(("arbitrary",)),
        name="rwkv_scan_bwd",
    )(r, w, k, kk, b, vB, dyB, ckpt, bd, e, diag, *scatter_srcs)
    return tuple(res[:6]) + (list(res[6:]),)


def seq_cumsum(x, reverse, name):
    S, C = x.shape
    tb = min(256, S)
    nb = S // tb

    def body(x_ref, o_ref, carry):
        i = pl.program_id(0)

        @pl.when(i == 0)
        def _():
            carry[...] = jnp.zeros_like(carry)

        ri = lax.broadcasted_iota(jnp.int32, (tb, tb), 0)
        ci = lax.broadcasted_iota(jnp.int32, (tb, tb), 1)
        tri = jnp.where((ci >= ri) if reverse else (ci <= ri), 1.0, 0.0).astype(F32)
        xb = x_ref[...]
        out = jnp.dot(tri, xb, precision=lax.Precision.HIGHEST, preferred_element_type=F32) + carry[...]
        o_ref[...] = out
        carry[...] = carry[...] + jnp.sum(xb, axis=0, keepdims=True)

    idx = (lambda i: (nb - 1 - i, 0)) if reverse else (lambda i: (i, 0))
    return pl.pallas_call(
        body,
        grid=(nb,),
        in_specs=[pl.BlockSpec((tb, C), idx)],
        out_specs=pl.BlockSpec((tb, C), idx),
        out_shape=jax.ShapeDtypeStruct((S, C), F32),
        scratch_shapes=[pltpu.VMEM((1, C), F32)],
        compiler_params=_cparams(("arbitrary",)),
        name=name,
    )(x)


def _fox_logits(q, k, cq, ck, diagonal):
    s = _dot(q, k, _NT) * (HEAD_DIM ** -0.5) + cq - ck
    if not diagonal:
        return s
    row = lax.broadcasted_iota(jnp.int32, s.shape, 0)
    col = lax.broadcasted_iota(jnp.int32, s.shape, 1)
    return jnp.where(col <= row, s, -jnp.inf)


def _fox_tiles(n, by_query):
    pairs = [(i, j) for i in range(n) for j in range(i + 1)] if by_query else \
            [(i, j) for j in range(n) for i in range(j, n)]
    return (jnp.asarray(np.array([p[0] for p in pairs], np.int32)),
            jnp.asarray(np.array([p[1] for p in pairs], np.int32)))


def _fox_specs(t, Dh):
    qs = pl.BlockSpec((None, t, Dh), lambda h, s, qt, kt: (h, qt[s], 0))
    ks = pl.BlockSpec((None, t, Dh), lambda h, s, qt, kt: (h, kt[s], 0))
    cqs = pl.BlockSpec((None, t, 1), lambda h, s, qt, kt: (h, qt[s], 0))
    cks = pl.BlockSpec((None, 1, t), lambda h, s, qt, kt: (h, 0, kt[s]))
    return qs, ks, cqs, cks


def _fox_call(body, tiles, Hh, in_specs, out_specs, out_shape, scratch, name, args):
    spec = pltpu.PrefetchScalarGridSpec(num_scalar_prefetch=2, grid=(Hh, tiles[0].shape[0]), in_specs=in_specs,
                                        out_specs=out_specs, scratch_shapes=scratch)
    return pl.pallas_call(body, grid_spec=spec, out_shape=out_shape,
                          compiler_params=_cparams(("parallel", "arbitrary")), name=name)(*tiles, *args)


def fox_fwd(q, k, v, c_col, c_row):
    Hh, S, Dh = q.shape
    tq = tk = min(512, S)

    def body(qt_ref, kt_ref, q_ref, k_ref, v_ref, cq_ref, ck_ref, o_ref, lse_ref, m_s, l_s, acc_s):
        qi, ki = qt_ref[pl.program_id(1)], kt_ref[pl.program_id(1)]

        @pl.when(ki == 0)
        def _():
            m_s[...] = jnp.full_like(m_s, -jnp.inf)
            l_s[...] = jnp.zeros_like(l_s)
            acc_s[...] = jnp.zeros_like(acc_s)

        def tile(diagonal):
            s = _fox_logits(q_ref[...], k_ref[...], cq_ref[...], ck_ref[...], diagonal)
            m_old = m_s[...]
            m_new = jnp.maximum(m_old, jnp.max(s, axis=-1, keepdims=True))
            alpha = jnp.exp(m_old - m_new)
            p = jnp.exp(s - m_new)
            l_s[...] = alpha * l_s[...] + jnp.sum(p, axis=-1, keepdims=True)
            acc_s[...] = alpha * acc_s[...] + _dot(p, v_ref[...], _NN)
            m_s[...] = m_new

        @pl.when(ki != qi)
        def _():
            tile(False)

        @pl.when(ki == qi)
        def _():
            tile(True)
            o_ref[...] = acc_s[...] / l_s[...]
            lse_ref[...] = m_s[...] + jnp.log(l_s[...])

    qs, ks, cqs, cks = _fox_specs(tq, Dh)
    return _fox_call(
        body, _fox_tiles(S // tq, True), Hh, [qs, ks, ks, cqs, cks], [qs, cqs],
        [jax.ShapeDtypeStruct((Hh, S, Dh), F32), jax.ShapeDtypeStruct((Hh, S, 1), F32)],
        [pltpu.VMEM((tq, 1), F32), pltpu.VMEM((tq, 1), F32), pltpu.VMEM((tq, Dh), F32)],
        "fox_fwd", (q, k, v, c_col, c_row))


def fox_bwd(q, k, v, c_col, c_row, o, lse, do):
    Hh, S, Dh = q.shape
    tq = tk = min(512, S)
    nk = S // tk

    def body(qt_ref, kt_ref, q_ref, k_ref, v_ref, cq_ref, ck_ref, o_ref, lse_ref, do_ref,
             dq_ref, dr_ref, dk_ref, dv_ref, dc_ref, acc_s, row_s):
        step = pl.program_id(1)
        qi, ki = qt_ref[step], kt_ref[step]

        @pl.when(step == 0)
        def _():
            dk_ref[...] = jnp.zeros_like(dk_ref)
            dv_ref[...] = jnp.zeros_like(dv_ref)
            dc_ref[...] = jnp.zeros_like(dc_ref)

        @pl.when(ki == 0)
        def _():
            acc_s[...] = jnp.zeros_like(acc_s)
            row_s[...] = jnp.zeros_like(row_s)

        def tile(diagonal):
            q_t, kb, vb, do_t = q_ref[...], k_ref[...], v_ref[...], do_ref[...]
            s = _fox_logits(q_t, kb, cq_ref[...], ck_ref[...], diagonal)
            p = jnp.exp(s - lse_ref[...])
            delta = jnp.sum(do_t * o_ref[...], axis=-1, keepdims=True)
            ds = p * (_dot(do_t, vb, _NT) - delta)
            acc_s[...] += _dot(ds, kb, _NN)
            row_s[...] += jnp.sum(ds, axis=-1, keepdims=True)
            dk_ref[ki] += _dot(ds, q_t, _TN) * (HEAD_DIM ** -0.5)
            dv_ref[ki] += _dot(p, do_t, _TN)
            dc_ref[ki] += jnp.sum(ds, axis=0, keepdims=True)

        @pl.when(ki != qi)
        def _():
            tile(False)

        @pl.when(ki == qi)
        def _():
            tile(True)
            dq_ref[...] = acc_s[...] * (HEAD_DIM ** -0.5)
            dr_ref[...] = row_s[...]

    qs, ks, cqs, cks = _fox_specs(tq, Dh)
    head = lambda *blk: pl.BlockSpec((None,) + blk, lambda h, s, qt, kt: (h,) + (0,) * len(blk))
    dq, dr, dk, dv, dc = _fox_call(
        body, _fox_tiles(S // tq, True), Hh, [qs, ks, ks, cqs, cks, qs, cqs, qs],
        [qs, cqs, head(nk, tk, Dh), head(nk, tk, Dh), head(nk, 1, tk)],
        [jax.ShapeDtypeStruct((Hh, S, Dh), F32), jax.ShapeDtypeStruct((Hh, S, 1), F32),
         jax.ShapeDtypeStruct((Hh, nk, tk, Dh), F32), jax.ShapeDtypeStruct((Hh, nk, tk, Dh), F32),
         jax.ShapeDtypeStruct((Hh, nk, 1, tk), F32)],
        [pltpu.VMEM((tq, Dh), F32), pltpu.VMEM((tq, 1), F32)],
        "fox_bwd", (q, k, v, c_col, c_row, o, lse, do))
    return dq, dr, dk.reshape(Hh, S, Dh), dv.reshape(Hh, S, Dh), dc.reshape(Hh, 1, S)


def _heads(a, nh):
    S = a.shape[0]
    return a.reshape(S, nh, HEAD_DIM).transpose(1, 0, 2)


def _unheads(a):
    nh, S, _ = a.shape
    return a.transpose(1, 0, 2).reshape(S, nh * HEAD_DIM)


def _shift_down(a):
    return jnp.pad(a[:-1], ((1, 0), (0, 0)))


def _shift_up(a):
    return jnp.pad(a[1:], ((0, 1), (0, 0)))


def _block_diag_ones():
    i = np.arange(RWKV_DIM) // HEAD_DIM
    return jnp.asarray((i[:, None] == i[None, :]).astype(np.float32))


FFN_ROWS = 1024
FFN_COLS = 256


def _ffn_specs(S, F, tm, fc):
    nf = F // fc
    row = pl.BlockSpec((tm, D_MODEL), lambda i, j: (i, 0))
    vec = pl.BlockSpec((1, D_MODEL), lambda i, j: (0, 0))
    wg = pl.BlockSpec((D_MODEL, fc), lambda i, j: (0, j))
    wu = pl.BlockSpec((D_MODEL, fc), lambda i, j: (0, nf + j))
    wd = pl.BlockSpec((fc, D_MODEL), lambda i, j: (j, 0))
    hid = pl.BlockSpec((tm, fc), lambda i, j: (i, j))
    return nf, row, vec, wg, wu, wd, hid


def ffn_fwd(x, g_norm, w_gu, w_down, tag):
    S, F = x.shape[0], w_down.shape[0]
    tm, fc = min(FFN_ROWS, S), FFN_COLS
    nf, row, vec, wg, wu, wd, _ = _ffn_specs(S, F, tm, fc)

    def body(x_ref, g_ref, wg_ref, wu_ref, wd_ref, o_ref, hn_ref, hn_s, acc):
        j = pl.program_id(1)

        @pl.when(j == 0)
        def _():
            hn_s[...] = _rms(x_ref[...], g_ref[...]).astype(BF16)
            hn_ref[...] = hn_s[...]
            acc[...] = jnp.zeros_like(acc)

        g = _dot(hn_s[...], wg_ref[...], _NN)
        u = _dot(hn_s[...], wu_ref[...], _NN)
        acc[...] += _dot(g * _sigmoid_tanh(g) * u, wd_ref[...], _NN)

        @pl.when(j == nf - 1)
        def _():
            o_ref[...] = x_ref[...] + 0.5 * acc[...]

    out, hn = pl.pallas_call(
        body,
        grid=(S // tm, nf),
        in_specs=[row, vec, wg, wu, wd],
        out_specs=[row, row],
        out_shape=[jax.ShapeDtypeStruct((S, D_MODEL), F32), jax.ShapeDtypeStruct((S, D_MODEL), BF16)],
        scratch_shapes=[pltpu.VMEM((tm, D_MODEL), BF16), pltpu.VMEM((tm, D_MODEL), F32)],
        compiler_params=_cparams(("parallel", "arbitrary")),
        name=tag + "_fwd",
    )(x, g_norm, w_gu, w_gu, w_down)
    return out, (x, hn)


def ffn_bwd(dy, saved, g_norm, w_gu, w_down, tag):
    x, hn = saved
    S, F = x.shape[0], w_down.shape[0]
    tm, fc = min(FFN_ROWS, S), FFN_COLS
    nf, row, vec, wg, wu, wd, hid = _ffn_specs(S, F, tm, fc)

    def body(dy_ref, x_ref, hn_ref, g_ref, wg_ref, wu_ref, wd_ref, dx_ref, dgn_ref, a_ref, dg_ref, du_ref,
             dyh_s, dhn):
        i, j = pl.program_id(0), pl.program_id(1)

        @pl.when(j == 0)
        def _():
            dyh_s[...] = (0.5 * dy_ref[...]).astype(BF16)
            dhn[...] = jnp.zeros_like(dhn)

        hn_t = hn_ref[...]
        g = _dot(hn_t, wg_ref[...], _NN)
        u = _dot(hn_t, wu_ref[...], _NN)
        da = _dot(dyh_s[...], wd_ref[...], _NT)
        sig = _sigmoid_tanh(g)
        gs = g * sig
        a_ref[...] = (gs * u).astype(BF16)
        dg = ((da * u) * (sig + gs * (1.0 - sig))).astype(BF16)
        du = (da * gs).astype(BF16)
        dg_ref[...] = dg
        du_ref[...] = du
        dhn[...] += _dot(jnp.concatenate([dg, du], axis=1),
                         jnp.concatenate([wg_ref[...], wu_ref[...]], axis=1), _NT)

        @pl.when(j == nf - 1)
        def _():
            _, vjp_n = jax.vjp(_rms, x_ref[...], g_ref[...])
            dx, dgn = vjp_n(dhn[...])
            dx_ref[...] = dy_ref[...] + dx

            @pl.when(i == 0)
            def _():
                dgn_ref[...] = jnp.zeros_like(dgn_ref)

            dgn_ref[...] += dgn

    hshape = jax.ShapeDtypeStruct((S, F), BF16)
    dx, dgn, act, dg, du = pl.pallas_call(
        body,
        grid=(S // tm, nf),
        in_specs=[row, row, row, vec, wg, wu, wd],
        out_specs=[row, vec, hid, hid, hid],
        out_shape=[jax.ShapeDtypeStruct((S, D_MODEL), F32), jax.ShapeDtypeStruct((1, D_MODEL), F32),
                   hshape, hshape, hshape],
        scratch_shapes=[pltpu.VMEM((tm, D_MODEL), BF16), pltpu.VMEM((tm, D_MODEL), F32)],
        compiler_params=_cparams(("arbitrary", "arbitrary")),
        name=tag + "_bwd",
    )(dy, x, hn, g_norm, w_gu, w_gu, w_down)
    d_wdown = matmul(act, dy, "tn", tag + "_dwd", out_dtype=BF16, scale=0.5, tm=1408)
    d_wgu = jnp.concatenate([matmul(hn, dg, "tn", tag + "_dwg", out_dtype=BF16),
                             matmul(hn, du, "tn", tag + "_dwu", out_dtype=BF16)], axis=1)
    return dx, dgn, d_wgu, d_wdown


def ple_fwd(x, p_i, g_norm, w_gate, w_proj, tag):
    hn, = rowwise(_f_rms, [x], [g_norm], [(D_MODEL, BF16)], tag + "_rms")
    z = matmul(hn, w_gate, "nn", tag + "_gate")
    pp = matmul(p_i, w_proj, "nn", tag + "_proj")
    out, = rowwise(_f_ple, [x, z, pp], [], [(D_MODEL, F32)], tag + "_mix")
    return out, (x, hn, z, pp)


def ple_bwd(dy, saved, p_i, g_norm, w_gate, tag):
    x, hn, z, pp = saved
    (dz, dpp), _ = rowwise_vjp(_f_ple, [x, z, pp], [], [dy], tag + "_dmix", need=[False, True, True],
                               row_dtype=BF16)
    d_wproj = matmul(p_i, dpp, "tn", tag + "_dwp", out_dtype=BF16)
    d_wgate = matmul(hn, dz, "tn", tag + "_dwg", out_dtype=BF16)
    dhn = matmul(dz, w_gate, "nt", tag + "_dhn")
    (dx,), (dgn,) = rowwise_vjp(_f_rms_res, [x], [g_norm], [dhn, dy], tag + "_drms")
    return dx, dgn, d_wgate, d_wproj


def _swa_consts(sinks):
    slopes = np.asarray([2.0 ** (-(i + 1)) for i in range(SWA_HEADS)], np.float32)
    slope_col = jnp.asarray(np.repeat(slopes, BLOCK).reshape(SWA_KV_HEADS, SWA_GROUP * BLOCK, 1))
    sink_col = jnp.repeat(sinks.reshape(SWA_HEADS), BLOCK).reshape(SWA_KV_HEADS, SWA_GROUP * BLOCK, 1)
    return sink_col, slope_col


def even_mix_fwd(x, W, gather_src, later_weights):
    S = x.shape[0]
    hn, = rowwise(_f_rms, [x], [W["mix_norm0"]], [(D_MODEL, BF16)], "emix_rms")
    proj = matmul(hn, W["even_w_in"], "nn", "emix_in")
    qa = _heads(proj[:, :SWA_Q], SWA_HEADS).reshape(SWA_KV_HEADS, SWA_GROUP, S, HEAD_DIM)
    ka = _heads(proj[:, SWA_Q:SWA_Q + SWA_KV], SWA_KV_HEADS)
    va = _heads(proj[:, SWA_Q + SWA_KV:SWA_COLS], SWA_KV_HEADS)
    sink_col, slope_col = _swa_consts(W["swa_sinks"])
    ya = swa_fwd(qa, ka, va, sink_col, slope_col)
    ya = _unheads(ya.reshape(SWA_HEADS, S, HEAD_DIM))
    hb = proj[:, SWA_COLS:]
    h, = rowwise(_f_mix, [hb, _shift_down(hb)], [W["rwkv_mu"]], [(hb.shape[1], F32)], "rwkv_shift")
    hr, hk, hv = h[:, :512], h[:, 512:1024], h[:, 1024:1536]
    hw, ha, hg = h[:, 1536:1600], h[:, 1600:1664], h[:, 1664:1792]
    bd = _block_diag_ones()
    pre_params = [W["rwkv_w0"], W["rwkv_w2"], W["rwkv_a0"], W["rwkv_a2"], W["rwkv_g2"], W["rwkv_k_k"],
                  W["rwkv_k_a"]]
    decay, k2, kk, b, g = rowwise(_f_rwkv_pre, [hk, hw, ha, hg], pre_params + [bd],
                                  [(RWKV_DIM, F32)] * 5, "rwkv_pre")
    vT = _to_colblocks(hv)
    y, ckpt, gathered = rwkv_scan_fwd(hr, decay, k2, kk, b, vT, gather_src)
    late = later_weights(gathered)
    post_params = [W["rwkv_ln_w"], W["rwkv_ln_b"], W["rwkv_r_k"]]
    yb, = rowwise(_f_rwkv_post, [y, hr, k2, hv, g], post_params + [bd], [(RWKV_DIM, F32)], "rwkv_post")
    cat = jnp.concatenate([ya, yb], axis=1).astype(BF16)
    out = matmul(cat, late["even_w_out"], "nn", "emix_out", res=x)
    saved = (x, hn, qa, ka, va, sink_col, slope_col, hb, hr, hk, hv, hw, ha, hg, decay, k2, kk, b, g, vT,
             ckpt, y, cat)
    return out, saved, late


def even_mix_bwd(dy, saved, W, scatter_src):
    (x, hn, qa, ka, va, sink_col, slope_col, hb, hr, hk, hv, hw, ha, hg, decay, k2, kk, b, g, vT, ckpt, y,
     cat) = saved
    S = x.shape[0]
    grads = {}
    dcat = matmul(dy, W["even_w_out"], "nt", "emix_dcat")
    grads["even_w_out"] = matmul(cat, dy, "tn", "emix_dwout", out_dtype=BF16)
    dya, dyb = dcat[:, :SWA_Q], dcat[:, SWA_Q:]
    dya_h = _heads(dya, SWA_HEADS).reshape(SWA_KV_HEADS, SWA_GROUP, S, HEAD_DIM)
    dqa, dkp, dkc, dvp, dvc, dsink = swa_bwd(qa, ka, va, sink_col, slope_col, dya_h)
    shift_blk = lambda a: jnp.pad(a[:, BLOCK:], ((0, 0), (0, BLOCK), (0, 0)))
    dka = dkc + shift_blk(dkp)
    dva = dvc + shift_blk(dvp)
    grads["swa_sinks"] = dsink.reshape(SWA_HEADS, BLOCK).sum(axis=1).reshape(1, SWA_HEADS)
    dqa = _unheads(dqa.reshape(SWA_HEADS, S, HEAD_DIM))
    dka, dva = _unheads(dka), _unheads(dva)
    bd = _block_diag_ones()
    post_params = [W["rwkv_ln_w"], W["rwkv_ln_b"], W["rwkv_r_k"]]
    (d_y, d_r1, d_k2a, d_v1, d_g), (d_lnw, d_lnb, d_rk) = rowwise_vjp(
        _f_rwkv_post, [y, hr, k2, hv, g], post_params, [dyb], "rwkv_dpost", consts=[bd], tm=128)
    grads["rwkv_ln_w"], grads["rwkv_ln_b"], grads["rwkv_r_k"] = d_lnw, d_lnb, d_rk
    d_r2, d_w, d_k2b, d_kk, d_b, d_v2, exchanged = rwkv_scan_bwd(hr, decay, k2, kk, b, vT, _to_colblocks(d_y), ckpt,
                                                                  scatter_src)
    pre_params = [W["rwkv_w0"], W["rwkv_w2"], W["rwkv_a0"], W["rwkv_a2"], W["rwkv_g2"], W["rwkv_k_k"],
                  W["rwkv_k_a"]]
    (d_hk, d_hw, d_ha, d_hg), dpre = rowwise_vjp(
        _f_rwkv_pre, [hk, hw, ha, hg], pre_params, [d_w, d_k2a + d_k2b, d_kk, d_b, d_g], "rwkv_dpre",
        consts=[bd], tm=128)
    for nm, gval in zip(["rwkv_w0", "rwkv_w2", "rwkv_a0", "rwkv_a2", "rwkv_g2", "rwkv_k_k", "rwkv_k_a"], dpre):
        grads[nm] = gval
    d_h = jnp.concatenate([d_r1 + d_r2, d_hk, d_v1 + d_v2, d_hw, d_ha, d_hg], axis=1)
    (d_hb, d_sh), (d_mu,) = rowwise_vjp(_f_mix, [hb, _shift_down(hb)], [W["rwkv_mu"]], [d_h], "rwkv_dshift")
    grads["rwkv_mu"] = d_mu
    d_hb = d_hb + _shift_up(d_sh)
    dproj = jnp.concatenate([dqa, dka, dva, d_hb], axis=1).astype(BF16)
    grads["even_w_in"] = matmul(hn, dproj, "tn", "emix_dwin", out_dtype=BF16)
    dhn = matmul(dproj, W["even_w_in"], "nt", "emix_dhn")
    (dx,), (dgn,) = rowwise_vjp(_f_rms_res, [x], [W["mix_norm0"]], [dhn, dy], "emix_drms")
    grads["mix_norm0"] = dgn
    return dx, grads, exchanged


def odd_mix_fwd(x, W):
    S = x.shape[0]
    hn, = rowwise(_f_rms, [x], [W["mix_norm1"]], [(D_MODEL, BF16)], "omix_rms")
    proj = matmul(hn, W["fox_w_in"], "nn", "omix_in")
    q = _heads(proj[:, :FOX_DIM], FOX_HEADS).astype(BF16)
    k = _heads(proj[:, FOX_DIM:2 * FOX_DIM], FOX_HEADS).astype(BF16)
    v = _heads(proj[:, 2 * FOX_DIM:3 * FOX_DIM], FOX_HEADS).astype(BF16)
    fz = proj[:, 3 * FOX_DIM:]
    logf, = rowwise(_f_logf, [fz], [W["fox_b_f"]], [(128, F32)], "fox_logf")
    c = seq_cumsum(logf, False, "fox_cumsum")[:, :FOX_HEADS]
    c_col = c.T.reshape(FOX_HEADS, S, 1)
    c_row = c.T.reshape(FOX_HEADS, 1, S)
    o, lse = fox_fwd(q, k, v, c_col, c_row)
    yc = _unheads(o).astype(BF16)
    out = matmul(yc, W["fox_w_out"], "nn", "omix_out", res=x)
    return out, (x, hn, q, k, v, fz, c_col, c_row, o, lse, yc)


def odd_mix_bwd(dy, saved, W):
    x, hn, q, k, v, fz, c_col, c_row, o, lse, yc = saved
    S = x.shape[0]
    grads = {}
    dyc = matmul(dy, W["fox_w_out"], "nt", "omix_dyc")
    grads["fox_w_out"] = matmul(yc, dy, "tn", "omix_dwout", out_dtype=BF16)
    do = _heads(dyc, FOX_HEADS)
    dq, drow, dk, dv, dcol = fox_bwd(q, k, v, c_col, c_row, o, lse, do)
    dc = (drow.reshape(FOX_HEADS, S) - dcol.reshape(FOX_HEADS, S)).T
    dc = jnp.pad(dc, ((0, 0), (0, 128 - FOX_HEADS)))
    dlogf = seq_cumsum(dc, True, "fox_rcumsum")
    (dfz,), (dbf,) = rowwise_vjp(_f_logf, [fz], [W["fox_b_f"]], [dlogf], "fox_dlogf")
    grads["fox_b_f"] = dbf
    dproj = jnp.concatenate([_unheads(dq), _unheads(dk), _unheads(dv), dfz], axis=1).astype(BF16)
    grads["fox_w_in"] = matmul(hn, dproj, "tn", "omix_dwin", out_dtype=BF16)
    dhn = matmul(dproj, W["fox_w_in"], "nt", "omix_dhn")
    (dx,), (dgn,) = rowwise_vjp(_f_rms_res, [x], [W["mix_norm1"]], [dhn, dy], "omix_drms")
    grads["mix_norm1"] = dgn
    return dx, grads


def device_step(x, p, target, W, gather_src, layer1_weights, layer1_grads):
    W = dict(W)
    saved = []
    h = x
    for i in range(2):
        h, s1 = ffn_fwd(h, W[f"ffn1_norm{i}"], W[f"ffn1_w_gu{i}"], W[f"ffn1_w_down{i}"], f"ffn1_{i}")
        if i == 0:
            h, s2, late = even_mix_fwd(h, W, gather_src, layer1_weights)
            W.update(late)
        else:
            h, s2 = odd_mix_fwd(h, W)
        h, s3 = ffn_fwd(h, W[f"ffn2_norm{i}"], W[f"ffn2_w_gu{i}"], W[f"ffn2_w_down{i}"], f"ffn2_{i}")
        h, s4 = ple_fwd(h, p[i], W[f"ple_norm{i}"], W[f"ple_w_gate{i}"], W[f"ple_w_proj{i}"], f"ple_{i}")
        saved.append((s1, s2, s3, s4))
    dh, d_final, loss = loss_head(h, target, W["final_norm"])
    G = {"final_norm": d_final}
    for i in (1, 0):
        s1, s2, s3, s4 = saved[i]
        dh, G[f"ple_norm{i}"], G[f"ple_w_gate{i}"], G[f"ple_w_proj{i}"] = ple_bwd(
            dh, s4, p[i], W[f"ple_norm{i}"], W[f"ple_w_gate{i}"], f"ple_{i}")
        dh, G[f"ffn2_norm{i}"], G[f"ffn2_w_gu{i}"], G[f"ffn2_w_down{i}"] = ffn_bwd(
            dh, s3, W[f"ffn2_norm{i}"], W[f"ffn2_w_gu{i}"], W[f"ffn2_w_down{i}"], f"ffn2_{i}")
        if i == 0:
            dh, gm, exchanged = even_mix_bwd(dh, s2, W, layer1_grads(G))
        else:
            dh, gm = odd_mix_bwd(dh, s2, W)
        G.update(gm)
        dh, G[f"ffn1_norm{i}"], G[f"ffn1_w_gu{i}"], G[f"ffn1_w_down{i}"] = ffn_bwd(
            dh, s1, W[f"ffn1_norm{i}"], W[f"ffn1_w_gu{i}"], W[f"ffn1_w_down{i}"], f"ffn1_{i}")
    return loss, dh, G, exchanged


_MESH = pl.DeviceIdType.MESH
_ANY = pl.BlockSpec(memory_space=pl.ANY)


def _exchange_sems(n):
    return [pltpu.SemaphoreType.DMA((7 * n,)), pltpu.SemaphoreType.DMA((7 * n,)), pltpu.SemaphoreType.DMA((n,))]


def all_gather(xs, name):
    n = len(xs)

    def body(*refs):
        x_refs, out_refs = refs[:n], refs[n:2 * n]
        send_sems, recv_sems, local_sems = refs[2 * n:]
        x_, y_, c_ = lax.axis_index("x"), lax.axis_index("y"), lax.axis_index("c")
        me, sibling = (x_, y_, c_), (x_, y_, 1 - c_)
        chips = [(1 - x_, y_), (x_, 1 - y_), (1 - x_, 1 - y_)]

        def copy(b, k, block, to, from_input=False):
            slot = out_refs[b].at[4 * block[0] + 2 * block[1] + block[2]]
            return pltpu.make_async_remote_copy(
                src_ref=x_refs[b] if from_input else slot, dst_ref=slot,
                send_sem=send_sems.at[7 * b + k], recv_sem=recv_sems.at[7 * b + k], device_id=to,
                device_id_type=_MESH)

        bufs = range(n)
        mine = [pltpu.make_async_copy(x_refs[b], out_refs[b].at[4 * x_ + 2 * y_ + c_], local_sems.at[b]) for b in bufs]
        first = [copy(b, 0, me, sibling, True) for b in bufs]
        first += [copy(b, 1 + j, me, (*chip, c_), True) for j, chip in enumerate(chips) for b in bufs]
        for cp in mine + first:
            cp.start()
        passed = []
        for j, chip in enumerate(chips):
            for b in bufs:
                copy(b, 1 + j, (*chip, c_), me).wait_recv()
                passed.append(copy(b, 4 + j, (*chip, c_), sibling))
                passed[-1].start()
        for b in bufs:
            copy(b, 0, sibling, me).wait_recv()
            for j, chip in enumerate(chips):
                copy(b, 4 + j, (*chip, 1 - c_), me).wait_recv()
        for cp in first + passed:
            cp.wait_send()
        for cp in mine:
            cp.wait()

    return pl.pallas_call(
        body,
        out_shape=[jax.ShapeDtypeStruct((N_DEV,) + x.shape, x.dtype) for x in xs],
        in_specs=[_ANY] * n,
        out_specs=[_ANY] * n,
        scratch_shapes=_exchange_sems(n),
        name=name,
    )(*xs)


def _direct_exchange(gather, s_refs, r_refs, send_sems, recv_sems, local_sems):
    x_, y_, c_ = lax.axis_index("x"), lax.axis_index("y"), lax.axis_index("c")
    my = 4 * x_ + 2 * y_ + c_
    copies = []
    for b, (s_ref, r_ref) in enumerate(zip(s_refs, r_refs)):
        copies.append(pltpu.make_async_copy(s_ref if gather else s_ref.at[my], r_ref.at[my], local_sems.at[b]))
        for m in range(1, N_DEV):
            px = 1 - x_ if (m >> 2) & 1 else x_
            py = 1 - y_ if (m >> 1) & 1 else y_
            pc = 1 - c_ if m & 1 else c_
            copies.append(pltpu.make_async_remote_copy(
                src_ref=s_ref if gather else s_ref.at[4 * px + 2 * py + pc], dst_ref=r_ref.at[my],
                send_sem=send_sems.at[7 * b + m - 1], recv_sem=recv_sems.at[7 * b + m - 1],
                device_id=(px, py, pc), device_id_type=_MESH))
    return copies


def _exchange_during(step, n_steps, gather, s_refs, r_refs, send_sems, recv_sems, local_sems):
    copies = _direct_exchange(gather, s_refs, r_refs, send_sems, recv_sems, local_sems)

    @pl.when(step == 0)
    def _():
        for cp in copies:
            cp.start()

    @pl.when(step == n_steps - 1)
    def _():
        for cp in copies:
            cp.wait()


def _exchange_out_shapes(gather, srcs):
    return [jax.ShapeDtypeStruct(((N_DEV,) + s.shape) if gather else s.shape, s.dtype) for s in srcs]


def all_to_all(sends, name):
    n = len(sends)

    def body(*refs):
        copies = _direct_exchange(False, refs[:n], refs[n:2 * n], *refs[2 * n:])
        for cp in copies:
            cp.start()
        for cp in copies:
            cp.wait()

    return pl.pallas_call(
        body,
        out_shape=_exchange_out_shapes(False, sends),
        in_specs=[_ANY] * n,
        out_specs=[_ANY] * n,
        scratch_shapes=_exchange_sems(n),
        name=name,
    )(*sends)


def adamw(w, m, v, parts, name, tm=256):
    R, C = w.shape
    tm = _pick(R, tm, 8) if R >= 8 else R

    def body(w_ref, m_ref, v_ref, p_ref, g_ref, d_ref, nm_ref, nv_ref):
        g = p_ref[0].astype(F32)
        for s in range(1, N_DEV):
            g = g + p_ref[s].astype(F32)
        nm = ADAM_B1 * m_ref[...] + (1.0 - ADAM_B1) * g
        nv = ADAM_B2 * v_ref[...] + (1.0 - ADAM_B2) * (g * g)
        m_hat = nm / (1.0 - ADAM_B1 ** ADAM_STEP)
        v_hat = nv / (1.0 - ADAM_B2 ** ADAM_STEP)
        g_ref[...] = g
        d_ref[...] = -ADAM_LR * (m_hat / (jnp.sqrt(v_hat) + ADAM_EPS) + ADAM_WD * w_ref[...])
        nm_ref[...] = nm
        nv_ref[...] = nv

    row = pl.BlockSpec((tm, C), lambda i: (i, 0))
    out = jax.ShapeDtypeStruct((R, C), F32)
    return pl.pallas_call(
        body,
        grid=(R // tm,),
        in_specs=[row, row, row, pl.BlockSpec((N_DEV, tm, C), lambda i: (0, i, 0))],
        out_specs=[row] * 4,
        out_shape=[out] * 4,
        compiler_params=_cparams(("parallel",)),
        name=name,
    )(w, m, v, parts)


_WEIGHTS = ["ffn1_norm", "ffn1_w_gu", "ffn1_w_down", "mix_norm", "ffn2_norm", "ffn2_w_gu", "ffn2_w_down",
            "ple_norm", "ple_w_gate", "ple_w_proj", "even_w_in", "even_w_out", "swa_sinks", "rwkv_mu",
            "rwkv_w0", "rwkv_w2", "rwkv_a0", "rwkv_a2", "rwkv_g2", "rwkv_k_k", "rwkv_k_a", "rwkv_r_k",
            "rwkv_ln_w", "rwkv_ln_b", "fox_w_in", "fox_b_f", "fox_w_out", "final_norm"]
_SHARD_AXIS = {"ffn1_w_gu": 2, "ffn1_w_down": 1, "ffn2_w_gu": 2, "ffn2_w_down": 1, "ple_w_gate": 1,
               "ple_w_proj": 2, "even_w_in": 2, "even_w_out": 1, "rwkv_w2": 2, "rwkv_a2": 2, "rwkv_g2": 2,
               "fox_w_in": 2, "fox_w_out": 1}
_SHARDED = [n for n in _WEIGHTS if n in _SHARD_AXIS]
_REPLICATED = [n for n in _WEIGHTS if n not in _SHARD_AXIS]
_PER_LAYER = ("ffn1_w_gu", "ffn1_w_down", "ffn2_w_gu", "ffn2_w_down", "ple_w_gate", "ple_w_proj")
_ALL_PIECES = ([(n, 0) for n in _PER_LAYER] + [(n, 0) for n in ("even_w_in", "even_w_out", "rwkv_w2", "rwkv_a2", "rwkv_g2")]
               + [(n, 1) for n in _PER_LAYER] + [("fox_w_in", 0), ("fox_w_out", 0)])
_FIRST_WEIGHTS = [(n, 0) for n in ("ffn1_w_gu", "ffn1_w_down", "even_w_in", "rwkv_w2", "rwkv_a2", "rwkv_g2")]
_PIECES = [_FIRST_WEIGHTS, [pc for pc in _ALL_PIECES if pc not in _FIRST_WEIGHTS]]
_LATE_GRADS = _FIRST_WEIGHTS + [("even_w_out", 0)]
_GRAD_PIECES = [_LATE_GRADS, [pc for pc in _ALL_PIECES if pc not in _LATE_GRADS]]
_PACK_LANES = 1024
_PACK_ROW_TILE = 256


def _piece_key(piece):
    name, idx = piece
    return f"{name}{idx}" if name in _PER_LAYER else name


_KINDS = ("gu", "rows", "misc")


def _kind(piece):
    if piece[0] in ("ffn1_w_gu", "ffn2_w_gu"):
        return "gu"
    return "rows" if _SHARD_AXIS[piece[0]] == 1 else "misc"


def _of_kind(pieces, shapes, kind):
    return [(pc, shp) for pc, shp in zip(pieces, shapes) if _kind(pc) == kind]


def _pad_rows(flat, axis):
    pad = [(0, 0)] * flat.ndim
    pad[axis] = (0, -flat.shape[axis] % _PACK_ROW_TILE)
    return jnp.pad(flat, pad)


def _bundle(get, pieces, dtype):
    take = lambda kind: [get(pc).astype(dtype) for pc in pieces if _kind(pc) == kind]
    return [jnp.stack(take("gu")), jnp.concatenate(take("rows"), axis=0),
            _pad_rows(jnp.concatenate([a.reshape(-1, _PACK_LANES) for a in take("misc")], axis=0), 0)]


def _unbundle(bufs, pieces, shapes):
    out = {}
    gu = _of_kind(pieces, shapes, "gu")
    stacked = bufs[0].reshape((len(gu),) + gu[0][1])
    for j, (pc, _) in enumerate(gu):
        out[pc] = stacked[j]
    for buf, kind in ((bufs[1], "rows"), (bufs[2], "misc")):
        r0 = 0
        for pc, shp in _of_kind(pieces, shapes, kind):
            n = math.prod(shp) // _PACK_LANES
            out[pc] = buf[r0:r0 + n].reshape(shp)
            r0 += n
    return out


def _unshard(gathered, pieces, shapes):
    full = {}
    for j, (pc, shp) in enumerate(_of_kind(pieces, shapes, "gu")):
        full[_piece_key(pc)] = jnp.moveaxis(gathered[0][:, j], 0, 1).reshape(shp[0], N_DEV * shp[1])
    r0 = 0
    for pc, shp in _of_kind(pieces, shapes, "rows"):
        full[_piece_key(pc)] = gathered[1][:, r0:r0 + shp[0]].reshape(N_DEV * shp[0], shp[1])
        r0 += shp[0]
    r0 = 0
    for pc, shp in _of_kind(pieces, shapes, "misc"):
        n = math.prod(shp) // _PACK_LANES
        seg = gathered[2][:, r0:r0 + n].reshape((N_DEV,) + shp)
        full[_piece_key(pc)] = jnp.moveaxis(seg, 0, 1).reshape(shp[0], N_DEV * shp[1])
        r0 += n
    return full


def _to_shards(full, pieces, shapes):
    get = lambda pc: full[_piece_key(pc)].astype(BF16)
    cols = lambda pc, shp: jnp.moveaxis(get(pc).reshape(shp[0], N_DEV, shp[1]), 1, 0)
    gu = jnp.stack([cols(pc, shp) for pc, shp in _of_kind(pieces, shapes, "gu")], axis=1)
    rows = jnp.concatenate([get(pc).reshape((N_DEV,) + shp) for pc, shp in _of_kind(pieces, shapes, "rows")], axis=1)
    misc = jnp.concatenate([cols(pc, shp).reshape(N_DEV, -1, _PACK_LANES)
                            for pc, shp in _of_kind(pieces, shapes, "misc")], axis=1)
    return [gu, rows, _pad_rows(misc, 1)]


def _layer_weights(full):
    W = dict(full)
    if "fox_w_in" in W:
        W["fox_w_in"] = jnp.pad(W["fox_w_in"], ((0, 0), (0, FOX_IN_PAD - W["fox_w_in"].shape[1])))
    for n in ("rwkv_w2", "rwkv_a2", "rwkv_g2"):
        if n in W:
            W[n] = W[n].astype(F32)
    return W


def _pack_small(vals):
    flat = jnp.concatenate([v.reshape(1, -1) for v in vals], axis=1)
    n = flat.shape[1]
    return jnp.pad(flat, ((0, 0), (0, -n % 128)))


def _unpack_small(flat, shapes):
    out, c0 = [], 0
    for shp in shapes:
        n = math.prod(shp)
        out.append(flat[0, c0:c0 + n].reshape(shp))
        c0 += n
    return out


def kernel(x, p, ffn1_norm, ffn1_w_gu, ffn1_w_down, mix_norm, ffn2_norm, ffn2_w_gu, ffn2_w_down, ple_norm, ple_w_gate, ple_w_proj, even_w_in, even_w_out, swa_sinks, rwkv_mu, rwkv_w0, rwkv_w2, rwkv_a0, rwkv_a2, rwkv_g2, rwkv_k_k, rwkv_k_a, rwkv_r_k, rwkv_ln_w, rwkv_ln_b, fox_w_in, fox_b_f, fox_w_out, final_norm, loss_target, m_ffn1_norm, m_ffn1_w_gu, m_ffn1_w_down, m_mix_norm, m_ffn2_norm, m_ffn2_w_gu, m_ffn2_w_down, m_ple_norm, m_ple_w_gate, m_ple_w_proj, m_even_w_in, m_even_w_out, m_swa_sinks, m_rwkv_mu, m_rwkv_w0, m_rwkv_w2, m_rwkv_a0, m_rwkv_a2, m_rwkv_g2, m_rwkv_k_k, m_rwkv_k_a, m_rwkv_r_k, m_rwkv_ln_w, m_rwkv_ln_b, m_fox_w_in, m_fox_b_f, m_fox_w_out, m_final_norm, v_ffn1_norm, v_ffn1_w_gu, v_ffn1_w_down, v_mix_norm, v_ffn2_norm, v_ffn2_w_gu, v_ffn2_w_down, v_ple_norm, v_ple_w_gate, v_ple_w_proj, v_even_w_in, v_even_w_out, v_swa_sinks, v_rwkv_mu, v_rwkv_w0, v_rwkv_w2, v_rwkv_a0, v_rwkv_a2, v_rwkv_g2, v_rwkv_k_k, v_rwkv_k_a, v_rwkv_r_k, v_rwkv_ln_w, v_rwkv_ln_b, v_fox_w_in, v_fox_b_f, v_fox_w_out, v_final_norm):
    given = dict(locals())
    w = {n: given[n] for n in _WEIGHTS}
    m = {n: given["m_" + n] for n in _WEIGHTS}
    v = {n: given["v_" + n] for n in _WEIGHTS}
    small_shapes = [w[n].shape for n in _REPLICATED]
    piece = lambda d, pc: d[pc[0]][pc[1]]
    shapes = [[piece(w, pc).shape for pc in pieces] for pieces in _PIECES]
    gshapes = [[piece(w, pc).shape for pc in pieces] for pieces in _GRAD_PIECES]
    w_send = [_bundle(lambda pc: piece(w, pc), pieces, BF16) for pieces in _PIECES]

    W = _layer_weights(_unshard(all_gather(w_send[0], "weights_all_gather"), _PIECES[0], shapes[0]))
    for i in range(2):
        for n in ("ffn1_norm", "mix_norm", "ffn2_norm", "ple_norm"):
            W[f"{n}{i}"] = w[n][i].reshape(1, -1)
    for n in ("swa_sinks", "rwkv_mu", "rwkv_w0", "rwkv_a0", "rwkv_k_k", "rwkv_k_a", "rwkv_r_k", "rwkv_ln_w",
              "rwkv_ln_b", "final_norm"):
        W[n] = w[n].reshape(1, -1)
    n_f = fox_b_f.shape[1]
    W["fox_b_f"] = jnp.pad(fox_b_f.reshape(1, n_f), ((0, 0), (0, 128 - n_f)))
    n_fox = fox_w_in.shape[2] * N_DEV

    def layer1_weights(gathered):
        return _layer_weights(_unshard(gathered, _PIECES[1], shapes[1]))

    def early_grads(G):
        G = dict(G, fox_w_in=G["fox_w_in"][:, :n_fox])
        return _to_shards(G, _GRAD_PIECES[1], gshapes[1])

    loss_row, dx, G, parts_early = device_step(x[0], p[:, 0], loss_target[0], W, w_send[1], layer1_weights,
                                               early_grads)

    parts = [all_to_all(_to_shards(G, _GRAD_PIECES[0], gshapes[0]), "grads_all_to_all"), parts_early]
    out_g, out_d, out_m, out_v = {}, {}, {}, {}
    rows2d = lambda a, lead: a.reshape(a.shape[:lead] + (-1, a.shape[-1]))
    for li, pieces in enumerate(_GRAD_PIECES):
        wmv = [_bundle(lambda pc, d=d: piece(d, pc), pieces, F32) for d in (w, m, v)]
        res = [adamw(*[rows2d(b[ki], 0) for b in wmv], rows2d(parts[li][ki], 1), f"adamw_{kind}{li}")
               for ki, kind in enumerate(_KINDS)]
        for oi, out in enumerate((out_g, out_d, out_m, out_v)):
            for pc, a in _unbundle([r[oi] for r in res], pieces, gshapes[li]).items():
                out.setdefault(pc[0], {})[pc[1]] = a
    for out in (out_g, out_d, out_m, out_v):
        for n in _SHARDED:
            out[n] = jnp.stack([out[n][i] for i in sorted(out[n])])

    gsmall = {}
    for n in ("ffn1_norm", "mix_norm", "ffn2_norm", "ple_norm"):
        gsmall[n] = jnp.concatenate([G[f"{n}0"], G[f"{n}1"]], axis=0)
    for n in ("swa_sinks", "rwkv_mu", "rwkv_w0", "rwkv_a0", "rwkv_k_k", "rwkv_k_a", "rwkv_r_k", "rwkv_ln_w",
              "rwkv_ln_b", "final_norm"):
        gsmall[n] = G[n]
    gsmall["fox_b_f"] = G["fox_b_f"][:, :n_f]
    small = _pack_small([gsmall[n] for n in _REPLICATED] + [loss_row[:, :1]])
    small_parts = all_gather([small], "small_all_gather")[0]
    pad1 = lambda vals: _pack_small(vals + [jnp.zeros((1, 1), F32)])
    gs, ds, nms, nvs = adamw(pad1([w[n] for n in _REPLICATED]), pad1([m[n] for n in _REPLICATED]),
                             pad1([v[n] for n in _REPLICATED]), small_parts, "adamw_replicated")
    out_g.update(zip(_REPLICATED, _unpack_small(gs, small_shapes)))
    out_d.update(zip(_REPLICATED, _unpack_small(ds, small_shapes)))
    out_m.update(zip(_REPLICATED, _unpack_small(nms, small_shapes)))
    out_v.update(zip(_REPLICATED, _unpack_small(nvs, small_shapes)))
    n_small = sum(math.prod(s) for s in small_shapes)
    loss = gs[0, n_small]

    return (loss, dx[None], *[out_g[n] for n in _WEIGHTS], *[out_d[n] for n in _WEIGHTS],
            *[out_m[n] for n in _WEIGHTS], *[out_v[n] for n in _WEIGHTS])
```

```python
import functools
import math

import numpy as np
import jax
import jax.numpy as jnp
from jax import lax
from jax.experimental import pallas as pl
from jax.experimental.pallas import tpu as pltpu

F32 = jnp.float32
BF16 = jnp.bfloat16

D_MODEL = 1024
HEAD_DIM = 64
BLOCK = 128
SWA_HEADS = 8
SWA_KV_HEADS = 2
SWA_GROUP = 4
RWKV_HEADS = 8
RWKV_DIM = 512
FOX_HEADS = 16
FOX_DIM = 1024
D_FF = 2816
NORM_EPS = 1e-6
GN_EPS = 64e-5
L2_EPS = 1e-12
SWA_Q = 512
SWA_KV = 128
SWA_COLS = 768
FOX_IN_PAD = 3200
N_DEV = 8
ADAM_LR = 0.001
ADAM_B1 = 0.9
ADAM_B2 = 0.999
ADAM_EPS = 1e-08
ADAM_WD = 0.01
ADAM_STEP = 10

V7X_VMEM_LIMIT = 56 * 1024 * 1024
SCAN_GROUP = 8
SCAN_CHUNK = 32

_NN = (((1,), (0,)), ((), ()))
_NT = (((1,), (1,)), ((), ()))
_TN = (((0,), (0,)), ((), ()))
_DIMS = {"nn": _NN, "nt": _NT, "tn": _TN}


def _pick(n, target, mult=128):
    best = None
    for t in range(mult, min(n, target) + 1, mult):
        if n % t == 0:
            best = t
    return best or n


def _cparams(sem):
    return pltpu.CompilerParams(dimension_semantics=sem, vmem_limit_bytes=V7X_VMEM_LIMIT)


def _dot(a, b, dims):
    return lax.dot_general(a.astype(BF16), b.astype(BF16), dims, preferred_element_type=F32)


@jax.custom_vjp
def bdot(a, b):
    return _dot(a, b, _NN)


def _bdot_fwd(a, b):
    return _dot(a, b, _NN), (a, b)


def _bdot_bwd(res, g):
    a, b = res
    return _dot(g, b, _NT), _dot(a, g, _TN)


bdot.defvjp(_bdot_fwd, _bdot_bwd)


@jax.custom_vjp
def bdot_nt(a, b):
    return _dot(a, b, _NT)


def _bdot_nt_fwd(a, b):
    return _dot(a, b, _NT), (a, b)


def _bdot_nt_bwd(res, g):
    a, b = res
    return _dot(g, b, _NN), _dot(g, a, _TN)


bdot_nt.defvjp(_bdot_nt_fwd, _bdot_nt_bwd)


@jax.custom_vjp
def _segsum(x, bd):
    return _dot2(x, bd.astype(BF16))


def _segsum_fwd(x, bd):
    return _segsum(x, bd), bd


def _segsum_bwd(bd, g):
    return _dot2(g, bd.astype(BF16)), jnp.zeros_like(bd)


_segsum.defvjp(_segsum_fwd, _segsum_bwd)


def _sigmoid(x):
    return 1.0 / (1.0 + jnp.exp(-x))


def _sigmoid_tanh(x):
    return 0.5 * jnp.tanh(0.5 * x) + 0.5


def _softplus(x):
    return jnp.maximum(x, 0.0) + jnp.log(1.0 + jnp.exp(-jnp.abs(x)))


def matmul(a, b, mode, name, out_dtype=F32, scale=1.0, res=None, tm=512, tn=1408, tk=1024):
    if mode == "nn":
        (M, K), (K2, N) = a.shape, b.shape
    elif mode == "nt":
        (M, K), (N, K2) = a.shape, b.shape
    else:
        (K, M), (K2, N) = a.shape, b.shape
    assert K == K2, (a.shape, b.shape, mode)
    tm, tn, tk = _pick(M, tm), _pick(N, tn), _pick(K, tk)
    nk = K // tk
    has_res = res is not None

    def body(*refs):
        if has_res:
            a_ref, b_ref, r_ref, o_ref, acc = refs
        else:
            a_ref, b_ref, o_ref, acc = refs
        kk = pl.program_id(2)

        @pl.when(kk == 0)
        def _():
            acc[...] = jnp.zeros_like(acc)

        acc[...] += _dot(a_ref[...], b_ref[...], _DIMS[mode])

        @pl.when(kk == nk - 1)
        def _():
            v = acc[...]
            if scale != 1.0:
                v = v * scale
            if has_res:
                v = v + r_ref[...].astype(F32)
            o_ref[...] = v.astype(out_dtype)

    if mode == "tn":
        a_spec = pl.BlockSpec((tk, tm), lambda i, j, k: (k, i))
    else:
        a_spec = pl.BlockSpec((tm, tk), lambda i, j, k: (i, k))
    if mode == "nt":
        b_spec = pl.BlockSpec((tn, tk), lambda i, j, k: (j, k))
    else:
        b_spec = pl.BlockSpec((tk, tn), lambda i, j, k: (k, j))
    o_spec = pl.BlockSpec((tm, tn), lambda i, j, k: (i, j))
    in_specs = [a_spec, b_spec] + ([o_spec] if has_res else [])
    args = (a, b) + ((res,) if has_res else ())
    return pl.pallas_call(
        body,
        grid=(M // tm, N // tn, nk),
        in_specs=in_specs,
        out_specs=o_spec,
        out_shape=jax.ShapeDtypeStruct((M, N), out_dtype),
        scratch_shapes=[pltpu.VMEM((tm, tn), F32)],
        compiler_params=_cparams(("parallel", "parallel", "arbitrary")),
        name=name,
    )(*args)


def _row_spec(r, tm):
    if isinstance(r, tuple):
        arr, width, blk = r
        return arr, pl.BlockSpec((tm, width), lambda i, blk=blk: (i, blk))
    return r, pl.BlockSpec((tm, r.shape[1]), lambda i: (i, 0))


def _whole_spec(p):
    return pl.BlockSpec(p.shape, lambda i: (0,) * p.ndim)


def rowwise(fn, rows, params, outs, name, tm=256):
    arrs, specs = zip(*[_row_spec(r, tm) for r in rows])
    S = arrs[0].shape[0]
    tm = min(tm, S)
    arrs, specs = zip(*[_row_spec(r, tm) for r in rows])
    n_in = len(rows) + len(params)

    def body(*refs):
        res = fn(*[r[...] for r in refs[:n_in]])
        for o_ref, v in zip(refs[n_in:], res):
            o_ref[...] = v.astype(o_ref.dtype)

    return pl.pallas_call(
        body,
        grid=(S // tm,),
        in_specs=list(specs) + [_whole_spec(p) for p in params],
        out_specs=[pl.BlockSpec((tm, c), lambda i: (i, 0)) for c, _ in outs],
        out_shape=[jax.ShapeDtypeStruct((S, c), dt) for c, dt in outs],
        compiler_params=_cparams(("parallel",)),
        name=name,
    )(*arrs, *params)


def rowwise_vjp(fn, rows, params, cots, name, need=None, row_dtype=F32, consts=(), tm=256):
    nr, npar, nc, nk = len(rows), len(params), len(cots), len(consts)
    need = [True] * nr if need is None else need
    arrs, _ = zip(*[_row_spec(r, tm) for r in rows])
    S = arrs[0].shape[0]
    tm = min(tm, S)
    arrs, specs = zip(*[_row_spec(r, tm) for r in rows])
    carrs, cspecs = zip(*[_row_spec(c, tm) for c in cots])
    widths = [s.block_shape[1] for s in specs]
    n_in = nr + npar + nk + nc

    def body(*refs):
        i = pl.program_id(0)
        xs = [r[...].astype(F32) for r in refs[:nr]]
        ps = [r[...] for r in refs[nr:nr + npar]]
        ks = [r[...] for r in refs[nr + npar:nr + npar + nk]]
        cs = [r[...].astype(F32) for r in refs[nr + npar + nk:n_in]]
        outs, vjp = jax.vjp(lambda *a: fn(*a, *ks), *xs, *ps)
        grads = vjp(tuple(cs))
        o = n_in
        for j in range(nr):
            if need[j]:
                refs[o][...] = grads[j].astype(refs[o].dtype)
                o += 1
        for j in range(npar):
            g_ref = refs[o + j]

            @pl.when(i == 0)
            def _(g_ref=g_ref):
                g_ref[...] = jnp.zeros_like(g_ref)

            g_ref[...] += grads[nr + j]

    out_specs = [pl.BlockSpec((tm, w), lambda i: (i, 0)) for w, nd in zip(widths, need) if nd]
    out_shape = [jax.ShapeDtypeStruct((S, w), row_dtype) for w, nd in zip(widths, need) if nd]
    out_specs += [_whole_spec(p) for p in params]
    out_shape += [jax.ShapeDtypeStruct(p.shape, F32) for p in params]
    res = pl.pallas_call(
        body,
        grid=(S // tm,),
        in_specs=list(specs) + [_whole_spec(p) for p in params] + [_whole_spec(k) for k in consts] + list(cspecs),
        out_specs=out_specs,
        out_shape=out_shape,
        compiler_params=_cparams(("arbitrary",)),
        name=name,
    )(*arrs, *params, *consts, *carrs)
    nrow = sum(need)
    return list(res[:nrow]), list(res[nrow:])


def _rms(x, g):
    return x * lax.rsqrt(jnp.mean(x * x, axis=-1, keepdims=True) + NORM_EPS) * g


def _f_rms(x, g):
    return (_rms(x, g),)


def _f_rms_res(x, g):
    return _rms(x, g), x


def _f_ple(x, z, pp):
    return (x + _sigmoid(z) * pp,)


def _f_mix(h, sh, mu):
    return (h + (sh - h) * mu,)


def _f_logf(fz, bf):
    return (-_softplus(-(fz + bf)),)


def _f_rwkv_pre(hk, hw, ha, hg, w0, w2, a0, a2, g2, k_k, k_a, bd):
    wlog = -_softplus(-(w0 + bdot(jnp.tanh(hw), w2))) - 0.5
    a = _sigmoid(a0 + bdot(ha, a2))
    g = bdot(_sigmoid(hg), g2)
    kk = hk * k_k
    kk = kk / jnp.maximum(jnp.sqrt(_segsum(kk * kk, bd)), L2_EPS)
    k2 = hk * (1.0 + (a - 1.0) * k_a)
    decay = jnp.exp(-jnp.exp(wlog))
    return decay, k2, kk, kk * a, g


def _f_rwkv_post(y, r, k2, v, g, ln_w, ln_b, r_k, bd):
    mean = _segsum(y, bd) * (1.0 / HEAD_DIM)
    d = y - mean
    var = _segsum(d * d, bd) * (1.0 / HEAD_DIM)
    yn = d * lax.rsqrt(var + GN_EPS) * ln_w + ln_b
    yn = yn + _segsum(r * k2 * r_k, bd) * v
    return (yn * g,)


def loss_head(x, target, gf, tm=256):
    S, D = x.shape
    tm = min(tm, S)

    def f(xt, g, tt):
        err = _rms(xt, g) - tt
        return 0.5 * jnp.sum(err * err) * (1.0 / D)

    def body(x_ref, t_ref, g_ref, dx_ref, dg_ref, l_ref):
        i = pl.program_id(0)
        val, (dx, dg) = jax.value_and_grad(f, argnums=(0, 1))(x_ref[...], g_ref[...], t_ref[...])

        @pl.when(i == 0)
        def _():
            dg_ref[...] = jnp.zeros_like(dg_ref)
            l_ref[...] = jnp.zeros_like(l_ref)

        dx_ref[...] = dx
        dg_ref[...] += dg
        l_ref[...] += jnp.full(l_ref.shape, val, F32)

    row = pl.BlockSpec((tm, D), lambda i: (i, 0))
    vec = pl.BlockSpec((1, D), lambda i: (0, 0))
    return pl.pallas_call(
        body,
        grid=(S // tm,),
        in_specs=[row, row, vec],
        out_specs=[row, vec, pl.BlockSpec((1, 128), lambda i: (0, 0))],
        out_shape=[jax.ShapeDtypeStruct((S, D), F32), jax.ShapeDtypeStruct((1, D), F32),
                   jax.ShapeDtypeStruct((1, 128), F32)],
        compiler_params=_cparams(("arbitrary",)),
        name="loss_head",
    )(x, target, gf)


def _swa_block(q, kp, kc, vp, vc, sink, slope, n):
    k = jnp.concatenate([kp, kc], axis=0)
    v = jnp.concatenate([vp, vc], axis=0)
    rows = q.shape[0]
    logits = bdot_nt(q, k) * (HEAD_DIM ** -0.5)
    qi = lax.broadcasted_iota(jnp.int32, (rows, 2 * BLOCK), 0) & (BLOCK - 1)
    ki = lax.broadcasted_iota(jnp.int32, (rows, 2 * BLOCK), 1)
    dist = qi + BLOCK - ki
    valid = (dist >= 0) & (dist < BLOCK) & ((n - 1) * BLOCK + ki >= 0)
    logits = logits - slope * dist.astype(F32)
    logits = jnp.where(valid, logits, -jnp.inf)
    m = jnp.maximum(jnp.max(logits, axis=-1, keepdims=True), sink)
    pr = jnp.exp(logits - m)
    denom = jnp.sum(pr, axis=-1, keepdims=True) + jnp.exp(sink - m)
    return bdot(pr / denom, v)


def _swa_specs(S):
    nb = S // BLOCK
    q_spec = pl.BlockSpec((None, SWA_GROUP, BLOCK, HEAD_DIM), lambda h, n: (h, 0, n, 0))
    kc_spec = pl.BlockSpec((None, BLOCK, HEAD_DIM), lambda h, n: (h, n, 0))
    kp_spec = pl.BlockSpec((None, BLOCK, HEAD_DIM), lambda h, n: (h, jnp.maximum(n - 1, 0), 0))
    col_spec = pl.BlockSpec((None, SWA_GROUP * BLOCK, 1), lambda h, n: (h, 0, 0))
    return nb, q_spec, kp_spec, kc_spec, col_spec


def swa_fwd(q, k, v, sink_col, slope_col):
    S = q.shape[2]
    nb, q_spec, kp_spec, kc_spec, col_spec = _swa_specs(S)

    def body(q_ref, kp_ref, kc_ref, vp_ref, vc_ref, s_ref, a_ref, o_ref):
        n = pl.program_id(1)
        qq = q_ref[...].reshape(SWA_GROUP * BLOCK, HEAD_DIM)
        out = _swa_block(qq, kp_ref[...], kc_ref[...], vp_ref[...], vc_ref[...], s_ref[...], a_ref[...], n)
        o_ref[...] = out.reshape(SWA_GROUP, BLOCK, HEAD_DIM)

    return pl.pallas_call(
        body,
        grid=(SWA_KV_HEADS, nb),
        in_specs=[q_spec, kp_spec, kc_spec, kp_spec, kc_spec, col_spec, col_spec],
        out_specs=q_spec,
        out_shape=jax.ShapeDtypeStruct(q.shape, F32),
        compiler_params=_cparams(("parallel", "parallel")),
        name="swa_fwd",
    )(q, k, k, v, v, sink_col, slope_col)


def swa_bwd(q, k, v, sink_col, slope_col, dout):
    S = q.shape[2]
    nb, q_spec, kp_spec, kc_spec, col_spec = _swa_specs(S)

    def body(q_ref, kp_ref, kc_ref, vp_ref, vc_ref, s_ref, a_ref, do_ref,
             dq_ref, dkp_ref, dkc_ref, dvp_ref, dvc_ref, ds_ref):
        n = pl.program_id(1)
        qq = q_ref[...].reshape(SWA_GROUP * BLOCK, HEAD_DIM)
        slope = a_ref[...]
        f = lambda a, b, c, d, e, s: _swa_block(a, b, c, d, e, s, slope, n)
        _, vjp = jax.vjp(f, qq, kp_ref[...], kc_ref[...], vp_ref[...], vc_ref[...], s_ref[...])
        dq, dkp, dkc, dvp, dvc, ds = vjp(do_ref[...].reshape(SWA_GROUP * BLOCK, HEAD_DIM))
        dq_ref[...] = dq.reshape(SWA_GROUP, BLOCK, HEAD_DIM)
        dkp_ref[...] = dkp
        dkc_ref[...] = dkc
        dvp_ref[...] = dvp
        dvc_ref[...] = dvc

        @pl.when(n == 0)
        def _():
            ds_ref[...] = jnp.zeros_like(ds_ref)

        ds_ref[...] += ds

    kv_shape = jax.ShapeDtypeStruct(k.shape, F32)
    return pl.pallas_call(
        body,
        grid=(SWA_KV_HEADS, nb),
        in_specs=[q_spec, kp_spec, kc_spec, kp_spec, kc_spec, col_spec, col_spec, q_spec],
        out_specs=[q_spec, kc_spec, kc_spec, kc_spec, kc_spec, col_spec],
        out_shape=[jax.ShapeDtypeStruct(q.shape, F32), kv_shape, kv_shape, kv_shape, kv_shape,
                   jax.ShapeDtypeStruct(sink_col.shape, F32)],
        compiler_params=_cparams(("parallel", "arbitrary")),
        name="swa_bwd",
    )(q, k, k, v, v, sink_col, slope_col, dout)


def _split2(x):
    hi = x.astype(BF16)
    return (x - hi.astype(F32)).astype(BF16), hi


def _dot2_many(xs, m, single=False):
    rows = xs[0].shape[0]
    if single:
        res = jnp.dot(jnp.concatenate([x.astype(BF16) for x in xs], axis=0), m, preferred_element_type=F32)
        return [res[i * rows:(i + 1) * rows] for i in range(len(xs))]
    res = jnp.dot(jnp.concatenate([p for x in xs for p in _split2(x)], axis=0), m, preferred_element_type=F32)
    return [res[(2 * i) * rows:(2 * i + 1) * rows] + res[(2 * i + 1) * rows:(2 * i + 2) * rows]
            for i in range(len(xs))]


def _dot2(x, m):
    return _dot2_many([x], m)[0]


def _seg_sums(xs, bd, single=False):
    w = bd.shape[0]
    halves = _dot2_many([x[:, i:i + w] for x in xs for i in range(0, x.shape[1], w)], bd, single)
    n = xs[0].shape[1] // w
    return [jnp.concatenate(halves[i * n:(i + 1) * n], axis=1) for i in range(len(xs))]


def _seg_sum(x, bd):
    return _seg_sums([x], bd)[0]


def _scan_consts():
    r = np.arange(256)
    bd = (r[:, None] // HEAD_DIM == r[None, :] // HEAD_DIM).astype(np.float32)
    c = np.arange(RWKV_DIM)
    e = (np.arange(HEAD_DIM)[:, None] // SCAN_GROUP == c[None, :] // HEAD_DIM).astype(np.float32)
    diag = (np.arange(HEAD_DIM)[:, None] == c[None, :] % HEAD_DIM).astype(np.float32)
    return jnp.asarray(bd, BF16), jnp.asarray(e, BF16), jnp.asarray(diag, F32)


def _to_colblocks(a):
    S = a.shape[0]
    a = a.reshape(S // SCAN_GROUP, SCAN_GROUP, RWKV_HEADS, HEAD_DIM)
    return a.transpose(0, 3, 2, 1).reshape(S // SCAN_GROUP, HEAD_DIM, RWKV_HEADS * SCAN_GROUP)


def _roll_up(rows):
    return pltpu.roll(rows, rows.shape[0] - 1, 0)


def _scan_pair_rows(aux, base, kk_ref, w_ref, b_ref, k_ref, bd):
    G = SCAN_GROUP
    kk_nx = _roll_up(kk_ref[pl.ds(base, G), :])
    aux[0] = w_ref[pl.ds(base, G), :] * kk_nx
    aux[1], aux[2] = _seg_sums([b_ref[pl.ds(base, G), :] * kk_nx, k_ref[pl.ds(base, G), :] * kk_nx], bd)


def _scan_pair(St, t0, base, col_g, lane_t, aux, kk_ref, w_ref, b_ref, k_ref, bd, e):
    t1 = t0 + 1
    row = lambda ref, t: ref[pl.ds(base + t, 1), :]
    arow = lambda i: aux[i, pl.ds(t0, 1), :]
    u0, m1 = _seg_sums([St * row(kk_ref, t0), St * arow(0)], bd)
    v0, v1 = _dot2_many([jnp.where(lane_t == t0, col_g, 0.0), jnp.where(lane_t == t1, col_g, 0.0)], e)
    u1 = m1 - u0 * arow(1) + v0 * arow(2)
    S0 = St * row(w_ref, t0) - u0 * row(b_ref, t0) + v0 * row(k_ref, t0)
    S1 = S0 * row(w_ref, t1) - u1 * row(b_ref, t1) + v1 * row(k_ref, t1)
    return (S0, S1), (u0, u1), (v0, v1)


def rwkv_scan_fwd(r, w, k, kk, b, vB, gather_srcs):
    S, C = r.shape
    N, G = HEAD_DIM, SCAN_GROUP
    chunk = min(SCAN_CHUNK, S)
    nchunk, ng = S // chunk, chunk // G
    bd, e, diag = _scan_consts()

    nx = len(gather_srcs)

    def body(*refs):
        r_ref, w_ref, k_ref, kk_ref, b_ref, vB_ref, bd_ref, e_ref, dg_ref = refs[:9]
        y_ref, ck_ref = refs[9 + nx:11 + nx]
        S_ref, aux, send_sems, recv_sems, local_sems = refs[11 + 2 * nx:]
        c = pl.program_id(0)
        _exchange_during(c, nchunk, True, refs[9:9 + nx], refs[11 + nx:11 + 2 * nx], send_sems, recv_sems, local_sems)

        @pl.when(c == 0)
        def _():
            S_ref[...] = jnp.zeros_like(S_ref)

        ck_ref[...] = S_ref[...]
        sub = lax.broadcasted_iota(jnp.int32, (G, C), 0)
        lane_t = lax.broadcasted_iota(jnp.int32, (N, N), 1) & (G - 1)

        def group(g, St):
            base = pl.multiple_of(g * G, G)
            vb = vB_ref[g]
            _scan_pair_rows(aux, base, kk_ref, w_ref, b_ref, k_ref, bd_ref[...])
            ys = jnp.zeros((G, C), F32)
            def emit(ys, states, t0):
                steps = (t0, t0 + 1)
                y_bs = _seg_sums([S_t * r_ref[pl.ds(base + tt, 1), :] for S_t, tt in zip(states, steps)], bd_ref[...],
                                 single=True)
                for y_b, tt in zip(y_bs, steps):
                    ys = jnp.where(sub == tt, jnp.sum(y_b * dg_ref[...], axis=0, keepdims=True), ys)
                return ys

            pending = None
            for t0 in range(0, G, 2):
                states, _, _ = _scan_pair(St, t0, base, vb, lane_t, aux, kk_ref, w_ref, b_ref, k_ref, bd_ref[...],
                                          e_ref[...])
                if pending is not None:
                    ys = emit(ys, *pending)
                pending = (states, t0)
                St = states[1]
            y_ref[pl.ds(base, G), :] = emit(ys, *pending)
            return St

        S_ref[...] = lax.fori_loop(0, ng, group, S_ref[...])

    row = pl.BlockSpec((chunk, C), lambda c: (c, 0))
    col = pl.BlockSpec((ng, N, N), lambda c: (c, 0, 0))
    res = pl.pallas_call(
        body,
        grid=(nchunk,),
        in_specs=[row] * 5 + [col, _whole_spec(bd), _whole_spec(e), _whole_spec(diag)] + [_ANY] * nx,
        out_specs=[row, pl.BlockSpec((None, N, C), lambda c: (c, 0, 0))] + [_ANY] * nx,
        out_shape=[jax.ShapeDtypeStruct((S, C), F32), jax.ShapeDtypeStruct((nchunk, N, C), F32)]
        + _exchange_out_shapes(True, gather_srcs),
        scratch_shapes=[pltpu.VMEM((N, C), F32), pltpu.VMEM((3, G, C), F32)] + _exchange_sems(nx),
        compiler_params=_cparams(("arbitrary",)),
        name="rwkv_scan_fwd",
    )(r, w, k, kk, b, vB, bd, e, diag, *gather_srcs)
    return res[0], res[1], list(res[2:])


def rwkv_scan_bwd(r, w, k, kk, b, vB, dyB, ckpt, scatter_srcs):
    S, C = r.shape
    N, G = HEAD_DIM, SCAN_GROUP
    chunk = min(SCAN_CHUNK, S)
    nchunk, ng = S // chunk, chunk // G
    nsteps = nchunk + 1
    bd, e, diag = _scan_consts()
    nx = len(scatter_srcs)

    def body(*refs):
        wf_ref, kf_ref, kkf_ref, bf_ref, vBf_ref, ck_ref = refs[:6]
        r_ref, w_ref, k_ref, kk_ref, b_ref, dyB_ref, bd_ref, e_ref, dg_ref = refs[6:15]
        dr_ref, dw_ref, dk_ref, dkk_ref, db_ref, dv_ref = refs[15 + nx:21 + nx]
        G_ref, sbuf, ubuf, vbuf, aux_f, aux_b, send_sems, recv_sems, local_sems = refs[21 + 2 * nx:]
        c = pl.program_id(0)
        _exchange_during(c, nsteps, False, refs[15:15 + nx], refs[21 + nx:21 + 2 * nx], send_sems, recv_sems,
                         local_sems)

        @pl.when(c == 0)
        def _():
            G_ref[...] = jnp.zeros_like(G_ref)
            sbuf[...] = jnp.zeros_like(sbuf)
            ubuf[...] = jnp.zeros_like(ubuf)
            vbuf[...] = jnp.zeros_like(vbuf)

        sf = c % 2
        sb = 1 - sf
        lane_t = lax.broadcasted_iota(jnp.int32, (N, N), 1) & (G - 1)
        sub = lax.broadcasted_iota(jnp.int32, (G, C), 0)
        colsum = lambda a: jnp.sum(a, axis=0, keepdims=True)

        def group(g, carry):
            St, Gt = carry
            base_f = pl.multiple_of(g * G, G)
            gb = ng - 1 - g
            base_b = pl.multiple_of(gb * G, G)
            vb, dyb = vBf_ref[g], dyB_ref[gb]
            row = lambda ref, t: ref[pl.ds(base_b + t, 1), :]
            _scan_pair_rows(aux_f, base_f, kkf_ref, wf_ref, bf_ref, kf_ref, bd_ref[...])
            b8 = b_ref[pl.ds(base_b, G), :]
            aux_b[0] = _roll_up(w_ref[pl.ds(base_b, G), :]) * b8
            aux_b[1], aux_b[2] = _seg_sums([_roll_up(kk_ref[pl.ds(base_b, G), :]) * b8,
                                            r_ref[pl.ds(base_b, G), :] * b8], bd_ref[...])
            rows = [jnp.zeros((G, C), F32) for _ in range(6)]

            def emit(rows, steps):
                d_vs = _seg_sums([Gt_ * row(k_ref, tt) for tt, Gt_, _, _ in steps], bd_ref[...], single=True)
                for (tt, Gt_, du_b, dy_b), d_vb in zip(steps, d_vs):
                    Sp, Sc = sbuf[sb, base_b + tt], sbuf[sb, base_b + tt + 1]
                    new = (colsum(Sc * dy_b), colsum(Gt_ * Sp), colsum(Gt_ * vbuf[sb, base_b + tt]),
                           colsum(Sp * du_b), -colsum(Gt_ * ubuf[sb, base_b + tt]), colsum(d_vb * dg_ref[...]))
                    rows = [jnp.where(sub == tt, n_, acc) for n_, acc in zip(new, rows)]
                return rows

            pending = None
            for i in range(G // 2):
                t0 = 2 * i
                states, us, vs = _scan_pair(St, t0, base_f, vb, lane_t, aux_f, kkf_ref, wf_ref, bf_ref, kf_ref,
                                            bd_ref[...], e_ref[...])
                for j, S_before in enumerate((St, states[0])):
                    sbuf[sf, base_f + t0 + j] = S_before
                    ubuf[sf, base_f + t0 + j] = us[j]
                    vbuf[sf, base_f + t0 + j] = vs[j]
                St = states[1]
                t0 = G - 2 - 2 * i
                t1 = t0 + 1
                arow = lambda j, t0=t0: aux_b[j, pl.ds(t0, 1), :]
                dy1, dy0 = _dot2_many([jnp.where(lane_t == t1, dyb, 0.0), jnp.where(lane_t == t0, dyb, 0.0)],
                                      e_ref[...])
                G1 = Gt + dy1 * row(r_ref, t1)
                m1, m2 = _seg_sums([G1 * row(b_ref, t1), G1 * arow(0)], bd_ref[...])
                du1 = -m1
                du0 = -(m2 + du1 * arow(1) + dy0 * arow(2))
                G0 = G1 * row(w_ref, t1) + du1 * row(kk_ref, t1) + dy0 * row(r_ref, t0)
                G_next = G0 * row(w_ref, t0) + du0 * row(kk_ref, t0)
                if pending is not None:
                    rows = emit(rows, pending)
                pending = ((t1, G1, du1, dy1), (t0, G0, du0, dy0))
                Gt = G_next
            rows = emit(rows, pending)
            for ref, val in zip((dr_ref, dw_ref, dk_ref, dkk_ref, db_ref, dv_ref), rows):
                ref[pl.ds(base_b, G), :] = val
            return St, Gt

        St, Gt = lax.fori_loop(0, ng, group, (ck_ref[...], G_ref[...]))
        sbuf[sf, chunk] = St
        G_ref[...] = jnp.where(c >= 1, Gt, G_ref[...])

    fwd_chunk = lambda c: jnp.maximum(nchunk - 1 - c, 0)
    bwd_chunk = lambda c: jnp.minimum(nchunk - c, nchunk - 1)
    row_f = pl.BlockSpec((chunk, C), lambda c: (fwd_chunk(c), 0))
    row_b = pl.BlockSpec((chunk, C), lambda c: (bwd_chunk(c), 0))
    col_f = pl.BlockSpec((ng, N, N), lambda c: (fwd_chunk(c), 0, 0))
    col_b = pl.BlockSpec((ng, N, N), lambda c: (bwd_chunk(c), 0, 0))
    rshape = jax.ShapeDtypeStruct((S, C), F32)
    res = pl.pallas_call(
        body,
        grid=(nsteps,),
        in_specs=[row_f] * 4 + [col_f, pl.BlockSpec((None, N, C), lambda c: (fwd_chunk(c), 0, 0))]
        + [row_b] * 5 + [col_b, _whole_spec(bd), _whole_spec(e), _whole_spec(diag)] + [_ANY] * nx,
        out_specs=[row_b] * 6 + [_ANY] * nx,
        out_shape=[rshape] * 6 + _exchange_out_shapes(False, scatter_srcs),
        scratch_shapes=[pltpu.VMEM((N, C), F32), pltpu.VMEM((2, chunk + 1, N, C), F32),
                        pltpu.VMEM((2, chunk, N, C), F32), pltpu.VMEM((2, chunk, N, C), F32),
                        pltpu.VMEM((3, G, C), F32), pltpu.VMEM((3, G, C), F32)] + _exchange_sems(nx),
        compiler_params=_cparams(("arbitrary",)),
        name="rwkv_scan_bwd",
    )(w, k, kk, b, vB, ckpt, r, w, k, kk, b, dyB, bd, e, diag, *scatter_srcs)
    return tuple(res[:6]) + (list(res[6:]),)


def seq_cumsum(x, reverse, name):
    S, C = x.shape
    tb = min(256, S)
    nb = S // tb

    def body(x_ref, o_ref, carry):
        i = pl.program_id(0)

        @pl.when(i == 0)
        def _():
            carry[...] = jnp.zeros_like(carry)

        ri = lax.broadcasted_iota(jnp.int32, (tb, tb), 0)
        ci = lax.broadcasted_iota(jnp.int32, (tb, tb), 1)
        tri = jnp.where((ci >= ri) if reverse else (ci <= ri), 1.0, 0.0).astype(F32)
        xb = x_ref[...]
        out = jnp.dot(tri, xb, precision=lax.Precision.HIGHEST, preferred_element_type=F32) + carry[...]
        o_ref[...] = out
        carry[...] = carry[...] + jnp.sum(xb, axis=0, keepdims=True)

    idx = (lambda i: (nb - 1 - i, 0)) if reverse else (lambda i: (i, 0))
    return pl.pallas_call(
        body,
        grid=(nb,),
        in_specs=[pl.BlockSpec((tb, C), idx)],
        out_specs=pl.BlockSpec((tb, C), idx),
        out_shape=jax.ShapeDtypeStruct((S, C), F32),
        scratch_shapes=[pltpu.VMEM((1, C), F32)],
        compiler_params=_cparams(("arbitrary",)),
        name=name,
    )(x)


def _fox_logits(q, k, cq, ck, diagonal):
    s = _dot(q, k, _NT) * (HEAD_DIM ** -0.5) + cq - ck
    if not diagonal:
        return s
    row = lax.broadcasted_iota(jnp.int32, s.shape, 0)
    col = lax.broadcasted_iota(jnp.int32, s.shape, 1)
    return jnp.where(col <= row, s, -jnp.inf)


def _fox_tiles(n, by_query):
    pairs = [(i, j) for i in range(n) for j in range(i + 1)] if by_query else \
            [(i, j) for j in range(n) for i in range(j, n)]
    return (jnp.asarray(np.array([p[0] for p in pairs], np.int32)),
            jnp.asarray(np.array([p[1] for p in pairs], np.int32)))


def _fox_specs(t, Dh):
    qs = pl.BlockSpec((None, t, Dh), lambda h, s, qt, kt: (h, qt[s], 0))
    ks = pl.BlockSpec((None, t, Dh), lambda h, s, qt, kt: (h, kt[s], 0))
    cqs = pl.BlockSpec((None, t, 1), lambda h, s, qt, kt: (h, qt[s], 0))
    cks = pl.BlockSpec((None, 1, t), lambda h, s, qt, kt: (h, 0, kt[s]))
    return qs, ks, cqs, cks


def _fox_call(body, tiles, Hh, in_specs, out_specs, out_shape, scratch, name, args):
    spec = pltpu.PrefetchScalarGridSpec(num_scalar_prefetch=2, grid=(Hh, tiles[0].shape[0]), in_specs=in_specs,
                                        out_specs=out_specs, scratch_shapes=scratch)
    return pl.pallas_call(body, grid_spec=spec, out_shape=out_shape,
                          compiler_params=_cparams(("parallel", "arbitrary")), name=name)(*tiles, *args)


def fox_fwd(q, k, v, c_col, c_row):
    Hh, S, Dh = q.shape
    tq = tk = min(512, S)

    def body(qt_ref, kt_ref, q_ref, k_ref, v_ref, cq_ref, ck_ref, o_ref, lse_ref, m_s, l_s, acc_s):
        qi, ki = qt_ref[pl.program_id(1)], kt_ref[pl.program_id(1)]

        @pl.when(ki == 0)
        def _():
            m_s[...] = jnp.full_like(m_s, -jnp.inf)
            l_s[...] = jnp.zeros_like(l_s)
            acc_s[...] = jnp.zeros_like(acc_s)

        def tile(diagonal):
            s = _fox_logits(q_ref[...], k_ref[...], cq_ref[...], ck_ref[...], diagonal)
            m_old = m_s[...]
            m_new = jnp.maximum(m_old, jnp.max(s, axis=-1, keepdims=True))
            alpha = jnp.exp(m_old - m_new)
            p = jnp.exp(s - m_new)
            l_s[...] = alpha * l_s[...] + jnp.sum(p, axis=-1, keepdims=True)
            acc_s[...] = alpha * acc_s[...] + _dot(p, v_ref[...], _NN)
            m_s[...] = m_new

        @pl.when(ki != qi)
        def _():
            tile(False)

        @pl.when(ki == qi)
        def _():
            tile(True)
            o_ref[...] = acc_s[...] / l_s[...]
            lse_ref[...] = m_s[...] + jnp.log(l_s[...])

    qs, ks, cqs, cks = _fox_specs(tq, Dh)
    return _fox_call(
        body, _fox_tiles(S // tq, True), Hh, [qs, ks, ks, cqs, cks], [qs, cqs],
        [jax.ShapeDtypeStruct((Hh, S, Dh), F32), jax.ShapeDtypeStruct((Hh, S, 1), F32)],
        [pltpu.VMEM((tq, 1), F32), pltpu.VMEM((tq, 1), F32), pltpu.VMEM((tq, Dh), F32)],
        "fox_fwd", (q, k, v, c_col, c_row))


def fox_bwd(q, k, v, c_col, c_row, o, lse, do):
    Hh, S, Dh = q.shape
    tq = tk = min(512, S)
    nk = S // tk

    def body(qt_ref, kt_ref, q_ref, k_ref, v_ref, cq_ref, ck_ref, o_ref, lse_ref, do_ref,
             dq_ref, dr_ref, dk_ref, dv_ref, dc_ref, acc_s, row_s):
        step = pl.program_id(1)
        qi, ki = qt_ref[step], kt_ref[step]

        @pl.when(step == 0)
        def _():
            dk_ref[...] = jnp.zeros_like(dk_ref)
            dv_ref[...] = jnp.zeros_like(dv_ref)
            dc_ref[...] = jnp.zeros_like(dc_ref)

        @pl.when(ki == 0)
        def _():
            acc_s[...] = jnp.zeros_like(acc_s)
            row_s[...] = jnp.zeros_like(row_s)

        def tile(diagonal):
            q_t, kb, vb, do_t = q_ref[...], k_ref[...], v_ref[...], do_ref[...]
            s = _fox_logits(q_t, kb, cq_ref[...], ck_ref[...], diagonal)
            p = jnp.exp(s - lse_ref[...])
            delta = jnp.sum(do_t * o_ref[...], axis=-1, keepdims=True)
            ds = p * (_dot(do_t, vb, _NT) - delta)
            acc_s[...] += _dot(ds, kb, _NN)
            row_s[...] += jnp.sum(ds, axis=-1, keepdims=True)
            dk_ref[ki] += _dot(ds, q_t, _TN) * (HEAD_DIM ** -0.5)
            dv_ref[ki] += _dot(p, do_t, _TN)
            dc_ref[ki] += jnp.sum(ds, axis=0, keepdims=True)

        @pl.when(ki != qi)
        def _():
            tile(False)

        @pl.when(ki == qi)
        def _():
            tile(True)
            dq_ref[...] = acc_s[...] * (HEAD_DIM ** -0.5)
            dr_ref[...] = row_s[...]

    qs, ks, cqs, cks = _fox_specs(tq, Dh)
    head = lambda *blk: pl.BlockSpec((None,) + blk, lambda h, s, qt, kt: (h,) + (0,) * len(blk))
    dq, dr, dk, dv, dc = _fox_call(
        body, _fox_tiles(S // tq, True), Hh, [qs, ks, ks, cqs, cks, qs, cqs, qs],
        [qs, cqs, head(nk, tk, Dh), head(nk, tk, Dh), head(nk, 1, tk)],
        [jax.ShapeDtypeStruct((Hh, S, Dh), F32), jax.ShapeDtypeStruct((Hh, S, 1), F32),
         jax.ShapeDtypeStruct((Hh, nk, tk, Dh), F32), jax.ShapeDtypeStruct((Hh, nk, tk, Dh), F32),
         jax.ShapeDtypeStruct((Hh, nk, 1, tk), F32)],
        [pltpu.VMEM((tq, Dh), F32), pltpu.VMEM((tq, 1), F32)],
        "fox_bwd", (q, k, v, c_col, c_row, o, lse, do))
    return dq, dr, dk.reshape(Hh, S, Dh), dv.reshape(Hh, S, Dh), dc.reshape(Hh, 1, S)


def _heads(a, nh):
    S = a.shape[0]
    return a.reshape(S, nh, HEAD_DIM).transpose(1, 0, 2)


def _unheads(a):
    nh, S, _ = a.shape
    return a.transpose(1, 0, 2).reshape(S, nh * HEAD_DIM)


def _shift_down(a):
    return jnp.pad(a[:-1], ((1, 0), (0, 0)))


def _shift_up(a):
    return jnp.pad(a[1:], ((0, 1), (0, 0)))


def _block_diag_ones():
    i = np.arange(RWKV_DIM) // HEAD_DIM
    return jnp.asarray((i[:, None] == i[None, :]).astype(np.float32))


FFN_ROWS = 1024
FFN_COLS = 256


def _ffn_specs(S, F, tm, fc):
    nf = F // fc
    row = pl.BlockSpec((tm, D_MODEL), lambda i, j: (i, 0))
    vec = pl.BlockSpec((1, D_MODEL), lambda i, j: (0, 0))
    wg = pl.BlockSpec((D_MODEL, fc), lambda i, j: (0, j))
    wu = pl.BlockSpec((D_MODEL, fc), lambda i, j: (0, nf + j))
    wd = pl.BlockSpec((fc, D_MODEL), lambda i, j: (j, 0))
    hid = pl.BlockSpec((tm, fc), lambda i, j: (i, j))
    return nf, row, vec, wg, wu, wd, hid


def ffn_fwd(x, g_norm, w_gu, w_down, tag):
    S, F = x.shape[0], w_down.shape[0]
    tm, fc = min(FFN_ROWS, S), FFN_COLS
    nf, row, vec, wg, wu, wd, _ = _ffn_specs(S, F, tm, fc)

    def body(x_ref, g_ref, wg_ref, wu_ref, wd_ref, o_ref, hn_ref, hn_s, acc):
        j = pl.program_id(1)

        @pl.when(j == 0)
        def _():
            hn_s[...] = _rms(x_ref[...], g_ref[...]).astype(BF16)
            hn_ref[...] = hn_s[...]
            acc[...] = jnp.zeros_like(acc)

        g = _dot(hn_s[...], wg_ref[...], _NN)
        u = _dot(hn_s[...], wu_ref[...], _NN)
        acc[...] += _dot(g * _sigmoid_tanh(g) * u, wd_ref[...], _NN)

        @pl.when(j == nf - 1)
        def _():
            o_ref[...] = x_ref[...] + 0.5 * acc[...]

    out, hn = pl.pallas_call(
        body,
        grid=(S // tm, nf),
        in_specs=[row, vec, wg, wu, wd],
        out_specs=[row, row],
        out_shape=[jax.ShapeDtypeStruct((S, D_MODEL), F32), jax.ShapeDtypeStruct((S, D_MODEL), BF16)],
        scratch_shapes=[pltpu.VMEM((tm, D_MODEL), BF16), pltpu.VMEM((tm, D_MODEL), F32)],
        compiler_params=_cparams(("parallel", "arbitrary")),
        name=tag + "_fwd",
    )(x, g_norm, w_gu, w_gu, w_down)
    return out, (x, hn)


def ffn_bwd(dy, saved, g_norm, w_gu, w_down, tag):
    x, hn = saved
    S, F = x.shape[0], w_down.shape[0]
    tm, fc = min(FFN_ROWS, S), FFN_COLS
    nf, row, vec, wg, wu, wd, hid = _ffn_specs(S, F, tm, fc)

    def body(dy_ref, x_ref, hn_ref, g_ref, wg_ref, wu_ref, wd_ref, dx_ref, dgn_ref, a_ref, dg_ref, du_ref,
             dyh_s, dhn):
        i, j = pl.program_id(0), pl.program_id(1)

        @pl.when(j == 0)
        def _():
            dyh_s[...] = (0.5 * dy_ref[...]).astype(BF16)
            dhn[...] = jnp.zeros_like(dhn)

        hn_t = hn_ref[...]
        g = _dot(hn_t, wg_ref[...], _NN)
        u = _dot(hn_t, wu_ref[...], _NN)
        da = _dot(dyh_s[...], wd_ref[...], _NT)
        sig = _sigmoid_tanh(g)
        gs = g * sig
        a_ref[...] = (gs * u).astype(BF16)
        dg = ((da * u) * (sig + gs * (1.0 - sig))).astype(BF16)
        du = (da * gs).astype(BF16)
        dg_ref[...] = dg
        du_ref[...] = du
        dhn[...] += _dot(jnp.concatenate([dg, du], axis=1),
                         jnp.concatenate([wg_ref[...], wu_ref[...]], axis=1), _NT)

        @pl.when(j == nf - 1)
        def _():
            _, vjp_n = jax.vjp(_rms, x_ref[...], g_ref[...])
            dx, dgn = vjp_n(dhn[...])
            dx_ref[...] = dy_ref[...] + dx

            @pl.when(i == 0)
            def _():
                dgn_ref[...] = jnp.zeros_like(dgn_ref)

            dgn_ref[...] += dgn

    hshape = jax.ShapeDtypeStruct((S, F), BF16)
    dx, dgn, act, dg, du = pl.pallas_call(
        body,
        grid=(S // tm, nf),
        in_specs=[row, row, row, vec, wg, wu, wd],
        out_specs=[row, vec, hid, hid, hid],
        out_shape=[jax.ShapeDtypeStruct((S, D_MODEL), F32), jax.ShapeDtypeStruct((1, D_MODEL), F32),
                   hshape, hshape, hshape],
        scratch_shapes=[pltpu.VMEM((tm, D_MODEL), BF16), pltpu.VMEM((tm, D_MODEL), F32)],
        compiler_params=_cparams(("arbitrary", "arbitrary")),
        name=tag + "_bwd",
    )(dy, x, hn, g_norm, w_gu, w_gu, w_down)
    d_wdown = matmul(act, dy, "tn", tag + "_dwd", out_dtype=BF16, scale=0.5, tm=1408)
    d_wgu = jnp.concatenate([matmul(hn, dg, "tn", tag + "_dwg", out_dtype=BF16),
                             matmul(hn, du, "tn", tag + "_dwu", out_dtype=BF16)], axis=1)
    return dx, dgn, d_wgu, d_wdown


def ple_fwd(x, p_i, g_norm, w_gate, w_proj, tag):
    hn, = rowwise(_f_rms, [x], [g_norm], [(D_MODEL, BF16)], tag + "_rms")
    z = matmul(hn, w_gate, "nn", tag + "_gate")
    pp = matmul(p_i, w_proj, "nn", tag + "_proj")
    out, = rowwise(_f_ple, [x, z, pp], [], [(D_MODEL, F32)], tag + "_mix")
    return out, (x, hn, z, pp)


def ple_bwd(dy, saved, p_i, g_norm, w_gate, tag):
    x, hn, z, pp = saved
    (dz, dpp), _ = rowwise_vjp(_f_ple, [x, z, pp], [], [dy], tag + "_dmix", need=[False, True, True],
                               row_dtype=BF16)
    d_wproj = matmul(p_i, dpp, "tn", tag + "_dwp", out_dtype=BF16)
    d_wgate = matmul(hn, dz, "tn", tag + "_dwg", out_dtype=BF16)
    dhn = matmul(dz, w_gate, "nt", tag + "_dhn")
    (dx,), (dgn,) = rowwise_vjp(_f_rms_res, [x], [g_norm], [dhn, dy], tag + "_drms")
    return dx, dgn, d_wgate, d_wproj


def _swa_consts(sinks):
    slopes = np.asarray([2.0 ** (-(i + 1)) for i in range(SWA_HEADS)], np.float32)
    slope_col = jnp.asarray(np.repeat(slopes, BLOCK).reshape(SWA_KV_HEADS, SWA_GROUP * BLOCK, 1))
    sink_col = jnp.repeat(sinks.reshape(SWA_HEADS), BLOCK).reshape(SWA_KV_HEADS, SWA_GROUP * BLOCK, 1)
    return sink_col, slope_col


def even_mix_fwd(x, W, gather_src, later_weights):
    S = x.shape[0]
    hn, = rowwise(_f_rms, [x], [W["mix_norm0"]], [(D_MODEL, BF16)], "emix_rms")
    proj = matmul(hn, W["even_w_in"], "nn", "emix_in")
    qa = _heads(proj[:, :SWA_Q], SWA_HEADS).reshape(SWA_KV_HEADS, SWA_GROUP, S, HEAD_DIM)
    ka = _heads(proj[:, SWA_Q:SWA_Q + SWA_KV], SWA_KV_HEADS)
    va = _heads(proj[:, SWA_Q + SWA_KV:SWA_COLS], SWA_KV_HEADS)
    sink_col, slope_col = _swa_consts(W["swa_sinks"])
    ya = swa_fwd(qa, ka, va, sink_col, slope_col)
    ya = _unheads(ya.reshape(SWA_HEADS, S, HEAD_DIM))
    hb = proj[:, SWA_COLS:]
    h, = rowwise(_f_mix, [hb, _shift_down(hb)], [W["rwkv_mu"]], [(hb.shape[1], F32)], "rwkv_shift")
    hr, hk, hv = h[:, :512], h[:, 512:1024], h[:, 1024:1536]
    hw, ha, hg = h[:, 1536:1600], h[:, 1600:1664], h[:, 1664:1792]
    bd = _block_diag_ones()
    pre_params = [W["rwkv_w0"], W["rwkv_w2"], W["rwkv_a0"], W["rwkv_a2"], W["rwkv_g2"], W["rwkv_k_k"],
                  W["rwkv_k_a"]]
    decay, k2, kk, b, g = rowwise(_f_rwkv_pre, [hk, hw, ha, hg], pre_params + [bd],
                                  [(RWKV_DIM, F32)] * 5, "rwkv_pre")
    vT = _to_colblocks(hv)
    y, ckpt, gathered = rwkv_scan_fwd(hr, decay, k2, kk, b, vT, gather_src)
    late = later_weights(gathered)
    post_params = [W["rwkv_ln_w"], W["rwkv_ln_b"], W["rwkv_r_k"]]
    yb, = rowwise(_f_rwkv_post, [y, hr, k2, hv, g], post_params + [bd], [(RWKV_DIM, F32)], "rwkv_post")
    cat = jnp.concatenate([ya, yb], axis=1).astype(BF16)
    out = matmul(cat, late["even_w_out"], "nn", "emix_out", res=x)
    saved = (x, hn, qa, ka, va, sink_col, slope_col, hb, hr, hk, hv, hw, ha, hg, decay, k2, kk, b, g, vT,
             ckpt, y, cat)
    return out, saved, late


def even_mix_bwd(dy, saved, W, scatter_src):
    (x, hn, qa, ka, va, sink_col, slope_col, hb, hr, hk, hv, hw, ha, hg, decay, k2, kk, b, g, vT, ckpt, y,
     cat) = saved
    S = x.shape[0]
    grads = {}
    dcat = matmul(dy, W["even_w_out"], "nt", "emix_dcat")
    grads["even_w_out"] = matmul(cat, dy, "tn", "emix_dwout", out_dtype=BF16)
    dya, dyb = dcat[:, :SWA_Q], dcat[:, SWA_Q:]
    dya_h = _heads(dya, SWA_HEADS).reshape(SWA_KV_HEADS, SWA_GROUP, S, HEAD_DIM)
    dqa, dkp, dkc, dvp, dvc, dsink = swa_bwd(qa, ka, va, sink_col, slope_col, dya_h)
    shift_blk = lambda a: jnp.pad(a[:, BLOCK:], ((0, 0), (0, BLOCK), (0, 0)))
    dka = dkc + shift_blk(dkp)
    dva = dvc + shift_blk(dvp)
    grads["swa_sinks"] = dsink.reshape(SWA_HEADS, BLOCK).sum(axis=1).reshape(1, SWA_HEADS)
    dqa = _unheads(dqa.reshape(SWA_HEADS, S, HEAD_DIM))
    dka, dva = _unheads(dka), _unheads(dva)
    bd = _block_diag_ones()
    post_params = [W["rwkv_ln_w"], W["rwkv_ln_b"], W["rwkv_r_k"]]
    (d_y, d_r1, d_k2a, d_v1, d_g), (d_lnw, d_lnb, d_rk) = rowwise_vjp(
        _f_rwkv_post, [y, hr, k2, hv, g], post_params, [dyb], "rwkv_dpost", consts=[bd], tm=128)
    grads["rwkv_ln_w"], grads["rwkv_ln_b"], grads["rwkv_r_k"] = d_lnw, d_lnb, d_rk
    d_r2, d_w, d_k2b, d_kk, d_b, d_v2, exchanged = rwkv_scan_bwd(hr, decay, k2, kk, b, vT, _to_colblocks(d_y), ckpt,
                                                                  scatter_src)
    pre_params = [W["rwkv_w0"], W["rwkv_w2"], W["rwkv_a0"], W["rwkv_a2"], W["rwkv_g2"], W["rwkv_k_k"],
                  W["rwkv_k_a"]]
    (d_hk, d_hw, d_ha, d_hg), dpre = rowwise_vjp(
        _f_rwkv_pre, [hk, hw, ha, hg], pre_params, [d_w, d_k2a + d_k2b, d_kk, d_b, d_g], "rwkv_dpre",
        consts=[bd], tm=128)
    for nm, gval in zip(["rwkv_w0", "rwkv_w2", "rwkv_a0", "rwkv_a2", "rwkv_g2", "rwkv_k_k", "rwkv_k_a"], dpre):
        grads[nm] = gval
    d_h = jnp.concatenate([d_r1 + d_r2, d_hk, d_v1 + d_v2, d_hw, d_ha, d_hg], axis=1)
    (d_hb, d_sh), (d_mu,) = rowwise_vjp(_f_mix, [hb, _shift_down(hb)], [W["rwkv_mu"]], [d_h], "rwkv_dshift")
    grads["rwkv_mu"] = d_mu
    d_hb = d_hb + _shift_up(d_sh)
    dproj = jnp.concatenate([dqa, dka, dva, d_hb], axis=1).astype(BF16)
    grads["even_w_in"] = matmul(hn, dproj, "tn", "emix_dwin", out_dtype=BF16)
    dhn = matmul(dproj, W["even_w_in"], "nt", "emix_dhn")
    (dx,), (dgn,) = rowwise_vjp(_f_rms_res, [x], [W["mix_norm0"]], [dhn, dy], "emix_drms")
    grads["mix_norm0"] = dgn
    return dx, grads, exchanged


def odd_mix_fwd(x, W):
    S = x.shape[0]
    hn, = rowwise(_f_rms, [x], [W["mix_norm1"]], [(D_MODEL, BF16)], "omix_rms")
    proj = matmul(hn, W["fox_w_in"], "nn", "omix_in")
    q = _heads(proj[:, :FOX_DIM], FOX_HEADS).astype(BF16)
    k = _heads(proj[:, FOX_DIM:2 * FOX_DIM], FOX_HEADS).astype(BF16)
    v = _heads(proj[:, 2 * FOX_DIM:3 * FOX_DIM], FOX_HEADS).astype(BF16)
    fz = proj[:, 3 * FOX_DIM:]
    logf, = rowwise(_f_logf, [fz], [W["fox_b_f"]], [(128, F32)], "fox_logf")
    c = seq_cumsum(logf, False, "fox_cumsum")[:, :FOX_HEADS]
    c_col = c.T.reshape(FOX_HEADS, S, 1)
    c_row = c.T.reshape(FOX_HEADS, 1, S)
    o, lse = fox_fwd(q, k, v, c_col, c_row)
    yc = _unheads(o).astype(BF16)
    out = matmul(yc, W["fox_w_out"], "nn", "omix_out", res=x)
    return out, (x, hn, q, k, v, fz, c_col, c_row, o, lse, yc)


def odd_mix_bwd(dy, saved, W):
    x, hn, q, k, v, fz, c_col, c_row, o, lse, yc = saved
    S = x.shape[0]
    grads = {}
    dyc = matmul(dy, W["fox_w_out"], "nt", "omix_dyc")
    grads["fox_w_out"] = matmul(yc, dy, "tn", "omix_dwout", out_dtype=BF16)
    do = _heads(dyc, FOX_HEADS)
    dq, drow, dk, dv, dcol = fox_bwd(q, k, v, c_col, c_row, o, lse, do)
    dc = (drow.reshape(FOX_HEADS, S) - dcol.reshape(FOX_HEADS, S)).T
    dc = jnp.pad(dc, ((0, 0), (0, 128 - FOX_HEADS)))
    dlogf = seq_cumsum(dc, True, "fox_rcumsum")
    (dfz,), (dbf,) = rowwise_vjp(_f_logf, [fz], [W["fox_b_f"]], [dlogf], "fox_dlogf")
    grads["fox_b_f"] = dbf
    dproj = jnp.concatenate([_unheads(dq), _unheads(dk), _unheads(dv), dfz], axis=1).astype(BF16)
    grads["fox_w_in"] = matmul(hn, dproj, "tn", "omix_dwin", out_dtype=BF16)
    dhn = matmul(dproj, W["fox_w_in"], "nt", "omix_dhn")
    (dx,), (dgn,) = rowwise_vjp(_f_rms_res, [x], [W["mix_norm1"]], [dhn, dy], "omix_drms")
    grads["mix_norm1"] = dgn
    return dx, grads


def device_step(x, p, target, W, gather_src, layer1_weights, layer1_grads):
    W = dict(W)
    saved = []
    h = x
    for i in range(2):
        h, s1 = ffn_fwd(h, W[f"ffn1_norm{i}"], W[f"ffn1_w_gu{i}"], W[f"ffn1_w_down{i}"], f"ffn1_{i}")
        if i == 0:
            h, s2, late = even_mix_fwd(h, W, gather_src, layer1_weights)
            W.update(late)
        else:
            h, s2 = odd_mix_fwd(h, W)
        h, s3 = ffn_fwd(h, W[f"ffn2_norm{i}"], W[f"ffn2_w_gu{i}"], W[f"ffn2_w_down{i}"], f"ffn2_{i}")
        h, s4 = ple_fwd(h, p[i], W[f"ple_norm{i}"], W[f"ple_w_gate{i}"], W[f"ple_w_proj{i}"], f"ple_{i}")
        saved.append((s1, s2, s3, s4))
    dh, d_final, loss = loss_head(h, target, W["final_norm"])
    G = {"final_norm": d_final}
    for i in (1, 0):
        s1, s2, s3, s4 = saved[i]
        dh, G[f"ple_norm{i}"], G[f"ple_w_gate{i}"], G[f"ple_w_proj{i}"] = ple_bwd(
            dh, s4, p[i], W[f"ple_norm{i}"], W[f"ple_w_gate{i}"], f"ple_{i}")
        dh, G[f"ffn2_norm{i}"], G[f"ffn2_w_gu{i}"], G[f"ffn2_w_down{i}"] = ffn_bwd(
            dh, s3, W[f"ffn2_norm{i}"], W[f"ffn2_w_gu{i}"], W[f"ffn2_w_down{i}"], f"ffn2_{i}")
        if i == 0:
            dh, gm, exchanged = even_mix_bwd(dh, s2, W, layer1_grads(G))
        else:
            dh, gm = odd_mix_bwd(dh, s2, W)
        G.update(gm)
        dh, G[f"ffn1_norm{i}"], G[f"ffn1_w_gu{i}"], G[f"ffn1_w_down{i}"] = ffn_bwd(
            dh, s1, W[f"ffn1_norm{i}"], W[f"ffn1_w_gu{i}"], W[f"ffn1_w_down{i}"], f"ffn1_{i}")
    return loss, dh, G, exchanged


_MESH = pl.DeviceIdType.MESH
_ANY = pl.BlockSpec(memory_space=pl.ANY)


def _exchange_sems(n):
    return [pltpu.SemaphoreType.DMA((7 * n,)), pltpu.SemaphoreType.DMA((7 * n,)), pltpu.SemaphoreType.DMA((n,))]


def all_gather(xs, name):
    n = len(xs)

    def body(*refs):
        x_refs, out_refs = refs[:n], refs[n:2 * n]
        send_sems, recv_sems, local_sems = refs[2 * n:]
        x_, y_, c_ = lax.axis_index("x"), lax.axis_index("y"), lax.axis_index("c")
        me, sibling = (x_, y_, c_), (x_, y_, 1 - c_)
        chips = [(1 - x_, y_), (x_, 1 - y_), (1 - x_, 1 - y_)]

        def copy(b, k, block, to, from_input=False):
            slot = out_refs[b].at[4 * block[0] + 2 * block[1] + block[2]]
            return pltpu.make_async_remote_copy(
                src_ref=x_refs[b] if from_input else slot, dst_ref=slot,
                send_sem=send_sems.at[7 * b + k], recv_sem=recv_sems.at[7 * b + k], device_id=to,
                device_id_type=_MESH)

        bufs = range(n)
        mine = [pltpu.make_async_copy(x_refs[b], out_refs[b].at[4 * x_ + 2 * y_ + c_], local_sems.at[b]) for b in bufs]
        first = [copy(b, 0, me, sibling, True) for b in bufs]
        first += [copy(b, 1 + j, me, (*chip, c_), True) for j, chip in enumerate(chips) for b in bufs]
        for cp in mine + first:
            cp.start()
        passed = []
        for j, chip in enumerate(chips):
            for b in bufs:
                copy(b, 1 + j, (*chip, c_), me).wait_recv()
                passed.append(copy(b, 4 + j, (*chip, c_), sibling))
                passed[-1].start()
        for b in bufs:
            copy(b, 0, sibling, me).wait_recv()
            for j, chip in enumerate(chips):
                copy(b, 4 + j, (*chip, 1 - c_), me).wait_recv()
        for cp in first + passed:
            cp.wait_send()
        for cp in mine:
            cp.wait()

    return pl.pallas_call(
        body,
        out_shape=[jax.ShapeDtypeStruct((N_DEV,) + x.shape, x.dtype) for x in xs],
        in_specs=[_ANY] * n,
        out_specs=[_ANY] * n,
        scratch_shapes=_exchange_sems(n),
        name=name,
    )(*xs)


def _direct_exchange(gather, s_refs, r_refs, send_sems, recv_sems, local_sems):
    x_, y_, c_ = lax.axis_index("x"), lax.axis_index("y"), lax.axis_index("c")
    my = 4 * x_ + 2 * y_ + c_
    copies = []
    for b, (s_ref, r_ref) in enumerate(zip(s_refs, r_refs)):
        copies.append(pltpu.make_async_copy(s_ref if gather else s_ref.at[my], r_ref.at[my], local_sems.at[b]))
        for m in range(1, N_DEV):
            px = 1 - x_ if (m >> 2) & 1 else x_
            py = 1 - y_ if (m >> 1) & 1 else y_
            pc = 1 - c_ if m & 1 else c_
            copies.append(pltpu.make_async_remote_copy(
                src_ref=s_ref if gather else s_ref.at[4 * px + 2 * py + pc], dst_ref=r_ref.at[my],
                send_sem=send_sems.at[7 * b + m - 1], recv_sem=recv_sems.at[7 * b + m - 1],
                device_id=(px, py, pc), device_id_type=_MESH))
    return copies


def _exchange_during(step, n_steps, gather, s_refs, r_refs, send_sems, recv_sems, local_sems):
    copies = _direct_exchange(gather, s_refs, r_refs, send_sems, recv_sems, local_sems)

    @pl.when(step == 0)
    def _():
        for cp in copies:
            cp.start()

    @pl.when(step == n_steps - 1)
    def _():
        for cp in copies:
            cp.wait()


def _exchange_out_shapes(gather, srcs):
    return [jax.ShapeDtypeStruct(((N_DEV,) + s.shape) if gather else s.shape, s.dtype) for s in srcs]


def all_to_all(sends, name):
    n = len(sends)

    def body(*refs):
        copies = _direct_exchange(False, refs[:n], refs[n:2 * n], *refs[2 * n:])
        for cp in copies:
            cp.start()
        for cp in copies:
            cp.wait()

    return pl.pallas_call(
        body,
        out_shape=_exchange_out_shapes(False, sends),
        in_specs=[_ANY] * n,
        out_specs=[_ANY] * n,
        scratch_shapes=_exchange_sems(n),
        name=name,
    )(*sends)


def adamw(w, m, v, parts, name, tm=256):
    R, C = w.shape
    tm = _pick(R, tm, 8) if R >= 8 else R

    def body(w_ref, m_ref, v_ref, p_ref, g_ref, d_ref, nm_ref, nv_ref):
        g = p_ref[0].astype(F32)
        for s in range(1, N_DEV):
            g = g + p_ref[s].astype(F32)
        nm = ADAM_B1 * m_ref[...] + (1.0 - ADAM_B1) * g
        nv = ADAM_B2 * v_ref[...] + (1.0 - ADAM_B2) * (g * g)
        m_hat = nm / (1.0 - ADAM_B1 ** ADAM_STEP)
        v_hat = nv / (1.0 - ADAM_B2 ** ADAM_STEP)
        g_ref[...] = g
        d_ref[...] = -ADAM_LR * (m_hat / (jnp.sqrt(v_hat) + ADAM_EPS) + ADAM_WD * w_ref[...])
        nm_ref[...] = nm
        nv_ref[...] = nv

    row = pl.BlockSpec((tm, C), lambda i: (i, 0))
    out = jax.ShapeDtypeStruct((R, C), F32)
    return pl.pallas_call(
        body,
        grid=(R // tm,),
        in_specs=[row, row, row, pl.BlockSpec((N_DEV, tm, C), lambda i: (0, i, 0))],
        out_specs=[row] * 4,
        out_shape=[out] * 4,
        compiler_params=_cparams(("parallel",)),
        name=name,
    )(w, m, v, parts)


_WEIGHTS = ["ffn1_norm", "ffn1_w_gu", "ffn1_w_down", "mix_norm", "ffn2_norm", "ffn2_w_gu", "ffn2_w_down",
            "ple_norm", "ple_w_gate", "ple_w_proj", "even_w_in", "even_w_out", "swa_sinks", "rwkv_mu",
            "rwkv_w0", "rwkv_w2", "rwkv_a0", "rwkv_a2", "rwkv_g2", "rwkv_k_k", "rwkv_k_a", "rwkv_r_k",
            "rwkv_ln_w", "rwkv_ln_b", "fox_w_in", "fox_b_f", "fox_w_out", "final_norm"]
_SHARD_AXIS = {"ffn1_w_gu": 2, "ffn1_w_down": 1, "ffn2_w_gu": 2, "ffn2_w_down": 1, "ple_w_gate": 1,
               "ple_w_proj": 2, "even_w_in": 2, "even_w_out": 1, "rwkv_w2": 2, "rwkv_a2": 2, "rwkv_g2": 2,
               "fox_w_in": 2, "fox_w_out": 1}
_SHARDED = [n for n in _WEIGHTS if n in _SHARD_AXIS]
_REPLICATED = [n for n in _WEIGHTS if n not in _SHARD_AXIS]
_PER_LAYER = ("ffn1_w_gu", "ffn1_w_down", "ffn2_w_gu", "ffn2_w_down", "ple_w_gate", "ple_w_proj")
_ALL_PIECES = ([(n, 0) for n in _PER_LAYER] + [(n, 0) for n in ("even_w_in", "even_w_out", "rwkv_w2", "rwkv_a2", "rwkv_g2")]
               + [(n, 1) for n in _PER_LAYER] + [("fox_w_in", 0), ("fox_w_out", 0)])
_FIRST_WEIGHTS = [(n, 0) for n in ("ffn1_w_gu", "ffn1_w_down", "even_w_in", "rwkv_w2", "rwkv_a2", "rwkv_g2")]
_PIECES = [_FIRST_WEIGHTS, [pc for pc in _ALL_PIECES if pc not in _FIRST_WEIGHTS]]
_LATE_GRADS = _FIRST_WEIGHTS + [("even_w_out", 0)]
_GRAD_PIECES = [_LATE_GRADS, [pc for pc in _ALL_PIECES if pc not in _LATE_GRADS]]
_PACK_LANES = 1024
_PACK_ROW_TILE = 256


def _piece_key(piece):
    name, idx = piece
    return f"{name}{idx}" if name in _PER_LAYER else name


_KINDS = ("gu", "rows", "misc")


def _kind(piece):
    if piece[0] in ("ffn1_w_gu", "ffn2_w_gu"):
        return "gu"
    return "rows" if _SHARD_AXIS[piece[0]] == 1 else "misc"


def _of_kind(pieces, shapes, kind):
    return [(pc, shp) for pc, shp in zip(pieces, shapes) if _kind(pc) == kind]


def _pad_rows(flat, axis):
    pad = [(0, 0)] * flat.ndim
    pad[axis] = (0, -flat.shape[axis] % _PACK_ROW_TILE)
    return jnp.pad(flat, pad)


def _bundle(get, pieces, dtype):
    take = lambda kind: [get(pc).astype(dtype) for pc in pieces if _kind(pc) == kind]
    return [jnp.stack(take("gu")), jnp.concatenate(take("rows"), axis=0),
            _pad_rows(jnp.concatenate([a.reshape(-1, _PACK_LANES) for a in take("misc")], axis=0), 0)]


def _unbundle(bufs, pieces, shapes):
    out = {}
    gu = _of_kind(pieces, shapes, "gu")
    stacked = bufs[0].reshape((len(gu),) + gu[0][1])
    for j, (pc, _) in enumerate(gu):
        out[pc] = stacked[j]
    for buf, kind in ((bufs[1], "rows"), (bufs[2], "misc")):
        r0 = 0
        for pc, shp in _of_kind(pieces, shapes, kind):
            n = math.prod(shp) // _PACK_LANES
            out[pc] = buf[r0:r0 + n].reshape(shp)
            r0 += n
    return out


def _unshard(gathered, pieces, shapes):
    full = {}
    for j, (pc, shp) in enumerate(_of_kind(pieces, shapes, "gu")):
        full[_piece_key(pc)] = jnp.moveaxis(gathered[0][:, j], 0, 1).reshape(shp[0], N_DEV * shp[1])
    r0 = 0
    for pc, shp in _of_kind(pieces, shapes, "rows"):
        full[_piece_key(pc)] = gathered[1][:, r0:r0 + shp[0]].reshape(N_DEV * shp[0], shp[1])
        r0 += shp[0]
    r0 = 0
    for pc, shp in _of_kind(pieces, shapes, "misc"):
        n = math.prod(shp) // _PACK_LANES
        seg = gathered[2][:, r0:r0 + n].reshape((N_DEV,) + shp)
        full[_piece_key(pc)] = jnp.moveaxis(seg, 0, 1).reshape(shp[0], N_DEV * shp[1])
        r0 += n
    return full


def _to_shards(full, pieces, shapes):
    get = lambda pc: full[_piece_key(pc)].astype(BF16)
    cols = lambda pc, shp: jnp.moveaxis(get(pc).reshape(shp[0], N_DEV, shp[1]), 1, 0)
    gu = jnp.stack([cols(pc, shp) for pc, shp in _of_kind(pieces, shapes, "gu")], axis=1)
    rows = jnp.concatenate([get(pc).reshape((N_DEV,) + shp) for pc, shp in _of_kind(pieces, shapes, "rows")], axis=1)
    misc = jnp.concatenate([cols(pc, shp).reshape(N_DEV, -1, _PACK_LANES)
                            for pc, shp in _of_kind(pieces, shapes, "misc")], axis=1)
    return [gu, rows, _pad_rows(misc, 1)]


def _layer_weights(full):
    W = dict(full)
    if "fox_w_in" in W:
        W["fox_w_in"] = jnp.pad(W["fox_w_in"], ((0, 0), (0, FOX_IN_PAD - W["fox_w_in"].shape[1])))
    for n in ("rwkv_w2", "rwkv_a2", "rwkv_g2"):
        if n in W:
            W[n] = W[n].astype(F32)
    return W


def _pack_small(vals):
    flat = jnp.concatenate([v.reshape(1, -1) for v in vals], axis=1)
    n = flat.shape[1]
    return jnp.pad(flat, ((0, 0), (0, -n % 128)))


def _unpack_small(flat, shapes):
    out, c0 = [], 0
    for shp in shapes:
        n = math.prod(shp)
        out.append(flat[0, c0:c0 + n].reshape(shp))
        c0 += n
    return out


def kernel(x, p, ffn1_norm, ffn1_w_gu, ffn1_w_down, mix_norm, ffn2_norm, ffn2_w_gu, ffn2_w_down, ple_norm, ple_w_gate, ple_w_proj, even_w_in, even_w_out, swa_sinks, rwkv_mu, rwkv_w0, rwkv_w2, rwkv_a0, rwkv_a2, rwkv_g2, rwkv_k_k, rwkv_k_a, rwkv_r_k, rwkv_ln_w, rwkv_ln_b, fox_w_in, fox_b_f, fox_w_out, final_norm, loss_target, m_ffn1_norm, m_ffn1_w_gu, m_ffn1_w_down, m_mix_norm, m_ffn2_norm, m_ffn2_w_gu, m_ffn2_w_down, m_ple_norm, m_ple_w_gate, m_ple_w_proj, m_even_w_in, m_even_w_out, m_swa_sinks, m_rwkv_mu, m_rwkv_w0, m_rwkv_w2, m_rwkv_a0, m_rwkv_a2, m_rwkv_g2, m_rwkv_k_k, m_rwkv_k_a, m_rwkv_r_k, m_rwkv_ln_w, m_rwkv_ln_b, m_fox_w_in, m_fox_b_f, m_fox_w_out, m_final_norm, v_ffn1_norm, v_ffn1_w_gu, v_ffn1_w_down, v_mix_norm, v_ffn2_norm, v_ffn2_w_gu, v_ffn2_w_down, v_ple_norm, v_ple_w_gate, v_ple_w_proj, v_even_w_in, v_even_w_out, v_swa_sinks, v_rwkv_mu, v_rwkv_w0, v_rwkv_w2, v_rwkv_a0, v_rwkv_a2, v_rwkv_g2, v_rwkv_k_k, v_rwkv_k_a, v_rwkv_r_k, v_rwkv_ln_w, v_rwkv_ln_b, v_fox_w_in, v_fox_b_f, v_fox_w_out, v_final_norm):
    given = dict(locals())
    w = {n: given[n] for n in _WEIGHTS}
    m = {n: given["m_" + n] for n in _WEIGHTS}
    v = {n: given["v_" + n] for n in _WEIGHTS}
    small_shapes = [w[n].shape for n in _REPLICATED]
    piece = lambda d, pc: d[pc[0]][pc[1]]
    shapes = [[piece(w, pc).shape for pc in pieces] for pieces in _PIECES]
    gshapes = [[piece(w, pc).shape for pc in pieces] for pieces in _GRAD_PIECES]
    w_send = [_bundle(lambda pc: piece(w, pc), pieces, BF16) for pieces in _PIECES]

    W = _layer_weights(_unshard(all_gather(w_send[0], "weights_all_gather"), _PIECES[0], shapes[0]))
    for i in range(2):
        for n in ("ffn1_norm", "mix_norm", "ffn2_norm", "ple_norm"):
            W[f"{n}{i}"] = w[n][i].reshape(1, -1)
    for n in ("swa_sinks", "rwkv_mu", "rwkv_w0", "rwkv_a0", "rwkv_k_k", "rwkv_k_a", "rwkv_r_k", "rwkv_ln_w",
              "rwkv_ln_b", "final_norm"):
        W[n] = w[n].reshape(1, -1)
    n_f = fox_b_f.shape[1]
    W["fox_b_f"] = jnp.pad(fox_b_f.reshape(1, n_f), ((0, 0), (0, 128 - n_f)))
    n_fox = fox_w_in.shape[2] * N_DEV

    def layer1_weights(gathered):
        return _layer_weights(_unshard(gathered, _PIECES[1], shapes[1]))

    def early_grads(G):
        G = dict(G, fox_w_in=G["fox_w_in"][:, :n_fox])
        return _to_shards(G, _GRAD_PIECES[1], gshapes[1])

    loss_row, dx, G, parts_early = device_step(x[0], p[:, 0], loss_target[0], W, w_send[1], layer1_weights,
                                               early_grads)

    parts = [all_to_all(_to_shards(G, _GRAD_PIECES[0], gshapes[0]), "grads_all_to_all"), parts_early]
    out_g, out_d, out_m, out_v = {}, {}, {}, {}
    rows2d = lambda a, lead: a.reshape(a.shape[:lead] + (-1, a.shape[-1]))
    for li, pieces in enumerate(_GRAD_PIECES):
        wmv = [_bundle(lambda pc, d=d: piece(d, pc), pieces, F32) for d in (w, m, v)]
        res = [adamw(*[rows2d(b[ki], 0) for b in wmv], rows2d(parts[li][ki], 1), f"adamw_{kind}{li}")
               for ki, kind in enumerate(_KINDS)]
        for oi, out in enumerate((out_g, out_d, out_m, out_v)):
            for pc, a in _unbundle([r[oi] for r in res], pieces, gshapes[li]).items():
                out.setdefault(pc[0], {})[pc[1]] = a
    for out in (out_g, out_d, out_m, out_v):
        for n in _SHARDED:
            out[n] = jnp.stack([out[n][i] for i in sorted(out[n])])

    gsmall = {}
    for n in ("ffn1_norm", "mix_norm", "ffn2_norm", "ple_norm"):
        gsmall[n] = jnp.concatenate([G[f"{n}0"], G[f"{n}1"]], axis=0)
    for n in ("swa_sinks", "rwkv_mu", "rwkv_w0", "rwkv_a0", "rwkv_k_k", "rwkv_k_a", "rwkv_r_k", "rwkv_ln_w",
              "rwkv_ln_b", "final_norm"):
        gsmall[n] = G[n]
    gsmall["fox_b_f"] = G["fox_b_f"][:, :n_f]
    small = _pack_small([gsmall[n] for n in _REPLICATED] + [loss_row[:, :1]])
    small_parts = all_gather([small], "small_all_gather")[0]
    pad1 = lambda vals: _pack_small(vals + [jnp.zeros((1, 1), F32)])
    gs, ds, nms, nvs = adamw(pad1([w[n] for n in _REPLICATED]), pad1([m[n] for n in _REPLICATED]),
                             pad1([v[n] for n in _REPLICATED]), small_parts, "adamw_replicated")
    out_g.update(zip(_REPLICATED, _unpack_small(gs, small_shapes)))
    out_d.update(zip(_REPLICATED, _unpack_small(ds, small_shapes)))
    out_m.update(zip(_REPLICATED, _unpack_small(nms, small_shapes)))
    out_v.update(zip(_REPLICATED, _unpack_small(nvs, small_shapes)))
    n_small = sum(math.prod(s) for s in small_shapes)
    loss = gs[0, n_small]

    return (loss, dx[None], *[out_g[n] for n in _WEIGHTS], *[out_d[n] for n in _WEIGHTS],
            *[out_m[n] for n in _WEIGHTS], *[out_v[n] for n in _WEIGHTS])
```

```python
import functools
import math

import numpy as np
import jax
import jax.numpy as jnp
from jax import lax
from jax.experimental import pallas as pl
from jax.experimental.pallas import tpu as pltpu

F32 = jnp.float32
BF16 = jnp.bfloat16

D_MODEL = 1024
HEAD_DIM = 64
BLOCK = 128
SWA_HEADS = 8
SWA_KV_HEADS = 2
SWA_GROUP = 4
RWKV_HEADS = 8
RWKV_DIM = 512
FOX_HEADS = 16
FOX_DIM = 1024
D_FF = 2816
NORM_EPS = 1e-6
GN_EPS = 64e-5
L2_EPS = 1e-12
SWA_Q = 512
SWA_KV = 128
SWA_COLS = 768
FOX_IN_PAD = 3200
N_DEV = 8
ADAM_LR = 0.001
ADAM_B1 = 0.9
ADAM_B2 = 0.999
ADAM_EPS = 1e-08
ADAM_WD = 0.01
ADAM_STEP = 10

V7X_VMEM_LIMIT = 56 * 1024 * 1024
FOX_TILE = 512
CUMSUM_BLOCK = 256
SCAN_GROUP = 8
SCAN_CHUNK = 32

_NN = (((1,), (0,)), ((), ()))
_NT = (((1,), (1,)), ((), ()))
_TN = (((0,), (0,)), ((), ()))
_DIMS = {"nn": _NN, "nt": _NT, "tn": _TN}


def _pick(n, target, mult=128):
    best = None
    for t in range(mult, min(n, target) + 1, mult):
        if n % t == 0:
            best = t
    return best or n


def _cparams(sem):
    return pltpu.CompilerParams(dimension_semantics=sem, vmem_limit_bytes=V7X_VMEM_LIMIT)


def _dot(a, b, dims):
    return lax.dot_general(a.astype(BF16), b.astype(BF16), dims, preferred_element_type=F32)


@jax.custom_vjp
def bdot(a, b):
    return _dot(a, b, _NN)


def _bdot_fwd(a, b):
    return _dot(a, b, _NN), (a, b)


def _bdot_bwd(res, g):
    a, b = res
    return _dot(g, b, _NT), _dot(a, g, _TN)


bdot.defvjp(_bdot_fwd, _bdot_bwd)


@jax.custom_vjp
def bdot_nt(a, b):
    return _dot(a, b, _NT)


def _bdot_nt_fwd(a, b):
    return _dot(a, b, _NT), (a, b)


def _bdot_nt_bwd(res, g):
    a, b = res
    return _dot(g, b, _NN), _dot(g, a, _TN)


bdot_nt.defvjp(_bdot_nt_fwd, _bdot_nt_bwd)


@jax.custom_vjp
def _segsum(x, bd):
    return _dot2(x, bd.astype(BF16))


def _segsum_fwd(x, bd):
    return _segsum(x, bd), bd


def _segsum_bwd(bd, g):
    return _dot2(g, bd.astype(BF16)), jnp.zeros_like(bd)


_segsum.defvjp(_segsum_fwd, _segsum_bwd)


def _sigmoid(x):
    return 1.0 / (1.0 + jnp.exp(-x))


def _sigmoid_tanh(x):
    return 0.5 * jnp.tanh(0.5 * x) + 0.5


def _softplus(x):
    return jnp.maximum(x, 0.0) + jnp.log(1.0 + jnp.exp(-jnp.abs(x)))


def matmul(a, b, mode, name, out_dtype=F32, scale=1.0, res=None, tm=1024, tn=1408, tk=1024):
    if mode == "nn":
        (M, K), (K2, N) = a.shape, b.shape
    elif mode == "nt":
        (M, K), (N, K2) = a.shape, b.shape
    else:
        (K, M), (K2, N) = a.shape, b.shape
    assert K == K2, (a.shape, b.shape, mode)
    tm, tn, tk = _pick(M, tm), _pick(N, tn), _pick(K, tk)
    nk = K // tk
    has_res = res is not None

    def body(*refs):
        if has_res:
            a_ref, b_ref, r_ref, o_ref, acc = refs
        else:
            a_ref, b_ref, o_ref, acc = refs
        kk = pl.program_id(2)

        @pl.when(kk == 0)
        def _():
            acc[...] = jnp.zeros_like(acc)

        acc[...] += _dot(a_ref[...], b_ref[...], _DIMS[mode])

        @pl.when(kk == nk - 1)
        def _():
            v = acc[...]
            if scale != 1.0:
                v = v * scale
            if has_res:
                v = v + r_ref[...].astype(F32)
            o_ref[...] = v.astype(out_dtype)

    if mode == "tn":
        a_spec = pl.BlockSpec((tk, tm), lambda i, j, k: (k, i))
    else:
        a_spec = pl.BlockSpec((tm, tk), lambda i, j, k: (i, k))
    if mode == "nt":
        b_spec = pl.BlockSpec((tn, tk), lambda i, j, k: (j, k))
    else:
        b_spec = pl.BlockSpec((tk, tn), lambda i, j, k: (k, j))
    o_spec = pl.BlockSpec((tm, tn), lambda i, j, k: (i, j))
    in_specs = [a_spec, b_spec] + ([o_spec] if has_res else [])
    args = (a, b) + ((res,) if has_res else ())
    return pl.pallas_call(
        body,
        grid=(M // tm, N // tn, nk),
        in_specs=in_specs,
        out_specs=o_spec,
        out_shape=jax.ShapeDtypeStruct((M, N), out_dtype),
        scratch_shapes=[pltpu.VMEM((tm, tn), F32)],
        compiler_params=_cparams(("parallel", "parallel", "arbitrary")),
        name=name,
    )(*args)


def _row_spec(r, tm):
    if isinstance(r, tuple):
        arr, width, blk = r
        return arr, pl.BlockSpec((tm, width), lambda i, blk=blk: (i, blk))
    return r, pl.BlockSpec((tm, r.shape[1]), lambda i: (i, 0))


def _whole_spec(p):
    return pl.BlockSpec(p.shape, lambda i: (0,) * p.ndim)


def rowwise(fn, rows, params, outs, name, tm=256):
    arrs, specs = zip(*[_row_spec(r, tm) for r in rows])
    S = arrs[0].shape[0]
    tm = min(tm, S)
    arrs, specs = zip(*[_row_spec(r, tm) for r in rows])
    n_in = len(rows) + len(params)

    def body(*refs):
        res = fn(*[r[...] for r in refs[:n_in]])
        for o_ref, v in zip(refs[n_in:], res):
            o_ref[...] = v.astype(o_ref.dtype)

    return pl.pallas_call(
        body,
        grid=(S // tm,),
        in_specs=list(specs) + [_whole_spec(p) for p in params],
        out_specs=[pl.BlockSpec((tm, c), lambda i: (i, 0)) for c, _ in outs],
        out_shape=[jax.ShapeDtypeStruct((S, c), dt) for c, dt in outs],
        compiler_params=_cparams(("parallel",)),
        name=name,
    )(*arrs, *params)


def rowwise_vjp(fn, rows, params, cots, name, need=None, row_dtype=F32, consts=(), tm=256):
    nr, npar, nc, nk = len(rows), len(params), len(cots), len(consts)
    need = [True] * nr if need is None else need
    arrs, _ = zip(*[_row_spec(r, tm) for r in rows])
    S = arrs[0].shape[0]
    tm = min(tm, S)
    arrs, specs = zip(*[_row_spec(r, tm) for r in rows])
    carrs, cspecs = zip(*[_row_spec(c, tm) for c in cots])
    widths = [s.block_shape[1] for s in specs]
    n_in = nr + npar + nk + nc

    def body(*refs):
        i = pl.program_id(0)
        xs = [r[...].astype(F32) for r in refs[:nr]]
        ps = [r[...] for r in refs[nr:nr + npar]]
        ks = [r[...] for r in refs[nr + npar:nr + npar + nk]]
        cs = [r[...].astype(F32) for r in refs[nr + npar + nk:n_in]]
        outs, vjp = jax.vjp(lambda *a: fn(*a, *ks), *xs, *ps)
        grads = vjp(tuple(cs))
        o = n_in
        for j in range(nr):
            if need[j]:
                refs[o][...] = grads[j].astype(refs[o].dtype)
                o += 1
        for j in range(npar):
            g_ref = refs[o + j]

            @pl.when(i == 0)
            def _(g_ref=g_ref):
                g_ref[...] = jnp.zeros_like(g_ref)

            g_ref[...] += grads[nr + j]

    out_specs = [pl.BlockSpec((tm, w), lambda i: (i, 0)) for w, nd in zip(widths, need) if nd]
    out_shape = [jax.ShapeDtypeStruct((S, w), row_dtype) for w, nd in zip(widths, need) if nd]
    out_specs += [_whole_spec(p) for p in params]
    out_shape += [jax.ShapeDtypeStruct(p.shape, F32) for p in params]
    res = pl.pallas_call(
        body,
        grid=(S // tm,),
        in_specs=list(specs) + [_whole_spec(p) for p in params] + [_whole_spec(k) for k in consts] + list(cspecs),
        out_specs=out_specs,
        out_shape=out_shape,
        compiler_params=_cparams(("arbitrary",)),
        name=name,
    )(*arrs, *params, *consts, *carrs)
    nrow = sum(need)
    return list(res[:nrow]), list(res[nrow:])


def _rms(x, g):
    return x * lax.rsqrt(jnp.mean(x * x, axis=-1, keepdims=True) + NORM_EPS) * g


def _f_rms(x, g):
    return (_rms(x, g),)


def _f_rms_res(x, g):
    return _rms(x, g), x


def _f_ple(x, z, pp):
    return (x + _sigmoid(z) * pp,)


def _f_mix(h, sh, mu):
    return (h + (sh - h) * mu,)


def _f_logf(fz, bf):
    return (-_softplus(-(fz + bf)),)


def _f_rwkv_pre(hk, hw, ha, hg, w0, w2, a0, a2, g2, k_k, k_a, bd):
    wlog = -_softplus(-(w0 + bdot(jnp.tanh(hw), w2))) - 0.5
    a = _sigmoid(a0 + bdot(ha, a2))
    g = bdot(_sigmoid(hg), g2)
    kk = hk * k_k
    kk = kk / jnp.maximum(jnp.sqrt(_segsum(kk * kk, bd)), L2_EPS)
    k2 = hk * (1.0 + (a - 1.0) * k_a)
    decay = jnp.exp(-jnp.exp(wlog))
    return decay, k2, kk, kk * a, g


def _f_rwkv_post(y, r, k2, v, g, ln_w, ln_b, r_k, bd):
    mean = _segsum(y, bd) * (1.0 / HEAD_DIM)
    d = y - mean
    var = _segsum(d * d, bd) * (1.0 / HEAD_DIM)
    yn = d * lax.rsqrt(var + GN_EPS) * ln_w + ln_b
    yn = yn + _segsum(r * k2 * r_k, bd) * v
    return (yn * g,)


def loss_head(x, target, gf, tm=256):
    S, D = x.shape
    tm = min(tm, S)

    def f(xt, g, tt):
        err = _rms(xt, g) - tt
        return 0.5 * jnp.sum(err * err) * (1.0 / D)

    def body(x_ref, t_ref, g_ref, dx_ref, dg_ref, l_ref):
        i = pl.program_id(0)
        val, (dx, dg) = jax.value_and_grad(f, argnums=(0, 1))(x_ref[...], g_ref[...], t_ref[...])

        @pl.when(i == 0)
        def _():
            dg_ref[...] = jnp.zeros_like(dg_ref)
            l_ref[...] = jnp.zeros_like(l_ref)

        dx_ref[...] = dx
        dg_ref[...] += dg
        l_ref[...] += jnp.full(l_ref.shape, val, F32)

    row = pl.BlockSpec((tm, D), lambda i: (i, 0))
    vec = pl.BlockSpec((1, D), lambda i: (0, 0))
    return pl.pallas_call(
        body,
        grid=(S // tm,),
        in_specs=[row, row, vec],
        out_specs=[row, vec, pl.BlockSpec((1, 128), lambda i: (0, 0))],
        out_shape=[jax.ShapeDtypeStruct((S, D), F32), jax.ShapeDtypeStruct((1, D), F32),
                   jax.ShapeDtypeStruct((1, 128), F32)],
        compiler_params=_cparams(("arbitrary",)),
        name="loss_head",
    )(x, target, gf)


def _swa_block(q, kp, kc, vp, vc, sink, slope, n):
    k = jnp.concatenate([kp, kc], axis=0)
    v = jnp.concatenate([vp, vc], axis=0)
    rows = q.shape[0]
    logits = bdot_nt(q, k) * (HEAD_DIM ** -0.5)
    qi = lax.broadcasted_iota(jnp.int32, (rows, 2 * BLOCK), 0) & (BLOCK - 1)
    ki = lax.broadcasted_iota(jnp.int32, (rows, 2 * BLOCK), 1)
    dist = qi + BLOCK - ki
    valid = (dist >= 0) & (dist < BLOCK) & ((n - 1) * BLOCK + ki >= 0)
    logits = logits - slope * dist.astype(F32)
    logits = jnp.where(valid, logits, -jnp.inf)
    m = jnp.maximum(jnp.max(logits, axis=-1, keepdims=True), sink)
    pr = jnp.exp(logits - m)
    denom = jnp.sum(pr, axis=-1, keepdims=True) + jnp.exp(sink - m)
    return bdot(pr / denom, v)


def _swa_specs(S):
    nb = S // BLOCK
    q_spec = pl.BlockSpec((None, SWA_GROUP, BLOCK, HEAD_DIM), lambda h, n: (h, 0, n, 0))
    kc_spec = pl.BlockSpec((None, BLOCK, HEAD_DIM), lambda h, n: (h, n, 0))
    kp_spec = pl.BlockSpec((None, BLOCK, HEAD_DIM), lambda h, n: (h, jnp.maximum(n - 1, 0), 0))
    col_spec = pl.BlockSpec((None, SWA_GROUP * BLOCK, 1), lambda h, n: (h, 0, 0))
    return nb, q_spec, kp_spec, kc_spec, col_spec


def swa_fwd(q, k, v, sink_col, slope_col):
    S = q.shape[2]
    nb, q_spec, kp_spec, kc_spec, col_spec = _swa_specs(S)

    def body(q_ref, kp_ref, kc_ref, vp_ref, vc_ref, s_ref, a_ref, o_ref):
        n = pl.program_id(1)
        qq = q_ref[...].reshape(SWA_GROUP * BLOCK, HEAD_DIM)
        out = _swa_block(qq, kp_ref[...], kc_ref[...], vp_ref[...], vc_ref[...], s_ref[...], a_ref[...], n)
        o_ref[...] = out.reshape(SWA_GROUP, BLOCK, HEAD_DIM)

    return pl.pallas_call(
        body,
        grid=(SWA_KV_HEADS, nb),
        in_specs=[q_spec, kp_spec, kc_spec, kp_spec, kc_spec, col_spec, col_spec],
        out_specs=q_spec,
        out_shape=jax.ShapeDtypeStruct(q.shape, F32),
        compiler_params=_cparams(("parallel", "parallel")),
        name="swa_fwd",
    )(q, k, k, v, v, sink_col, slope_col)


def swa_bwd(q, k, v, sink_col, slope_col, dout):
    S = q.shape[2]
    nb, q_spec, kp_spec, kc_spec, col_spec = _swa_specs(S)

    def body(q_ref, kp_ref, kc_ref, vp_ref, vc_ref, s_ref, a_ref, do_ref,
             dq_ref, dkp_ref, dkc_ref, dvp_ref, dvc_ref, ds_ref):
        n = pl.program_id(1)
        qq = q_ref[...].reshape(SWA_GROUP * BLOCK, HEAD_DIM)
        slope = a_ref[...]
        f = lambda a, b, c, d, e, s: _swa_block(a, b, c, d, e, s, slope, n)
        _, vjp = jax.vjp(f, qq, kp_ref[...], kc_ref[...], vp_ref[...], vc_ref[...], s_ref[...])
        dq, dkp, dkc, dvp, dvc, ds = vjp(do_ref[...].reshape(SWA_GROUP * BLOCK, HEAD_DIM))
        dq_ref[...] = dq.reshape(SWA_GROUP, BLOCK, HEAD_DIM)
        dkp_ref[...] = dkp
        dkc_ref[...] = dkc
        dvp_ref[...] = dvp
        dvc_ref[...] = dvc

        @pl.when(n == 0)
        def _():
            ds_ref[...] = jnp.zeros_like(ds_ref)

        ds_ref[...] += ds

    kv_shape = jax.ShapeDtypeStruct(k.shape, F32)
    return pl.pallas_call(
        body,
        grid=(SWA_KV_HEADS, nb),
        in_specs=[q_spec, kp_spec, kc_spec, kp_spec, kc_spec, col_spec, col_spec, q_spec],
        out_specs=[q_spec, kc_spec, kc_spec, kc_spec, kc_spec, col_spec],
        out_shape=[jax.ShapeDtypeStruct(q.shape, F32), kv_shape, kv_shape, kv_shape, kv_shape,
                   jax.ShapeDtypeStruct(sink_col.shape, F32)],
        compiler_params=_cparams(("parallel", "arbitrary")),
        name="swa_bwd",
    )(q, k, k, v, v, sink_col, slope_col, dout)


def _split2(x):
    hi = x.astype(BF16)
    return (x - hi.astype(F32)).astype(BF16), hi


def _dot2_many(xs, m, single=False):
    rows = xs[0].shape[0]
    if single:
        res = jnp.dot(jnp.concatenate([x.astype(BF16) for x in xs], axis=0), m, preferred_element_type=F32)
        return [res[i * rows:(i + 1) * rows] for i in range(len(xs))]
    res = jnp.dot(jnp.concatenate([p for x in xs for p in _split2(x)], axis=0), m, preferred_element_type=F32)
    return [res[(2 * i) * rows:(2 * i + 1) * rows] + res[(2 * i + 1) * rows:(2 * i + 2) * rows]
            for i in range(len(xs))]


def _dot2(x, m):
    return _dot2_many([x], m)[0]


def _seg_sums(xs, bd, single=False):
    w = bd.shape[0]
    halves = _dot2_many([x[:, i:i + w] for x in xs for i in range(0, x.shape[1], w)], bd, single)
    n = xs[0].shape[1] // w
    return [jnp.concatenate(halves[i * n:(i + 1) * n], axis=1) for i in range(len(xs))]


def _seg_sum(x, bd):
    return _seg_sums([x], bd)[0]


def _scan_consts():
    r = np.arange(256)
    bd = (r[:, None] // HEAD_DIM == r[None, :] // HEAD_DIM).astype(np.float32)
    c = np.arange(RWKV_DIM)
    e = (np.arange(HEAD_DIM)[:, None] // SCAN_GROUP == c[None, :] // HEAD_DIM).astype(np.float32)
    diag = (np.arange(HEAD_DIM)[:, None] == c[None, :] % HEAD_DIM).astype(np.float32)
    return jnp.asarray(bd, BF16), jnp.asarray(e, BF16), jnp.asarray(diag, F32)


def _to_colblocks(a):
    S = a.shape[0]
    a = a.reshape(S // SCAN_GROUP, SCAN_GROUP, RWKV_HEADS, HEAD_DIM)
    return a.transpose(0, 3, 2, 1).reshape(S // SCAN_GROUP, HEAD_DIM, RWKV_HEADS * SCAN_GROUP)


def _roll_up(rows):
    return pltpu.roll(rows, rows.shape[0] - 1, 0)


def _scan_pair_rows(aux, base, kk_ref, w_ref, b_ref, k_ref, bd):
    G = SCAN_GROUP
    kk_nx = _roll_up(kk_ref[pl.ds(base, G), :])
    aux[0] = w_ref[pl.ds(base, G), :] * kk_nx
    aux[1], aux[2] = _seg_sums([b_ref[pl.ds(base, G), :] * kk_nx, k_ref[pl.ds(base, G), :] * kk_nx], bd)


def _scan_pair(St, t0, base, col_g, lane_t, aux, kk_ref, w_ref, b_ref, k_ref, bd, e):
    t1 = t0 + 1
    row = lambda ref, t: ref[pl.ds(base + t, 1), :]
    arow = lambda i: aux[i, pl.ds(t0, 1), :]
    u0, m1 = _seg_sums([St * row(kk_ref, t0), St * arow(0)], bd)
    v0, v1 = _dot2_many([jnp.where(lane_t == t0, col_g, 0.0), jnp.where(lane_t == t1, col_g, 0.0)], e)
    u1 = m1 - u0 * arow(1) + v0 * arow(2)
    S0 = St * row(w_ref, t0) - u0 * row(b_ref, t0) + v0 * row(k_ref, t0)
    S1 = S0 * row(w_ref, t1) - u1 * row(b_ref, t1) + v1 * row(k_ref, t1)
    return (S0, S1), (u0, u1), (v0, v1)


def rwkv_scan_fwd(r, w, k, kk, b, vB, gather_srcs):
    S, C = r.shape
    N, G = HEAD_DIM, SCAN_GROUP
    chunk = min(SCAN_CHUNK, S)
    nchunk, ng = S // chunk, chunk // G
    bd, e, diag = _scan_consts()

    nx = len(gather_srcs)

    def body(*refs):
        r_ref, w_ref, k_ref, kk_ref, b_ref, vB_ref, bd_ref, e_ref, dg_ref = refs[:9]
        y_ref, ck_ref = refs[9 + nx:11 + nx]
        S_ref, aux, send_sems, recv_sems, local_sems = refs[11 + 2 * nx:]
        c = pl.program_id(0)
        _exchange_during(c, nchunk, True, refs[9:9 + nx], refs[11 + nx:11 + 2 * nx], send_sems, recv_sems, local_sems)

        @pl.when(c == 0)
        def _():
            S_ref[...] = jnp.zeros_like(S_ref)

        ck_ref[...] = S_ref[...]
        sub = lax.broadcasted_iota(jnp.int32, (G, C), 0)
        lane_t = lax.broadcasted_iota(jnp.int32, (N, N), 1) & (G - 1)

        def group(g, St):
            base = pl.multiple_of(g * G, G)
            vb = vB_ref[g]
            _scan_pair_rows(aux, base, kk_ref, w_ref, b_ref, k_ref, bd_ref[...])
            ys = jnp.zeros((G, C), F32)
            def emit(ys, states, t0):
                steps = (t0, t0 + 1)
                y_bs = _seg_sums([S_t * r_ref[pl.ds(base + tt, 1), :] for S_t, tt in zip(states, steps)], bd_ref[...],
                                 single=True)
                for y_b, tt in zip(y_bs, steps):
                    ys = jnp.where(sub == tt, jnp.sum(y_b * dg_ref[...], axis=0, keepdims=True), ys)
                return ys

            pending = None
            for t0 in range(0, G, 2):
                states, _, _ = _scan_pair(St, t0, base, vb, lane_t, aux, kk_ref, w_ref, b_ref, k_ref, bd_ref[...],
                                          e_ref[...])
                if pending is not None:
                    ys = emit(ys, *pending)
                pending = (states, t0)
                St = states[1]
            y_ref[pl.ds(base, G), :] = emit(ys, *pending)
            return St

        S_ref[...] = lax.fori_loop(0, ng, group, S_ref[...])

    row = pl.BlockSpec((chunk, C), lambda c: (c, 0))
    col = pl.BlockSpec((ng, N, N), lambda c: (c, 0, 0))
    res = pl.pallas_call(
        body,
        grid=(nchunk,),
        in_specs=[row] * 5 + [col, _whole_spec(bd), _whole_spec(e), _whole_spec(diag)] + [_ANY] * nx,
        out_specs=[row, pl.BlockSpec((None, N, C), lambda c: (c, 0, 0))] + [_ANY] * nx,
        out_shape=[jax.ShapeDtypeStruct((S, C), F32), jax.ShapeDtypeStruct((nchunk, N, C), F32)]
        + _exchange_out_shapes(True, gather_srcs),
        scratch_shapes=[pltpu.VMEM((N, C), F32), pltpu.VMEM((3, G, C), F32)] + _exchange_sems(nx),
        compiler_params=_cparams(("arbitrary",)),
        name="rwkv_scan_fwd",
    )(r, w, k, kk, b, vB, bd, e, diag, *gather_srcs)
    return res[0], res[1], list(res[2:])


def rwkv_scan_bwd(r, w, k, kk, b, vB, dyB, ckpt, scatter_srcs):
    S, C = r.shape
    N, G = HEAD_DIM, SCAN_GROUP
    chunk = min(SCAN_CHUNK, S)
    nchunk, ng = S // chunk, chunk // G
    nsteps = nchunk + 1
    bd, e, diag = _scan_consts()
    nx = len(scatter_srcs)

    def body(*refs):
        wf_ref, kf_ref, kkf_ref, bf_ref, vBf_ref, ck_ref = refs[:6]
        r_ref, w_ref, k_ref, kk_ref, b_ref, dyB_ref, bd_ref, e_ref, dg_ref = refs[6:15]
        dr_ref, dw_ref, dk_ref, dkk_ref, db_ref, dv_ref = refs[15 + nx:21 + nx]
        G_ref, sbuf, ubuf, vbuf, aux_f, aux_b, send_sems, recv_sems, local_sems = refs[21 + 2 * nx:]
        c = pl.program_id(0)
        _exchange_during(c, nsteps, False, refs[15:15 + nx], refs[21 + nx:21 + 2 * nx], send_sems, recv_sems,
                         local_sems)

        @pl.when(c == 0)
        def _():
            G_ref[...] = jnp.zeros_like(G_ref)
            sbuf[...] = jnp.zeros_like(sbuf)
            ubuf[...] = jnp.zeros_like(ubuf)
            vbuf[...] = jnp.zeros_like(vbuf)

        sf = c % 2
        sb = 1 - sf
        lane_t = lax.broadcasted_iota(jnp.int32, (N, N), 1) & (G - 1)
        sub = lax.broadcasted_iota(jnp.int32, (G, C), 0)
        colsum = lambda a: jnp.sum(a, axis=0, keepdims=True)

        def group(g, carry):
            St, Gt = carry
            base_f = pl.multiple_of(g * G, G)
            gb = ng - 1 - g
            base_b = pl.multiple_of(gb * G, G)
            vb, dyb = vBf_ref[g], dyB_ref[gb]
            row = lambda ref, t: ref[pl.ds(base_b + t, 1), :]
            _scan_pair_rows(aux_f, base_f, kkf_ref, wf_ref, bf_ref, kf_ref, bd_ref[...])
            b8 = b_ref[pl.ds(base_b, G), :]
            aux_b[0] = _roll_up(w_ref[pl.ds(base_b, G), :]) * b8
            aux_b[1], aux_b[2] = _seg_sums([_roll_up(kk_ref[pl.ds(base_b, G), :]) * b8,
                                            r_ref[pl.ds(base_b, G), :] * b8], bd_ref[...])
            rows = [jnp.zeros((G, C), F32) for _ in range(6)]

            def emit(rows, steps):
                d_vs = _seg_sums([Gt_ * row(k_ref, tt) for tt, Gt_, _, _ in steps], bd_ref[...], single=True)
                for (tt, Gt_, du_b, dy_b), d_vb in zip(steps, d_vs):
                    Sp, Sc = sbuf[sb, base_b + tt], sbuf[sb, base_b + tt + 1]
                    new = (colsum(Sc * dy_b), colsum(Gt_ * Sp), colsum(Gt_ * vbuf[sb, base_b + tt]),
                           colsum(Sp * du_b), -colsum(Gt_ * ubuf[sb, base_b + tt]), colsum(d_vb * dg_ref[...]))
                    rows = [jnp.where(sub == tt, n_, acc) for n_, acc in zip(new, rows)]
                return rows

            pending = None
            for i in range(G // 2):
                t0 = 2 * i
                states, us, vs = _scan_pair(St, t0, base_f, vb, lane_t, aux_f, kkf_ref, wf_ref, bf_ref, kf_ref,
                                            bd_ref[...], e_ref[...])
                for j, S_before in enumerate((St, states[0])):
                    sbuf[sf, base_f + t0 + j] = S_before
                    ubuf[sf, base_f + t0 + j] = us[j]
                    vbuf[sf, base_f + t0 + j] = vs[j]
                St = states[1]
                t0 = G - 2 - 2 * i
                t1 = t0 + 1
                arow = lambda j, t0=t0: aux_b[j, pl.ds(t0, 1), :]
                dy1, dy0 = _dot2_many([jnp.where(lane_t == t1, dyb, 0.0), jnp.where(lane_t == t0, dyb, 0.0)],
                                      e_ref[...])
                G1 = Gt + dy1 * row(r_ref, t1)
                m1, m2 = _seg_sums([G1 * row(b_ref, t1), G1 * arow(0)], bd_ref[...])
                du1 = -m1
                du0 = -(m2 + du1 * arow(1) + dy0 * arow(2))
                G0 = G1 * row(w_ref, t1) + du1 * row(kk_ref, t1) + dy0 * row(r_ref, t0)
                G_next = G0 * row(w_ref, t0) + du0 * row(kk_ref, t0)
                if pending is not None:
                    rows = emit(rows, pending)
                pending = ((t1, G1, du1, dy1), (t0, G0, du0, dy0))
                Gt = G_next
            rows = emit(rows, pending)
            for ref, val in zip((dr_ref, dw_ref, dk_ref, dkk_ref, db_ref, dv_ref), rows):
                ref[pl.ds(base_b, G), :] = val
            return St, Gt

        St, Gt = lax.fori_loop(0, ng, group, (ck_ref[...], G_ref[...]))
        sbuf[sf, chunk] = St
        G_ref[...] = jnp.where(c >= 1, Gt, G_ref[...])

    fwd_chunk = lambda c: jnp.maximum(nchunk - 1 - c, 0)
    bwd_chunk = lambda c: jnp.minimum(nchunk - c, nchunk - 1)
    row_f = pl.BlockSpec((chunk, C), lambda c: (fwd_chunk(c), 0))
    row_b = pl.BlockSpec((chunk, C), lambda c: (bwd_chunk(c), 0))
    col_f = pl.BlockSpec((ng, N, N), lambda c: (fwd_chunk(c), 0, 0))
    col_b = pl.BlockSpec((ng, N, N), lambda c: (bwd_chunk(c), 0, 0))
    rshape = jax.ShapeDtypeStruct((S, C), F32)
    res = pl.pallas_call(
        body,
        grid=(nsteps,),
        in_specs=[row_f] * 4 + [col_f, pl.BlockSpec((None, N, C), lambda c: (fwd_chunk(c), 0, 0))]
        + [row_b] * 5 + [col_b, _whole_spec(bd), _whole_spec(e), _whole_spec(diag)] + [_ANY] * nx,
        out_specs=[row_b] * 6 + [_ANY] * nx,
        out_shape=[rshape] * 6 + _exchange_out_shapes(False, scatter_srcs),
        scratch_shapes=[pltpu.VMEM((N, C), F32), pltpu.VMEM((2, chunk + 1, N, C), F32),
                        pltpu.VMEM((2, chunk, N, C), F32), pltpu.VMEM((2, chunk, N, C), F32),
                        pltpu.VMEM((3, G, C), F32), pltpu.VMEM((3, G, C), F32)] + _exchange_sems(nx),
        compiler_params=_cparams(("arbitrary",)),
        name="rwkv_scan_bwd",
    )(w, k, kk, b, vB, ckpt, r, w, k, kk, b, dyB, bd, e, diag, *scatter_srcs)
    return tuple(res[:6]) + (list(res[6:]),)


def seq_cumsum(x, reverse, name):
    S, C = x.shape
    tb = min(CUMSUM_BLOCK, S)
    nb = S // tb

    def body(x_ref, o_ref, carry):
        i = pl.program_id(0)

        @pl.when(i == 0)
        def _():
            carry[...] = jnp.zeros_like(carry)

        ri = lax.broadcasted_iota(jnp.int32, (tb, tb), 0)
        ci = lax.broadcasted_iota(jnp.int32, (tb, tb), 1)
        tri = jnp.where((ci >= ri) if reverse else (ci <= ri), 1.0, 0.0).astype(F32)
        xb = x_ref[...]
        out = jnp.dot(tri, xb, precision=lax.Precision.HIGHEST, preferred_element_type=F32) + carry[...]
        o_ref[...] = out
        carry[...] = carry[...] + jnp.sum(xb, axis=0, keepdims=True)

    idx = (lambda i: (nb - 1 - i, 0)) if reverse else (lambda i: (i, 0))
    return pl.pallas_call(
        body,
        grid=(nb,),
        in_specs=[pl.BlockSpec((tb, C), idx)],
        out_specs=pl.BlockSpec((tb, C), idx),
        out_shape=jax.ShapeDtypeStruct((S, C), F32),
        scratch_shapes=[pltpu.VMEM((1, C), F32)],
        compiler_params=_cparams(("arbitrary",)),
        name=name,
    )(x)


def _fox_logits(q, k, cq, ck, diagonal):
    s = _dot(q, k, _NT) * (HEAD_DIM ** -0.5) + cq - ck
    if not diagonal:
        return s
    row = lax.broadcasted_iota(jnp.int32, s.shape, 0)
    col = lax.broadcasted_iota(jnp.int32, s.shape, 1)
    return jnp.where(col <= row, s, -jnp.inf)


def _fox_tiles(n, by_query):
    pairs = [(i, j) for i in range(n) for j in range(i + 1)] if by_query else \
            [(i, j) for j in range(n) for i in range(j, n)]
    return (jnp.asarray(np.array([p[0] for p in pairs], np.int32)),
            jnp.asarray(np.array([p[1] for p in pairs], np.int32)))


def _fox_specs(t, Dh):
    qs = pl.BlockSpec((None, t, Dh), lambda h, s, qt, kt: (h, qt[s], 0))
    ks = pl.BlockSpec((None, t, Dh), lambda h, s, qt, kt: (h, kt[s], 0))
    cqs = pl.BlockSpec((None, t, 1), lambda h, s, qt, kt: (h, qt[s], 0))
    cks = pl.BlockSpec((None, 1, t), lambda h, s, qt, kt: (h, 0, kt[s]))
    return qs, ks, cqs, cks


def _fox_call(body, tiles, Hh, in_specs, out_specs, out_shape, scratch, name, args):
    spec = pltpu.PrefetchScalarGridSpec(num_scalar_prefetch=2, grid=(Hh, tiles[0].shape[0]), in_specs=in_specs,
                                        out_specs=out_specs, scratch_shapes=scratch)
    return pl.pallas_call(body, grid_spec=spec, out_shape=out_shape,
                          compiler_params=_cparams(("parallel", "arbitrary")), name=name)(*tiles, *args)


def fox_fwd(q, k, v, c_col, c_row):
    Hh, S, Dh = q.shape
    tq = tk = min(FOX_TILE, S)

    def body(qt_ref, kt_ref, q_ref, k_ref, v_ref, cq_ref, ck_ref, o_ref, lse_ref, m_s, l_s, acc_s):
        qi, ki = qt_ref[pl.program_id(1)], kt_ref[pl.program_id(1)]

        @pl.when(ki == 0)
        def _():
            m_s[...] = jnp.full_like(m_s, -jnp.inf)
            l_s[...] = jnp.zeros_like(l_s)
            acc_s[...] = jnp.zeros_like(acc_s)

        def tile(diagonal):
            s = _fox_logits(q_ref[...], k_ref[...], cq_ref[...], ck_ref[...], diagonal)
            m_old = m_s[...]
            m_new = jnp.maximum(m_old, jnp.max(s, axis=-1, keepdims=True))
            alpha = jnp.exp(m_old - m_new)
            p = jnp.exp(s - m_new)
            l_s[...] = alpha * l_s[...] + jnp.sum(p, axis=-1, keepdims=True)
            acc_s[...] = alpha * acc_s[...] + _dot(p, v_ref[...], _NN)
            m_s[...] = m_new

        @pl.when(ki != qi)
        def _():
            tile(False)

        @pl.when(ki == qi)
        def _():
            tile(True)
            o_ref[...] = acc_s[...] / l_s[...]
            lse_ref[...] = m_s[...] + jnp.log(l_s[...])

    qs, ks, cqs, cks = _fox_specs(tq, Dh)
    return _fox_call(
        body, _fox_tiles(S // tq, True), Hh, [qs, ks, ks, cqs, cks], [qs, cqs],
        [jax.ShapeDtypeStruct((Hh, S, Dh), F32), jax.ShapeDtypeStruct((Hh, S, 1), F32)],
        [pltpu.VMEM((tq, 1), F32), pltpu.VMEM((tq, 1), F32), pltpu.VMEM((tq, Dh), F32)],
        "fox_fwd", (q, k, v, c_col, c_row))


def fox_bwd(q, k, v, c_col, c_row, o, lse, do):
    Hh, S, Dh = q.shape
    tq = tk = min(FOX_TILE, S)
    nk = S // tk

    def body(qt_ref, kt_ref, q_ref, k_ref, v_ref, cq_ref, ck_ref, o_ref, lse_ref, do_ref,
             dq_ref, dr_ref, dk_ref, dv_ref, dc_ref, acc_s, row_s):
        step = pl.program_id(1)
        qi, ki = qt_ref[step], kt_ref[step]

        @pl.when(step == 0)
        def _():
            dk_ref[...] = jnp.zeros_like(dk_ref)
            dv_ref[...] = jnp.zeros_like(dv_ref)
            dc_ref[...] = jnp.zeros_like(dc_ref)

        @pl.when(ki == 0)
        def _():
            acc_s[...] = jnp.zeros_like(acc_s)
            row_s[...] = jnp.zeros_like(row_s)

        def tile(diagonal):
            q_t, kb, vb, do_t = q_ref[...], k_ref[...], v_ref[...], do_ref[...]
            s = _fox_logits(q_t, kb, cq_ref[...], ck_ref[...], diagonal)
            p = jnp.exp(s - lse_ref[...])
            delta = jnp.sum(do_t * o_ref[...], axis=-1, keepdims=True)
            ds = p * (_dot(do_t, vb, _NT) - delta)
            acc_s[...] += _dot(ds, kb, _NN)
            row_s[...] += jnp.sum(ds, axis=-1, keepdims=True)
            dk_ref[ki] += _dot(ds, q_t, _TN) * (HEAD_DIM ** -0.5)
            dv_ref[ki] += _dot(p, do_t, _TN)
            dc_ref[ki] += jnp.sum(ds, axis=0, keepdims=True)

        @pl.when(ki != qi)
        def _():
            tile(False)

        @pl.when(ki == qi)
        def _():
            tile(True)
            dq_ref[...] = acc_s[...] * (HEAD_DIM ** -0.5)
            dr_ref[...] = row_s[...]

    qs, ks, cqs, cks = _fox_specs(tq, Dh)
    head = lambda *blk: pl.BlockSpec((None,) + blk, lambda h, s, qt, kt: (h,) + (0,) * len(blk))
    dq, dr, dk, dv, dc = _fox_call(
        body, _fox_tiles(S // tq, True), Hh, [qs, ks, ks, cqs, cks, qs, cqs, qs],
        [qs, cqs, head(nk, tk, Dh), head(nk, tk, Dh), head(nk, 1, tk)],
        [jax.ShapeDtypeStruct((Hh, S, Dh), F32), jax.ShapeDtypeStruct((Hh, S, 1), F32),
         jax.ShapeDtypeStruct((Hh, nk, tk, Dh), F32), jax.ShapeDtypeStruct((Hh, nk, tk, Dh), F32),
         jax.ShapeDtypeStruct((Hh, nk, 1, tk), F32)],
        [pltpu.VMEM((tq, Dh), F32), pltpu.VMEM((tq, 1), F32)],
        "fox_bwd", (q, k, v, c_col, c_row, o, lse, do))
    return dq, dr, dk.reshape(Hh, S, Dh), dv.reshape(Hh, S, Dh), dc.reshape(Hh, 1, S)


def _heads(a, nh):
    S = a.shape[0]
    return a.reshape(S, nh, HEAD_DIM).transpose(1, 0, 2)


def _unheads(a):
    nh, S, _ = a.shape
    return a.transpose(1, 0, 2).reshape(S, nh * HEAD_DIM)


def _shift_down(a):
    return jnp.pad(a[:-1], ((1, 0), (0, 0)))


def _shift_up(a):
    return jnp.pad(a[1:], ((0, 1), (0, 0)))


def _block_diag_ones():
    i = np.arange(RWKV_DIM) // HEAD_DIM
    return jnp.asarray((i[:, None] == i[None, :]).astype(np.float32))


FFN_ROWS = 1024
FFN_COLS = 256


def _ffn_specs(S, F, tm, fc):
    nf = F // fc
    row = pl.BlockSpec((tm, D_MODEL), lambda i, j: (i, 0))
    vec = pl.BlockSpec((1, D_MODEL), lambda i, j: (0, 0))
    wg = pl.BlockSpec((D_MODEL, fc), lambda i, j: (0, j))
    wu = pl.BlockSpec((D_MODEL, fc), lambda i, j: (0, nf + j))
    wd = pl.BlockSpec((fc, D_MODEL), lambda i, j: (j, 0))
    hid = pl.BlockSpec((tm, fc), lambda i, j: (i, j))
    return nf, row, vec, wg, wu, wd, hid


def ffn_fwd(x, g_norm, w_gu, w_down, tag):
    S, F = x.shape[0], w_down.shape[0]
    tm, fc = min(FFN_ROWS, S), FFN_COLS
    nf, row, vec, wg, wu, wd, _ = _ffn_specs(S, F, tm, fc)

    def body(x_ref, g_ref, wg_ref, wu_ref, wd_ref, o_ref, hn_ref, hn_s, acc):
        j = pl.program_id(1)

        @pl.when(j == 0)
        def _():
            hn_s[...] = _rms(x_ref[...], g_ref[...]).astype(BF16)
            hn_ref[...] = hn_s[...]
            acc[...] = jnp.zeros_like(acc)

        g = _dot(hn_s[...], wg_ref[...], _NN)
        u = _dot(hn_s[...], wu_ref[...], _NN)
        acc[...] += _dot(g * _sigmoid_tanh(g) * u, wd_ref[...], _NN)

        @pl.when(j == nf - 1)
        def _():
            o_ref[...] = x_ref[...] + 0.5 * acc[...]

    out, hn = pl.pallas_call(
        body,
        grid=(S // tm, nf),
        in_specs=[row, vec, wg, wu, wd],
        out_specs=[row, row],
        out_shape=[jax.ShapeDtypeStruct((S, D_MODEL), F32), jax.ShapeDtypeStruct((S, D_MODEL), BF16)],
        scratch_shapes=[pltpu.VMEM((tm, D_MODEL), BF16), pltpu.VMEM((tm, D_MODEL), F32)],
        compiler_params=_cparams(("parallel", "arbitrary")),
        name=tag + "_fwd",
    )(x, g_norm, w_gu, w_gu, w_down)
    return out, (x, hn)


def ffn_bwd(dy, saved, g_norm, w_gu, w_down, tag):
    x, hn = saved
    S, F = x.shape[0], w_down.shape[0]
    tm, fc = min(FFN_ROWS, S), FFN_COLS
    nf, row, vec, wg, wu, wd, hid = _ffn_specs(S, F, tm, fc)

    def body(dy_ref, x_ref, hn_ref, g_ref, wg_ref, wu_ref, wd_ref, dx_ref, dgn_ref, a_ref, dg_ref, du_ref,
             dyh_s, dhn):
        i, j = pl.program_id(0), pl.program_id(1)

        @pl.when(j == 0)
        def _():
            dyh_s[...] = (0.5 * dy_ref[...]).astype(BF16)
            dhn[...] = jnp.zeros_like(dhn)

        hn_t = hn_ref[...]
        g = _dot(hn_t, wg_ref[...], _NN)
        u = _dot(hn_t, wu_ref[...], _NN)
        da = _dot(dyh_s[...], wd_ref[...], _NT)
        sig = _sigmoid_tanh(g)
        gs = g * sig
        a_ref[...] = (gs * u).astype(BF16)
        dg = ((da * u) * (sig + gs * (1.0 - sig))).astype(BF16)
        du = (da * gs).astype(BF16)
        dg_ref[...] = dg
        du_ref[...] = du
        dhn[...] += _dot(jnp.concatenate([dg, du], axis=1),
                         jnp.concatenate([wg_ref[...], wu_ref[...]], axis=1), _NT)

        @pl.when(j == nf - 1)
        def _():
            _, vjp_n = jax.vjp(_rms, x_ref[...], g_ref[...])
            dx, dgn = vjp_n(dhn[...])
            dx_ref[...] = dy_ref[...] + dx

            @pl.when(i == 0)
            def _():
                dgn_ref[...] = jnp.zeros_like(dgn_ref)

            dgn_ref[...] += dgn

    hshape = jax.ShapeDtypeStruct((S, F), BF16)
    dx, dgn, act, dg, du = pl.pallas_call(
        body,
        grid=(S // tm, nf),
        in_specs=[row, row, row, vec, wg, wu, wd],
        out_specs=[row, vec, hid, hid, hid],
        out_shape=[jax.ShapeDtypeStruct((S, D_MODEL), F32), jax.ShapeDtypeStruct((1, D_MODEL), F32),
                   hshape, hshape, hshape],
        scratch_shapes=[pltpu.VMEM((tm, D_MODEL), BF16), pltpu.VMEM((tm, D_MODEL), F32)],
        compiler_params=_cparams(("arbitrary", "arbitrary")),
        name=tag + "_bwd",
    )(dy, x, hn, g_norm, w_gu, w_gu, w_down)
    d_wdown = matmul(act, dy, "tn", tag + "_dwd", out_dtype=BF16, scale=0.5, tm=1408)
    d_wgu = jnp.concatenate([matmul(hn, dg, "tn", tag + "_dwg", out_dtype=BF16),
                             matmul(hn, du, "tn", tag + "_dwu", out_dtype=BF16)], axis=1)
    return dx, dgn, d_wgu, d_wdown


def ple_fwd(x, p_i, g_norm, w_gate, w_proj, tag):
    hn, = rowwise(_f_rms, [x], [g_norm], [(D_MODEL, BF16)], tag + "_rms")
    z = matmul(hn, w_gate, "nn", tag + "_gate")
    pp = matmul(p_i, w_proj, "nn", tag + "_proj")
    out, = rowwise(_f_ple, [x, z, pp], [], [(D_MODEL, F32)], tag + "_mix")
    return out, (x, hn, z, pp)


def ple_bwd(dy, saved, p_i, g_norm, w_gate, tag):
    x, hn, z, pp = saved
    (dz, dpp), _ = rowwise_vjp(_f_ple, [x, z, pp], [], [dy], tag + "_dmix", need=[False, True, True],
                               row_dtype=BF16)
    d_wproj = matmul(p_i, dpp, "tn", tag + "_dwp", out_dtype=BF16)
    d_wgate = matmul(hn, dz, "tn", tag + "_dwg", out_dtype=BF16)
    dhn = matmul(dz, w_gate, "nt", tag + "_dhn")
    (dx,), (dgn,) = rowwise_vjp(_f_rms_res, [x], [g_norm], [dhn, dy], tag + "_drms")
    return dx, dgn, d_wgate, d_wproj


def _swa_consts(sinks):
    slopes = np.asarray([2.0 ** (-(i + 1)) for i in range(SWA_HEADS)], np.float32)
    slope_col = jnp.asarray(np.repeat(slopes, BLOCK).reshape(SWA_KV_HEADS, SWA_GROUP * BLOCK, 1))
    sink_col = jnp.repeat(sinks.reshape(SWA_HEADS), BLOCK).reshape(SWA_KV_HEADS, SWA_GROUP * BLOCK, 1)
    return sink_col, slope_col


def even_mix_fwd(x, W, gather_src, later_weights):
    S = x.shape[0]
    hn, = rowwise(_f_rms, [x], [W["mix_norm0"]], [(D_MODEL, BF16)], "emix_rms")
    proj = matmul(hn, W["even_w_in"], "nn", "emix_in")
    qa = _heads(proj[:, :SWA_Q], SWA_HEADS).reshape(SWA_KV_HEADS, SWA_GROUP, S, HEAD_DIM)
    ka = _heads(proj[:, SWA_Q:SWA_Q + SWA_KV], SWA_KV_HEADS)
    va = _heads(proj[:, SWA_Q + SWA_KV:SWA_COLS], SWA_KV_HEADS)
    sink_col, slope_col = _swa_consts(W["swa_sinks"])
    ya = swa_fwd(qa, ka, va, sink_col, slope_col)
    ya = _unheads(ya.reshape(SWA_HEADS, S, HEAD_DIM))
    hb = proj[:, SWA_COLS:]
    h, = rowwise(_f_mix, [hb, _shift_down(hb)], [W["rwkv_mu"]], [(hb.shape[1], F32)], "rwkv_shift")
    hr, hk, hv = h[:, :512], h[:, 512:1024], h[:, 1024:1536]
    hw, ha, hg = h[:, 1536:1600], h[:, 1600:1664], h[:, 1664:1792]
    bd = _block_diag_ones()
    pre_params = [W["rwkv_w0"], W["rwkv_w2"], W["rwkv_a0"], W["rwkv_a2"], W["rwkv_g2"], W["rwkv_k_k"],
                  W["rwkv_k_a"]]
    decay, k2, kk, b, g = rowwise(_f_rwkv_pre, [hk, hw, ha, hg], pre_params + [bd],
                                  [(RWKV_DIM, F32)] * 5, "rwkv_pre")
    vT = _to_colblocks(hv)
    y, ckpt, gathered = rwkv_scan_fwd(hr, decay, k2, kk, b, vT, gather_src)
    late = later_weights(gathered)
    post_params = [W["rwkv_ln_w"], W["rwkv_ln_b"], W["rwkv_r_k"]]
    yb, = rowwise(_f_rwkv_post, [y, hr, k2, hv, g], post_params + [bd], [(RWKV_DIM, F32)], "rwkv_post")
    cat = jnp.concatenate([ya, yb], axis=1).astype(BF16)
    out = matmul(cat, late["even_w_out"], "nn", "emix_out", res=x)
    saved = (x, hn, qa, ka, va, sink_col, slope_col, hb, hr, hk, hv, hw, ha, hg, decay, k2, kk, b, g, vT,
             ckpt, y, cat)
    return out, saved, late


def even_mix_bwd(dy, saved, W, scatter_src):
    (x, hn, qa, ka, va, sink_col, slope_col, hb, hr, hk, hv, hw, ha, hg, decay, k2, kk, b, g, vT, ckpt, y,
     cat) = saved
    S = x.shape[0]
    grads = {}
    dcat = matmul(dy, W["even_w_out"], "nt", "emix_dcat")
    grads["even_w_out"] = matmul(cat, dy, "tn", "emix_dwout", out_dtype=BF16)
    dya, dyb = dcat[:, :SWA_Q], dcat[:, SWA_Q:]
    dya_h = _heads(dya, SWA_HEADS).reshape(SWA_KV_HEADS, SWA_GROUP, S, HEAD_DIM)
    dqa, dkp, dkc, dvp, dvc, dsink = swa_bwd(qa, ka, va, sink_col, slope_col, dya_h)
    shift_blk = lambda a: jnp.pad(a[:, BLOCK:], ((0, 0), (0, BLOCK), (0, 0)))
    dka = dkc + shift_blk(dkp)
    dva = dvc + shift_blk(dvp)
    grads["swa_sinks"] = dsink.reshape(SWA_HEADS, BLOCK).sum(axis=1).reshape(1, SWA_HEADS)
    dqa = _unheads(dqa.reshape(SWA_HEADS, S, HEAD_DIM))
    dka, dva = _unheads(dka), _unheads(dva)
    bd = _block_diag_ones()
    post_params = [W["rwkv_ln_w"], W["rwkv_ln_b"], W["rwkv_r_k"]]
    (d_y, d_r1, d_k2a, d_v1, d_g), (d_lnw, d_lnb, d_rk) = rowwise_vjp(
        _f_rwkv_post, [y, hr, k2, hv, g], post_params, [dyb], "rwkv_dpost", consts=[bd], tm=128)
    grads["rwkv_ln_w"], grads["rwkv_ln_b"], grads["rwkv_r_k"] = d_lnw, d_lnb, d_rk
    d_r2, d_w, d_k2b, d_kk, d_b, d_v2, exchanged = rwkv_scan_bwd(hr, decay, k2, kk, b, vT, _to_colblocks(d_y), ckpt,
                                                                  scatter_src)
    pre_params = [W["rwkv_w0"], W["rwkv_w2"], W["rwkv_a0"], W["rwkv_a2"], W["rwkv_g2"], W["rwkv_k_k"],
                  W["rwkv_k_a"]]
    (d_hk, d_hw, d_ha, d_hg), dpre = rowwise_vjp(
        _f_rwkv_pre, [hk, hw, ha, hg], pre_params, [d_w, d_k2a + d_k2b, d_kk, d_b, d_g], "rwkv_dpre",
        consts=[bd], tm=128)
    for nm, gval in zip(["rwkv_w0", "rwkv_w2", "rwkv_a0", "rwkv_a2", "rwkv_g2", "rwkv_k_k", "rwkv_k_a"], dpre):
        grads[nm] = gval
    d_h = jnp.concatenate([d_r1 + d_r2, d_hk, d_v1 + d_v2, d_hw, d_ha, d_hg], axis=1)
    (d_hb, d_sh), (d_mu,) = rowwise_vjp(_f_mix, [hb, _shift_down(hb)], [W["rwkv_mu"]], [d_h], "rwkv_dshift")
    grads["rwkv_mu"] = d_mu
    d_hb = d_hb + _shift_up(d_sh)
    dproj = jnp.concatenate([dqa, dka, dva, d_hb], axis=1).astype(BF16)
    grads["even_w_in"] = matmul(hn, dproj, "tn", "emix_dwin", out_dtype=BF16)
    dhn = matmul(dproj, W["even_w_in"], "nt", "emix_dhn")
    (dx,), (dgn,) = rowwise_vjp(_f_rms_res, [x], [W["mix_norm0"]], [dhn, dy], "emix_drms")
    grads["mix_norm0"] = dgn
    return dx, grads, exchanged


def odd_mix_fwd(x, W):
    S = x.shape[0]
    hn, = rowwise(_f_rms, [x], [W["mix_norm1"]], [(D_MODEL, BF16)], "omix_rms")
    proj = matmul(hn, W["fox_w_in"], "nn", "omix_in")
    q = _heads(proj[:, :FOX_DIM], FOX_HEADS).astype(BF16)
    k = _heads(proj[:, FOX_DIM:2 * FOX_DIM], FOX_HEADS).astype(BF16)
    v = _heads(proj[:, 2 * FOX_DIM:3 * FOX_DIM], FOX_HEADS).astype(BF16)
    fz = proj[:, 3 * FOX_DIM:]
    logf, = rowwise(_f_logf, [fz], [W["fox_b_f"]], [(128, F32)], "fox_logf")
    c = seq_cumsum(logf, False, "fox_cumsum")[:, :FOX_HEADS]
    c_col = c.T.reshape(FOX_HEADS, S, 1)
    c_row = c.T.reshape(FOX_HEADS, 1, S)
    o, lse = fox_fwd(q, k, v, c_col, c_row)
    yc = _unheads(o).astype(BF16)
    out = matmul(yc, W["fox_w_out"], "nn", "omix_out", res=x)
    return out, (x, hn, q, k, v, fz, c_col, c_row, o, lse, yc)


def odd_mix_bwd(dy, saved, W):
    x, hn, q, k, v, fz, c_col, c_row, o, lse, yc = saved
    S = x.shape[0]
    grads = {}
    dyc = matmul(dy, W["fox_w_out"], "nt", "omix_dyc")
    grads["fox_w_out"] = matmul(yc, dy, "tn", "omix_dwout", out_dtype=BF16)
    do = _heads(dyc, FOX_HEADS)
    dq, drow, dk, dv, dcol = fox_bwd(q, k, v, c_col, c_row, o, lse, do)
    dc = (drow.reshape(FOX_HEADS, S) - dcol.reshape(FOX_HEADS, S)).T
    dc = jnp.pad(dc, ((0, 0), (0, 128 - FOX_HEADS)))
    dlogf = seq_cumsum(dc, True, "fox_rcumsum")
    (dfz,), (dbf,) = rowwise_vjp(_f_logf, [fz], [W["fox_b_f"]], [dlogf], "fox_dlogf")
    grads["fox_b_f"] = dbf
    dproj = jnp.concatenate([_unheads(dq), _unheads(dk), _unheads(dv), dfz], axis=1).astype(BF16)
    grads["fox_w_in"] = matmul(hn, dproj, "tn", "omix_dwin", out_dtype=BF16)
    dhn = matmul(dproj, W["fox_w_in"], "nt", "omix_dhn")
    (dx,), (dgn,) = rowwise_vjp(_f_rms_res, [x], [W["mix_norm1"]], [dhn, dy], "omix_drms")
    grads["mix_norm1"] = dgn
    return dx, grads


def device_step(x, p, target, W, gather_src, layer1_weights, layer1_grads):
    W = dict(W)
    saved = []
    h = x
    for i in range(2):
        h, s1 = ffn_fwd(h, W[f"ffn1_norm{i}"], W[f"ffn1_w_gu{i}"], W[f"ffn1_w_down{i}"], f"ffn1_{i}")
        if i == 0:
            h, s2, late = even_mix_fwd(h, W, gather_src, layer1_weights)
            W.update(late)
        else:
            h, s2 = odd_mix_fwd(h, W)
        h, s3 = ffn_fwd(h, W[f"ffn2_norm{i}"], W[f"ffn2_w_gu{i}"], W[f"ffn2_w_down{i}"], f"ffn2_{i}")
        h, s4 = ple_fwd(h, p[i], W[f"ple_norm{i}"], W[f"ple_w_gate{i}"], W[f"ple_w_proj{i}"], f"ple_{i}")
        saved.append((s1, s2, s3, s4))
    dh, d_final, loss = loss_head(h, target, W["final_norm"])
    G = {"final_norm": d_final}
    for i in (1, 0):
        s1, s2, s3, s4 = saved[i]
        dh, G[f"ple_norm{i}"], G[f"ple_w_gate{i}"], G[f"ple_w_proj{i}"] = ple_bwd(
            dh, s4, p[i], W[f"ple_norm{i}"], W[f"ple_w_gate{i}"], f"ple_{i}")
        dh, G[f"ffn2_norm{i}"], G[f"ffn2_w_gu{i}"], G[f"ffn2_w_down{i}"] = ffn_bwd(
            dh, s3, W[f"ffn2_norm{i}"], W[f"ffn2_w_gu{i}"], W[f"ffn2_w_down{i}"], f"ffn2_{i}")
        if i == 0:
            dh, gm, exchanged = even_mix_bwd(dh, s2, W, layer1_grads(G))
        else:
            dh, gm = odd_mix_bwd(dh, s2, W)
        G.update(gm)
        dh, G[f"ffn1_norm{i}"], G[f"ffn1_w_gu{i}"], G[f"ffn1_w_down{i}"] = ffn_bwd(
            dh, s1, W[f"ffn1_norm{i}"], W[f"ffn1_w_gu{i}"], W[f"ffn1_w_down{i}"], f"ffn1_{i}")
    return loss, dh, G, exchanged


_MESH = pl.DeviceIdType.MESH
_ANY = pl.BlockSpec(memory_space=pl.ANY)


def _exchange_sems(n):
    return [pltpu.SemaphoreType.DMA((7 * n,)), pltpu.SemaphoreType.DMA((7 * n,)), pltpu.SemaphoreType.DMA((n,))]


def all_gather(xs, name):
    n = len(xs)

    def body(*refs):
        x_refs, out_refs = refs[:n], refs[n:2 * n]
        send_sems, recv_sems, local_sems = refs[2 * n:]
        x_, y_, c_ = lax.axis_index("x"), lax.axis_index("y"), lax.axis_index("c")
        me, sibling = (x_, y_, c_), (x_, y_, 1 - c_)
        chips = [(1 - x_, y_), (x_, 1 - y_), (1 - x_, 1 - y_)]

        def copy(b, k, block, to, from_input=False):
            slot = out_refs[b].at[4 * block[0] + 2 * block[1] + block[2]]
            return pltpu.make_async_remote_copy(
                src_ref=x_refs[b] if from_input else slot, dst_ref=slot,
                send_sem=send_sems.at[7 * b + k], recv_sem=recv_sems.at[7 * b + k], device_id=to,
                device_id_type=_MESH)

        bufs = range(n)
        mine = [pltpu.make_async_copy(x_refs[b], out_refs[b].at[4 * x_ + 2 * y_ + c_], local_sems.at[b]) for b in bufs]
        first = [copy(b, 0, me, sibling, True) for b in bufs]
        first += [copy(b, 1 + j, me, (*chip, c_), True) for j, chip in enumerate(chips) for b in bufs]
        for cp in mine + first:
            cp.start()
        passed = []
        for j, chip in enumerate(chips):
            for b in bufs:
                copy(b, 1 + j, (*chip, c_), me).wait_recv()
                passed.append(copy(b, 4 + j, (*chip, c_), sibling))
                passed[-1].start()
        for b in bufs:
            copy(b, 0, sibling, me).wait_recv()
            for j, chip in enumerate(chips):
                copy(b, 4 + j, (*chip, 1 - c_), me).wait_recv()
        for cp in first + passed:
            cp.wait_send()
        for cp in mine:
            cp.wait()

    return pl.pallas_call(
        body,
        out_shape=[jax.ShapeDtypeStruct((N_DEV,) + x.shape, x.dtype) for x in xs],
        in_specs=[_ANY] * n,
        out_specs=[_ANY] * n,
        scratch_shapes=_exchange_sems(n),
        name=name,
    )(*xs)


def _direct_exchange(gather, s_refs, r_refs, send_sems, recv_sems, local_sems):
    x_, y_, c_ = lax.axis_index("x"), lax.axis_index("y"), lax.axis_index("c")
    my = 4 * x_ + 2 * y_ + c_
    copies = []
    for b, (s_ref, r_ref) in enumerate(zip(s_refs, r_refs)):
        copies.append(pltpu.make_async_copy(s_ref if gather else s_ref.at[my], r_ref.at[my], local_sems.at[b]))
        for m in range(1, N_DEV):
            px = 1 - x_ if (m >> 2) & 1 else x_
            py = 1 - y_ if (m >> 1) & 1 else y_
            pc = 1 - c_ if m & 1 else c_
            copies.append(pltpu.make_async_remote_copy(
                src_ref=s_ref if gather else s_ref.at[4 * px + 2 * py + pc], dst_ref=r_ref.at[my],
                send_sem=send_sems.at[7 * b + m - 1], recv_sem=recv_sems.at[7 * b + m - 1],
                device_id=(px, py, pc), device_id_type=_MESH))
    return copies


def _exchange_during(step, n_steps, gather, s_refs, r_refs, send_sems, recv_sems, local_sems):
    copies = _direct_exchange(gather, s_refs, r_refs, send_sems, recv_sems, local_sems)

    @pl.when(step == 0)
    def _():
        for cp in copies:
            cp.start()

    @pl.when(step == n_steps - 1)
    def _():
        for cp in copies:
            cp.wait()


def _exchange_out_shapes(gather, srcs):
    return [jax.ShapeDtypeStruct(((N_DEV,) + s.shape) if gather else s.shape, s.dtype) for s in srcs]


def all_to_all(sends, name):
    n = len(sends)

    def body(*refs):
        copies = _direct_exchange(False, refs[:n], refs[n:2 * n], *refs[2 * n:])
        for cp in copies:
            cp.start()
        for cp in copies:
            cp.wait()

    return pl.pallas_call(
        body,
        out_shape=_exchange_out_shapes(False, sends),
        in_specs=[_ANY] * n,
        out_specs=[_ANY] * n,
        scratch_shapes=_exchange_sems(n),
        name=name,
    )(*sends)


def adamw(w, m, v, parts, name, tm=256):
    R, C = w.shape
    tm = _pick(R, tm, 8) if R >= 8 else R

    def body(w_ref, m_ref, v_ref, p_ref, g_ref, d_ref, nm_ref, nv_ref):
        g = p_ref[0].astype(F32)
        for s in range(1, N_DEV):
            g = g + p_ref[s].astype(F32)
        nm = ADAM_B1 * m_ref[...] + (1.0 - ADAM_B1) * g
        nv = ADAM_B2 * v_ref[...] + (1.0 - ADAM_B2) * (g * g)
        m_hat = nm / (1.0 - ADAM_B1 ** ADAM_STEP)
        v_hat = nv / (1.0 - ADAM_B2 ** ADAM_STEP)
        g_ref[...] = g
        d_ref[...] = -ADAM_LR * (m_hat / (jnp.sqrt(v_hat) + ADAM_EPS) + ADAM_WD * w_ref[...])
        nm_ref[...] = nm
        nv_ref[...] = nv

    row = pl.BlockSpec((tm, C), lambda i: (i, 0))
    out = jax.ShapeDtypeStruct((R, C), F32)
    return pl.pallas_call(
        body,
        grid=(R // tm,),
        in_specs=[row, row, row, pl.BlockSpec((N_DEV, tm, C), lambda i: (0, i, 0))],
        out_specs=[row] * 4,
        out_shape=[out] * 4,
        compiler_params=_cparams(("parallel",)),
        name=name,
    )(w, m, v, parts)


_WEIGHTS = ["ffn1_norm", "ffn1_w_gu", "ffn1_w_down", "mix_norm", "ffn2_norm", "ffn2_w_gu", "ffn2_w_down",
            "ple_norm", "ple_w_gate", "ple_w_proj", "even_w_in", "even_w_out", "swa_sinks", "rwkv_mu",
            "rwkv_w0", "rwkv_w2", "rwkv_a0", "rwkv_a2", "rwkv_g2", "rwkv_k_k", "rwkv_k_a", "rwkv_r_k",
            "rwkv_ln_w", "rwkv_ln_b", "fox_w_in", "fox_b_f", "fox_w_out", "final_norm"]
_SHARD_AXIS = {"ffn1_w_gu": 2, "ffn1_w_down": 1, "ffn2_w_gu": 2, "ffn2_w_down": 1, "ple_w_gate": 1,
               "ple_w_proj": 2, "even_w_in": 2, "even_w_out": 1, "rwkv_w2": 2, "rwkv_a2": 2, "rwkv_g2": 2,
               "fox_w_in": 2, "fox_w_out": 1}
_SHARDED = [n for n in _WEIGHTS if n in _SHARD_AXIS]
_REPLICATED = [n for n in _WEIGHTS if n not in _SHARD_AXIS]
_PER_LAYER = ("ffn1_w_gu", "ffn1_w_down", "ffn2_w_gu", "ffn2_w_down", "ple_w_gate", "ple_w_proj")
_ALL_PIECES = ([(n, 0) for n in _PER_LAYER] + [(n, 0) for n in ("even_w_in", "even_w_out", "rwkv_w2", "rwkv_a2", "rwkv_g2")]
               + [(n, 1) for n in _PER_LAYER] + [("fox_w_in", 0), ("fox_w_out", 0)])
_FIRST_WEIGHTS = [(n, 0) for n in ("ffn1_w_gu", "ffn1_w_down", "even_w_in", "rwkv_w2", "rwkv_a2", "rwkv_g2")]
_PIECES = [_FIRST_WEIGHTS, [pc for pc in _ALL_PIECES if pc not in _FIRST_WEIGHTS]]
_LATE_GRADS = _FIRST_WEIGHTS + [("even_w_out", 0)]
_GRAD_PIECES = [_LATE_GRADS, [pc for pc in _ALL_PIECES if pc not in _LATE_GRADS]]
_PACK_LANES = 1024
_PACK_ROW_TILE = 256


def _piece_key(piece):
    name, idx = piece
    return f"{name}{idx}" if name in _PER_LAYER else name


_KINDS = ("gu", "rows", "misc")


def _kind(piece):
    if piece[0] in ("ffn1_w_gu", "ffn2_w_gu"):
        return "gu"
    return "rows" if _SHARD_AXIS[piece[0]] == 1 else "misc"


def _of_kind(pieces, shapes, kind):
    return [(pc, shp) for pc, shp in zip(pieces, shapes) if _kind(pc) == kind]


def _pad_rows(flat, axis):
    pad = [(0, 0)] * flat.ndim
    pad[axis] = (0, -flat.shape[axis] % _PACK_ROW_TILE)
    return jnp.pad(flat, pad)


def _bundle(get, pieces, dtype):
    take = lambda kind: [get(pc).astype(dtype) for pc in pieces if _kind(pc) == kind]
    return [jnp.stack(take("gu")), jnp.concatenate(take("rows"), axis=0),
            _pad_rows(jnp.concatenate([a.reshape(-1, _PACK_LANES) for a in take("misc")], axis=0), 0)]


def _unbundle(bufs, pieces, shapes):
    out = {}
    gu = _of_kind(pieces, shapes, "gu")
    stacked = bufs[0].reshape((len(gu),) + gu[0][1])
    for j, (pc, _) in enumerate(gu):
        out[pc] = stacked[j]
    for buf, kind in ((bufs[1], "rows"), (bufs[2], "misc")):
        r0 = 0
        for pc, shp in _of_kind(pieces, shapes, kind):
            n = math.prod(shp) // _PACK_LANES
            out[pc] = buf[r0:r0 + n].reshape(shp)
            r0 += n
    return out


def _unshard(gathered, pieces, shapes):
    full = {}
    for j, (pc, shp) in enumerate(_of_kind(pieces, shapes, "gu")):
        full[_piece_key(pc)] = jnp.moveaxis(gathered[0][:, j], 0, 1).reshape(shp[0], N_DEV * shp[1])
    r0 = 0
    for pc, shp in _of_kind(pieces, shapes, "rows"):
        full[_piece_key(pc)] = gathered[1][:, r0:r0 + shp[0]].reshape(N_DEV * shp[0], shp[1])
        r0 += shp[0]
    r0 = 0
    for pc, shp in _of_kind(pieces, shapes, "misc"):
        n = math.prod(shp) // _PACK_LANES
        seg = gathered[2][:, r0:r0 + n].reshape((N_DEV,) + shp)
        full[_piece_key(pc)] = jnp.moveaxis(seg, 0, 1).reshape(shp[0], N_DEV * shp[1])
        r0 += n
    return full


def _to_shards(full, pieces, shapes):
    get = lambda pc: full[_piece_key(pc)].astype(BF16)
    cols = lambda pc, shp: jnp.moveaxis(get(pc).reshape(shp[0], N_DEV, shp[1]), 1, 0)
    gu = jnp.stack([cols(pc, shp) for pc, shp in _of_kind(pieces, shapes, "gu")], axis=1)
    rows = jnp.concatenate([get(pc).reshape((N_DEV,) + shp) for pc, shp in _of_kind(pieces, shapes, "rows")], axis=1)
    misc = jnp.concatenate([cols(pc, shp).reshape(N_DEV, -1, _PACK_LANES)
                            for pc, shp in _of_kind(pieces, shapes, "misc")], axis=1)
    return [gu, rows, _pad_rows(misc, 1)]


def _layer_weights(full):
    W = dict(full)
    if "fox_w_in" in W:
        W["fox_w_in"] = jnp.pad(W["fox_w_in"], ((0, 0), (0, FOX_IN_PAD - W["fox_w_in"].shape[1])))
    for n in ("rwkv_w2", "rwkv_a2", "rwkv_g2"):
        if n in W:
            W[n] = W[n].astype(F32)
    return W


def _pack_small(vals):
    flat = jnp.concatenate([v.reshape(1, -1) for v in vals], axis=1)
    n = flat.shape[1]
    return jnp.pad(flat, ((0, 0), (0, -n % 128)))


def _unpack_small(flat, shapes):
    out, c0 = [], 0
    for shp in shapes:
        n = math.prod(shp)
        out.append(flat[0, c0:c0 + n].reshape(shp))
        c0 += n
    return out


def kernel(x, p, ffn1_norm, ffn1_w_gu, ffn1_w_down, mix_norm, ffn2_norm, ffn2_w_gu, ffn2_w_down, ple_norm, ple_w_gate, ple_w_proj, even_w_in, even_w_out, swa_sinks, rwkv_mu, rwkv_w0, rwkv_w2, rwkv_a0, rwkv_a2, rwkv_g2, rwkv_k_k, rwkv_k_a, rwkv_r_k, rwkv_ln_w, rwkv_ln_b, fox_w_in, fox_b_f, fox_w_out, final_norm, loss_target, m_ffn1_norm, m_ffn1_w_gu, m_ffn1_w_down, m_mix_norm, m_ffn2_norm, m_ffn2_w_gu, m_ffn2_w_down, m_ple_norm, m_ple_w_gate, m_ple_w_proj, m_even_w_in, m_even_w_out, m_swa_sinks, m_rwkv_mu, m_rwkv_w0, m_rwkv_w2, m_rwkv_a0, m_rwkv_a2, m_rwkv_g2, m_rwkv_k_k, m_rwkv_k_a, m_rwkv_r_k, m_rwkv_ln_w, m_rwkv_ln_b, m_fox_w_in, m_fox_b_f, m_fox_w_out, m_final_norm, v_ffn1_norm, v_ffn1_w_gu, v_ffn1_w_down, v_mix_norm, v_ffn2_norm, v_ffn2_w_gu, v_ffn2_w_down, v_ple_norm, v_ple_w_gate, v_ple_w_proj, v_even_w_in, v_even_w_out, v_swa_sinks, v_rwkv_mu, v_rwkv_w0, v_rwkv_w2, v_rwkv_a0, v_rwkv_a2, v_rwkv_g2, v_rwkv_k_k, v_rwkv_k_a, v_rwkv_r_k, v_rwkv_ln_w, v_rwkv_ln_b, v_fox_w_in, v_fox_b_f, v_fox_w_out, v_final_norm):
    given = dict(locals())
    w = {n: given[n] for n in _WEIGHTS}
    m = {n: given["m_" + n] for n in _WEIGHTS}
    v = {n: given["v_" + n] for n in _WEIGHTS}
    small_shapes = [w[n].shape for n in _REPLICATED]
    piece = lambda d, pc: d[pc[0]][pc[1]]
    shapes = [[piece(w, pc).shape for pc in pieces] for pieces in _PIECES]
    gshapes = [[piece(w, pc).shape for pc in pieces] for pieces in _GRAD_PIECES]
    w_send = [_bundle(lambda pc: piece(w, pc), pieces, BF16) for pieces in _PIECES]

    W = _layer_weights(_unshard(all_gather(w_send[0], "weights_all_gather"), _PIECES[0], shapes[0]))
    for i in range(2):
        for n in ("ffn1_norm", "mix_norm", "ffn2_norm", "ple_norm"):
            W[f"{n}{i}"] = w[n][i].reshape(1, -1)
    for n in ("swa_sinks", "rwkv_mu", "rwkv_w0", "rwkv_a0", "rwkv_k_k", "rwkv_k_a", "rwkv_r_k", "rwkv_ln_w",
              "rwkv_ln_b", "final_norm"):
        W[n] = w[n].reshape(1, -1)
    n_f = fox_b_f.shape[1]
    W["fox_b_f"] = jnp.pad(fox_b_f.reshape(1, n_f), ((0, 0), (0, 128 - n_f)))
    n_fox = fox_w_in.shape[2] * N_DEV

    def layer1_weights(gathered):
        return _layer_weights(_unshard(gathered, _PIECES[1], shapes[1]))

    def early_grads(G):
        G = dict(G, fox_w_in=G["fox_w_in"][:, :n_fox])
        return _to_shards(G, _GRAD_PIECES[1], gshapes[1])

    loss_row, dx, G, parts_early = device_step(x[0], p[:, 0], loss_target[0], W, w_send[1], layer1_weights,
                                               early_grads)

    parts = [all_to_all(_to_shards(G, _GRAD_PIECES[0], gshapes[0]), "grads_all_to_all"), parts_early]
    out_g, out_d, out_m, out_v = {}, {}, {}, {}
    rows2d = lambda a, lead: a.reshape(a.shape[:lead] + (-1, a.shape[-1]))
    for li, pieces in enumerate(_GRAD_PIECES):
        wmv = [_bundle(lambda pc, d=d: piece(d, pc), pieces, F32) for d in (w, m, v)]
        res = [adamw(*[rows2d(b[ki], 0) for b in wmv], rows2d(parts[li][ki], 1), f"adamw_{kind}{li}")
               for ki, kind in enumerate(_KINDS)]
        for oi, out in enumerate((out_g, out_d, out_m, out_v)):
            for pc, a in _unbundle([r[oi] for r in res], pieces, gshapes[li]).items():
                out.setdefault(pc[0], {})[pc[1]] = a
    for out in (out_g, out_d, out_m, out_v):
        for n in _SHARDED:
            out[n] = jnp.stack([out[n][i] for i in sorted(out[n])])

    gsmall = {}
    for n in ("ffn1_norm", "mix_norm", "ffn2_norm", "ple_norm"):
        gsmall[n] = jnp.concatenate([G[f"{n}0"], G[f"{n}1"]], axis=0)
    for n in ("swa_sinks", "rwkv_mu", "rwkv_w0", "rwkv_a0", "rwkv_k_k", "rwkv_k_a", "rwkv_r_k", "rwkv_ln_w",
              "rwkv_ln_b", "final_norm"):
        gsmall[n] = G[n]
    gsmall["fox_b_f"] = G["fox_b_f"][:, :n_f]
    small = _pack_small([gsmall[n] for n in _REPLICATED] + [loss_row[:, :1]])
    small_parts = all_gather([small], "small_all_gather")[0]
    pad1 = lambda vals: _pack_small(vals + [jnp.zeros((1, 1), F32)])
    gs, ds, nms, nvs = adamw(pad1([w[n] for n in _REPLICATED]), pad1([m[n] for n in _REPLICATED]),
                             pad1([v[n] for n in _REPLICATED]), small_parts, "adamw_replicated")
    out_g.update(zip(_REPLICATED, _unpack_small(gs, small_shapes)))
    out_d.update(zip(_REPLICATED, _unpack_small(ds, small_shapes)))
    out_m.update(zip(_REPLICATED, _unpack_small(nms, small_shapes)))
    out_v.update(zip(_REPLICATED, _unpack_small(nvs, small_shapes)))
    n_small = sum(math.prod(s) for s in small_shapes)
    loss = gs[0, n_small]

    return (loss, dx[None], *[out_g[n] for n in _WEIGHTS], *[out_d[n] for n in _WEIGHTS],
            *[out_m[n] for n in _WEIGHTS], *[out_v[n] for n in _WEIGHTS])
```

```python
import functools
import math

import numpy as np
import jax
import jax.numpy as jnp
from jax import lax
from jax.experimental import pallas as pl
from jax.experimental.pallas import tpu as pltpu

F32 = jnp.float32
BF16 = jnp.bfloat16

D_MODEL = 1024
HEAD_DIM = 64
BLOCK = 128
SWA_HEADS = 8
SWA_KV_HEADS = 2
SWA_GROUP = 4
RWKV_HEADS = 8
RWKV_DIM = 512
FOX_HEADS = 16
FOX_DIM = 1024
D_FF = 2816
NORM_EPS = 1e-6
GN_EPS = 64e-5
L2_EPS = 1e-12
SWA_Q = 512
SWA_KV = 128
SWA_COLS = 768
FOX_IN_PAD = 3200
N_DEV = 8
ADAM_LR = 0.001
ADAM_B1 = 0.9
ADAM_B2 = 0.999
ADAM_EPS = 1e-08
ADAM_WD = 0.01
ADAM_STEP = 10

V7X_VMEM_LIMIT = 56 * 1024 * 1024
FOX_TILE = 512
CUMSUM_BLOCK = 256
SCAN_GROUP = 8
SCAN_CHUNK = 32

_NN = (((1,), (0,)), ((), ()))
_NT = (((1,), (1,)), ((), ()))
_TN = (((0,), (0,)), ((), ()))
_DIMS = {"nn": _NN, "nt": _NT, "tn": _TN}


def _pick(n, target, mult=128):
    best = None
    for t in range(mult, min(n, target) + 1, mult):
        if n % t == 0:
            best = t
    return best or n


def _cparams(sem):
    return pltpu.CompilerParams(dimension_semantics=sem, vmem_limit_bytes=V7X_VMEM_LIMIT)


def _dot(a, b, dims):
    return lax.dot_general(a.astype(BF16), b.astype(BF16), dims, preferred_element_type=F32)


@jax.custom_vjp
def bdot(a, b):
    return _dot(a, b, _NN)


def _bdot_fwd(a, b):
    return _dot(a, b, _NN), (a, b)


def _bdot_bwd(res, g):
    a, b = res
    return _dot(g, b, _NT), _dot(a, g, _TN)


bdot.defvjp(_bdot_fwd, _bdot_bwd)


@jax.custom_vjp
def bdot_nt(a, b):
    return _dot(a, b, _NT)


def _bdot_nt_fwd(a, b):
    return _dot(a, b, _NT), (a, b)


def _bdot_nt_bwd(res, g):
    a, b = res
    return _dot(g, b, _NN), _dot(g, a, _TN)


bdot_nt.defvjp(_bdot_nt_fwd, _bdot_nt_bwd)


@jax.custom_vjp
def _segsum(x, bd):
    return _dot2(x, bd.astype(BF16))


def _segsum_fwd(x, bd):
    return _segsum(x, bd), bd


def _segsum_bwd(bd, g):
    return _dot2(g, bd.astype(BF16)), jnp.zeros_like(bd)


_segsum.defvjp(_segsum_fwd, _segsum_bwd)


def _sigmoid(x):
    return 1.0 / (1.0 + jnp.exp(-x))


def _sigmoid_tanh(x):
    return 0.5 * jnp.tanh(0.5 * x) + 0.5


def _softplus(x):
    return jnp.maximum(x, 0.0) + jnp.log(1.0 + jnp.exp(-jnp.abs(x)))


def matmul(a, b, mode, name, out_dtype=F32, scale=1.0, res=None, tm=1024, tn=1408, tk=1024):
    if mode == "nn":
        (M, K), (K2, N) = a.shape, b.shape
    elif mode == "nt":
        (M, K), (N, K2) = a.shape, b.shape
    else:
        (K, M), (K2, N) = a.shape, b.shape
    assert K == K2, (a.shape, b.shape, mode)
    tm, tn, tk = _pick(M, tm), _pick(N, tn), _pick(K, tk)
    nk = K // tk
    has_res = res is not None

    def body(*refs):
        if has_res:
            a_ref, b_ref, r_ref, o_ref, acc = refs
        else:
            a_ref, b_ref, o_ref, acc = refs
        kk = pl.program_id(2)

        @pl.when(kk == 0)
        def _():
            acc[...] = jnp.zeros_like(acc)

        acc[...] += _dot(a_ref[...], b_ref[...], _DIMS[mode])

        @pl.when(kk == nk - 1)
        def _():
            v = acc[...]
            if scale != 1.0:
                v = v * scale
            if has_res:
                v = v + r_ref[...].astype(F32)
            o_ref[...] = v.astype(out_dtype)

    if mode == "tn":
        a_spec = pl.BlockSpec((tk, tm), lambda i, j, k: (k, i))
    else:
        a_spec = pl.BlockSpec((tm, tk), lambda i, j, k: (i, k))
    if mode == "nt":
        b_spec = pl.BlockSpec((tn, tk), lambda i, j, k: (j, k))
    else:
        b_spec = pl.BlockSpec((tk, tn), lambda i, j, k: (k, j))
    o_spec = pl.BlockSpec((tm, tn), lambda i, j, k: (i, j))
    in_specs = [a_spec, b_spec] + ([o_spec] if has_res else [])
    args = (a, b) + ((res,) if has_res else ())
    return pl.pallas_call(
        body,
        grid=(M // tm, N // tn, nk),
        in_specs=in_specs,
        out_specs=o_spec,
        out_shape=jax.ShapeDtypeStruct((M, N), out_dtype),
        scratch_shapes=[pltpu.VMEM((tm, tn), F32)],
        compiler_params=_cparams(("parallel", "parallel", "arbitrary")),
        name=name,
    )(*args)


def _row_spec(r, tm):
    if isinstance(r, tuple):
        arr, width, blk = r
        return arr, pl.BlockSpec((tm, width), lambda i, blk=blk: (i, blk))
    return r, pl.BlockSpec((tm, r.shape[1]), lambda i: (i, 0))


def _whole_spec(p):
    return pl.BlockSpec(p.shape, lambda i: (0,) * p.ndim)


def rowwise(fn, rows, params, outs, name, tm=256):
    arrs, specs = zip(*[_row_spec(r, tm) for r in rows])
    S = arrs[0].shape[0]
    tm = min(tm, S)
    arrs, specs = zip(*[_row_spec(r, tm) for r in rows])
    n_in = len(rows) + len(params)

    def body(*refs):
        res = fn(*[r[...] for r in refs[:n_in]])
        for o_ref, v in zip(refs[n_in:], res):
            o_ref[...] = v.astype(o_ref.dtype)

    return pl.pallas_call(
        body,
        grid=(S // tm,),
        in_specs=list(specs) + [_whole_spec(p) for p in params],
        out_specs=[pl.BlockSpec((tm, c), lambda i: (i, 0)) for c, _ in outs],
        out_shape=[jax.ShapeDtypeStruct((S, c), dt) for c, dt in outs],
        compiler_params=_cparams(("parallel",)),
        name=name,
    )(*arrs, *params)


def rowwise_vjp(fn, rows, params, cots, name, need=None, row_dtype=F32, consts=(), tm=256):
    nr, npar, nc, nk = len(rows), len(params), len(cots), len(consts)
    need = [True] * nr if need is None else need
    arrs, _ = zip(*[_row_spec(r, tm) for r in rows])
    S = arrs[0].shape[0]
    tm = min(tm, S)
    arrs, specs = zip(*[_row_spec(r, tm) for r in rows])
    carrs, cspecs = zip(*[_row_spec(c, tm) for c in cots])
    widths = [s.block_shape[1] for s in specs]
    n_in = nr + npar + nk + nc

    def body(*refs):
        i = pl.program_id(0)
        xs = [r[...].astype(F32) for r in refs[:nr]]
        ps = [r[...] for r in refs[nr:nr + npar]]
        ks = [r[...] for r in refs[nr + npar:nr + npar + nk]]
        cs = [r[...].astype(F32) for r in refs[nr + npar + nk:n_in]]
        outs, vjp = jax.vjp(lambda *a: fn(*a, *ks), *xs, *ps)
        grads = vjp(tuple(cs))
        o = n_in
        for j in range(nr):
            if need[j]:
                refs[o][...] = grads[j].astype(refs[o].dtype)
                o += 1
        for j in range(npar):
            g_ref = refs[o + j]

            @pl.when(i == 0)
            def _(g_ref=g_ref):
                g_ref[...] = jnp.zeros_like(g_ref)

            g_ref[...] += grads[nr + j]

    out_specs = [pl.BlockSpec((tm, w), lambda i: (i, 0)) for w, nd in zip(widths, need) if nd]
    out_shape = [jax.ShapeDtypeStruct((S, w), row_dtype) for w, nd in zip(widths, need) if nd]
    out_specs += [_whole_spec(p) for p in params]
    out_shape += [jax.ShapeDtypeStruct(p.shape, F32) for p in params]
    res = pl.pallas_call(
        body,
        grid=(S // tm,),
        in_specs=list(specs) + [_whole_spec(p) for p in params] + [_whole_spec(k) for k in consts] + list(cspecs),
        out_specs=out_specs,
        out_shape=out_shape,
        compiler_params=_cparams(("arbitrary",)),
        name=name,
    )(*arrs, *params, *consts, *carrs)
    nrow = sum(need)
    return list(res[:nrow]), list(res[nrow:])


def _rms(x, g):
    return x * lax.rsqrt(jnp.mean(x * x, axis=-1, keepdims=True) + NORM_EPS) * g


def _f_rms(x, g):
    return (_rms(x, g),)


def _f_rms_res(x, g):
    return _rms(x, g), x


def _f_ple(x, z, pp):
    return (x + _sigmoid(z) * pp,)


def _f_mix(h, sh, mu):
    return (h + (sh - h) * mu,)


def _f_logf(fz, bf):
    return (-_softplus(-(fz + bf)),)


def _f_rwkv_pre(hk, hw, ha, hg, w0, w2, a0, a2, g2, k_k, k_a, bd):
    wlog = -_softplus(-(w0 + bdot(jnp.tanh(hw), w2))) - 0.5
    a = _sigmoid(a0 + bdot(ha, a2))
    g = bdot(_sigmoid(hg), g2)
    kk = hk * k_k
    kk = kk / jnp.maximum(jnp.sqrt(_segsum(kk * kk, bd)), L2_EPS)
    k2 = hk * (1.0 + (a - 1.0) * k_a)
    decay = jnp.exp(-jnp.exp(wlog))
    return decay, k2, kk, kk * a, g


def _f_rwkv_post(y, r, k2, v, g, ln_w, ln_b, r_k, bd):
    mean = _segsum(y, bd) * (1.0 / HEAD_DIM)
    d = y - mean
    var = _segsum(d * d, bd) * (1.0 / HEAD_DIM)
    yn = d * lax.rsqrt(var + GN_EPS) * ln_w + ln_b
    yn = yn + _segsum(r * k2 * r_k, bd) * v
    return (yn * g,)


def loss_head(x, target, gf, tm=256):
    S, D = x.shape
    tm = min(tm, S)

    def f(xt, g, tt):
        err = _rms(xt, g) - tt
        return 0.5 * jnp.sum(err * err) * (1.0 / D)

    def body(x_ref, t_ref, g_ref, dx_ref, dg_ref, l_ref):
        i = pl.program_id(0)
        val, (dx, dg) = jax.value_and_grad(f, argnums=(0, 1))(x_ref[...], g_ref[...], t_ref[...])

        @pl.when(i == 0)
        def _():
            dg_ref[...] = jnp.zeros_like(dg_ref)
            l_ref[...] = jnp.zeros_like(l_ref)

        dx_ref[...] = dx
        dg_ref[...] += dg
        l_ref[...] += jnp.full(l_ref.shape, val, F32)

    row = pl.BlockSpec((tm, D), lambda i: (i, 0))
    vec = pl.BlockSpec((1, D), lambda i: (0, 0))
    return pl.pallas_call(
        body,
        grid=(S // tm,),
        in_specs=[row, row, vec],
        out_specs=[row, vec, pl.BlockSpec((1, 128), lambda i: (0, 0))],
        out_shape=[jax.ShapeDtypeStruct((S, D), F32), jax.ShapeDtypeStruct((1, D), F32),
                   jax.ShapeDtypeStruct((1, 128), F32)],
        compiler_params=_cparams(("arbitrary",)),
        name="loss_head",
    )(x, target, gf)


def _swa_block(q, kp, kc, vp, vc, sink, slope, n):
    k = jnp.concatenate([kp, kc], axis=0)
    v = jnp.concatenate([vp, vc], axis=0)
    rows = q.shape[0]
    logits = bdot_nt(q, k) * (HEAD_DIM ** -0.5)
    qi = lax.broadcasted_iota(jnp.int32, (rows, 2 * BLOCK), 0) & (BLOCK - 1)
    ki = lax.broadcasted_iota(jnp.int32, (rows, 2 * BLOCK), 1)
    dist = qi + BLOCK - ki
    valid = (dist >= 0) & (dist < BLOCK) & ((n - 1) * BLOCK + ki >= 0)
    logits = logits - slope * dist.astype(F32)
    logits = jnp.where(valid, logits, -jnp.inf)
    m = jnp.maximum(jnp.max(logits, axis=-1, keepdims=True), sink)
    pr = jnp.exp(logits - m)
    denom = jnp.sum(pr, axis=-1, keepdims=True) + jnp.exp(sink - m)
    return bdot(pr / denom, v)


def _swa_specs(S):
    nb = S // BLOCK
    q_spec = pl.BlockSpec((None, SWA_GROUP, BLOCK, HEAD_DIM), lambda h, n: (h, 0, n, 0))
    kc_spec = pl.BlockSpec((None, BLOCK, HEAD_DIM), lambda h, n: (h, n, 0))
    kp_spec = pl.BlockSpec((None, BLOCK, HEAD_DIM), lambda h, n: (h, jnp.maximum(n - 1, 0), 0))
    col_spec = pl.BlockSpec((None, SWA_GROUP * BLOCK, 1), lambda h, n: (h, 0, 0))
    return nb, q_spec, kp_spec, kc_spec, col_spec


def swa_fwd(q, k, v, sink_col, slope_col):
    S = q.shape[2]
    nb, q_spec, kp_spec, kc_spec, col_spec = _swa_specs(S)

    def body(q_ref, kp_ref, kc_ref, vp_ref, vc_ref, s_ref, a_ref, o_ref):
        n = pl.program_id(1)
        qq = q_ref[...].reshape(SWA_GROUP * BLOCK, HEAD_DIM)
        out = _swa_block(qq, kp_ref[...], kc_ref[...], vp_ref[...], vc_ref[...], s_ref[...], a_ref[...], n)
        o_ref[...] = out.reshape(SWA_GROUP, BLOCK, HEAD_DIM)

    return pl.pallas_call(
        body,
        grid=(SWA_KV_HEADS, nb),
        in_specs=[q_spec, kp_spec, kc_spec, kp_spec, kc_spec, col_spec, col_spec],
        out_specs=q_spec,
        out_shape=jax.ShapeDtypeStruct(q.shape, F32),
        compiler_params=_cparams(("parallel", "parallel")),
        name="swa_fwd",
    )(q, k, k, v, v, sink_col, slope_col)


def swa_bwd(q, k, v, sink_col, slope_col, dout):
    S = q.shape[2]
    nb, q_spec, kp_spec, kc_spec, col_spec = _swa_specs(S)

    def body(q_ref, kp_ref, kc_ref, vp_ref, vc_ref, s_ref, a_ref, do_ref,
             dq_ref, dkp_ref, dkc_ref, dvp_ref, dvc_ref, ds_ref):
        n = pl.program_id(1)
        qq = q_ref[...].reshape(SWA_GROUP * BLOCK, HEAD_DIM)
        slope = a_ref[...]
        f = lambda a, b, c, d, e, s: _swa_block(a, b, c, d, e, s, slope, n)
        _, vjp = jax.vjp(f, qq, kp_ref[...], kc_ref[...], vp_ref[...], vc_ref[...], s_ref[...])
        dq, dkp, dkc, dvp, dvc, ds = vjp(do_ref[...].reshape(SWA_GROUP * BLOCK, HEAD_DIM))
        dq_ref[...] = dq.reshape(SWA_GROUP, BLOCK, HEAD_DIM)
        dkp_ref[...] = dkp
        dkc_ref[...] = dkc
        dvp_ref[...] = dvp
        dvc_ref[...] = dvc

        @pl.when(n == 0)
        def _():
            ds_ref[...] = jnp.zeros_like(ds_ref)

        ds_ref[...] += ds

    kv_shape = jax.ShapeDtypeStruct(k.shape, F32)
    return pl.pallas_call(
        body,
        grid=(SWA_KV_HEADS, nb),
        in_specs=[q_spec, kp_spec, kc_spec, kp_spec, kc_spec, col_spec, col_spec, q_spec],
        out_specs=[q_spec, kc_spec, kc_spec, kc_spec, kc_spec, col_spec],
        out_shape=[jax.ShapeDtypeStruct(q.shape, F32), kv_shape, kv_shape, kv_shape, kv_shape,
                   jax.ShapeDtypeStruct(sink_col.shape, F32)],
        compiler_params=_cparams(("parallel", "arbitrary")),
        name="swa_bwd",
    )(q, k, k, v, v, sink_col, slope_col, dout)


def _split2(x):
    hi = x.astype(BF16)
    return (x - hi.astype(F32)).astype(BF16), hi


def _dot2_many(xs, m, single=False):
    rows = xs[0].shape[0]
    if single:
        res = jnp.dot(jnp.concatenate([x.astype(BF16) for x in xs], axis=0), m, preferred_element_type=F32)
        return [res[i * rows:(i + 1) * rows] for i in range(len(xs))]
    res = jnp.dot(jnp.concatenate([p for x in xs for p in _split2(x)], axis=0), m, preferred_element_type=F32)
    return [res[(2 * i) * rows:(2 * i + 1) * rows] + res[(2 * i + 1) * rows:(2 * i + 2) * rows]
            for i in range(len(xs))]


def _dot2(x, m):
    return _dot2_many([x], m)[0]


def _seg_sums(xs, bd, single=False):
    w = bd.shape[0]
    halves = _dot2_many([x[:, i:i + w] for x in xs for i in range(0, x.shape[1], w)], bd, single)
    n = xs[0].shape[1] // w
    return [jnp.concatenate(halves[i * n:(i + 1) * n], axis=1) for i in range(len(xs))]


def _seg_sum(x, bd):
    return _seg_sums([x], bd)[0]


def _scan_consts():
    r = np.arange(256)
    bd = (r[:, None] // HEAD_DIM == r[None, :] // HEAD_DIM).astype(np.float32)
    c = np.arange(RWKV_DIM)
    e = (np.arange(HEAD_DIM)[:, None] // SCAN_GROUP == c[None, :] // HEAD_DIM).astype(np.float32)
    diag = (np.arange(HEAD_DIM)[:, None] == c[None, :] % HEAD_DIM).astype(np.float32)
    return jnp.asarray(bd, BF16), jnp.asarray(e, BF16), jnp.asarray(diag, F32)


def _to_colblocks(a):
    S = a.shape[0]
    a = a.reshape(S // SCAN_GROUP, SCAN_GROUP, RWKV_HEADS, HEAD_DIM)
    return a.transpose(0, 3, 2, 1).reshape(S // SCAN_GROUP, HEAD_DIM, RWKV_HEADS * SCAN_GROUP)


def _roll_up(rows):
    return pltpu.roll(rows, rows.shape[0] - 1, 0)


def _scan_pair_rows(aux, base, kk_ref, w_ref, b_ref, k_ref, bd):
    G = SCAN_GROUP
    kk_nx = _roll_up(kk_ref[pl.ds(base, G), :])
    aux[0] = w_ref[pl.ds(base, G), :] * kk_nx
    aux[1], aux[2] = _seg_sums([b_ref[pl.ds(base, G), :] * kk_nx, k_ref[pl.ds(base, G), :] * kk_nx], bd)


def _scan_pair(St, t0, base, col_g, lane_t, aux, kk_ref, w_ref, b_ref, k_ref, bd, e):
    t1 = t0 + 1
    row = lambda ref, t: ref[pl.ds(base + t, 1), :]
    arow = lambda i: aux[i, pl.ds(t0, 1), :]
    u0, m1 = _seg_sums([St * row(kk_ref, t0), St * arow(0)], bd)
    v0, v1 = _dot2_many([jnp.where(lane_t == t0, col_g, 0.0), jnp.where(lane_t == t1, col_g, 0.0)], e)
    u1 = m1 - u0 * arow(1) + v0 * arow(2)
    S0 = St * row(w_ref, t0) - u0 * row(b_ref, t0) + v0 * row(k_ref, t0)
    S1 = S0 * row(w_ref, t1) - u1 * row(b_ref, t1) + v1 * row(k_ref, t1)
    return (S0, S1), (u0, u1), (v0, v1)


def rwkv_scan_fwd(r, w, k, kk, b, vB, gather_srcs):
    S, C = r.shape
    N, G = HEAD_DIM, SCAN_GROUP
    chunk = min(SCAN_CHUNK, S)
    nchunk, ng = S // chunk, chunk // G
    bd, e, diag = _scan_consts()

    nx = len(gather_srcs)

    def body(*refs):
        r_ref, w_ref, k_ref, kk_ref, b_ref, vB_ref, bd_ref, e_ref, dg_ref = refs[:9]
        y_ref, ck_ref = refs[9 + nx:11 + nx]
        S_ref, aux, send_sems, recv_sems, local_sems = refs[11 + 2 * nx:]
        c = pl.program_id(0)
        _exchange_during(c, nchunk, True, refs[9:9 + nx], refs[11 + nx:11 + 2 * nx], send_sems, recv_sems, local_sems)

        @pl.when(c == 0)
        def _():
            S_ref[...] = jnp.zeros_like(S_ref)

        ck_ref[...] = S_ref[...]
        sub = lax.broadcasted_iota(jnp.int32, (G, C), 0)
        lane_t = lax.broadcasted_iota(jnp.int32, (N, N), 1) & (G - 1)

        def group(g, St):
            base = pl.multiple_of(g * G, G)
            vb = vB_ref[g]
            _scan_pair_rows(aux, base, kk_ref, w_ref, b_ref, k_ref, bd_ref[...])
            ys = jnp.zeros((G, C), F32)
            def emit(ys, states, t0):
                steps = (t0, t0 + 1)
                y_bs = _seg_sums([S_t * r_ref[pl.ds(base + tt, 1), :] for S_t, tt in zip(states, steps)], bd_ref[...],
                                 single=True)
                for y_b, tt in zip(y_bs, steps):
                    ys = jnp.where(sub == tt, jnp.sum(y_b * dg_ref[...], axis=0, keepdims=True), ys)
                return ys

            pending = None
            for t0 in range(0, G, 2):
                states, _, _ = _scan_pair(St, t0, base, vb, lane_t, aux, kk_ref, w_ref, b_ref, k_ref, bd_ref[...],
                                          e_ref[...])
                if pending is not None:
                    ys = emit(ys, *pending)
                pending = (states, t0)
                St = states[1]
            y_ref[pl.ds(base, G), :] = emit(ys, *pending)
            return St

        S_ref[...] = lax.fori_loop(0, ng, group, S_ref[...])

    row = pl.BlockSpec((chunk, C), lambda c: (c, 0))
    col = pl.BlockSpec((ng, N, N), lambda c: (c, 0, 0))
    res = pl.pallas_call(
        body,
        grid=(nchunk,),
        in_specs=[row] * 5 + [col, _whole_spec(bd), _whole_spec(e), _whole_spec(diag)] + [_ANY] * nx,
        out_specs=[row, pl.BlockSpec((None, N, C), lambda c: (c, 0, 0))] + [_ANY] * nx,
        out_shape=[jax.ShapeDtypeStruct((S, C), F32), jax.ShapeDtypeStruct((nchunk, N, C), F32)]
        + _exchange_out_shapes(True, gather_srcs),
        scratch_shapes=[pltpu.VMEM((N, C), F32), pltpu.VMEM((3, G, C), F32)] + _exchange_sems(nx),
        compiler_params=_cparams(("arbitrary",)),
        name="rwkv_scan_fwd",
    )(r, w, k, kk, b, vB, bd, e, diag, *gather_srcs)
    return res[0], res[1], list(res[2:])


def rwkv_scan_bwd(r, w, k, kk, b, vB, dyB, ckpt, scatter_srcs):
    S, C = r.shape
    N, G = HEAD_DIM, SCAN_GROUP
    chunk = min(SCAN_CHUNK, S)
    nchunk, ng = S // chunk, chunk // G
    nsteps = nchunk + 1
    bd, e, diag = _scan_consts()
    nx = len(scatter_srcs)

    def body(*refs):
        wf_ref, kf_ref, kkf_ref, bf_ref, vBf_ref, ck_ref = refs[:6]
        r_ref, w_ref, k_ref, kk_ref, b_ref, dyB_ref, bd_ref, e_ref, dg_ref = refs[6:15]
        dr_ref, dw_ref, dk_ref, dkk_ref, db_ref, dv_ref = refs[15 + nx:21 + nx]
        G_ref, sbuf, ubuf, vbuf, aux_f, aux_b, send_sems, recv_sems, local_sems = refs[21 + 2 * nx:]
        c = pl.program_id(0)
        _exchange_during(c, nsteps, False, refs[15:15 + nx], refs[21 + nx:21 + 2 * nx], send_sems, recv_sems,
                         local_sems)

        @pl.when(c == 0)
        def _():
            G_ref[...] = jnp.zeros_like(G_ref)
            sbuf[...] = jnp.zeros_like(sbuf)
            ubuf[...] = jnp.zeros_like(ubuf)
            vbuf[...] = jnp.zeros_like(vbuf)

        sf = c % 2
        sb = 1 - sf
        lane_t = lax.broadcasted_iota(jnp.int32, (N, N), 1) & (G - 1)
        sub = lax.broadcasted_iota(jnp.int32, (G, C), 0)
        colsum = lambda a: jnp.sum(a, axis=0, keepdims=True)

        def group(g, carry):
            St, Gt = carry
            base_f = pl.multiple_of(g * G, G)
            gb = ng - 1 - g
            base_b = pl.multiple_of(gb * G, G)
            vb, dyb = vBf_ref[g], dyB_ref[gb]
            row = lambda ref, t: ref[pl.ds(base_b + t, 1), :]
            _scan_pair_rows(aux_f, base_f, kkf_ref, wf_ref, bf_ref, kf_ref, bd_ref[...])
            b8 = b_ref[pl.ds(base_b, G), :]
            aux_b[0] = _roll_up(w_ref[pl.ds(base_b, G), :]) * b8
            aux_b[1], aux_b[2] = _seg_sums([_roll_up(kk_ref[pl.ds(base_b, G), :]) * b8,
                                            r_ref[pl.ds(base_b, G), :] * b8], bd_ref[...])
            rows = [jnp.zeros((G, C), F32) for _ in range(6)]

            def emit(rows, steps):
                d_vs = _seg_sums([Gt_ * row(k_ref, tt) for tt, Gt_, _, _ in steps], bd_ref[...], single=True)
                for (tt, Gt_, du_b, dy_b), d_vb in zip(steps, d_vs):
                    Sp, Sc = sbuf[sb, base_b + tt], sbuf[sb, base_b + tt + 1]
                    new = (colsum(Sc * dy_b), colsum(Gt_ * Sp), colsum(Gt_ * vbuf[sb, base_b + tt]),
                           colsum(Sp * du_b), -colsum(Gt_ * ubuf[sb, base_b + tt]), colsum(d_vb * dg_ref[...]))
                    rows = [jnp.where(sub == tt, n_, acc) for n_, acc in zip(new, rows)]
                return rows

            pending = None
            for i in range(G // 2):
                t0 = 2 * i
                states, us, vs = _scan_pair(St, t0, base_f, vb, lane_t, aux_f, kkf_ref, wf_ref, bf_ref, kf_ref,
                                            bd_ref[...], e_ref[...])
                for j, S_before in enumerate((St, states[0])):
                    sbuf[sf, base_f + t0 + j] = S_before
                    ubuf[sf, base_f + t0 + j] = us[j]
                    vbuf[sf, base_f + t0 + j] = vs[j]
                St = states[1]
                t0 = G - 2 - 2 * i
                t1 = t0 + 1
                arow = lambda j, t0=t0: aux_b[j, pl.ds(t0, 1), :]
                dy1, dy0 = _dot2_many([jnp.where(lane_t == t1, dyb, 0.0), jnp.where(lane_t == t0, dyb, 0.0)],
                                      e_ref[...])
                G1 = Gt + dy1 * row(r_ref, t1)
                m1, m2 = _seg_sums([G1 * row(b_ref, t1), G1 * arow(0)], bd_ref[...])
                du1 = -m1
                du0 = -(m2 + du1 * arow(1) + dy0 * arow(2))
                G0 = G1 * row(w_ref, t1) + du1 * row(kk_ref, t1) + dy0 * row(r_ref, t0)
                G_next = G0 * row(w_ref, t0) + du0 * row(kk_ref, t0)
                if pending is not None:
                    rows = emit(rows, pending)
                pending = ((t1, G1, du1, dy1), (t0, G0, du0, dy0))
                Gt = G_next
            rows = emit(rows, pending)
            for ref, val in zip((dr_ref, dw_ref, dk_ref, dkk_ref, db_ref, dv_ref), rows):
                ref[pl.ds(base_b, G), :] = val
            return St, Gt

        St, Gt = lax.fori_loop(0, ng, group, (ck_ref[...], G_ref[...]))
        sbuf[sf, chunk] = St
        G_ref[...] = jnp.where(c >= 1, Gt, G_ref[...])

    fwd_chunk = lambda c: jnp.maximum(nchunk - 1 - c, 0)
    bwd_chunk = lambda c: jnp.minimum(nchunk - c, nchunk - 1)
    row_f = pl.BlockSpec((chunk, C), lambda c: (fwd_chunk(c), 0))
    row_b = pl.BlockSpec((chunk, C), lambda c: (bwd_chunk(c), 0))
    col_f = pl.BlockSpec((ng, N, N), lambda c: (fwd_chunk(c), 0, 0))
    col_b = pl.BlockSpec((ng, N, N), lambda c: (bwd_chunk(c), 0, 0))
    rshape = jax.ShapeDtypeStruct((S, C), F32)
    res = pl.pallas_call(
        body,
        grid=(nsteps,),
        in_specs=[row_f] * 4 + [col_f, pl.BlockSpec((None, N, C), lambda c: (fwd_chunk(c), 0, 0))]
        + [row_b] * 5 + [col_b, _whole_spec(bd), _whole_spec(e), _whole_spec(diag)] + [_ANY] * nx,
        out_specs=[row_b] * 6 + [_ANY] * nx,
        out_shape=[rshape] * 6 + _exchange_out_shapes(False, scatter_srcs),
        scratch_shapes=[pltpu.VMEM((N, C), F32), pltpu.VMEM((2, chunk + 1, N, C), F32),
                        pltpu.VMEM((2, chunk, N, C), F32), pltpu.VMEM((2, chunk, N, C), F32),
                        pltpu.VMEM((3, G, C), F32), pltpu.VMEM((3, G, C), F32)] + _exchange_sems(nx),
        compiler_params=_cparams(("arbitrary",)),
        name="rwkv_scan_bwd",
    )(w, k, kk, b, vB, ckpt, r, w, k, kk, b, dyB, bd, e, diag, *scatter_srcs)
    return tuple(res[:6]) + (list(res[6:]),)


def seq_cumsum(x, reverse, name):
    S, C = x.shape
    tb = min(CUMSUM_BLOCK, S)
    nb = S // tb

    def body(x_ref, o_ref, carry):
        i = pl.program_id(0)

        @pl.when(i == 0)
        def _():
            carry[...] = jnp.zeros_like(carry)

        ri = lax.broadcasted_iota(jnp.int32, (tb, tb), 0)
        ci = lax.broadcasted_iota(jnp.int32, (tb, tb), 1)
        tri = jnp.where((ci >= ri) if reverse else (ci <= ri), 1.0, 0.0).astype(F32)
        xb = x_ref[...]
        out = jnp.dot(tri, xb, precision=lax.Precision.HIGHEST, preferred_element_type=F32) + carry[...]
        o_ref[...] = out
        carry[...] = carry[...] + jnp.sum(xb, axis=0, keepdims=True)

    idx = (lambda i: (nb - 1 - i, 0)) if reverse else (lambda i: (i, 0))
    return pl.pallas_call(
        body,
        grid=(nb,),
        in_specs=[pl.BlockSpec((tb, C), idx)],
        out_specs=pl.BlockSpec((tb, C), idx),
        out_shape=jax.ShapeDtypeStruct((S, C), F32),
        scratch_shapes=[pltpu.VMEM((1, C), F32)],
        compiler_params=_cparams(("arbitrary",)),
        name=name,
    )(x)


def _fox_logits(q, k, cq, ck, diagonal):
    s = _dot(q, k, _NT) * (HEAD_DIM ** -0.5) + cq - ck
    if not diagonal:
        return s
    row = lax.broadcasted_iota(jnp.int32, s.shape, 0)
    col = lax.broadcasted_iota(jnp.int32, s.shape, 1)
    return jnp.where(col <= row, s, -jnp.inf)


def _fox_tiles(n, by_query):
    pairs = [(i, j) for i in range(n) for j in range(i + 1)] if by_query else \
            [(i, j) for j in range(n) for i in range(j, n)]
    return (jnp.asarray(np.array([p[0] for p in pairs], np.int32)),
            jnp.asarray(np.array([p[1] for p in pairs], np.int32)))


def _fox_specs(t, Dh):
    qs = pl.BlockSpec((None, t, Dh), lambda h, s, qt, kt: (h, qt[s], 0))
    ks = pl.BlockSpec((None, t, Dh), lambda h, s, qt, kt: (h, kt[s], 0))
    cqs = pl.BlockSpec((None, t, 1), lambda h, s, qt, kt: (h, qt[s], 0))
    cks = pl.BlockSpec((None, 1, t), lambda h, s, qt, kt: (h, 0, kt[s]))
    return qs, ks, cqs, cks


def _fox_call(body, tiles, Hh, in_specs, out_specs, out_shape, scratch, name, args):
    spec = pltpu.PrefetchScalarGridSpec(num_scalar_prefetch=2, grid=(Hh, tiles[0].shape[0]), in_specs=in_specs,
                                        out_specs=out_specs, scratch_shapes=scratch)
    return pl.pallas_call(body, grid_spec=spec, out_shape=out_shape,
                          compiler_params=_cparams(("parallel", "arbitrary")), name=name)(*tiles, *args)


def fox_fwd(q, k, v, c_col, c_row):
    Hh, S, Dh = q.shape
    tq = tk = min(FOX_TILE, S)

    def body(qt_ref, kt_ref, q_ref, k_ref, v_ref, cq_ref, ck_ref, o_ref, lse_ref, m_s, l_s, acc_s):
        qi, ki = qt_ref[pl.program_id(1)], kt_ref[pl.program_id(1)]

        @pl.when(ki == 0)
        def _():
            m_s[...] = jnp.full_like(m_s, -jnp.inf)
            l_s[...] = jnp.zeros_like(l_s)
            acc_s[...] = jnp.zeros_like(acc_s)

        def tile(diagonal):
            s = _fox_logits(q_ref[...], k_ref[...], cq_ref[...], ck_ref[...], diagonal)
            m_old = m_s[...]
            m_new = jnp.maximum(m_old, jnp.max(s, axis=-1, keepdims=True))
            alpha = jnp.exp(m_old - m_new)
            p = jnp.exp(s - m_new)
            l_s[...] = alpha * l_s[...] + jnp.sum(p, axis=-1, keepdims=True)
            acc_s[...] = alpha * acc_s[...] + _dot(p, v_ref[...], _NN)
            m_s[...] = m_new

        @pl.when(ki != qi)
        def _():
            tile(False)

        @pl.when(ki == qi)
        def _():
            tile(True)
            o_ref[...] = acc_s[...] / l_s[...]
            lse_ref[...] = m_s[...] + jnp.log(l_s[...])

    qs, ks, cqs, cks = _fox_specs(tq, Dh)
    return _fox_call(
        body, _fox_tiles(S // tq, True), Hh, [qs, ks, ks, cqs, cks], [qs, cqs],
        [jax.ShapeDtypeStruct((Hh, S, Dh), F32), jax.ShapeDtypeStruct((Hh, S, 1), F32)],
        [pltpu.VMEM((tq, 1), F32), pltpu.VMEM((tq, 1), F32), pltpu.VMEM((tq, Dh), F32)],
        "fox_fwd", (q, k, v, c_col, c_row))


def fox_bwd(q, k, v, c_col, c_row, o, lse, do):
    Hh, S, Dh = q.shape
    tq = tk = min(FOX_TILE, S)
    nk = S // tk

    def body(qt_ref, kt_ref, q_ref, k_ref, v_ref, cq_ref, ck_ref, o_ref, lse_ref, do_ref,
             dq_ref, dr_ref, dk_ref, dv_ref, dc_ref, acc_s, row_s):
        step = pl.program_id(1)
        qi, ki = qt_ref[step], kt_ref[step]

        @pl.when(step == 0)
        def _():
            dk_ref[...] = jnp.zeros_like(dk_ref)
            dv_ref[...] = jnp.zeros_like(dv_ref)
            dc_ref[...] = jnp.zeros_like(dc_ref)

        @pl.when(ki == 0)
        def _():
            acc_s[...] = jnp.zeros_like(acc_s)
            row_s[...] = jnp.zeros_like(row_s)

        def tile(diagonal):
            q_t, kb, vb, do_t = q_ref[...], k_ref[...], v_ref[...], do_ref[...]
            s = _fox_logits(q_t, kb, cq_ref[...], ck_ref[...], diagonal)
            p = jnp.exp(s - lse_ref[...])
            delta = jnp.sum(do_t * o_ref[...], axis=-1, keepdims=True)
            ds = p * (_dot(do_t, vb, _NT) - delta)
            acc_s[...] += _dot(ds, kb, _NN)
            row_s[...] += jnp.sum(ds, axis=-1, keepdims=True)
            dk_ref[ki] += _dot(ds, q_t, _TN) * (HEAD_DIM ** -0.5)
            dv_ref[ki] += _dot(p, do_t, _TN)
            dc_ref[ki] += jnp.sum(ds, axis=0, keepdims=True)

        @pl.when(ki != qi)
        def _():
            tile(False)

        @pl.when(ki == qi)
        def _():
            tile(True)
            dq_ref[...] = acc_s[...] * (HEAD_DIM ** -0.5)
            dr_ref[...] = row_s[...]

    qs, ks, cqs, cks = _fox_specs(tq, Dh)
    head = lambda *blk: pl.BlockSpec((None,) + blk, lambda h, s, qt, kt: (h,) + (0,) * len(blk))
    dq, dr, dk, dv, dc = _fox_call(
        body, _fox_tiles(S // tq, True), Hh, [qs, ks, ks, cqs, cks, qs, cqs, qs],
        [qs, cqs, head(nk, tk, Dh), head(nk, tk, Dh), head(nk, 1, tk)],
        [jax.ShapeDtypeStruct((Hh, S, Dh), F32), jax.ShapeDtypeStruct((Hh, S, 1), F32),
         jax.ShapeDtypeStruct((Hh, nk, tk, Dh), F32), jax.ShapeDtypeStruct((Hh, nk, tk, Dh), F32),
         jax.ShapeDtypeStruct((Hh, nk, 1, tk), F32)],
        [pltpu.VMEM((tq, Dh), F32), pltpu.VMEM((tq, 1), F32)],
        "fox_bwd", (q, k, v, c_col, c_row, o, lse, do))
    return dq, dr, dk.reshape(Hh, S, Dh), dv.reshape(Hh, S, Dh), dc.reshape(Hh, 1, S)


def _heads(a, nh):
    S = a.shape[0]
    return a.reshape(S, nh, HEAD_DIM).transpose(1, 0, 2)


def _unheads(a):
    nh, S, _ = a.shape
    return a.transpose(1, 0, 2).reshape(S, nh * HEAD_DIM)


def _shift_down(a):
    return jnp.pad(a[:-1], ((1, 0), (0, 0)))


def _shift_up(a):
    return jnp.pad(a[1:], ((0, 1), (0, 0)))


def _block_diag_ones():
    i = np.arange(RWKV_DIM) // HEAD_DIM
    return jnp.asarray((i[:, None] == i[None, :]).astype(np.float32))


FFN_ROWS = 1024
FFN_COLS = 256


def _ffn_specs(S, F, tm, fc):
    nf = F // fc
    row = pl.BlockSpec((tm, D_MODEL), lambda i, j: (i, 0))
    vec = pl.BlockSpec((1, D_MODEL), lambda i, j: (0, 0))
    wg = pl.BlockSpec((D_MODEL, fc), lambda i, j: (0, j))
    wu = pl.BlockSpec((D_MODEL, fc), lambda i, j: (0, nf + j))
    wd = pl.BlockSpec((fc, D_MODEL), lambda i, j: (j, 0))
    hid = pl.BlockSpec((tm, fc), lambda i, j: (i, j))
    return nf, row, vec, wg, wu, wd, hid


def ffn_fwd(x, g_norm, w_gu, w_down, tag):
    S, F = x.shape[0], w_down.shape[0]
    tm, fc = min(FFN_ROWS, S), FFN_COLS
    nf, row, vec, wg, wu, wd, _ = _ffn_specs(S, F, tm, fc)

    def body(x_ref, g_ref, wg_ref, wu_ref, wd_ref, o_ref, hn_ref, hn_s, acc):
        j = pl.program_id(1)

        @pl.when(j == 0)
        def _():
            hn_s[...] = _rms(x_ref[...], g_ref[...]).astype(BF16)
            hn_ref[...] = hn_s[...]
            acc[...] = jnp.zeros_like(acc)

        g = _dot(hn_s[...], wg_ref[...], _NN)
        u = _dot(hn_s[...], wu_ref[...], _NN)
        acc[...] += _dot(g * _sigmoid_tanh(g) * u, wd_ref[...], _NN)

        @pl.when(j == nf - 1)
        def _():
            o_ref[...] = x_ref[...] + 0.5 * acc[...]

    out, hn = pl.pallas_call(
        body,
        grid=(S // tm, nf),
        in_specs=[row, vec, wg, wu, wd],
        out_specs=[row, row],
        out_shape=[jax.ShapeDtypeStruct((S, D_MODEL), F32), jax.ShapeDtypeStruct((S, D_MODEL), BF16)],
        scratch_shapes=[pltpu.VMEM((tm, D_MODEL), BF16), pltpu.VMEM((tm, D_MODEL), F32)],
        compiler_params=_cparams(("parallel", "arbitrary")),
        name=tag + "_fwd",
    )(x, g_norm, w_gu, w_gu, w_down)
    return out, (x, hn)


def ffn_bwd(dy, saved, g_norm, w_gu, w_down, tag, scatter_srcs=()):
    x, hn = saved
    S, F = x.shape[0], w_down.shape[0]
    tm, fc = min(FFN_ROWS, S), FFN_COLS
    nf, row, vec, wg, wu, wd, hid = _ffn_specs(S, F, tm, fc)
    nx = len(scatter_srcs)

    def body(*refs):
        dy_ref, x_ref, hn_ref, g_ref, wg_ref, wu_ref, wd_ref = refs[:7]
        dx_ref, dgn_ref, a_ref, dg_ref, du_ref = refs[7 + nx:12 + nx]
        dyh_s, dhn = refs[12 + 2 * nx:14 + 2 * nx]
        i, j = pl.program_id(0), pl.program_id(1)
        if nx:
            _exchange_during(i * nf + j, (S // tm) * nf, False, refs[7:7 + nx], refs[12 + nx:12 + 2 * nx],
                             *refs[14 + 2 * nx:])

        @pl.when(j == 0)
        def _():
            dyh_s[...] = (0.5 * dy_ref[...]).astype(BF16)
            dhn[...] = jnp.zeros_like(dhn)

        hn_t = hn_ref[...]
        g = _dot(hn_t, wg_ref[...], _NN)
        u = _dot(hn_t, wu_ref[...], _NN)
        da = _dot(dyh_s[...], wd_ref[...], _NT)
        sig = _sigmoid_tanh(g)
        gs = g * sig
        a_ref[...] = (gs * u).astype(BF16)
        dg = ((da * u) * (sig + gs * (1.0 - sig))).astype(BF16)
        du = (da * gs).astype(BF16)
        dg_ref[...] = dg
        du_ref[...] = du
        dhn[...] += _dot(jnp.concatenate([dg, du], axis=1),
                         jnp.concatenate([wg_ref[...], wu_ref[...]], axis=1), _NT)

        @pl.when(j == nf - 1)
        def _():
            _, vjp_n = jax.vjp(_rms, x_ref[...], g_ref[...])
            dx, dgn = vjp_n(dhn[...])
            dx_ref[...] = dy_ref[...] + dx

            @pl.when(i == 0)
            def _():
                dgn_ref[...] = jnp.zeros_like(dgn_ref)

            dgn_ref[...] += dgn

    hshape = jax.ShapeDtypeStruct((S, F), BF16)
    res = pl.pallas_call(
        body,
        grid=(S // tm, nf),
        in_specs=[row, row, row, vec, wg, wu, wd] + [_ANY] * nx,
        out_specs=[row, vec, hid, hid, hid] + [_ANY] * nx,
        out_shape=[jax.ShapeDtypeStruct((S, D_MODEL), F32), jax.ShapeDtypeStruct((1, D_MODEL), F32),
                   hshape, hshape, hshape] + _exchange_out_shapes(False, scatter_srcs),
        scratch_shapes=[pltpu.VMEM((tm, D_MODEL), BF16), pltpu.VMEM((tm, D_MODEL), F32)]
        + (_exchange_sems(nx) if nx else []),
        compiler_params=_cparams(("arbitrary", "arbitrary")),
        name=tag + "_bwd",
    )(dy, x, hn, g_norm, w_gu, w_gu, w_down, *scatter_srcs)
    dx, dgn, act, dg, du = res[:5]
    d_wdown = matmul(act, dy, "tn", tag + "_dwd", out_dtype=BF16, scale=0.5, tm=1408)
    d_wgu = jnp.concatenate([matmul(hn, dg, "tn", tag + "_dwg", out_dtype=BF16),
                             matmul(hn, du, "tn", tag + "_dwu", out_dtype=BF16)], axis=1)
    return dx, dgn, d_wgu, d_wdown, list(res[5:])


def ple_fwd(x, p_i, g_norm, w_gate, w_proj, tag):
    hn, = rowwise(_f_rms, [x], [g_norm], [(D_MODEL, BF16)], tag + "_rms")
    z = matmul(hn, w_gate, "nn", tag + "_gate")
    pp = matmul(p_i, w_proj, "nn", tag + "_proj")
    out, = rowwise(_f_ple, [x, z, pp], [], [(D_MODEL, F32)], tag + "_mix")
    return out, (x, hn, z, pp)


def ple_bwd(dy, saved, p_i, g_norm, w_gate, tag):
    x, hn, z, pp = saved
    (dz, dpp), _ = rowwise_vjp(_f_ple, [x, z, pp], [], [dy], tag + "_dmix", need=[False, True, True],
                               row_dtype=BF16)
    d_wproj = matmul(p_i, dpp, "tn", tag + "_dwp", out_dtype=BF16)
    d_wgate = matmul(hn, dz, "tn", tag + "_dwg", out_dtype=BF16)
    dhn = matmul(dz, w_gate, "nt", tag + "_dhn")
    (dx,), (dgn,) = rowwise_vjp(_f_rms_res, [x], [g_norm], [dhn, dy], tag + "_drms")
    return dx, dgn, d_wgate, d_wproj


def _swa_consts(sinks):
    slopes = np.asarray([2.0 ** (-(i + 1)) for i in range(SWA_HEADS)], np.float32)
    slope_col = jnp.asarray(np.repeat(slopes, BLOCK).reshape(SWA_KV_HEADS, SWA_GROUP * BLOCK, 1))
    sink_col = jnp.repeat(sinks.reshape(SWA_HEADS), BLOCK).reshape(SWA_KV_HEADS, SWA_GROUP * BLOCK, 1)
    return sink_col, slope_col


def even_mix_fwd(x, W, gather_src, later_weights):
    S = x.shape[0]
    hn, = rowwise(_f_rms, [x], [W["mix_norm0"]], [(D_MODEL, BF16)], "emix_rms")
    proj = matmul(hn, W["even_w_in"], "nn", "emix_in")
    qa = _heads(proj[:, :SWA_Q], SWA_HEADS).reshape(SWA_KV_HEADS, SWA_GROUP, S, HEAD_DIM)
    ka = _heads(proj[:, SWA_Q:SWA_Q + SWA_KV], SWA_KV_HEADS)
    va = _heads(proj[:, SWA_Q + SWA_KV:SWA_COLS], SWA_KV_HEADS)
    sink_col, slope_col = _swa_consts(W["swa_sinks"])
    ya = swa_fwd(qa, ka, va, sink_col, slope_col)
    ya = _unheads(ya.reshape(SWA_HEADS, S, HEAD_DIM))
    hb = proj[:, SWA_COLS:]
    h, = rowwise(_f_mix, [hb, _shift_down(hb)], [W["rwkv_mu"]], [(hb.shape[1], F32)], "rwkv_shift")
    hr, hk, hv = h[:, :512], h[:, 512:1024], h[:, 1024:1536]
    hw, ha, hg = h[:, 1536:1600], h[:, 1600:1664], h[:, 1664:1792]
    bd = _block_diag_ones()
    pre_params = [W["rwkv_w0"], W["rwkv_w2"], W["rwkv_a0"], W["rwkv_a2"], W["rwkv_g2"], W["rwkv_k_k"],
                  W["rwkv_k_a"]]
    decay, k2, kk, b, g = rowwise(_f_rwkv_pre, [hk, hw, ha, hg], pre_params + [bd],
                                  [(RWKV_DIM, F32)] * 5, "rwkv_pre")
    vT = _to_colblocks(hv)
    y, ckpt, gathered = rwkv_scan_fwd(hr, decay, k2, kk, b, vT, gather_src)
    late = later_weights(gathered)
    post_params = [W["rwkv_ln_w"], W["rwkv_ln_b"], W["rwkv_r_k"]]
    yb, = rowwise(_f_rwkv_post, [y, hr, k2, hv, g], post_params + [bd], [(RWKV_DIM, F32)], "rwkv_post")
    cat = jnp.concatenate([ya, yb], axis=1).astype(BF16)
    out = matmul(cat, late["even_w_out"], "nn", "emix_out", res=x)
    saved = (x, hn, qa, ka, va, sink_col, slope_col, hb, hr, hk, hv, hw, ha, hg, decay, k2, kk, b, g, vT,
             ckpt, y, cat)
    return out, saved, late


def even_mix_bwd(dy, saved, W, scatter_src):
    (x, hn, qa, ka, va, sink_col, slope_col, hb, hr, hk, hv, hw, ha, hg, decay, k2, kk, b, g, vT, ckpt, y,
     cat) = saved
    S = x.shape[0]
    grads = {}
    dcat = matmul(dy, W["even_w_out"], "nt", "emix_dcat")
    grads["even_w_out"] = matmul(cat, dy, "tn", "emix_dwout", out_dtype=BF16)
    dya, dyb = dcat[:, :SWA_Q], dcat[:, SWA_Q:]
    dya_h = _heads(dya, SWA_HEADS).reshape(SWA_KV_HEADS, SWA_GROUP, S, HEAD_DIM)
    dqa, dkp, dkc, dvp, dvc, dsink = swa_bwd(qa, ka, va, sink_col, slope_col, dya_h)
    shift_blk = lambda a: jnp.pad(a[:, BLOCK:], ((0, 0), (0, BLOCK), (0, 0)))
    dka = dkc + shift_blk(dkp)
    dva = dvc + shift_blk(dvp)
    grads["swa_sinks"] = dsink.reshape(SWA_HEADS, BLOCK).sum(axis=1).reshape(1, SWA_HEADS)
    dqa = _unheads(dqa.reshape(SWA_HEADS, S, HEAD_DIM))
    dka, dva = _unheads(dka), _unheads(dva)
    bd = _block_diag_ones()
    post_params = [W["rwkv_ln_w"], W["rwkv_ln_b"], W["rwkv_r_k"]]
    (d_y, d_r1, d_k2a, d_v1, d_g), (d_lnw, d_lnb, d_rk) = rowwise_vjp(
        _f_rwkv_post, [y, hr, k2, hv, g], post_params, [dyb], "rwkv_dpost", consts=[bd], tm=128)
    grads["rwkv_ln_w"], grads["rwkv_ln_b"], grads["rwkv_r_k"] = d_lnw, d_lnb, d_rk
    d_r2, d_w, d_k2b, d_kk, d_b, d_v2, exchanged = rwkv_scan_bwd(hr, decay, k2, kk, b, vT, _to_colblocks(d_y), ckpt,
                                                                  scatter_src)
    pre_params = [W["rwkv_w0"], W["rwkv_w2"], W["rwkv_a0"], W["rwkv_a2"], W["rwkv_g2"], W["rwkv_k_k"],
                  W["rwkv_k_a"]]
    (d_hk, d_hw, d_ha, d_hg), dpre = rowwise_vjp(
        _f_rwkv_pre, [hk, hw, ha, hg], pre_params, [d_w, d_k2a + d_k2b, d_kk, d_b, d_g], "rwkv_dpre",
        consts=[bd], tm=128)
    for nm, gval in zip(["rwkv_w0", "rwkv_w2", "rwkv_a0", "rwkv_a2", "rwkv_g2", "rwkv_k_k", "rwkv_k_a"], dpre):
        grads[nm] = gval
    d_h = jnp.concatenate([d_r1 + d_r2, d_hk, d_v1 + d_v2, d_hw, d_ha, d_hg], axis=1)
    (d_hb, d_sh), (d_mu,) = rowwise_vjp(_f_mix, [hb, _shift_down(hb)], [W["rwkv_mu"]], [d_h], "rwkv_dshift")
    grads["rwkv_mu"] = d_mu
    d_hb = d_hb + _shift_up(d_sh)
    dproj = jnp.concatenate([dqa, dka, dva, d_hb], axis=1).astype(BF16)
    grads["even_w_in"] = matmul(hn, dproj, "tn", "emix_dwin", out_dtype=BF16)
    dhn = matmul(dproj, W["even_w_in"], "nt", "emix_dhn")
    (dx,), (dgn,) = rowwise_vjp(_f_rms_res, [x], [W["mix_norm0"]], [dhn, dy], "emix_drms")
    grads["mix_norm0"] = dgn
    return dx, grads, exchanged


def odd_mix_fwd(x, W):
    S = x.shape[0]
    hn, = rowwise(_f_rms, [x], [W["mix_norm1"]], [(D_MODEL, BF16)], "omix_rms")
    proj = matmul(hn, W["fox_w_in"], "nn", "omix_in")
    q = _heads(proj[:, :FOX_DIM], FOX_HEADS).astype(BF16)
    k = _heads(proj[:, FOX_DIM:2 * FOX_DIM], FOX_HEADS).astype(BF16)
    v = _heads(proj[:, 2 * FOX_DIM:3 * FOX_DIM], FOX_HEADS).astype(BF16)
    fz = proj[:, 3 * FOX_DIM:]
    logf, = rowwise(_f_logf, [fz], [W["fox_b_f"]], [(128, F32)], "fox_logf")
    c = seq_cumsum(logf, False, "fox_cumsum")[:, :FOX_HEADS]
    c_col = c.T.reshape(FOX_HEADS, S, 1)
    c_row = c.T.reshape(FOX_HEADS, 1, S)
    o, lse = fox_fwd(q, k, v, c_col, c_row)
    yc = _unheads(o).astype(BF16)
    out = matmul(yc, W["fox_w_out"], "nn", "omix_out", res=x)
    return out, (x, hn, q, k, v, fz, c_col, c_row, o, lse, yc)


def odd_mix_bwd(dy, saved, W):
    x, hn, q, k, v, fz, c_col, c_row, o, lse, yc = saved
    S = x.shape[0]
    grads = {}
    dyc = matmul(dy, W["fox_w_out"], "nt", "omix_dyc")
    grads["fox_w_out"] = matmul(yc, dy, "tn", "omix_dwout", out_dtype=BF16)
    do = _heads(dyc, FOX_HEADS)
    dq, drow, dk, dv, dcol = fox_bwd(q, k, v, c_col, c_row, o, lse, do)
    dc = (drow.reshape(FOX_HEADS, S) - dcol.reshape(FOX_HEADS, S)).T
    dc = jnp.pad(dc, ((0, 0), (0, 128 - FOX_HEADS)))
    dlogf = seq_cumsum(dc, True, "fox_rcumsum")
    (dfz,), (dbf,) = rowwise_vjp(_f_logf, [fz], [W["fox_b_f"]], [dlogf], "fox_dlogf")
    grads["fox_b_f"] = dbf
    dproj = jnp.concatenate([_unheads(dq), _unheads(dk), _unheads(dv), dfz], axis=1).astype(BF16)
    grads["fox_w_in"] = matmul(hn, dproj, "tn", "omix_dwin", out_dtype=BF16)
    dhn = matmul(dproj, W["fox_w_in"], "nt", "omix_dhn")
    (dx,), (dgn,) = rowwise_vjp(_f_rms_res, [x], [W["mix_norm1"]], [dhn, dy], "omix_drms")
    grads["mix_norm1"] = dgn
    return dx, grads


def device_step(x, p, target, W, gather_src, layer1_weights, layer1_grads, mixer_grads):
    W = dict(W)
    saved = []
    h = x
    for i in range(2):
        h, s1 = ffn_fwd(h, W[f"ffn1_norm{i}"], W[f"ffn1_w_gu{i}"], W[f"ffn1_w_down{i}"], f"ffn1_{i}")
        if i == 0:
            h, s2, late = even_mix_fwd(h, W, gather_src, layer1_weights)
            W.update(late)
        else:
            h, s2 = odd_mix_fwd(h, W)
        h, s3 = ffn_fwd(h, W[f"ffn2_norm{i}"], W[f"ffn2_w_gu{i}"], W[f"ffn2_w_down{i}"], f"ffn2_{i}")
        h, s4 = ple_fwd(h, p[i], W[f"ple_norm{i}"], W[f"ple_w_gate{i}"], W[f"ple_w_proj{i}"], f"ple_{i}")
        saved.append((s1, s2, s3, s4))
    dh, d_final, loss = loss_head(h, target, W["final_norm"])
    G = {"final_norm": d_final}
    for i in (1, 0):
        s1, s2, s3, s4 = saved[i]
        dh, G[f"ple_norm{i}"], G[f"ple_w_gate{i}"], G[f"ple_w_proj{i}"] = ple_bwd(
            dh, s4, p[i], W[f"ple_norm{i}"], W[f"ple_w_gate{i}"], f"ple_{i}")
        dh, G[f"ffn2_norm{i}"], G[f"ffn2_w_gu{i}"], G[f"ffn2_w_down{i}"], _ = ffn_bwd(
            dh, s3, W[f"ffn2_norm{i}"], W[f"ffn2_w_gu{i}"], W[f"ffn2_w_down{i}"], f"ffn2_{i}")
        if i == 0:
            dh, gm, exchanged = even_mix_bwd(dh, s2, W, layer1_grads(G))
        else:
            dh, gm = odd_mix_bwd(dh, s2, W)
        G.update(gm)
        dh, G[f"ffn1_norm{i}"], G[f"ffn1_w_gu{i}"], G[f"ffn1_w_down{i}"], exchanged_mid = ffn_bwd(
            dh, s1, W[f"ffn1_norm{i}"], W[f"ffn1_w_gu{i}"], W[f"ffn1_w_down{i}"], f"ffn1_{i}",
            scatter_srcs=mixer_grads(G) if i == 0 else ())
    return loss, dh, G, exchanged, exchanged_mid


_MESH = pl.DeviceIdType.MESH
_ANY = pl.BlockSpec(memory_space=pl.ANY)


def _exchange_sems(n):
    return [pltpu.SemaphoreType.DMA((7 * n,)), pltpu.SemaphoreType.DMA((7 * n,)), pltpu.SemaphoreType.DMA((n,))]


def all_gather(xs, name):
    n = len(xs)

    def body(*refs):
        x_refs, out_refs = refs[:n], refs[n:2 * n]
        send_sems, recv_sems, local_sems = refs[2 * n:]
        x_, y_, c_ = lax.axis_index("x"), lax.axis_index("y"), lax.axis_index("c")
        me, sibling = (x_, y_, c_), (x_, y_, 1 - c_)
        chips = [(1 - x_, y_), (x_, 1 - y_), (1 - x_, 1 - y_)]

        def copy(b, k, block, to, from_input=False):
            slot = out_refs[b].at[4 * block[0] + 2 * block[1] + block[2]]
            return pltpu.make_async_remote_copy(
                src_ref=x_refs[b] if from_input else slot, dst_ref=slot,
                send_sem=send_sems.at[7 * b + k], recv_sem=recv_sems.at[7 * b + k], device_id=to,
                device_id_type=_MESH)

        bufs = range(n)
        mine = [pltpu.make_async_copy(x_refs[b], out_refs[b].at[4 * x_ + 2 * y_ + c_], local_sems.at[b]) for b in bufs]
        first = [copy(b, 0, me, sibling, True) for b in bufs]
        first += [copy(b, 1 + j, me, (*chip, c_), True) for j, chip in enumerate(chips) for b in bufs]
        for cp in mine + first:
            cp.start()
        passed = []
        for j, chip in enumerate(chips):
            for b in bufs:
                copy(b, 1 + j, (*chip, c_), me).wait_recv()
                passed.append(copy(b, 4 + j, (*chip, c_), sibling))
                passed[-1].start()
        for b in bufs:
            copy(b, 0, sibling, me).wait_recv()
            for j, chip in enumerate(chips):
                copy(b, 4 + j, (*chip, 1 - c_), me).wait_recv()
        for cp in first + passed:
            cp.wait_send()
        for cp in mine:
            cp.wait()

    return pl.pallas_call(
        body,
        out_shape=[jax.ShapeDtypeStruct((N_DEV,) + x.shape, x.dtype) for x in xs],
        in_specs=[_ANY] * n,
        out_specs=[_ANY] * n,
        scratch_shapes=_exchange_sems(n),
        name=name,
    )(*xs)


def _direct_exchange(gather, s_refs, r_refs, send_sems, recv_sems, local_sems):
    x_, y_, c_ = lax.axis_index("x"), lax.axis_index("y"), lax.axis_index("c")
    my = 4 * x_ + 2 * y_ + c_
    copies = []
    for b, (s_ref, r_ref) in enumerate(zip(s_refs, r_refs)):
        copies.append(pltpu.make_async_copy(s_ref if gather else s_ref.at[my], r_ref.at[my], local_sems.at[b]))
        for m in range(1, N_DEV):
            px = 1 - x_ if (m >> 2) & 1 else x_
            py = 1 - y_ if (m >> 1) & 1 else y_
            pc = 1 - c_ if m & 1 else c_
            copies.append(pltpu.make_async_remote_copy(
                src_ref=s_ref if gather else s_ref.at[4 * px + 2 * py + pc], dst_ref=r_ref.at[my],
                send_sem=send_sems.at[7 * b + m - 1], recv_sem=recv_sems.at[7 * b + m - 1],
                device_id=(px, py, pc), device_id_type=_MESH))
    return copies


def _exchange_during(step, n_steps, gather, s_refs, r_refs, send_sems, recv_sems, local_sems):
    copies = _direct_exchange(gather, s_refs, r_refs, send_sems, recv_sems, local_sems)

    @pl.when(step == 0)
    def _():
        for cp in copies:
            cp.start()

    @pl.when(step == n_steps - 1)
    def _():
        for cp in copies:
            cp.wait()


def _exchange_out_shapes(gather, srcs):
    return [jax.ShapeDtypeStruct(((N_DEV,) + s.shape) if gather else s.shape, s.dtype) for s in srcs]


def all_to_all(sends, name):
    n = len(sends)

    def body(*refs):
        copies = _direct_exchange(False, refs[:n], refs[n:2 * n], *refs[2 * n:])
        for cp in copies:
            cp.start()
        for cp in copies:
            cp.wait()

    return pl.pallas_call(
        body,
        out_shape=_exchange_out_shapes(False, sends),
        in_specs=[_ANY] * n,
        out_specs=[_ANY] * n,
        scratch_shapes=_exchange_sems(n),
        name=name,
    )(*sends)


def adamw(w, m, v, parts, name, tm=256):
    R, C = w.shape
    tm = _pick(R, tm, 8) if R >= 8 else R

    def body(w_ref, m_ref, v_ref, p_ref, g_ref, d_ref, nm_ref, nv_ref):
        g = p_ref[0].astype(F32)
        for s in range(1, N_DEV):
            g = g + p_ref[s].astype(F32)
        nm = ADAM_B1 * m_ref[...] + (1.0 - ADAM_B1) * g
        nv = ADAM_B2 * v_ref[...] + (1.0 - ADAM_B2) * (g * g)
        m_hat = nm / (1.0 - ADAM_B1 ** ADAM_STEP)
        v_hat = nv / (1.0 - ADAM_B2 ** ADAM_STEP)
        g_ref[...] = g
        d_ref[...] = -ADAM_LR * (m_hat / (jnp.sqrt(v_hat) + ADAM_EPS) + ADAM_WD * w_ref[...])
        nm_ref[...] = nm
        nv_ref[...] = nv

    row = pl.BlockSpec((tm, C), lambda i: (i, 0))
    out = jax.ShapeDtypeStruct((R, C), F32)
    return pl.pallas_call(
        body,
        grid=(R // tm,),
        in_specs=[row, row, row, pl.BlockSpec((N_DEV, tm, C), lambda i: (0, i, 0))],
        out_specs=[row] * 4,
        out_shape=[out] * 4,
        compiler_params=_cparams(("parallel",)),
        name=name,
    )(w, m, v, parts)


_WEIGHTS = ["ffn1_norm", "ffn1_w_gu", "ffn1_w_down", "mix_norm", "ffn2_norm", "ffn2_w_gu", "ffn2_w_down",
            "ple_norm", "ple_w_gate", "ple_w_proj", "even_w_in", "even_w_out", "swa_sinks", "rwkv_mu",
            "rwkv_w0", "rwkv_w2", "rwkv_a0", "rwkv_a2", "rwkv_g2", "rwkv_k_k", "rwkv_k_a", "rwkv_r_k",
            "rwkv_ln_w", "rwkv_ln_b", "fox_w_in", "fox_b_f", "fox_w_out", "final_norm"]
_SHARD_AXIS = {"ffn1_w_gu": 2, "ffn1_w_down": 1, "ffn2_w_gu": 2, "ffn2_w_down": 1, "ple_w_gate": 1,
               "ple_w_proj": 2, "even_w_in": 2, "even_w_out": 1, "rwkv_w2": 2, "rwkv_a2": 2, "rwkv_g2": 2,
               "fox_w_in": 2, "fox_w_out": 1}
_SHARDED = [n for n in _WEIGHTS if n in _SHARD_AXIS]
_REPLICATED = [n for n in _WEIGHTS if n not in _SHARD_AXIS]
_PER_LAYER = ("ffn1_w_gu", "ffn1_w_down", "ffn2_w_gu", "ffn2_w_down", "ple_w_gate", "ple_w_proj")
_ALL_PIECES = ([(n, 0) for n in _PER_LAYER] + [(n, 0) for n in ("even_w_in", "even_w_out", "rwkv_w2", "rwkv_a2", "rwkv_g2")]
               + [(n, 1) for n in _PER_LAYER] + [("fox_w_in", 0), ("fox_w_out", 0)])
_FIRST_WEIGHTS = [(n, 0) for n in ("ffn1_w_gu", "ffn1_w_down", "even_w_in", "rwkv_w2", "rwkv_a2", "rwkv_g2")]
_PIECES = [_FIRST_WEIGHTS, [pc for pc in _ALL_PIECES if pc not in _FIRST_WEIGHTS]]
_LATE_GRADS = _FIRST_WEIGHTS + [("even_w_out", 0)]
_LAST_GRADS = [("ffn1_w_gu", 0), ("ffn1_w_down", 0)]
_GRAD_PIECES = [_LAST_GRADS, [pc for pc in _LATE_GRADS if pc not in _LAST_GRADS],
                [pc for pc in _ALL_PIECES if pc not in _LATE_GRADS]]
_PACK_LANES = 1024
_PACK_ROW_TILE = 256


def _piece_key(piece):
    name, idx = piece
    return f"{name}{idx}" if name in _PER_LAYER else name


_KINDS = ("gu", "rows", "misc")


def _kind(piece):
    if piece[0] in ("ffn1_w_gu", "ffn2_w_gu"):
        return "gu"
    return "rows" if _SHARD_AXIS[piece[0]] == 1 else "misc"


def _of_kind(pieces, shapes, kind):
    return [(pc, shp) for pc, shp in zip(pieces, shapes) if _kind(pc) == kind]


def _pad_rows(flat, axis):
    pad = [(0, 0)] * flat.ndim
    pad[axis] = (0, -flat.shape[axis] % _PACK_ROW_TILE)
    return jnp.pad(flat, pad)


def _kinds_of(pieces):
    return [kind for kind in _KINDS if any(_kind(pc) == kind for pc in pieces)]


def _bundle(get, pieces, dtype):
    make = {"gu": jnp.stack,
            "rows": lambda ps: jnp.concatenate(ps, axis=0),
            "misc": lambda ps: _pad_rows(jnp.concatenate([a.reshape(-1, _PACK_LANES) for a in ps], axis=0), 0)}
    return [make[kind]([get(pc).astype(dtype) for pc in pieces if _kind(pc) == kind]) for kind in _kinds_of(pieces)]


def _unbundle(bufs, pieces, shapes):
    out = {}
    for buf, kind in zip(bufs, _kinds_of(pieces)):
        of_kind = _of_kind(pieces, shapes, kind)
        if kind == "gu":
            stacked = buf.reshape((len(of_kind),) + of_kind[0][1])
            for j, (pc, _) in enumerate(of_kind):
                out[pc] = stacked[j]
            continue
        r0 = 0
        for pc, shp in of_kind:
            n = math.prod(shp) // _PACK_LANES
            out[pc] = buf[r0:r0 + n].reshape(shp)
            r0 += n
    return out


def _unshard(gathered, pieces, shapes):
    full = {}
    for j, (pc, shp) in enumerate(_of_kind(pieces, shapes, "gu")):
        full[_piece_key(pc)] = jnp.moveaxis(gathered[0][:, j], 0, 1).reshape(shp[0], N_DEV * shp[1])
    r0 = 0
    for pc, shp in _of_kind(pieces, shapes, "rows"):
        full[_piece_key(pc)] = gathered[1][:, r0:r0 + shp[0]].reshape(N_DEV * shp[0], shp[1])
        r0 += shp[0]
    r0 = 0
    for pc, shp in _of_kind(pieces, shapes, "misc"):
        n = math.prod(shp) // _PACK_LANES
        seg = gathered[2][:, r0:r0 + n].reshape((N_DEV,) + shp)
        full[_piece_key(pc)] = jnp.moveaxis(seg, 0, 1).reshape(shp[0], N_DEV * shp[1])
        r0 += n
    return full


def _to_shards(full, pieces, shapes):
    get = lambda pc: full[_piece_key(pc)].astype(BF16)
    cols = lambda pc, shp: jnp.moveaxis(get(pc).reshape(shp[0], N_DEV, shp[1]), 1, 0)
    make = {"gu": lambda ps: jnp.stack([cols(pc, shp) for pc, shp in ps], axis=1),
            "rows": lambda ps: jnp.concatenate([get(pc).reshape((N_DEV,) + shp) for pc, shp in ps], axis=1),
            "misc": lambda ps: _pad_rows(jnp.concatenate([cols(pc, shp).reshape(N_DEV, -1, _PACK_LANES)
                                                          for pc, shp in ps], axis=1), 1)}
    return [make[kind](_of_kind(pieces, shapes, kind)) for kind in _kinds_of(pieces)]


def _layer_weights(full):
    W = dict(full)
    if "fox_w_in" in W:
        W["fox_w_in"] = jnp.pad(W["fox_w_in"], ((0, 0), (0, FOX_IN_PAD - W["fox_w_in"].shape[1])))
    for n in ("rwkv_w2", "rwkv_a2", "rwkv_g2"):
        if n in W:
            W[n] = W[n].astype(F32)
    return W


def _pack_small(vals):
    flat = jnp.concatenate([v.reshape(1, -1) for v in vals], axis=1)
    n = flat.shape[1]
    return jnp.pad(flat, ((0, 0), (0, -n % 128)))


def _unpack_small(flat, shapes):
    out, c0 = [], 0
    for shp in shapes:
        n = math.prod(shp)
        out.append(flat[0, c0:c0 + n].reshape(shp))
        c0 += n
    return out


def kernel(x, p, ffn1_norm, ffn1_w_gu, ffn1_w_down, mix_norm, ffn2_norm, ffn2_w_gu, ffn2_w_down, ple_norm, ple_w_gate, ple_w_proj, even_w_in, even_w_out, swa_sinks, rwkv_mu, rwkv_w0, rwkv_w2, rwkv_a0, rwkv_a2, rwkv_g2, rwkv_k_k, rwkv_k_a, rwkv_r_k, rwkv_ln_w, rwkv_ln_b, fox_w_in, fox_b_f, fox_w_out, final_norm, loss_target, m_ffn1_norm, m_ffn1_w_gu, m_ffn1_w_down, m_mix_norm, m_ffn2_norm, m_ffn2_w_gu, m_ffn2_w_down, m_ple_norm, m_ple_w_gate, m_ple_w_proj, m_even_w_in, m_even_w_out, m_swa_sinks, m_rwkv_mu, m_rwkv_w0, m_rwkv_w2, m_rwkv_a0, m_rwkv_a2, m_rwkv_g2, m_rwkv_k_k, m_rwkv_k_a, m_rwkv_r_k, m_rwkv_ln_w, m_rwkv_ln_b, m_fox_w_in, m_fox_b_f, m_fox_w_out, m_final_norm, v_ffn1_norm, v_ffn1_w_gu, v_ffn1_w_down, v_mix_norm, v_ffn2_norm, v_ffn2_w_gu, v_ffn2_w_down, v_ple_norm, v_ple_w_gate, v_ple_w_proj, v_even_w_in, v_even_w_out, v_swa_sinks, v_rwkv_mu, v_rwkv_w0, v_rwkv_w2, v_rwkv_a0, v_rwkv_a2, v_rwkv_g2, v_rwkv_k_k, v_rwkv_k_a, v_rwkv_r_k, v_rwkv_ln_w, v_rwkv_ln_b, v_fox_w_in, v_fox_b_f, v_fox_w_out, v_final_norm):
    given = dict(locals())
    w = {n: given[n] for n in _WEIGHTS}
    m = {n: given["m_" + n] for n in _WEIGHTS}
    v = {n: given["v_" + n] for n in _WEIGHTS}
    small_shapes = [w[n].shape for n in _REPLICATED]
    piece = lambda d, pc: d[pc[0]][pc[1]]
    shapes = [[piece(w, pc).shape for pc in pieces] for pieces in _PIECES]
    gshapes = [[piece(w, pc).shape for pc in pieces] for pieces in _GRAD_PIECES]
    w_send = [_bundle(lambda pc: piece(w, pc), pieces, BF16) for pieces in _PIECES]

    W = _layer_weights(_unshard(all_gather(w_send[0], "weights_all_gather"), _PIECES[0], shapes[0]))
    for i in range(2):
        for n in ("ffn1_norm", "mix_norm", "ffn2_norm", "ple_norm"):
            W[f"{n}{i}"] = w[n][i].reshape(1, -1)
    for n in ("swa_sinks", "rwkv_mu", "rwkv_w0", "rwkv_a0", "rwkv_k_k", "rwkv_k_a", "rwkv_r_k", "rwkv_ln_w",
              "rwkv_ln_b", "final_norm"):
        W[n] = w[n].reshape(1, -1)
    n_f = fox_b_f.shape[1]
    W["fox_b_f"] = jnp.pad(fox_b_f.reshape(1, n_f), ((0, 0), (0, 128 - n_f)))
    n_fox = fox_w_in.shape[2] * N_DEV

    def layer1_weights(gathered):
        return _layer_weights(_unshard(gathered, _PIECES[1], shapes[1]))

    def early_grads(G):
        G = dict(G, fox_w_in=G["fox_w_in"][:, :n_fox])
        return _to_shards(G, _GRAD_PIECES[2], gshapes[2])

    def mixer_grads(G):
        return _to_shards(G, _GRAD_PIECES[1], gshapes[1])

    loss_row, dx, G, parts_early, parts_mixer = device_step(x[0], p[:, 0], loss_target[0], W, w_send[1],
                                                            layer1_weights, early_grads, mixer_grads)

    parts = [all_to_all(_to_shards(G, _GRAD_PIECES[0], gshapes[0]), "grads_all_to_all"), parts_mixer, parts_early]
    out_g, out_d, out_m, out_v = {}, {}, {}, {}
    rows2d = lambda a, lead: a.reshape(a.shape[:lead] + (-1, a.shape[-1]))
    for li, pieces in enumerate(_GRAD_PIECES):
        wmv = [_bundle(lambda pc, d=d: piece(d, pc), pieces, F32) for d in (w, m, v)]
        res = [adamw(*[rows2d(b[ki], 0) for b in wmv], rows2d(parts[li][ki], 1), f"adamw_{kind}{li}")
               for ki, kind in enumerate(_kinds_of(pieces))]
        for oi, out in enumerate((out_g, out_d, out_m, out_v)):
            for pc, a in _unbundle([r[oi] for r in res], pieces, gshapes[li]).items():
                out.setdefault(pc[0], {})[pc[1]] = a
    for out in (out_g, out_d, out_m, out_v):
        for n in _SHARDED:
            out[n] = jnp.stack([out[n][i] for i in sorted(out[n])])

    gsmall = {}
    for n in ("ffn1_norm", "mix_norm", "ffn2_norm", "ple_norm"):
        gsmall[n] = jnp.concatenate([G[f"{n}0"], G[f"{n}1"]], axis=0)
    for n in ("swa_sinks", "rwkv_mu", "rwkv_w0", "rwkv_a0", "rwkv_k_k", "rwkv_k_a", "rwkv_r_k", "rwkv_ln_w",
              "rwkv_ln_b", "final_norm"):
        gsmall[n] = G[n]
    gsmall["fox_b_f"] = G["fox_b_f"][:, :n_f]
    small = _pack_small([gsmall[n] for n in _REPLICATED] + [loss_row[:, :1]])
    small_parts = all_gather([small], "small_all_gather")[0]
    pad1 = lambda vals: _pack_small(vals + [jnp.zeros((1, 1), F32)])
    gs, ds, nms, nvs = adamw(pad1([w[n] for n in _REPLICATED]), pad1([m[n] for n in _REPLICATED]),
                             pad1([v[n] for n in _REPLICATED]), small_parts, "adamw_replicated")
    out_g.update(zip(_REPLICATED, _unpack_small(gs, small_shapes)))
    out_d.update(zip(_REPLICATED, _unpack_small(ds, small_shapes)))
    out_m.update(zip(_REPLICATED, _unpack_small(nms, small_shapes)))
    out_v.update(zip(_REPLICATED, _unpack_small(nvs, small_shapes)))
    n_small = sum(math.prod(s) for s in small_shapes)
    loss = gs[0, n_small]

    return (loss, dx[None], *[out_g[n] for n in _WEIGHTS], *[out_d[n] for n in _WEIGHTS],
            *[out_m[n] for n in _WEIGHTS], *[out_v[n] for n in _WEIGHTS])
```

```python
import functools
import math

import numpy as np
import jax
import jax.numpy as jnp
from jax import lax
from jax.experimental import pallas as pl
from jax.experimental.pallas import tpu as pltpu

F32 = jnp.float32
BF16 = jnp.bfloat16

D_MODEL = 1024
HEAD_DIM = 64
BLOCK = 128
SWA_HEADS = 8
SWA_KV_HEADS = 2
SWA_GROUP = 4
RWKV_HEADS = 8
RWKV_DIM = 512
FOX_HEADS = 16
FOX_DIM = 1024
D_FF = 2816
NORM_EPS = 1e-6
GN_EPS = 64e-5
L2_EPS = 1e-12
SWA_Q = 512
SWA_KV = 128
SWA_COLS = 768
FOX_IN_PAD = 3200
N_DEV = 8
ADAM_LR = 0.001
ADAM_B1 = 0.9
ADAM_B2 = 0.999
ADAM_EPS = 1e-08
ADAM_WD = 0.01
ADAM_STEP = 10

V7X_VMEM_LIMIT = 56 * 1024 * 1024
FOX_TILE = 512
CUMSUM_BLOCK = 256
SCAN_GROUP = 8
SCAN_CHUNK = 32

_NN = (((1,), (0,)), ((), ()))
_NT = (((1,), (1,)), ((), ()))
_TN = (((0,), (0,)), ((), ()))
_DIMS = {"nn": _NN, "nt": _NT, "tn": _TN}


def _pick(n, target, mult=128):
    best = None
    for t in range(mult, min(n, target) + 1, mult):
        if n % t == 0:
            best = t
    return best or n


def _cparams(sem):
    return pltpu.CompilerParams(dimension_semantics=sem, vmem_limit_bytes=V7X_VMEM_LIMIT)


def _dot(a, b, dims):
    return lax.dot_general(a.astype(BF16), b.astype(BF16), dims, preferred_element_type=F32)


@jax.custom_vjp
def bdot(a, b):
    return _dot(a, b, _NN)


def _bdot_fwd(a, b):
    return _dot(a, b, _NN), (a, b)


def _bdot_bwd(res, g):
    a, b = res
    return _dot(g, b, _NT), _dot(a, g, _TN)


bdot.defvjp(_bdot_fwd, _bdot_bwd)


@jax.custom_vjp
def bdot_nt(a, b):
    return _dot(a, b, _NT)


def _bdot_nt_fwd(a, b):
    return _dot(a, b, _NT), (a, b)


def _bdot_nt_bwd(res, g):
    a, b = res
    return _dot(g, b, _NN), _dot(g, a, _TN)


bdot_nt.defvjp(_bdot_nt_fwd, _bdot_nt_bwd)


@jax.custom_vjp
def _segsum(x, bd):
    return _dot2(x, bd.astype(BF16))


def _segsum_fwd(x, bd):
    return _segsum(x, bd), bd


def _segsum_bwd(bd, g):
    return _dot2(g, bd.astype(BF16)), jnp.zeros_like(bd)


_segsum.defvjp(_segsum_fwd, _segsum_bwd)


def _sigmoid(x):
    return 1.0 / (1.0 + jnp.exp(-x))


def _sigmoid_tanh(x):
    return 0.5 * jnp.tanh(0.5 * x) + 0.5


def _softplus(x):
    return jnp.maximum(x, 0.0) + jnp.log(1.0 + jnp.exp(-jnp.abs(x)))


def matmul(a, b, mode, name, out_dtype=F32, scale=1.0, res=None, tm=1024, tn=1408, tk=1024):
    if mode == "nn":
        (M, K), (K2, N) = a.shape, b.shape
    elif mode == "nt":
        (M, K), (N, K2) = a.shape, b.shape
    else:
        (K, M), (K2, N) = a.shape, b.shape
    assert K == K2, (a.shape, b.shape, mode)
    tm, tn, tk = _pick(M, tm), _pick(N, tn), _pick(K, tk)
    nk = K // tk
    has_res = res is not None

    def body(*refs):
        if has_res:
            a_ref, b_ref, r_ref, o_ref, acc = refs
        else:
            a_ref, b_ref, o_ref, acc = refs
        kk = pl.program_id(2)

        @pl.when(kk == 0)
        def _():
            acc[...] = jnp.zeros_like(acc)

        acc[...] += _dot(a_ref[...], b_ref[...], _DIMS[mode])

        @pl.when(kk == nk - 1)
        def _():
            v = acc[...]
            if scale != 1.0:
                v = v * scale
            if has_res:
                v = v + r_ref[...].astype(F32)
            o_ref[...] = v.astype(out_dtype)

    if mode == "tn":
        a_spec = pl.BlockSpec((tk, tm), lambda i, j, k: (k, i))
    else:
        a_spec = pl.BlockSpec((tm, tk), lambda i, j, k: (i, k))
    if mode == "nt":
        b_spec = pl.BlockSpec((tn, tk), lambda i, j, k: (j, k))
    else:
        b_spec = pl.BlockSpec((tk, tn), lambda i, j, k: (k, j))
    o_spec = pl.BlockSpec((tm, tn), lambda i, j, k: (i, j))
    in_specs = [a_spec, b_spec] + ([o_spec] if has_res else [])
    args = (a, b) + ((res,) if has_res else ())
    return pl.pallas_call(
        body,
        grid=(M // tm, N // tn, nk),
        in_specs=in_specs,
        out_specs=o_spec,
        out_shape=jax.ShapeDtypeStruct((M, N), out_dtype),
        scratch_shapes=[pltpu.VMEM((tm, tn), F32)],
        compiler_params=_cparams(("parallel", "parallel", "arbitrary")),
        name=name,
    )(*args)


def _row_spec(r, tm):
    if isinstance(r, tuple):
        arr, width, blk = r
        return arr, pl.BlockSpec((tm, width), lambda i, blk=blk: (i, blk))
    return r, pl.BlockSpec((tm, r.shape[1]), lambda i: (i, 0))


def _whole_spec(p):
    return pl.BlockSpec(p.shape, lambda i: (0,) * p.ndim)


def rowwise(fn, rows, params, outs, name, tm=256):
    arrs, specs = zip(*[_row_spec(r, tm) for r in rows])
    S = arrs[0].shape[0]
    tm = min(tm, S)
    arrs, specs = zip(*[_row_spec(r, tm) for r in rows])
    n_in = len(rows) + len(params)

    def body(*refs):
        res = fn(*[r[...] for r in refs[:n_in]])
        for o_ref, v in zip(refs[n_in:], res):
            o_ref[...] = v.astype(o_ref.dtype)

    return pl.pallas_call(
        body,
        grid=(S // tm,),
        in_specs=list(specs) + [_whole_spec(p) for p in params],
        out_specs=[pl.BlockSpec((tm, c), lambda i: (i, 0)) for c, _ in outs],
        out_shape=[jax.ShapeDtypeStruct((S, c), dt) for c, dt in outs],
        compiler_params=_cparams(("parallel",)),
        name=name,
    )(*arrs, *params)


def rowwise_vjp(fn, rows, params, cots, name, need=None, row_dtype=F32, consts=(), tm=256):
    nr, npar, nc, nk = len(rows), len(params), len(cots), len(consts)
    need = [True] * nr if need is None else need
    arrs, _ = zip(*[_row_spec(r, tm) for r in rows])
    S = arrs[0].shape[0]
    tm = min(tm, S)
    arrs, specs = zip(*[_row_spec(r, tm) for r in rows])
    carrs, cspecs = zip(*[_row_spec(c, tm) for c in cots])
    widths = [s.block_shape[1] for s in specs]
    n_in = nr + npar + nk + nc

    def body(*refs):
        i = pl.program_id(0)
        xs = [r[...].astype(F32) for r in refs[:nr]]
        ps = [r[...] for r in refs[nr:nr + npar]]
        ks = [r[...] for r in refs[nr + npar:nr + npar + nk]]
        cs = [r[...].astype(F32) for r in refs[nr + npar + nk:n_in]]
        outs, vjp = jax.vjp(lambda *a: fn(*a, *ks), *xs, *ps)
        grads = vjp(tuple(cs))
        o = n_in
        for j in range(nr):
            if need[j]:
                refs[o][...] = grads[j].astype(refs[o].dtype)
                o += 1
        for j in range(npar):
            g_ref = refs[o + j]

            @pl.when(i == 0)
            def _(g_ref=g_ref):
                g_ref[...] = jnp.zeros_like(g_ref)

            g_ref[...] += grads[nr + j]

    out_specs = [pl.BlockSpec((tm, w), lambda i: (i, 0)) for w, nd in zip(widths, need) if nd]
    out_shape = [jax.ShapeDtypeStruct((S, w), row_dtype) for w, nd in zip(widths, need) if nd]
    out_specs += [_whole_spec(p) for p in params]
    out_shape += [jax.ShapeDtypeStruct(p.shape, F32) for p in params]
    res = pl.pallas_call(
        body,
        grid=(S // tm,),
        in_specs=list(specs) + [_whole_spec(p) for p in params] + [_whole_spec(k) for k in consts] + list(cspecs),
        out_specs=out_specs,
        out_shape=out_shape,
        compiler_params=_cparams(("arbitrary",)),
        name=name,
    )(*arrs, *params, *consts, *carrs)
    nrow = sum(need)
    return list(res[:nrow]), list(res[nrow:])


def _rms(x, g):
    return x * lax.rsqrt(jnp.mean(x * x, axis=-1, keepdims=True) + NORM_EPS) * g


def _f_rms(x, g):
    return (_rms(x, g),)


def _f_rms_res(x, g):
    return _rms(x, g), x


def _f_ple(x, z, pp):
    return (x + _sigmoid(z) * pp,)


def _f_mix(h, sh, mu):
    return (h + (sh - h) * mu,)


def _f_logf(fz, bf):
    return (-_softplus(-(fz + bf)),)


def _f_rwkv_pre(hk, hw, ha, hg, w0, w2, a0, a2, g2, k_k, k_a, bd):
    wlog = -_softplus(-(w0 + bdot(jnp.tanh(hw), w2))) - 0.5
    a = _sigmoid(a0 + bdot(ha, a2))
    g = bdot(_sigmoid(hg), g2)
    kk = hk * k_k
    kk = kk / jnp.maximum(jnp.sqrt(_segsum(kk * kk, bd)), L2_EPS)
    k2 = hk * (1.0 + (a - 1.0) * k_a)
    decay = jnp.exp(-jnp.exp(wlog))
    return decay, k2, kk, kk * a, g


def _f_rwkv_post(y, r, k2, v, g, ln_w, ln_b, r_k, bd):
    mean = _segsum(y, bd) * (1.0 / HEAD_DIM)
    d = y - mean
    var = _segsum(d * d, bd) * (1.0 / HEAD_DIM)
    yn = d * lax.rsqrt(var + GN_EPS) * ln_w + ln_b
    yn = yn + _segsum(r * k2 * r_k, bd) * v
    return (yn * g,)


def loss_head(x, target, gf, tm=256):
    S, D = x.shape
    tm = min(tm, S)

    def f(xt, g, tt):
        err = _rms(xt, g) - tt
        return 0.5 * jnp.sum(err * err) * (1.0 / D)

    def body(x_ref, t_ref, g_ref, dx_ref, dg_ref, l_ref):
        i = pl.program_id(0)
        val, (dx, dg) = jax.value_and_grad(f, argnums=(0, 1))(x_ref[...], g_ref[...], t_ref[...])

        @pl.when(i == 0)
        def _():
            dg_ref[...] = jnp.zeros_like(dg_ref)
            l_ref[...] = jnp.zeros_like(l_ref)

        dx_ref[...] = dx
        dg_ref[...] += dg
        l_ref[...] += jnp.full(l_ref.shape, val, F32)

    row = pl.BlockSpec((tm, D), lambda i: (i, 0))
    vec = pl.BlockSpec((1, D), lambda i: (0, 0))
    return pl.pallas_call(
        body,
        grid=(S // tm,),
        in_specs=[row, row, vec],
        out_specs=[row, vec, pl.BlockSpec((1, 128), lambda i: (0, 0))],
        out_shape=[jax.ShapeDtypeStruct((S, D), F32), jax.ShapeDtypeStruct((1, D), F32),
                   jax.ShapeDtypeStruct((1, 128), F32)],
        compiler_params=_cparams(("arbitrary",)),
        name="loss_head",
    )(x, target, gf)


def _swa_block(q, kp, kc, vp, vc, sink, slope, n):
    k = jnp.concatenate([kp, kc], axis=0)
    v = jnp.concatenate([vp, vc], axis=0)
    rows = q.shape[0]
    logits = bdot_nt(q, k) * (HEAD_DIM ** -0.5)
    qi = lax.broadcasted_iota(jnp.int32, (rows, 2 * BLOCK), 0) & (BLOCK - 1)
    ki = lax.broadcasted_iota(jnp.int32, (rows, 2 * BLOCK), 1)
    dist = qi + BLOCK - ki
    valid = (dist >= 0) & (dist < BLOCK) & ((n - 1) * BLOCK + ki >= 0)
    logits = logits - slope * dist.astype(F32)
    logits = jnp.where(valid, logits, -jnp.inf)
    m = jnp.maximum(jnp.max(logits, axis=-1, keepdims=True), sink)
    pr = jnp.exp(logits - m)
    denom = jnp.sum(pr, axis=-1, keepdims=True) + jnp.exp(sink - m)
    return bdot(pr / denom, v)


def _swa_specs(S):
    nb = S // BLOCK
    q_spec = pl.BlockSpec((None, SWA_GROUP, BLOCK, HEAD_DIM), lambda h, n: (h, 0, n, 0))
    kc_spec = pl.BlockSpec((None, BLOCK, HEAD_DIM), lambda h, n: (h, n, 0))
    kp_spec = pl.BlockSpec((None, BLOCK, HEAD_DIM), lambda h, n: (h, jnp.maximum(n - 1, 0), 0))
    col_spec = pl.BlockSpec((None, SWA_GROUP * BLOCK, 1), lambda h, n: (h, 0, 0))
    return nb, q_spec, kp_spec, kc_spec, col_spec


def swa_fwd(q, k, v, sink_col, slope_col):
    S = q.shape[2]
    nb, q_spec, kp_spec, kc_spec, col_spec = _swa_specs(S)

    def body(q_ref, kp_ref, kc_ref, vp_ref, vc_ref, s_ref, a_ref, o_ref):
        n = pl.program_id(1)
        qq = q_ref[...].reshape(SWA_GROUP * BLOCK, HEAD_DIM)
        out = _swa_block(qq, kp_ref[...], kc_ref[...], vp_ref[...], vc_ref[...], s_ref[...], a_ref[...], n)
        o_ref[...] = out.reshape(SWA_GROUP, BLOCK, HEAD_DIM)

    return pl.pallas_call(
        body,
        grid=(SWA_KV_HEADS, nb),
        in_specs=[q_spec, kp_spec, kc_spec, kp_spec, kc_spec, col_spec, col_spec],
        out_specs=q_spec,
        out_shape=jax.ShapeDtypeStruct(q.shape, F32),
        compiler_params=_cparams(("parallel", "parallel")),
        name="swa_fwd",
    )(q, k, k, v, v, sink_col, slope_col)


def swa_bwd(q, k, v, sink_col, slope_col, dout):
    S = q.shape[2]
    nb, q_spec, kp_spec, kc_spec, col_spec = _swa_specs(S)

    def body(q_ref, kp_ref, kc_ref, vp_ref, vc_ref, s_ref, a_ref, do_ref,
             dq_ref, dkp_ref, dkc_ref, dvp_ref, dvc_ref, ds_ref):
        n = pl.program_id(1)
        qq = q_ref[...].reshape(SWA_GROUP * BLOCK, HEAD_DIM)
        slope = a_ref[...]
        f = lambda a, b, c, d, e, s: _swa_block(a, b, c, d, e, s, slope, n)
        _, vjp = jax.vjp(f, qq, kp_ref[...], kc_ref[...], vp_ref[...], vc_ref[...], s_ref[...])
        dq, dkp, dkc, dvp, dvc, ds = vjp(do_ref[...].reshape(SWA_GROUP * BLOCK, HEAD_DIM))
        dq_ref[...] = dq.reshape(SWA_GROUP, BLOCK, HEAD_DIM)
        dkp_ref[...] = dkp
        dkc_ref[...] = dkc
        dvp_ref[...] = dvp
        dvc_ref[...] = dvc

        @pl.when(n == 0)
        def _():
            ds_ref[...] = jnp.zeros_like(ds_ref)

        ds_ref[...] += ds

    kv_shape = jax.ShapeDtypeStruct(k.shape, F32)
    return pl.pallas_call(
        body,
        grid=(SWA_KV_HEADS, nb),
        in_specs=[q_spec, kp_spec, kc_spec, kp_spec, kc_spec, col_spec, col_spec, q_spec],
        out_specs=[q_spec, kc_spec, kc_spec, kc_spec, kc_spec, col_spec],
        out_shape=[jax.ShapeDtypeStruct(q.shape, F32), kv_shape, kv_shape, kv_shape, kv_shape,
                   jax.ShapeDtypeStruct(sink_col.shape, F32)],
        compiler_params=_cparams(("parallel", "arbitrary")),
        name="swa_bwd",
    )(q, k, k, v, v, sink_col, slope_col, dout)


def _split2(x):
    hi = x.astype(BF16)
    return (x - hi.astype(F32)).astype(BF16), hi


def _dot2_many(xs, m, single=False):
    rows = xs[0].shape[0]
    if single:
        res = jnp.dot(jnp.concatenate([x.astype(BF16) for x in xs], axis=0), m, preferred_element_type=F32)
        return [res[i * rows:(i + 1) * rows] for i in range(len(xs))]
    res = jnp.dot(jnp.concatenate([p for x in xs for p in _split2(x)], axis=0), m, preferred_element_type=F32)
    return [res[(2 * i) * rows:(2 * i + 1) * rows] + res[(2 * i + 1) * rows:(2 * i + 2) * rows]
            for i in range(len(xs))]


def _dot2(x, m):
    return _dot2_many([x], m)[0]


def _seg_sums(xs, bd, single=False):
    w = bd.shape[0]
    halves = _dot2_many([x[:, i:i + w] for x in xs for i in range(0, x.shape[1], w)], bd, single)
    n = xs[0].shape[1] // w
    return [jnp.concatenate(halves[i * n:(i + 1) * n], axis=1) for i in range(len(xs))]


def _seg_sum(x, bd):
    return _seg_sums([x], bd)[0]


def _scan_consts():
    r = np.arange(256)
    bd = (r[:, None] // HEAD_DIM == r[None, :] // HEAD_DIM).astype(np.float32)
    c = np.arange(RWKV_DIM)
    e = (np.arange(HEAD_DIM)[:, None] // SCAN_GROUP == c[None, :] // HEAD_DIM).astype(np.float32)
    diag = (np.arange(HEAD_DIM)[:, None] == c[None, :] % HEAD_DIM).astype(np.float32)
    return jnp.asarray(bd, BF16), jnp.asarray(e, BF16), jnp.asarray(diag, F32)


def _to_colblocks(a):
    S = a.shape[0]
    a = a.reshape(S // SCAN_GROUP, SCAN_GROUP, RWKV_HEADS, HEAD_DIM)
    return a.transpose(0, 3, 2, 1).reshape(S // SCAN_GROUP, HEAD_DIM, RWKV_HEADS * SCAN_GROUP)


def _roll_up(rows):
    return pltpu.roll(rows, rows.shape[0] - 1, 0)


def _scan_pair_rows(aux, base, kk_ref, w_ref, b_ref, k_ref, bd):
    G = SCAN_GROUP
    kk_nx = _roll_up(kk_ref[pl.ds(base, G), :])
    aux[0] = w_ref[pl.ds(base, G), :] * kk_nx
    aux[1], aux[2] = _seg_sums([b_ref[pl.ds(base, G), :] * kk_nx, k_ref[pl.ds(base, G), :] * kk_nx], bd)


def _scan_pair(St, t0, base, col_g, lane_t, aux, kk_ref, w_ref, b_ref, k_ref, bd, e):
    t1 = t0 + 1
    row = lambda ref, t: ref[pl.ds(base + t, 1), :]
    arow = lambda i: aux[i, pl.ds(t0, 1), :]
    u0, m1 = _seg_sums([St * row(kk_ref, t0), St * arow(0)], bd)
    v0, v1 = _dot2_many([jnp.where(lane_t == t0, col_g, 0.0), jnp.where(lane_t == t1, col_g, 0.0)], e)
    u1 = m1 - u0 * arow(1) + v0 * arow(2)
    S0 = St * row(w_ref, t0) - u0 * row(b_ref, t0) + v0 * row(k_ref, t0)
    S1 = S0 * row(w_ref, t1) - u1 * row(b_ref, t1) + v1 * row(k_ref, t1)
    return (S0, S1), (u0, u1), (v0, v1)


def rwkv_scan_fwd(r, w, k, kk, b, vB, gather_srcs):
    S, C = r.shape
    N, G = HEAD_DIM, SCAN_GROUP
    chunk = min(SCAN_CHUNK, S)
    nchunk, ng = S // chunk, chunk // G
    bd, e, diag = _scan_consts()

    nx = len(gather_srcs)

    def body(*refs):
        r_ref, w_ref, k_ref, kk_ref, b_ref, vB_ref, bd_ref, e_ref, dg_ref = refs[:9]
        y_ref, ck_ref = refs[9 + nx:11 + nx]
        S_ref, aux, send_sems, recv_sems, local_sems = refs[11 + 2 * nx:]
        c = pl.program_id(0)
        _exchange_during(c, nchunk, True, refs[9:9 + nx], refs[11 + nx:11 + 2 * nx], send_sems, recv_sems, local_sems)

        @pl.when(c == 0)
        def _():
            S_ref[...] = jnp.zeros_like(S_ref)

        ck_ref[...] = S_ref[...]
        sub = lax.broadcasted_iota(jnp.int32, (G, C), 0)
        lane_t = lax.broadcasted_iota(jnp.int32, (N, N), 1) & (G - 1)

        def group(g, St):
            base = pl.multiple_of(g * G, G)
            vb = vB_ref[g]
            _scan_pair_rows(aux, base, kk_ref, w_ref, b_ref, k_ref, bd_ref[...])
            ys = jnp.zeros((G, C), F32)
            def emit(ys, states, t0):
                steps = (t0, t0 + 1)
                y_bs = _seg_sums([S_t * r_ref[pl.ds(base + tt, 1), :] for S_t, tt in zip(states, steps)], bd_ref[...],
                                 single=True)
                for y_b, tt in zip(y_bs, steps):
                    ys = jnp.where(sub == tt, jnp.sum(y_b * dg_ref[...], axis=0, keepdims=True), ys)
                return ys

            pending = None
            for t0 in range(0, G, 2):
                states, _, _ = _scan_pair(St, t0, base, vb, lane_t, aux, kk_ref, w_ref, b_ref, k_ref, bd_ref[...],
                                          e_ref[...])
                if pending is not None:
                    ys = emit(ys, *pending)
                pending = (states, t0)
                St = states[1]
            y_ref[pl.ds(base, G), :] = emit(ys, *pending)
            return St

        S_ref[...] = lax.fori_loop(0, ng, group, S_ref[...])

    row = pl.BlockSpec((chunk, C), lambda c: (c, 0))
    col = pl.BlockSpec((ng, N, N), lambda c: (c, 0, 0))
    res = pl.pallas_call(
        body,
        grid=(nchunk,),
        in_specs=[row] * 5 + [col, _whole_spec(bd), _whole_spec(e), _whole_spec(diag)] + [_ANY] * nx,
        out_specs=[row, pl.BlockSpec((None, N, C), lambda c: (c, 0, 0))] + [_ANY] * nx,
        out_shape=[jax.ShapeDtypeStruct((S, C), F32), jax.ShapeDtypeStruct((nchunk, N, C), F32)]
        + _exchange_out_shapes(True, gather_srcs),
        scratch_shapes=[pltpu.VMEM((N, C), F32), pltpu.VMEM((3, G, C), F32)] + _exchange_sems(nx),
        compiler_params=_cparams(("arbitrary",)),
        name="rwkv_scan_fwd",
    )(r, w, k, kk, b, vB, bd, e, diag, *gather_srcs)
    return res[0], res[1], list(res[2:])


def rwkv_scan_bwd(r, w, k, kk, b, vB, dyB, ckpt, scatter_srcs):
    S, C = r.shape
    N, G = HEAD_DIM, SCAN_GROUP
    chunk = min(SCAN_CHUNK, S)
    nchunk, ng = S // chunk, chunk // G
    nsteps = nchunk + 1
    bd, e, diag = _scan_consts()
    nx = len(scatter_srcs)

    def body(*refs):
        wf_ref, kf_ref, kkf_ref, bf_ref, vBf_ref, ck_ref = refs[:6]
        r_ref, w_ref, k_ref, kk_ref, b_ref, dyB_ref, bd_ref, e_ref, dg_ref = refs[6:15]
        dr_ref, dw_ref, dk_ref, dkk_ref, db_ref, dv_ref = refs[15 + nx:21 + nx]
        G_ref, sbuf, ubuf, vbuf, aux_f, aux_b, send_sems, recv_sems, local_sems = refs[21 + 2 * nx:]
        c = pl.program_id(0)
        _exchange_during(c, nsteps, False, refs[15:15 + nx], refs[21 + nx:21 + 2 * nx], send_sems, recv_sems,
                         local_sems)

        @pl.when(c == 0)
        def _():
            G_ref[...] = jnp.zeros_like(G_ref)
            sbuf[...] = jnp.zeros_like(sbuf)
            ubuf[...] = jnp.zeros_like(ubuf)
            vbuf[...] = jnp.zeros_like(vbuf)

        sf = c % 2
        sb = 1 - sf
        lane_t = lax.broadcasted_iota(jnp.int32, (N, N), 1) & (G - 1)
        sub = lax.broadcasted_iota(jnp.int32, (G, C), 0)
        colsum = lambda a: jnp.sum(a, axis=0, keepdims=True)

        def group(g, carry):
            St, Gt = carry
            base_f = pl.multiple_of(g * G, G)
            gb = ng - 1 - g
            base_b = pl.multiple_of(gb * G, G)
            vb, dyb = vBf_ref[g], dyB_ref[gb]
            row = lambda ref, t: ref[pl.ds(base_b + t, 1), :]
            _scan_pair_rows(aux_f, base_f, kkf_ref, wf_ref, bf_ref, kf_ref, bd_ref[...])
            b8 = b_ref[pl.ds(base_b, G), :]
            aux_b[0] = _roll_up(w_ref[pl.ds(base_b, G), :]) * b8
            aux_b[1], aux_b[2] = _seg_sums([_roll_up(kk_ref[pl.ds(base_b, G), :]) * b8,
                                            r_ref[pl.ds(base_b, G), :] * b8], bd_ref[...])
            rows = [jnp.zeros((G, C), F32) for _ in range(6)]

            def emit(rows, steps):
                d_vs = _seg_sums([Gt_ * row(k_ref, tt) for tt, Gt_, _, _ in steps], bd_ref[...], single=True)
                for (tt, Gt_, du_b, dy_b), d_vb in zip(steps, d_vs):
                    Sp, Sc = sbuf[sb, base_b + tt], sbuf[sb, base_b + tt + 1]
                    new = (colsum(Sc * dy_b), colsum(Gt_ * Sp), colsum(Gt_ * vbuf[sb, base_b + tt]),
                           colsum(Sp * du_b), -colsum(Gt_ * ubuf[sb, base_b + tt]), colsum(d_vb * dg_ref[...]))
                    rows = [jnp.where(sub == tt, n_, acc) for n_, acc in zip(new, rows)]
                return rows

            pending = None
            for i in range(G // 2):
                t0 = 2 * i
                states, us, vs = _scan_pair(St, t0, base_f, vb, lane_t, aux_f, kkf_ref, wf_ref, bf_ref, kf_ref,
                                            bd_ref[...], e_ref[...])
                for j, S_before in enumerate((St, states[0])):
                    sbuf[sf, base_f + t0 + j] = S_before
                    ubuf[sf, base_f + t0 + j] = us[j]
                    vbuf[sf, base_f + t0 + j] = vs[j]
                St = states[1]
                t0 = G - 2 - 2 * i
                t1 = t0 + 1
                arow = lambda j, t0=t0: aux_b[j, pl.ds(t0, 1), :]
                dy1, dy0 = _dot2_many([jnp.where(lane_t == t1, dyb, 0.0), jnp.where(lane_t == t0, dyb, 0.0)],
                                      e_ref[...])
                G1 = Gt + dy1 * row(r_ref, t1)
                m1, m2 = _seg_sums([G1 * row(b_ref, t1), G1 * arow(0)], bd_ref[...])
                du1 = -m1
                du0 = -(m2 + du1 * arow(1) + dy0 * arow(2))
                G0 = G1 * row(w_ref, t1) + du1 * row(kk_ref, t1) + dy0 * row(r_ref, t0)
                G_next = G0 * row(w_ref, t0) + du0 * row(kk_ref, t0)
                if pending is not None:
                    rows = emit(rows, pending)
                pending = ((t1, G1, du1, dy1), (t0, G0, du0, dy0))
                Gt = G_next
            rows = emit(rows, pending)
            for ref, val in zip((dr_ref, dw_ref, dk_ref, dkk_ref, db_ref, dv_ref), rows):
                ref[pl.ds(base_b, G), :] = val
            return St, Gt

        St, Gt = lax.fori_loop(0, ng, group, (ck_ref[...], G_ref[...]))
        sbuf[sf, chunk] = St
        G_ref[...] = jnp.where(c >= 1, Gt, G_ref[...])

    fwd_chunk = lambda c: jnp.maximum(nchunk - 1 - c, 0)
    bwd_chunk = lambda c: jnp.minimum(nchunk - c, nchunk - 1)
    row_f = pl.BlockSpec((chunk, C), lambda c: (fwd_chunk(c), 0))
    row_b = pl.BlockSpec((chunk, C), lambda c: (bwd_chunk(c), 0))
    col_f = pl.BlockSpec((ng, N, N), lambda c: (fwd_chunk(c), 0, 0))
    col_b = pl.BlockSpec((ng, N, N), lambda c: (bwd_chunk(c), 0, 0))
    rshape = jax.ShapeDtypeStruct((S, C), F32)
    res = pl.pallas_call(
        body,
        grid=(nsteps,),
        in_specs=[row_f] * 4 + [col_f, pl.BlockSpec((None, N, C), lambda c: (fwd_chunk(c), 0, 0))]
        + [row_b] * 5 + [col_b, _whole_spec(bd), _whole_spec(e), _whole_spec(diag)] + [_ANY] * nx,
        out_specs=[row_b] * 6 + [_ANY] * nx,
        out_shape=[rshape] * 6 + _exchange_out_shapes(False, scatter_srcs),
        scratch_shapes=[pltpu.VMEM((N, C), F32), pltpu.VMEM((2, chunk + 1, N, C), F32),
                        pltpu.VMEM((2, chunk, N, C), F32), pltpu.VMEM((2, chunk, N, C), F32),
                        pltpu.VMEM((3, G, C), F32), pltpu.VMEM((3, G, C), F32)] + _exchange_sems(nx),
        compiler_params=_cparams(("arbitrary",)),
        name="rwkv_scan_bwd",
    )(w, k, kk, b, vB, ckpt, r, w, k, kk, b, dyB, bd, e, diag, *scatter_srcs)
    return tuple(res[:6]) + (list(res[6:]),)


def seq_cumsum(x, reverse, name):
    S, C = x.shape
    tb = min(CUMSUM_BLOCK, S)
    nb = S // tb

    def body(x_ref, o_ref, carry):
        i = pl.program_id(0)

        @pl.when(i == 0)
        def _():
            carry[...] = jnp.zeros_like(carry)

        ri = lax.broadcasted_iota(jnp.int32, (tb, tb), 0)
        ci = lax.broadcasted_iota(jnp.int32, (tb, tb), 1)
        tri = jnp.where((ci >= ri) if reverse else (ci <= ri), 1.0, 0.0).astype(F32)
        xb = x_ref[...]
        out = jnp.dot(tri, xb, precision=lax.Precision.HIGHEST, preferred_element_type=F32) + carry[...]
        o_ref[...] = out
        carry[...] = carry[...] + jnp.sum(xb, axis=0, keepdims=True)

    idx = (lambda i: (nb - 1 - i, 0)) if reverse else (lambda i: (i, 0))
    return pl.pallas_call(
        body,
        grid=(nb,),
        in_specs=[pl.BlockSpec((tb, C), idx)],
        out_specs=pl.BlockSpec((tb, C), idx),
        out_shape=jax.ShapeDtypeStruct((S, C), F32),
        scratch_shapes=[pltpu.VMEM((1, C), F32)],
        compiler_params=_cparams(("arbitrary",)),
        name=name,
    )(x)


def _fox_logits(q, k, cq, ck, diagonal):
    s = _dot(q, k, _NT) * (HEAD_DIM ** -0.5) + cq - ck
    if not diagonal:
        return s
    row = lax.broadcasted_iota(jnp.int32, s.shape, 0)
    col = lax.broadcasted_iota(jnp.int32, s.shape, 1)
    return jnp.where(col <= row, s, -jnp.inf)


def _fox_tiles(n, by_query):
    pairs = [(i, j) for i in range(n) for j in range(i + 1)] if by_query else \
            [(i, j) for j in range(n) for i in range(j, n)]
    return (jnp.asarray(np.array([p[0] for p in pairs], np.int32)),
            jnp.asarray(np.array([p[1] for p in pairs], np.int32)))


def _fox_specs(t, Dh):
    qs = pl.BlockSpec((None, t, Dh), lambda h, s, qt, kt: (h, qt[s], 0))
    ks = pl.BlockSpec((None, t, Dh), lambda h, s, qt, kt: (h, kt[s], 0))
    cqs = pl.BlockSpec((None, t, 1), lambda h, s, qt, kt: (h, qt[s], 0))
    cks = pl.BlockSpec((None, 1, t), lambda h, s, qt, kt: (h, 0, kt[s]))
    return qs, ks, cqs, cks


def _fox_call(body, tiles, Hh, in_specs, out_specs, out_shape, scratch, name, args):
    spec = pltpu.PrefetchScalarGridSpec(num_scalar_prefetch=2, grid=(Hh, tiles[0].shape[0]), in_specs=in_specs,
                                        out_specs=out_specs, scratch_shapes=scratch)
    return pl.pallas_call(body, grid_spec=spec, out_shape=out_shape,
                          compiler_params=_cparams(("parallel", "arbitrary")), name=name)(*tiles, *args)


def fox_fwd(q, k, v, c_col, c_row):
    Hh, S, Dh = q.shape
    tq = tk = min(FOX_TILE, S)

    def body(qt_ref, kt_ref, q_ref, k_ref, v_ref, cq_ref, ck_ref, o_ref, lse_ref, m_s, l_s, acc_s):
        qi, ki = qt_ref[pl.program_id(1)], kt_ref[pl.program_id(1)]

        @pl.when(ki == 0)
        def _():
            m_s[...] = jnp.full_like(m_s, -jnp.inf)
            l_s[...] = jnp.zeros_like(l_s)
            acc_s[...] = jnp.zeros_like(acc_s)

        def tile(diagonal):
            s = _fox_logits(q_ref[...], k_ref[...], cq_ref[...], ck_ref[...], diagonal)
            m_old = m_s[...]
            m_new = jnp.maximum(m_old, jnp.max(s, axis=-1, keepdims=True))
            alpha = jnp.exp(m_old - m_new)
            p = jnp.exp(s - m_new)
            l_s[...] = alpha * l_s[...] + jnp.sum(p, axis=-1, keepdims=True)
            acc_s[...] = alpha * acc_s[...] + _dot(p, v_ref[...], _NN)
            m_s[...] = m_new

        @pl.when(ki != qi)
        def _():
            tile(False)

        @pl.when(ki == qi)
        def _():
            tile(True)
            o_ref[...] = acc_s[...] / l_s[...]
            lse_ref[...] = m_s[...] + jnp.log(l_s[...])

    qs, ks, cqs, cks = _fox_specs(tq, Dh)
    return _fox_call(
        body, _fox_tiles(S // tq, True), Hh, [qs, ks, ks, cqs, cks], [qs, cqs],
        [jax.ShapeDtypeStruct((Hh, S, Dh), F32), jax.ShapeDtypeStruct((Hh, S, 1), F32)],
        [pltpu.VMEM((tq, 1), F32), pltpu.VMEM((tq, 1), F32), pltpu.VMEM((tq, Dh), F32)],
        "fox_fwd", (q, k, v, c_col, c_row))


def fox_bwd(q, k, v, c_col, c_row, o, lse, do):
    Hh, S, Dh = q.shape
    tq = tk = min(FOX_TILE, S)
    nk = S // tk

    def body(qt_ref, kt_ref, q_ref, k_ref, v_ref, cq_ref, ck_ref, o_ref, lse_ref, do_ref,
             dq_ref, dr_ref, dk_ref, dv_ref, dc_ref, acc_s, row_s):
        step = pl.program_id(1)
        qi, ki = qt_ref[step], kt_ref[step]

        @pl.when(step == 0)
        def _():
            dk_ref[...] = jnp.zeros_like(dk_ref)
            dv_ref[...] = jnp.zeros_like(dv_ref)
            dc_ref[...] = jnp.zeros_like(dc_ref)

        @pl.when(ki == 0)
        def _():
            acc_s[...] = jnp.zeros_like(acc_s)
            row_s[...] = jnp.zeros_like(row_s)

        def tile(diagonal):
            q_t, kb, vb, do_t = q_ref[...], k_ref[...], v_ref[...], do_ref[...]
            s = _fox_logits(q_t, kb, cq_ref[...], ck_ref[...], diagonal)
            p = jnp.exp(s - lse_ref[...])
            delta = jnp.sum(do_t * o_ref[...], axis=-1, keepdims=True)
            ds = p * (_dot(do_t, vb, _NT) - delta)
            acc_s[...] += _dot(ds, kb, _NN)
            row_s[...] += jnp.sum(ds, axis=-1, keepdims=True)
            dk_ref[ki] += _dot(ds, q_t, _TN) * (HEAD_DIM ** -0.5)
            dv_ref[ki] += _dot(p, do_t, _TN)
            dc_ref[ki] += jnp.sum(ds, axis=0, keepdims=True)

        @pl.when(ki != qi)
        def _():
            tile(False)

        @pl.when(ki == qi)
        def _():
            tile(True)
            dq_ref[...] = acc_s[...] * (HEAD_DIM ** -0.5)
            dr_ref[...] = row_s[...]

    qs, ks, cqs, cks = _fox_specs(tq, Dh)
    head = lambda *blk: pl.BlockSpec((None,) + blk, lambda h, s, qt, kt: (h,) + (0,) * len(blk))
    dq, dr, dk, dv, dc = _fox_call(
        body, _fox_tiles(S // tq, True), Hh, [qs, ks, ks, cqs, cks, qs, cqs, qs],
        [qs, cqs, head(nk, tk, Dh), head(nk, tk, Dh), head(nk, 1, tk)],
        [jax.ShapeDtypeStruct((Hh, S, Dh), F32), jax.ShapeDtypeStruct((Hh, S, 1), F32),
         jax.ShapeDtypeStruct((Hh, nk, tk, Dh), F32), jax.ShapeDtypeStruct((Hh, nk, tk, Dh), F32),
         jax.ShapeDtypeStruct((Hh, nk, 1, tk), F32)],
        [pltpu.VMEM((tq, Dh), F32), pltpu.VMEM((tq, 1), F32)],
        "fox_bwd", (q, k, v, c_col, c_row, o, lse, do))
    return dq, dr, dk.reshape(Hh, S, Dh), dv.reshape(Hh, S, Dh), dc.reshape(Hh, 1, S)


def _heads(a, nh):
    S = a.shape[0]
    return a.reshape(S, nh, HEAD_DIM).transpose(1, 0, 2)


def _unheads(a):
    nh, S, _ = a.shape
    return a.transpose(1, 0, 2).reshape(S, nh * HEAD_DIM)


def _shift_down(a):
    return jnp.pad(a[:-1], ((1, 0), (0, 0)))


def _shift_up(a):
    return jnp.pad(a[1:], ((0, 1), (0, 0)))


def _block_diag_ones():
    i = np.arange(RWKV_DIM) // HEAD_DIM
    return jnp.asarray((i[:, None] == i[None, :]).astype(np.float32))


FFN_ROWS = 1024
FFN_COLS = 256


def _ffn_specs(S, F, tm, fc):
    nf = F // fc
    row = pl.BlockSpec((tm, D_MODEL), lambda i, j: (i, 0))
    vec = pl.BlockSpec((1, D_MODEL), lambda i, j: (0, 0))
    wg = pl.BlockSpec((D_MODEL, fc), lambda i, j: (0, j))
    wu = pl.BlockSpec((D_MODEL, fc), lambda i, j: (0, nf + j))
    wd = pl.BlockSpec((fc, D_MODEL), lambda i, j: (j, 0))
    hid = pl.BlockSpec((tm, fc), lambda i, j: (i, j))
    return nf, row, vec, wg, wu, wd, hid


def ffn_fwd(x, g_norm, w_gu, w_down, tag):
    S, F = x.shape[0], w_down.shape[0]
    tm, fc = min(FFN_ROWS, S), FFN_COLS
    nf, row, vec, wg, wu, wd, _ = _ffn_specs(S, F, tm, fc)

    def body(x_ref, g_ref, wg_ref, wu_ref, wd_ref, o_ref, hn_ref, hn_s, acc):
        j = pl.program_id(1)

        @pl.when(j == 0)
        def _():
            hn_s[...] = _rms(x_ref[...], g_ref[...]).astype(BF16)
            hn_ref[...] = hn_s[...]
            acc[...] = jnp.zeros_like(acc)

        g = _dot(hn_s[...], wg_ref[...], _NN)
        u = _dot(hn_s[...], wu_ref[...], _NN)
        acc[...] += _dot(g * _sigmoid_tanh(g) * u, wd_ref[...], _NN)

        @pl.when(j == nf - 1)
        def _():
            o_ref[...] = x_ref[...] + 0.5 * acc[...]

    out, hn = pl.pallas_call(
        body,
        grid=(S // tm, nf),
        in_specs=[row, vec, wg, wu, wd],
        out_specs=[row, row],
        out_shape=[jax.ShapeDtypeStruct((S, D_MODEL), F32), jax.ShapeDtypeStruct((S, D_MODEL), BF16)],
        scratch_shapes=[pltpu.VMEM((tm, D_MODEL), BF16), pltpu.VMEM((tm, D_MODEL), F32)],
        compiler_params=_cparams(("parallel", "arbitrary")),
        name=tag + "_fwd",
    )(x, g_norm, w_gu, w_gu, w_down)
    return out, (x, hn)


def ffn_bwd(dy, saved, g_norm, w_gu, w_down, tag, scatter_srcs=()):
    x, hn = saved
    S, F = x.shape[0], w_down.shape[0]
    tm, fc = min(FFN_ROWS, S), FFN_COLS
    nf, row, vec, wg, wu, wd, hid = _ffn_specs(S, F, tm, fc)
    nx = len(scatter_srcs)

    def body(*refs):
        dy_ref, x_ref, hn_ref, g_ref, wg_ref, wu_ref, wd_ref = refs[:7]
        dx_ref, dgn_ref, a_ref, dg_ref, du_ref = refs[7 + nx:12 + nx]
        dyh_s, dhn = refs[12 + 2 * nx:14 + 2 * nx]
        i, j = pl.program_id(0), pl.program_id(1)
        if nx:
            _exchange_during(i * nf + j, (S // tm) * nf, False, refs[7:7 + nx], refs[12 + nx:12 + 2 * nx],
                             *refs[14 + 2 * nx:])

        @pl.when(j == 0)
        def _():
            dyh_s[...] = (0.5 * dy_ref[...]).astype(BF16)
            dhn[...] = jnp.zeros_like(dhn)

        hn_t = hn_ref[...]
        g = _dot(hn_t, wg_ref[...], _NN)
        u = _dot(hn_t, wu_ref[...], _NN)
        da = _dot(dyh_s[...], wd_ref[...], _NT)
        sig = _sigmoid_tanh(g)
        gs = g * sig
        a_ref[...] = (gs * u).astype(BF16)
        dg = ((da * u) * (sig + gs * (1.0 - sig))).astype(BF16)
        du = (da * gs).astype(BF16)
        dg_ref[...] = dg
        du_ref[...] = du
        dhn[...] += _dot(jnp.concatenate([dg, du], axis=1),
                         jnp.concatenate([wg_ref[...], wu_ref[...]], axis=1), _NT)

        @pl.when(j == nf - 1)
        def _():
            _, vjp_n = jax.vjp(_rms, x_ref[...], g_ref[...])
            dx, dgn = vjp_n(dhn[...])
            dx_ref[...] = dy_ref[...] + dx

            @pl.when(i == 0)
            def _():
                dgn_ref[...] = jnp.zeros_like(dgn_ref)

            dgn_ref[...] += dgn

    hshape = jax.ShapeDtypeStruct((S, F), BF16)
    res = pl.pallas_call(
        body,
        grid=(S // tm, nf),
        in_specs=[row, row, row, vec, wg, wu, wd] + [_ANY] * nx,
        out_specs=[row, vec, hid, hid, hid] + [_ANY] * nx,
        out_shape=[jax.ShapeDtypeStruct((S, D_MODEL), F32), jax.ShapeDtypeStruct((1, D_MODEL), F32),
                   hshape, hshape, hshape] + _exchange_out_shapes(False, scatter_srcs),
        scratch_shapes=[pltpu.VMEM((tm, D_MODEL), BF16), pltpu.VMEM((tm, D_MODEL), F32)]
        + (_exchange_sems(nx) if nx else []),
        compiler_params=_cparams(("arbitrary", "arbitrary")),
        name=tag + "_bwd",
    )(dy, x, hn, g_norm, w_gu, w_gu, w_down, *scatter_srcs)
    dx, dgn, act, dg, du = res[:5]
    d_wdown = matmul(act, dy, "tn", tag + "_dwd", out_dtype=BF16, scale=0.5, tm=1408)
    d_wgu = jnp.concatenate([matmul(hn, dg, "tn", tag + "_dwg", out_dtype=BF16),
                             matmul(hn, du, "tn", tag + "_dwu", out_dtype=BF16)], axis=1)
    return dx, dgn, d_wgu, d_wdown, list(res[5:])


def ple_fwd(x, p_i, g_norm, w_gate, w_proj, tag):
    hn, = rowwise(_f_rms, [x], [g_norm], [(D_MODEL, BF16)], tag + "_rms")
    z = matmul(hn, w_gate, "nn", tag + "_gate")
    pp = matmul(p_i, w_proj, "nn", tag + "_proj")
    out, = rowwise(_f_ple, [x, z, pp], [], [(D_MODEL, F32)], tag + "_mix")
    return out, (x, hn, z, pp)


def ple_bwd(dy, saved, p_i, g_norm, w_gate, tag):
    x, hn, z, pp = saved
    (dz, dpp), _ = rowwise_vjp(_f_ple, [x, z, pp], [], [dy], tag + "_dmix", need=[False, True, True],
                               row_dtype=BF16)
    d_wproj = matmul(p_i, dpp, "tn", tag + "_dwp", out_dtype=BF16)
    d_wgate = matmul(hn, dz, "tn", tag + "_dwg", out_dtype=BF16)
    dhn = matmul(dz, w_gate, "nt", tag + "_dhn")
    (dx,), (dgn,) = rowwise_vjp(_f_rms_res, [x], [g_norm], [dhn, dy], tag + "_drms")
    return dx, dgn, d_wgate, d_wproj


def _swa_consts(sinks):
    slopes = np.asarray([2.0 ** (-(i + 1)) for i in range(SWA_HEADS)], np.float32)
    slope_col = jnp.asarray(np.repeat(slopes, BLOCK).reshape(SWA_KV_HEADS, SWA_GROUP * BLOCK, 1))
    sink_col = jnp.repeat(sinks.reshape(SWA_HEADS), BLOCK).reshape(SWA_KV_HEADS, SWA_GROUP * BLOCK, 1)
    return sink_col, slope_col


def even_mix_fwd(x, W, gather_src, later_weights):
    S = x.shape[0]
    hn, = rowwise(_f_rms, [x], [W["mix_norm0"]], [(D_MODEL, BF16)], "emix_rms")
    proj = matmul(hn, W["even_w_in"], "nn", "emix_in")
    qa = _heads(proj[:, :SWA_Q], SWA_HEADS).reshape(SWA_KV_HEADS, SWA_GROUP, S, HEAD_DIM)
    ka = _heads(proj[:, SWA_Q:SWA_Q + SWA_KV], SWA_KV_HEADS)
    va = _heads(proj[:, SWA_Q + SWA_KV:SWA_COLS], SWA_KV_HEADS)
    sink_col, slope_col = _swa_consts(W["swa_sinks"])
    ya = swa_fwd(qa, ka, va, sink_col, slope_col)
    ya = _unheads(ya.reshape(SWA_HEADS, S, HEAD_DIM))
    hb = proj[:, SWA_COLS:]
    h, = rowwise(_f_mix, [hb, _shift_down(hb)], [W["rwkv_mu"]], [(hb.shape[1], F32)], "rwkv_shift")
    hr, hk, hv = h[:, :512], h[:, 512:1024], h[:, 1024:1536]
    hw, ha, hg = h[:, 1536:1600], h[:, 1600:1664], h[:, 1664:1792]
    bd = _block_diag_ones()
    pre_params = [W["rwkv_w0"], W["rwkv_w2"], W["rwkv_a0"], W["rwkv_a2"], W["rwkv_g2"], W["rwkv_k_k"],
                  W["rwkv_k_a"]]
    decay, k2, kk, b, g = rowwise(_f_rwkv_pre, [hk, hw, ha, hg], pre_params + [bd],
                                  [(RWKV_DIM, F32)] * 5, "rwkv_pre")
    vT = _to_colblocks(hv)
    y, ckpt, gathered = rwkv_scan_fwd(hr, decay, k2, kk, b, vT, gather_src)
    late = later_weights(gathered)
    post_params = [W["rwkv_ln_w"], W["rwkv_ln_b"], W["rwkv_r_k"]]
    yb, = rowwise(_f_rwkv_post, [y, hr, k2, hv, g], post_params + [bd], [(RWKV_DIM, F32)], "rwkv_post")
    cat = jnp.concatenate([ya, yb], axis=1).astype(BF16)
    out = matmul(cat, late["even_w_out"], "nn", "emix_out", res=x)
    saved = (x, hn, qa, ka, va, sink_col, slope_col, hb, hr, hk, hv, hw, ha, hg, decay, k2, kk, b, g, vT,
             ckpt, y, cat)
    return out, saved, late


def even_mix_bwd(dy, saved, W, scatter_src):
    (x, hn, qa, ka, va, sink_col, slope_col, hb, hr, hk, hv, hw, ha, hg, decay, k2, kk, b, g, vT, ckpt, y,
     cat) = saved
    S = x.shape[0]
    grads = {}
    dcat = matmul(dy, W["even_w_out"], "nt", "emix_dcat")
    grads["even_w_out"] = matmul(cat, dy, "tn", "emix_dwout", out_dtype=BF16)
    dya, dyb = dcat[:, :SWA_Q], dcat[:, SWA_Q:]
    dya_h = _heads(dya, SWA_HEADS).reshape(SWA_KV_HEADS, SWA_GROUP, S, HEAD_DIM)
    dqa, dkp, dkc, dvp, dvc, dsink = swa_bwd(qa, ka, va, sink_col, slope_col, dya_h)
    shift_blk = lambda a: jnp.pad(a[:, BLOCK:], ((0, 0), (0, BLOCK), (0, 0)))
    dka = dkc + shift_blk(dkp)
    dva = dvc + shift_blk(dvp)
    grads["swa_sinks"] = dsink.reshape(SWA_HEADS, BLOCK).sum(axis=1).reshape(1, SWA_HEADS)
    dqa = _unheads(dqa.reshape(SWA_HEADS, S, HEAD_DIM))
    dka, dva = _unheads(dka), _unheads(dva)
    bd = _block_diag_ones()
    post_params = [W["rwkv_ln_w"], W["rwkv_ln_b"], W["rwkv_r_k"]]
    (d_y, d_r1, d_k2a, d_v1, d_g), (d_lnw, d_lnb, d_rk) = rowwise_vjp(
        _f_rwkv_post, [y, hr, k2, hv, g], post_params, [dyb], "rwkv_dpost", consts=[bd], tm=128)
    grads["rwkv_ln_w"], grads["rwkv_ln_b"], grads["rwkv_r_k"] = d_lnw, d_lnb, d_rk
    d_r2, d_w, d_k2b, d_kk, d_b, d_v2, exchanged = rwkv_scan_bwd(hr, decay, k2, kk, b, vT, _to_colblocks(d_y), ckpt,
                                                                  scatter_src)
    pre_params = [W["rwkv_w0"], W["rwkv_w2"], W["rwkv_a0"], W["rwkv_a2"], W["rwkv_g2"], W["rwkv_k_k"],
                  W["rwkv_k_a"]]
    (d_hk, d_hw, d_ha, d_hg), dpre = rowwise_vjp(
        _f_rwkv_pre, [hk, hw, ha, hg], pre_params, [d_w, d_k2a + d_k2b, d_kk, d_b, d_g], "rwkv_dpre",
        consts=[bd], tm=128)
    for nm, gval in zip(["rwkv_w0", "rwkv_w2", "rwkv_a0", "rwkv_a2", "rwkv_g2", "rwkv_k_k", "rwkv_k_a"], dpre):
        grads[nm] = gval
    d_h = jnp.concatenate([d_r1 + d_r2, d_hk, d_v1 + d_v2, d_hw, d_ha, d_hg], axis=1)
    (d_hb, d_sh), (d_mu,) = rowwise_vjp(_f_mix, [hb, _shift_down(hb)], [W["rwkv_mu"]], [d_h], "rwkv_dshift")
    grads["rwkv_mu"] = d_mu
    d_hb = d_hb + _shift_up(d_sh)
    dproj = jnp.concatenate([dqa, dka, dva, d_hb], axis=1).astype(BF16)
    grads["even_w_in"] = matmul(hn, dproj, "tn", "emix_dwin", out_dtype=BF16)
    dhn = matmul(dproj, W["even_w_in"], "nt", "emix_dhn")
    (dx,), (dgn,) = rowwise_vjp(_f_rms_res, [x], [W["mix_norm0"]], [dhn, dy], "emix_drms")
    grads["mix_norm0"] = dgn
    return dx, grads, exchanged


def odd_mix_fwd(x, W):
    S = x.shape[0]
    hn, = rowwise(_f_rms, [x], [W["mix_norm1"]], [(D_MODEL, BF16)], "omix_rms")
    qkv = matmul(hn, W["fox_w_in"][:, :3 * FOX_DIM], "nn", "omix_in", out_dtype=BF16)
    fz = matmul(hn, W["fox_w_in"][:, 3 * FOX_DIM:], "nn", "omix_gate")
    q = _heads(qkv[:, :FOX_DIM], FOX_HEADS)
    k = _heads(qkv[:, FOX_DIM:2 * FOX_DIM], FOX_HEADS)
    v = _heads(qkv[:, 2 * FOX_DIM:], FOX_HEADS)
    logf, = rowwise(_f_logf, [fz], [W["fox_b_f"]], [(128, F32)], "fox_logf")
    c = seq_cumsum(logf, False, "fox_cumsum")[:, :FOX_HEADS]
    c_col = c.T.reshape(FOX_HEADS, S, 1)
    c_row = c.T.reshape(FOX_HEADS, 1, S)
    o, lse = fox_fwd(q, k, v, c_col, c_row)
    yc = _unheads(o).astype(BF16)
    out = matmul(yc, W["fox_w_out"], "nn", "omix_out", res=x)
    return out, (x, hn, q, k, v, fz, c_col, c_row, o, lse, yc)


def odd_mix_bwd(dy, saved, W):
    x, hn, q, k, v, fz, c_col, c_row, o, lse, yc = saved
    S = x.shape[0]
    grads = {}
    dyc = matmul(dy, W["fox_w_out"], "nt", "omix_dyc")
    grads["fox_w_out"] = matmul(yc, dy, "tn", "omix_dwout", out_dtype=BF16)
    do = _heads(dyc, FOX_HEADS)
    dq, drow, dk, dv, dcol = fox_bwd(q, k, v, c_col, c_row, o, lse, do)
    dc = (drow.reshape(FOX_HEADS, S) - dcol.reshape(FOX_HEADS, S)).T
    dc = jnp.pad(dc, ((0, 0), (0, 128 - FOX_HEADS)))
    dlogf = seq_cumsum(dc, True, "fox_rcumsum")
    (dfz,), (dbf,) = rowwise_vjp(_f_logf, [fz], [W["fox_b_f"]], [dlogf], "fox_dlogf")
    grads["fox_b_f"] = dbf
    dproj = jnp.concatenate([_unheads(dq), _unheads(dk), _unheads(dv), dfz], axis=1).astype(BF16)
    grads["fox_w_in"] = matmul(hn, dproj, "tn", "omix_dwin", out_dtype=BF16)
    dhn = matmul(dproj, W["fox_w_in"], "nt", "omix_dhn")
    (dx,), (dgn,) = rowwise_vjp(_f_rms_res, [x], [W["mix_norm1"]], [dhn, dy], "omix_drms")
    grads["mix_norm1"] = dgn
    return dx, grads


def device_step(x, p, target, W, gather_src, layer1_weights, layer1_grads, mixer_grads):
    W = dict(W)
    saved = []
    h = x
    for i in range(2):
        h, s1 = ffn_fwd(h, W[f"ffn1_norm{i}"], W[f"ffn1_w_gu{i}"], W[f"ffn1_w_down{i}"], f"ffn1_{i}")
        if i == 0:
            h, s2, late = even_mix_fwd(h, W, gather_src, layer1_weights)
            W.update(late)
        else:
            h, s2 = odd_mix_fwd(h, W)
        h, s3 = ffn_fwd(h, W[f"ffn2_norm{i}"], W[f"ffn2_w_gu{i}"], W[f"ffn2_w_down{i}"], f"ffn2_{i}")
        h, s4 = ple_fwd(h, p[i], W[f"ple_norm{i}"], W[f"ple_w_gate{i}"], W[f"ple_w_proj{i}"], f"ple_{i}")
        saved.append((s1, s2, s3, s4))
    dh, d_final, loss = loss_head(h, target, W["final_norm"])
    G = {"final_norm": d_final}
    for i in (1, 0):
        s1, s2, s3, s4 = saved[i]
        dh, G[f"ple_norm{i}"], G[f"ple_w_gate{i}"], G[f"ple_w_proj{i}"] = ple_bwd(
            dh, s4, p[i], W[f"ple_norm{i}"], W[f"ple_w_gate{i}"], f"ple_{i}")
        dh, G[f"ffn2_norm{i}"], G[f"ffn2_w_gu{i}"], G[f"ffn2_w_down{i}"], _ = ffn_bwd(
            dh, s3, W[f"ffn2_norm{i}"], W[f"ffn2_w_gu{i}"], W[f"ffn2_w_down{i}"], f"ffn2_{i}")
        if i == 0:
            dh, gm, exchanged = even_mix_bwd(dh, s2, W, layer1_grads(G))
        else:
            dh, gm = odd_mix_bwd(dh, s2, W)
        G.update(gm)
        dh, G[f"ffn1_norm{i}"], G[f"ffn1_w_gu{i}"], G[f"ffn1_w_down{i}"], exchanged_mid = ffn_bwd(
            dh, s1, W[f"ffn1_norm{i}"], W[f"ffn1_w_gu{i}"], W[f"ffn1_w_down{i}"], f"ffn1_{i}",
            scatter_srcs=mixer_grads(G) if i == 0 else ())
    return loss, dh, G, exchanged, exchanged_mid


_MESH = pl.DeviceIdType.MESH
_ANY = pl.BlockSpec(memory_space=pl.ANY)


def _exchange_sems(n):
    return [pltpu.SemaphoreType.DMA((7 * n,)), pltpu.SemaphoreType.DMA((7 * n,)), pltpu.SemaphoreType.DMA((n,))]


def all_gather(xs, name):
    n = len(xs)

    def body(*refs):
        x_refs, out_refs = refs[:n], refs[n:2 * n]
        send_sems, recv_sems, local_sems = refs[2 * n:]
        x_, y_, c_ = lax.axis_index("x"), lax.axis_index("y"), lax.axis_index("c")
        me, sibling = (x_, y_, c_), (x_, y_, 1 - c_)
        chips = [(1 - x_, y_), (x_, 1 - y_), (1 - x_, 1 - y_)]

        def copy(b, k, block, to, from_input=False):
            slot = out_refs[b].at[4 * block[0] + 2 * block[1] + block[2]]
            return pltpu.make_async_remote_copy(
                src_ref=x_refs[b] if from_input else slot, dst_ref=slot,
                send_sem=send_sems.at[7 * b + k], recv_sem=recv_sems.at[7 * b + k], device_id=to,
                device_id_type=_MESH)

        bufs = range(n)
        mine = [pltpu.make_async_copy(x_refs[b], out_refs[b].at[4 * x_ + 2 * y_ + c_], local_sems.at[b]) for b in bufs]
        first = [copy(b, 0, me, sibling, True) for b in bufs]
        first += [copy(b, 1 + j, me, (*chip, c_), True) for j, chip in enumerate(chips) for b in bufs]
        for cp in mine + first:
            cp.start()
        passed = []
        for j, chip in enumerate(chips):
            for b in bufs:
                copy(b, 1 + j, (*chip, c_), me).wait_recv()
                passed.append(copy(b, 4 + j, (*chip, c_), sibling))
                passed[-1].start()
        for b in bufs:
            copy(b, 0, sibling, me).wait_recv()
            for j, chip in enumerate(chips):
                copy(b, 4 + j, (*chip, 1 - c_), me).wait_recv()
        for cp in first + passed:
            cp.wait_send()
        for cp in mine:
            cp.wait()

    return pl.pallas_call(
        body,
        out_shape=[jax.ShapeDtypeStruct((N_DEV,) + x.shape, x.dtype) for x in xs],
        in_specs=[_ANY] * n,
        out_specs=[_ANY] * n,
        scratch_shapes=_exchange_sems(n),
        name=name,
    )(*xs)


def _direct_exchange(gather, s_refs, r_refs, send_sems, recv_sems, local_sems):
    x_, y_, c_ = lax.axis_index("x"), lax.axis_index("y"), lax.axis_index("c")
    my = 4 * x_ + 2 * y_ + c_
    copies = []
    for b, (s_ref, r_ref) in enumerate(zip(s_refs, r_refs)):
        copies.append(pltpu.make_async_copy(s_ref if gather else s_ref.at[my], r_ref.at[my], local_sems.at[b]))
        for m in range(1, N_DEV):
            px = 1 - x_ if (m >> 2) & 1 else x_
            py = 1 - y_ if (m >> 1) & 1 else y_
            pc = 1 - c_ if m & 1 else c_
            copies.append(pltpu.make_async_remote_copy(
                src_ref=s_ref if gather else s_ref.at[4 * px + 2 * py + pc], dst_ref=r_ref.at[my],
                send_sem=send_sems.at[7 * b + m - 1], recv_sem=recv_sems.at[7 * b + m - 1],
                device_id=(px, py, pc), device_id_type=_MESH))
    return copies


def _exchange_during(step, n_steps, gather, s_refs, r_refs, send_sems, recv_sems, local_sems):
    copies = _direct_exchange(gather, s_refs, r_refs, send_sems, recv_sems, local_sems)

    @pl.when(step == 0)
    def _():
        for cp in copies:
            cp.start()

    @pl.when(step == n_steps - 1)
    def _():
        for cp in copies:
            cp.wait()


def _exchange_out_shapes(gather, srcs):
    return [jax.ShapeDtypeStruct(((N_DEV,) + s.shape) if gather else s.shape, s.dtype) for s in srcs]


def all_to_all(sends, name):
    n = len(sends)

    def body(*refs):
        copies = _direct_exchange(False, refs[:n], refs[n:2 * n], *refs[2 * n:])
        for cp in copies:
            cp.start()
        for cp in copies:
            cp.wait()

    return pl.pallas_call(
        body,
        out_shape=_exchange_out_shapes(False, sends),
        in_specs=[_ANY] * n,
        out_specs=[_ANY] * n,
        scratch_shapes=_exchange_sems(n),
        name=name,
    )(*sends)


def adamw(w, m, v, parts, name, tm=256):
    R, C = w.shape
    tm = _pick(R, tm, 8) if R >= 8 else R

    def body(w_ref, m_ref, v_ref, p_ref, g_ref, d_ref, nm_ref, nv_ref):
        g = p_ref[0].astype(F32)
        for s in range(1, N_DEV):
            g = g + p_ref[s].astype(F32)
        nm = ADAM_B1 * m_ref[...] + (1.0 - ADAM_B1) * g
        nv = ADAM_B2 * v_ref[...] + (1.0 - ADAM_B2) * (g * g)
        m_hat = nm / (1.0 - ADAM_B1 ** ADAM_STEP)
        v_hat = nv / (1.0 - ADAM_B2 ** ADAM_STEP)
        g_ref[...] = g
        d_ref[...] = -ADAM_LR * (m_hat / (jnp.sqrt(v_hat) + ADAM_EPS) + ADAM_WD * w_ref[...])
        nm_ref[...] = nm
        nv_ref[...] = nv

    row = pl.BlockSpec((tm, C), lambda i: (i, 0))
    out = jax.ShapeDtypeStruct((R, C), F32)
    return pl.pallas_call(
        body,
        grid=(R // tm,),
        in_specs=[row, row, row, pl.BlockSpec((N_DEV, tm, C), lambda i: (0, i, 0))],
        out_specs=[row] * 4,
        out_shape=[out] * 4,
        compiler_params=_cparams(("parallel",)),
        name=name,
    )(w, m, v, parts)


_WEIGHTS = ["ffn1_norm", "ffn1_w_gu", "ffn1_w_down", "mix_norm", "ffn2_norm", "ffn2_w_gu", "ffn2_w_down",
            "ple_norm", "ple_w_gate", "ple_w_proj", "even_w_in", "even_w_out", "swa_sinks", "rwkv_mu",
            "rwkv_w0", "rwkv_w2", "rwkv_a0", "rwkv_a2", "rwkv_g2", "rwkv_k_k", "rwkv_k_a", "rwkv_r_k",
            "rwkv_ln_w", "rwkv_ln_b", "fox_w_in", "fox_b_f", "fox_w_out", "final_norm"]
_SHARD_AXIS = {"ffn1_w_gu": 2, "ffn1_w_down": 1, "ffn2_w_gu": 2, "ffn2_w_down": 1, "ple_w_gate": 1,
               "ple_w_proj": 2, "even_w_in": 2, "even_w_out": 1, "rwkv_w2": 2, "rwkv_a2": 2, "rwkv_g2": 2,
               "fox_w_in": 2, "fox_w_out": 1}
_SHARDED = [n for n in _WEIGHTS if n in _SHARD_AXIS]
_REPLICATED = [n for n in _WEIGHTS if n not in _SHARD_AXIS]
_PER_LAYER = ("ffn1_w_gu", "ffn1_w_down", "ffn2_w_gu", "ffn2_w_down", "ple_w_gate", "ple_w_proj")
_ALL_PIECES = ([(n, 0) for n in _PER_LAYER] + [(n, 0) for n in ("even_w_in", "even_w_out", "rwkv_w2", "rwkv_a2", "rwkv_g2")]
               + [(n, 1) for n in _PER_LAYER] + [("fox_w_in", 0), ("fox_w_out", 0)])
_FIRST_WEIGHTS = [(n, 0) for n in ("ffn1_w_gu", "ffn1_w_down", "even_w_in", "rwkv_w2", "rwkv_a2", "rwkv_g2")]
_PIECES = [_FIRST_WEIGHTS, [pc for pc in _ALL_PIECES if pc not in _FIRST_WEIGHTS]]
_LATE_GRADS = _FIRST_WEIGHTS + [("even_w_out", 0)]
_LAST_GRADS = [("ffn1_w_gu", 0), ("ffn1_w_down", 0)]
_GRAD_PIECES = [_LAST_GRADS, [pc for pc in _LATE_GRADS if pc not in _LAST_GRADS],
                [pc for pc in _ALL_PIECES if pc not in _LATE_GRADS]]
_PACK_LANES = 1024
_PACK_ROW_TILE = 256


def _piece_key(piece):
    name, idx = piece
    return f"{name}{idx}" if name in _PER_LAYER else name


_KINDS = ("gu", "rows", "misc")


def _kind(piece):
    if piece[0] in ("ffn1_w_gu", "ffn2_w_gu"):
        return "gu"
    return "rows" if _SHARD_AXIS[piece[0]] == 1 else "misc"


def _of_kind(pieces, shapes, kind):
    return [(pc, shp) for pc, shp in zip(pieces, shapes) if _kind(pc) == kind]


def _pad_rows(flat, axis):
    pad = [(0, 0)] * flat.ndim
    pad[axis] = (0, -flat.shape[axis] % _PACK_ROW_TILE)
    return jnp.pad(flat, pad)


def _kinds_of(pieces):
    return [kind for kind in _KINDS if any(_kind(pc) == kind for pc in pieces)]


def _bundle(get, pieces, dtype):
    make = {"gu": jnp.stack,
            "rows": lambda ps: jnp.concatenate(ps, axis=0),
            "misc": lambda ps: _pad_rows(jnp.concatenate([a.reshape(-1, _PACK_LANES) for a in ps], axis=0), 0)}
    return [make[kind]([get(pc).astype(dtype) for pc in pieces if _kind(pc) == kind]) for kind in _kinds_of(pieces)]


def _unbundle(bufs, pieces, shapes):
    out = {}
    for buf, kind in zip(bufs, _kinds_of(pieces)):
        of_kind = _of_kind(pieces, shapes, kind)
        if kind == "gu":
            stacked = buf.reshape((len(of_kind),) + of_kind[0][1])
            for j, (pc, _) in enumerate(of_kind):
                out[pc] = stacked[j]
            continue
        r0 = 0
        for pc, shp in of_kind:
            n = math.prod(shp) // _PACK_LANES
            out[pc] = buf[r0:r0 + n].reshape(shp)
            r0 += n
    return out


def _unshard(gathered, pieces, shapes):
    full = {}
    for j, (pc, shp) in enumerate(_of_kind(pieces, shapes, "gu")):
        full[_piece_key(pc)] = jnp.moveaxis(gathered[0][:, j], 0, 1).reshape(shp[0], N_DEV * shp[1])
    r0 = 0
    for pc, shp in _of_kind(pieces, shapes, "rows"):
        full[_piece_key(pc)] = gathered[1][:, r0:r0 + shp[0]].reshape(N_DEV * shp[0], shp[1])
        r0 += shp[0]
    r0 = 0
    for pc, shp in _of_kind(pieces, shapes, "misc"):
        n = math.prod(shp) // _PACK_LANES
        seg = gathered[2][:, r0:r0 + n].reshape((N_DEV,) + shp)
        full[_piece_key(pc)] = jnp.moveaxis(seg, 0, 1).reshape(shp[0], N_DEV * shp[1])
        r0 += n
    return full


def _to_shards(full, pieces, shapes):
    get = lambda pc: full[_piece_key(pc)].astype(BF16)
    cols = lambda pc, shp: jnp.moveaxis(get(pc).reshape(shp[0], N_DEV, shp[1]), 1, 0)
    make = {"gu": lambda ps: jnp.stack([cols(pc, shp) for pc, shp in ps], axis=1),
            "rows": lambda ps: jnp.concatenate([get(pc).reshape((N_DEV,) + shp) for pc, shp in ps], axis=1),
            "misc": lambda ps: _pad_rows(jnp.concatenate([cols(pc, shp).reshape(N_DEV, -1, _PACK_LANES)
                                                          for pc, shp in ps], axis=1), 1)}
    return [make[kind](_of_kind(pieces, shapes, kind)) for kind in _kinds_of(pieces)]


def _layer_weights(full):
    W = dict(full)
    if "fox_w_in" in W:
        W["fox_w_in"] = jnp.pad(W["fox_w_in"], ((0, 0), (0, FOX_IN_PAD - W["fox_w_in"].shape[1])))
    for n in ("rwkv_w2", "rwkv_a2", "rwkv_g2"):
        if n in W:
            W[n] = W[n].astype(F32)
    return W


def _pack_small(vals):
    flat = jnp.concatenate([v.reshape(1, -1) for v in vals], axis=1)
    n = flat.shape[1]
    return jnp.pad(flat, ((0, 0), (0, -n % 128)))


def _unpack_small(flat, shapes):
    out, c0 = [], 0
    for shp in shapes:
        n = math.prod(shp)
        out.append(flat[0, c0:c0 + n].reshape(shp))
        c0 += n
    return out


def kernel(x, p, ffn1_norm, ffn1_w_gu, ffn1_w_down, mix_norm, ffn2_norm, ffn2_w_gu, ffn2_w_down, ple_norm, ple_w_gate, ple_w_proj, even_w_in, even_w_out, swa_sinks, rwkv_mu, rwkv_w0, rwkv_w2, rwkv_a0, rwkv_a2, rwkv_g2, rwkv_k_k, rwkv_k_a, rwkv_r_k, rwkv_ln_w, rwkv_ln_b, fox_w_in, fox_b_f, fox_w_out, final_norm, loss_target, m_ffn1_norm, m_ffn1_w_gu, m_ffn1_w_down, m_mix_norm, m_ffn2_norm, m_ffn2_w_gu, m_ffn2_w_down, m_ple_norm, m_ple_w_gate, m_ple_w_proj, m_even_w_in, m_even_w_out, m_swa_sinks, m_rwkv_mu, m_rwkv_w0, m_rwkv_w2, m_rwkv_a0, m_rwkv_a2, m_rwkv_g2, m_rwkv_k_k, m_rwkv_k_a, m_rwkv_r_k, m_rwkv_ln_w, m_rwkv_ln_b, m_fox_w_in, m_fox_b_f, m_fox_w_out, m_final_norm, v_ffn1_norm, v_ffn1_w_gu, v_ffn1_w_down, v_mix_norm, v_ffn2_norm, v_ffn2_w_gu, v_ffn2_w_down, v_ple_norm, v_ple_w_gate, v_ple_w_proj, v_even_w_in, v_even_w_out, v_swa_sinks, v_rwkv_mu, v_rwkv_w0, v_rwkv_w2, v_rwkv_a0, v_rwkv_a2, v_rwkv_g2, v_rwkv_k_k, v_rwkv_k_a, v_rwkv_r_k, v_rwkv_ln_w, v_rwkv_ln_b, v_fox_w_in, v_fox_b_f, v_fox_w_out, v_final_norm):
    given = dict(locals())
    w = {n: given[n] for n in _WEIGHTS}
    m = {n: given["m_" + n] for n in _WEIGHTS}
    v = {n: given["v_" + n] for n in _WEIGHTS}
    small_shapes = [w[n].shape for n in _REPLICATED]
    piece = lambda d, pc: d[pc[0]][pc[1]]
    shapes = [[piece(w, pc).shape for pc in pieces] for pieces in _PIECES]
    gshapes = [[piece(w, pc).shape for pc in pieces] for pieces in _GRAD_PIECES]
    w_send = [_bundle(lambda pc: piece(w, pc), pieces, BF16) for pieces in _PIECES]

    W = _layer_weights(_unshard(all_gather(w_send[0], "weights_all_gather"), _PIECES[0], shapes[0]))
    for i in range(2):
        for n in ("ffn1_norm", "mix_norm", "ffn2_norm", "ple_norm"):
            W[f"{n}{i}"] = w[n][i].reshape(1, -1)
    for n in ("swa_sinks", "rwkv_mu", "rwkv_w0", "rwkv_a0", "rwkv_k_k", "rwkv_k_a", "rwkv_r_k", "rwkv_ln_w",
              "rwkv_ln_b", "final_norm"):
        W[n] = w[n].reshape(1, -1)
    n_f = fox_b_f.shape[1]
    W["fox_b_f"] = jnp.pad(fox_b_f.reshape(1, n_f), ((0, 0), (0, 128 - n_f)))
    n_fox = fox_w_in.shape[2] * N_DEV

    def layer1_weights(gathered):
        return _layer_weights(_unshard(gathered, _PIECES[1], shapes[1]))

    def early_grads(G):
        G = dict(G, fox_w_in=G["fox_w_in"][:, :n_fox])
        return _to_shards(G, _GRAD_PIECES[2], gshapes[2])

    def mixer_grads(G):
        return _to_shards(G, _GRAD_PIECES[1], gshapes[1])

    loss_row, dx, G, parts_early, parts_mixer = device_step(x[0], p[:, 0], loss_target[0], W, w_send[1],
                                                            layer1_weights, early_grads, mixer_grads)

    parts = [all_to_all(_to_shards(G, _GRAD_PIECES[0], gshapes[0]), "grads_all_to_all"), parts_mixer, parts_early]
    out_g, out_d, out_m, out_v = {}, {}, {}, {}
    rows2d = lambda a, lead: a.reshape(a.shape[:lead] + (-1, a.shape[-1]))
    for li, pieces in enumerate(_GRAD_PIECES):
        wmv = [_bundle(lambda pc, d=d: piece(d, pc), pieces, F32) for d in (w, m, v)]
        res = [adamw(*[rows2d(b[ki], 0) for b in wmv], rows2d(parts[li][ki], 1), f"adamw_{kind}{li}")
               for ki, kind in enumerate(_kinds_of(pieces))]
        for oi, out in enumerate((out_g, out_d, out_m, out_v)):
            for pc, a in _unbundle([r[oi] for r in res], pieces, gshapes[li]).items():
                out.setdefault(pc[0], {})[pc[1]] = a
    for out in (out_g, out_d, out_m, out_v):
        for n in _SHARDED:
            out[n] = jnp.stack([out[n][i] for i in sorted(out[n])])

    gsmall = {}
    for n in ("ffn1_norm", "mix_norm", "ffn2_norm", "ple_norm"):
        gsmall[n] = jnp.concatenate([G[f"{n}0"], G[f"{n}1"]], axis=0)
    for n in ("swa_sinks", "rwkv_mu", "rwkv_w0", "rwkv_a0", "rwkv_k_k", "rwkv_k_a", "rwkv_r_k", "rwkv_ln_w",
              "rwkv_ln_b", "final_norm"):
        gsmall[n] = G[n]
    gsmall["fox_b_f"] = G["fox_b_f"][:, :n_f]
    small = _pack_small([gsmall[n] for n in _REPLICATED] + [loss_row[:, :1]])
    small_parts = all_gather([small], "small_all_gather")[0]
    pad1 = lambda vals: _pack_small(vals + [jnp.zeros((1, 1), F32)])
    gs, ds, nms, nvs = adamw(pad1([w[n] for n in _REPLICATED]), pad1([m[n] for n in _REPLICATED]),
                             pad1([v[n] for n in _REPLICATED]), small_parts, "adamw_replicated")
    out_g.update(zip(_REPLICATED, _unpack_small(gs, small_shapes)))
    out_d.update(zip(_REPLICATED, _unpack_small(ds, small_shapes)))
    out_m.update(zip(_REPLICATED, _unpack_small(nms, small_shapes)))
    out_v.update(zip(_REPLICATED, _unpack_small(nvs, small_shapes)))
    n_small = sum(math.prod(s) for s in small_shapes)
    loss = gs[0, n_small]

    return (loss, dx[None], *[out_g[n] for n in _WEIGHTS], *[out_d[n] for n in _WEIGHTS],
            *[out_m[n] for n in _WEIGHTS], *[out_v[n] for n in _WEIGHTS])
```

```python
import functools
import math

import numpy as np
import jax
import jax.numpy as jnp
from jax import lax
from jax.experimental import pallas as pl
from jax.experimental.pallas import tpu as pltpu

F32 = jnp.float32
BF16 = jnp.bfloat16

D_MODEL = 1024
HEAD_DIM = 64
BLOCK = 128
SWA_HEADS = 8
SWA_KV_HEADS = 2
SWA_GROUP = 4
RWKV_HEADS = 8
RWKV_DIM = 512
FOX_HEADS = 16
FOX_DIM = 1024
D_FF = 2816
NORM_EPS = 1e-6
GN_EPS = 64e-5
L2_EPS = 1e-12
SWA_Q = 512
SWA_KV = 128
SWA_COLS = 768
FOX_IN_PAD = 3200
N_DEV = 8
ADAM_LR = 0.001
ADAM_B1 = 0.9
ADAM_B2 = 0.999
ADAM_EPS = 1e-08
ADAM_WD = 0.01
ADAM_STEP = 10

V7X_VMEM_LIMIT = 56 * 1024 * 1024
FOX_TILE = 512
CUMSUM_BLOCK = 256
SCAN_GROUP = 8
SCAN_CHUNK = 32

_NN = (((1,), (0,)), ((), ()))
_NT = (((1,), (1,)), ((), ()))
_TN = (((0,), (0,)), ((), ()))
_DIMS = {"nn": _NN, "nt": _NT, "tn": _TN}


def _pick(n, target, mult=128):
    best = None
    for t in range(mult, min(n, target) + 1, mult):
        if n % t == 0:
            best = t
    return best or n


def _cparams(sem):
    return pltpu.CompilerParams(dimension_semantics=sem, vmem_limit_bytes=V7X_VMEM_LIMIT)


def _dot(a, b, dims):
    return lax.dot_general(a.astype(BF16), b.astype(BF16), dims, preferred_element_type=F32)


@jax.custom_vjp
def bdot(a, b):
    return _dot(a, b, _NN)


def _bdot_fwd(a, b):
    return _dot(a, b, _NN), (a, b)


def _bdot_bwd(res, g):
    a, b = res
    return _dot(g, b, _NT), _dot(a, g, _TN)


bdot.defvjp(_bdot_fwd, _bdot_bwd)


@jax.custom_vjp
def bdot_nt(a, b):
    return _dot(a, b, _NT)


def _bdot_nt_fwd(a, b):
    return _dot(a, b, _NT), (a, b)


def _bdot_nt_bwd(res, g):
    a, b = res
    return _dot(g, b, _NN), _dot(g, a, _TN)


bdot_nt.defvjp(_bdot_nt_fwd, _bdot_nt_bwd)


@jax.custom_vjp
def _segsum(x, bd):
    return _dot2(x, bd.astype(BF16))


def _segsum_fwd(x, bd):
    return _segsum(x, bd), bd


def _segsum_bwd(bd, g):
    return _dot2(g, bd.astype(BF16)), jnp.zeros_like(bd)


_segsum.defvjp(_segsum_fwd, _segsum_bwd)


def _sigmoid(x):
    return 1.0 / (1.0 + jnp.exp(-x))


def _sigmoid_tanh(x):
    return 0.5 * jnp.tanh(0.5 * x) + 0.5


def _softplus(x):
    return jnp.maximum(x, 0.0) + jnp.log(1.0 + jnp.exp(-jnp.abs(x)))


def matmul(a, b, mode, name, out_dtype=F32, scale=1.0, res=None, tm=1024, tn=1408, tk=1024):
    if mode == "nn":
        (M, K), (K2, N) = a.shape, b.shape
    elif mode == "nt":
        (M, K), (N, K2) = a.shape, b.shape
    else:
        (K, M), (K2, N) = a.shape, b.shape
    assert K == K2, (a.shape, b.shape, mode)
    tm, tn, tk = _pick(M, tm), _pick(N, tn), _pick(K, tk)
    nk = K // tk
    has_res = res is not None

    def body(*refs):
        if has_res:
            a_ref, b_ref, r_ref, o_ref, acc = refs
        else:
            a_ref, b_ref, o_ref, acc = refs
        kk = pl.program_id(2)

        @pl.when(kk == 0)
        def _():
            acc[...] = jnp.zeros_like(acc)

        acc[...] += _dot(a_ref[...], b_ref[...], _DIMS[mode])

        @pl.when(kk == nk - 1)
        def _():
            v = acc[...]
            if scale != 1.0:
                v = v * scale
            if has_res:
                v = v + r_ref[...].astype(F32)
            o_ref[...] = v.astype(out_dtype)

    if mode == "tn":
        a_spec = pl.BlockSpec((tk, tm), lambda i, j, k: (k, i))
    else:
        a_spec = pl.BlockSpec((tm, tk), lambda i, j, k: (i, k))
    if mode == "nt":
        b_spec = pl.BlockSpec((tn, tk), lambda i, j, k: (j, k))
    else:
        b_spec = pl.BlockSpec((tk, tn), lambda i, j, k: (k, j))
    o_spec = pl.BlockSpec((tm, tn), lambda i, j, k: (i, j))
    in_specs = [a_spec, b_spec] + ([o_spec] if has_res else [])
    args = (a, b) + ((res,) if has_res else ())
    return pl.pallas_call(
        body,
        grid=(M // tm, N // tn, nk),
        in_specs=in_specs,
        out_specs=o_spec,
        out_shape=jax.ShapeDtypeStruct((M, N), out_dtype),
        scratch_shapes=[pltpu.VMEM((tm, tn), F32)],
        compiler_params=_cparams(("parallel", "parallel", "arbitrary")),
        name=name,
    )(*args)


def _row_spec(r, tm):
    if isinstance(r, tuple):
        arr, width, blk = r
        return arr, pl.BlockSpec((tm, width), lambda i, blk=blk: (i, blk))
    return r, pl.BlockSpec((tm, r.shape[1]), lambda i: (i, 0))


def _whole_spec(p):
    return pl.BlockSpec(p.shape, lambda i: (0,) * p.ndim)


def rowwise(fn, rows, params, outs, name, tm=512):
    arrs, specs = zip(*[_row_spec(r, tm) for r in rows])
    S = arrs[0].shape[0]
    tm = min(tm, S)
    arrs, specs = zip(*[_row_spec(r, tm) for r in rows])
    n_in = len(rows) + len(params)

    def body(*refs):
        res = fn(*[r[...] for r in refs[:n_in]])
        for o_ref, v in zip(refs[n_in:], res):
            o_ref[...] = v.astype(o_ref.dtype)

    return pl.pallas_call(
        body,
        grid=(S // tm,),
        in_specs=list(specs) + [_whole_spec(p) for p in params],
        out_specs=[pl.BlockSpec((tm, c), lambda i: (i, 0)) for c, _ in outs],
        out_shape=[jax.ShapeDtypeStruct((S, c), dt) for c, dt in outs],
        compiler_params=_cparams(("parallel",)),
        name=name,
    )(*arrs, *params)


def rowwise_vjp(fn, rows, params, cots, name, need=None, row_dtype=F32, consts=(), tm=512):
    nr, npar, nc, nk = len(rows), len(params), len(cots), len(consts)
    need = [True] * nr if need is None else need
    arrs, _ = zip(*[_row_spec(r, tm) for r in rows])
    S = arrs[0].shape[0]
    tm = min(tm, S)
    arrs, specs = zip(*[_row_spec(r, tm) for r in rows])
    carrs, cspecs = zip(*[_row_spec(c, tm) for c in cots])
    widths = [s.block_shape[1] for s in specs]
    n_in = nr + npar + nk + nc

    def body(*refs):
        i = pl.program_id(0)
        xs = [r[...].astype(F32) for r in refs[:nr]]
        ps = [r[...] for r in refs[nr:nr + npar]]
        ks = [r[...] for r in refs[nr + npar:nr + npar + nk]]
        cs = [r[...].astype(F32) for r in refs[nr + npar + nk:n_in]]
        outs, vjp = jax.vjp(lambda *a: fn(*a, *ks), *xs, *ps)
        grads = vjp(tuple(cs))
        o = n_in
        for j in range(nr):
            if need[j]:
                refs[o][...] = grads[j].astype(refs[o].dtype)
                o += 1
        for j in range(npar):
            g_ref = refs[o + j]

            @pl.when(i == 0)
            def _(g_ref=g_ref):
                g_ref[...] = jnp.zeros_like(g_ref)

            g_ref[...] += grads[nr + j]

    out_specs = [pl.BlockSpec((tm, w), lambda i: (i, 0)) for w, nd in zip(widths, need) if nd]
    out_shape = [jax.ShapeDtypeStruct((S, w), row_dtype) for w, nd in zip(widths, need) if nd]
    out_specs += [_whole_spec(p) for p in params]
    out_shape += [jax.ShapeDtypeStruct(p.shape, F32) for p in params]
    res = pl.pallas_call(
        body,
        grid=(S // tm,),
        in_specs=list(specs) + [_whole_spec(p) for p in params] + [_whole_spec(k) for k in consts] + list(cspecs),
        out_specs=out_specs,
        out_shape=out_shape,
        compiler_params=_cparams(("arbitrary",)),
        name=name,
    )(*arrs, *params, *consts, *carrs)
    nrow = sum(need)
    return list(res[:nrow]), list(res[nrow:])


def _rms(x, g):
    return x * lax.rsqrt(jnp.mean(x * x, axis=-1, keepdims=True) + NORM_EPS) * g


def _f_rms(x, g):
    return (_rms(x, g),)


def _f_rms_res(x, g):
    return _rms(x, g), x


def _f_ple(x, z, pp):
    return (x + _sigmoid(z) * pp,)


def _f_mix(h, sh, mu):
    return (h + (sh - h) * mu,)


def _f_logf(fz, bf):
    return (-_softplus(-(fz + bf)),)


def _f_rwkv_pre(hk, hw, ha, hg, w0, w2, a0, a2, g2, k_k, k_a, bd):
    wlog = -_softplus(-(w0 + bdot(jnp.tanh(hw), w2))) - 0.5
    a = _sigmoid(a0 + bdot(ha, a2))
    g = bdot(_sigmoid(hg), g2)
    kk = hk * k_k
    kk = kk / jnp.maximum(jnp.sqrt(_segsum(kk * kk, bd)), L2_EPS)
    k2 = hk * (1.0 + (a - 1.0) * k_a)
    decay = jnp.exp(-jnp.exp(wlog))
    return decay, k2, kk, kk * a, g


def _f_rwkv_post(y, r, k2, v, g, ln_w, ln_b, r_k, bd):
    mean = _segsum(y, bd) * (1.0 / HEAD_DIM)
    d = y - mean
    var = _segsum(d * d, bd) * (1.0 / HEAD_DIM)
    yn = d * lax.rsqrt(var + GN_EPS) * ln_w + ln_b
    yn = yn + _segsum(r * k2 * r_k, bd) * v
    return (yn * g,)


def loss_head(x, target, gf, tm=256):
    S, D = x.shape
    tm = min(tm, S)

    def f(xt, g, tt):
        err = _rms(xt, g) - tt
        return 0.5 * jnp.sum(err * err) * (1.0 / D)

    def body(x_ref, t_ref, g_ref, dx_ref, dg_ref, l_ref):
        i = pl.program_id(0)
        val, (dx, dg) = jax.value_and_grad(f, argnums=(0, 1))(x_ref[...], g_ref[...], t_ref[...])

        @pl.when(i == 0)
        def _():
            dg_ref[...] = jnp.zeros_like(dg_ref)
            l_ref[...] = jnp.zeros_like(l_ref)

        dx_ref[...] = dx
        dg_ref[...] += dg
        l_ref[...] += jnp.full(l_ref.shape, val, F32)

    row = pl.BlockSpec((tm, D), lambda i: (i, 0))
    vec = pl.BlockSpec((1, D), lambda i: (0, 0))
    return pl.pallas_call(
        body,
        grid=(S // tm,),
        in_specs=[row, row, vec],
        out_specs=[row, vec, pl.BlockSpec((1, 128), lambda i: (0, 0))],
        out_shape=[jax.ShapeDtypeStruct((S, D), F32), jax.ShapeDtypeStruct((1, D), F32),
                   jax.ShapeDtypeStruct((1, 128), F32)],
        compiler_params=_cparams(("arbitrary",)),
        name="loss_head",
    )(x, target, gf)


def _swa_block(q, kp, kc, vp, vc, sink, slope, n):
    k = jnp.concatenate([kp, kc], axis=0)
    v = jnp.concatenate([vp, vc], axis=0)
    rows = q.shape[0]
    logits = bdot_nt(q, k) * (HEAD_DIM ** -0.5)
    qi = lax.broadcasted_iota(jnp.int32, (rows, 2 * BLOCK), 0) & (BLOCK - 1)
    ki = lax.broadcasted_iota(jnp.int32, (rows, 2 * BLOCK), 1)
    dist = qi + BLOCK - ki
    valid = (dist >= 0) & (dist < BLOCK) & ((n - 1) * BLOCK + ki >= 0)
    logits = logits - slope * dist.astype(F32)
    logits = jnp.where(valid, logits, -jnp.inf)
    m = jnp.maximum(jnp.max(logits, axis=-1, keepdims=True), sink)
    pr = jnp.exp(logits - m)
    denom = jnp.sum(pr, axis=-1, keepdims=True) + jnp.exp(sink - m)
    return bdot(pr / denom, v)


def _swa_specs(S):
    nb = S // BLOCK
    q_spec = pl.BlockSpec((None, SWA_GROUP, BLOCK, HEAD_DIM), lambda h, n: (h, 0, n, 0))
    kc_spec = pl.BlockSpec((None, BLOCK, HEAD_DIM), lambda h, n: (h, n, 0))
    kp_spec = pl.BlockSpec((None, BLOCK, HEAD_DIM), lambda h, n: (h, jnp.maximum(n - 1, 0), 0))
    col_spec = pl.BlockSpec((None, SWA_GROUP * BLOCK, 1), lambda h, n: (h, 0, 0))
    return nb, q_spec, kp_spec, kc_spec, col_spec


def swa_fwd(q, k, v, sink_col, slope_col):
    S = q.shape[2]
    nb, q_spec, kp_spec, kc_spec, col_spec = _swa_specs(S)

    def body(q_ref, kp_ref, kc_ref, vp_ref, vc_ref, s_ref, a_ref, o_ref):
        n = pl.program_id(1)
        qq = q_ref[...].reshape(SWA_GROUP * BLOCK, HEAD_DIM)
        out = _swa_block(qq, kp_ref[...], kc_ref[...], vp_ref[...], vc_ref[...], s_ref[...], a_ref[...], n)
        o_ref[...] = out.reshape(SWA_GROUP, BLOCK, HEAD_DIM)

    return pl.pallas_call(
        body,
        grid=(SWA_KV_HEADS, nb),
        in_specs=[q_spec, kp_spec, kc_spec, kp_spec, kc_spec, col_spec, col_spec],
        out_specs=q_spec,
        out_shape=jax.ShapeDtypeStruct(q.shape, F32),
        compiler_params=_cparams(("parallel", "parallel")),
        name="swa_fwd",
    )(q, k, k, v, v, sink_col, slope_col)


def swa_bwd(q, k, v, sink_col, slope_col, dout):
    S = q.shape[2]
    nb, q_spec, kp_spec, kc_spec, col_spec = _swa_specs(S)

    def body(q_ref, kp_ref, kc_ref, vp_ref, vc_ref, s_ref, a_ref, do_ref,
             dq_ref, dkp_ref, dkc_ref, dvp_ref, dvc_ref, ds_ref):
        n = pl.program_id(1)
        qq = q_ref[...].reshape(SWA_GROUP * BLOCK, HEAD_DIM)
        slope = a_ref[...]
        f = lambda a, b, c, d, e, s: _swa_block(a, b, c, d, e, s, slope, n)
        _, vjp = jax.vjp(f, qq, kp_ref[...], kc_ref[...], vp_ref[...], vc_ref[...], s_ref[...])
        dq, dkp, dkc, dvp, dvc, ds = vjp(do_ref[...].reshape(SWA_GROUP * BLOCK, HEAD_DIM))
        dq_ref[...] = dq.reshape(SWA_GROUP, BLOCK, HEAD_DIM)
        dkp_ref[...] = dkp
        dkc_ref[...] = dkc
        dvp_ref[...] = dvp
        dvc_ref[...] = dvc

        @pl.when(n == 0)
        def _():
            ds_ref[...] = jnp.zeros_like(ds_ref)

        ds_ref[...] += ds

    kv_shape = jax.ShapeDtypeStruct(k.shape, F32)
    return pl.pallas_call(
        body,
        grid=(SWA_KV_HEADS, nb),
        in_specs=[q_spec, kp_spec, kc_spec, kp_spec, kc_spec, col_spec, col_spec, q_spec],
        out_specs=[q_spec, kc_spec, kc_spec, kc_spec, kc_spec, col_spec],
        out_shape=[jax.ShapeDtypeStruct(q.shape, F32), kv_shape, kv_shape, kv_shape, kv_shape,
                   jax.ShapeDtypeStruct(sink_col.shape, F32)],
        compiler_params=_cparams(("parallel", "arbitrary")),
        name="swa_bwd",
    )(q, k, k, v, v, sink_col, slope_col, dout)


def _split2(x):
    hi = x.astype(BF16)
    return (x - hi.astype(F32)).astype(BF16), hi


def _dot2_many(xs, m, single=False):
    rows = xs[0].shape[0]
    if single:
        res = jnp.dot(jnp.concatenate([x.astype(BF16) for x in xs], axis=0), m, preferred_element_type=F32)
        return [res[i * rows:(i + 1) * rows] for i in range(len(xs))]
    res = jnp.dot(jnp.concatenate([p for x in xs for p in _split2(x)], axis=0), m, preferred_element_type=F32)
    return [res[(2 * i) * rows:(2 * i + 1) * rows] + res[(2 * i + 1) * rows:(2 * i + 2) * rows]
            for i in range(len(xs))]


def _dot2(x, m):
    return _dot2_many([x], m)[0]


def _seg_sums(xs, bd, single=False):
    w = bd.shape[0]
    halves = _dot2_many([x[:, i:i + w] for x in xs for i in range(0, x.shape[1], w)], bd, single)
    n = xs[0].shape[1] // w
    return [jnp.concatenate(halves[i * n:(i + 1) * n], axis=1) for i in range(len(xs))]


def _seg_sum(x, bd):
    return _seg_sums([x], bd)[0]


def _scan_consts():
    r = np.arange(256)
    bd = (r[:, None] // HEAD_DIM == r[None, :] // HEAD_DIM).astype(np.float32)
    c = np.arange(RWKV_DIM)
    e = (np.arange(HEAD_DIM)[:, None] // SCAN_GROUP == c[None, :] // HEAD_DIM).astype(np.float32)
    diag = (np.arange(HEAD_DIM)[:, None] == c[None, :] % HEAD_DIM).astype(np.float32)
    return jnp.asarray(bd, BF16), jnp.asarray(e, BF16), jnp.asarray(diag, F32)


def _to_colblocks(a):
    S = a.shape[0]
    a = a.reshape(S // SCAN_GROUP, SCAN_GROUP, RWKV_HEADS, HEAD_DIM)
    return a.transpose(0, 3, 2, 1).reshape(S // SCAN_GROUP, HEAD_DIM, RWKV_HEADS * SCAN_GROUP)


def _roll_up(rows):
    return pltpu.roll(rows, rows.shape[0] - 1, 0)


def _scan_pair_rows(aux, base, kk_ref, w_ref, b_ref, k_ref, bd):
    G = SCAN_GROUP
    kk_nx = _roll_up(kk_ref[pl.ds(base, G), :])
    aux[0] = w_ref[pl.ds(base, G), :] * kk_nx
    aux[1], aux[2] = _seg_sums([b_ref[pl.ds(base, G), :] * kk_nx, k_ref[pl.ds(base, G), :] * kk_nx], bd)


def _scan_pair(St, t0, base, col_g, lane_t, aux, kk_ref, w_ref, b_ref, k_ref, bd, e):
    t1 = t0 + 1
    row = lambda ref, t: ref[pl.ds(base + t, 1), :]
    arow = lambda i: aux[i, pl.ds(t0, 1), :]
    u0, m1 = _seg_sums([St * row(kk_ref, t0), St * arow(0)], bd)
    v0, v1 = _dot2_many([jnp.where(lane_t == t0, col_g, 0.0), jnp.where(lane_t == t1, col_g, 0.0)], e)
    u1 = m1 - u0 * arow(1) + v0 * arow(2)
    S0 = St * row(w_ref, t0) - u0 * row(b_ref, t0) + v0 * row(k_ref, t0)
    S1 = S0 * row(w_ref, t1) - u1 * row(b_ref, t1) + v1 * row(k_ref, t1)
    return (S0, S1), (u0, u1), (v0, v1)


def rwkv_scan_fwd(r, w, k, kk, b, vB, gather_srcs):
    S, C = r.shape
    N, G = HEAD_DIM, SCAN_GROUP
    chunk = min(SCAN_CHUNK, S)
    nchunk, ng = S // chunk, chunk // G
    bd, e, diag = _scan_consts()

    nx = len(gather_srcs)

    def body(*refs):
        r_ref, w_ref, k_ref, kk_ref, b_ref, vB_ref, bd_ref, e_ref, dg_ref = refs[:9]
        y_ref, ck_ref = refs[9 + nx:11 + nx]
        S_ref, aux, send_sems, recv_sems, local_sems = refs[11 + 2 * nx:]
        c = pl.program_id(0)
        _exchange_during(c, nchunk, True, refs[9:9 + nx], refs[11 + nx:11 + 2 * nx], send_sems, recv_sems, local_sems)

        @pl.when(c == 0)
        def _():
            S_ref[...] = jnp.zeros_like(S_ref)

        ck_ref[...] = S_ref[...]
        sub = lax.broadcasted_iota(jnp.int32, (G, C), 0)
        lane_t = lax.broadcasted_iota(jnp.int32, (N, N), 1) & (G - 1)

        def group(g, St):
            base = pl.multiple_of(g * G, G)
            vb = vB_ref[g]
            _scan_pair_rows(aux, base, kk_ref, w_ref, b_ref, k_ref, bd_ref[...])
            ys = jnp.zeros((G, C), F32)
            def emit(ys, states, t0):
                steps = (t0, t0 + 1)
                y_bs = _seg_sums([S_t * r_ref[pl.ds(base + tt, 1), :] for S_t, tt in zip(states, steps)], bd_ref[...],
                                 single=True)
                for y_b, tt in zip(y_bs, steps):
                    ys = jnp.where(sub == tt, jnp.sum(y_b * dg_ref[...], axis=0, keepdims=True), ys)
                return ys

            pending = None
            for t0 in range(0, G, 2):
                states, _, _ = _scan_pair(St, t0, base, vb, lane_t, aux, kk_ref, w_ref, b_ref, k_ref, bd_ref[...],
                                          e_ref[...])
                if pending is not None:
                    ys = emit(ys, *pending)
                pending = (states, t0)
                St = states[1]
            y_ref[pl.ds(base, G), :] = emit(ys, *pending)
            return St

        S_ref[...] = lax.fori_loop(0, ng, group, S_ref[...])

    row = pl.BlockSpec((chunk, C), lambda c: (c, 0))
    col = pl.BlockSpec((ng, N, N), lambda c: (c, 0, 0))
    res = pl.pallas_call(
        body,
        grid=(nchunk,),
        in_specs=[row] * 5 + [col, _whole_spec(bd), _whole_spec(e), _whole_spec(diag)] + [_ANY] * nx,
        out_specs=[row, pl.BlockSpec((None, N, C), lambda c: (c, 0, 0))] + [_ANY] * nx,
        out_shape=[jax.ShapeDtypeStruct((S, C), F32), jax.ShapeDtypeStruct((nchunk, N, C), F32)]
        + _exchange_out_shapes(True, gather_srcs),
        scratch_shapes=[pltpu.VMEM((N, C), F32), pltpu.VMEM((3, G, C), F32)] + _exchange_sems(nx),
        compiler_params=_cparams(("arbitrary",)),
        name="rwkv_scan_fwd",
    )(r, w, k, kk, b, vB, bd, e, diag, *gather_srcs)
    return res[0], res[1], list(res[2:])


def rwkv_scan_bwd(r, w, k, kk, b, vB, dyB, ckpt, scatter_srcs):
    S, C = r.shape
    N, G = HEAD_DIM, SCAN_GROUP
    chunk = min(SCAN_CHUNK, S)
    nchunk, ng = S // chunk, chunk // G
    nsteps = nchunk + 1
    bd, e, diag = _scan_consts()
    nx = len(scatter_srcs)

    def body(*refs):
        wf_ref, kf_ref, kkf_ref, bf_ref, vBf_ref, ck_ref = refs[:6]
        r_ref, w_ref, k_ref, kk_ref, b_ref, dyB_ref, bd_ref, e_ref, dg_ref = refs[6:15]
        dr_ref, dw_ref, dk_ref, dkk_ref, db_ref, dv_ref = refs[15 + nx:21 + nx]
        G_ref, sbuf, ubuf, vbuf, aux_f, aux_b, send_sems, recv_sems, local_sems = refs[21 + 2 * nx:]
        c = pl.program_id(0)
        _exchange_during(c, nsteps, False, refs[15:15 + nx], refs[21 + nx:21 + 2 * nx], send_sems, recv_sems,
                         local_sems)

        @pl.when(c == 0)
        def _():
            G_ref[...] = jnp.zeros_like(G_ref)
            sbuf[...] = jnp.zeros_like(sbuf)
            ubuf[...] = jnp.zeros_like(ubuf)
            vbuf[...] = jnp.zeros_like(vbuf)

        sf = c % 2
        sb = 1 - sf
        lane_t = lax.broadcasted_iota(jnp.int32, (N, N), 1) & (G - 1)
        sub = lax.broadcasted_iota(jnp.int32, (G, C), 0)
        colsum = lambda a: jnp.sum(a, axis=0, keepdims=True)

        def group(g, carry):
            St, Gt = carry
            base_f = pl.multiple_of(g * G, G)
            gb = ng - 1 - g
            base_b = pl.multiple_of(gb * G, G)
            vb, dyb = vBf_ref[g], dyB_ref[gb]
            row = lambda ref, t: ref[pl.ds(base_b + t, 1), :]
            _scan_pair_rows(aux_f, base_f, kkf_ref, wf_ref, bf_ref, kf_ref, bd_ref[...])
            b8 = b_ref[pl.ds(base_b, G), :]
            aux_b[0] = _roll_up(w_ref[pl.ds(base_b, G), :]) * b8
            aux_b[1], aux_b[2] = _seg_sums([_roll_up(kk_ref[pl.ds(base_b, G), :]) * b8,
                                            r_ref[pl.ds(base_b, G), :] * b8], bd_ref[...])
            rows = [jnp.zeros((G, C), F32) for _ in range(6)]

            def emit(rows, steps):
                d_vs = _seg_sums([Gt_ * row(k_ref, tt) for tt, Gt_, _, _ in steps], bd_ref[...], single=True)
                for (tt, Gt_, du_b, dy_b), d_vb in zip(steps, d_vs):
                    Sp, Sc = sbuf[sb, base_b + tt], sbuf[sb, base_b + tt + 1]
                    new = (colsum(Sc * dy_b), colsum(Gt_ * Sp), colsum(Gt_ * vbuf[sb, base_b + tt]),
                           colsum(Sp * du_b), -colsum(Gt_ * ubuf[sb, base_b + tt]), colsum(d_vb * dg_ref[...]))
                    rows = [jnp.where(sub == tt, n_, acc) for n_, acc in zip(new, rows)]
                return rows

            pending = None
            for i in range(G // 2):
                t0 = 2 * i
                states, us, vs = _scan_pair(St, t0, base_f, vb, lane_t, aux_f, kkf_ref, wf_ref, bf_ref, kf_ref,
                                            bd_ref[...], e_ref[...])
                for j, S_before in enumerate((St, states[0])):
                    sbuf[sf, base_f + t0 + j] = S_before
                    ubuf[sf, base_f + t0 + j] = us[j]
                    vbuf[sf, base_f + t0 + j] = vs[j]
                St = states[1]
                t0 = G - 2 - 2 * i
                t1 = t0 + 1
                arow = lambda j, t0=t0: aux_b[j, pl.ds(t0, 1), :]
                dy1, dy0 = _dot2_many([jnp.where(lane_t == t1, dyb, 0.0), jnp.where(lane_t == t0, dyb, 0.0)],
                                      e_ref[...])
                G1 = Gt + dy1 * row(r_ref, t1)
                m1, m2 = _seg_sums([G1 * row(b_ref, t1), G1 * arow(0)], bd_ref[...])
                du1 = -m1
                du0 = -(m2 + du1 * arow(1) + dy0 * arow(2))
                G0 = G1 * row(w_ref, t1) + du1 * row(kk_ref, t1) + dy0 * row(r_ref, t0)
                G_next = G0 * row(w_ref, t0) + du0 * row(kk_ref, t0)
                if pending is not None:
                    rows = emit(rows, pending)
                pending = ((t1, G1, du1, dy1), (t0, G0, du0, dy0))
                Gt = G_next
            rows = emit(rows, pending)
            for ref, val in zip((dr_ref, dw_ref, dk_ref, dkk_ref, db_ref, dv_ref), rows):
                ref[pl.ds(base_b, G), :] = val
            return St, Gt

        St, Gt = lax.fori_loop(0, ng, group, (ck_ref[...], G_ref[...]))
        sbuf[sf, chunk] = St
        G_ref[...] = jnp.where(c >= 1, Gt, G_ref[...])

    fwd_chunk = lambda c: jnp.maximum(nchunk - 1 - c, 0)
    bwd_chunk = lambda c: jnp.minimum(nchunk - c, nchunk - 1)
    row_f = pl.BlockSpec((chunk, C), lambda c: (fwd_chunk(c), 0))
    row_b = pl.BlockSpec((chunk, C), lambda c: (bwd_chunk(c), 0))
    col_f = pl.BlockSpec((ng, N, N), lambda c: (fwd_chunk(c), 0, 0))
    col_b = pl.BlockSpec((ng, N, N), lambda c: (bwd_chunk(c), 0, 0))
    rshape = jax.ShapeDtypeStruct((S, C), F32)
    res = pl.pallas_call(
        body,
        grid=(nsteps,),
        in_specs=[row_f] * 4 + [col_f, pl.BlockSpec((None, N, C), lambda c: (fwd_chunk(c), 0, 0))]
        + [row_b] * 5 + [col_b, _whole_spec(bd), _whole_spec(e), _whole_spec(diag)] + [_ANY] * nx,
        out_specs=[row_b] * 6 + [_ANY] * nx,
        out_shape=[rshape] * 6 + _exchange_out_shapes(False, scatter_srcs),
        scratch_shapes=[pltpu.VMEM((N, C), F32), pltpu.VMEM((2, chunk + 1, N, C), F32),
                        pltpu.VMEM((2, chunk, N, C), F32), pltpu.VMEM((2, chunk, N, C), F32),
                        pltpu.VMEM((3, G, C), F32), pltpu.VMEM((3, G, C), F32)] + _exchange_sems(nx),
        compiler_params=_cparams(("arbitrary",)),
        name="rwkv_scan_bwd",
    )(w, k, kk, b, vB, ckpt, r, w, k, kk, b, dyB, bd, e, diag, *scatter_srcs)
    return tuple(res[:6]) + (list(res[6:]),)


def seq_cumsum(x, reverse, name):
    S, C = x.shape
    tb = min(CUMSUM_BLOCK, S)
    nb = S // tb

    def body(x_ref, o_ref, carry):
        i = pl.program_id(0)

        @pl.when(i == 0)
        def _():
            carry[...] = jnp.zeros_like(carry)

        ri = lax.broadcasted_iota(jnp.int32, (tb, tb), 0)
        ci = lax.broadcasted_iota(jnp.int32, (tb, tb), 1)
        tri = jnp.where((ci >= ri) if reverse else (ci <= ri), 1.0, 0.0).astype(F32)
        xb = x_ref[...]
        out = jnp.dot(tri, xb, precision=lax.Precision.HIGHEST, preferred_element_type=F32) + carry[...]
        o_ref[...] = out
        carry[...] = carry[...] + jnp.sum(xb, axis=0, keepdims=True)

    idx = (lambda i: (nb - 1 - i, 0)) if reverse else (lambda i: (i, 0))
    return pl.pallas_call(
        body,
        grid=(nb,),
        in_specs=[pl.BlockSpec((tb, C), idx)],
        out_specs=pl.BlockSpec((tb, C), idx),
        out_shape=jax.ShapeDtypeStruct((S, C), F32),
        scratch_shapes=[pltpu.VMEM((1, C), F32)],
        compiler_params=_cparams(("arbitrary",)),
        name=name,
    )(x)


def _fox_logits(q, k, cq, ck, diagonal):
    s = _dot(q, k, _NT) * (HEAD_DIM ** -0.5) + cq - ck
    if not diagonal:
        return s
    row = lax.broadcasted_iota(jnp.int32, s.shape, 0)
    col = lax.broadcasted_iota(jnp.int32, s.shape, 1)
    return jnp.where(col <= row, s, -jnp.inf)


def _fox_tiles(n, by_query):
    pairs = [(i, j) for i in range(n) for j in range(i + 1)] if by_query else \
            [(i, j) for j in range(n) for i in range(j, n)]
    return (jnp.asarray(np.array([p[0] for p in pairs], np.int32)),
            jnp.asarray(np.array([p[1] for p in pairs], np.int32)))


def _fox_specs(t, Dh):
    qs = pl.BlockSpec((None, t, Dh), lambda h, s, qt, kt: (h, qt[s], 0))
    ks = pl.BlockSpec((None, t, Dh), lambda h, s, qt, kt: (h, kt[s], 0))
    cqs = pl.BlockSpec((None, t, 1), lambda h, s, qt, kt: (h, qt[s], 0))
    cks = pl.BlockSpec((None, 1, t), lambda h, s, qt, kt: (h, 0, kt[s]))
    return qs, ks, cqs, cks


def _fox_call(body, tiles, Hh, in_specs, out_specs, out_shape, scratch, name, args):
    spec = pltpu.PrefetchScalarGridSpec(num_scalar_prefetch=2, grid=(Hh, tiles[0].shape[0]), in_specs=in_specs,
                                        out_specs=out_specs, scratch_shapes=scratch)
    return pl.pallas_call(body, grid_spec=spec, out_shape=out_shape,
                          compiler_params=_cparams(("parallel", "arbitrary")), name=name)(*tiles, *args)


def fox_fwd(q, k, v, c_col, c_row):
    Hh, S, Dh = q.shape
    tq = tk = min(FOX_TILE, S)

    def body(qt_ref, kt_ref, q_ref, k_ref, v_ref, cq_ref, ck_ref, o_ref, lse_ref, m_s, l_s, acc_s):
        qi, ki = qt_ref[pl.program_id(1)], kt_ref[pl.program_id(1)]

        @pl.when(ki == 0)
        def _():
            m_s[...] = jnp.full_like(m_s, -jnp.inf)
            l_s[...] = jnp.zeros_like(l_s)
            acc_s[...] = jnp.zeros_like(acc_s)

        def tile(diagonal):
            s = _fox_logits(q_ref[...], k_ref[...], cq_ref[...], ck_ref[...], diagonal)
            m_old = m_s[...]
            m_new = jnp.maximum(m_old, jnp.max(s, axis=-1, keepdims=True))
            alpha = jnp.exp(m_old - m_new)
            p = jnp.exp(s - m_new)
            l_s[...] = alpha * l_s[...] + jnp.sum(p, axis=-1, keepdims=True)
            acc_s[...] = alpha * acc_s[...] + _dot(p, v_ref[...], _NN)
            m_s[...] = m_new

        @pl.when(ki != qi)
        def _():
            tile(False)

        @pl.when(ki == qi)
        def _():
            tile(True)
            o_ref[...] = acc_s[...] / l_s[...]
            lse_ref[...] = m_s[...] + jnp.log(l_s[...])

    qs, ks, cqs, cks = _fox_specs(tq, Dh)
    return _fox_call(
        body, _fox_tiles(S // tq, True), Hh, [qs, ks, ks, cqs, cks], [qs, cqs],
        [jax.ShapeDtypeStruct((Hh, S, Dh), F32), jax.ShapeDtypeStruct((Hh, S, 1), F32)],
        [pltpu.VMEM((tq, 1), F32), pltpu.VMEM((tq, 1), F32), pltpu.VMEM((tq, Dh), F32)],
        "fox_fwd", (q, k, v, c_col, c_row))


def fox_bwd(q, k, v, c_col, c_row, o, lse, do):
    Hh, S, Dh = q.shape
    tq = tk = min(FOX_TILE, S)
    nk = S // tk

    def body(qt_ref, kt_ref, q_ref, k_ref, v_ref, cq_ref, ck_ref, o_ref, lse_ref, do_ref,
             dq_ref, dr_ref, dk_ref, dv_ref, dc_ref, acc_s, row_s):
        step = pl.program_id(1)
        qi, ki = qt_ref[step], kt_ref[step]

        @pl.when(step == 0)
        def _():
            dk_ref[...] = jnp.zeros_like(dk_ref)
            dv_ref[...] = jnp.zeros_like(dv_ref)
            dc_ref[...] = jnp.zeros_like(dc_ref)

        @pl.when(ki == 0)
        def _():
            acc_s[...] = jnp.zeros_like(acc_s)
            row_s[...] = jnp.zeros_like(row_s)

        def tile(diagonal):
            q_t, kb, vb, do_t = q_ref[...], k_ref[...], v_ref[...], do_ref[...]
            s = _fox_logits(q_t, kb, cq_ref[...], ck_ref[...], diagonal)
            p = jnp.exp(s - lse_ref[...])
            delta = jnp.sum(do_t * o_ref[...], axis=-1, keepdims=True)
            ds = p * (_dot(do_t, vb, _NT) - delta)
            acc_s[...] += _dot(ds, kb, _NN)
            row_s[...] += jnp.sum(ds, axis=-1, keepdims=True)
            dk_ref[ki] += _dot(ds, q_t, _TN) * (HEAD_DIM ** -0.5)
            dv_ref[ki] += _dot(p, do_t, _TN)
            dc_ref[ki] += jnp.sum(ds, axis=0, keepdims=True)

        @pl.when(ki != qi)
        def _():
            tile(False)

        @pl.when(ki == qi)
        def _():
            tile(True)
            dq_ref[...] = acc_s[...] * (HEAD_DIM ** -0.5)
            dr_ref[...] = row_s[...]

    qs, ks, cqs, cks = _fox_specs(tq, Dh)
    head = lambda *blk: pl.BlockSpec((None,) + blk, lambda h, s, qt, kt: (h,) + (0,) * len(blk))
    dq, dr, dk, dv, dc = _fox_call(
        body, _fox_tiles(S // tq, True), Hh, [qs, ks, ks, cqs, cks, qs, cqs, qs],
        [qs, cqs, head(nk, tk, Dh), head(nk, tk, Dh), head(nk, 1, tk)],
        [jax.ShapeDtypeStruct((Hh, S, Dh), F32), jax.ShapeDtypeStruct((Hh, S, 1), F32),
         jax.ShapeDtypeStruct((Hh, nk, tk, Dh), F32), jax.ShapeDtypeStruct((Hh, nk, tk, Dh), F32),
         jax.ShapeDtypeStruct((Hh, nk, 1, tk), F32)],
        [pltpu.VMEM((tq, Dh), F32), pltpu.VMEM((tq, 1), F32)],
        "fox_bwd", (q, k, v, c_col, c_row, o, lse, do))
    return dq, dr, dk.reshape(Hh, S, Dh), dv.reshape(Hh, S, Dh), dc.reshape(Hh, 1, S)


def _heads(a, nh):
    S = a.shape[0]
    return a.reshape(S, nh, HEAD_DIM).transpose(1, 0, 2)


def _unheads(a):
    nh, S, _ = a.shape
    return a.transpose(1, 0, 2).reshape(S, nh * HEAD_DIM)


def _shift_down(a):
    return jnp.pad(a[:-1], ((1, 0), (0, 0)))


def _shift_up(a):
    return jnp.pad(a[1:], ((0, 1), (0, 0)))


def _block_diag_ones():
    i = np.arange(RWKV_DIM) // HEAD_DIM
    return jnp.asarray((i[:, None] == i[None, :]).astype(np.float32))


FFN_ROWS = 1024
FFN_COLS = 256


def _ffn_specs(S, F, tm, fc):
    nf = F // fc
    row = pl.BlockSpec((tm, D_MODEL), lambda i, j: (i, 0))
    vec = pl.BlockSpec((1, D_MODEL), lambda i, j: (0, 0))
    wg = pl.BlockSpec((D_MODEL, fc), lambda i, j: (0, j))
    wu = pl.BlockSpec((D_MODEL, fc), lambda i, j: (0, nf + j))
    wd = pl.BlockSpec((fc, D_MODEL), lambda i, j: (j, 0))
    hid = pl.BlockSpec((tm, fc), lambda i, j: (i, j))
    return nf, row, vec, wg, wu, wd, hid


def ffn_fwd(x, g_norm, w_gu, w_down, tag):
    S, F = x.shape[0], w_down.shape[0]
    tm, fc = min(FFN_ROWS, S), FFN_COLS
    nf, row, vec, wg, wu, wd, _ = _ffn_specs(S, F, tm, fc)

    def body(x_ref, g_ref, wg_ref, wu_ref, wd_ref, o_ref, hn_ref, hn_s, acc):
        j = pl.program_id(1)

        @pl.when(j == 0)
        def _():
            hn_s[...] = _rms(x_ref[...], g_ref[...]).astype(BF16)
            hn_ref[...] = hn_s[...]
            acc[...] = jnp.zeros_like(acc)

        g = _dot(hn_s[...], wg_ref[...], _NN)
        u = _dot(hn_s[...], wu_ref[...], _NN)
        acc[...] += _dot(g * _sigmoid_tanh(g) * u, wd_ref[...], _NN)

        @pl.when(j == nf - 1)
        def _():
            o_ref[...] = x_ref[...] + 0.5 * acc[...]

    out, hn = pl.pallas_call(
        body,
        grid=(S // tm, nf),
        in_specs=[row, vec, wg, wu, wd],
        out_specs=[row, row],
        out_shape=[jax.ShapeDtypeStruct((S, D_MODEL), F32), jax.ShapeDtypeStruct((S, D_MODEL), BF16)],
        scratch_shapes=[pltpu.VMEM((tm, D_MODEL), BF16), pltpu.VMEM((tm, D_MODEL), F32)],
        compiler_params=_cparams(("parallel", "arbitrary")),
        name=tag + "_fwd",
    )(x, g_norm, w_gu, w_gu, w_down)
    return out, (x, hn)


def ffn_bwd(dy, saved, g_norm, w_gu, w_down, tag, scatter_srcs=()):
    x, hn = saved
    S, F = x.shape[0], w_down.shape[0]
    tm, fc = min(FFN_ROWS, S), FFN_COLS
    nf, row, vec, wg, wu, wd, hid = _ffn_specs(S, F, tm, fc)
    nx = len(scatter_srcs)

    def body(*refs):
        dy_ref, x_ref, hn_ref, g_ref, wg_ref, wu_ref, wd_ref = refs[:7]
        dx_ref, dgn_ref, a_ref, dg_ref, du_ref = refs[7 + nx:12 + nx]
        dyh_s, dhn = refs[12 + 2 * nx:14 + 2 * nx]
        i, j = pl.program_id(0), pl.program_id(1)
        if nx:
            _exchange_during(i * nf + j, (S // tm) * nf, False, refs[7:7 + nx], refs[12 + nx:12 + 2 * nx],
                             *refs[14 + 2 * nx:])

        @pl.when(j == 0)
        def _():
            dyh_s[...] = (0.5 * dy_ref[...]).astype(BF16)
            dhn[...] = jnp.zeros_like(dhn)

        hn_t = hn_ref[...]
        g = _dot(hn_t, wg_ref[...], _NN)
        u = _dot(hn_t, wu_ref[...], _NN)
        da = _dot(dyh_s[...], wd_ref[...], _NT)
        sig = _sigmoid_tanh(g)
        gs = g * sig
        a_ref[...] = (gs * u).astype(BF16)
        dg = ((da * u) * (sig + gs * (1.0 - sig))).astype(BF16)
        du = (da * gs).astype(BF16)
        dg_ref[...] = dg
        du_ref[...] = du
        dhn[...] += _dot(jnp.concatenate([dg, du], axis=1),
                         jnp.concatenate([wg_ref[...], wu_ref[...]], axis=1), _NT)

        @pl.when(j == nf - 1)
        def _():
            _, vjp_n = jax.vjp(_rms, x_ref[...], g_ref[...])
            dx, dgn = vjp_n(dhn[...])
            dx_ref[...] = dy_ref[...] + dx

            @pl.when(i == 0)
            def _():
                dgn_ref[...] = jnp.zeros_like(dgn_ref)

            dgn_ref[...] += dgn

    hshape = jax.ShapeDtypeStruct((S, F), BF16)
    res = pl.pallas_call(
        body,
        grid=(S // tm, nf),
        in_specs=[row, row, row, vec, wg, wu, wd] + [_ANY] * nx,
        out_specs=[row, vec, hid, hid, hid] + [_ANY] * nx,
        out_shape=[jax.ShapeDtypeStruct((S, D_MODEL), F32), jax.ShapeDtypeStruct((1, D_MODEL), F32),
                   hshape, hshape, hshape] + _exchange_out_shapes(False, scatter_srcs),
        scratch_shapes=[pltpu.VMEM((tm, D_MODEL), BF16), pltpu.VMEM((tm, D_MODEL), F32)]
        + (_exchange_sems(nx) if nx else []),
        compiler_params=_cparams(("arbitrary", "arbitrary")),
        name=tag + "_bwd",
    )(dy, x, hn, g_norm, w_gu, w_gu, w_down, *scatter_srcs)
    dx, dgn, act, dg, du = res[:5]
    d_wdown = matmul(act, dy, "tn", tag + "_dwd", out_dtype=BF16, scale=0.5, tm=1408)
    d_wgu = jnp.concatenate([matmul(hn, dg, "tn", tag + "_dwg", out_dtype=BF16),
                             matmul(hn, du, "tn", tag + "_dwu", out_dtype=BF16)], axis=1)
    return dx, dgn, d_wgu, d_wdown, list(res[5:])


def ple_fwd(x, p_i, g_norm, w_gate, w_proj, tag):
    hn, = rowwise(_f_rms, [x], [g_norm], [(D_MODEL, BF16)], tag + "_rms")
    z = matmul(hn, w_gate, "nn", tag + "_gate")
    pp = matmul(p_i, w_proj, "nn", tag + "_proj")
    out, = rowwise(_f_ple, [x, z, pp], [], [(D_MODEL, F32)], tag + "_mix")
    return out, (x, hn, z, pp)


def ple_bwd(dy, saved, p_i, g_norm, w_gate, tag):
    x, hn, z, pp = saved
    (dz, dpp), _ = rowwise_vjp(_f_ple, [x, z, pp], [], [dy], tag + "_dmix", need=[False, True, True],
                               row_dtype=BF16)
    d_wproj = matmul(p_i, dpp, "tn", tag + "_dwp", out_dtype=BF16)
    d_wgate = matmul(hn, dz, "tn", tag + "_dwg", out_dtype=BF16)
    dhn = matmul(dz, w_gate, "nt", tag + "_dhn")
    (dx,), (dgn,) = rowwise_vjp(_f_rms_res, [x], [g_norm], [dhn, dy], tag + "_drms")
    return dx, dgn, d_wgate, d_wproj


def _swa_consts(sinks):
    slopes = np.asarray([2.0 ** (-(i + 1)) for i in range(SWA_HEADS)], np.float32)
    slope_col = jnp.asarray(np.repeat(slopes, BLOCK).reshape(SWA_KV_HEADS, SWA_GROUP * BLOCK, 1))
    sink_col = jnp.repeat(sinks.reshape(SWA_HEADS), BLOCK).reshape(SWA_KV_HEADS, SWA_GROUP * BLOCK, 1)
    return sink_col, slope_col


def even_mix_fwd(x, W, gather_src, later_weights):
    S = x.shape[0]
    hn, = rowwise(_f_rms, [x], [W["mix_norm0"]], [(D_MODEL, BF16)], "emix_rms")
    proj = matmul(hn, W["even_w_in"], "nn", "emix_in")
    qa = _heads(proj[:, :SWA_Q], SWA_HEADS).reshape(SWA_KV_HEADS, SWA_GROUP, S, HEAD_DIM)
    ka = _heads(proj[:, SWA_Q:SWA_Q + SWA_KV], SWA_KV_HEADS)
    va = _heads(proj[:, SWA_Q + SWA_KV:SWA_COLS], SWA_KV_HEADS)
    sink_col, slope_col = _swa_consts(W["swa_sinks"])
    ya = swa_fwd(qa, ka, va, sink_col, slope_col)
    ya = _unheads(ya.reshape(SWA_HEADS, S, HEAD_DIM))
    hb = proj[:, SWA_COLS:]
    h, = rowwise(_f_mix, [hb, _shift_down(hb)], [W["rwkv_mu"]], [(hb.shape[1], F32)], "rwkv_shift")
    hr, hk, hv = h[:, :512], h[:, 512:1024], h[:, 1024:1536]
    hw, ha, hg = h[:, 1536:1600], h[:, 1600:1664], h[:, 1664:1792]
    bd = _block_diag_ones()
    pre_params = [W["rwkv_w0"], W["rwkv_w2"], W["rwkv_a0"], W["rwkv_a2"], W["rwkv_g2"], W["rwkv_k_k"],
                  W["rwkv_k_a"]]
    decay, k2, kk, b, g = rowwise(_f_rwkv_pre, [hk, hw, ha, hg], pre_params + [bd],
                                  [(RWKV_DIM, F32)] * 5, "rwkv_pre")
    vT = _to_colblocks(hv)
    y, ckpt, gathered = rwkv_scan_fwd(hr, decay, k2, kk, b, vT, gather_src)
    late = later_weights(gathered)
    post_params = [W["rwkv_ln_w"], W["rwkv_ln_b"], W["rwkv_r_k"]]
    yb, = rowwise(_f_rwkv_post, [y, hr, k2, hv, g], post_params + [bd], [(RWKV_DIM, F32)], "rwkv_post")
    cat = jnp.concatenate([ya, yb], axis=1).astype(BF16)
    out = matmul(cat, late["even_w_out"], "nn", "emix_out", res=x)
    saved = (x, hn, qa, ka, va, sink_col, slope_col, hb, hr, hk, hv, hw, ha, hg, decay, k2, kk, b, g, vT,
             ckpt, y, cat)
    return out, saved, late


def even_mix_bwd(dy, saved, W, scatter_src):
    (x, hn, qa, ka, va, sink_col, slope_col, hb, hr, hk, hv, hw, ha, hg, decay, k2, kk, b, g, vT, ckpt, y,
     cat) = saved
    S = x.shape[0]
    grads = {}
    dcat = matmul(dy, W["even_w_out"], "nt", "emix_dcat")
    grads["even_w_out"] = matmul(cat, dy, "tn", "emix_dwout", out_dtype=BF16)
    dya, dyb = dcat[:, :SWA_Q], dcat[:, SWA_Q:]
    dya_h = _heads(dya, SWA_HEADS).reshape(SWA_KV_HEADS, SWA_GROUP, S, HEAD_DIM)
    dqa, dkp, dkc, dvp, dvc, dsink = swa_bwd(qa, ka, va, sink_col, slope_col, dya_h)
    shift_blk = lambda a: jnp.pad(a[:, BLOCK:], ((0, 0), (0, BLOCK), (0, 0)))
    dka = dkc + shift_blk(dkp)
    dva = dvc + shift_blk(dvp)
    grads["swa_sinks"] = dsink.reshape(SWA_HEADS, BLOCK).sum(axis=1).reshape(1, SWA_HEADS)
    dqa = _unheads(dqa.reshape(SWA_HEADS, S, HEAD_DIM))
    dka, dva = _unheads(dka), _unheads(dva)
    bd = _block_diag_ones()
    post_params = [W["rwkv_ln_w"], W["rwkv_ln_b"], W["rwkv_r_k"]]
    (d_y, d_r1, d_k2a, d_v1, d_g), (d_lnw, d_lnb, d_rk) = rowwise_vjp(
        _f_rwkv_post, [y, hr, k2, hv, g], post_params, [dyb], "rwkv_dpost", consts=[bd], tm=128)
    grads["rwkv_ln_w"], grads["rwkv_ln_b"], grads["rwkv_r_k"] = d_lnw, d_lnb, d_rk
    d_r2, d_w, d_k2b, d_kk, d_b, d_v2, exchanged = rwkv_scan_bwd(hr, decay, k2, kk, b, vT, _to_colblocks(d_y), ckpt,
                                                                  scatter_src)
    pre_params = [W["rwkv_w0"], W["rwkv_w2"], W["rwkv_a0"], W["rwkv_a2"], W["rwkv_g2"], W["rwkv_k_k"],
                  W["rwkv_k_a"]]
    (d_hk, d_hw, d_ha, d_hg), dpre = rowwise_vjp(
        _f_rwkv_pre, [hk, hw, ha, hg], pre_params, [d_w, d_k2a + d_k2b, d_kk, d_b, d_g], "rwkv_dpre",
        consts=[bd], tm=128)
    for nm, gval in zip(["rwkv_w0", "rwkv_w2", "rwkv_a0", "rwkv_a2", "rwkv_g2", "rwkv_k_k", "rwkv_k_a"], dpre):
        grads[nm] = gval
    d_h = jnp.concatenate([d_r1 + d_r2, d_hk, d_v1 + d_v2, d_hw, d_ha, d_hg], axis=1)
    (d_hb, d_sh), (d_mu,) = rowwise_vjp(_f_mix, [hb, _shift_down(hb)], [W["rwkv_mu"]], [d_h], "rwkv_dshift",
                                        tm=256)
    grads["rwkv_mu"] = d_mu
    d_hb = d_hb + _shift_up(d_sh)
    dproj = jnp.concatenate([dqa, dka, dva, d_hb], axis=1).astype(BF16)
    grads["even_w_in"] = matmul(hn, dproj, "tn", "emix_dwin", out_dtype=BF16)
    dhn = matmul(dproj, W["even_w_in"], "nt", "emix_dhn")
    (dx,), (dgn,) = rowwise_vjp(_f_rms_res, [x], [W["mix_norm0"]], [dhn, dy], "emix_drms")
    grads["mix_norm0"] = dgn
    return dx, grads, exchanged


def odd_mix_fwd(x, W):
    S = x.shape[0]
    hn, = rowwise(_f_rms, [x], [W["mix_norm1"]], [(D_MODEL, BF16)], "omix_rms")
    qkv = matmul(hn, W["fox_w_in"][:, :3 * FOX_DIM], "nn", "omix_in", out_dtype=BF16)
    fz = matmul(hn, W["fox_w_in"][:, 3 * FOX_DIM:], "nn", "omix_gate")
    q = _heads(qkv[:, :FOX_DIM], FOX_HEADS)
    k = _heads(qkv[:, FOX_DIM:2 * FOX_DIM], FOX_HEADS)
    v = _heads(qkv[:, 2 * FOX_DIM:], FOX_HEADS)
    logf, = rowwise(_f_logf, [fz], [W["fox_b_f"]], [(128, F32)], "fox_logf")
    c = seq_cumsum(logf, False, "fox_cumsum")[:, :FOX_HEADS]
    c_col = c.T.reshape(FOX_HEADS, S, 1)
    c_row = c.T.reshape(FOX_HEADS, 1, S)
    o, lse = fox_fwd(q, k, v, c_col, c_row)
    yc = _unheads(o).astype(BF16)
    out = matmul(yc, W["fox_w_out"], "nn", "omix_out", res=x)
    return out, (x, hn, q, k, v, fz, c_col, c_row, o, lse, yc)


def odd_mix_bwd(dy, saved, W):
    x, hn, q, k, v, fz, c_col, c_row, o, lse, yc = saved
    S = x.shape[0]
    grads = {}
    dyc = matmul(dy, W["fox_w_out"], "nt", "omix_dyc")
    grads["fox_w_out"] = matmul(yc, dy, "tn", "omix_dwout", out_dtype=BF16)
    do = _heads(dyc, FOX_HEADS)
    dq, drow, dk, dv, dcol = fox_bwd(q, k, v, c_col, c_row, o, lse, do)
    dc = (drow.reshape(FOX_HEADS, S) - dcol.reshape(FOX_HEADS, S)).T
    dc = jnp.pad(dc, ((0, 0), (0, 128 - FOX_HEADS)))
    dlogf = seq_cumsum(dc, True, "fox_rcumsum")
    (dfz,), (dbf,) = rowwise_vjp(_f_logf, [fz], [W["fox_b_f"]], [dlogf], "fox_dlogf")
    grads["fox_b_f"] = dbf
    dproj = jnp.concatenate([_unheads(dq), _unheads(dk), _unheads(dv), dfz], axis=1).astype(BF16)
    grads["fox_w_in"] = matmul(hn, dproj, "tn", "omix_dwin", out_dtype=BF16)
    dhn = matmul(dproj, W["fox_w_in"], "nt", "omix_dhn")
    (dx,), (dgn,) = rowwise_vjp(_f_rms_res, [x], [W["mix_norm1"]], [dhn, dy], "omix_drms")
    grads["mix_norm1"] = dgn
    return dx, grads


def device_step(x, p, target, W, gather_src, layer1_weights, layer1_grads, mixer_grads):
    W = dict(W)
    saved = []
    h = x
    for i in range(2):
        h, s1 = ffn_fwd(h, W[f"ffn1_norm{i}"], W[f"ffn1_w_gu{i}"], W[f"ffn1_w_down{i}"], f"ffn1_{i}")
        if i == 0:
            h, s2, late = even_mix_fwd(h, W, gather_src, layer1_weights)
            W.update(late)
        else:
            h, s2 = odd_mix_fwd(h, W)
        h, s3 = ffn_fwd(h, W[f"ffn2_norm{i}"], W[f"ffn2_w_gu{i}"], W[f"ffn2_w_down{i}"], f"ffn2_{i}")
        h, s4 = ple_fwd(h, p[i], W[f"ple_norm{i}"], W[f"ple_w_gate{i}"], W[f"ple_w_proj{i}"], f"ple_{i}")
        saved.append((s1, s2, s3, s4))
    dh, d_final, loss = loss_head(h, target, W["final_norm"])
    G = {"final_norm": d_final}
    for i in (1, 0):
        s1, s2, s3, s4 = saved[i]
        dh, G[f"ple_norm{i}"], G[f"ple_w_gate{i}"], G[f"ple_w_proj{i}"] = ple_bwd(
            dh, s4, p[i], W[f"ple_norm{i}"], W[f"ple_w_gate{i}"], f"ple_{i}")
        dh, G[f"ffn2_norm{i}"], G[f"ffn2_w_gu{i}"], G[f"ffn2_w_down{i}"], _ = ffn_bwd(
            dh, s3, W[f"ffn2_norm{i}"], W[f"ffn2_w_gu{i}"], W[f"ffn2_w_down{i}"], f"ffn2_{i}")
        if i == 0:
            dh, gm, exchanged = even_mix_bwd(dh, s2, W, layer1_grads(G))
        else:
            dh, gm = odd_mix_bwd(dh, s2, W)
        G.update(gm)
        dh, G[f"ffn1_norm{i}"], G[f"ffn1_w_gu{i}"], G[f"ffn1_w_down{i}"], exchanged_mid = ffn_bwd(
            dh, s1, W[f"ffn1_norm{i}"], W[f"ffn1_w_gu{i}"], W[f"ffn1_w_down{i}"], f"ffn1_{i}",
            scatter_srcs=mixer_grads(G) if i == 0 else ())
    return loss, dh, G, exchanged, exchanged_mid


_MESH = pl.DeviceIdType.MESH
_ANY = pl.BlockSpec(memory_space=pl.ANY)


def _exchange_sems(n):
    return [pltpu.SemaphoreType.DMA((7 * n,)), pltpu.SemaphoreType.DMA((7 * n,)), pltpu.SemaphoreType.DMA((n,))]


def all_gather(xs, name):
    n = len(xs)

    def body(*refs):
        x_refs, out_refs = refs[:n], refs[n:2 * n]
        send_sems, recv_sems, local_sems = refs[2 * n:]
        x_, y_, c_ = lax.axis_index("x"), lax.axis_index("y"), lax.axis_index("c")
        me, sibling = (x_, y_, c_), (x_, y_, 1 - c_)
        chips = [(1 - x_, y_), (x_, 1 - y_), (1 - x_, 1 - y_)]

        def copy(b, k, block, to, from_input=False):
            slot = out_refs[b].at[4 * block[0] + 2 * block[1] + block[2]]
            return pltpu.make_async_remote_copy(
                src_ref=x_refs[b] if from_input else slot, dst_ref=slot,
                send_sem=send_sems.at[7 * b + k], recv_sem=recv_sems.at[7 * b + k], device_id=to,
                device_id_type=_MESH)

        bufs = range(n)
        mine = [pltpu.make_async_copy(x_refs[b], out_refs[b].at[4 * x_ + 2 * y_ + c_], local_sems.at[b]) for b in bufs]
        first = [copy(b, 0, me, sibling, True) for b in bufs]
        first += [copy(b, 1 + j, me, (*chip, c_), True) for j, chip in enumerate(chips) for b in bufs]
        for cp in mine + first:
            cp.start()
        passed = []
        for j, chip in enumerate(chips):
            for b in bufs:
                copy(b, 1 + j, (*chip, c_), me).wait_recv()
                passed.append(copy(b, 4 + j, (*chip, c_), sibling))
                passed[-1].start()
        for b in bufs:
            copy(b, 0, sibling, me).wait_recv()
            for j, chip in enumerate(chips):
                copy(b, 4 + j, (*chip, 1 - c_), me).wait_recv()
        for cp in first + passed:
            cp.wait_send()
        for cp in mine:
            cp.wait()

    return pl.pallas_call(
        body,
        out_shape=[jax.ShapeDtypeStruct((N_DEV,) + x.shape, x.dtype) for x in xs],
        in_specs=[_ANY] * n,
        out_specs=[_ANY] * n,
        scratch_shapes=_exchange_sems(n),
        name=name,
    )(*xs)


def _direct_exchange(gather, s_refs, r_refs, send_sems, recv_sems, local_sems):
    x_, y_, c_ = lax.axis_index("x"), lax.axis_index("y"), lax.axis_index("c")
    my = 4 * x_ + 2 * y_ + c_
    copies = []
    for b, (s_ref, r_ref) in enumerate(zip(s_refs, r_refs)):
        copies.append(pltpu.make_async_copy(s_ref if gather else s_ref.at[my], r_ref.at[my], local_sems.at[b]))
        for m in range(1, N_DEV):
            px = 1 - x_ if (m >> 2) & 1 else x_
            py = 1 - y_ if (m >> 1) & 1 else y_
            pc = 1 - c_ if m & 1 else c_
            copies.append(pltpu.make_async_remote_copy(
                src_ref=s_ref if gather else s_ref.at[4 * px + 2 * py + pc], dst_ref=r_ref.at[my],
                send_sem=send_sems.at[7 * b + m - 1], recv_sem=recv_sems.at[7 * b + m - 1],
                device_id=(px, py, pc), device_id_type=_MESH))
    return copies


def _exchange_during(step, n_steps, gather, s_refs, r_refs, send_sems, recv_sems, local_sems):
    copies = _direct_exchange(gather, s_refs, r_refs, send_sems, recv_sems, local_sems)

    @pl.when(step == 0)
    def _():
        for cp in copies:
            cp.start()

    @pl.when(step == n_steps - 1)
    def _():
        for cp in copies:
            cp.wait()


def _exchange_out_shapes(gather, srcs):
    return [jax.ShapeDtypeStruct(((N_DEV,) + s.shape) if gather else s.shape, s.dtype) for s in srcs]


def all_to_all(sends, name):
    n = len(sends)

    def body(*refs):
        copies = _direct_exchange(False, refs[:n], refs[n:2 * n], *refs[2 * n:])
        for cp in copies:
            cp.start()
        for cp in copies:
            cp.wait()

    return pl.pallas_call(
        body,
        out_shape=_exchange_out_shapes(False, sends),
        in_specs=[_ANY] * n,
        out_specs=[_ANY] * n,
        scratch_shapes=_exchange_sems(n),
        name=name,
    )(*sends)


def adamw(w, m, v, parts, name, tm=256):
    R, C = w.shape
    tm = _pick(R, tm, 8) if R >= 8 else R

    def body(w_ref, m_ref, v_ref, p_ref, g_ref, d_ref, nm_ref, nv_ref):
        g = p_ref[0].astype(F32)
        for s in range(1, N_DEV):
            g = g + p_ref[s].astype(F32)
        nm = ADAM_B1 * m_ref[...] + (1.0 - ADAM_B1) * g
        nv = ADAM_B2 * v_ref[...] + (1.0 - ADAM_B2) * (g * g)
        m_hat = nm / (1.0 - ADAM_B1 ** ADAM_STEP)
        v_hat = nv / (1.0 - ADAM_B2 ** ADAM_STEP)
        g_ref[...] = g
        d_ref[...] = -ADAM_LR * (m_hat / (jnp.sqrt(v_hat) + ADAM_EPS) + ADAM_WD * w_ref[...])
        nm_ref[...] = nm
        nv_ref[...] = nv

    row = pl.BlockSpec((tm, C), lambda i: (i, 0))
    out = jax.ShapeDtypeStruct((R, C), F32)
    return pl.pallas_call(
        body,
        grid=(R // tm,),
        in_specs=[row, row, row, pl.BlockSpec((N_DEV, tm, C), lambda i: (0, i, 0))],
        out_specs=[row] * 4,
        out_shape=[out] * 4,
        compiler_params=_cparams(("parallel",)),
        name=name,
    )(w, m, v, parts)


_WEIGHTS = ["ffn1_norm", "ffn1_w_gu", "ffn1_w_down", "mix_norm", "ffn2_norm", "ffn2_w_gu", "ffn2_w_down",
            "ple_norm", "ple_w_gate", "ple_w_proj", "even_w_in", "even_w_out", "swa_sinks", "rwkv_mu",
            "rwkv_w0", "rwkv_w2", "rwkv_a0", "rwkv_a2", "rwkv_g2", "rwkv_k_k", "rwkv_k_a", "rwkv_r_k",
            "rwkv_ln_w", "rwkv_ln_b", "fox_w_in", "fox_b_f", "fox_w_out", "final_norm"]
_SHARD_AXIS = {"ffn1_w_gu": 2, "ffn1_w_down": 1, "ffn2_w_gu": 2, "ffn2_w_down": 1, "ple_w_gate": 1,
               "ple_w_proj": 2, "even_w_in": 2, "even_w_out": 1, "rwkv_w2": 2, "rwkv_a2": 2, "rwkv_g2": 2,
               "fox_w_in": 2, "fox_w_out": 1}
_SHARDED = [n for n in _WEIGHTS if n in _SHARD_AXIS]
_REPLICATED = [n for n in _WEIGHTS if n not in _SHARD_AXIS]
_PER_LAYER = ("ffn1_w_gu", "ffn1_w_down", "ffn2_w_gu", "ffn2_w_down", "ple_w_gate", "ple_w_proj")
_ALL_PIECES = ([(n, 0) for n in _PER_LAYER] + [(n, 0) for n in ("even_w_in", "even_w_out", "rwkv_w2", "rwkv_a2", "rwkv_g2")]
               + [(n, 1) for n in _PER_LAYER] + [("fox_w_in", 0), ("fox_w_out", 0)])
_FIRST_WEIGHTS = [(n, 0) for n in ("ffn1_w_gu", "ffn1_w_down", "even_w_in", "rwkv_w2", "rwkv_a2", "rwkv_g2")]
_PIECES = [_FIRST_WEIGHTS, [pc for pc in _ALL_PIECES if pc not in _FIRST_WEIGHTS]]
_LATE_GRADS = _FIRST_WEIGHTS + [("even_w_out", 0)]
_LAST_GRADS = [("ffn1_w_gu", 0), ("ffn1_w_down", 0)]
_GRAD_PIECES = [_LAST_GRADS, [pc for pc in _LATE_GRADS if pc not in _LAST_GRADS],
                [pc for pc in _ALL_PIECES if pc not in _LATE_GRADS]]
_PACK_LANES = 1024
_PACK_ROW_TILE = 256


def _piece_key(piece):
    name, idx = piece
    return f"{name}{idx}" if name in _PER_LAYER else name


_KINDS = ("gu", "rows", "misc")


def _kind(piece):
    if piece[0] in ("ffn1_w_gu", "ffn2_w_gu"):
        return "gu"
    return "rows" if _SHARD_AXIS[piece[0]] == 1 else "misc"


def _of_kind(pieces, shapes, kind):
    return [(pc, shp) for pc, shp in zip(pieces, shapes) if _kind(pc) == kind]


def _pad_rows(flat, axis):
    pad = [(0, 0)] * flat.ndim
    pad[axis] = (0, -flat.shape[axis] % _PACK_ROW_TILE)
    return jnp.pad(flat, pad)


def _kinds_of(pieces):
    return [kind for kind in _KINDS if any(_kind(pc) == kind for pc in pieces)]


def _bundle(get, pieces, dtype):
    make = {"gu": jnp.stack,
            "rows": lambda ps: jnp.concatenate(ps, axis=0),
            "misc": lambda ps: _pad_rows(jnp.concatenate([a.reshape(-1, _PACK_LANES) for a in ps], axis=0), 0)}
    return [make[kind]([get(pc).astype(dtype) for pc in pieces if _kind(pc) == kind]) for kind in _kinds_of(pieces)]


def _unbundle(bufs, pieces, shapes):
    out = {}
    for buf, kind in zip(bufs, _kinds_of(pieces)):
        of_kind = _of_kind(pieces, shapes, kind)
        if kind == "gu":
            stacked = buf.reshape((len(of_kind),) + of_kind[0][1])
            for j, (pc, _) in enumerate(of_kind):
                out[pc] = stacked[j]
            continue
        r0 = 0
        for pc, shp in of_kind:
            n = math.prod(shp) // _PACK_LANES
            out[pc] = buf[r0:r0 + n].reshape(shp)
            r0 += n
    return out


def _unshard(gathered, pieces, shapes):
    full = {}
    for j, (pc, shp) in enumerate(_of_kind(pieces, shapes, "gu")):
        full[_piece_key(pc)] = jnp.moveaxis(gathered[0][:, j], 0, 1).reshape(shp[0], N_DEV * shp[1])
    r0 = 0
    for pc, shp in _of_kind(pieces, shapes, "rows"):
        full[_piece_key(pc)] = gathered[1][:, r0:r0 + shp[0]].reshape(N_DEV * shp[0], shp[1])
        r0 += shp[0]
    r0 = 0
    for pc, shp in _of_kind(pieces, shapes, "misc"):
        n = math.prod(shp) // _PACK_LANES
        seg = gathered[2][:, r0:r0 + n].reshape((N_DEV,) + shp)
        full[_piece_key(pc)] = jnp.moveaxis(seg, 0, 1).reshape(shp[0], N_DEV * shp[1])
        r0 += n
    return full


def _to_shards(full, pieces, shapes):
    get = lambda pc: full[_piece_key(pc)].astype(BF16)
    cols = lambda pc, shp: jnp.moveaxis(get(pc).reshape(shp[0], N_DEV, shp[1]), 1, 0)
    make = {"gu": lambda ps: jnp.stack([cols(pc, shp) for pc, shp in ps], axis=1),
            "rows": lambda ps: jnp.concatenate([get(pc).reshape((N_DEV,) + shp) for pc, shp in ps], axis=1),
            "misc": lambda ps: _pad_rows(jnp.concatenate([cols(pc, shp).reshape(N_DEV, -1, _PACK_LANES)
                                                          for pc, shp in ps], axis=1), 1)}
    return [make[kind](_of_kind(pieces, shapes, kind)) for kind in _kinds_of(pieces)]


def _layer_weights(full):
    W = dict(full)
    if "fox_w_in" in W:
        W["fox_w_in"] = jnp.pad(W["fox_w_in"], ((0, 0), (0, FOX_IN_PAD - W["fox_w_in"].shape[1])))
    for n in ("rwkv_w2", "rwkv_a2", "rwkv_g2"):
        if n in W:
            W[n] = W[n].astype(F32)
    return W


def _pack_small(vals):
    flat = jnp.concatenate([v.reshape(1, -1) for v in vals], axis=1)
    n = flat.shape[1]
    return jnp.pad(flat, ((0, 0), (0, -n % 128)))


def _unpack_small(flat, shapes):
    out, c0 = [], 0
    for shp in shapes:
        n = math.prod(shp)
        out.append(flat[0, c0:c0 + n].reshape(shp))
        c0 += n
    return out


def kernel(x, p, ffn1_norm, ffn1_w_gu, ffn1_w_down, mix_norm, ffn2_norm, ffn2_w_gu, ffn2_w_down, ple_norm, ple_w_gate, ple_w_proj, even_w_in, even_w_out, swa_sinks, rwkv_mu, rwkv_w0, rwkv_w2, rwkv_a0, rwkv_a2, rwkv_g2, rwkv_k_k, rwkv_k_a, rwkv_r_k, rwkv_ln_w, rwkv_ln_b, fox_w_in, fox_b_f, fox_w_out, final_norm, loss_target, m_ffn1_norm, m_ffn1_w_gu, m_ffn1_w_down, m_mix_norm, m_ffn2_norm, m_ffn2_w_gu, m_ffn2_w_down, m_ple_norm, m_ple_w_gate, m_ple_w_proj, m_even_w_in, m_even_w_out, m_swa_sinks, m_rwkv_mu, m_rwkv_w0, m_rwkv_w2, m_rwkv_a0, m_rwkv_a2, m_rwkv_g2, m_rwkv_k_k, m_rwkv_k_a, m_rwkv_r_k, m_rwkv_ln_w, m_rwkv_ln_b, m_fox_w_in, m_fox_b_f, m_fox_w_out, m_final_norm, v_ffn1_norm, v_ffn1_w_gu, v_ffn1_w_down, v_mix_norm, v_ffn2_norm, v_ffn2_w_gu, v_ffn2_w_down, v_ple_norm, v_ple_w_gate, v_ple_w_proj, v_even_w_in, v_even_w_out, v_swa_sinks, v_rwkv_mu, v_rwkv_w0, v_rwkv_w2, v_rwkv_a0, v_rwkv_a2, v_rwkv_g2, v_rwkv_k_k, v_rwkv_k_a, v_rwkv_r_k, v_rwkv_ln_w, v_rwkv_ln_b, v_fox_w_in, v_fox_b_f, v_fox_w_out, v_final_norm):
    given = dict(locals())
    w = {n: given[n] for n in _WEIGHTS}
    m = {n: given["m_" + n] for n in _WEIGHTS}
    v = {n: given["v_" + n] for n in _WEIGHTS}
    small_shapes = [w[n].shape for n in _REPLICATED]
    piece = lambda d, pc: d[pc[0]][pc[1]]
    shapes = [[piece(w, pc).shape for pc in pieces] for pieces in _PIECES]
    gshapes = [[piece(w, pc).shape for pc in pieces] for pieces in _GRAD_PIECES]
    w_send = [_bundle(lambda pc: piece(w, pc), pieces, BF16) for pieces in _PIECES]

    W = _layer_weights(_unshard(all_gather(w_send[0], "weights_all_gather"), _PIECES[0], shapes[0]))
    for i in range(2):
        for n in ("ffn1_norm", "mix_norm", "ffn2_norm", "ple_norm"):
            W[f"{n}{i}"] = w[n][i].reshape(1, -1)
    for n in ("swa_sinks", "rwkv_mu", "rwkv_w0", "rwkv_a0", "rwkv_k_k", "rwkv_k_a", "rwkv_r_k", "rwkv_ln_w",
              "rwkv_ln_b", "final_norm"):
        W[n] = w[n].reshape(1, -1)
    n_f = fox_b_f.shape[1]
    W["fox_b_f"] = jnp.pad(fox_b_f.reshape(1, n_f), ((0, 0), (0, 128 - n_f)))
    n_fox = fox_w_in.shape[2] * N_DEV

    def layer1_weights(gathered):
        return _layer_weights(_unshard(gathered, _PIECES[1], shapes[1]))

    def early_grads(G):
        G = dict(G, fox_w_in=G["fox_w_in"][:, :n_fox])
        return _to_shards(G, _GRAD_PIECES[2], gshapes[2])

    def mixer_grads(G):
        return _to_shards(G, _GRAD_PIECES[1], gshapes[1])

    loss_row, dx, G, parts_early, parts_mixer = device_step(x[0], p[:, 0], loss_target[0], W, w_send[1],
                                                            layer1_weights, early_grads, mixer_grads)

    parts = [all_to_all(_to_shards(G, _GRAD_PIECES[0], gshapes[0]), "grads_all_to_all"), parts_mixer, parts_early]
    out_g, out_d, out_m, out_v = {}, {}, {}, {}
    rows2d = lambda a, lead: a.reshape(a.shape[:lead] + (-1, a.shape[-1]))
    for li, pieces in enumerate(_GRAD_PIECES):
        wmv = [_bundle(lambda pc, d=d: piece(d, pc), pieces, F32) for d in (w, m, v)]
        res = [adamw(*[rows2d(b[ki], 0) for b in wmv], rows2d(parts[li][ki], 1), f"adamw_{kind}{li}")
               for ki, kind in enumerate(_kinds_of(pieces))]
        for oi, out in enumerate((out_g, out_d, out_m, out_v)):
            for pc, a in _unbundle([r[oi] for r in res], pieces, gshapes[li]).items():
                out.setdefault(pc[0], {})[pc[1]] = a
    for out in (out_g, out_d, out_m, out_v):
        for n in _SHARDED:
            out[n] = jnp.stack([out[n][i] for i in sorted(out[n])])

    gsmall = {}
    for n in ("ffn1_norm", "mix_norm", "ffn2_norm", "ple_norm"):
        gsmall[n] = jnp.concatenate([G[f"{n}0"], G[f"{n}1"]], axis=0)
    for n in ("swa_sinks", "rwkv_mu", "rwkv_w0", "rwkv_a0", "rwkv_k_k", "rwkv_k_a", "rwkv_r_k", "rwkv_ln_w",
              "rwkv_ln_b", "final_norm"):
        gsmall[n] = G[n]
    gsmall["fox_b_f"] = G["fox_b_f"][:, :n_f]
    small = _pack_small([gsmall[n] for n in _REPLICATED] + [loss_row[:, :1]])
    small_parts = all_gather([small], "small_all_gather")[0]
    pad1 = lambda vals: _pack_small(vals + [jnp.zeros((1, 1), F32)])
    gs, ds, nms, nvs = adamw(pad1([w[n] for n in _REPLICATED]), pad1([m[n] for n in _REPLICATED]),
                             pad1([v[n] for n in _REPLICATED]), small_parts, "adamw_replicated")
    out_g.update(zip(_REPLICATED, _unpack_small(gs, small_shapes)))
    out_d.update(zip(_REPLICATED, _unpack_small(ds, small_shapes)))
    out_m.update(zip(_REPLICATED, _unpack_small(nms, small_shapes)))
    out_v.update(zip(_REPLICATED, _unpack_small(nvs, small_shapes)))
    n_small = sum(math.prod(s) for s in small_shapes)
    loss = gs[0, n_small]

    return (loss, dx[None], *[out_g[n] for n in _WEIGHTS], *[out_d[n] for n in _WEIGHTS],
            *[out_m[n] for n in _WEIGHTS], *[out_v[n] for n in _WEIGHTS])
```

```python
import functools
import math

import numpy as np
import jax
import jax.numpy as jnp
from jax import lax
from jax.experimental import pallas as pl
from jax.experimental.pallas import tpu as pltpu

F32 = jnp.float32
BF16 = jnp.bfloat16

D_MODEL = 1024
HEAD_DIM = 64
BLOCK = 128
SWA_HEADS = 8
SWA_KV_HEADS = 2
SWA_GROUP = 4
RWKV_HEADS = 8
RWKV_DIM = 512
FOX_HEADS = 16
FOX_DIM = 1024
D_FF = 2816
NORM_EPS = 1e-6
GN_EPS = 64e-5
L2_EPS = 1e-12
SWA_Q = 512
SWA_KV = 128
SWA_COLS = 768
FOX_IN_PAD = 3200
N_DEV = 8
ADAM_LR = 0.001
ADAM_B1 = 0.9
ADAM_B2 = 0.999
ADAM_EPS = 1e-08
ADAM_WD = 0.01
ADAM_STEP = 10

V7X_VMEM_LIMIT = 56 * 1024 * 1024
FOX_TILE = 512
CUMSUM_BLOCK = 256
SCAN_GROUP = 8
SCAN_CHUNK = 32

_NN = (((1,), (0,)), ((), ()))
_NT = (((1,), (1,)), ((), ()))
_TN = (((0,), (0,)), ((), ()))
_DIMS = {"nn": _NN, "nt": _NT, "tn": _TN}


def _pick(n, target, mult=128):
    best = None
    for t in range(mult, min(n, target) + 1, mult):
        if n % t == 0:
            best = t
    return best or n


def _cparams(sem):
    return pltpu.CompilerParams(dimension_semantics=sem, vmem_limit_bytes=V7X_VMEM_LIMIT)


def _dot(a, b, dims):
    return lax.dot_general(a.astype(BF16), b.astype(BF16), dims, preferred_element_type=F32)


@jax.custom_vjp
def bdot(a, b):
    return _dot(a, b, _NN)


def _bdot_fwd(a, b):
    return _dot(a, b, _NN), (a, b)


def _bdot_bwd(res, g):
    a, b = res
    return _dot(g, b, _NT), _dot(a, g, _TN)


bdot.defvjp(_bdot_fwd, _bdot_bwd)


@jax.custom_vjp
def bdot_nt(a, b):
    return _dot(a, b, _NT)


def _bdot_nt_fwd(a, b):
    return _dot(a, b, _NT), (a, b)


def _bdot_nt_bwd(res, g):
    a, b = res
    return _dot(g, b, _NN), _dot(g, a, _TN)


bdot_nt.defvjp(_bdot_nt_fwd, _bdot_nt_bwd)


@jax.custom_vjp
def _segsum(x, bd):
    return _dot2(x, bd.astype(BF16))


def _segsum_fwd(x, bd):
    return _segsum(x, bd), bd


def _segsum_bwd(bd, g):
    return _dot2(g, bd.astype(BF16)), jnp.zeros_like(bd)


_segsum.defvjp(_segsum_fwd, _segsum_bwd)


def _sigmoid(x):
    return 1.0 / (1.0 + jnp.exp(-x))


def _sigmoid_tanh(x):
    return 0.5 * jnp.tanh(0.5 * x) + 0.5


def _softplus(x):
    return jnp.maximum(x, 0.0) + jnp.log(1.0 + jnp.exp(-jnp.abs(x)))


def matmul(a, b, mode, name, out_dtype=F32, scale=1.0, res=None, tm=1024, tn=1408, tk=1024):
    if mode == "nn":
        (M, K), (K2, N) = a.shape, b.shape
    elif mode == "nt":
        (M, K), (N, K2) = a.shape, b.shape
    else:
        (K, M), (K2, N) = a.shape, b.shape
    assert K == K2, (a.shape, b.shape, mode)
    tm, tn, tk = _pick(M, tm), _pick(N, tn), _pick(K, tk)
    nk = K // tk
    has_res = res is not None

    def body(*refs):
        if has_res:
            a_ref, b_ref, r_ref, o_ref, acc = refs
        else:
            a_ref, b_ref, o_ref, acc = refs
        kk = pl.program_id(2)

        @pl.when(kk == 0)
        def _():
            acc[...] = jnp.zeros_like(acc)

        acc[...] += _dot(a_ref[...], b_ref[...], _DIMS[mode])

        @pl.when(kk == nk - 1)
        def _():
            v = acc[...]
            if scale != 1.0:
                v = v * scale
            if has_res:
                v = v + r_ref[...].astype(F32)
            o_ref[...] = v.astype(out_dtype)

    if mode == "tn":
        a_spec = pl.BlockSpec((tk, tm), lambda i, j, k: (k, i))
    else:
        a_spec = pl.BlockSpec((tm, tk), lambda i, j, k: (i, k))
    if mode == "nt":
        b_spec = pl.BlockSpec((tn, tk), lambda i, j, k: (j, k))
    else:
        b_spec = pl.BlockSpec((tk, tn), lambda i, j, k: (k, j))
    o_spec = pl.BlockSpec((tm, tn), lambda i, j, k: (i, j))
    in_specs = [a_spec, b_spec] + ([o_spec] if has_res else [])
    args = (a, b) + ((res,) if has_res else ())
    return pl.pallas_call(
        body,
        grid=(M // tm, N // tn, nk),
        in_specs=in_specs,
        out_specs=o_spec,
        out_shape=jax.ShapeDtypeStruct((M, N), out_dtype),
        scratch_shapes=[pltpu.VMEM((tm, tn), F32)],
        compiler_params=_cparams(("parallel", "parallel", "arbitrary")),
        name=name,
    )(*args)


def _row_spec(r, tm):
    if isinstance(r, tuple):
        arr, width, blk = r
        return arr, pl.BlockSpec((tm, width), lambda i, blk=blk: (i, blk))
    return r, pl.BlockSpec((tm, r.shape[1]), lambda i: (i, 0))


def _whole_spec(p):
    return pl.BlockSpec(p.shape, lambda i: (0,) * p.ndim)


def rowwise(fn, rows, params, outs, name, tm=512):
    arrs, specs = zip(*[_row_spec(r, tm) for r in rows])
    S = arrs[0].shape[0]
    tm = min(tm, S)
    arrs, specs = zip(*[_row_spec(r, tm) for r in rows])
    n_in = len(rows) + len(params)

    def body(*refs):
        res = fn(*[r[...] for r in refs[:n_in]])
        for o_ref, v in zip(refs[n_in:], res):
            o_ref[...] = v.astype(o_ref.dtype)

    return pl.pallas_call(
        body,
        grid=(S // tm,),
        in_specs=list(specs) + [_whole_spec(p) for p in params],
        out_specs=[pl.BlockSpec((tm, c), lambda i: (i, 0)) for c, _ in outs],
        out_shape=[jax.ShapeDtypeStruct((S, c), dt) for c, dt in outs],
        compiler_params=_cparams(("parallel",)),
        name=name,
    )(*arrs, *params)


def rowwise_vjp(fn, rows, params, cots, name, need=None, row_dtype=F32, consts=(), tm=512):
    nr, npar, nc, nk = len(rows), len(params), len(cots), len(consts)
    need = [True] * nr if need is None else need
    arrs, _ = zip(*[_row_spec(r, tm) for r in rows])
    S = arrs[0].shape[0]
    tm = min(tm, S)
    arrs, specs = zip(*[_row_spec(r, tm) for r in rows])
    carrs, cspecs = zip(*[_row_spec(c, tm) for c in cots])
    widths = [s.block_shape[1] for s in specs]
    n_in = nr + npar + nk + nc

    def body(*refs):
        i = pl.program_id(0)
        xs = [r[...].astype(F32) for r in refs[:nr]]
        ps = [r[...] for r in refs[nr:nr + npar]]
        ks = [r[...] for r in refs[nr + npar:nr + npar + nk]]
        cs = [r[...].astype(F32) for r in refs[nr + npar + nk:n_in]]
        outs, vjp = jax.vjp(lambda *a: fn(*a, *ks), *xs, *ps)
        grads = vjp(tuple(cs))
        o = n_in
        for j in range(nr):
            if need[j]:
                refs[o][...] = grads[j].astype(refs[o].dtype)
                o += 1
        for j in range(npar):
            g_ref = refs[o + j]

            @pl.when(i == 0)
            def _(g_ref=g_ref):
                g_ref[...] = jnp.zeros_like(g_ref)

            g_ref[...] += grads[nr + j]

    out_specs = [pl.BlockSpec((tm, w), lambda i: (i, 0)) for w, nd in zip(widths, need) if nd]
    out_shape = [jax.ShapeDtypeStruct((S, w), row_dtype) for w, nd in zip(widths, need) if nd]
    out_specs += [_whole_spec(p) for p in params]
    out_shape += [jax.ShapeDtypeStruct(p.shape, F32) for p in params]
    res = pl.pallas_call(
        body,
        grid=(S // tm,),
        in_specs=list(specs) + [_whole_spec(p) for p in params] + [_whole_spec(k) for k in consts] + list(cspecs),
        out_specs=out_specs,
        out_shape=out_shape,
        compiler_params=_cparams(("arbitrary",)),
        name=name,
    )(*arrs, *params, *consts, *carrs)
    nrow = sum(need)
    return list(res[:nrow]), list(res[nrow:])


def _rms(x, g):
    return x * lax.rsqrt(jnp.mean(x * x, axis=-1, keepdims=True) + NORM_EPS) * g


def _f_rms(x, g):
    return (_rms(x, g),)


def _f_rms_res(x, g):
    return _rms(x, g), x


def _f_ple(x, z, pp):
    return (x + _sigmoid(z) * pp,)


def _f_mix(h, sh, mu):
    return (h + (sh - h) * mu,)


def _f_logf(fz, bf):
    return (-_softplus(-(fz + bf)),)


def _f_rwkv_pre(hk, hw, ha, hg, w0, w2, a0, a2, g2, k_k, k_a, bd):
    wlog = -_softplus(-(w0 + bdot(jnp.tanh(hw), w2))) - 0.5
    a = _sigmoid(a0 + bdot(ha, a2))
    g = bdot(_sigmoid(hg), g2)
    kk = hk * k_k
    kk = kk / jnp.maximum(jnp.sqrt(_segsum(kk * kk, bd)), L2_EPS)
    k2 = hk * (1.0 + (a - 1.0) * k_a)
    decay = jnp.exp(-jnp.exp(wlog))
    return decay, k2, kk, kk * a, g


def _f_rwkv_post(y, r, k2, v, g, ln_w, ln_b, r_k, bd):
    mean = _segsum(y, bd) * (1.0 / HEAD_DIM)
    d = y - mean
    var = _segsum(d * d, bd) * (1.0 / HEAD_DIM)
    yn = d * lax.rsqrt(var + GN_EPS) * ln_w + ln_b
    yn = yn + _segsum(r * k2 * r_k, bd) * v
    return (yn * g,)


def loss_head(x, target, gf, tm=256):
    S, D = x.shape
    tm = min(tm, S)

    def f(xt, g, tt):
        err = _rms(xt, g) - tt
        return 0.5 * jnp.sum(err * err) * (1.0 / D)

    def body(x_ref, t_ref, g_ref, dx_ref, dg_ref, l_ref):
        i = pl.program_id(0)
        val, (dx, dg) = jax.value_and_grad(f, argnums=(0, 1))(x_ref[...], g_ref[...], t_ref[...])

        @pl.when(i == 0)
        def _():
            dg_ref[...] = jnp.zeros_like(dg_ref)
            l_ref[...] = jnp.zeros_like(l_ref)

        dx_ref[...] = dx
        dg_ref[...] += dg
        l_ref[...] += jnp.full(l_ref.shape, val, F32)

    row = pl.BlockSpec((tm, D), lambda i: (i, 0))
    vec = pl.BlockSpec((1, D), lambda i: (0, 0))
    return pl.pallas_call(
        body,
        grid=(S // tm,),
        in_specs=[row, row, vec],
        out_specs=[row, vec, pl.BlockSpec((1, 128), lambda i: (0, 0))],
        out_shape=[jax.ShapeDtypeStruct((S, D), F32), jax.ShapeDtypeStruct((1, D), F32),
                   jax.ShapeDtypeStruct((1, 128), F32)],
        compiler_params=_cparams(("arbitrary",)),
        name="loss_head",
    )(x, target, gf)


def _swa_block(q, kp, kc, vp, vc, sink, slope, n):
    k = jnp.concatenate([kp, kc], axis=0)
    v = jnp.concatenate([vp, vc], axis=0)
    rows = q.shape[0]
    logits = bdot_nt(q, k) * (HEAD_DIM ** -0.5)
    qi = lax.broadcasted_iota(jnp.int32, (rows, 2 * BLOCK), 0) & (BLOCK - 1)
    ki = lax.broadcasted_iota(jnp.int32, (rows, 2 * BLOCK), 1)
    dist = qi + BLOCK - ki
    valid = (dist >= 0) & (dist < BLOCK) & ((n - 1) * BLOCK + ki >= 0)
    logits = logits - slope * dist.astype(F32)
    logits = jnp.where(valid, logits, -jnp.inf)
    m = jnp.maximum(jnp.max(logits, axis=-1, keepdims=True), sink)
    pr = jnp.exp(logits - m)
    denom = jnp.sum(pr, axis=-1, keepdims=True) + jnp.exp(sink - m)
    return bdot(pr / denom, v)


def _swa_specs(S):
    nb = S // BLOCK
    q_spec = pl.BlockSpec((None, SWA_GROUP, BLOCK, HEAD_DIM), lambda h, n: (h, 0, n, 0))
    kc_spec = pl.BlockSpec((None, BLOCK, HEAD_DIM), lambda h, n: (h, n, 0))
    kp_spec = pl.BlockSpec((None, BLOCK, HEAD_DIM), lambda h, n: (h, jnp.maximum(n - 1, 0), 0))
    col_spec = pl.BlockSpec((None, SWA_GROUP * BLOCK, 1), lambda h, n: (h, 0, 0))
    return nb, q_spec, kp_spec, kc_spec, col_spec


def swa_fwd(q, k, v, sink_col, slope_col):
    S = q.shape[2]
    nb, q_spec, kp_spec, kc_spec, col_spec = _swa_specs(S)

    def body(q_ref, kp_ref, kc_ref, vp_ref, vc_ref, s_ref, a_ref, o_ref):
        n = pl.program_id(1)
        qq = q_ref[...].reshape(SWA_GROUP * BLOCK, HEAD_DIM)
        out = _swa_block(qq, kp_ref[...], kc_ref[...], vp_ref[...], vc_ref[...], s_ref[...], a_ref[...], n)
        o_ref[...] = out.reshape(SWA_GROUP, BLOCK, HEAD_DIM)

    return pl.pallas_call(
        body,
        grid=(SWA_KV_HEADS, nb),
        in_specs=[q_spec, kp_spec, kc_spec, kp_spec, kc_spec, col_spec, col_spec],
        out_specs=q_spec,
        out_shape=jax.ShapeDtypeStruct(q.shape, F32),
        compiler_params=_cparams(("parallel", "parallel")),
        name="swa_fwd",
    )(q, k, k, v, v, sink_col, slope_col)


def swa_bwd(q, k, v, sink_col, slope_col, dout):
    S = q.shape[2]
    nb, q_spec, kp_spec, kc_spec, col_spec = _swa_specs(S)

    def body(q_ref, kp_ref, kc_ref, vp_ref, vc_ref, s_ref, a_ref, do_ref,
             dq_ref, dkp_ref, dkc_ref, dvp_ref, dvc_ref, ds_ref):
        n = pl.program_id(1)
        qq = q_ref[...].reshape(SWA_GROUP * BLOCK, HEAD_DIM)
        slope = a_ref[...]
        f = lambda a, b, c, d, e, s: _swa_block(a, b, c, d, e, s, slope, n)
        _, vjp = jax.vjp(f, qq, kp_ref[...], kc_ref[...], vp_ref[...], vc_ref[...], s_ref[...])
        dq, dkp, dkc, dvp, dvc, ds = vjp(do_ref[...].reshape(SWA_GROUP * BLOCK, HEAD_DIM))
        dq_ref[...] = dq.reshape(SWA_GROUP, BLOCK, HEAD_DIM)
        dkp_ref[...] = dkp
        dkc_ref[...] = dkc
        dvp_ref[...] = dvp
        dvc_ref[...] = dvc

        @pl.when(n == 0)
        def _():
            ds_ref[...] = jnp.zeros_like(ds_ref)

        ds_ref[...] += ds

    kv_shape = jax.ShapeDtypeStruct(k.shape, F32)
    return pl.pallas_call(
        body,
        grid=(SWA_KV_HEADS, nb),
        in_specs=[q_spec, kp_spec, kc_spec, kp_spec, kc_spec, col_spec, col_spec, q_spec],
        out_specs=[q_spec, kc_spec, kc_spec, kc_spec, kc_spec, col_spec],
        out_shape=[jax.ShapeDtypeStruct(q.shape, F32), kv_shape, kv_shape, kv_shape, kv_shape,
                   jax.ShapeDtypeStruct(sink_col.shape, F32)],
        compiler_params=_cparams(("parallel", "arbitrary")),
        name="swa_bwd",
    )(q, k, k, v, v, sink_col, slope_col, dout)


def _split2(x):
    hi = x.astype(BF16)
    return (x - hi.astype(F32)).astype(BF16), hi


def _dot2_many(xs, m, single=False):
    rows = xs[0].shape[0]
    if single:
        res = jnp.dot(jnp.concatenate([x.astype(BF16) for x in xs], axis=0), m, preferred_element_type=F32)
        return [res[i * rows:(i + 1) * rows] for i in range(len(xs))]
    res = jnp.dot(jnp.concatenate([p for x in xs for p in _split2(x)], axis=0), m, preferred_element_type=F32)
    return [res[(2 * i) * rows:(2 * i + 1) * rows] + res[(2 * i + 1) * rows:(2 * i + 2) * rows]
            for i in range(len(xs))]


def _dot2(x, m):
    return _dot2_many([x], m)[0]


def _seg_sums(xs, bd, single=False):
    w = bd.shape[0]
    halves = _dot2_many([x[:, i:i + w] for x in xs for i in range(0, x.shape[1], w)], bd, single)
    n = xs[0].shape[1] // w
    return [jnp.concatenate(halves[i * n:(i + 1) * n], axis=1) for i in range(len(xs))]


def _seg_sum(x, bd):
    return _seg_sums([x], bd)[0]


def _scan_consts():
    r = np.arange(256)
    bd = (r[:, None] // HEAD_DIM == r[None, :] // HEAD_DIM).astype(np.float32)
    c = np.arange(RWKV_DIM)
    e = (np.arange(HEAD_DIM)[:, None] // SCAN_GROUP == c[None, :] // HEAD_DIM).astype(np.float32)
    diag = (np.arange(HEAD_DIM)[:, None] == c[None, :] % HEAD_DIM).astype(np.float32)
    return jnp.asarray(bd, BF16), jnp.asarray(e, BF16), jnp.asarray(diag, F32)


def _to_colblocks(a):
    S = a.shape[0]
    a = a.reshape(S // SCAN_GROUP, SCAN_GROUP, RWKV_HEADS, HEAD_DIM)
    return a.transpose(0, 3, 2, 1).reshape(S // SCAN_GROUP, HEAD_DIM, RWKV_HEADS * SCAN_GROUP)


def _roll_up(rows):
    return pltpu.roll(rows, rows.shape[0] - 1, 0)


def _scan_pair_rows(aux, base, kk_ref, w_ref, b_ref, k_ref, bd):
    G = SCAN_GROUP
    kk_nx = _roll_up(kk_ref[pl.ds(base, G), :])
    aux[0] = w_ref[pl.ds(base, G), :] * kk_nx
    aux[1], aux[2] = _seg_sums([b_ref[pl.ds(base, G), :] * kk_nx, k_ref[pl.ds(base, G), :] * kk_nx], bd)


def _scan_pair(St, t0, base, col_g, lane_t, aux, kk_ref, w_ref, b_ref, k_ref, bd, e):
    t1 = t0 + 1
    row = lambda ref, t: ref[pl.ds(base + t, 1), :]
    arow = lambda i: aux[i, pl.ds(t0, 1), :]
    u0, m1 = _seg_sums([St * row(kk_ref, t0), St * arow(0)], bd)
    v0, v1 = _dot2_many([jnp.where(lane_t == t0, col_g, 0.0), jnp.where(lane_t == t1, col_g, 0.0)], e)
    u1 = m1 - u0 * arow(1) + v0 * arow(2)
    S0 = St * row(w_ref, t0) - u0 * row(b_ref, t0) + v0 * row(k_ref, t0)
    S1 = S0 * row(w_ref, t1) - u1 * row(b_ref, t1) + v1 * row(k_ref, t1)
    return (S0, S1), (u0, u1), (v0, v1)


def rwkv_scan_fwd(r, w, k, kk, b, vB, gather_srcs):
    S, C = r.shape
    N, G = HEAD_DIM, SCAN_GROUP
    chunk = min(SCAN_CHUNK, S)
    nchunk, ng = S // chunk, chunk // G
    bd, e, diag = _scan_consts()

    nx = len(gather_srcs)

    def body(*refs):
        r_ref, w_ref, k_ref, kk_ref, b_ref, vB_ref, bd_ref, e_ref, dg_ref = refs[:9]
        y_ref, ck_ref = refs[9 + nx:11 + nx]
        S_ref, aux, send_sems, recv_sems, local_sems = refs[11 + 2 * nx:]
        c = pl.program_id(0)
        _exchange_during(c, nchunk, True, refs[9:9 + nx], refs[11 + nx:11 + 2 * nx], send_sems, recv_sems, local_sems)

        @pl.when(c == 0)
        def _():
            S_ref[...] = jnp.zeros_like(S_ref)

        ck_ref[...] = S_ref[...]
        sub = lax.broadcasted_iota(jnp.int32, (G, C), 0)
        lane_t = lax.broadcasted_iota(jnp.int32, (N, N), 1) & (G - 1)

        def group(g, St):
            base = pl.multiple_of(g * G, G)
            vb = vB_ref[g]
            _scan_pair_rows(aux, base, kk_ref, w_ref, b_ref, k_ref, bd_ref[...])
            ys = jnp.zeros((G, C), F32)
            def emit(ys, states, t0):
                steps = (t0, t0 + 1)
                y_bs = _seg_sums([S_t * r_ref[pl.ds(base + tt, 1), :] for S_t, tt in zip(states, steps)], bd_ref[...],
                                 single=True)
                for y_b, tt in zip(y_bs, steps):
                    ys = jnp.where(sub == tt, jnp.sum(y_b * dg_ref[...], axis=0, keepdims=True), ys)
                return ys

            pending = None
            for t0 in range(0, G, 2):
                states, _, _ = _scan_pair(St, t0, base, vb, lane_t, aux, kk_ref, w_ref, b_ref, k_ref, bd_ref[...],
                                          e_ref[...])
                if pending is not None:
                    ys = emit(ys, *pending)
                pending = (states, t0)
                St = states[1]
            y_ref[pl.ds(base, G), :] = emit(ys, *pending)
            return St

        S_ref[...] = lax.fori_loop(0, ng, group, S_ref[...])

    row = pl.BlockSpec((chunk, C), lambda c: (c, 0))
    col = pl.BlockSpec((ng, N, N), lambda c: (c, 0, 0))
    res = pl.pallas_call(
        body,
        grid=(nchunk,),
        in_specs=[row] * 5 + [col, _whole_spec(bd), _whole_spec(e), _whole_spec(diag)] + [_ANY] * nx,
        out_specs=[row, pl.BlockSpec((None, N, C), lambda c: (c, 0, 0))] + [_ANY] * nx,
        out_shape=[jax.ShapeDtypeStruct((S, C), F32), jax.ShapeDtypeStruct((nchunk, N, C), F32)]
        + _exchange_out_shapes(True, gather_srcs),
        scratch_shapes=[pltpu.VMEM((N, C), F32), pltpu.VMEM((3, G, C), F32)] + _exchange_sems(nx),
        compiler_params=_cparams(("arbitrary",)),
        name="rwkv_scan_fwd",
    )(r, w, k, kk, b, vB, bd, e, diag, *gather_srcs)
    return res[0], res[1], list(res[2:])


def rwkv_scan_bwd(r, w, k, kk, b, vB, dyB, ckpt, scatter_srcs):
    S, C = r.shape
    N, G = HEAD_DIM, SCAN_GROUP
    chunk = min(SCAN_CHUNK, S)
    nchunk, ng = S // chunk, chunk // G
    nsteps = nchunk + 1
    bd, e, diag = _scan_consts()
    nx = len(scatter_srcs)

    def body(*refs):
        wf_ref, kf_ref, kkf_ref, bf_ref, vBf_ref, ck_ref = refs[:6]
        r_ref, w_ref, k_ref, kk_ref, b_ref, dyB_ref, bd_ref, e_ref, dg_ref = refs[6:15]
        dr_ref, dw_ref, dk_ref, dkk_ref, db_ref, dv_ref = refs[15 + nx:21 + nx]
        G_ref, sbuf, ubuf, vbuf, aux_f, aux_b, send_sems, recv_sems, local_sems = refs[21 + 2 * nx:]
        c = pl.program_id(0)
        _exchange_during(c, nsteps, False, refs[15:15 + nx], refs[21 + nx:21 + 2 * nx], send_sems, recv_sems,
                         local_sems)

        @pl.when(c == 0)
        def _():
            G_ref[...] = jnp.zeros_like(G_ref)
            sbuf[...] = jnp.zeros_like(sbuf)
            ubuf[...] = jnp.zeros_like(ubuf)
            vbuf[...] = jnp.zeros_like(vbuf)

        sf = c % 2
        sb = 1 - sf
        lane_t = lax.broadcasted_iota(jnp.int32, (N, N), 1) & (G - 1)
        sub = lax.broadcasted_iota(jnp.int32, (G, C), 0)
        colsum = lambda a: jnp.sum(a, axis=0, keepdims=True)

        def group(g, carry):
            St, Gt = carry
            base_f = pl.multiple_of(g * G, G)
            gb = ng - 1 - g
            base_b = pl.multiple_of(gb * G, G)
            vb, dyb = vBf_ref[g], dyB_ref[gb]
            row = lambda ref, t: ref[pl.ds(base_b + t, 1), :]
            _scan_pair_rows(aux_f, base_f, kkf_ref, wf_ref, bf_ref, kf_ref, bd_ref[...])
            b8 = b_ref[pl.ds(base_b, G), :]
            aux_b[0] = _roll_up(w_ref[pl.ds(base_b, G), :]) * b8
            aux_b[1], aux_b[2] = _seg_sums([_roll_up(kk_ref[pl.ds(base_b, G), :]) * b8,
                                            r_ref[pl.ds(base_b, G), :] * b8], bd_ref[...])
            rows = [jnp.zeros((G, C), F32) for _ in range(6)]

            def emit(rows, steps):
                d_vs = _seg_sums([Gt_ * row(k_ref, tt) for tt, Gt_, _, _ in steps], bd_ref[...], single=True)
                for (tt, Gt_, du_b, dy_b), d_vb in zip(steps, d_vs):
                    Sp, Sc = sbuf[sb, base_b + tt], sbuf[sb, base_b + tt + 1]
                    new = (colsum(Sc * dy_b), colsum(Gt_ * Sp), colsum(Gt_ * vbuf[sb, base_b + tt]),
                           colsum(Sp * du_b), -colsum(Gt_ * ubuf[sb, base_b + tt]), colsum(d_vb * dg_ref[...]))
                    rows = [jnp.where(sub == tt, n_, acc) for n_, acc in zip(new, rows)]
                return rows

            pending = None
            for i in range(G // 2):
                t0 = 2 * i
                states, us, vs = _scan_pair(St, t0, base_f, vb, lane_t, aux_f, kkf_ref, wf_ref, bf_ref, kf_ref,
                                            bd_ref[...], e_ref[...])
                for j, S_before in enumerate((St, states[0])):
                    sbuf[sf, base_f + t0 + j] = S_before
                    ubuf[sf, base_f + t0 + j] = us[j]
                    vbuf[sf, base_f + t0 + j] = vs[j]
                St = states[1]
                t0 = G - 2 - 2 * i
                t1 = t0 + 1
                arow = lambda j, t0=t0: aux_b[j, pl.ds(t0, 1), :]
                dy1, dy0 = _dot2_many([jnp.where(lane_t == t1, dyb, 0.0), jnp.where(lane_t == t0, dyb, 0.0)],
                                      e_ref[...])
                G1 = Gt + dy1 * row(r_ref, t1)
                m1, m2 = _seg_sums([G1 * row(b_ref, t1), G1 * arow(0)], bd_ref[...])
                du1 = -m1
                du0 = -(m2 + du1 * arow(1) + dy0 * arow(2))
                G0 = G1 * row(w_ref, t1) + du1 * row(kk_ref, t1) + dy0 * row(r_ref, t0)
                G_next = G0 * row(w_ref, t0) + du0 * row(kk_ref, t0)
                if pending is not None:
                    rows = emit(rows, pending)
                pending = ((t1, G1, du1, dy1), (t0, G0, du0, dy0))
                Gt = G_next
            rows = emit(rows, pending)
            for ref, val in zip((dr_ref, dw_ref, dk_ref, dkk_ref, db_ref, dv_ref), rows):
                ref[pl.ds(base_b, G), :] = val
            return St, Gt

        St, Gt = lax.fori_loop(0, ng, group, (ck_ref[...], G_ref[...]))
        sbuf[sf, chunk] = St
        G_ref[...] = jnp.where(c >= 1, Gt, G_ref[...])

    fwd_chunk = lambda c: jnp.maximum(nchunk - 1 - c, 0)
    bwd_chunk = lambda c: jnp.minimum(nchunk - c, nchunk - 1)
    row_f = pl.BlockSpec((chunk, C), lambda c: (fwd_chunk(c), 0))
    row_b = pl.BlockSpec((chunk, C), lambda c: (bwd_chunk(c), 0))
    col_f = pl.BlockSpec((ng, N, N), lambda c: (fwd_chunk(c), 0, 0))
    col_b = pl.BlockSpec((ng, N, N), lambda c: (bwd_chunk(c), 0, 0))
    rshape = jax.ShapeDtypeStruct((S, C), F32)
    res = pl.pallas_call(
        body,
        grid=(nsteps,),
        in_specs=[row_f] * 4 + [col_f, pl.BlockSpec((None, N, C), lambda c: (fwd_chunk(c), 0, 0))]
        + [row_b] * 5 + [col_b, _whole_spec(bd), _whole_spec(e), _whole_spec(diag)] + [_ANY] * nx,
        out_specs=[row_b] * 6 + [_ANY] * nx,
        out_shape=[rshape] * 6 + _exchange_out_shapes(False, scatter_srcs),
        scratch_shapes=[pltpu.VMEM((N, C), F32), pltpu.VMEM((2, chunk + 1, N, C), F32),
                        pltpu.VMEM((2, chunk, N, C), F32), pltpu.VMEM((2, chunk, N, C), F32),
                        pltpu.VMEM((3, G, C), F32), pltpu.VMEM((3, G, C), F32)] + _exchange_sems(nx),
        compiler_params=_cparams(("arbitrary",)),
        name="rwkv_scan_bwd",
    )(w, k, kk, b, vB, ckpt, r, w, k, kk, b, dyB, bd, e, diag, *scatter_srcs)
    return tuple(res[:6]) + (list(res[6:]),)


def seq_cumsum(x, reverse, name):
    S, C = x.shape
    tb = min(CUMSUM_BLOCK, S)
    nb = S // tb

    def body(x_ref, o_ref, carry):
        i = pl.program_id(0)

        @pl.when(i == 0)
        def _():
            carry[...] = jnp.zeros_like(carry)

        ri = lax.broadcasted_iota(jnp.int32, (tb, tb), 0)
        ci = lax.broadcasted_iota(jnp.int32, (tb, tb), 1)
        tri = jnp.where((ci >= ri) if reverse else (ci <= ri), 1.0, 0.0).astype(F32)
        xb = x_ref[...]
        out = jnp.dot(tri, xb, precision=lax.Precision.HIGHEST, preferred_element_type=F32) + carry[...]
        o_ref[...] = out
        carry[...] = carry[...] + jnp.sum(xb, axis=0, keepdims=True)

    idx = (lambda i: (nb - 1 - i, 0)) if reverse else (lambda i: (i, 0))
    return pl.pallas_call(
        body,
        grid=(nb,),
        in_specs=[pl.BlockSpec((tb, C), idx)],
        out_specs=pl.BlockSpec((tb, C), idx),
        out_shape=jax.ShapeDtypeStruct((S, C), F32),
        scratch_shapes=[pltpu.VMEM((1, C), F32)],
        compiler_params=_cparams(("arbitrary",)),
        name=name,
    )(x)


def _fox_logits(q, k, cq, ck, diagonal):
    s = _dot(q, k, _NT) * (HEAD_DIM ** -0.5) + cq - ck
    if not diagonal:
        return s
    row = lax.broadcasted_iota(jnp.int32, s.shape, 0)
    col = lax.broadcasted_iota(jnp.int32, s.shape, 1)
    return jnp.where(col <= row, s, -jnp.inf)


def _fox_tiles(n, by_query):
    pairs = [(i, j) for i in range(n) for j in range(i + 1)] if by_query else \
            [(i, j) for j in range(n) for i in range(j, n)]
    return (jnp.asarray(np.array([p[0] for p in pairs], np.int32)),
            jnp.asarray(np.array([p[1] for p in pairs], np.int32)))


def _fox_specs(t, Dh):
    qs = pl.BlockSpec((None, t, Dh), lambda h, s, qt, kt: (h, qt[s], 0))
    ks = pl.BlockSpec((None, t, Dh), lambda h, s, qt, kt: (h, kt[s], 0))
    cqs = pl.BlockSpec((None, t, 1), lambda h, s, qt, kt: (h, qt[s], 0))
    cks = pl.BlockSpec((None, 1, t), lambda h, s, qt, kt: (h, 0, kt[s]))
    return qs, ks, cqs, cks


def _fox_call(body, tiles, Hh, in_specs, out_specs, out_shape, scratch, name, args):
    spec = pltpu.PrefetchScalarGridSpec(num_scalar_prefetch=2, grid=(Hh, tiles[0].shape[0]), in_specs=in_specs,
                                        out_specs=out_specs, scratch_shapes=scratch)
    return pl.pallas_call(body, grid_spec=spec, out_shape=out_shape,
                          compiler_params=_cparams(("parallel", "arbitrary")), name=name)(*tiles, *args)


def fox_fwd(q, k, v, c_col, c_row):
    Hh, S, Dh = q.shape
    tq = tk = min(FOX_TILE, S)

    def body(qt_ref, kt_ref, q_ref, k_ref, v_ref, cq_ref, ck_ref, o_ref, lse_ref, m_s, l_s, acc_s):
        qi, ki = qt_ref[pl.program_id(1)], kt_ref[pl.program_id(1)]

        @pl.when(ki == 0)
        def _():
            m_s[...] = jnp.full_like(m_s, -jnp.inf)
            l_s[...] = jnp.zeros_like(l_s)
            acc_s[...] = jnp.zeros_like(acc_s)

        def tile(diagonal):
            s = _fox_logits(q_ref[...], k_ref[...], cq_ref[...], ck_ref[...], diagonal)
            m_old = m_s[...]
            m_new = jnp.maximum(m_old, jnp.max(s, axis=-1, keepdims=True))
            alpha = jnp.exp(m_old - m_new)
            p = jnp.exp(s - m_new)
            l_s[...] = alpha * l_s[...] + jnp.sum(p, axis=-1, keepdims=True)
            acc_s[...] = alpha * acc_s[...] + _dot(p, v_ref[...], _NN)
            m_s[...] = m_new

        @pl.when(ki != qi)
        def _():
            tile(False)

        @pl.when(ki == qi)
        def _():
            tile(True)
            o_ref[...] = acc_s[...] / l_s[...]
            lse_ref[...] = m_s[...] + jnp.log(l_s[...])

    qs, ks, cqs, cks = _fox_specs(tq, Dh)
    return _fox_call(
        body, _fox_tiles(S // tq, True), Hh, [qs, ks, ks, cqs, cks], [qs, cqs],
        [jax.ShapeDtypeStruct((Hh, S, Dh), F32), jax.ShapeDtypeStruct((Hh, S, 1), F32)],
        [pltpu.VMEM((tq, 1), F32), pltpu.VMEM((tq, 1), F32), pltpu.VMEM((tq, Dh), F32)],
        "fox_fwd", (q, k, v, c_col, c_row))


def fox_bwd(q, k, v, c_col, c_row, o, lse, do):
    Hh, S, Dh = q.shape
    tq = tk = min(FOX_TILE, S)
    nk = S // tk

    def body(qt_ref, kt_ref, q_ref, k_ref, v_ref, cq_ref, ck_ref, o_ref, lse_ref, do_ref,
             dq_ref, dr_ref, dk_ref, dv_ref, dc_ref, acc_s, row_s):
        step = pl.program_id(1)
        qi, ki = qt_ref[step], kt_ref[step]

        @pl.when(step == 0)
        def _():
            dk_ref[...] = jnp.zeros_like(dk_ref)
            dv_ref[...] = jnp.zeros_like(dv_ref)
            dc_ref[...] = jnp.zeros_like(dc_ref)

        @pl.when(ki == 0)
        def _():
            acc_s[...] = jnp.zeros_like(acc_s)
            row_s[...] = jnp.zeros_like(row_s)

        def tile(diagonal):
            q_t, kb, vb, do_t = q_ref[...], k_ref[...], v_ref[...], do_ref[...]
            s = _fox_logits(q_t, kb, cq_ref[...], ck_ref[...], diagonal)
            p = jnp.exp(s - lse_ref[...])
            delta = jnp.sum(do_t * o_ref[...], axis=-1, keepdims=True)
            ds = p * (_dot(do_t, vb, _NT) - delta)
            acc_s[...] += _dot(ds, kb, _NN)
            row_s[...] += jnp.sum(ds, axis=-1, keepdims=True)
            dk_ref[ki] += _dot(ds, q_t, _TN) * (HEAD_DIM ** -0.5)
            dv_ref[ki] += _dot(p, do_t, _TN)
            dc_ref[ki] += jnp.sum(ds, axis=0, keepdims=True)

        @pl.when(ki != qi)
        def _():
            tile(False)

        @pl.when(ki == qi)
        def _():
            tile(True)
            dq_ref[...] = acc_s[...] * (HEAD_DIM ** -0.5)
            dr_ref[...] = row_s[...]

    qs, ks, cqs, cks = _fox_specs(tq, Dh)
    head = lambda *blk: pl.BlockSpec((None,) + blk, lambda h, s, qt, kt: (h,) + (0,) * len(blk))
    dq, dr, dk, dv, dc = _fox_call(
        body, _fox_tiles(S // tq, True), Hh, [qs, ks, ks, cqs, cks, qs, cqs, qs],
        [qs, cqs, head(nk, tk, Dh), head(nk, tk, Dh), head(nk, 1, tk)],
        [jax.ShapeDtypeStruct((Hh, S, Dh), F32), jax.ShapeDtypeStruct((Hh, S, 1), F32),
         jax.ShapeDtypeStruct((Hh, nk, tk, Dh), F32), jax.ShapeDtypeStruct((Hh, nk, tk, Dh), F32),
         jax.ShapeDtypeStruct((Hh, nk, 1, tk), F32)],
        [pltpu.VMEM((tq, Dh), F32), pltpu.VMEM((tq, 1), F32)],
        "fox_bwd", (q, k, v, c_col, c_row, o, lse, do))
    return dq, dr, dk.reshape(Hh, S, Dh), dv.reshape(Hh, S, Dh), dc.reshape(Hh, 1, S)


def _heads(a, nh):
    S = a.shape[0]
    return a.reshape(S, nh, HEAD_DIM).transpose(1, 0, 2)


def _unheads(a):
    nh, S, _ = a.shape
    return a.transpose(1, 0, 2).reshape(S, nh * HEAD_DIM)


def _shift_down(a):
    return jnp.pad(a[:-1], ((1, 0), (0, 0)))


def _shift_up(a):
    return jnp.pad(a[1:], ((0, 1), (0, 0)))


def _block_diag_ones():
    i = np.arange(RWKV_DIM) // HEAD_DIM
    return jnp.asarray((i[:, None] == i[None, :]).astype(np.float32))


FFN_ROWS = 1024
FFN_COLS = 256


def _ffn_specs(S, F, tm, fc):
    nf = F // fc
    row = pl.BlockSpec((tm, D_MODEL), lambda i, j: (i, 0))
    vec = pl.BlockSpec((1, D_MODEL), lambda i, j: (0, 0))
    wg = pl.BlockSpec((D_MODEL, fc), lambda i, j: (0, j))
    wu = pl.BlockSpec((D_MODEL, fc), lambda i, j: (0, nf + j))
    wd = pl.BlockSpec((fc, D_MODEL), lambda i, j: (j, 0))
    hid = pl.BlockSpec((tm, fc), lambda i, j: (i, j))
    return nf, row, vec, wg, wu, wd, hid


def ffn_fwd(x, g_norm, w_gu, w_down, tag):
    S, F = x.shape[0], w_down.shape[0]
    tm, fc = min(FFN_ROWS, S), FFN_COLS
    nf, row, vec, wg, wu, wd, _ = _ffn_specs(S, F, tm, fc)

    def body(x_ref, g_ref, wg_ref, wu_ref, wd_ref, o_ref, hn_ref, hn_s, acc):
        j = pl.program_id(1)

        @pl.when(j == 0)
        def _():
            hn_s[...] = _rms(x_ref[...], g_ref[...]).astype(BF16)
            hn_ref[...] = hn_s[...]
            acc[...] = jnp.zeros_like(acc)

        g = _dot(hn_s[...], wg_ref[...], _NN)
        u = _dot(hn_s[...], wu_ref[...], _NN)
        acc[...] += _dot(g * _sigmoid_tanh(g) * u, wd_ref[...], _NN)

        @pl.when(j == nf - 1)
        def _():
            o_ref[...] = x_ref[...] + 0.5 * acc[...]

    out, hn = pl.pallas_call(
        body,
        grid=(S // tm, nf),
        in_specs=[row, vec, wg, wu, wd],
        out_specs=[row, row],
        out_shape=[jax.ShapeDtypeStruct((S, D_MODEL), F32), jax.ShapeDtypeStruct((S, D_MODEL), BF16)],
        scratch_shapes=[pltpu.VMEM((tm, D_MODEL), BF16), pltpu.VMEM((tm, D_MODEL), F32)],
        compiler_params=_cparams(("parallel", "arbitrary")),
        name=tag + "_fwd",
    )(x, g_norm, w_gu, w_gu, w_down)
    return out, (x, hn)


def ffn_bwd(dy, saved, g_norm, w_gu, w_down, tag, scatter_srcs=()):
    x, hn = saved
    S, F = x.shape[0], w_down.shape[0]
    tm, fc = min(FFN_ROWS, S), FFN_COLS
    nf, row, vec, wg, wu, wd, hid = _ffn_specs(S, F, tm, fc)
    nx = len(scatter_srcs)

    def body(*refs):
        dy_ref, x_ref, hn_ref, g_ref, wg_ref, wu_ref, wd_ref = refs[:7]
        dx_ref, dgn_ref, a_ref, dg_ref, du_ref = refs[7 + nx:12 + nx]
        dyh_s, dhn = refs[12 + 2 * nx:14 + 2 * nx]
        i, j = pl.program_id(0), pl.program_id(1)
        if nx:
            _exchange_during(i * nf + j, (S // tm) * nf, False, refs[7:7 + nx], refs[12 + nx:12 + 2 * nx],
                             *refs[14 + 2 * nx:])

        @pl.when(j == 0)
        def _():
            dyh_s[...] = (0.5 * dy_ref[...]).astype(BF16)
            dhn[...] = jnp.zeros_like(dhn)

        hn_t = hn_ref[...]
        g = _dot(hn_t, wg_ref[...], _NN)
        u = _dot(hn_t, wu_ref[...], _NN)
        da = _dot(dyh_s[...], wd_ref[...], _NT)
        sig = _sigmoid_tanh(g)
        gs = g * sig
        a_ref[...] = (gs * u).astype(BF16)
        dg = ((da * u) * (sig + gs * (1.0 - sig))).astype(BF16)
        du = (da * gs).astype(BF16)
        dg_ref[...] = dg
        du_ref[...] = du
        dhn[...] += _dot(jnp.concatenate([dg, du], axis=1),
                         jnp.concatenate([wg_ref[...], wu_ref[...]], axis=1), _NT)

        @pl.when(j == nf - 1)
        def _():
            _, vjp_n = jax.vjp(_rms, x_ref[...], g_ref[...])
            dx, dgn = vjp_n(dhn[...])
            dx_ref[...] = dy_ref[...] + dx

            @pl.when(i == 0)
            def _():
                dgn_ref[...] = jnp.zeros_like(dgn_ref)

            dgn_ref[...] += dgn

    hshape = jax.ShapeDtypeStruct((S, F), BF16)
    res = pl.pallas_call(
        body,
        grid=(S // tm, nf),
        in_specs=[row, row, row, vec, wg, wu, wd] + [_ANY] * nx,
        out_specs=[row, vec, hid, hid, hid] + [_ANY] * nx,
        out_shape=[jax.ShapeDtypeStruct((S, D_MODEL), F32), jax.ShapeDtypeStruct((1, D_MODEL), F32),
                   hshape, hshape, hshape] + _exchange_out_shapes(False, scatter_srcs),
        scratch_shapes=[pltpu.VMEM((tm, D_MODEL), BF16), pltpu.VMEM((tm, D_MODEL), F32)]
        + (_exchange_sems(nx) if nx else []),
        compiler_params=_cparams(("arbitrary", "arbitrary")),
        name=tag + "_bwd",
    )(dy, x, hn, g_norm, w_gu, w_gu, w_down, *scatter_srcs)
    dx, dgn, act, dg, du = res[:5]
    d_wdown = matmul(act, dy, "tn", tag + "_dwd", out_dtype=BF16, scale=0.5, tm=1408)
    d_wgu = jnp.concatenate([matmul(hn, dg, "tn", tag + "_dwg", out_dtype=BF16),
                             matmul(hn, du, "tn", tag + "_dwu", out_dtype=BF16)], axis=1)
    return dx, dgn, d_wgu, d_wdown, list(res[5:])


def ple_fwd(x, p_i, g_norm, w_gate, w_proj, tag):
    hn, = rowwise(_f_rms, [x], [g_norm], [(D_MODEL, BF16)], tag + "_rms")
    z = matmul(hn, w_gate, "nn", tag + "_gate")
    pp = matmul(p_i, w_proj, "nn", tag + "_proj")
    out, = rowwise(_f_ple, [x, z, pp], [], [(D_MODEL, F32)], tag + "_mix")
    return out, (x, hn, z, pp)


def ple_bwd(dy, saved, p_i, g_norm, w_gate, tag):
    x, hn, z, pp = saved
    (dz, dpp), _ = rowwise_vjp(_f_ple, [x, z, pp], [], [dy], tag + "_dmix", need=[False, True, True],
                               row_dtype=BF16)
    d_wproj = matmul(p_i, dpp, "tn", tag + "_dwp", out_dtype=BF16)
    d_wgate = matmul(hn, dz, "tn", tag + "_dwg", out_dtype=BF16)
    dhn = matmul(dz, w_gate, "nt", tag + "_dhn")
    (dx,), (dgn,) = rowwise_vjp(_f_rms_res, [x], [g_norm], [dhn, dy], tag + "_drms")
    return dx, dgn, d_wgate, d_wproj


def _swa_consts(sinks):
    slopes = np.asarray([2.0 ** (-(i + 1)) for i in range(SWA_HEADS)], np.float32)
    slope_col = jnp.asarray(np.repeat(slopes, BLOCK).reshape(SWA_KV_HEADS, SWA_GROUP * BLOCK, 1))
    sink_col = jnp.repeat(sinks.reshape(SWA_HEADS), BLOCK).reshape(SWA_KV_HEADS, SWA_GROUP * BLOCK, 1)
    return sink_col, slope_col


def even_mix_fwd(x, W, gather_src, later_weights):
    S = x.shape[0]
    hn, = rowwise(_f_rms, [x], [W["mix_norm0"]], [(D_MODEL, BF16)], "emix_rms")
    proj = matmul(hn, W["even_w_in"], "nn", "emix_in")
    qa = _heads(proj[:, :SWA_Q], SWA_HEADS).reshape(SWA_KV_HEADS, SWA_GROUP, S, HEAD_DIM)
    ka = _heads(proj[:, SWA_Q:SWA_Q + SWA_KV], SWA_KV_HEADS)
    va = _heads(proj[:, SWA_Q + SWA_KV:SWA_COLS], SWA_KV_HEADS)
    sink_col, slope_col = _swa_consts(W["swa_sinks"])
    ya = swa_fwd(qa, ka, va, sink_col, slope_col)
    ya = _unheads(ya.reshape(SWA_HEADS, S, HEAD_DIM))
    hb = proj[:, SWA_COLS:]
    h, = rowwise(_f_mix, [hb, _shift_down(hb)], [W["rwkv_mu"]], [(hb.shape[1], F32)], "rwkv_shift")
    hr, hk, hv = h[:, :512], h[:, 512:1024], h[:, 1024:1536]
    hw, ha, hg = h[:, 1536:1600], h[:, 1600:1664], h[:, 1664:1792]
    bd = _block_diag_ones()
    pre_params = [W["rwkv_w0"], W["rwkv_w2"], W["rwkv_a0"], W["rwkv_a2"], W["rwkv_g2"], W["rwkv_k_k"],
                  W["rwkv_k_a"]]
    decay, k2, kk, b, g = rowwise(_f_rwkv_pre, [hk, hw, ha, hg], pre_params + [bd],
                                  [(RWKV_DIM, F32)] * 5, "rwkv_pre")
    vT = _to_colblocks(hv)
    y, ckpt, gathered = rwkv_scan_fwd(hr, decay, k2, kk, b, vT, gather_src)
    late = later_weights(gathered)
    post_params = [W["rwkv_ln_w"], W["rwkv_ln_b"], W["rwkv_r_k"]]
    yb, = rowwise(_f_rwkv_post, [y, hr, k2, hv, g], post_params + [bd], [(RWKV_DIM, F32)], "rwkv_post")
    cat = jnp.concatenate([ya, yb], axis=1).astype(BF16)
    out = matmul(cat, late["even_w_out"], "nn", "emix_out", res=x)
    saved = (x, hn, qa, ka, va, sink_col, slope_col, hb, hr, hk, hv, hw, ha, hg, decay, k2, kk, b, g, vT,
             ckpt, y, cat)
    return out, saved, late


def even_mix_bwd(dy, saved, W, scatter_src):
    (x, hn, qa, ka, va, sink_col, slope_col, hb, hr, hk, hv, hw, ha, hg, decay, k2, kk, b, g, vT, ckpt, y,
     cat) = saved
    S = x.shape[0]
    grads = {}
    dcat = matmul(dy, W["even_w_out"], "nt", "emix_dcat")
    grads["even_w_out"] = matmul(cat, dy, "tn", "emix_dwout", out_dtype=BF16)
    dya, dyb = dcat[:, :SWA_Q], dcat[:, SWA_Q:]
    dya_h = _heads(dya, SWA_HEADS).reshape(SWA_KV_HEADS, SWA_GROUP, S, HEAD_DIM)
    dqa, dkp, dkc, dvp, dvc, dsink = swa_bwd(qa, ka, va, sink_col, slope_col, dya_h)
    shift_blk = lambda a: jnp.pad(a[:, BLOCK:], ((0, 0), (0, BLOCK), (0, 0)))
    dka = dkc + shift_blk(dkp)
    dva = dvc + shift_blk(dvp)
    grads["swa_sinks"] = dsink.reshape(SWA_HEADS, BLOCK).sum(axis=1).reshape(1, SWA_HEADS)
    dqa = _unheads(dqa.reshape(SWA_HEADS, S, HEAD_DIM))
    dka, dva = _unheads(dka), _unheads(dva)
    bd = _block_diag_ones()
    post_params = [W["rwkv_ln_w"], W["rwkv_ln_b"], W["rwkv_r_k"]]
    (d_y, d_r1, d_k2a, d_v1, d_g), (d_lnw, d_lnb, d_rk) = rowwise_vjp(
        _f_rwkv_post, [y, hr, k2, hv, g], post_params, [dyb], "rwkv_dpost", consts=[bd], tm=256)
    grads["rwkv_ln_w"], grads["rwkv_ln_b"], grads["rwkv_r_k"] = d_lnw, d_lnb, d_rk
    d_r2, d_w, d_k2b, d_kk, d_b, d_v2, exchanged = rwkv_scan_bwd(hr, decay, k2, kk, b, vT, _to_colblocks(d_y), ckpt,
                                                                  scatter_src)
    pre_params = [W["rwkv_w0"], W["rwkv_w2"], W["rwkv_a0"], W["rwkv_a2"], W["rwkv_g2"], W["rwkv_k_k"],
                  W["rwkv_k_a"]]
    (d_hk, d_hw, d_ha, d_hg), dpre = rowwise_vjp(
        _f_rwkv_pre, [hk, hw, ha, hg], pre_params, [d_w, d_k2a + d_k2b, d_kk, d_b, d_g], "rwkv_dpre",
        consts=[bd], tm=256)
    for nm, gval in zip(["rwkv_w0", "rwkv_w2", "rwkv_a0", "rwkv_a2", "rwkv_g2", "rwkv_k_k", "rwkv_k_a"], dpre):
        grads[nm] = gval
    d_h = jnp.concatenate([d_r1 + d_r2, d_hk, d_v1 + d_v2, d_hw, d_ha, d_hg], axis=1)
    (d_hb, d_sh), (d_mu,) = rowwise_vjp(_f_mix, [hb, _shift_down(hb)], [W["rwkv_mu"]], [d_h], "rwkv_dshift",
                                        tm=256)
    grads["rwkv_mu"] = d_mu
    d_hb = d_hb + _shift_up(d_sh)
    dproj = jnp.concatenate([dqa, dka, dva, d_hb], axis=1).astype(BF16)
    grads["even_w_in"] = matmul(hn, dproj, "tn", "emix_dwin", out_dtype=BF16)
    dhn = matmul(dproj, W["even_w_in"], "nt", "emix_dhn")
    (dx,), (dgn,) = rowwise_vjp(_f_rms_res, [x], [W["mix_norm0"]], [dhn, dy], "emix_drms")
    grads["mix_norm0"] = dgn
    return dx, grads, exchanged


def odd_mix_fwd(x, W):
    S = x.shape[0]
    hn, = rowwise(_f_rms, [x], [W["mix_norm1"]], [(D_MODEL, BF16)], "omix_rms")
    qkv = matmul(hn, W["fox_w_in"][:, :3 * FOX_DIM], "nn", "omix_in", out_dtype=BF16)
    fz = matmul(hn, W["fox_w_in"][:, 3 * FOX_DIM:], "nn", "omix_gate")
    q = _heads(qkv[:, :FOX_DIM], FOX_HEADS)
    k = _heads(qkv[:, FOX_DIM:2 * FOX_DIM], FOX_HEADS)
    v = _heads(qkv[:, 2 * FOX_DIM:], FOX_HEADS)
    logf, = rowwise(_f_logf, [fz], [W["fox_b_f"]], [(128, F32)], "fox_logf")
    c = seq_cumsum(logf, False, "fox_cumsum")[:, :FOX_HEADS]
    c_col = c.T.reshape(FOX_HEADS, S, 1)
    c_row = c.T.reshape(FOX_HEADS, 1, S)
    o, lse = fox_fwd(q, k, v, c_col, c_row)
    yc = _unheads(o).astype(BF16)
    out = matmul(yc, W["fox_w_out"], "nn", "omix_out", res=x)
    return out, (x, hn, q, k, v, fz, c_col, c_row, o, lse, yc)


def odd_mix_bwd(dy, saved, W):
    x, hn, q, k, v, fz, c_col, c_row, o, lse, yc = saved
    S = x.shape[0]
    grads = {}
    dyc = matmul(dy, W["fox_w_out"], "nt", "omix_dyc")
    grads["fox_w_out"] = matmul(yc, dy, "tn", "omix_dwout", out_dtype=BF16)
    do = _heads(dyc, FOX_HEADS)
    dq, drow, dk, dv, dcol = fox_bwd(q, k, v, c_col, c_row, o, lse, do)
    dc = (drow.reshape(FOX_HEADS, S) - dcol.reshape(FOX_HEADS, S)).T
    dc = jnp.pad(dc, ((0, 0), (0, 128 - FOX_HEADS)))
    dlogf = seq_cumsum(dc, True, "fox_rcumsum")
    (dfz,), (dbf,) = rowwise_vjp(_f_logf, [fz], [W["fox_b_f"]], [dlogf], "fox_dlogf")
    grads["fox_b_f"] = dbf
    dproj = jnp.concatenate([_unheads(dq), _unheads(dk), _unheads(dv), dfz], axis=1).astype(BF16)
    grads["fox_w_in"] = matmul(hn, dproj, "tn", "omix_dwin", out_dtype=BF16)
    dhn = matmul(dproj, W["fox_w_in"], "nt", "omix_dhn")
    (dx,), (dgn,) = rowwise_vjp(_f_rms_res, [x], [W["mix_norm1"]], [dhn, dy], "omix_drms")
    grads["mix_norm1"] = dgn
    return dx, grads


def device_step(x, p, target, W, gather_src, layer1_weights, layer1_grads, mixer_grads):
    W = dict(W)
    saved = []
    h = x
    for i in range(2):
        h, s1 = ffn_fwd(h, W[f"ffn1_norm{i}"], W[f"ffn1_w_gu{i}"], W[f"ffn1_w_down{i}"], f"ffn1_{i}")
        if i == 0:
            h, s2, late = even_mix_fwd(h, W, gather_src, layer1_weights)
            W.update(late)
        else:
            h, s2 = odd_mix_fwd(h, W)
        h, s3 = ffn_fwd(h, W[f"ffn2_norm{i}"], W[f"ffn2_w_gu{i}"], W[f"ffn2_w_down{i}"], f"ffn2_{i}")
        h, s4 = ple_fwd(h, p[i], W[f"ple_norm{i}"], W[f"ple_w_gate{i}"], W[f"ple_w_proj{i}"], f"ple_{i}")
        saved.append((s1, s2, s3, s4))
    dh, d_final, loss = loss_head(h, target, W["final_norm"])
    G = {"final_norm": d_final}
    for i in (1, 0):
        s1, s2, s3, s4 = saved[i]
        dh, G[f"ple_norm{i}"], G[f"ple_w_gate{i}"], G[f"ple_w_proj{i}"] = ple_bwd(
            dh, s4, p[i], W[f"ple_norm{i}"], W[f"ple_w_gate{i}"], f"ple_{i}")
        dh, G[f"ffn2_norm{i}"], G[f"ffn2_w_gu{i}"], G[f"ffn2_w_down{i}"], _ = ffn_bwd(
            dh, s3, W[f"ffn2_norm{i}"], W[f"ffn2_w_gu{i}"], W[f"ffn2_w_down{i}"], f"ffn2_{i}")
        if i == 0:
            dh, gm, exchanged = even_mix_bwd(dh, s2, W, layer1_grads(G))
        else:
            dh, gm = odd_mix_bwd(dh, s2, W)
        G.update(gm)
        dh, G[f"ffn1_norm{i}"], G[f"ffn1_w_gu{i}"], G[f"ffn1_w_down{i}"], exchanged_mid = ffn_bwd(
            dh, s1, W[f"ffn1_norm{i}"], W[f"ffn1_w_gu{i}"], W[f"ffn1_w_down{i}"], f"ffn1_{i}",
            scatter_srcs=mixer_grads(G) if i == 0 else ())
    return loss, dh, G, exchanged, exchanged_mid


_MESH = pl.DeviceIdType.MESH
_ANY = pl.BlockSpec(memory_space=pl.ANY)


def _exchange_sems(n):
    return [pltpu.SemaphoreType.DMA((7 * n,)), pltpu.SemaphoreType.DMA((7 * n,)), pltpu.SemaphoreType.DMA((n,))]


def all_gather(xs, name):
    n = len(xs)

    def body(*refs):
        x_refs, out_refs = refs[:n], refs[n:2 * n]
        send_sems, recv_sems, local_sems = refs[2 * n:]
        x_, y_, c_ = lax.axis_index("x"), lax.axis_index("y"), lax.axis_index("c")
        me, sibling = (x_, y_, c_), (x_, y_, 1 - c_)
        chips = [(1 - x_, y_), (x_, 1 - y_), (1 - x_, 1 - y_)]

        def copy(b, k, block, to, from_input=False):
            slot = out_refs[b].at[4 * block[0] + 2 * block[1] + block[2]]
            return pltpu.make_async_remote_copy(
                src_ref=x_refs[b] if from_input else slot, dst_ref=slot,
                send_sem=send_sems.at[7 * b + k], recv_sem=recv_sems.at[7 * b + k], device_id=to,
                device_id_type=_MESH)

        bufs = range(n)
        mine = [pltpu.make_async_copy(x_refs[b], out_refs[b].at[4 * x_ + 2 * y_ + c_], local_sems.at[b]) for b in bufs]
        first = [copy(b, 0, me, sibling, True) for b in bufs]
        first += [copy(b, 1 + j, me, (*chip, c_), True) for j, chip in enumerate(chips) for b in bufs]
        for cp in mine + first:
            cp.start()
        passed = []
        for j, chip in enumerate(chips):
            for b in bufs:
                copy(b, 1 + j, (*chip, c_), me).wait_recv()
                passed.append(copy(b, 4 + j, (*chip, c_), sibling))
                passed[-1].start()
        for b in bufs:
            copy(b, 0, sibling, me).wait_recv()
            for j, chip in enumerate(chips):
                copy(b, 4 + j, (*chip, 1 - c_), me).wait_recv()
        for cp in first + passed:
            cp.wait_send()
        for cp in mine:
            cp.wait()

    return pl.pallas_call(
        body,
        out_shape=[jax.ShapeDtypeStruct((N_DEV,) + x.shape, x.dtype) for x in xs],
        in_specs=[_ANY] * n,
        out_specs=[_ANY] * n,
        scratch_shapes=_exchange_sems(n),
        name=name,
    )(*xs)


def _direct_exchange(gather, s_refs, r_refs, send_sems, recv_sems, local_sems):
    x_, y_, c_ = lax.axis_index("x"), lax.axis_index("y"), lax.axis_index("c")
    my = 4 * x_ + 2 * y_ + c_
    copies = []
    for b, (s_ref, r_ref) in enumerate(zip(s_refs, r_refs)):
        copies.append(pltpu.make_async_copy(s_ref if gather else s_ref.at[my], r_ref.at[my], local_sems.at[b]))
        for m in range(1, N_DEV):
            px = 1 - x_ if (m >> 2) & 1 else x_
            py = 1 - y_ if (m >> 1) & 1 else y_
            pc = 1 - c_ if m & 1 else c_
            copies.append(pltpu.make_async_remote_copy(
                src_ref=s_ref if gather else s_ref.at[4 * px + 2 * py + pc], dst_ref=r_ref.at[my],
                send_sem=send_sems.at[7 * b + m - 1], recv_sem=recv_sems.at[7 * b + m - 1],
                device_id=(px, py, pc), device_id_type=_MESH))
    return copies


def _exchange_during(step, n_steps, gather, s_refs, r_refs, send_sems, recv_sems, local_sems):
    copies = _direct_exchange(gather, s_refs, r_refs, send_sems, recv_sems, local_sems)

    @pl.when(step == 0)
    def _():
        for cp in copies:
            cp.start()

    @pl.when(step == n_steps - 1)
    def _():
        for cp in copies:
            cp.wait()


def _exchange_out_shapes(gather, srcs):
    return [jax.ShapeDtypeStruct(((N_DEV,) + s.shape) if gather else s.shape, s.dtype) for s in srcs]


def all_to_all(sends, name):
    n = len(sends)

    def body(*refs):
        copies = _direct_exchange(False, refs[:n], refs[n:2 * n], *refs[2 * n:])
        for cp in copies:
            cp.start()
        for cp in copies:
            cp.wait()

    return pl.pallas_call(
        body,
        out_shape=_exchange_out_shapes(False, sends),
        in_specs=[_ANY] * n,
        out_specs=[_ANY] * n,
        scratch_shapes=_exchange_sems(n),
        name=name,
    )(*sends)


def adamw(w, m, v, parts, name, tm=256):
    R, C = w.shape
    tm = _pick(R, tm, 8) if R >= 8 else R

    def body(w_ref, m_ref, v_ref, p_ref, g_ref, d_ref, nm_ref, nv_ref):
        g = p_ref[0].astype(F32)
        for s in range(1, N_DEV):
            g = g + p_ref[s].astype(F32)
        nm = ADAM_B1 * m_ref[...] + (1.0 - ADAM_B1) * g
        nv = ADAM_B2 * v_ref[...] + (1.0 - ADAM_B2) * (g * g)
        m_hat = nm / (1.0 - ADAM_B1 ** ADAM_STEP)
        v_hat = nv / (1.0 - ADAM_B2 ** ADAM_STEP)
        g_ref[...] = g
        d_ref[...] = -ADAM_LR * (m_hat / (jnp.sqrt(v_hat) + ADAM_EPS) + ADAM_WD * w_ref[...])
        nm_ref[...] = nm
        nv_ref[...] = nv

    row = pl.BlockSpec((tm, C), lambda i: (i, 0))
    out = jax.ShapeDtypeStruct((R, C), F32)
    return pl.pallas_call(
        body,
        grid=(R // tm,),
        in_specs=[row, row, row, pl.BlockSpec((N_DEV, tm, C), lambda i: (0, i, 0))],
        out_specs=[row] * 4,
        out_shape=[out] * 4,
        compiler_params=_cparams(("parallel",)),
        name=name,
    )(w, m, v, parts)


_WEIGHTS = ["ffn1_norm", "ffn1_w_gu", "ffn1_w_down", "mix_norm", "ffn2_norm", "ffn2_w_gu", "ffn2_w_down",
            "ple_norm", "ple_w_gate", "ple_w_proj", "even_w_in", "even_w_out", "swa_sinks", "rwkv_mu",
            "rwkv_w0", "rwkv_w2", "rwkv_a0", "rwkv_a2", "rwkv_g2", "rwkv_k_k", "rwkv_k_a", "rwkv_r_k",
            "rwkv_ln_w", "rwkv_ln_b", "fox_w_in", "fox_b_f", "fox_w_out", "final_norm"]
_SHARD_AXIS = {"ffn1_w_gu": 2, "ffn1_w_down": 1, "ffn2_w_gu": 2, "ffn2_w_down": 1, "ple_w_gate": 1,
               "ple_w_proj": 2, "even_w_in": 2, "even_w_out": 1, "rwkv_w2": 2, "rwkv_a2": 2, "rwkv_g2": 2,
               "fox_w_in": 2, "fox_w_out": 1}
_SHARDED = [n for n in _WEIGHTS if n in _SHARD_AXIS]
_REPLICATED = [n for n in _WEIGHTS if n not in _SHARD_AXIS]
_PER_LAYER = ("ffn1_w_gu", "ffn1_w_down", "ffn2_w_gu", "ffn2_w_down", "ple_w_gate", "ple_w_proj")
_ALL_PIECES = ([(n, 0) for n in _PER_LAYER] + [(n, 0) for n in ("even_w_in", "even_w_out", "rwkv_w2", "rwkv_a2", "rwkv_g2")]
               + [(n, 1) for n in _PER_LAYER] + [("fox_w_in", 0), ("fox_w_out", 0)])
_FIRST_WEIGHTS = [(n, 0) for n in ("ffn1_w_gu", "ffn1_w_down", "even_w_in", "rwkv_w2", "rwkv_a2", "rwkv_g2")]
_PIECES = [_FIRST_WEIGHTS, [pc for pc in _ALL_PIECES if pc not in _FIRST_WEIGHTS]]
_LATE_GRADS = _FIRST_WEIGHTS + [("even_w_out", 0)]
_LAST_GRADS = [("ffn1_w_gu", 0), ("ffn1_w_down", 0)]
_GRAD_PIECES = [_LAST_GRADS, [pc for pc in _LATE_GRADS if pc not in _LAST_GRADS],
                [pc for pc in _ALL_PIECES if pc not in _LATE_GRADS]]
_PACK_LANES = 1024
_PACK_ROW_TILE = 256


def _piece_key(piece):
    name, idx = piece
    return f"{name}{idx}" if name in _PER_LAYER else name


_KINDS = ("gu", "rows", "misc")


def _kind(piece):
    if piece[0] in ("ffn1_w_gu", "ffn2_w_gu"):
        return "gu"
    return "rows" if _SHARD_AXIS[piece[0]] == 1 else "misc"


def _of_kind(pieces, shapes, kind):
    return [(pc, shp) for pc, shp in zip(pieces, shapes) if _kind(pc) == kind]


def _pad_rows(flat, axis):
    pad = [(0, 0)] * flat.ndim
    pad[axis] = (0, -flat.shape[axis] % _PACK_ROW_TILE)
    return jnp.pad(flat, pad)


def _kinds_of(pieces):
    return [kind for kind in _KINDS if any(_kind(pc) == kind for pc in pieces)]


def _bundle(get, pieces, dtype):
    make = {"gu": jnp.stack,
            "rows": lambda ps: jnp.concatenate(ps, axis=0),
            "misc": lambda ps: _pad_rows(jnp.concatenate([a.reshape(-1, _PACK_LANES) for a in ps], axis=0), 0)}
    return [make[kind]([get(pc).astype(dtype) for pc in pieces if _kind(pc) == kind]) for kind in _kinds_of(pieces)]


def _unbundle(bufs, pieces, shapes):
    out = {}
    for buf, kind in zip(bufs, _kinds_of(pieces)):
        of_kind = _of_kind(pieces, shapes, kind)
        if kind == "gu":
            stacked = buf.reshape((len(of_kind),) + of_kind[0][1])
            for j, (pc, _) in enumerate(of_kind):
                out[pc] = stacked[j]
            continue
        r0 = 0
        for pc, shp in of_kind:
            n = math.prod(shp) // _PACK_LANES
            out[pc] = buf[r0:r0 + n].reshape(shp)
            r0 += n
    return out


def _unshard(gathered, pieces, shapes):
    full = {}
    for j, (pc, shp) in enumerate(_of_kind(pieces, shapes, "gu")):
        full[_piece_key(pc)] = jnp.moveaxis(gathered[0][:, j], 0, 1).reshape(shp[0], N_DEV * shp[1])
    r0 = 0
    for pc, shp in _of_kind(pieces, shapes, "rows"):
        full[_piece_key(pc)] = gathered[1][:, r0:r0 + shp[0]].reshape(N_DEV * shp[0], shp[1])
        r0 += shp[0]
    r0 = 0
    for pc, shp in _of_kind(pieces, shapes, "misc"):
        n = math.prod(shp) // _PACK_LANES
        seg = gathered[2][:, r0:r0 + n].reshape((N_DEV,) + shp)
        full[_piece_key(pc)] = jnp.moveaxis(seg, 0, 1).reshape(shp[0], N_DEV * shp[1])
        r0 += n
    return full


def _to_shards(full, pieces, shapes):
    get = lambda pc: full[_piece_key(pc)].astype(BF16)
    cols = lambda pc, shp: jnp.moveaxis(get(pc).reshape(shp[0], N_DEV, shp[1]), 1, 0)
    make = {"gu": lambda ps: jnp.stack([cols(pc, shp) for pc, shp in ps], axis=1),
            "rows": lambda ps: jnp.concatenate([get(pc).reshape((N_DEV,) + shp) for pc, shp in ps], axis=1),
            "misc": lambda ps: _pad_rows(jnp.concatenate([cols(pc, shp).reshape(N_DEV, -1, _PACK_LANES)
                                                          for pc, shp in ps], axis=1), 1)}
    return [make[kind](_of_kind(pieces, shapes, kind)) for kind in _kinds_of(pieces)]


def _layer_weights(full):
    W = dict(full)
    if "fox_w_in" in W:
        W["fox_w_in"] = jnp.pad(W["fox_w_in"], ((0, 0), (0, FOX_IN_PAD - W["fox_w_in"].shape[1])))
    for n in ("rwkv_w2", "rwkv_a2", "rwkv_g2"):
        if n in W:
            W[n] = W[n].astype(F32)
    return W


def _pack_small(vals):
    flat = jnp.concatenate([v.reshape(1, -1) for v in vals], axis=1)
    n = flat.shape[1]
    return jnp.pad(flat, ((0, 0), (0, -n % 128)))


def _unpack_small(flat, shapes):
    out, c0 = [], 0
    for shp in shapes:
        n = math.prod(shp)
        out.append(flat[0, c0:c0 + n].reshape(shp))
        c0 += n
    return out


def kernel(x, p, ffn1_norm, ffn1_w_gu, ffn1_w_down, mix_norm, ffn2_norm, ffn2_w_gu, ffn2_w_down, ple_norm, ple_w_gate, ple_w_proj, even_w_in, even_w_out, swa_sinks, rwkv_mu, rwkv_w0, rwkv_w2, rwkv_a0, rwkv_a2, rwkv_g2, rwkv_k_k, rwkv_k_a, rwkv_r_k, rwkv_ln_w, rwkv_ln_b, fox_w_in, fox_b_f, fox_w_out, final_norm, loss_target, m_ffn1_norm, m_ffn1_w_gu, m_ffn1_w_down, m_mix_norm, m_ffn2_norm, m_ffn2_w_gu, m_ffn2_w_down, m_ple_norm, m_ple_w_gate, m_ple_w_proj, m_even_w_in, m_even_w_out, m_swa_sinks, m_rwkv_mu, m_rwkv_w0, m_rwkv_w2, m_rwkv_a0, m_rwkv_a2, m_rwkv_g2, m_rwkv_k_k, m_rwkv_k_a, m_rwkv_r_k, m_rwkv_ln_w, m_rwkv_ln_b, m_fox_w_in, m_fox_b_f, m_fox_w_out, m_final_norm, v_ffn1_norm, v_ffn1_w_gu, v_ffn1_w_down, v_mix_norm, v_ffn2_norm, v_ffn2_w_gu, v_ffn2_w_down, v_ple_norm, v_ple_w_gate, v_ple_w_proj, v_even_w_in, v_even_w_out, v_swa_sinks, v_rwkv_mu, v_rwkv_w0, v_rwkv_w2, v_rwkv_a0, v_rwkv_a2, v_rwkv_g2, v_rwkv_k_k, v_rwkv_k_a, v_rwkv_r_k, v_rwkv_ln_w, v_rwkv_ln_b, v_fox_w_in, v_fox_b_f, v_fox_w_out, v_final_norm):
    given = dict(locals())
    w = {n: given[n] for n in _WEIGHTS}
    m = {n: given["m_" + n] for n in _WEIGHTS}
    v = {n: given["v_" + n] for n in _WEIGHTS}
    small_shapes = [w[n].shape for n in _REPLICATED]
    piece = lambda d, pc: d[pc[0]][pc[1]]
    shapes = [[piece(w, pc).shape for pc in pieces] for pieces in _PIECES]
    gshapes = [[piece(w, pc).shape for pc in pieces] for pieces in _GRAD_PIECES]
    w_send = [_bundle(lambda pc: piece(w, pc), pieces, BF16) for pieces in _PIECES]

    W = _layer_weights(_unshard(all_gather(w_send[0], "weights_all_gather"), _PIECES[0], shapes[0]))
    for i in range(2):
        for n in ("ffn1_norm", "mix_norm", "ffn2_norm", "ple_norm"):
            W[f"{n}{i}"] = w[n][i].reshape(1, -1)
    for n in ("swa_sinks", "rwkv_mu", "rwkv_w0", "rwkv_a0", "rwkv_k_k", "rwkv_k_a", "rwkv_r_k", "rwkv_ln_w",
              "rwkv_ln_b", "final_norm"):
        W[n] = w[n].reshape(1, -1)
    n_f = fox_b_f.shape[1]
    W["fox_b_f"] = jnp.pad(fox_b_f.reshape(1, n_f), ((0, 0), (0, 128 - n_f)))
    n_fox = fox_w_in.shape[2] * N_DEV

    def layer1_weights(gathered):
        return _layer_weights(_unshard(gathered, _PIECES[1], shapes[1]))

    def early_grads(G):
        G = dict(G, fox_w_in=G["fox_w_in"][:, :n_fox])
        return _to_shards(G, _GRAD_PIECES[2], gshapes[2])

    def mixer_grads(G):
        return _to_shards(G, _GRAD_PIECES[1], gshapes[1])

    loss_row, dx, G, parts_early, parts_mixer = device_step(x[0], p[:, 0], loss_target[0], W, w_send[1],
                                                            layer1_weights, early_grads, mixer_grads)

    parts = [all_to_all(_to_shards(G, _GRAD_PIECES[0], gshapes[0]), "grads_all_to_all"), parts_mixer, parts_early]
    out_g, out_d, out_m, out_v = {}, {}, {}, {}
    rows2d = lambda a, lead: a.reshape(a.shape[:lead] + (-1, a.shape[-1]))
    for li, pieces in enumerate(_GRAD_PIECES):
        wmv = [_bundle(lambda pc, d=d: piece(d, pc), pieces, F32) for d in (w, m, v)]
        res = [adamw(*[rows2d(b[ki], 0) for b in wmv], rows2d(parts[li][ki], 1), f"adamw_{kind}{li}")
               for ki, kind in enumerate(_kinds_of(pieces))]
        for oi, out in enumerate((out_g, out_d, out_m, out_v)):
            for pc, a in _unbundle([r[oi] for r in res], pieces, gshapes[li]).items():
                out.setdefault(pc[0], {})[pc[1]] = a
    for out in (out_g, out_d, out_m, out_v):
        for n in _SHARDED:
            out[n] = jnp.stack([out[n][i] for i in sorted(out[n])])

    gsmall = {}
    for n in ("ffn1_norm", "mix_norm", "ffn2_norm", "ple_norm"):
        gsmall[n] = jnp.concatenate([G[f"{n}0"], G[f"{n}1"]], axis=0)
    for n in ("swa_sinks", "rwkv_mu", "rwkv_w0", "rwkv_a0", "rwkv_k_k", "rwkv_k_a", "rwkv_r_k", "rwkv_ln_w",
              "rwkv_ln_b", "final_norm"):
        gsmall[n] = G[n]
    gsmall["fox_b_f"] = G["fox_b_f"][:, :n_f]
    small = _pack_small([gsmall[n] for n in _REPLICATED] + [loss_row[:, :1]])
    small_parts = all_gather([small], "small_all_gather")[0]
    pad1 = lambda vals: _pack_small(vals + [jnp.zeros((1, 1), F32)])
    gs, ds, nms, nvs = adamw(pad1([w[n] for n in _REPLICATED]), pad1([m[n] for n in _REPLICATED]),
                             pad1([v[n] for n in _REPLICATED]), small_parts, "adamw_replicated")
    out_g.update(zip(_REPLICATED, _unpack_small(gs, small_shapes)))
    out_d.update(zip(_REPLICATED, _unpack_small(ds, small_shapes)))
    out_m.update(zip(_REPLICATED, _unpack_small(nms, small_shapes)))
    out_v.update(zip(_REPLICATED, _unpack_small(nvs, small_shapes)))
    n_small = sum(math.prod(s) for s in small_shapes)
    loss = gs[0, n_small]

    return (loss, dx[None], *[out_g[n] for n in _WEIGHTS], *[out_d[n] for n in _WEIGHTS],
            *[out_m[n] for n in _WEIGHTS], *[out_v[n] for n in _WEIGHTS])
```

```python
import functools
import math

import numpy as np
import jax
import jax.numpy as jnp
from jax import lax
from jax.experimental import pallas as pl
from jax.experimental.pallas import tpu as pltpu

F32 = jnp.float32
BF16 = jnp.bfloat16

D_MODEL = 1024
HEAD_DIM = 64
BLOCK = 128
SWA_HEADS = 8
SWA_KV_HEADS = 2
SWA_GROUP = 4
RWKV_HEADS = 8
RWKV_DIM = 512
FOX_HEADS = 16
FOX_DIM = 1024
D_FF = 2816
NORM_EPS = 1e-6
GN_EPS = 64e-5
L2_EPS = 1e-12
SWA_Q = 512
SWA_KV = 128
SWA_COLS = 768
FOX_IN_PAD = 3200
N_DEV = 8
ADAM_LR = 0.001
ADAM_B1 = 0.9
ADAM_B2 = 0.999
ADAM_EPS = 1e-08
ADAM_WD = 0.01
ADAM_STEP = 10

V7X_VMEM_LIMIT = 56 * 1024 * 1024
FOX_TILE = 512
CUMSUM_BLOCK = 256
SCAN_GROUP = 8
SCAN_CHUNK = 32

_NN = (((1,), (0,)), ((), ()))
_NT = (((1,), (1,)), ((), ()))
_TN = (((0,), (0,)), ((), ()))
_DIMS = {"nn": _NN, "nt": _NT, "tn": _TN}


def _pick(n, target, mult=128):
    best = None
    for t in range(mult, min(n, target) + 1, mult):
        if n % t == 0:
            best = t
    return best or n


def _cparams(sem):
    return pltpu.CompilerParams(dimension_semantics=sem, vmem_limit_bytes=V7X_VMEM_LIMIT)


def _dot(a, b, dims):
    return lax.dot_general(a.astype(BF16), b.astype(BF16), dims, preferred_element_type=F32)


@jax.custom_vjp
def bdot(a, b):
    return _dot(a, b, _NN)


def _bdot_fwd(a, b):
    return _dot(a, b, _NN), (a, b)


def _bdot_bwd(res, g):
    a, b = res
    return _dot(g, b, _NT), _dot(a, g, _TN)


bdot.defvjp(_bdot_fwd, _bdot_bwd)


@jax.custom_vjp
def bdot_nt(a, b):
    return _dot(a, b, _NT)


def _bdot_nt_fwd(a, b):
    return _dot(a, b, _NT), (a, b)


def _bdot_nt_bwd(res, g):
    a, b = res
    return _dot(g, b, _NN), _dot(g, a, _TN)


bdot_nt.defvjp(_bdot_nt_fwd, _bdot_nt_bwd)


@jax.custom_vjp
def _segsum(x, bd):
    return _dot2(x, bd.astype(BF16))


def _segsum_fwd(x, bd):
    return _segsum(x, bd), bd


def _segsum_bwd(bd, g):
    return _dot2(g, bd.astype(BF16)), jnp.zeros_like(bd)


_segsum.defvjp(_segsum_fwd, _segsum_bwd)


def _sigmoid(x):
    return 1.0 / (1.0 + jnp.exp(-x))


def _sigmoid_tanh(x):
    return 0.5 * jnp.tanh(0.5 * x) + 0.5


def _softplus(x):
    return jnp.maximum(x, 0.0) + jnp.log(1.0 + jnp.exp(-jnp.abs(x)))


def matmul(a, b, mode, name, out_dtype=F32, scale=1.0, res=None, tm=1024, tn=1408, tk=1024):
    if mode == "nn":
        (M, K), (K2, N) = a.shape, b.shape
    elif mode == "nt":
        (M, K), (N, K2) = a.shape, b.shape
    else:
        (K, M), (K2, N) = a.shape, b.shape
    assert K == K2, (a.shape, b.shape, mode)
    tm, tn, tk = _pick(M, tm), _pick(N, tn), _pick(K, tk)
    nk = K // tk
    has_res = res is not None

    def body(*refs):
        if has_res:
            a_ref, b_ref, r_ref, o_ref, acc = refs
        else:
            a_ref, b_ref, o_ref, acc = refs
        kk = pl.program_id(2)

        @pl.when(kk == 0)
        def _():
            acc[...] = jnp.zeros_like(acc)

        acc[...] += _dot(a_ref[...], b_ref[...], _DIMS[mode])

        @pl.when(kk == nk - 1)
        def _():
            v = acc[...]
            if scale != 1.0:
                v = v * scale
            if has_res:
                v = v + r_ref[...].astype(F32)
            o_ref[...] = v.astype(out_dtype)

    if mode == "tn":
        a_spec = pl.BlockSpec((tk, tm), lambda i, j, k: (k, i))
    else:
        a_spec = pl.BlockSpec((tm, tk), lambda i, j, k: (i, k))
    if mode == "nt":
        b_spec = pl.BlockSpec((tn, tk), lambda i, j, k: (j, k))
    else:
        b_spec = pl.BlockSpec((tk, tn), lambda i, j, k: (k, j))
    o_spec = pl.BlockSpec((tm, tn), lambda i, j, k: (i, j))
    in_specs = [a_spec, b_spec] + ([o_spec] if has_res else [])
    args = (a, b) + ((res,) if has_res else ())
    return pl.pallas_call(
        body,
        grid=(M // tm, N // tn, nk),
        in_specs=in_specs,
        out_specs=o_spec,
        out_shape=jax.ShapeDtypeStruct((M, N), out_dtype),
        scratch_shapes=[pltpu.VMEM((tm, tn), F32)],
        compiler_params=_cparams(("parallel", "parallel", "arbitrary")),
        name=name,
    )(*args)


def _row_spec(r, tm):
    if isinstance(r, tuple):
        arr, width, blk = r
        return arr, pl.BlockSpec((tm, width), lambda i, blk=blk: (i, blk))
    return r, pl.BlockSpec((tm, r.shape[1]), lambda i: (i, 0))


def _whole_spec(p):
    return pl.BlockSpec(p.shape, lambda i: (0,) * p.ndim)


def rowwise(fn, rows, params, outs, name, tm=512):
    arrs, specs = zip(*[_row_spec(r, tm) for r in rows])
    S = arrs[0].shape[0]
    tm = min(tm, S)
    arrs, specs = zip(*[_row_spec(r, tm) for r in rows])
    n_in = len(rows) + len(params)

    def body(*refs):
        res = fn(*[r[...] for r in refs[:n_in]])
        for o_ref, v in zip(refs[n_in:], res):
            o_ref[...] = v.astype(o_ref.dtype)

    return pl.pallas_call(
        body,
        grid=(S // tm,),
        in_specs=list(specs) + [_whole_spec(p) for p in params],
        out_specs=[pl.BlockSpec((tm, c), lambda i: (i, 0)) for c, _ in outs],
        out_shape=[jax.ShapeDtypeStruct((S, c), dt) for c, dt in outs],
        compiler_params=_cparams(("parallel",)),
        name=name,
    )(*arrs, *params)


def rowwise_vjp(fn, rows, params, cots, name, need=None, row_dtype=F32, consts=(), tm=512):
    nr, npar, nc, nk = len(rows), len(params), len(cots), len(consts)
    need = [True] * nr if need is None else need
    arrs, _ = zip(*[_row_spec(r, tm) for r in rows])
    S = arrs[0].shape[0]
    tm = min(tm, S)
    arrs, specs = zip(*[_row_spec(r, tm) for r in rows])
    carrs, cspecs = zip(*[_row_spec(c, tm) for c in cots])
    widths = [s.block_shape[1] for s in specs]
    n_in = nr + npar + nk + nc

    def body(*refs):
        i = pl.program_id(0)
        xs = [r[...].astype(F32) for r in refs[:nr]]
        ps = [r[...] for r in refs[nr:nr + npar]]
        ks = [r[...] for r in refs[nr + npar:nr + npar + nk]]
        cs = [r[...].astype(F32) for r in refs[nr + npar + nk:n_in]]
        outs, vjp = jax.vjp(lambda *a: fn(*a, *ks), *xs, *ps)
        grads = vjp(tuple(cs))
        o = n_in
        for j in range(nr):
            if need[j]:
                refs[o][...] = grads[j].astype(refs[o].dtype)
                o += 1
        for j in range(npar):
            g_ref = refs[o + j]

            @pl.when(i == 0)
            def _(g_ref=g_ref):
                g_ref[...] = jnp.zeros_like(g_ref)

            g_ref[...] += grads[nr + j]

    out_specs = [pl.BlockSpec((tm, w), lambda i: (i, 0)) for w, nd in zip(widths, need) if nd]
    out_shape = [jax.ShapeDtypeStruct((S, w), row_dtype) for w, nd in zip(widths, need) if nd]
    out_specs += [_whole_spec(p) for p in params]
    out_shape += [jax.ShapeDtypeStruct(p.shape, F32) for p in params]
    res = pl.pallas_call(
        body,
        grid=(S // tm,),
        in_specs=list(specs) + [_whole_spec(p) for p in params] + [_whole_spec(k) for k in consts] + list(cspecs),
        out_specs=out_specs,
        out_shape=out_shape,
        compiler_params=_cparams(("arbitrary",)),
        name=name,
    )(*arrs, *params, *consts, *carrs)
    nrow = sum(need)
    return list(res[:nrow]), list(res[nrow:])


def _rms(x, g):
    return x * lax.rsqrt(jnp.mean(x * x, axis=-1, keepdims=True) + NORM_EPS) * g


def _f_rms(x, g):
    return (_rms(x, g),)


def _f_rms_res(x, g):
    return _rms(x, g), x


def _f_ple(x, z, pp):
    return (x + _sigmoid(z) * pp,)


def _f_mix(h, sh, mu):
    return (h + (sh - h) * mu,)


def _f_logf(fz, bf):
    return (-_softplus(-(fz + bf)),)


def _f_rwkv_pre(hk, hw, ha, hg, w0, w2, a0, a2, g2, k_k, k_a, bd):
    wlog = -_softplus(-(w0 + bdot(jnp.tanh(hw), w2))) - 0.5
    a = _sigmoid(a0 + bdot(ha, a2))
    g = bdot(_sigmoid(hg), g2)
    kk = hk * k_k
    kk = kk / jnp.maximum(jnp.sqrt(_segsum(kk * kk, bd)), L2_EPS)
    k2 = hk * (1.0 + (a - 1.0) * k_a)
    decay = jnp.exp(-jnp.exp(wlog))
    return decay, k2, kk, kk * a, g


def _f_rwkv_post(y, r, k2, v, g, ln_w, ln_b, r_k, bd):
    mean = _segsum(y, bd) * (1.0 / HEAD_DIM)
    d = y - mean
    var = _segsum(d * d, bd) * (1.0 / HEAD_DIM)
    yn = d * lax.rsqrt(var + GN_EPS) * ln_w + ln_b
    yn = yn + _segsum(r * k2 * r_k, bd) * v
    return (yn * g,)


def loss_head(x, target, gf, tm=256):
    S, D = x.shape
    tm = min(tm, S)

    def f(xt, g, tt):
        err = _rms(xt, g) - tt
        return 0.5 * jnp.sum(err * err) * (1.0 / D)

    def body(x_ref, t_ref, g_ref, dx_ref, dg_ref, l_ref):
        i = pl.program_id(0)
        val, (dx, dg) = jax.value_and_grad(f, argnums=(0, 1))(x_ref[...], g_ref[...], t_ref[...])

        @pl.when(i == 0)
        def _():
            dg_ref[...] = jnp.zeros_like(dg_ref)
            l_ref[...] = jnp.zeros_like(l_ref)

        dx_ref[...] = dx
        dg_ref[...] += dg
        l_ref[...] += jnp.full(l_ref.shape, val, F32)

    row = pl.BlockSpec((tm, D), lambda i: (i, 0))
    vec = pl.BlockSpec((1, D), lambda i: (0, 0))
    return pl.pallas_call(
        body,
        grid=(S // tm,),
        in_specs=[row, row, vec],
        out_specs=[row, vec, pl.BlockSpec((1, 128), lambda i: (0, 0))],
        out_shape=[jax.ShapeDtypeStruct((S, D), F32), jax.ShapeDtypeStruct((1, D), F32),
                   jax.ShapeDtypeStruct((1, 128), F32)],
        compiler_params=_cparams(("arbitrary",)),
        name="loss_head",
    )(x, target, gf)


def _swa_block(q, kp, kc, vp, vc, sink, slope, n):
    k = jnp.concatenate([kp, kc], axis=0)
    v = jnp.concatenate([vp, vc], axis=0)
    rows = q.shape[0]
    logits = bdot_nt(q, k) * (HEAD_DIM ** -0.5)
    qi = lax.broadcasted_iota(jnp.int32, (rows, 2 * BLOCK), 0) & (BLOCK - 1)
    ki = lax.broadcasted_iota(jnp.int32, (rows, 2 * BLOCK), 1)
    dist = qi + BLOCK - ki
    valid = (dist >= 0) & (dist < BLOCK) & ((n - 1) * BLOCK + ki >= 0)
    logits = logits - slope * dist.astype(F32)
    logits = jnp.where(valid, logits, -jnp.inf)
    m = jnp.maximum(jnp.max(logits, axis=-1, keepdims=True), sink)
    pr = jnp.exp(logits - m)
    denom = jnp.sum(pr, axis=-1, keepdims=True) + jnp.exp(sink - m)
    return bdot(pr / denom, v)


def _swa_specs(S):
    nb = S // BLOCK
    q_spec = pl.BlockSpec((None, SWA_GROUP, BLOCK, HEAD_DIM), lambda h, n: (h, 0, n, 0))
    kc_spec = pl.BlockSpec((None, BLOCK, HEAD_DIM), lambda h, n: (h, n, 0))
    kp_spec = pl.BlockSpec((None, BLOCK, HEAD_DIM), lambda h, n: (h, jnp.maximum(n - 1, 0), 0))
    col_spec = pl.BlockSpec((None, SWA_GROUP * BLOCK, 1), lambda h, n: (h, 0, 0))
    return nb, q_spec, kp_spec, kc_spec, col_spec


def swa_fwd(q, k, v, sink_col, slope_col):
    S = q.shape[2]
    nb, q_spec, kp_spec, kc_spec, col_spec = _swa_specs(S)

    def body(q_ref, kp_ref, kc_ref, vp_ref, vc_ref, s_ref, a_ref, o_ref):
        n = pl.program_id(1)
        qq = q_ref[...].reshape(SWA_GROUP * BLOCK, HEAD_DIM)
        out = _swa_block(qq, kp_ref[...], kc_ref[...], vp_ref[...], vc_ref[...], s_ref[...], a_ref[...], n)
        o_ref[...] = out.reshape(SWA_GROUP, BLOCK, HEAD_DIM)

    return pl.pallas_call(
        body,
        grid=(SWA_KV_HEADS, nb),
        in_specs=[q_spec, kp_spec, kc_spec, kp_spec, kc_spec, col_spec, col_spec],
        out_specs=q_spec,
        out_shape=jax.ShapeDtypeStruct(q.shape, F32),
        compiler_params=_cparams(("parallel", "parallel")),
        name="swa_fwd",
    )(q, k, k, v, v, sink_col, slope_col)


def swa_bwd(q, k, v, sink_col, slope_col, dout):
    S = q.shape[2]
    nb, q_spec, kp_spec, kc_spec, col_spec = _swa_specs(S)

    def body(q_ref, kp_ref, kc_ref, vp_ref, vc_ref, s_ref, a_ref, do_ref,
             dq_ref, dkp_ref, dkc_ref, dvp_ref, dvc_ref, ds_ref):
        n = pl.program_id(1)
        qq = q_ref[...].reshape(SWA_GROUP * BLOCK, HEAD_DIM)
        slope = a_ref[...]
        f = lambda a, b, c, d, e, s: _swa_block(a, b, c, d, e, s, slope, n)
        _, vjp = jax.vjp(f, qq, kp_ref[...], kc_ref[...], vp_ref[...], vc_ref[...], s_ref[...])
        dq, dkp, dkc, dvp, dvc, ds = vjp(do_ref[...].reshape(SWA_GROUP * BLOCK, HEAD_DIM))
        dq_ref[...] = dq.reshape(SWA_GROUP, BLOCK, HEAD_DIM)
        dkp_ref[...] = dkp
        dkc_ref[...] = dkc
        dvp_ref[...] = dvp
        dvc_ref[...] = dvc

        @pl.when(n == 0)
        def _():
            ds_ref[...] = jnp.zeros_like(ds_ref)

        ds_ref[...] += ds

    kv_shape = jax.ShapeDtypeStruct(k.shape, F32)
    return pl.pallas_call(
        body,
        grid=(SWA_KV_HEADS, nb),
        in_specs=[q_spec, kp_spec, kc_spec, kp_spec, kc_spec, col_spec, col_spec, q_spec],
        out_specs=[q_spec, kc_spec, kc_spec, kc_spec, kc_spec, col_spec],
        out_shape=[jax.ShapeDtypeStruct(q.shape, F32), kv_shape, kv_shape, kv_shape, kv_shape,
                   jax.ShapeDtypeStruct(sink_col.shape, F32)],
        compiler_params=_cparams(("parallel", "arbitrary")),
        name="swa_bwd",
    )(q, k, k, v, v, sink_col, slope_col, dout)


def _split2(x):
    hi = x.astype(BF16)
    return (x - hi.astype(F32)).astype(BF16), hi


def _dot2_many(xs, m, single=False):
    rows = xs[0].shape[0]
    if single:
        res = jnp.dot(jnp.concatenate([x.astype(BF16) for x in xs], axis=0), m, preferred_element_type=F32)
        return [res[i * rows:(i + 1) * rows] for i in range(len(xs))]
    res = jnp.dot(jnp.concatenate([p for x in xs for p in _split2(x)], axis=0), m, preferred_element_type=F32)
    return [res[(2 * i) * rows:(2 * i + 1) * rows] + res[(2 * i + 1) * rows:(2 * i + 2) * rows]
            for i in range(len(xs))]


def _dot2(x, m):
    return _dot2_many([x], m)[0]


def _seg_sums(xs, bd, single=False):
    w = bd.shape[0]
    halves = _dot2_many([x[:, i:i + w] for x in xs for i in range(0, x.shape[1], w)], bd, single)
    n = xs[0].shape[1] // w
    return [jnp.concatenate(halves[i * n:(i + 1) * n], axis=1) for i in range(len(xs))]


def _seg_sum(x, bd):
    return _seg_sums([x], bd)[0]


def _scan_consts():
    r = np.arange(256)
    bd = (r[:, None] // HEAD_DIM == r[None, :] // HEAD_DIM).astype(np.float32)
    c = np.arange(RWKV_DIM)
    e = (np.arange(HEAD_DIM)[:, None] // SCAN_GROUP == c[None, :] // HEAD_DIM).astype(np.float32)
    diag = (np.arange(HEAD_DIM)[:, None] == c[None, :] % HEAD_DIM).astype(np.float32)
    return jnp.asarray(bd, BF16), jnp.asarray(e, BF16), jnp.asarray(diag, F32)


def _to_colblocks(a):
    S = a.shape[0]
    a = a.reshape(S // SCAN_GROUP, SCAN_GROUP, RWKV_HEADS, HEAD_DIM)
    return a.transpose(0, 3, 2, 1).reshape(S // SCAN_GROUP, HEAD_DIM, RWKV_HEADS * SCAN_GROUP)


def _roll_up(rows):
    return pltpu.roll(rows, rows.shape[0] - 1, 0)


def _scan_pair_rows(aux, base, kk_ref, w_ref, b_ref, k_ref, bd):
    G = SCAN_GROUP
    kk_nx = _roll_up(kk_ref[pl.ds(base, G), :])
    aux[0] = w_ref[pl.ds(base, G), :] * kk_nx
    aux[1], aux[2] = _seg_sums([b_ref[pl.ds(base, G), :] * kk_nx, k_ref[pl.ds(base, G), :] * kk_nx], bd)


def _scan_pair(St, t0, base, col_g, lane_t, aux, kk_ref, w_ref, b_ref, k_ref, bd, e):
    t1 = t0 + 1
    row = lambda ref, t: ref[pl.ds(base + t, 1), :]
    arow = lambda i: aux[i, pl.ds(t0, 1), :]
    u0, m1 = _seg_sums([St * row(kk_ref, t0), St * arow(0)], bd)
    v0, v1 = _dot2_many([jnp.where(lane_t == t0, col_g, 0.0), jnp.where(lane_t == t1, col_g, 0.0)], e)
    u1 = m1 - u0 * arow(1) + v0 * arow(2)
    S0 = St * row(w_ref, t0) - u0 * row(b_ref, t0) + v0 * row(k_ref, t0)
    S1 = S0 * row(w_ref, t1) - u1 * row(b_ref, t1) + v1 * row(k_ref, t1)
    return (S0, S1), (u0, u1), (v0, v1)


def rwkv_scan_fwd(r, w, k, kk, b, vB, gather_srcs):
    S, C = r.shape
    N, G = HEAD_DIM, SCAN_GROUP
    chunk = min(SCAN_CHUNK, S)
    nchunk, ng = S // chunk, chunk // G
    bd, e, diag = _scan_consts()

    nx = len(gather_srcs)

    def body(*refs):
        r_ref, w_ref, k_ref, kk_ref, b_ref, vB_ref, bd_ref, e_ref, dg_ref = refs[:9]
        y_ref, ck_ref = refs[9 + nx:11 + nx]
        S_ref, aux, send_sems, recv_sems, local_sems = refs[11 + 2 * nx:]
        c = pl.program_id(0)
        _exchange_during(c, nchunk, True, refs[9:9 + nx], refs[11 + nx:11 + 2 * nx], send_sems, recv_sems, local_sems)

        @pl.when(c == 0)
        def _():
            S_ref[...] = jnp.zeros_like(S_ref)

        ck_ref[...] = S_ref[...]
        sub = lax.broadcasted_iota(jnp.int32, (G, C), 0)
        lane_t = lax.broadcasted_iota(jnp.int32, (N, N), 1) & (G - 1)

        def group(g, St):
            base = pl.multiple_of(g * G, G)
            vb = vB_ref[g]
            _scan_pair_rows(aux, base, kk_ref, w_ref, b_ref, k_ref, bd_ref[...])
            ys = jnp.zeros((G, C), F32)
            def emit(ys, states, t0):
                steps = (t0, t0 + 1)
                y_bs = _seg_sums([S_t * r_ref[pl.ds(base + tt, 1), :] for S_t, tt in zip(states, steps)], bd_ref[...],
                                 single=True)
                for y_b, tt in zip(y_bs, steps):
                    ys = jnp.where(sub == tt, jnp.sum(y_b * dg_ref[...], axis=0, keepdims=True), ys)
                return ys

            pending = None
            for t0 in range(0, G, 2):
                states, _, _ = _scan_pair(St, t0, base, vb, lane_t, aux, kk_ref, w_ref, b_ref, k_ref, bd_ref[...],
                                          e_ref[...])
                if pending is not None:
                    ys = emit(ys, *pending)
                pending = (states, t0)
                St = states[1]
            y_ref[pl.ds(base, G), :] = emit(ys, *pending)
            return St

        S_ref[...] = lax.fori_loop(0, ng, group, S_ref[...])

    row = pl.BlockSpec((chunk, C), lambda c: (c, 0))
    col = pl.BlockSpec((ng, N, N), lambda c: (c, 0, 0))
    res = pl.pallas_call(
        body,
        grid=(nchunk,),
        in_specs=[row] * 5 + [col, _whole_spec(bd), _whole_spec(e), _whole_spec(diag)] + [_ANY] * nx,
        out_specs=[row, pl.BlockSpec((None, N, C), lambda c: (c, 0, 0))] + [_ANY] * nx,
        out_shape=[jax.ShapeDtypeStruct((S, C), F32), jax.ShapeDtypeStruct((nchunk, N, C), F32)]
        + _exchange_out_shapes(True, gather_srcs),
        scratch_shapes=[pltpu.VMEM((N, C), F32), pltpu.VMEM((3, G, C), F32)] + _exchange_sems(nx),
        compiler_params=_cparams(("arbitrary",)),
        name="rwkv_scan_fwd",
    )(r, w, k, kk, b, vB, bd, e, diag, *gather_srcs)
    return res[0], res[1], list(res[2:])


def rwkv_scan_bwd(r, w, k, kk, b, vB, dyB, ckpt, scatter_srcs):
    S, C = r.shape
    N, G = HEAD_DIM, SCAN_GROUP
    chunk = min(SCAN_CHUNK, S)
    nchunk, ng = S // chunk, chunk // G
    nsteps = nchunk + 1
    bd, e, diag = _scan_consts()
    nx = len(scatter_srcs)

    def body(*refs):
        wf_ref, kf_ref, kkf_ref, bf_ref, vBf_ref, ck_ref = refs[:6]
        r_ref, w_ref, k_ref, kk_ref, b_ref, dyB_ref, bd_ref, e_ref, dg_ref = refs[6:15]
        dr_ref, dw_ref, dk_ref, dkk_ref, db_ref, dv_ref = refs[15 + nx:21 + nx]
        G_ref, sbuf, ubuf, vbuf, aux_f, aux_b, send_sems, recv_sems, local_sems = refs[21 + 2 * nx:]
        c = pl.program_id(0)
        _exchange_during(c, nsteps, False, refs[15:15 + nx], refs[21 + nx:21 + 2 * nx], send_sems, recv_sems,
                         local_sems)

        @pl.when(c == 0)
        def _():
            G_ref[...] = jnp.zeros_like(G_ref)
            sbuf[...] = jnp.zeros_like(sbuf)
            ubuf[...] = jnp.zeros_like(ubuf)
            vbuf[...] = jnp.zeros_like(vbuf)

        sf = c % 2
        sb = 1 - sf
        lane_t = lax.broadcasted_iota(jnp.int32, (N, N), 1) & (G - 1)
        sub = lax.broadcasted_iota(jnp.int32, (G, C), 0)
        colsum = lambda a: jnp.sum(a, axis=0, keepdims=True)

        def group(g, carry):
            St, Gt = carry
            base_f = pl.multiple_of(g * G, G)
            gb = ng - 1 - g
            base_b = pl.multiple_of(gb * G, G)
            vb, dyb = vBf_ref[g], dyB_ref[gb]
            row = lambda ref, t: ref[pl.ds(base_b + t, 1), :]
            _scan_pair_rows(aux_f, base_f, kkf_ref, wf_ref, bf_ref, kf_ref, bd_ref[...])
            b8 = b_ref[pl.ds(base_b, G), :]
            aux_b[0] = _roll_up(w_ref[pl.ds(base_b, G), :]) * b8
            aux_b[1], aux_b[2] = _seg_sums([_roll_up(kk_ref[pl.ds(base_b, G), :]) * b8,
                                            r_ref[pl.ds(base_b, G), :] * b8], bd_ref[...])
            rows = [jnp.zeros((G, C), F32) for _ in range(6)]

            def emit(rows, steps):
                d_vs = _seg_sums([Gt_ * row(k_ref, tt) for tt, Gt_, _, _ in steps], bd_ref[...], single=True)
                for (tt, Gt_, du_b, dy_b), d_vb in zip(steps, d_vs):
                    Sp, Sc = sbuf[sb, base_b + tt], sbuf[sb, base_b + tt + 1]
                    new = (colsum(Sc * dy_b), colsum(Gt_ * Sp), colsum(Gt_ * vbuf[sb, base_b + tt]),
                           colsum(Sp * du_b), -colsum(Gt_ * ubuf[sb, base_b + tt]), colsum(d_vb * dg_ref[...]))
                    rows = [jnp.where(sub == tt, n_, acc) for n_, acc in zip(new, rows)]
                return rows

            pending = None
            for i in range(G // 2):
                t0 = 2 * i
                states, us, vs = _scan_pair(St, t0, base_f, vb, lane_t, aux_f, kkf_ref, wf_ref, bf_ref, kf_ref,
                                            bd_ref[...], e_ref[...])
                for j, S_before in enumerate((St, states[0])):
                    sbuf[sf, base_f + t0 + j] = S_before
                    ubuf[sf, base_f + t0 + j] = us[j]
                    vbuf[sf, base_f + t0 + j] = vs[j]
                St = states[1]
                t0 = G - 2 - 2 * i
                t1 = t0 + 1
                arow = lambda j, t0=t0: aux_b[j, pl.ds(t0, 1), :]
                dy1, dy0 = _dot2_many([jnp.where(lane_t == t1, dyb, 0.0), jnp.where(lane_t == t0, dyb, 0.0)],
                                      e_ref[...])
                G1 = Gt + dy1 * row(r_ref, t1)
                m1, m2 = _seg_sums([G1 * row(b_ref, t1), G1 * arow(0)], bd_ref[...])
                du1 = -m1
                du0 = -(m2 + du1 * arow(1) + dy0 * arow(2))
                G0 = G1 * row(w_ref, t1) + du1 * row(kk_ref, t1) + dy0 * row(r_ref, t0)
                G_next = G0 * row(w_ref, t0) + du0 * row(kk_ref, t0)
                if pending is not None:
                    rows = emit(rows, pending)
                pending = ((t1, G1, du1, dy1), (t0, G0, du0, dy0))
                Gt = G_next
            rows = emit(rows, pending)
            for ref, val in zip((dr_ref, dw_ref, dk_ref, dkk_ref, db_ref, dv_ref), rows):
                ref[pl.ds(base_b, G), :] = val
            return St, Gt

        St, Gt = lax.fori_loop(0, ng, group, (ck_ref[...], G_ref[...]))
        sbuf[sf, chunk] = St
        G_ref[...] = jnp.where(c >= 1, Gt, G_ref[...])

    fwd_chunk = lambda c: jnp.maximum(nchunk - 1 - c, 0)
    bwd_chunk = lambda c: jnp.minimum(nchunk - c, nchunk - 1)
    row_f = pl.BlockSpec((chunk, C), lambda c: (fwd_chunk(c), 0))
    row_b = pl.BlockSpec((chunk, C), lambda c: (bwd_chunk(c), 0))
    col_f = pl.BlockSpec((ng, N, N), lambda c: (fwd_chunk(c), 0, 0))
    col_b = pl.BlockSpec((ng, N, N), lambda c: (bwd_chunk(c), 0, 0))
    rshape = jax.ShapeDtypeStruct((S, C), F32)
    res = pl.pallas_call(
        body,
        grid=(nsteps,),
        in_specs=[row_f] * 4 + [col_f, pl.BlockSpec((None, N, C), lambda c: (fwd_chunk(c), 0, 0))]
        + [row_b] * 5 + [col_b, _whole_spec(bd), _whole_spec(e), _whole_spec(diag)] + [_ANY] * nx,
        out_specs=[row_b] * 6 + [_ANY] * nx,
        out_shape=[rshape] * 6 + _exchange_out_shapes(False, scatter_srcs),
        scratch_shapes=[pltpu.VMEM((N, C), F32), pltpu.VMEM((2, chunk + 1, N, C), F32),
                        pltpu.VMEM((2, chunk, N, C), F32), pltpu.VMEM((2, chunk, N, C), F32),
                        pltpu.VMEM((3, G, C), F32), pltpu.VMEM((3, G, C), F32)] + _exchange_sems(nx),
        compiler_params=_cparams(("arbitrary",)),
        name="rwkv_scan_bwd",
    )(w, k, kk, b, vB, ckpt, r, w, k, kk, b, dyB, bd, e, diag, *scatter_srcs)
    return tuple(res[:6]) + (list(res[6:]),)


def seq_cumsum(x, reverse, name):
    S, C = x.shape
    tb = min(CUMSUM_BLOCK, S)
    nb = S // tb

    def body(x_ref, o_ref, carry):
        i = pl.program_id(0)

        @pl.when(i == 0)
        def _():
            carry[...] = jnp.zeros_like(carry)

        ri = lax.broadcasted_iota(jnp.int32, (tb, tb), 0)
        ci = lax.broadcasted_iota(jnp.int32, (tb, tb), 1)
        tri = jnp.where((ci >= ri) if reverse else (ci <= ri), 1.0, 0.0).astype(F32)
        xb = x_ref[...]
        out = jnp.dot(tri, xb, precision=lax.Precision.HIGHEST, preferred_element_type=F32) + carry[...]
        o_ref[...] = out
        carry[...] = carry[...] + jnp.sum(xb, axis=0, keepdims=True)

    idx = (lambda i: (nb - 1 - i, 0)) if reverse else (lambda i: (i, 0))
    return pl.pallas_call(
        body,
        grid=(nb,),
        in_specs=[pl.BlockSpec((tb, C), idx)],
        out_specs=pl.BlockSpec((tb, C), idx),
        out_shape=jax.ShapeDtypeStruct((S, C), F32),
        scratch_shapes=[pltpu.VMEM((1, C), F32)],
        compiler_params=_cparams(("arbitrary",)),
        name=name,
    )(x)


def _fox_logits(q, k, cq, ck, diagonal):
    s = _dot(q, k, _NT) * (HEAD_DIM ** -0.5) + cq - ck
    if diagonal is False:
        return s
    row = lax.broadcasted_iota(jnp.int32, s.shape, 0) + (0 if diagonal is True else diagonal)
    col = lax.broadcasted_iota(jnp.int32, s.shape, 1)
    return jnp.where(col <= row, s, -jnp.inf)


def _fox_tiles(n, by_query):
    pairs = [(i, j) for i in range(n) for j in range(i + 1)] if by_query else \
            [(i, j) for j in range(n) for i in range(j, n)]
    return (jnp.asarray(np.array([p[0] for p in pairs], np.int32)),
            jnp.asarray(np.array([p[1] for p in pairs], np.int32)))


def _fox_specs(t, Dh):
    qs = pl.BlockSpec((None, t, Dh), lambda h, s, qt, kt: (h, qt[s], 0))
    ks = pl.BlockSpec((None, t, Dh), lambda h, s, qt, kt: (h, kt[s], 0))
    cqs = pl.BlockSpec((None, t, 1), lambda h, s, qt, kt: (h, qt[s], 0))
    cks = pl.BlockSpec((None, 1, t), lambda h, s, qt, kt: (h, 0, kt[s]))
    return qs, ks, cqs, cks


def _fox_call(body, tiles, Hh, in_specs, out_specs, out_shape, scratch, name, args):
    spec = pltpu.PrefetchScalarGridSpec(num_scalar_prefetch=2, grid=(Hh, tiles[0].shape[0]), in_specs=in_specs,
                                        out_specs=out_specs, scratch_shapes=scratch)
    return pl.pallas_call(body, grid_spec=spec, out_shape=out_shape,
                          compiler_params=_cparams(("parallel", "arbitrary")), name=name)(*tiles, *args)


def fox_fwd(q, k, v, c_col, c_row):
    Hh, S, Dh = q.shape
    tk = min(FOX_TILE, S)
    tq = tk // 2

    def body(qt_ref, kt_ref, q_ref, k_ref, v_ref, cq_ref, ck_ref, o_ref, lse_ref, m_s, l_s, acc_s):
        qi, ki = qt_ref[pl.program_id(1)], kt_ref[pl.program_id(1)]
        last = qi // 2

        @pl.when(ki == 0)
        def _():
            m_s[...] = jnp.full_like(m_s, -jnp.inf)
            l_s[...] = jnp.zeros_like(l_s)
            acc_s[...] = jnp.zeros_like(acc_s)

        def tile(diagonal):
            s = _fox_logits(q_ref[...], k_ref[...], cq_ref[...], ck_ref[...], diagonal)
            m_old = m_s[...]
            m_new = jnp.maximum(m_old, jnp.max(s, axis=-1, keepdims=True))
            alpha = jnp.exp(m_old - m_new)
            p = jnp.exp(s - m_new)
            l_s[...] = alpha * l_s[...] + jnp.sum(p, axis=-1, keepdims=True)
            acc_s[...] = alpha * acc_s[...] + _dot(p, v_ref[...], _NN)
            m_s[...] = m_new

        @pl.when(ki != last)
        def _():
            tile(False)

        @pl.when(ki == last)
        def _():
            tile((qi % 2) * tq)
            o_ref[...] = acc_s[...] / l_s[...]
            lse_ref[...] = m_s[...] + jnp.log(l_s[...])

    pairs = [(i, j) for i in range(S // tq) for j in range(i // 2 + 1)]
    tiles = tuple(jnp.asarray(np.array([p[a] for p in pairs], np.int32)) for a in (0, 1))
    qs, _, cqs, _ = _fox_specs(tq, Dh)
    _, ks, _, cks = _fox_specs(tk, Dh)
    return _fox_call(
        body, tiles, Hh, [qs, ks, ks, cqs, cks], [qs, cqs],
        [jax.ShapeDtypeStruct((Hh, S, Dh), F32), jax.ShapeDtypeStruct((Hh, S, 1), F32)],
        [pltpu.VMEM((tq, 1), F32), pltpu.VMEM((tq, 1), F32), pltpu.VMEM((tq, Dh), F32)],
        "fox_fwd", (q, k, v, c_col, c_row))


def fox_bwd(q, k, v, c_col, c_row, o, lse, do):
    Hh, S, Dh = q.shape
    tq = tk = min(FOX_TILE, S)
    nk = S // tk

    def body(qt_ref, kt_ref, q_ref, k_ref, v_ref, cq_ref, ck_ref, o_ref, lse_ref, do_ref,
             dq_ref, dr_ref, dk_ref, dv_ref, dc_ref, acc_s, row_s):
        step = pl.program_id(1)
        qi, ki = qt_ref[step], kt_ref[step]

        @pl.when(step == 0)
        def _():
            dk_ref[...] = jnp.zeros_like(dk_ref)
            dv_ref[...] = jnp.zeros_like(dv_ref)
            dc_ref[...] = jnp.zeros_like(dc_ref)

        @pl.when(ki == 0)
        def _():
            acc_s[...] = jnp.zeros_like(acc_s)
            row_s[...] = jnp.zeros_like(row_s)

        def tile(diagonal):
            q_t, kb, vb, do_t = q_ref[...], k_ref[...], v_ref[...], do_ref[...]
            s = _fox_logits(q_t, kb, cq_ref[...], ck_ref[...], diagonal)
            p = jnp.exp(s - lse_ref[...])
            delta = jnp.sum(do_t * o_ref[...], axis=-1, keepdims=True)
            ds = p * (_dot(do_t, vb, _NT) - delta)
            acc_s[...] += _dot(ds, kb, _NN)
            row_s[...] += jnp.sum(ds, axis=-1, keepdims=True)
            dk_ref[ki] += _dot(ds, q_t, _TN) * (HEAD_DIM ** -0.5)
            dv_ref[ki] += _dot(p, do_t, _TN)
            dc_ref[ki] += jnp.sum(ds, axis=0, keepdims=True)

        @pl.when(ki != qi)
        def _():
            tile(False)

        @pl.when(ki == qi)
        def _():
            tile(True)
            dq_ref[...] = acc_s[...] * (HEAD_DIM ** -0.5)
            dr_ref[...] = row_s[...]

    qs, ks, cqs, cks = _fox_specs(tq, Dh)
    head = lambda *blk: pl.BlockSpec((None,) + blk, lambda h, s, qt, kt: (h,) + (0,) * len(blk))
    dq, dr, dk, dv, dc = _fox_call(
        body, _fox_tiles(S // tq, True), Hh, [qs, ks, ks, cqs, cks, qs, cqs, qs],
        [qs, cqs, head(nk, tk, Dh), head(nk, tk, Dh), head(nk, 1, tk)],
        [jax.ShapeDtypeStruct((Hh, S, Dh), F32), jax.ShapeDtypeStruct((Hh, S, 1), F32),
         jax.ShapeDtypeStruct((Hh, nk, tk, Dh), F32), jax.ShapeDtypeStruct((Hh, nk, tk, Dh), F32),
         jax.ShapeDtypeStruct((Hh, nk, 1, tk), F32)],
        [pltpu.VMEM((tq, Dh), F32), pltpu.VMEM((tq, 1), F32)],
        "fox_bwd", (q, k, v, c_col, c_row, o, lse, do))
    return dq, dr, dk.reshape(Hh, S, Dh), dv.reshape(Hh, S, Dh), dc.reshape(Hh, 1, S)


def _heads(a, nh):
    S = a.shape[0]
    return a.reshape(S, nh, HEAD_DIM).transpose(1, 0, 2)


def _unheads(a):
    nh, S, _ = a.shape
    return a.transpose(1, 0, 2).reshape(S, nh * HEAD_DIM)


def _shift_down(a):
    return jnp.pad(a[:-1], ((1, 0), (0, 0)))


def _shift_up(a):
    return jnp.pad(a[1:], ((0, 1), (0, 0)))


def _block_diag_ones():
    i = np.arange(RWKV_DIM) // HEAD_DIM
    return jnp.asarray((i[:, None] == i[None, :]).astype(np.float32))


FFN_ROWS = 1024
FFN_COLS = 256


def _ffn_specs(S, F, tm, fc):
    nf = F // fc
    row = pl.BlockSpec((tm, D_MODEL), lambda i, j: (i, 0))
    vec = pl.BlockSpec((1, D_MODEL), lambda i, j: (0, 0))
    wg = pl.BlockSpec((D_MODEL, fc), lambda i, j: (0, j))
    wu = pl.BlockSpec((D_MODEL, fc), lambda i, j: (0, nf + j))
    wd = pl.BlockSpec((fc, D_MODEL), lambda i, j: (j, 0))
    hid = pl.BlockSpec((tm, fc), lambda i, j: (i, j))
    return nf, row, vec, wg, wu, wd, hid


def ffn_fwd(x, g_norm, w_gu, w_down, tag):
    S, F = x.shape[0], w_down.shape[0]
    tm, fc = min(FFN_ROWS, S), FFN_COLS
    nf, row, vec, wg, wu, wd, _ = _ffn_specs(S, F, tm, fc)

    def body(x_ref, g_ref, wg_ref, wu_ref, wd_ref, o_ref, hn_ref, hn_s, acc):
        j = pl.program_id(1)

        @pl.when(j == 0)
        def _():
            hn_s[...] = _rms(x_ref[...], g_ref[...]).astype(BF16)
            hn_ref[...] = hn_s[...]
            acc[...] = jnp.zeros_like(acc)

        g = _dot(hn_s[...], wg_ref[...], _NN)
        u = _dot(hn_s[...], wu_ref[...], _NN)
        acc[...] += _dot(g * _sigmoid_tanh(g) * u, wd_ref[...], _NN)

        @pl.when(j == nf - 1)
        def _():
            o_ref[...] = x_ref[...] + 0.5 * acc[...]

    out, hn = pl.pallas_call(
        body,
        grid=(S // tm, nf),
        in_specs=[row, vec, wg, wu, wd],
        out_specs=[row, row],
        out_shape=[jax.ShapeDtypeStruct((S, D_MODEL), F32), jax.ShapeDtypeStruct((S, D_MODEL), BF16)],
        scratch_shapes=[pltpu.VMEM((tm, D_MODEL), BF16), pltpu.VMEM((tm, D_MODEL), F32)],
        compiler_params=_cparams(("parallel", "arbitrary")),
        name=tag + "_fwd",
    )(x, g_norm, w_gu, w_gu, w_down)
    return out, (x, hn)


def ffn_bwd(dy, saved, g_norm, w_gu, w_down, tag, scatter_srcs=()):
    x, hn = saved
    S, F = x.shape[0], w_down.shape[0]
    tm, fc = min(FFN_ROWS, S), FFN_COLS
    nf, row, vec, wg, wu, wd, hid = _ffn_specs(S, F, tm, fc)
    nx = len(scatter_srcs)

    def body(*refs):
        dy_ref, x_ref, hn_ref, g_ref, wg_ref, wu_ref, wd_ref = refs[:7]
        dx_ref, dgn_ref, a_ref, dg_ref, du_ref = refs[7 + nx:12 + nx]
        dyh_s, dhn = refs[12 + 2 * nx:14 + 2 * nx]
        i, j = pl.program_id(0), pl.program_id(1)
        if nx:
            _exchange_during(i * nf + j, (S // tm) * nf, False, refs[7:7 + nx], refs[12 + nx:12 + 2 * nx],
                             *refs[14 + 2 * nx:])

        @pl.when(j == 0)
        def _():
            dyh_s[...] = (0.5 * dy_ref[...]).astype(BF16)
            dhn[...] = jnp.zeros_like(dhn)

        hn_t = hn_ref[...]
        g = _dot(hn_t, wg_ref[...], _NN)
        u = _dot(hn_t, wu_ref[...], _NN)
        da = _dot(dyh_s[...], wd_ref[...], _NT)
        sig = _sigmoid_tanh(g)
        gs = g * sig
        a_ref[...] = (gs * u).astype(BF16)
        dg = ((da * u) * (sig + gs * (1.0 - sig))).astype(BF16)
        du = (da * gs).astype(BF16)
        dg_ref[...] = dg
        du_ref[...] = du
        dhn[...] += _dot(jnp.concatenate([dg, du], axis=1),
                         jnp.concatenate([wg_ref[...], wu_ref[...]], axis=1), _NT)

        @pl.when(j == nf - 1)
        def _():
            _, vjp_n = jax.vjp(_rms, x_ref[...], g_ref[...])
            dx, dgn = vjp_n(dhn[...])
            dx_ref[...] = dy_ref[...] + dx

            @pl.when(i == 0)
            def _():
                dgn_ref[...] = jnp.zeros_like(dgn_ref)

            dgn_ref[...] += dgn

    hshape = jax.ShapeDtypeStruct((S, F), BF16)
    res = pl.pallas_call(
        body,
        grid=(S // tm, nf),
        in_specs=[row, row, row, vec, wg, wu, wd] + [_ANY] * nx,
        out_specs=[row, vec, hid, hid, hid] + [_ANY] * nx,
        out_shape=[jax.ShapeDtypeStruct((S, D_MODEL), F32), jax.ShapeDtypeStruct((1, D_MODEL), F32),
                   hshape, hshape, hshape] + _exchange_out_shapes(False, scatter_srcs),
        scratch_shapes=[pltpu.VMEM((tm, D_MODEL), BF16), pltpu.VMEM((tm, D_MODEL), F32)]
        + (_exchange_sems(nx) if nx else []),
        compiler_params=_cparams(("arbitrary", "arbitrary")),
        name=tag + "_bwd",
    )(dy, x, hn, g_norm, w_gu, w_gu, w_down, *scatter_srcs)
    dx, dgn, act, dg, du = res[:5]
    d_wdown = matmul(act, dy, "tn", tag + "_dwd", out_dtype=BF16, scale=0.5, tm=1408)
    d_wgu = jnp.concatenate([matmul(hn, dg, "tn", tag + "_dwg", out_dtype=BF16),
                             matmul(hn, du, "tn", tag + "_dwu", out_dtype=BF16)], axis=1)
    return dx, dgn, d_wgu, d_wdown, list(res[5:])


def ple_fwd(x, p_i, g_norm, w_gate, w_proj, tag):
    hn, = rowwise(_f_rms, [x], [g_norm], [(D_MODEL, BF16)], tag + "_rms")
    z = matmul(hn, w_gate, "nn", tag + "_gate")
    pp = matmul(p_i, w_proj, "nn", tag + "_proj")
    out, = rowwise(_f_ple, [x, z, pp], [], [(D_MODEL, F32)], tag + "_mix")
    return out, (x, hn, z, pp)


def ple_bwd(dy, saved, p_i, g_norm, w_gate, tag):
    x, hn, z, pp = saved
    (dz, dpp), _ = rowwise_vjp(_f_ple, [x, z, pp], [], [dy], tag + "_dmix", need=[False, True, True],
                               row_dtype=BF16)
    d_wproj = matmul(p_i, dpp, "tn", tag + "_dwp", out_dtype=BF16)
    d_wgate = matmul(hn, dz, "tn", tag + "_dwg", out_dtype=BF16)
    dhn = matmul(dz, w_gate, "nt", tag + "_dhn")
    (dx,), (dgn,) = rowwise_vjp(_f_rms_res, [x], [g_norm], [dhn, dy], tag + "_drms")
    return dx, dgn, d_wgate, d_wproj


def _swa_consts(sinks):
    slopes = np.asarray([2.0 ** (-(i + 1)) for i in range(SWA_HEADS)], np.float32)
    slope_col = jnp.asarray(np.repeat(slopes, BLOCK).reshape(SWA_KV_HEADS, SWA_GROUP * BLOCK, 1))
    sink_col = jnp.repeat(sinks.reshape(SWA_HEADS), BLOCK).reshape(SWA_KV_HEADS, SWA_GROUP * BLOCK, 1)
    return sink_col, slope_col


def even_mix_fwd(x, W, gather_src, later_weights):
    S = x.shape[0]
    hn, = rowwise(_f_rms, [x], [W["mix_norm0"]], [(D_MODEL, BF16)], "emix_rms")
    proj = matmul(hn, W["even_w_in"], "nn", "emix_in")
    qa = _heads(proj[:, :SWA_Q], SWA_HEADS).reshape(SWA_KV_HEADS, SWA_GROUP, S, HEAD_DIM)
    ka = _heads(proj[:, SWA_Q:SWA_Q + SWA_KV], SWA_KV_HEADS)
    va = _heads(proj[:, SWA_Q + SWA_KV:SWA_COLS], SWA_KV_HEADS)
    sink_col, slope_col = _swa_consts(W["swa_sinks"])
    ya = swa_fwd(qa, ka, va, sink_col, slope_col)
    ya = _unheads(ya.reshape(SWA_HEADS, S, HEAD_DIM))
    hb = proj[:, SWA_COLS:]
    h, = rowwise(_f_mix, [hb, _shift_down(hb)], [W["rwkv_mu"]], [(hb.shape[1], F32)], "rwkv_shift")
    hr, hk, hv = h[:, :512], h[:, 512:1024], h[:, 1024:1536]
    hw, ha, hg = h[:, 1536:1600], h[:, 1600:1664], h[:, 1664:1792]
    bd = _block_diag_ones()
    pre_params = [W["rwkv_w0"], W["rwkv_w2"], W["rwkv_a0"], W["rwkv_a2"], W["rwkv_g2"], W["rwkv_k_k"],
                  W["rwkv_k_a"]]
    decay, k2, kk, b, g = rowwise(_f_rwkv_pre, [hk, hw, ha, hg], pre_params + [bd],
                                  [(RWKV_DIM, F32)] * 5, "rwkv_pre")
    vT = _to_colblocks(hv)
    y, ckpt, gathered = rwkv_scan_fwd(hr, decay, k2, kk, b, vT, gather_src)
    late = later_weights(gathered)
    post_params = [W["rwkv_ln_w"], W["rwkv_ln_b"], W["rwkv_r_k"]]
    yb, = rowwise(_f_rwkv_post, [y, hr, k2, hv, g], post_params + [bd], [(RWKV_DIM, F32)], "rwkv_post")
    cat = jnp.concatenate([ya, yb], axis=1).astype(BF16)
    out = matmul(cat, late["even_w_out"], "nn", "emix_out", res=x)
    saved = (x, hn, qa, ka, va, sink_col, slope_col, hb, hr, hk, hv, hw, ha, hg, decay, k2, kk, b, g, vT,
             ckpt, y, cat)
    return out, saved, late


def even_mix_bwd(dy, saved, W, scatter_src):
    (x, hn, qa, ka, va, sink_col, slope_col, hb, hr, hk, hv, hw, ha, hg, decay, k2, kk, b, g, vT, ckpt, y,
     cat) = saved
    S = x.shape[0]
    grads = {}
    dcat = matmul(dy, W["even_w_out"], "nt", "emix_dcat")
    grads["even_w_out"] = matmul(cat, dy, "tn", "emix_dwout", out_dtype=BF16)
    dya, dyb = dcat[:, :SWA_Q], dcat[:, SWA_Q:]
    dya_h = _heads(dya, SWA_HEADS).reshape(SWA_KV_HEADS, SWA_GROUP, S, HEAD_DIM)
    dqa, dkp, dkc, dvp, dvc, dsink = swa_bwd(qa, ka, va, sink_col, slope_col, dya_h)
    shift_blk = lambda a: jnp.pad(a[:, BLOCK:], ((0, 0), (0, BLOCK), (0, 0)))
    dka = dkc + shift_blk(dkp)
    dva = dvc + shift_blk(dvp)
    grads["swa_sinks"] = dsink.reshape(SWA_HEADS, BLOCK).sum(axis=1).reshape(1, SWA_HEADS)
    dqa = _unheads(dqa.reshape(SWA_HEADS, S, HEAD_DIM))
    dka, dva = _unheads(dka), _unheads(dva)
    bd = _block_diag_ones()
    post_params = [W["rwkv_ln_w"], W["rwkv_ln_b"], W["rwkv_r_k"]]
    (d_y, d_r1, d_k2a, d_v1, d_g), (d_lnw, d_lnb, d_rk) = rowwise_vjp(
        _f_rwkv_post, [y, hr, k2, hv, g], post_params, [dyb], "rwkv_dpost", consts=[bd], tm=128)
    grads["rwkv_ln_w"], grads["rwkv_ln_b"], grads["rwkv_r_k"] = d_lnw, d_lnb, d_rk
    d_r2, d_w, d_k2b, d_kk, d_b, d_v2, exchanged = rwkv_scan_bwd(hr, decay, k2, kk, b, vT, _to_colblocks(d_y), ckpt,
                                                                  scatter_src)
    pre_params = [W["rwkv_w0"], W["rwkv_w2"], W["rwkv_a0"], W["rwkv_a2"], W["rwkv_g2"], W["rwkv_k_k"],
                  W["rwkv_k_a"]]
    (d_hk, d_hw, d_ha, d_hg), dpre = rowwise_vjp(
        _f_rwkv_pre, [hk, hw, ha, hg], pre_params, [d_w, d_k2a + d_k2b, d_kk, d_b, d_g], "rwkv_dpre",
        consts=[bd], tm=128)
    for nm, gval in zip(["rwkv_w0", "rwkv_w2", "rwkv_a0", "rwkv_a2", "rwkv_g2", "rwkv_k_k", "rwkv_k_a"], dpre):
        grads[nm] = gval
    d_h = jnp.concatenate([d_r1 + d_r2, d_hk, d_v1 + d_v2, d_hw, d_ha, d_hg], axis=1)
    (d_hb, d_sh), (d_mu,) = rowwise_vjp(_f_mix, [hb, _shift_down(hb)], [W["rwkv_mu"]], [d_h], "rwkv_dshift",
                                        tm=256)
    grads["rwkv_mu"] = d_mu
    d_hb = d_hb + _shift_up(d_sh)
    dproj = jnp.concatenate([dqa, dka, dva, d_hb], axis=1).astype(BF16)
    grads["even_w_in"] = matmul(hn, dproj, "tn", "emix_dwin", out_dtype=BF16)
    dhn = matmul(dproj, W["even_w_in"], "nt", "emix_dhn")
    (dx,), (dgn,) = rowwise_vjp(_f_rms_res, [x], [W["mix_norm0"]], [dhn, dy], "emix_drms")
    grads["mix_norm0"] = dgn
    return dx, grads, exchanged


def odd_mix_fwd(x, W):
    S = x.shape[0]
    hn, = rowwise(_f_rms, [x], [W["mix_norm1"]], [(D_MODEL, BF16)], "omix_rms")
    qkv = matmul(hn, W["fox_w_in"][:, :3 * FOX_DIM], "nn", "omix_in", out_dtype=BF16)
    fz = matmul(hn, W["fox_w_in"][:, 3 * FOX_DIM:], "nn", "omix_gate")
    q = _heads(qkv[:, :FOX_DIM], FOX_HEADS)
    k = _heads(qkv[:, FOX_DIM:2 * FOX_DIM], FOX_HEADS)
    v = _heads(qkv[:, 2 * FOX_DIM:], FOX_HEADS)
    logf, = rowwise(_f_logf, [fz], [W["fox_b_f"]], [(128, F32)], "fox_logf")
    c = seq_cumsum(logf, False, "fox_cumsum")[:, :FOX_HEADS]
    c_col = c.T.reshape(FOX_HEADS, S, 1)
    c_row = c.T.reshape(FOX_HEADS, 1, S)
    o, lse = fox_fwd(q, k, v, c_col, c_row)
    yc = _unheads(o).astype(BF16)
    out = matmul(yc, W["fox_w_out"], "nn", "omix_out", res=x)
    return out, (x, hn, q, k, v, fz, c_col, c_row, o, lse, yc)


def odd_mix_bwd(dy, saved, W):
    x, hn, q, k, v, fz, c_col, c_row, o, lse, yc = saved
    S = x.shape[0]
    grads = {}
    dyc = matmul(dy, W["fox_w_out"], "nt", "omix_dyc")
    grads["fox_w_out"] = matmul(yc, dy, "tn", "omix_dwout", out_dtype=BF16)
    do = _heads(dyc, FOX_HEADS)
    dq, drow, dk, dv, dcol = fox_bwd(q, k, v, c_col, c_row, o, lse, do)
    dc = (drow.reshape(FOX_HEADS, S) - dcol.reshape(FOX_HEADS, S)).T
    dc = jnp.pad(dc, ((0, 0), (0, 128 - FOX_HEADS)))
    dlogf = seq_cumsum(dc, True, "fox_rcumsum")
    (dfz,), (dbf,) = rowwise_vjp(_f_logf, [fz], [W["fox_b_f"]], [dlogf], "fox_dlogf")
    grads["fox_b_f"] = dbf
    dproj = jnp.concatenate([_unheads(dq), _unheads(dk), _unheads(dv), dfz], axis=1).astype(BF16)
    grads["fox_w_in"] = matmul(hn, dproj, "tn", "omix_dwin", out_dtype=BF16)
    dhn = matmul(dproj, W["fox_w_in"], "nt", "omix_dhn")
    (dx,), (dgn,) = rowwise_vjp(_f_rms_res, [x], [W["mix_norm1"]], [dhn, dy], "omix_drms")
    grads["mix_norm1"] = dgn
    return dx, grads


def device_step(x, p, target, W, gather_src, layer1_weights, layer1_grads, mixer_grads):
    W = dict(W)
    saved = []
    h = x
    for i in range(2):
        h, s1 = ffn_fwd(h, W[f"ffn1_norm{i}"], W[f"ffn1_w_gu{i}"], W[f"ffn1_w_down{i}"], f"ffn1_{i}")
        if i == 0:
            h, s2, late = even_mix_fwd(h, W, gather_src, layer1_weights)
            W.update(late)
        else:
            h, s2 = odd_mix_fwd(h, W)
        h, s3 = ffn_fwd(h, W[f"ffn2_norm{i}"], W[f"ffn2_w_gu{i}"], W[f"ffn2_w_down{i}"], f"ffn2_{i}")
        h, s4 = ple_fwd(h, p[i], W[f"ple_norm{i}"], W[f"ple_w_gate{i}"], W[f"ple_w_proj{i}"], f"ple_{i}")
        saved.append((s1, s2, s3, s4))
    dh, d_final, loss = loss_head(h, target, W["final_norm"])
    G = {"final_norm": d_final}
    for i in (1, 0):
        s1, s2, s3, s4 = saved[i]
        dh, G[f"ple_norm{i}"], G[f"ple_w_gate{i}"], G[f"ple_w_proj{i}"] = ple_bwd(
            dh, s4, p[i], W[f"ple_norm{i}"], W[f"ple_w_gate{i}"], f"ple_{i}")
        dh, G[f"ffn2_norm{i}"], G[f"ffn2_w_gu{i}"], G[f"ffn2_w_down{i}"], _ = ffn_bwd(
            dh, s3, W[f"ffn2_norm{i}"], W[f"ffn2_w_gu{i}"], W[f"ffn2_w_down{i}"], f"ffn2_{i}")
        if i == 0:
            dh, gm, exchanged = even_mix_bwd(dh, s2, W, layer1_grads(G))
        else:
            dh, gm = odd_mix_bwd(dh, s2, W)
        G.update(gm)
        dh, G[f"ffn1_norm{i}"], G[f"ffn1_w_gu{i}"], G[f"ffn1_w_down{i}"], exchanged_mid = ffn_bwd(
            dh, s1, W[f"ffn1_norm{i}"], W[f"ffn1_w_gu{i}"], W[f"ffn1_w_down{i}"], f"ffn1_{i}",
            scatter_srcs=mixer_grads(G) if i == 0 else ())
    return loss, dh, G, exchanged, exchanged_mid


_MESH = pl.DeviceIdType.MESH
_ANY = pl.BlockSpec(memory_space=pl.ANY)


def _exchange_sems(n):
    return [pltpu.SemaphoreType.DMA((7 * n,)), pltpu.SemaphoreType.DMA((7 * n,)), pltpu.SemaphoreType.DMA((n,))]


def all_gather(xs, name):
    n = len(xs)

    def body(*refs):
        x_refs, out_refs = refs[:n], refs[n:2 * n]
        send_sems, recv_sems, local_sems = refs[2 * n:]
        x_, y_, c_ = lax.axis_index("x"), lax.axis_index("y"), lax.axis_index("c")
        me, sibling = (x_, y_, c_), (x_, y_, 1 - c_)
        chips = [(1 - x_, y_), (x_, 1 - y_), (1 - x_, 1 - y_)]

        def copy(b, k, block, to, from_input=False):
            slot = out_refs[b].at[4 * block[0] + 2 * block[1] + block[2]]
            return pltpu.make_async_remote_copy(
                src_ref=x_refs[b] if from_input else slot, dst_ref=slot,
                send_sem=send_sems.at[7 * b + k], recv_sem=recv_sems.at[7 * b + k], device_id=to,
                device_id_type=_MESH)

        bufs = range(n)
        mine = [pltpu.make_async_copy(x_refs[b], out_refs[b].at[4 * x_ + 2 * y_ + c_], local_sems.at[b]) for b in bufs]
        first = [copy(b, 0, me, sibling, True) for b in bufs]
        first += [copy(b, 1 + j, me, (*chip, c_), True) for j, chip in enumerate(chips) for b in bufs]
        for cp in mine + first:
            cp.start()
        passed = []
        for j, chip in enumerate(chips):
            for b in bufs:
                copy(b, 1 + j, (*chip, c_), me).wait_recv()
                passed.append(copy(b, 4 + j, (*chip, c_), sibling))
                passed[-1].start()
        for b in bufs:
            copy(b, 0, sibling, me).wait_recv()
            for j, chip in enumerate(chips):
                copy(b, 4 + j, (*chip, 1 - c_), me).wait_recv()
        for cp in first + passed:
            cp.wait_send()
        for cp in mine:
            cp.wait()

    return pl.pallas_call(
        body,
        out_shape=[jax.ShapeDtypeStruct((N_DEV,) + x.shape, x.dtype) for x in xs],
        in_specs=[_ANY] * n,
        out_specs=[_ANY] * n,
        scratch_shapes=_exchange_sems(n),
        name=name,
    )(*xs)


def _direct_exchange(gather, s_refs, r_refs, send_sems, recv_sems, local_sems):
    x_, y_, c_ = lax.axis_index("x"), lax.axis_index("y"), lax.axis_index("c")
    my = 4 * x_ + 2 * y_ + c_
    copies = []
    for b, (s_ref, r_ref) in enumerate(zip(s_refs, r_refs)):
        copies.append(pltpu.make_async_copy(s_ref if gather else s_ref.at[my], r_ref.at[my], local_sems.at[b]))
        for m in range(1, N_DEV):
            px = 1 - x_ if (m >> 2) & 1 else x_
            py = 1 - y_ if (m >> 1) & 1 else y_
            pc = 1 - c_ if m & 1 else c_
            copies.append(pltpu.make_async_remote_copy(
                src_ref=s_ref if gather else s_ref.at[4 * px + 2 * py + pc], dst_ref=r_ref.at[my],
                send_sem=send_sems.at[7 * b + m - 1], recv_sem=recv_sems.at[7 * b + m - 1],
                device_id=(px, py, pc), device_id_type=_MESH))
    return copies


def _exchange_during(step, n_steps, gather, s_refs, r_refs, send_sems, recv_sems, local_sems):
    copies = _direct_exchange(gather, s_refs, r_refs, send_sems, recv_sems, local_sems)

    @pl.when(step == 0)
    def _():
        for cp in copies:
            cp.start()

    @pl.when(step == n_steps - 1)
    def _():
        for cp in copies:
            cp.wait()


def _exchange_out_shapes(gather, srcs):
    return [jax.ShapeDtypeStruct(((N_DEV,) + s.shape) if gather else s.shape, s.dtype) for s in srcs]


def all_to_all(sends, name):
    n = len(sends)

    def body(*refs):
        copies = _direct_exchange(False, refs[:n], refs[n:2 * n], *refs[2 * n:])
        for cp in copies:
            cp.start()
        for cp in copies:
            cp.wait()

    return pl.pallas_call(
        body,
        out_shape=_exchange_out_shapes(False, sends),
        in_specs=[_ANY] * n,
        out_specs=[_ANY] * n,
        scratch_shapes=_exchange_sems(n),
        name=name,
    )(*sends)


def adamw(w, m, v, parts, name, tm=256):
    R, C = w.shape
    tm = _pick(R, tm, 8) if R >= 8 else R

    def body(w_ref, m_ref, v_ref, p_ref, g_ref, d_ref, nm_ref, nv_ref):
        g = p_ref[0].astype(F32)
        for s in range(1, N_DEV):
            g = g + p_ref[s].astype(F32)
        nm = ADAM_B1 * m_ref[...] + (1.0 - ADAM_B1) * g
        nv = ADAM_B2 * v_ref[...] + (1.0 - ADAM_B2) * (g * g)
        m_hat = nm / (1.0 - ADAM_B1 ** ADAM_STEP)
        v_hat = nv / (1.0 - ADAM_B2 ** ADAM_STEP)
        g_ref[...] = g
        d_ref[...] = -ADAM_LR * (m_hat / (jnp.sqrt(v_hat) + ADAM_EPS) + ADAM_WD * w_ref[...])
        nm_ref[...] = nm
        nv_ref[...] = nv

    row = pl.BlockSpec((tm, C), lambda i: (i, 0))
    out = jax.ShapeDtypeStruct((R, C), F32)
    return pl.pallas_call(
        body,
        grid=(R // tm,),
        in_specs=[row, row, row, pl.BlockSpec((N_DEV, tm, C), lambda i: (0, i, 0))],
        out_specs=[row] * 4,
        out_shape=[out] * 4,
        compiler_params=_cparams(("parallel",)),
        name=name,
    )(w, m, v, parts)


_WEIGHTS = ["ffn1_norm", "ffn1_w_gu", "ffn1_w_down", "mix_norm", "ffn2_norm", "ffn2_w_gu", "ffn2_w_down",
            "ple_norm", "ple_w_gate", "ple_w_proj", "even_w_in", "even_w_out", "swa_sinks", "rwkv_mu",
            "rwkv_w0", "rwkv_w2", "rwkv_a0", "rwkv_a2", "rwkv_g2", "rwkv_k_k", "rwkv_k_a", "rwkv_r_k",
            "rwkv_ln_w", "rwkv_ln_b", "fox_w_in", "fox_b_f", "fox_w_out", "final_norm"]
_SHARD_AXIS = {"ffn1_w_gu": 2, "ffn1_w_down": 1, "ffn2_w_gu": 2, "ffn2_w_down": 1, "ple_w_gate": 1,
               "ple_w_proj": 2, "even_w_in": 2, "even_w_out": 1, "rwkv_w2": 2, "rwkv_a2": 2, "rwkv_g2": 2,
               "fox_w_in": 2, "fox_w_out": 1}
_SHARDED = [n for n in _WEIGHTS if n in _SHARD_AXIS]
_REPLICATED = [n for n in _WEIGHTS if n not in _SHARD_AXIS]
_PER_LAYER = ("ffn1_w_gu", "ffn1_w_down", "ffn2_w_gu", "ffn2_w_down", "ple_w_gate", "ple_w_proj")
_ALL_PIECES = ([(n, 0) for n in _PER_LAYER] + [(n, 0) for n in ("even_w_in", "even_w_out", "rwkv_w2", "rwkv_a2", "rwkv_g2")]
               + [(n, 1) for n in _PER_LAYER] + [("fox_w_in", 0), ("fox_w_out", 0)])
_FIRST_WEIGHTS = [(n, 0) for n in ("ffn1_w_gu", "ffn1_w_down", "even_w_in", "rwkv_w2", "rwkv_a2", "rwkv_g2")]
_PIECES = [_FIRST_WEIGHTS, [pc for pc in _ALL_PIECES if pc not in _FIRST_WEIGHTS]]
_LATE_GRADS = _FIRST_WEIGHTS + [("even_w_out", 0)]
_LAST_GRADS = [("ffn1_w_gu", 0), ("ffn1_w_down", 0)]
_GRAD_PIECES = [_LAST_GRADS, [pc for pc in _LATE_GRADS if pc not in _LAST_GRADS],
                [pc for pc in _ALL_PIECES if pc not in _LATE_GRADS]]
_PACK_LANES = 1024
_PACK_ROW_TILE = 256


def _piece_key(piece):
    name, idx = piece
    return f"{name}{idx}" if name in _PER_LAYER else name


_KINDS = ("gu", "rows", "misc")


def _kind(piece):
    if piece[0] in ("ffn1_w_gu", "ffn2_w_gu"):
        return "gu"
    return "rows" if _SHARD_AXIS[piece[0]] == 1 else "misc"


def _of_kind(pieces, shapes, kind):
    return [(pc, shp) for pc, shp in zip(pieces, shapes) if _kind(pc) == kind]


def _pad_rows(flat, axis):
    pad = [(0, 0)] * flat.ndim
    pad[axis] = (0, -flat.shape[axis] % _PACK_ROW_TILE)
    return jnp.pad(flat, pad)


def _kinds_of(pieces):
    return [kind for kind in _KINDS if any(_kind(pc) == kind for pc in pieces)]


def _bundle(get, pieces, dtype):
    make = {"gu": jnp.stack,
            "rows": lambda ps: jnp.concatenate(ps, axis=0),
            "misc": lambda ps: _pad_rows(jnp.concatenate([a.reshape(-1, _PACK_LANES) for a in ps], axis=0), 0)}
    return [make[kind]([get(pc).astype(dtype) for pc in pieces if _kind(pc) == kind]) for kind in _kinds_of(pieces)]


def _unbundle(bufs, pieces, shapes):
    out = {}
    for buf, kind in zip(bufs, _kinds_of(pieces)):
        of_kind = _of_kind(pieces, shapes, kind)
        if kind == "gu":
            stacked = buf.reshape((len(of_kind),) + of_kind[0][1])
            for j, (pc, _) in enumerate(of_kind):
                out[pc] = stacked[j]
            continue
        r0 = 0
        for pc, shp in of_kind:
            n = math.prod(shp) // _PACK_LANES
            out[pc] = buf[r0:r0 + n].reshape(shp)
            r0 += n
    return out


def _unshard(gathered, pieces, shapes):
    full = {}
    for j, (pc, shp) in enumerate(_of_kind(pieces, shapes, "gu")):
        full[_piece_key(pc)] = jnp.moveaxis(gathered[0][:, j], 0, 1).reshape(shp[0], N_DEV * shp[1])
    r0 = 0
    for pc, shp in _of_kind(pieces, shapes, "rows"):
        full[_piece_key(pc)] = gathered[1][:, r0:r0 + shp[0]].reshape(N_DEV * shp[0], shp[1])
        r0 += shp[0]
    r0 = 0
    for pc, shp in _of_kind(pieces, shapes, "misc"):
        n = math.prod(shp) // _PACK_LANES
        seg = gathered[2][:, r0:r0 + n].reshape((N_DEV,) + shp)
        full[_piece_key(pc)] = jnp.moveaxis(seg, 0, 1).reshape(shp[0], N_DEV * shp[1])
        r0 += n
    return full


def _to_shards(full, pieces, shapes):
    get = lambda pc: full[_piece_key(pc)].astype(BF16)
    cols = lambda pc, shp: jnp.moveaxis(get(pc).reshape(shp[0], N_DEV, shp[1]), 1, 0)
    make = {"gu": lambda ps: jnp.stack([cols(pc, shp) for pc, shp in ps], axis=1),
            "rows": lambda ps: jnp.concatenate([get(pc).reshape((N_DEV,) + shp) for pc, shp in ps], axis=1),
            "misc": lambda ps: _pad_rows(jnp.concatenate([cols(pc, shp).reshape(N_DEV, -1, _PACK_LANES)
                                                          for pc, shp in ps], axis=1), 1)}
    return [make[kind](_of_kind(pieces, shapes, kind)) for kind in _kinds_of(pieces)]


def _layer_weights(full):
    W = dict(full)
    if "fox_w_in" in W:
        W["fox_w_in"] = jnp.pad(W["fox_w_in"], ((0, 0), (0, FOX_IN_PAD - W["fox_w_in"].shape[1])))
    for n in ("rwkv_w2", "rwkv_a2", "rwkv_g2"):
        if n in W:
            W[n] = W[n].astype(F32)
    return W


def _pack_small(vals):
    flat = jnp.concatenate([v.reshape(1, -1) for v in vals], axis=1)
    n = flat.shape[1]
    return jnp.pad(flat, ((0, 0), (0, -n % 128)))


def _unpack_small(flat, shapes):
    out, c0 = [], 0
    for shp in shapes:
        n = math.prod(shp)
        out.append(flat[0, c0:c0 + n].reshape(shp))
        c0 += n
    return out


def kernel(x, p, ffn1_norm, ffn1_w_gu, ffn1_w_down, mix_norm, ffn2_norm, ffn2_w_gu, ffn2_w_down, ple_norm, ple_w_gate, ple_w_proj, even_w_in, even_w_out, swa_sinks, rwkv_mu, rwkv_w0, rwkv_w2, rwkv_a0, rwkv_a2, rwkv_g2, rwkv_k_k, rwkv_k_a, rwkv_r_k, rwkv_ln_w, rwkv_ln_b, fox_w_in, fox_b_f, fox_w_out, final_norm, loss_target, m_ffn1_norm, m_ffn1_w_gu, m_ffn1_w_down, m_mix_norm, m_ffn2_norm, m_ffn2_w_gu, m_ffn2_w_down, m_ple_norm, m_ple_w_gate, m_ple_w_proj, m_even_w_in, m_even_w_out, m_swa_sinks, m_rwkv_mu, m_rwkv_w0, m_rwkv_w2, m_rwkv_a0, m_rwkv_a2, m_rwkv_g2, m_rwkv_k_k, m_rwkv_k_a, m_rwkv_r_k, m_rwkv_ln_w, m_rwkv_ln_b, m_fox_w_in, m_fox_b_f, m_fox_w_out, m_final_norm, v_ffn1_norm, v_ffn1_w_gu, v_ffn1_w_down, v_mix_norm, v_ffn2_norm, v_ffn2_w_gu, v_ffn2_w_down, v_ple_norm, v_ple_w_gate, v_ple_w_proj, v_even_w_in, v_even_w_out, v_swa_sinks, v_rwkv_mu, v_rwkv_w0, v_rwkv_w2, v_rwkv_a0, v_rwkv_a2, v_rwkv_g2, v_rwkv_k_k, v_rwkv_k_a, v_rwkv_r_k, v_rwkv_ln_w, v_rwkv_ln_b, v_fox_w_in, v_fox_b_f, v_fox_w_out, v_final_norm):
    given = dict(locals())
    w = {n: given[n] for n in _WEIGHTS}
    m = {n: given["m_" + n] for n in _WEIGHTS}
    v = {n: given["v_" + n] for n in _WEIGHTS}
    small_shapes = [w[n].shape for n in _REPLICATED]
    piece = lambda d, pc: d[pc[0]][pc[1]]
    shapes = [[piece(w, pc).shape for pc in pieces] for pieces in _PIECES]
    gshapes = [[piece(w, pc).shape for pc in pieces] for pieces in _GRAD_PIECES]
    w_send = [_bundle(lambda pc: piece(w, pc), pieces, BF16) for pieces in _PIECES]

    W = _layer_weights(_unshard(all_gather(w_send[0], "weights_all_gather"), _PIECES[0], shapes[0]))
    for i in range(2):
        for n in ("ffn1_norm", "mix_norm", "ffn2_norm", "ple_norm"):
            W[f"{n}{i}"] = w[n][i].reshape(1, -1)
    for n in ("swa_sinks", "rwkv_mu", "rwkv_w0", "rwkv_a0", "rwkv_k_k", "rwkv_k_a", "rwkv_r_k", "rwkv_ln_w",
              "rwkv_ln_b", "final_norm"):
        W[n] = w[n].reshape(1, -1)
    n_f = fox_b_f.shape[1]
    W["fox_b_f"] = jnp.pad(fox_b_f.reshape(1, n_f), ((0, 0), (0, 128 - n_f)))
    n_fox = fox_w_in.shape[2] * N_DEV

    def layer1_weights(gathered):
        return _layer_weights(_unshard(gathered, _PIECES[1], shapes[1]))

    def early_grads(G):
        G = dict(G, fox_w_in=G["fox_w_in"][:, :n_fox])
        return _to_shards(G, _GRAD_PIECES[2], gshapes[2])

    def mixer_grads(G):
        return _to_shards(G, _GRAD_PIECES[1], gshapes[1])

    loss_row, dx, G, parts_early, parts_mixer = device_step(x[0], p[:, 0], loss_target[0], W, w_send[1],
                                                            layer1_weights, early_grads, mixer_grads)

    parts = [all_to_all(_to_shards(G, _GRAD_PIECES[0], gshapes[0]), "grads_all_to_all"), parts_mixer, parts_early]
    out_g, out_d, out_m, out_v = {}, {}, {}, {}
    rows2d = lambda a, lead: a.reshape(a.shape[:lead] + (-1, a.shape[-1]))
    for li, pieces in enumerate(_GRAD_PIECES):
        wmv = [_bundle(lambda pc, d=d: piece(d, pc), pieces, F32) for d in (w, m, v)]
        res = [adamw(*[rows2d(b[ki], 0) for b in wmv], rows2d(parts[li][ki], 1), f"adamw_{kind}{li}")
               for ki, kind in enumerate(_kinds_of(pieces))]
        for oi, out in enumerate((out_g, out_d, out_m, out_v)):
            for pc, a in _unbundle([r[oi] for r in res], pieces, gshapes[li]).items():
                out.setdefault(pc[0], {})[pc[1]] = a
    for out in (out_g, out_d, out_m, out_v):
        for n in _SHARDED:
            out[n] = jnp.stack([out[n][i] for i in sorted(out[n])])

    gsmall = {}
    for n in ("ffn1_norm", "mix_norm", "ffn2_norm", "ple_norm"):
        gsmall[n] = jnp.concatenate([G[f"{n}0"], G[f"{n}1"]], axis=0)
    for n in ("swa_sinks", "rwkv_mu", "rwkv_w0", "rwkv_a0", "rwkv_k_k", "rwkv_k_a", "rwkv_r_k", "rwkv_ln_w",
              "rwkv_ln_b", "final_norm"):
        gsmall[n] = G[n]
    gsmall["fox_b_f"] = G["fox_b_f"][:, :n_f]
    small = _pack_small([gsmall[n] for n in _REPLICATED] + [loss_row[:, :1]])
    small_parts = all_gather([small], "small_all_gather")[0]
    pad1 = lambda vals: _pack_small(vals + [jnp.zeros((1, 1), F32)])
    gs, ds, nms, nvs = adamw(pad1([w[n] for n in _REPLICATED]), pad1([m[n] for n in _REPLICATED]),
                             pad1([v[n] for n in _REPLICATED]), small_parts, "adamw_replicated")
    out_g.update(zip(_REPLICATED, _unpack_small(gs, small_shapes)))
    out_d.update(zip(_REPLICATED, _unpack_small(ds, small_shapes)))
    out_m.update(zip(_REPLICATED, _unpack_small(nms, small_shapes)))
    out_v.update(zip(_REPLICATED, _unpack_small(nvs, small_shapes)))
    n_small = sum(math.prod(s) for s in small_shapes)
    loss = gs[0, n_small]

    return (loss, dx[None], *[out_g[n] for n in _WEIGHTS], *[out_d[n] for n in _WEIGHTS],
            *[out_m[n] for n in _WEIGHTS], *[out_v[n] for n in _WEIGHTS])
```
